```python
import jax, jax.numpy as jnp
from jax import lax
import numpy as np

D_MODEL = 1024
BATCH = 16
SEQ = 2048
DEPTH = 1

SB_HEADS = 8
SB_HEAD_DIM = 64
MLA_HEADS = 8
MLA_NOPE_DIM = 64
MLA_ROPE_DIM = 32
MLA_V_DIM = 64
Q_LORA_RANK = 384
KV_LORA_RANK = 256
D_FF = 2816
CONV_WIDTH = 3
BLOCK_Q = 128
ROPE_BASE = 10000.0
EPS = 1e-6

SB_WIDTH = SB_HEADS * SB_HEAD_DIM
MLA_WIDTH = MLA_HEADS * MLA_V_DIM
MIX_WIDTH = SB_WIDTH + MLA_WIDTH
IN_COLS = 3 * SB_WIDTH + Q_LORA_RANK + KV_LORA_RANK + MLA_ROPE_DIM
MLA_QK_DIM = MLA_NOPE_DIM + MLA_ROPE_DIM

kernel_name = "hymba_style_stickbreak_mla_convffn"


def rmsnorm(x, g):
    xf = x.astype(jnp.float32)
    y = xf * lax.rsqrt(jnp.mean(xf * xf, axis=-1, keepdims=True) + EPS)
    return (y * g.astype(jnp.float32)).astype(x.dtype)


def rope_tables(positions, dim):
    half = dim // 2
    inv_freq = 1.0 / (ROPE_BASE ** (jnp.arange(half, dtype=jnp.float32) * (2.0 / dim)))
    ang = positions.astype(jnp.float32)[..., None] * inv_freq
    return jnp.cos(ang), jnp.sin(ang)


def apply_rope(x, cos, sin):
    half = x.shape[-1] // 2
    xf = x.astype(jnp.float32)
    x1, x2 = xf[..., :half], xf[..., half:]
    out = jnp.concatenate([x1 * cos - x2 * sin, x2 * cos + x1 * sin], axis=-1)
    return out.astype(x.dtype)


def stick_breaking_attention(q, k, v):
    S, Dh = q.shape[1], q.shape[-1]
    scale = Dh ** -0.5
    outs = []
    for i in range(S // BLOCK_Q):
        q0 = i * BLOCK_Q
        kv_len = q0 + BLOCK_Q
        qb = q[:, q0:kv_len]
        kb = k[:, :kv_len]
        vb = v[:, :kv_len]
        z = jnp.einsum('bqhd,bkhd->bhqk', qb, kb, preferred_element_type=jnp.float32) * scale
        t_idx = q0 + jnp.arange(BLOCK_Q)[:, None]
        s_idx = jnp.arange(kv_len)[None, :]
        visible = s_idx < t_idx
        log_beta = jax.nn.log_sigmoid(z)
        log_keep = jnp.where(visible, jax.nn.log_sigmoid(-z), 0.0)
        tail = lax.cumsum(log_keep, axis=3, reverse=True) - log_keep
        a = jnp.where(visible, jnp.exp(log_beta + tail), 0.0)
        outs.append(jnp.einsum('bhqk,bkhd->bqhd', a.astype(v.dtype), vb))
    return jnp.concatenate(outs, axis=1)


def mla_attention(q_nope, q_rope, k_nope, k_rope, v):
    S = q_nope.shape[1]
    scale = MLA_QK_DIM ** -0.5
    outs = []
    for i in range(S // BLOCK_Q):
        q0 = i * BLOCK_Q
        kv_len = q0 + BLOCK_Q
        s = (jnp.einsum('bqhd,bkhd->bhqk', q_nope[:, q0:kv_len], k_nope[:, :kv_len],
                        preferred_element_type=jnp.float32)
             + jnp.einsum('bqhr,bkr->bhqk', q_rope[:, q0:kv_len], k_rope[:, :kv_len],
                          preferred_element_type=jnp.float32)) * scale
        t_idx = q0 + jnp.arange(BLOCK_Q)[:, None]
        s_idx = jnp.arange(kv_len)[None, :]
        s = jnp.where(s_idx <= t_idx, s, -jnp.inf)
        p = jax.nn.softmax(s, axis=-1)
        outs.append(jnp.einsum('bhqk,bkhd->bqhd', p.astype(v.dtype), v[:, :kv_len]))
    return jnp.concatenate(outs, axis=1)


def causal_depthwise_conv(h, w, b):
    C = h.shape[-1]
    y = lax.conv_general_dilated(
        h, w[:, None, :].astype(h.dtype), window_strides=(1,),
        padding=[(CONV_WIDTH - 1, 0)],
        dimension_numbers=('NWC', 'WIO', 'NWC'),
        feature_group_count=C)
    return y + b.astype(h.dtype)


def _fwd_setup_inputs(seed: int = 0) -> dict:
    key = jax.random.key(seed)
    ks = jax.random.split(key, 20)
    f32 = jnp.float32

    def nrm(k, shape, fan_in):
        return jax.random.normal(k, shape, f32) * (fan_in ** -0.5)

    def gain(k, shape):
        return 1.0 + 0.02 * jax.random.normal(k, shape, f32)

    x = jax.random.normal(ks[0], (BATCH, SEQ, D_MODEL), f32)
    offset = jax.random.randint(ks[1], (BATCH, 1), 0, 1024, dtype=jnp.int32)
    positions = (jnp.arange(SEQ, dtype=jnp.int32)[None, :] + offset).astype(jnp.int32)
    return {
        "x": x,
        "positions": positions,
        "g_mix": gain(ks[2], (DEPTH, D_MODEL)),
        "w_in": nrm(ks[3], (DEPTH, D_MODEL, IN_COLS), D_MODEL),
        "g_cq": gain(ks[4], (DEPTH, Q_LORA_RANK)),
        "w_uq": nrm(ks[5], (DEPTH, Q_LORA_RANK, MLA_HEADS * MLA_QK_DIM), Q_LORA_RANK),
        "g_ckv": gain(ks[6], (DEPTH, KV_LORA_RANK)),
        "w_ukv": nrm(ks[7], (DEPTH, KV_LORA_RANK, MLA_HEADS * (MLA_NOPE_DIM + MLA_V_DIM)), KV_LORA_RANK),
        "g_sb_out": gain(ks[8], (DEPTH, SB_WIDTH)),
        "g_mla_out": gain(ks[9], (DEPTH, MLA_WIDTH)),
        "w_out": nrm(ks[10], (DEPTH, MIX_WIDTH, D_MODEL), MIX_WIDTH),
        "g_ffn": gain(ks[11], (DEPTH, D_MODEL)),
        "w_up": nrm(ks[12], (DEPTH, D_MODEL, 2 * D_FF), D_MODEL),
        "conv_w": nrm(ks[13], (DEPTH, CONV_WIDTH, 2 * D_FF), CONV_WIDTH),
        "conv_b": 0.01 * jax.random.normal(ks[14], (DEPTH, 2 * D_FF), f32),
        "w_down": nrm(ks[15], (DEPTH, D_FF, D_MODEL), D_FF),
        "g_final": gain(ks[16], (D_MODEL,)),
    }


def _fwd_reference(x, positions, g_mix, w_in, g_cq, w_uq, g_ckv, w_ukv, g_sb_out, g_mla_out,
              w_out, g_ffn, w_up, conv_w, conv_b, w_down, g_final):
    B, S, _ = x.shape
    cos, sin = rope_tables(positions, MLA_ROPE_DIM)
    cos_h, sin_h = cos[:, :, None, :], sin[:, :, None, :]
    split_at = np.cumsum([SB_WIDTH, SB_WIDTH, SB_WIDTH, Q_LORA_RANK, KV_LORA_RANK])

    for l in range(DEPTH):
        h = rmsnorm(x, g_mix[l])
        p = h @ w_in[l]
        q_sb, k_sb, v_sb, c_q, c_kv, k_rope = jnp.split(p, split_at, axis=-1)

        o_sb = stick_breaking_attention(
            q_sb.reshape(B, S, SB_HEADS, SB_HEAD_DIM),
            k_sb.reshape(B, S, SB_HEADS, SB_HEAD_DIM),
            v_sb.reshape(B, S, SB_HEADS, SB_HEAD_DIM)).reshape(B, S, SB_WIDTH)

        q = (rmsnorm(c_q, g_cq[l]) @ w_uq[l]).reshape(B, S, MLA_HEADS, MLA_QK_DIM)
        q_nope, q_rope = q[..., :MLA_NOPE_DIM], q[..., MLA_NOPE_DIM:]
        q_rope = apply_rope(q_rope, cos_h, sin_h)
        kv = (rmsnorm(c_kv, g_ckv[l]) @ w_ukv[l]).reshape(B, S, MLA_HEADS, MLA_NOPE_DIM + MLA_V_DIM)
        k_nope, v_mla = kv[..., :MLA_NOPE_DIM], kv[..., MLA_NOPE_DIM:]
        k_rope = apply_rope(k_rope, cos, sin)
        o_mla = mla_attention(q_nope, q_rope, k_nope, k_rope, v_mla).reshape(B, S, MLA_WIDTH)

        o = jnp.concatenate([rmsnorm(o_sb, g_sb_out[l]), rmsnorm(o_mla, g_mla_out[l])], axis=-1)
        x = x + o @ w_out[l]

        u = rmsnorm(x, g_ffn[l]) @ w_up[l]
        u = causal_depthwise_conv(u, conv_w[l], conv_b[l])
        gate, val = u[..., :D_FF], u[..., D_FF:]
        x = x + (jax.nn.silu(gate) * val) @ w_down[l]

    return rmsnorm(x, g_final)


import jax as _jax
import jax.numpy as _jnp

TWIN_FORMAT = 'train_step'
FWD_PARAMS = ['x', 'positions', 'g_mix', 'w_in', 'g_cq', 'w_uq', 'g_ckv', 'w_ukv', 'g_sb_out', 'g_mla_out', 'w_out', 'g_ffn', 'w_up', 'conv_w', 'conv_b', 'w_down', 'g_final']
TWIN_WEIGHTS = ['g_mix', 'w_in', 'g_cq', 'w_uq', 'g_ckv', 'w_ukv', 'g_sb_out', 'g_mla_out', 'w_out', 'g_ffn', 'w_up', 'conv_w', 'conv_b', 'w_down', 'g_final']
TWIN_DIFF_INPUT = 'x'
TWIN_INPUTS = ['x', 'positions', 'g_mix', 'w_in', 'g_cq', 'w_uq', 'g_ckv', 'w_ukv', 'g_sb_out', 'g_mla_out', 'w_out', 'g_ffn', 'w_up', 'conv_w', 'conv_b', 'w_down', 'g_final', 'loss_target', 'm_g_mix', 'm_w_in', 'm_g_cq', 'm_w_uq', 'm_g_ckv', 'm_w_ukv', 'm_g_sb_out', 'm_g_mla_out', 'm_w_out', 'm_g_ffn', 'm_w_up', 'm_conv_w', 'm_conv_b', 'm_w_down', 'm_g_final', 'v_g_mix', 'v_w_in', 'v_g_cq', 'v_w_uq', 'v_g_ckv', 'v_w_ukv', 'v_g_sb_out', 'v_g_mla_out', 'v_w_out', 'v_g_ffn', 'v_w_up', 'v_conv_w', 'v_conv_b', 'v_w_down', 'v_g_final']
TWIN_OUTPUTS = ['loss', 'grad_x', 'grad_g_mix', 'grad_w_in', 'grad_g_cq', 'grad_w_uq', 'grad_g_ckv', 'grad_w_ukv', 'grad_g_sb_out', 'grad_g_mla_out', 'grad_w_out', 'grad_g_ffn', 'grad_w_up', 'grad_conv_w', 'grad_conv_b', 'grad_w_down', 'grad_g_final', 'delta_g_mix', 'delta_w_in', 'delta_g_cq', 'delta_w_uq', 'delta_g_ckv', 'delta_w_ukv', 'delta_g_sb_out', 'delta_g_mla_out', 'delta_w_out', 'delta_g_ffn', 'delta_w_up', 'delta_conv_w', 'delta_conv_b', 'delta_w_down', 'delta_g_final', 'new_m_g_mix', 'new_m_w_in', 'new_m_g_cq', 'new_m_w_uq', 'new_m_g_ckv', 'new_m_w_ukv', 'new_m_g_sb_out', 'new_m_g_mla_out', 'new_m_w_out', 'new_m_g_ffn', 'new_m_w_up', 'new_m_conv_w', 'new_m_conv_b', 'new_m_w_down', 'new_m_g_final', 'new_v_g_mix', 'new_v_w_in', 'new_v_g_cq', 'new_v_w_uq', 'new_v_g_ckv', 'new_v_w_ukv', 'new_v_g_sb_out', 'new_v_g_mla_out', 'new_v_w_out', 'new_v_g_ffn', 'new_v_w_up', 'new_v_conv_w', 'new_v_conv_b', 'new_v_w_down', 'new_v_g_final']
TWIN_LEAF_KINDS = {'loss': 'loss', 'grad_x': 'grad_x', 'grad_g_mix': 'grad_w', 'grad_w_in': 'grad_w', 'grad_g_cq': 'grad_w', 'grad_w_uq': 'grad_w', 'grad_g_ckv': 'grad_w', 'grad_w_ukv': 'grad_w', 'grad_g_sb_out': 'grad_w', 'grad_g_mla_out': 'grad_w', 'grad_w_out': 'grad_w', 'grad_g_ffn': 'grad_w', 'grad_w_up': 'grad_w', 'grad_conv_w': 'grad_w', 'grad_conv_b': 'grad_w', 'grad_w_down': 'grad_w', 'grad_g_final': 'grad_w', 'delta_g_mix': 'delta_w', 'delta_w_in': 'delta_w', 'delta_g_cq': 'delta_w', 'delta_w_uq': 'delta_w', 'delta_g_ckv': 'delta_w', 'delta_w_ukv': 'delta_w', 'delta_g_sb_out': 'delta_w', 'delta_g_mla_out': 'delta_w', 'delta_w_out': 'delta_w', 'delta_g_ffn': 'delta_w', 'delta_w_up': 'delta_w', 'delta_conv_w': 'delta_w', 'delta_conv_b': 'delta_w', 'delta_w_down': 'delta_w', 'delta_g_final': 'delta_w', 'new_m_g_mix': 'new_m', 'new_m_w_in': 'new_m', 'new_m_g_cq': 'new_m', 'new_m_w_uq': 'new_m', 'new_m_g_ckv': 'new_m', 'new_m_w_ukv': 'new_m', 'new_m_g_sb_out': 'new_m', 'new_m_g_mla_out': 'new_m', 'new_m_w_out': 'new_m', 'new_m_g_ffn': 'new_m', 'new_m_w_up': 'new_m', 'new_m_conv_w': 'new_m', 'new_m_conv_b': 'new_m', 'new_m_w_down': 'new_m', 'new_m_g_final': 'new_m', 'new_v_g_mix': 'new_v', 'new_v_w_in': 'new_v', 'new_v_g_cq': 'new_v', 'new_v_w_uq': 'new_v', 'new_v_g_ckv': 'new_v', 'new_v_w_ukv': 'new_v', 'new_v_g_sb_out': 'new_v', 'new_v_g_mla_out': 'new_v', 'new_v_w_out': 'new_v', 'new_v_g_ffn': 'new_v', 'new_v_w_up': 'new_v', 'new_v_conv_w': 'new_v', 'new_v_conv_b': 'new_v', 'new_v_w_down': 'new_v', 'new_v_g_final': 'new_v'}


def _forward(args):
    return _fwd_reference(*[args[k] for k in FWD_PARAMS])


def _output_shape():
    out = _jax.eval_shape(lambda: _forward(_fwd_setup_inputs(0)))
    return out.shape, out.dtype

N_MICROBATCH = 1
ADAM_LR = 0.001
ADAM_B1 = 0.9
ADAM_B2 = 0.999
ADAM_EPS = 1e-08
ADAM_WD = 0.01
ADAM_STEP = 10
PER_EXAMPLE_BATCH_AXIS = {'x': 0, 'positions': 0, 'loss_target': 0}
SHARED_INPUTS = []
_WEIGHT_DTYPES = {'g_mix': _jnp.float32, 'w_in': _jnp.float32, 'g_cq': _jnp.float32, 'w_uq': _jnp.float32, 'g_ckv': _jnp.float32, 'w_ukv': _jnp.float32, 'g_sb_out': _jnp.float32, 'g_mla_out': _jnp.float32, 'w_out': _jnp.float32, 'g_ffn': _jnp.float32, 'w_up': _jnp.float32, 'conv_w': _jnp.float32, 'conv_b': _jnp.float32, 'w_down': _jnp.float32, 'g_final': _jnp.float32}
MOMENT_SCALE = {'g_mix': 1.953102e-01, 'w_in': 1.310937e-01, 'g_cq': 1.533172e-01, 'w_uq': 1.177702e-01, 'g_ckv': 3.148763e-01, 'w_ukv': 1.357948e-01, 'g_sb_out': 1.346455e-01, 'g_mla_out': 1.502823e-01, 'w_out': 1.378121e-01, 'g_ffn': 1.088714e-01, 'w_up': 4.447767e-02, 'conv_w': 4.404851e-02, 'conv_b': 4.325552e-02, 'w_down': 7.304401e-02, 'g_final': 3.204555e+01}


def _to_microbatches(a, axis):
    t = _jnp.moveaxis(a, axis, 0)
    t = t.reshape((N_MICROBATCH, t.shape[0] // N_MICROBATCH) + t.shape[1:])
    return _jnp.moveaxis(t, 1, axis + 1)


def setup_inputs(seed: int = 0) -> dict:
    inp = _fwd_setup_inputs(seed)
    key = _jax.random.fold_in(_jax.random.key(seed), 7919)
    shape, _ = _output_shape()
    out = dict(inp)
    out["loss_target"] = _jax.random.normal(_jax.random.fold_in(key, 0), shape, _jnp.float32)
    for i, name in enumerate(TWIN_WEIGHTS):
        w = inp[name].astype(_jnp.float32)
        if MOMENT_SCALE is None:
            s = _jnp.sqrt(_jnp.mean(_jnp.square(w)) + 1e-30)
        else:
            s = MOMENT_SCALE[name]
        km, kv = _jax.random.split(_jax.random.fold_in(key, i + 1))
        out[name] = w
        out["m_" + name] = s * _jax.random.normal(km, w.shape, _jnp.float32)
        out["v_" + name] = (s * s) * _jax.random.uniform(kv, w.shape, _jnp.float32, 0.5, 1.5)
    if N_MICROBATCH > 1:
        for name, axis in PER_EXAMPLE_BATCH_AXIS.items():
            out[name] = _to_microbatches(out[name], axis)
    return {'x': out['x'], 'positions': out['positions'], 'g_mix': out['g_mix'], 'w_in': out['w_in'], 'g_cq': out['g_cq'], 'w_uq': out['w_uq'], 'g_ckv': out['g_ckv'], 'w_ukv': out['w_ukv'], 'g_sb_out': out['g_sb_out'], 'g_mla_out': out['g_mla_out'], 'w_out': out['w_out'], 'g_ffn': out['g_ffn'], 'w_up': out['w_up'], 'conv_w': out['conv_w'], 'conv_b': out['conv_b'], 'w_down': out['w_down'], 'g_final': out['g_final'], 'loss_target': out['loss_target'], 'm_g_mix': out['m_g_mix'], 'm_w_in': out['m_w_in'], 'm_g_cq': out['m_g_cq'], 'm_w_uq': out['m_w_uq'], 'm_g_ckv': out['m_g_ckv'], 'm_w_ukv': out['m_w_ukv'], 'm_g_sb_out': out['m_g_sb_out'], 'm_g_mla_out': out['m_g_mla_out'], 'm_w_out': out['m_w_out'], 'm_g_ffn': out['m_g_ffn'], 'm_w_up': out['m_w_up'], 'm_conv_w': out['m_conv_w'], 'm_conv_b': out['m_conv_b'], 'm_w_down': out['m_w_down'], 'm_g_final': out['m_g_final'], 'v_g_mix': out['v_g_mix'], 'v_w_in': out['v_w_in'], 'v_g_cq': out['v_g_cq'], 'v_w_uq': out['v_w_uq'], 'v_g_ckv': out['v_g_ckv'], 'v_w_ukv': out['v_w_ukv'], 'v_g_sb_out': out['v_g_sb_out'], 'v_g_mla_out': out['v_g_mla_out'], 'v_w_out': out['v_w_out'], 'v_g_ffn': out['v_g_ffn'], 'v_w_up': out['v_w_up'], 'v_conv_w': out['v_conv_w'], 'v_conv_b': out['v_conv_b'], 'v_w_down': out['v_w_down'], 'v_g_final': out['v_g_final']}


def _loss(weights, diff, rest, loss_target):
    with _jax.named_scope("forward"):
        args = {**rest, TWIN_DIFF_INPUT: diff, **{k: w.astype(_WEIGHT_DTYPES[k]) for k, w in weights.items()}}
        y = _forward(args)
    with _jax.named_scope("loss_head"):
        err = _jnp.square(y.astype(_jnp.float32) - loss_target)
        return 0.5 * _jnp.sum(_jnp.mean(err, axis=-1)) if err.ndim else 0.5 * err


def _adamw(w, g, m, v):
    m = ADAM_B1 * m + (1.0 - ADAM_B1) * g
    v = ADAM_B2 * v + (1.0 - ADAM_B2) * _jnp.square(g)
    m_hat = m / (1.0 - ADAM_B1 ** ADAM_STEP)
    v_hat = v / (1.0 - ADAM_B2 ** ADAM_STEP)
    delta = -ADAM_LR * (m_hat / (_jnp.sqrt(v_hat) + ADAM_EPS) + ADAM_WD * w)
    return delta, m, v


def reference(x, positions, g_mix, w_in, g_cq, w_uq, g_ckv, w_ukv, g_sb_out, g_mla_out, w_out, g_ffn, w_up, conv_w, conv_b, w_down, g_final, loss_target, m_g_mix, m_w_in, m_g_cq, m_w_uq, m_g_ckv, m_w_ukv, m_g_sb_out, m_g_mla_out, m_w_out, m_g_ffn, m_w_up, m_conv_w, m_conv_b, m_w_down, m_g_final, v_g_mix, v_w_in, v_g_cq, v_w_uq, v_g_ckv, v_w_ukv, v_g_sb_out, v_g_mla_out, v_w_out, v_g_ffn, v_w_up, v_conv_w, v_conv_b, v_w_down, v_g_final):
    given = dict(x=x, positions=positions, g_mix=g_mix, w_in=w_in, g_cq=g_cq, w_uq=w_uq, g_ckv=g_ckv, w_ukv=w_ukv, g_sb_out=g_sb_out, g_mla_out=g_mla_out, w_out=w_out, g_ffn=g_ffn, w_up=w_up, conv_w=conv_w, conv_b=conv_b, w_down=w_down, g_final=g_final, loss_target=loss_target, m_g_mix=m_g_mix, m_w_in=m_w_in, m_g_cq=m_g_cq, m_w_uq=m_w_uq, m_g_ckv=m_g_ckv, m_w_ukv=m_w_ukv, m_g_sb_out=m_g_sb_out, m_g_mla_out=m_g_mla_out, m_w_out=m_w_out, m_g_ffn=m_g_ffn, m_w_up=m_w_up, m_conv_w=m_conv_w, m_conv_b=m_conv_b, m_w_down=m_w_down, m_g_final=m_g_final, v_g_mix=v_g_mix, v_w_in=v_w_in, v_g_cq=v_g_cq, v_w_uq=v_w_uq, v_g_ckv=v_g_ckv, v_w_ukv=v_w_ukv, v_g_sb_out=v_g_sb_out, v_g_mla_out=v_g_mla_out, v_w_out=v_w_out, v_g_ffn=v_g_ffn, v_w_up=v_w_up, v_conv_w=v_conv_w, v_conv_b=v_conv_b, v_w_down=v_w_down, v_g_final=v_g_final)
    weights = {n: given[n] for n in TWIN_WEIGHTS}
    shared = {n: given[n] for n in SHARED_INPUTS}
    per_example = {n: given[n] for n in ['x', 'positions']}
    grad_fn = _jax.value_and_grad(_loss, argnums=(0, 1))

    def one_microbatch(ex, loss_target):
        ex = dict(ex)
        diff = ex.pop(TWIN_DIFF_INPUT)
        return grad_fn(weights, diff, {**shared, **ex}, loss_target)

    if N_MICROBATCH == 1:
        loss, (grad_w, grad_x) = one_microbatch(per_example, given["loss_target"])
    else:
        def body(carry, xs):
            loss_sum, grad_sum = carry
            l_k, (gw_k, gx_k) = one_microbatch(xs[0], xs[1])
            with _jax.named_scope("update"):
                return (loss_sum + l_k, _jax.tree.map(_jnp.add, grad_sum, gw_k)), gx_k

        init = (_jnp.zeros((), _jnp.float32), _jax.tree.map(_jnp.zeros_like, weights))
        (loss, grad_w), grad_x = _jax.lax.scan(body, init, (per_example, given["loss_target"]))
    with _jax.named_scope("update"):
        delta_w, new_m, new_v = {}, {}, {}
        for n in TWIN_WEIGHTS:
            delta_w[n], new_m[n], new_v[n] = _adamw(weights[n], grad_w[n], given["m_" + n], given["v_" + n])
    return (loss, grad_x, *[grad_w[n] for n in TWIN_WEIGHTS], *[delta_w[n] for n in TWIN_WEIGHTS],
            *[new_m[n] for n in TWIN_WEIGHTS], *[new_v[n] for n in TWIN_WEIGHTS])
```

```python
import functools

import jax
import jax.numpy as jnp
from jax import lax
from jax.experimental import pallas as pl
from jax.experimental.pallas import tpu as pltpu

F32 = jnp.float32
BF16 = jnp.bfloat16
MESH = pl.DeviceIdType.MESH

D_MODEL = 1024
HEADS = 8
HEAD_DIM = 64
ATT_W = HEADS * HEAD_DIM
ROPE_DIM = 32
ROPE_W = HEADS * ROPE_DIM
QK_DIM = HEAD_DIM + ROPE_DIM
Q_RANK = 384
KV_RANK = 256
D_FF = 2816
IN_COLS = 2208
IN_COLS_PAD = 2304
EPS = 1e-6
ROPE_BASE = 10000.0
SB_SCALE = HEAD_DIM ** -0.5
MLA_SCALE = QK_DIM ** -0.5
LANES = 128
N_CHIPS = 4
N_DEV = 8
VMEM_LIMIT = 48 * 1024 * 1024
ATT_TQ = 256
ATT_TK = 256
NEG_BIG = -1e30

ADAM_LR = 0.001
ADAM_B1 = 0.9
ADAM_B2 = 0.999
ADAM_EPS = 1e-08
ADAM_WD = 0.01
ADAM_STEP = 10

BIG_W = ("w_in", "w_uq", "w_ukv", "w_out", "w_up", "conv_w", "w_down")
BIG_SHARD = {
    "w_in": (D_MODEL, IN_COLS // 4, True),
    "w_uq": (Q_RANK, HEADS * QK_DIM // 4, True),
    "w_ukv": (KV_RANK, 2 * ATT_W // 4, True),
    "w_out": (2 * ATT_W // 4, D_MODEL, False),
    "w_up": (D_MODEL, 2 * D_FF // 4, True),
    "conv_w": (3, 2 * D_FF // 4, True),
    "w_down": (D_FF // 4, D_MODEL, False),
}
SMALL_W = ("g_mix", "g_cq", "g_ckv", "g_sb_out", "g_mla_out", "g_ffn", "conv_b", "g_final")
SMALL_N = {"g_mix": D_MODEL, "g_cq": Q_RANK, "g_ckv": KV_RANK, "g_sb_out": ATT_W, "g_mla_out": ATT_W,
           "g_ffn": D_MODEL, "conv_b": 2 * D_FF, "g_final": D_MODEL}


def _params(sem=None, **kw):
    return pltpu.CompilerParams(dimension_semantics=sem, vmem_limit_bytes=VMEM_LIMIT, **kw)


def _dot(a, b, dims):
    return lax.dot_general(a, b, (dims, ((), ())), preferred_element_type=F32)


def _nn(a, b):
    return _dot(a, b, ((1,), (0,)))


def _nt(a, b):
    return _dot(a, b, ((1,), (1,)))


def _tn(a, b):
    return _dot(a, b, ((0,), (0,)))


def _split2(x):
    hi = x.astype(BF16)
    lo = (x - hi.astype(F32)).astype(BF16)
    return hi, lo


def _split3(x):
    hi = x.astype(BF16)
    r1 = x - hi.astype(F32)
    mid = r1.astype(BF16)
    return hi, mid, (r1 - mid.astype(F32)).astype(BF16)


def _rms_r(x, d):
    return lax.rsqrt(jnp.sum(x * x, axis=-1, keepdims=True) * (1.0 / d) + EPS)


def _rms_bwd(x, g, dy, d):
    r = _rms_r(x, d)
    xhat = x * r
    gy = dy * g
    dx = r * (gy - xhat * (jnp.sum(xhat * gy, axis=-1, keepdims=True) * (1.0 / d)))
    return dx, jnp.sum(dy * xhat, axis=0, keepdims=True)


def _rot(x):
    lane = lax.broadcasted_iota(jnp.int32, x.shape, x.ndim - 1)
    n = x.shape[-1]
    return jnp.where((lane & 31) < 16, pltpu.roll(x, n - 16, x.ndim - 1), pltpu.roll(x, 16, x.ndim - 1))


def _fold4(x):
    return x + pltpu.roll(x, 32, 1) + pltpu.roll(x, 64, 1) + pltpu.roll(x, 96, 1)


def matmul_call(name, a, b, mode, out_dtype=F32, res=None, tm=512, tn=None):
    M, K = a.shape
    N = b.shape[1] if mode == "nn" else b.shape[0]
    tn = N if tn is None else tn
    assert M % tm == 0 and N % tn == 0

    def body(*refs):
        if res is None:
            a_ref, b_ref, o_ref = refs
        else:
            a_ref, b_ref, r_ref, o_ref = refs
        av = a_ref[...].astype(BF16)
        bv = b_ref[...].astype(BF16)
        acc = _nn(av, bv) if mode == "nn" else _nt(av, bv)
        if res is not None:
            acc = r_ref[...] + acc
        o_ref[...] = acc.astype(out_dtype)

    in_specs = [pl.BlockSpec((tm, K), lambda j, i: (i, 0))]
    if mode == "nn":
        in_specs.append(pl.BlockSpec((K, tn), lambda j, i: (0, j)))
    else:
        in_specs.append(pl.BlockSpec((tn, K), lambda j, i: (j, 0)))
    args = [a, b]
    if res is not None:
        in_specs.append(pl.BlockSpec((tm, tn), lambda j, i: (i, j)))
        args.append(res)
    return pl.pallas_call(
        body, name=name, grid=(N // tn, M // tm), in_specs=in_specs,
        out_specs=pl.BlockSpec((tm, tn), lambda j, i: (i, j)),
        out_shape=jax.ShapeDtypeStruct((M, N), out_dtype),
        compiler_params=_params(("parallel", "parallel")),
    )(*args)


def wgrad_call(name, a, b, tn=None, tt=512):
    T, M = a.shape
    N = b.shape[1]
    tn = N if tn is None else tn
    assert T % tt == 0 and N % tn == 0

    def body(a_ref, b_ref, o_ref):
        @pl.when(pl.program_id(1) == 0)
        def _():
            o_ref[...] = jnp.zeros_like(o_ref)

        o_ref[...] += _tn(a_ref[...].astype(BF16), b_ref[...].astype(BF16))

    return pl.pallas_call(
        body, name=name, grid=(N // tn, T // tt),
        in_specs=[pl.BlockSpec((tt, M), lambda j, t: (t, 0)), pl.BlockSpec((tt, tn), lambda j, t: (t, j))],
        out_specs=pl.BlockSpec((M, tn), lambda j, t: (0, j)),
        out_shape=jax.ShapeDtypeStruct((M, N), F32),
        compiler_params=_params(("parallel", "arbitrary")),
    )(a, b)


def rmsnorm_fwd_call(name, x, g, tm=512):
    T, d = x.shape

    def body(x_ref, g_ref, o_ref):
        x = x_ref[...]
        o_ref[...] = ((x * _rms_r(x, d)) * g_ref[...]).astype(BF16)

    return pl.pallas_call(
        body, name=name, grid=(T // tm,),
        in_specs=[pl.BlockSpec((tm, d), lambda i: (i, 0)), pl.BlockSpec((1, d), lambda i: (0, 0))],
        out_specs=pl.BlockSpec((tm, d), lambda i: (i, 0)),
        out_shape=jax.ShapeDtypeStruct((T, d), BF16),
        compiler_params=_params(("parallel",)),
    )(x, g)


def rmsnorm_bwd_call(name, x, g, dy, res, tm=512):
    T, d = x.shape

    def body(x_ref, g_ref, dy_ref, r_ref, dx_ref, dg_ref):
        @pl.when(pl.program_id(0) == 0)
        def _():
            dg_ref[...] = jnp.zeros_like(dg_ref)

        dx, dg = _rms_bwd(x_ref[...], g_ref[...], dy_ref[...], d)
        dx_ref[...] = r_ref[...] + dx
        dg_ref[...] += dg

    row = pl.BlockSpec((tm, d), lambda i: (i, 0))
    vec = pl.BlockSpec((1, d), lambda i: (0, 0))
    return pl.pallas_call(
        body, name=name, grid=(T // tm,), in_specs=[row, vec, row, row], out_specs=[row, vec],
        out_shape=[jax.ShapeDtypeStruct((T, d), F32), jax.ShapeDtypeStruct((1, d), F32)],
        compiler_params=_params(("arbitrary",)),
    )(x, g, dy, res)


def outnorm_fwd_call(o_sb, o_mla, g_sb, g_mla, tm=512):
    T = o_sb.shape[0]

    def body(a_ref, b_ref, ga_ref, gb_ref, o_ref):
        a = a_ref[...]
        b = b_ref[...]
        ya = (a * _rms_r(a, ATT_W)) * ga_ref[...]
        yb = (b * _rms_r(b, ATT_W)) * gb_ref[...]
        o_ref[...] = jnp.concatenate([ya, yb], axis=1).astype(BF16)

    row = pl.BlockSpec((tm, ATT_W), lambda i: (i, 0))
    vec = pl.BlockSpec((1, ATT_W), lambda i: (0, 0))
    return pl.pallas_call(
        body, name="outnorm_fwd", grid=(T // tm,), in_specs=[row, row, vec, vec],
        out_specs=pl.BlockSpec((tm, 2 * ATT_W), lambda i: (i, 0)),
        out_shape=jax.ShapeDtypeStruct((T, 2 * ATT_W), BF16),
        compiler_params=_params(("parallel",)),
    )(o_sb, o_mla, g_sb, g_mla)


def outnorm_bwd_call(o_sb, o_mla, g_sb, g_mla, do_cat, tm=512):
    T = o_sb.shape[0]

    def body(a_ref, b_ref, ga_ref, gb_ref, d_ref, da_ref, db_ref, dga_ref, dgb_ref):
        @pl.when(pl.program_id(0) == 0)
        def _():
            dga_ref[...] = jnp.zeros_like(dga_ref)
            dgb_ref[...] = jnp.zeros_like(dgb_ref)

        d = d_ref[...]
        da, dga = _rms_bwd(a_ref[...], ga_ref[...], d[:, :ATT_W], ATT_W)
        db, dgb = _rms_bwd(b_ref[...], gb_ref[...], d[:, ATT_W:], ATT_W)
        da_ref[...] = da
        db_ref[...] = db
        dga_ref[...] += dga
        dgb_ref[...] += dgb

    row = pl.BlockSpec((tm, ATT_W), lambda i: (i, 0))
    vec = pl.BlockSpec((1, ATT_W), lambda i: (0, 0))
    return pl.pallas_call(
        body, name="outnorm_bwd", grid=(T // tm,),
        in_specs=[row, row, vec, vec, pl.BlockSpec((tm, 2 * ATT_W), lambda i: (i, 0))],
        out_specs=[row, row, vec, vec],
        out_shape=[jax.ShapeDtypeStruct((T, ATT_W), F32), jax.ShapeDtypeStruct((T, ATT_W), F32),
                   jax.ShapeDtypeStruct((1, ATT_W), F32), jax.ShapeDtypeStruct((1, ATT_W), F32)],
        compiler_params=_params(("arbitrary",)),
    )(o_sb, o_mla, g_sb, g_mla, do_cat)


def final_loss_call(x2, g, target, tm=512):
    T, d = x2.shape

    def body(x_ref, g_ref, t_ref, dx_ref, loss_ref, dg_ref):
        @pl.when(pl.program_id(0) == 0)
        def _():
            loss_ref[...] = jnp.zeros_like(loss_ref)
            dg_ref[...] = jnp.zeros_like(dg_ref)

        x = x_ref[...]
        g = g_ref[...]
        y = (x * _rms_r(x, d)) * g
        err = y - t_ref[...]
        loss_ref[...] += jnp.sum(jnp.sum(err * err, axis=1, keepdims=True), axis=0, keepdims=True) * (0.5 / d)
        dx, dg = _rms_bwd(x, g, err * (1.0 / d), d)
        dx_ref[...] = dx
        dg_ref[...] += dg

    row = pl.BlockSpec((tm, d), lambda i: (i, 0))
    vec = pl.BlockSpec((1, d), lambda i: (0, 0))
    return pl.pallas_call(
        body, name="final_loss", grid=(T // tm,), in_specs=[row, vec, row],
        out_specs=[row, pl.BlockSpec((1, LANES), lambda i: (0, 0)), vec],
        out_shape=[jax.ShapeDtypeStruct((T, d), F32), jax.ShapeDtypeStruct((1, LANES), F32),
                   jax.ShapeDtypeStruct((1, d), F32)],
        compiler_params=_params(("arbitrary",)),
    )(x2, g, target)


def rope_tab_call(pos, inv_freq, tm=512):
    T = pos.shape[0]

    def body(p_ref, f_ref, c_ref, s_ref):
        ang = p_ref[...].astype(F32) * f_ref[...]
        lane = lax.broadcasted_iota(jnp.int32, ang.shape, 1)
        sn = jnp.sin(ang)
        c_ref[...] = jnp.cos(ang)
        s_ref[...] = jnp.where((lane & 31) < 16, -sn, sn)

    row = pl.BlockSpec((tm, LANES), lambda i: (i, 0))
    return pl.pallas_call(
        body, name="rope_tab", grid=(T // tm,),
        in_specs=[pl.BlockSpec((tm, 1), lambda i: (i, 0)), pl.BlockSpec((1, LANES), lambda i: (0, 0))],
        out_specs=[row, row],
        out_shape=[jax.ShapeDtypeStruct((T, LANES), F32)] * 2,
        compiler_params=_params(("parallel",)),
    )(pos, inv_freq)


def mla_prep_fwd_call(p, cos, sin, g_cq, g_ckv, w_uq_p, w_ukv_p, tm=512):
    T = p.shape[0]

    def body(cq_ref, ckvr_ref, c_ref, s_ref, gq_ref, gkv_ref, wq_ref, wkv_ref,
             qn_ref, qr_ref, kn_ref, vm_ref, krt_ref, cqn_ref, ckvn_ref):
        c = c_ref[...]
        s = s_ref[...]
        cq = cq_ref[...]
        cqn = ((cq * _rms_r(cq, Q_RANK)) * gq_ref[...]).astype(BF16)
        cqn_ref[...] = cqn
        q = _nn(cqn, wq_ref[...])
        qn_ref[...] = q[:, :ATT_W].astype(BF16)
        for g in range(ROPE_W // LANES):
            qr = q[:, ATT_W + g * LANES:ATT_W + (g + 1) * LANES]
            qr_ref[:, g * LANES:(g + 1) * LANES] = (qr * c + _rot(qr) * s).astype(BF16)
        ckvr = ckvr_ref[...]
        ckv = ckvr[:, :KV_RANK]
        ckvn = ((ckv * _rms_r(ckv, KV_RANK)) * gkv_ref[...]).astype(BF16)
        ckvn_ref[...] = ckvn
        kv = _nn(ckvn, wkv_ref[...])
        kn_ref[...] = kv[:, :ATT_W].astype(BF16)
        vm_ref[...] = kv[:, ATT_W:].astype(BF16)
        kr = _fold4(ckvr[:, KV_RANK:])
        krt_ref[...] = (kr * c + _rot(kr) * s).astype(BF16)

    def row(w, j=0):
        return pl.BlockSpec((tm, w), lambda i: (i, j))

    def full(a):
        return pl.BlockSpec(a.shape, lambda i: (0, 0))

    return pl.pallas_call(
        body, name="mla_prep_fwd", grid=(T // tm,),
        in_specs=[row(Q_RANK, 4), row(Q_RANK, 5), row(LANES), row(LANES), full(g_cq), full(g_ckv),
                  full(w_uq_p), full(w_ukv_p)],
        out_specs=[row(ATT_W), row(ROPE_W), row(ATT_W), row(ATT_W), row(LANES), row(Q_RANK), row(KV_RANK)],
        out_shape=[jax.ShapeDtypeStruct((T, w), BF16) for w in (ATT_W, ROPE_W, ATT_W, ATT_W, LANES, Q_RANK, KV_RANK)],
        compiler_params=_params(("parallel",)),
    )(p, p, cos, sin, g_cq, g_ckv, w_uq_p, w_ukv_p)


def mla_prep_bwd_call(p, cos, sin, g_cq, g_ckv, w_uq_p, w_ukv_p, dqn, dqr4, dkn, dvm, dkrt4, tm=512):
    T = p.shape[0]

    def body(cq_ref, ckvr_ref, c_ref, s_ref, gq_ref, gkv_ref, wq_ref, wkv_ref,
             dqn_ref, dqr4_ref, dkn_ref, dvm_ref, dkrt4_ref,
             dcq_ref, dckvr_ref, dq_ref, dkv_ref, dgq_ref, dgkv_ref):
        @pl.when(pl.program_id(0) == 0)
        def _():
            dgq_ref[...] = jnp.zeros_like(dgq_ref)
            dgkv_ref[...] = jnp.zeros_like(dgkv_ref)

        c = c_ref[...]
        s = s_ref[...]
        d4 = dqr4_ref[...]
        dqr = [d4[:, :128] + d4[:, 128:256], d4[:, 256:384] + d4[:, 384:]]
        dqr = [t * c + _rot(t * s) for t in dqr]
        dq = jnp.concatenate([dqn_ref[...]] + dqr, axis=1).astype(BF16)
        dq_ref[...] = dq
        dcq, dgq = _rms_bwd(cq_ref[...], gq_ref[...], _nt(dq, wq_ref[...]), Q_RANK)
        dcq_ref[...] = dcq
        dgq_ref[...] += dgq
        dkv = jnp.concatenate([dkn_ref[...], dvm_ref[...]], axis=1).astype(BF16)
        dkv_ref[...] = dkv
        ckvr = ckvr_ref[...]
        dckv, dgkv = _rms_bwd(ckvr[:, :KV_RANK], gkv_ref[...], _nt(dkv, wkv_ref[...]), KV_RANK)
        dgkv_ref[...] += dgkv
        k4 = dkrt4_ref[...]
        dkr = _fold4(k4[:, :128] + k4[:, 128:256] + k4[:, 256:384] + k4[:, 384:])
        dkr = dkr * c + _rot(dkr * s)
        lane = lax.broadcasted_iota(jnp.int32, dkr.shape, 1)
        dckvr_ref[...] = jnp.concatenate([dckv, jnp.where(lane < ROPE_DIM, dkr, 0.0)], axis=1)

    def row(w, j=0):
        return pl.BlockSpec((tm, w), lambda i: (i, j))

    def full(a):
        return pl.BlockSpec(a.shape, lambda i: (0, 0))

    return pl.pallas_call(
        body, name="mla_prep_bwd", grid=(T // tm,),
        in_specs=[row(Q_RANK, 4), row(Q_RANK, 5), row(LANES), row(LANES), full(g_cq), full(g_ckv),
                  full(w_uq_p), full(w_ukv_p), row(ATT_W), row(ATT_W), row(ATT_W), row(ATT_W), row(ATT_W)],
        out_specs=[row(Q_RANK), row(Q_RANK), row(ATT_W + ROPE_W), row(2 * ATT_W),
                   pl.BlockSpec((1, Q_RANK), lambda i: (0, 0)), pl.BlockSpec((1, KV_RANK), lambda i: (0, 0))],
        out_shape=[jax.ShapeDtypeStruct((T, Q_RANK), F32), jax.ShapeDtypeStruct((T, Q_RANK), F32),
                   jax.ShapeDtypeStruct((T, ATT_W + ROPE_W), BF16), jax.ShapeDtypeStruct((T, 2 * ATT_W), BF16),
                   jax.ShapeDtypeStruct((1, Q_RANK), F32), jax.ShapeDtypeStruct((1, KV_RANK), F32)],
        compiler_params=_params(("arbitrary",)),
    )(p, p, cos, sin, g_cq, g_ckv, w_uq_p, w_ukv_p, dqn, dqr4, dkn, dvm, dkrt4)


def _iota2(shape, axis):
    return lax.broadcasted_iota(jnp.int32, shape, axis)


def _head_masks():
    lane = _iota2((1, LANES), 1)
    return lane < HEAD_DIM, lane >= HEAD_DIM


def _pair(x, masks, dtype=BF16):
    return [jnp.where(m, x, 0.0).astype(dtype) for m in masks]


def _log_gates(z):
    soft = jnp.log1p(jnp.exp(-jnp.abs(z)))
    return jnp.minimum(z, 0.0) - soft, -jnp.maximum(z, 0.0) - soft


def _lane_selector(group):
    return jnp.where(_iota2((16, LANES), 1) // group == _iota2((16, LANES), 0), 1.0, 0.0).astype(BF16)


def _rows8(sel_t, x):
    hi = x.astype(BF16)
    r1 = x - hi.astype(F32)
    mid = r1.astype(BF16)
    lo = (r1 - mid.astype(F32)).astype(BF16)
    return _nt(sel_t, hi) + _nt(sel_t, mid) + _nt(sel_t, lo)


def _row_of(x8, j):
    return jnp.sum(jnp.where(_iota2(x8.shape, 0) == j, x8, 0.0), axis=0, keepdims=True)


def sb_fwd_call(p, B, S):
    T = B * S
    TQ, TK = ATT_TQ, ATT_TK
    nq = S // TQ

    def body(q_ref, k_ref, v_ref, o_ref, lt_ref):
        qi = pl.program_id(2)
        masks = _head_masks()
        qm = _pair(q_ref[...] * SB_SCALE, masks)
        row = _iota2((TQ, TK), 0)
        col = _iota2((TQ, TK), 1)
        tri = jnp.where(row > col, 1.0, 0.0).astype(BF16)
        tri2 = jnp.concatenate([tri, tri], axis=0)
        vis = col < row
        o_ref[...] = jnp.zeros_like(o_ref)

        def step(kb, carry, diag):
            k0 = pl.multiple_of(kb * TK, TK)
            k = k_ref[pl.ds(k0, TK), :].astype(BF16)
            vm = _pair(v_ref[pl.ds(k0, TK), :], masks)
            acc = jnp.zeros((TQ, LANES), F32)
            out = []
            for j in range(2):
                lb, lk = _log_gates(_nt(qm[j], k))
                if diag:
                    lk = jnp.where(vis, lk, 0.0)
                hi, lo = _split2(lk)
                tail = _nn(jnp.concatenate([hi, lo], axis=1), tri2) + carry[j]
                a = jnp.exp(lb + tail)
                if diag:
                    a = jnp.where(vis, a, 0.0)
                acc = acc + _nn(a.astype(BF16), vm[j])
                out.append(carry[j] + jnp.sum(lk, axis=1, keepdims=True))
            o_ref[...] += acc
            return tuple(out)

        zero = jnp.zeros((TQ, 1), F32)
        carry = step(qi, (zero, zero), True)
        c0, c1 = lax.fori_loop(0, qi, lambda i, c: step(qi - 1 - i, c, False), carry)
        lane = _iota2((TQ, LANES), 1)
        lt_ref[...] = jnp.where(lane == 0, c0, jnp.where(lane == 1, c1, 0.0))

    qspec = pl.BlockSpec((TQ, LANES), lambda b, h, i: (b * nq + i, h))
    return pl.pallas_call(
        body, name="sb_fwd", grid=(B, HEADS // 2, nq),
        in_specs=[qspec,
                  pl.BlockSpec((S, LANES), lambda b, h, i: (b, 4 + h)),
                  pl.BlockSpec((S, LANES), lambda b, h, i: (b, 8 + h))],
        out_specs=[qspec, qspec],
        out_shape=[jax.ShapeDtypeStruct((T, ATT_W), F32)] * 2,
        compiler_params=_params(("parallel", "parallel", "arbitrary")),
    )(p, p, p)


def sb_bwd_call(p, lt, do, B, S):
    T = B * S
    TQ, TK = ATT_TQ, ATT_TK
    nq = S // TQ

    def body(q_ref, k_ref, v_ref, lt_ref, do_ref, dq_ref, dk_ref, dv_ref):
        qi = pl.program_id(2)

        @pl.when(qi == 0)
        def _():
            dk_ref[...] = jnp.zeros_like(dk_ref)
            dv_ref[...] = jnp.zeros_like(dv_ref)

        masks = _head_masks()
        qm = _pair(q_ref[...] * SB_SCALE, masks)
        dom = _pair(do_ref[...], masks)
        l8 = _rows8(_lane_selector(1), lt_ref[...])
        ltot = [_row_of(l8, j) for j in range(2)]
        row = _iota2((TK, TQ), 0)
        col = _iota2((TK, TQ), 1)
        incl = jnp.where(col <= row, 1.0, 0.0).astype(BF16)
        incl3 = jnp.concatenate([incl, incl, incl], axis=1)
        excl = jnp.where(col < row, 1.0, 0.0).astype(BF16)
        excl2 = jnp.concatenate([excl, excl], axis=1)
        vis = row < col
        dq_ref[...] = jnp.zeros_like(dq_ref)

        def step(kb, carry, diag):
            k0 = pl.multiple_of(kb * TK, TK)
            kf = k_ref[pl.ds(k0, TK), :]
            k = kf.astype(BF16)
            km = _pair(kf, masks)
            v = v_ref[pl.ds(k0, TK), :].astype(BF16)
            dq = jnp.zeros((TQ, LANES), F32)
            dk = jnp.zeros((TK, LANES), F32)
            dv = jnp.zeros((TK, LANES), F32)
            out = []
            for j in range(2):
                c_j, e_j = carry[2 * j], carry[2 * j + 1]
                lb, lk = _log_gates(_nt(k, qm[j]))
                if diag:
                    lk = jnp.where(vis, lk, 0.0)
                left = _nn(incl3, jnp.concatenate(_split3(lk), axis=0)) + c_j
                a = jnp.exp(lb + (ltot[j] - left))
                if diag:
                    a = jnp.where(vis, a, 0.0)
                e = a * _nt(v, dom[j])
                before = _nn(excl2, jnp.concatenate(_split2(e), axis=0)) + e_j
                dz = e - jnp.exp(lb) * (e + before)
                if diag:
                    dz = jnp.where(vis, dz, 0.0)
                dzb = dz.astype(BF16)
                dk = dk + _nn(dzb, qm[j])
                dv = dv + _nn(a.astype(BF16), dom[j])
                dq = dq + _tn(dzb, km[j])
                out += [c_j + jnp.sum(lk, axis=0, keepdims=True), e_j + jnp.sum(e, axis=0, keepdims=True)]
            dq_ref[...] += dq
            dk_ref[pl.ds(k0, TK), :] += dk
            dv_ref[pl.ds(k0, TK), :] += dv
            return tuple(out)

        zero = jnp.zeros((1, TQ), F32)
        carry = lax.fori_loop(0, qi, lambda i, c: step(i, c, False), (zero,) * 4)
        step(qi, carry, True)
        dq_ref[...] *= SB_SCALE

    qspec = pl.BlockSpec((TQ, LANES), lambda b, h, i: (b * nq + i, h))
    sspec = pl.BlockSpec((S, LANES), lambda b, h, i: (b, h))
    return pl.pallas_call(
        body, name="sb_bwd", grid=(B, HEADS // 2, nq),
        in_specs=[qspec,
                  pl.BlockSpec((S, LANES), lambda b, h, i: (b, 4 + h)),
                  pl.BlockSpec((S, LANES), lambda b, h, i: (b, 8 + h)),
                  qspec, qspec],
        out_specs=[qspec, sspec, sspec],
        out_shape=[jax.ShapeDtypeStruct((T, ATT_W), F32)] * 3,
        compiler_params=_params(("parallel", "parallel", "arbitrary")),
    )(p, p, p, lt, do)


def _rope_masks(hp):
    grp = _iota2((1, LANES), 1) // ROPE_DIM
    return [grp == ((2 * hp + j) % 4) for j in range(2)]


def mla_fwd_call(qn, qr, kn, krt, vm, B, S):
    T = B * S
    TQ, TK = ATT_TQ, ATT_TK
    nq = S // TQ

    def body(qn_ref, qr_ref, kn_ref, kr_ref, v_ref, o_ref, lse_ref):
        hp = pl.program_id(1)
        qi = pl.program_id(2)
        masks = _head_masks()
        rmasks = _rope_masks(hp)
        qnv = qn_ref[...]
        qrv = qr_ref[...]
        qcat = [jnp.concatenate([jnp.where(masks[j], qnv, 0), jnp.where(rmasks[j], qrv, 0)], axis=1).astype(BF16)
                for j in range(2)]
        row = _iota2((TQ, TK), 0)
        col = _iota2((TQ, TK), 1)
        vis = col <= row

        def step(kb, carry, diag):
            k0 = pl.multiple_of(kb * TK, TK)
            kcat = jnp.concatenate([kn_ref[pl.ds(k0, TK), :], kr_ref[pl.ds(k0, TK), :]], axis=1)
            vmk = _pair(v_ref[pl.ds(k0, TK), :], masks)
            acc = carry[4]
            out = []
            scale = []
            for j in range(2):
                m_j, l_j = carry[2 * j], carry[2 * j + 1]
                s = _nt(qcat[j], kcat) * MLA_SCALE
                if diag:
                    s = jnp.where(vis, s, NEG_BIG)
                m_new = jnp.maximum(m_j, jnp.max(s, axis=1, keepdims=True))
                alpha = jnp.exp(m_j - m_new)
                pexp = jnp.exp(s - m_new)
                out += [m_new, alpha * l_j + jnp.sum(pexp, axis=1, keepdims=True)]
                scale.append(alpha)
                acc_add = _nn(pexp.astype(BF16), vmk[j])
                acc = acc * jnp.where(masks[j], alpha, 1.0) + acc_add
            return tuple(out) + (acc,)

        neg = jnp.full((TQ, 1), NEG_BIG, F32)
        zero = jnp.zeros((TQ, 1), F32)
        carry = step(qi, (neg, zero, neg, zero, jnp.zeros((TQ, LANES), F32)), True)
        m0, l0, m1, l1, acc = lax.fori_loop(0, qi, lambda i, c: step(qi - 1 - i, c, False), carry)
        o_ref[...] = acc * jnp.where(masks[0], 1.0 / l0, 1.0 / l1)
        lane = _iota2((TQ, LANES), 1)
        lse_ref[...] = jnp.where(lane == 0, m0 + jnp.log(l0), jnp.where(lane == 1, m1 + jnp.log(l1), 0.0))

    qspec = pl.BlockSpec((TQ, LANES), lambda b, h, i: (b * nq + i, h))
    sspec = pl.BlockSpec((S, LANES), lambda b, h, i: (b, h))
    return pl.pallas_call(
        body, name="mla_fwd", grid=(B, HEADS // 2, nq),
        in_specs=[qspec, pl.BlockSpec((TQ, LANES), lambda b, h, i: (b * nq + i, h // 2)), sspec,
                  pl.BlockSpec((S, LANES), lambda b, h, i: (b, 0)), sspec],
        out_specs=[qspec, qspec],
        out_shape=[jax.ShapeDtypeStruct((T, ATT_W), F32)] * 2,
        compiler_params=_params(("parallel", "parallel", "arbitrary")),
    )(qn, qr, kn, krt, vm)


def mla_bwd_call(qn, qr, kn, krt, vm, o, lse, do, B, S):
    T = B * S
    TQ, TK = ATT_TQ, ATT_TK
    nq = S // TQ

    def body(qn_ref, qr_ref, kn_ref, kr_ref, v_ref, o_ref, lse_ref, do_ref,
             dqn_ref, dqr_ref, dkn_ref, dv_ref, dkr_ref):
        hp = pl.program_id(1)
        qi = pl.program_id(2)

        @pl.when(qi == 0)
        def _():
            dkn_ref[...] = jnp.zeros_like(dkn_ref)
            dv_ref[...] = jnp.zeros_like(dv_ref)
            dkr_ref[...] = jnp.zeros_like(dkr_ref)

        masks = _head_masks()
        rmasks = _rope_masks(hp)
        qnv = qn_ref[...]
        qrv = qr_ref[...]
        qcat = [jnp.concatenate([jnp.where(masks[j], qnv, 0), jnp.where(rmasks[j], qrv, 0)], axis=1).astype(BF16)
                for j in range(2)]
        do = do_ref[...]
        dom = _pair(do, masks)
        d8 = _rows8(_lane_selector(HEAD_DIM), do * o_ref[...])
        l8 = _rows8(_lane_selector(1), lse_ref[...])
        dsum = [_row_of(d8, j) for j in range(2)]
        lse = [_row_of(l8, j) for j in range(2)]
        row = _iota2((TK, TQ), 0)
        col = _iota2((TK, TQ), 1)
        vis = row <= col
        dqn_ref[...] = jnp.zeros_like(dqn_ref)
        dqr_ref[...] = jnp.zeros_like(dqr_ref)

        def step(kb, diag):
            k0 = pl.multiple_of(kb * TK, TK)
            knv = kn_ref[pl.ds(k0, TK), :]
            krv = kr_ref[pl.ds(k0, TK), :]
            kcat = jnp.concatenate([knv, krv], axis=1)
            v = v_ref[pl.ds(k0, TK), :]
            dq = jnp.zeros((TQ, 2 * LANES), F32)
            dk = jnp.zeros((TK, 2 * LANES), F32)
            dv = jnp.zeros((TK, LANES), F32)
            for j in range(2):
                s = _nt(kcat, qcat[j]) * MLA_SCALE
                pr = jnp.exp(s - lse[j])
                if diag:
                    pr = jnp.where(vis, pr, 0.0)
                ds = (pr * (_nt(v, dom[j]) - dsum[j]) * MLA_SCALE).astype(BF16)
                dv = dv + _nn(pr.astype(BF16), dom[j])
                dk = dk + _nn(ds, qcat[j])
                kcat_j = jnp.concatenate([jnp.where(masks[j], knv, 0), jnp.where(rmasks[j], krv, 0)], axis=1)
                dq = dq + _tn(ds, kcat_j.astype(BF16))
            dqn_ref[...] += dq[:, :LANES]
            dqr_ref[...] += dq[:, LANES:]
            dkn_ref[pl.ds(k0, TK), :] += dk[:, :LANES]
            dkr_ref[pl.ds(k0, TK), :] += dk[:, LANES:]
            dv_ref[pl.ds(k0, TK), :] += dv

        step(qi, True)

        def loop(i, c):
            step(qi - 1 - i, False)
            return c

        lax.fori_loop(0, qi, loop, 0)

    qspec = pl.BlockSpec((TQ, LANES), lambda b, h, i: (b * nq + i, h))
    sspec = pl.BlockSpec((S, LANES), lambda b, h, i: (b, h))
    return pl.pallas_call(
        body, name="mla_bwd", grid=(B, HEADS // 2, nq),
        in_specs=[qspec, pl.BlockSpec((TQ, LANES), lambda b, h, i: (b * nq + i, h // 2)), sspec,
                  pl.BlockSpec((S, LANES), lambda b, h, i: (b, 0)), sspec, qspec, qspec, qspec],
        out_specs=[qspec, qspec, sspec, sspec, sspec],
        out_shape=[jax.ShapeDtypeStruct((T, ATT_W), F32)] * 5,
        compiler_params=_params(("parallel", "parallel", "arbitrary")),
    )(qn, qr, kn, krt, vm, o, lse, do)


CONV_TC = 256


def _shift_down(x, n):
    return jnp.where(_iota2(x.shape, 0) >= n, pltpu.roll(x, n, 0), 0.0)


def _shift_up(x, n):
    rows = x.shape[0]
    return jnp.where(_iota2(x.shape, 0) < rows - n, pltpu.roll(x, rows - n, 0), 0.0)


def _taps(w_ref):
    return [w_ref[k:k + 1, :] for k in range(3)]


def _conv3(u, w, b):
    return w[0] * _shift_down(u, 2) + w[1] * _shift_down(u, 1) + w[2] * u + b


def conv_act_fwd_call(u, conv_w, conv_b, B, S):
    T = B * S
    nc = D_FF // CONV_TC

    def body(ug_ref, uv_ref, wg_ref, wv_ref, bg_ref, bv_ref, a_ref):
        gate = _conv3(ug_ref[...], _taps(wg_ref), bg_ref[...])
        val = _conv3(uv_ref[...], _taps(wv_ref), bv_ref[...])
        a_ref[...] =(gate * (1.0 / (1.0 + jnp.exp(-gate))) * val).astype(BF16)

    def blk(rows, off):
        return pl.BlockSpec((rows, CONV_TC), lambda b, j: (b if rows == S else 0, off + j))

    return pl.pallas_call(
        body, name="conv_act_fwd", grid=(B, nc),
        in_specs=[blk(S, 0), blk(S, nc), blk(3, 0), blk(3, nc), blk(1, 0), blk(1, nc)],
        out_specs=blk(S, 0),
        out_shape=jax.ShapeDtypeStruct((T, D_FF), BF16),
        compiler_params=_params(("parallel", "parallel")),
    )(u, u, conv_w, conv_w, conv_b, conv_b)


def conv_act_bwd_call(u, da, conv_w, conv_b, B, S):
    T = B * S
    nc = D_FF // CONV_TC

    def body(ug_ref, uv_ref, da_ref, wg_ref, wv_ref, bg_ref, bv_ref,
             dug_ref, duv_ref, dwg_ref, dwv_ref, dbg_ref, dbv_ref):
        @pl.when(pl.program_id(1) == 0)
        def _():
            for r in (dwg_ref, dwv_ref, dbg_ref, dbv_ref):
                r[...] = jnp.zeros_like(r)

        ug = ug_ref[...]
        uv = uv_ref[...]
        wg = _taps(wg_ref)
        wv = _taps(wv_ref)
        gate = _conv3(ug, wg, bg_ref[...])
        val = _conv3(uv, wv, bv_ref[...])
        da = da_ref[...]
        sig = 1.0 / (1.0 + jnp.exp(-gate))
        dval = da * (gate * sig)
        dgate = da * val * (sig * (1.0 + gate * (1.0 - sig)))
        for u_, d, w, du_ref, dw_ref, db_ref in ((ug, dgate, wg, dug_ref, dwg_ref, dbg_ref),
                                                 (uv, dval, wv, duv_ref, dwv_ref, dbv_ref)):
            du_ref[...] = (w[2] * d + w[1] * _shift_up(d, 1) + w[0] * _shift_up(d, 2)).astype(BF16)
            db_ref[...] += jnp.sum(d, axis=0, keepdims=True)
            dw_ref[0:1, :] += jnp.sum(d * _shift_down(u_, 2), axis=0, keepdims=True)
            dw_ref[1:2, :] += jnp.sum(d * _shift_down(u_, 1), axis=0, keepdims=True)
            dw_ref[2:3, :] += jnp.sum(d * u_, axis=0, keepdims=True)

    def blk(rows, off):
        return pl.BlockSpec((rows, CONV_TC), lambda j, b: (b if rows == S else 0, off + j))

    return pl.pallas_call(
        body, name="conv_act_bwd", grid=(nc, B),
        in_specs=[blk(S, 0), blk(S, nc), blk(S, 0), blk(3, 0), blk(3, nc), blk(1, 0), blk(1, nc)],
        out_specs=[blk(S, 0), blk(S, 0), blk(3, 0), blk(3, 0), blk(1, 0), blk(1, 0)],
        out_shape=[jax.ShapeDtypeStruct((T, D_FF), BF16), jax.ShapeDtypeStruct((T, D_FF), BF16),
                   jax.ShapeDtypeStruct((3, D_FF), F32), jax.ShapeDtypeStruct((3, D_FF), F32),
                   jax.ShapeDtypeStruct((1, D_FF), F32), jax.ShapeDtypeStruct((1, D_FF), F32)],
        compiler_params=_params(("parallel", "arbitrary")),
    )(u, u, da, conv_w, conv_w, conv_b, conv_b)


CHIP_MASKS = ((1, 0), (0, 1), (1, 1))


def _place():
    return lax.axis_index("x"), lax.axis_index("y"), lax.axis_index("c")


def gather_chips_call(name, shard):
    R = shard.shape[0]

    def body(x_ref, out_ref, send_sems, recv_sems, local_sem):
        x, y, c = _place()
        mine = pltpu.make_async_copy(x_ref, out_ref.at[2 * x + y], local_sem)
        mine.start()
        sends = []
        for k, (fx, fy) in enumerate(CHIP_MASKS):
            cp = pltpu.make_async_remote_copy(
                src_ref=x_ref, dst_ref=out_ref.at[2 * x + y], send_sem=send_sems.at[k], recv_sem=recv_sems.at[k],
                device_id=(x ^ fx, y ^ fy, c), device_id_type=MESH)
            cp.start()
            sends.append(cp)
        for k, (fx, fy) in enumerate(CHIP_MASKS):
            pltpu.make_async_remote_copy(
                src_ref=x_ref, dst_ref=out_ref.at[2 * (x ^ fx) + (y ^ fy)], send_sem=send_sems.at[k],
                recv_sem=recv_sems.at[k], device_id=(x ^ fx, y ^ fy, c), device_id_type=MESH).wait_recv()
        for cp in sends:
            cp.wait_send()
        mine.wait()

    return pl.pallas_call(
        body, name=name,
        in_specs=[pl.BlockSpec(memory_space=pl.ANY)], out_specs=pl.BlockSpec(memory_space=pl.ANY),
        out_shape=jax.ShapeDtypeStruct((N_CHIPS, R, LANES), shard.dtype),
        scratch_shapes=[pltpu.SemaphoreType.DMA((3,)), pltpu.SemaphoreType.DMA((3,)), pltpu.SemaphoreType.DMA],
        compiler_params=_params(),
    )(shard)


def scatter_chips_call(name, parts):
    R = parts.shape[1]

    def body(p_ref, out_ref, send_sems, recv_sems):
        x, y, c = _place()
        sends = []
        for k, (fx, fy) in enumerate(CHIP_MASKS):
            cp = pltpu.make_async_remote_copy(
                src_ref=p_ref.at[2 * (x ^ fx) + (y ^ fy)], dst_ref=out_ref.at[k], send_sem=send_sems.at[k],
                recv_sem=recv_sems.at[k], device_id=(x ^ fx, y ^ fy, c), device_id_type=MESH)
            cp.start()
            sends.append(cp)
        for cp in sends:
            cp.wait_recv()
        for cp in sends:
            cp.wait_send()

    return pl.pallas_call(
        body, name=name,
        in_specs=[pl.BlockSpec(memory_space=pl.ANY)], out_specs=pl.BlockSpec(memory_space=pl.ANY),
        out_shape=jax.ShapeDtypeStruct((3, R, LANES), parts.dtype),
        scratch_shapes=[pltpu.SemaphoreType.DMA((3,)), pltpu.SemaphoreType.DMA((3,))],
        compiler_params=_params(),
    )(parts)


def sibling_swap_call(name, v):
    def body(v_ref, out_ref, send_sem, recv_sem):
        x, y, c = _place()
        cp = pltpu.make_async_remote_copy(src_ref=v_ref, dst_ref=out_ref, send_sem=send_sem, recv_sem=recv_sem,
                                          device_id=(x, y, 1 - c), device_id_type=MESH)
        cp.start()
        cp.wait_recv()
        cp.wait_send()

    return pl.pallas_call(
        body, name=name,
        in_specs=[pl.BlockSpec(memory_space=pl.ANY)], out_specs=pl.BlockSpec(memory_space=pl.ANY),
        out_shape=jax.ShapeDtypeStruct(v.shape, v.dtype),
        scratch_shapes=[pltpu.SemaphoreType.DMA, pltpu.SemaphoreType.DMA],
        compiler_params=_params(),
    )(v)


def sum_parts_call(name, parts, got, chip, tr=1024):
    R = parts.shape[1]
    assert R % tr == 0

    def body(chip_ref, p_ref, g_ref, o_ref):
        o_ref[...] = ((p_ref[0] + g_ref[0]) + g_ref[1]) + g_ref[2]

    return pl.pallas_call(
        body, name=name,
        grid_spec=pltpu.PrefetchScalarGridSpec(
            num_scalar_prefetch=1, grid=(R // tr,),
            in_specs=[pl.BlockSpec((1, tr, LANES), lambda i, chip_ref: (chip_ref[0], i, 0)),
                      pl.BlockSpec((3, tr, LANES), lambda i, chip_ref: (0, i, 0))],
            out_specs=pl.BlockSpec((tr, LANES), lambda i, chip_ref: (i, 0))),
        out_shape=jax.ShapeDtypeStruct((R, LANES), F32),
        compiler_params=_params(("parallel",)),
    )(chip, parts, got)


def _adamw(w, g, m, v):
    m = ADAM_B1 * m + (1.0 - ADAM_B1) * g
    v = ADAM_B2 * v + (1.0 - ADAM_B2) * (g * g)
    m_hat = m / (1.0 - ADAM_B1 ** ADAM_STEP)
    v_hat = v / (1.0 - ADAM_B2 ** ADAM_STEP)
    delta = -ADAM_LR * (m_hat / (jnp.sqrt(v_hat) + ADAM_EPS) + ADAM_WD * w)
    return delta, m, v


def adamw_call(name, ga, gb, w, m, v, tr):
    R = w.shape[0]
    assert R % tr == 0

    def body(*refs):
        if gb is None:
            ga_ref, w_ref, m_ref, v_ref, g_ref, d_ref, nm_ref, nv_ref = refs
            g = ga_ref[...]
        else:
            ga_ref, gb_ref, w_ref, m_ref, v_ref, g_ref, d_ref, nm_ref, nv_ref = refs
            g = ga_ref[...] + gb_ref[...]
        g_ref[...] = g
        d_ref[...], nm_ref[...], nv_ref[...] = _adamw(w_ref[...], g, m_ref[...], v_ref[...])

    spec = pl.BlockSpec((tr, LANES), lambda i: (i, 0))
    args = [ga, w, m, v] if gb is None else [ga, gb, w, m, v]
    return pl.pallas_call(
        body, name=name, grid=(R // tr,), in_specs=[spec] * len(args), out_specs=[spec] * 4,
        out_shape=[jax.ShapeDtypeStruct((R, LANES), F32)] * 4,
        compiler_params=_params(("parallel",)),
    )(*args)


def allsum_small_call(v):
    R = v.shape[0]

    def body(v_ref, out_ref, buf, send_sems, recv_sems):
        x, y, c = _place()
        me = 4 * x + 2 * y + c
        buf[me] = v_ref[...]
        sends = []
        for k in range(1, N_DEV):
            fx, fy, fc = (k >> 2) & 1, (k >> 1) & 1, k & 1
            cp = pltpu.make_async_remote_copy(
                src_ref=v_ref, dst_ref=buf.at[me], send_sem=send_sems.at[k - 1], recv_sem=recv_sems.at[k - 1],
                device_id=(x ^ fx, y ^ fy, c ^ fc), device_id_type=MESH)
            cp.start()
            sends.append(cp)
        for k in range(1, N_DEV):
            pltpu.make_async_remote_copy(
                src_ref=v_ref, dst_ref=buf.at[me ^ k], send_sem=send_sems.at[k - 1], recv_sem=recv_sems.at[k - 1],
                device_id=(x, y, c), device_id_type=MESH).wait_recv()
        acc = buf[0]
        for d in range(1, N_DEV):
            acc = acc + buf[d]
        out_ref[...] = acc
        for cp in sends:
            cp.wait_send()

    vm = pl.BlockSpec(memory_space=pltpu.VMEM)
    return pl.pallas_call(
        body, name="allsum_small", in_specs=[vm], out_specs=vm,
        out_shape=jax.ShapeDtypeStruct((R, LANES), F32),
        scratch_shapes=[pltpu.VMEM((N_DEV, R, LANES), F32), pltpu.SemaphoreType.DMA((N_DEV - 1,)),
                        pltpu.SemaphoreType.DMA((N_DEV - 1,))],
        compiler_params=_params(),
    )(v)


def _slab(flat, mult):
    n = flat.shape[-1]
    rows = -(-n // (LANES * mult)) * mult
    flat = jnp.pad(flat, [(0, 0)] * (flat.ndim - 1) + [(0, rows * LANES - n)])
    return flat.reshape(flat.shape[:-1] + (rows, LANES))


BIG_ROWS_MULT = 1024


def pack_shards(shards):
    return _slab(jnp.concatenate([shards[n].reshape(-1) for n in BIG_W]), BIG_ROWS_MULT)


def unpack_shards(slab):
    flat = slab.reshape(-1)
    out, off = {}, 0
    for n in BIG_W:
        r, c, _ = BIG_SHARD[n]
        out[n] = flat[off:off + r * c].reshape(r, c)
        off += r * c
    return out


def unpack_full(slabs):
    flat = slabs.reshape(N_CHIPS, -1)
    out, off = {}, 0
    for n in BIG_W:
        r, c, by_col = BIG_SHARD[n]
        blk = flat[:, off:off + r * c].reshape(N_CHIPS, r, c)
        out[n] = blk.transpose(1, 0, 2).reshape(r, N_CHIPS * c) if by_col else blk.reshape(N_CHIPS * r, c)
        off += r * c
    return out


def pack_full(full):
    parts = []
    for n in BIG_W:
        r, c, by_col = BIG_SHARD[n]
        a = full[n]
        a = a.reshape(r, N_CHIPS, c).transpose(1, 0, 2) if by_col else a.reshape(N_CHIPS, r, c)
        parts.append(a.reshape(N_CHIPS, r * c))
    return _slab(jnp.concatenate(parts, axis=1), BIG_ROWS_MULT)


def pack_small(vals):
    return _slab(jnp.concatenate([vals[n].reshape(-1) for n in SMALL_W] + [vals["loss"].reshape(-1)]), 8)


def unpack_small(slab, shapes):
    flat = slab.reshape(-1)
    out, off = {}, 0
    for n in SMALL_W:
        out[n] = flat[off:off + SMALL_N[n]].reshape(shapes[n])
        off += SMALL_N[n]
    out["loss"] = flat[off]
    return out


def _split_heads(w, a, b):
    r = w.shape[0]
    w3 = w.reshape(r, HEADS, a + b)
    return w3[:, :, :a].reshape(r, HEADS * a), w3[:, :, a:].reshape(r, HEADS * b)


def _merge_heads(wa, wb, a, b):
    r = wa.shape[0]
    return jnp.concatenate([wa.reshape(r, HEADS, a), wb.reshape(r, HEADS, b)], axis=2).reshape(r, HEADS * (a + b))


def kernel(x, positions, g_mix, w_in, g_cq, w_uq, g_ckv, w_ukv, g_sb_out, g_mla_out, w_out, g_ffn, w_up, conv_w, conv_b, w_down, g_final, loss_target, m_g_mix, m_w_in, m_g_cq, m_w_uq, m_g_ckv, m_w_ukv, m_g_sb_out, m_g_mla_out, m_w_out, m_g_ffn, m_w_up, m_conv_w, m_conv_b, m_w_down, m_g_final, v_g_mix, v_w_in, v_g_cq, v_w_uq, v_g_ckv, v_w_ukv, v_g_sb_out, v_g_mla_out, v_w_out, v_g_ffn, v_w_up, v_conv_w, v_conv_b, v_w_down, v_g_final):
    given = dict(locals())
    B, S, _ = x.shape
    T = B * S
    w_big = {n: given[n][0] for n in BIG_W}
    m_big = {n: given["m_" + n][0] for n in BIG_W}
    v_big = {n: given["v_" + n][0] for n in BIG_W}

    w_slab = pack_shards(w_big)
    full = unpack_full(gather_chips_call("gather_weights", w_slab.astype(BF16)))
    cw_slab = _slab(w_big["conv_w"].reshape(-1), 8)
    cw_all = gather_chips_call("gather_conv_w", cw_slab).reshape(N_CHIPS, -1)
    r, c, _ = BIG_SHARD["conv_w"]
    conv_w_full = cw_all[:, :r * c].reshape(N_CHIPS, r, c).transpose(1, 0, 2).reshape(r, N_CHIPS * c)
    w_in_p = jnp.pad(full["w_in"], ((0, 0), (0, IN_COLS_PAD - IN_COLS)))
    w_uq_p = jnp.concatenate(_split_heads(full["w_uq"], HEAD_DIM, ROPE_DIM), axis=1)
    w_ukv_p = jnp.concatenate(_split_heads(full["w_ukv"], HEAD_DIM, HEAD_DIM), axis=1)

    x2d = x.reshape(T, D_MODEL)
    half = ROPE_DIM // 2
    inv_freq = 1.0 / (ROPE_BASE ** (jnp.arange(half, dtype=F32) * (2.0 / ROPE_DIM)))
    cos, sin = rope_tab_call(positions.reshape(T, 1), jnp.tile(inv_freq, LANES // half).reshape(1, LANES))
    h = rmsnorm_fwd_call("norm_mix", x2d, g_mix)
    p = matmul_call("proj_in", h, w_in_p, "nn", tn=IN_COLS_PAD // 2)
    qn, qr, kn, vm, krt, cqn, ckvn = mla_prep_fwd_call(p, cos, sin, g_cq, g_ckv, w_uq_p, w_ukv_p)
    o_sb, lt_sb = sb_fwd_call(p, B, S)
    o_mla, lse = mla_fwd_call(qn, qr, kn, krt, vm, B, S)
    o_cat = outnorm_fwd_call(o_sb, o_mla, g_sb_out, g_mla_out)
    x1 = matmul_call("proj_out", o_cat, full["w_out"], "nn", res=x2d)
    hn = rmsnorm_fwd_call("norm_ffn", x1, g_ffn)
    u = matmul_call("ffn_up", hn, full["w_up"], "nn", tn=2 * D_FF // 4)
    act = conv_act_fwd_call(u, conv_w_full, conv_b, B, S)
    x2 = matmul_call("ffn_down", act, full["w_down"], "nn", res=x1)
    dx2, loss_row, dg_final = final_loss_call(x2, g_final.reshape(1, D_MODEL), loss_target.reshape(T, D_MODEL))

    gw = {}
    gw["w_down"] = wgrad_call("wgrad_down", act, dx2, tn=512)
    da = matmul_call("ffn_down_bwd", dx2, full["w_down"], "nt", tn=D_FF // 2)
    du_g, du_v, dcw_g, dcw_v, dcb_g, dcb_v = conv_act_bwd_call(u, da, conv_w_full, conv_b, B, S)
    du = jnp.concatenate([du_g, du_v], axis=1)
    gw["conv_w"] = jnp.concatenate([dcw_g, dcw_v], axis=1)
    gw["w_up"] = wgrad_call("wgrad_up", hn, du, tn=2 * D_FF // 4)
    dhn = matmul_call("ffn_up_bwd", du, full["w_up"], "nt", tn=512)
    dx1, dg_ffn = rmsnorm_bwd_call("norm_ffn_bwd", x1, g_ffn, dhn, dx2)
    gw["w_out"] = wgrad_call("wgrad_out", o_cat, dx1)
    do_cat = matmul_call("proj_out_bwd", dx1, full["w_out"], "nt")
    do_sb, do_mla, dg_sb_out, dg_mla_out = outnorm_bwd_call(o_sb, o_mla, g_sb_out, g_mla_out, do_cat)
    dq_sb, dk_sb, dv_sb = sb_bwd_call(p, lt_sb, do_sb, B, S)
    dqn, dqr4, dkn, dvm, dkrt4 = mla_bwd_call(qn, qr, kn, krt, vm, o_mla, lse, do_mla, B, S)
    dcq, dckvr, dq_cat, dkv_cat, dg_cq, dg_ckv = mla_prep_bwd_call(
        p, cos, sin, g_cq, g_ckv, w_uq_p, w_ukv_p, dqn, dqr4, dkn, dvm, dkrt4)
    dw_uq_p = wgrad_call("wgrad_uq", cqn, dq_cat)
    dw_ukv_p = wgrad_call("wgrad_ukv", ckvn, dkv_cat)
    gw["w_uq"] = _merge_heads(dw_uq_p[:, :ATT_W], dw_uq_p[:, ATT_W:], HEAD_DIM, ROPE_DIM)
    gw["w_ukv"] = _merge_heads(dw_ukv_p[:, :ATT_W], dw_ukv_p[:, ATT_W:], HEAD_DIM, HEAD_DIM)
    dp = jnp.concatenate([dq_sb, dk_sb, dv_sb, dcq, dckvr], axis=1)
    gw["w_in"] = wgrad_call("wgrad_in", h, dp, tn=IN_COLS_PAD // 2)[:, :IN_COLS]
    dh = matmul_call("proj_in_bwd", dp, w_in_p, "nt")
    grad_x, dg_mix = rmsnorm_bwd_call("norm_mix_bwd", x2d, g_mix, dh, dx1)

    xi, yi, ci = _place()
    chip = (2 * xi + yi).astype(jnp.int32).reshape(1)
    parts = pack_full(gw)
    got = scatter_chips_call("scatter_grads", parts)
    mine = sum_parts_call("sum_chips", parts, got, chip)
    other = sibling_swap_call("swap_sibling", mine)
    g_slab, d_slab, nm_slab, nv_slab = adamw_call(
        "adamw_big", mine, other, w_slab, pack_shards(m_big), pack_shards(v_big), tr=1024)
    big_out = [unpack_shards(s) for s in (g_slab, d_slab, nm_slab, nv_slab)]

    shapes = {n: given[n].shape for n in SMALL_W}
    small_g = {"g_mix": dg_mix, "g_cq": dg_cq, "g_ckv": dg_ckv, "g_sb_out": dg_sb_out, "g_mla_out": dg_mla_out,
               "g_ffn": dg_ffn, "conv_b": jnp.concatenate([dcb_g, dcb_v], axis=1), "g_final": dg_final,
               "loss": loss_row[0, :1]}
    zero = jnp.zeros((1,), F32)
    gs_slab = allsum_small_call(pack_small(small_g))
    ws = pack_small({**{n: given[n] for n in SMALL_W}, "loss": zero})
    ms = pack_small({**{n: given["m_" + n] for n in SMALL_W}, "loss": zero})
    vs = pack_small({**{n: given["v_" + n] for n in SMALL_W}, "loss": zero})
    _, ds_slab, nms_slab, nvs_slab = adamw_call("adamw_small", gs_slab, None, ws, ms, vs, tr=gs_slab.shape[0])
    small_out = [unpack_small(s, shapes) for s in (gs_slab, ds_slab, nms_slab, nvs_slab)]

    weights = ("g_mix", "w_in", "g_cq", "w_uq", "g_ckv", "w_ukv", "g_sb_out", "g_mla_out", "w_out", "g_ffn",
               "w_up", "conv_w", "conv_b", "w_down", "g_final")
    outs = [small_out[0]["loss"], grad_x.reshape(B, S, D_MODEL)]
    for k in range(4):
        for n in weights:
            outs.append(big_out[k][n][None] if n in BIG_W else small_out[k][n])
    return tuple(outs)
```

```python
import functools

import jax
import jax.numpy as jnp
from jax import lax
from jax.experimental import pallas as pl
from jax.experimental.pallas import tpu as pltpu

F32 = jnp.float32
BF16 = jnp.bfloat16
MESH = pl.DeviceIdType.MESH

D_MODEL = 1024
HEADS = 8
HEAD_DIM = 64
ATT_W = HEADS * HEAD_DIM
ROPE_DIM = 32
ROPE_W = HEADS * ROPE_DIM
QK_DIM = HEAD_DIM + ROPE_DIM
Q_RANK = 384
KV_RANK = 256
D_FF = 2816
IN_COLS = 2208
IN_COLS_PAD = 2304
EPS = 1e-6
ROPE_BASE = 10000.0
SB_SCALE = HEAD_DIM ** -0.5
MLA_SCALE = QK_DIM ** -0.5
LANES = 128
N_CHIPS = 4
N_DEV = 8
VMEM_LIMIT = 48 * 1024 * 1024
ATT_TQ = 256
ATT_TK = 256
NEG_BIG = -1e30

ADAM_LR = 0.001
ADAM_B1 = 0.9
ADAM_B2 = 0.999
ADAM_EPS = 1e-08
ADAM_WD = 0.01
ADAM_STEP = 10

BIG_W = ("w_in", "w_uq", "w_ukv", "w_out", "w_up", "conv_w", "w_down")
BIG_SHARD = {
    "w_in": (D_MODEL, IN_COLS // 4, True),
    "w_uq": (Q_RANK, HEADS * QK_DIM // 4, True),
    "w_ukv": (KV_RANK, 2 * ATT_W // 4, True),
    "w_out": (2 * ATT_W // 4, D_MODEL, False),
    "w_up": (D_MODEL, 2 * D_FF // 4, True),
    "conv_w": (3, 2 * D_FF // 4, True),
    "w_down": (D_FF // 4, D_MODEL, False),
}
SMALL_W = ("g_mix", "g_cq", "g_ckv", "g_sb_out", "g_mla_out", "g_ffn", "conv_b", "g_final")
SMALL_N = {"g_mix": D_MODEL, "g_cq": Q_RANK, "g_ckv": KV_RANK, "g_sb_out": ATT_W, "g_mla_out": ATT_W,
           "g_ffn": D_MODEL, "conv_b": 2 * D_FF, "g_final": D_MODEL}


def _params(sem=None, **kw):
    return pltpu.CompilerParams(dimension_semantics=sem, vmem_limit_bytes=VMEM_LIMIT, **kw)


def _dot(a, b, dims):
    return lax.dot_general(a, b, (dims, ((), ())), preferred_element_type=F32)


def _nn(a, b):
    return _dot(a, b, ((1,), (0,)))


def _nt(a, b):
    return _dot(a, b, ((1,), (1,)))


def _tn(a, b):
    return _dot(a, b, ((0,), (0,)))


def _split2(x):
    hi = x.astype(BF16)
    lo = (x - hi.astype(F32)).astype(BF16)
    return hi, lo


def _split3(x):
    hi = x.astype(BF16)
    r1 = x - hi.astype(F32)
    mid = r1.astype(BF16)
    return hi, mid, (r1 - mid.astype(F32)).astype(BF16)


def _rms_r(x, d):
    return lax.rsqrt(jnp.sum(x * x, axis=-1, keepdims=True) * (1.0 / d) + EPS)


def _rms_bwd(x, g, dy, d):
    r = _rms_r(x, d)
    xhat = x * r
    gy = dy * g
    dx = r * (gy - xhat * (jnp.sum(xhat * gy, axis=-1, keepdims=True) * (1.0 / d)))
    return dx, jnp.sum(dy * xhat, axis=0, keepdims=True)


def _rot(x):
    lane = lax.broadcasted_iota(jnp.int32, x.shape, x.ndim - 1)
    n = x.shape[-1]
    return jnp.where((lane & 31) < 16, pltpu.roll(x, n - 16, x.ndim - 1), pltpu.roll(x, 16, x.ndim - 1))


def _fold4(x):
    return x + pltpu.roll(x, 32, 1) + pltpu.roll(x, 64, 1) + pltpu.roll(x, 96, 1)


def matmul_call(name, a, b, mode, out_dtype=F32, res=None, tm=512, tn=None):
    M, K = a.shape
    N = b.shape[1] if mode == "nn" else b.shape[0]
    tn = N if tn is None else tn
    assert M % tm == 0 and N % tn == 0

    def body(*refs):
        if res is None:
            a_ref, b_ref, o_ref = refs
        else:
            a_ref, b_ref, r_ref, o_ref = refs
        av = a_ref[...].astype(BF16)
        bv = b_ref[...].astype(BF16)
        acc = _nn(av, bv) if mode == "nn" else _nt(av, bv)
        if res is not None:
            acc = r_ref[...] + acc
        o_ref[...] = acc.astype(out_dtype)

    in_specs = [pl.BlockSpec((tm, K), lambda j, i: (i, 0))]
    if mode == "nn":
        in_specs.append(pl.BlockSpec((K, tn), lambda j, i: (0, j)))
    else:
        in_specs.append(pl.BlockSpec((tn, K), lambda j, i: (j, 0)))
    args = [a, b]
    if res is not None:
        in_specs.append(pl.BlockSpec((tm, tn), lambda j, i: (i, j)))
        args.append(res)
    return pl.pallas_call(
        body, name=name, grid=(N // tn, M // tm), in_specs=in_specs,
        out_specs=pl.BlockSpec((tm, tn), lambda j, i: (i, j)),
        out_shape=jax.ShapeDtypeStruct((M, N), out_dtype),
        compiler_params=_params(("parallel", "parallel")),
    )(*args)


def wgrad_call(name, a, b, tn=None, tt=512, by_chip=False):
    T, M = a.shape
    N = b.shape[1]
    tn = N if tn is None else tn
    assert T % tt == 0 and N % tn == 0
    if by_chip:
        out_spec = pl.BlockSpec((None, M, tn), lambda j, t: (j, 0, 0))
        out_shape = jax.ShapeDtypeStruct((N // tn, M, tn), F32)
    else:
        out_spec = pl.BlockSpec((M, tn), lambda j, t: (0, j))
        out_shape = jax.ShapeDtypeStruct((M, N), F32)

    def body(a_ref, b_ref, o_ref):
        @pl.when(pl.program_id(1) == 0)
        def _():
            o_ref[...] = jnp.zeros_like(o_ref)

        o_ref[...] += _tn(a_ref[...].astype(BF16), b_ref[...].astype(BF16))

    return pl.pallas_call(
        body, name=name, grid=(N // tn, T // tt),
        in_specs=[pl.BlockSpec((tt, M), lambda j, t: (t, 0)), pl.BlockSpec((tt, tn), lambda j, t: (t, j))],
        out_specs=out_spec, out_shape=out_shape,
        compiler_params=_params(("parallel", "arbitrary")),
    )(a, b)


def rmsnorm_fwd_call(name, x, g, tm=512):
    T, d = x.shape

    def body(x_ref, g_ref, o_ref):
        x = x_ref[...]
        o_ref[...] = ((x * _rms_r(x, d)) * g_ref[...]).astype(BF16)

    return pl.pallas_call(
        body, name=name, grid=(T // tm,),
        in_specs=[pl.BlockSpec((tm, d), lambda i: (i, 0)), pl.BlockSpec((1, d), lambda i: (0, 0))],
        out_specs=pl.BlockSpec((tm, d), lambda i: (i, 0)),
        out_shape=jax.ShapeDtypeStruct((T, d), BF16),
        compiler_params=_params(("parallel",)),
    )(x, g)


def rmsnorm_bwd_call(name, x, g, dy, res, tm=512):
    T, d = x.shape

    def body(x_ref, g_ref, dy_ref, r_ref, dx_ref, dg_ref):
        @pl.when(pl.program_id(0) == 0)
        def _():
            dg_ref[...] = jnp.zeros_like(dg_ref)

        dx, dg = _rms_bwd(x_ref[...], g_ref[...], dy_ref[...], d)
        dx_ref[...] = r_ref[...] + dx
        dg_ref[...] += dg

    row = pl.BlockSpec((tm, d), lambda i: (i, 0))
    vec = pl.BlockSpec((1, d), lambda i: (0, 0))
    return pl.pallas_call(
        body, name=name, grid=(T // tm,), in_specs=[row, vec, row, row], out_specs=[row, vec],
        out_shape=[jax.ShapeDtypeStruct((T, d), F32), jax.ShapeDtypeStruct((1, d), F32)],
        compiler_params=_params(("arbitrary",)),
    )(x, g, dy, res)


def outnorm_fwd_call(o_sb, o_mla, g_sb, g_mla, tm=512):
    T = o_sb.shape[0]

    def body(a_ref, b_ref, ga_ref, gb_ref, o_ref):
        a = a_ref[...]
        b = b_ref[...]
        ya = (a * _rms_r(a, ATT_W)) * ga_ref[...]
        yb = (b * _rms_r(b, ATT_W)) * gb_ref[...]
        o_ref[...] = jnp.concatenate([ya, yb], axis=1).astype(BF16)

    row = pl.BlockSpec((tm, ATT_W), lambda i: (i, 0))
    vec = pl.BlockSpec((1, ATT_W), lambda i: (0, 0))
    return pl.pallas_call(
        body, name="outnorm_fwd", grid=(T // tm,), in_specs=[row, row, vec, vec],
        out_specs=pl.BlockSpec((tm, 2 * ATT_W), lambda i: (i, 0)),
        out_shape=jax.ShapeDtypeStruct((T, 2 * ATT_W), BF16),
        compiler_params=_params(("parallel",)),
    )(o_sb, o_mla, g_sb, g_mla)


def outnorm_bwd_call(o_sb, o_mla, g_sb, g_mla, do_cat, tm=512):
    T = o_sb.shape[0]

    def body(a_ref, b_ref, ga_ref, gb_ref, d_ref, da_ref, db_ref, dga_ref, dgb_ref):
        @pl.when(pl.program_id(0) == 0)
        def _():
            dga_ref[...] = jnp.zeros_like(dga_ref)
            dgb_ref[...] = jnp.zeros_like(dgb_ref)

        d = d_ref[...]
        da, dga = _rms_bwd(a_ref[...], ga_ref[...], d[:, :ATT_W], ATT_W)
        db, dgb = _rms_bwd(b_ref[...], gb_ref[...], d[:, ATT_W:], ATT_W)
        da_ref[...] = da
        db_ref[...] = db
        dga_ref[...] += dga
        dgb_ref[...] += dgb

    row = pl.BlockSpec((tm, ATT_W), lambda i: (i, 0))
    vec = pl.BlockSpec((1, ATT_W), lambda i: (0, 0))
    return pl.pallas_call(
        body, name="outnorm_bwd", grid=(T // tm,),
        in_specs=[row, row, vec, vec, pl.BlockSpec((tm, 2 * ATT_W), lambda i: (i, 0))],
        out_specs=[row, row, vec, vec],
        out_shape=[jax.ShapeDtypeStruct((T, ATT_W), F32), jax.ShapeDtypeStruct((T, ATT_W), F32),
                   jax.ShapeDtypeStruct((1, ATT_W), F32), jax.ShapeDtypeStruct((1, ATT_W), F32)],
        compiler_params=_params(("arbitrary",)),
    )(o_sb, o_mla, g_sb, g_mla, do_cat)


def final_loss_call(x2, g, target, tm=512):
    T, d = x2.shape

    def body(x_ref, g_ref, t_ref, dx_ref, loss_ref, dg_ref):
        @pl.when(pl.program_id(0) == 0)
        def _():
            loss_ref[...] = jnp.zeros_like(loss_ref)
            dg_ref[...] = jnp.zeros_like(dg_ref)

        x = x_ref[...]
        g = g_ref[...]
        y = (x * _rms_r(x, d)) * g
        err = y - t_ref[...]
        loss_ref[...] += jnp.sum(jnp.sum(err * err, axis=1, keepdims=True), axis=0, keepdims=True) * (0.5 / d)
        dx, dg = _rms_bwd(x, g, err * (1.0 / d), d)
        dx_ref[...] = dx
        dg_ref[...] += dg

    row = pl.BlockSpec((tm, d), lambda i: (i, 0))
    vec = pl.BlockSpec((1, d), lambda i: (0, 0))
    return pl.pallas_call(
        body, name="final_loss", grid=(T // tm,), in_specs=[row, vec, row],
        out_specs=[row, pl.BlockSpec((1, LANES), lambda i: (0, 0)), vec],
        out_shape=[jax.ShapeDtypeStruct((T, d), F32), jax.ShapeDtypeStruct((1, LANES), F32),
                   jax.ShapeDtypeStruct((1, d), F32)],
        compiler_params=_params(("arbitrary",)),
    )(x2, g, target)


def rope_tab_call(pos, inv_freq, tm=512):
    T = pos.shape[0]

    def body(p_ref, f_ref, c_ref, s_ref):
        ang = p_ref[...].astype(F32) * f_ref[...]
        lane = lax.broadcasted_iota(jnp.int32, ang.shape, 1)
        sn = jnp.sin(ang)
        c_ref[...] = jnp.cos(ang)
        s_ref[...] = jnp.where((lane & 31) < 16, -sn, sn)

    row = pl.BlockSpec((tm, LANES), lambda i: (i, 0))
    return pl.pallas_call(
        body, name="rope_tab", grid=(T // tm,),
        in_specs=[pl.BlockSpec((tm, 1), lambda i: (i, 0)), pl.BlockSpec((1, LANES), lambda i: (0, 0))],
        out_specs=[row, row],
        out_shape=[jax.ShapeDtypeStruct((T, LANES), F32)] * 2,
        compiler_params=_params(("parallel",)),
    )(pos, inv_freq)


def mla_prep_fwd_call(p, cos, sin, g_cq, g_ckv, w_uq_p, w_ukv_p, tm=512):
    T = p.shape[0]

    def body(cq_ref, ckvr_ref, c_ref, s_ref, gq_ref, gkv_ref, wq_ref, wkv_ref,
             qn_ref, qr_ref, kn_ref, vm_ref, krt_ref, cqn_ref, ckvn_ref):
        c = c_ref[...]
        s = s_ref[...]
        cq = cq_ref[...]
        cqn = ((cq * _rms_r(cq, Q_RANK)) * gq_ref[...]).astype(BF16)
        cqn_ref[...] = cqn
        q = _nn(cqn, wq_ref[...])
        qn_ref[...] = q[:, :ATT_W].astype(BF16)
        for g in range(ROPE_W // LANES):
            qr = q[:, ATT_W + g * LANES:ATT_W + (g + 1) * LANES]
            qr_ref[:, g * LANES:(g + 1) * LANES] = (qr * c + _rot(qr) * s).astype(BF16)
        ckvr = ckvr_ref[...]
        ckv = ckvr[:, :KV_RANK]
        ckvn = ((ckv * _rms_r(ckv, KV_RANK)) * gkv_ref[...]).astype(BF16)
        ckvn_ref[...] = ckvn
        kv = _nn(ckvn, wkv_ref[...])
        kn_ref[...] = kv[:, :ATT_W].astype(BF16)
        vm_ref[...] = kv[:, ATT_W:].astype(BF16)
        kr = _fold4(ckvr[:, KV_RANK:])
        krt_ref[...] = (kr * c + _rot(kr) * s).astype(BF16)

    def row(w, j=0):
        return pl.BlockSpec((tm, w), lambda i: (i, j))

    def full(a):
        return pl.BlockSpec(a.shape, lambda i: (0, 0))

    return pl.pallas_call(
        body, name="mla_prep_fwd", grid=(T // tm,),
        in_specs=[row(Q_RANK, 4), row(Q_RANK, 5), row(LANES), row(LANES), full(g_cq), full(g_ckv),
                  full(w_uq_p), full(w_ukv_p)],
        out_specs=[row(ATT_W), row(ROPE_W), row(ATT_W), row(ATT_W), row(LANES), row(Q_RANK), row(KV_RANK)],
        out_shape=[jax.ShapeDtypeStruct((T, w), BF16) for w in (ATT_W, ROPE_W, ATT_W, ATT_W, LANES, Q_RANK, KV_RANK)],
        compiler_params=_params(("parallel",)),
    )(p, p, cos, sin, g_cq, g_ckv, w_uq_p, w_ukv_p)


def mla_prep_bwd_call(p, cos, sin, g_cq, g_ckv, w_uq_p, w_ukv_p, dqn, dqr4, dkn, dvm, dkrt4, tm=512):
    T = p.shape[0]

    def body(cq_ref, ckvr_ref, c_ref, s_ref, gq_ref, gkv_ref, wq_ref, wkv_ref,
             dqn_ref, dqr4_ref, dkn_ref, dvm_ref, dkrt4_ref,
             dcq_ref, dckvr_ref, dq_ref, dkv_ref, dgq_ref, dgkv_ref):
        @pl.when(pl.program_id(0) == 0)
        def _():
            dgq_ref[...] = jnp.zeros_like(dgq_ref)
            dgkv_ref[...] = jnp.zeros_like(dgkv_ref)

        c = c_ref[...]
        s = s_ref[...]
        d4 = dqr4_ref[...]
        dqr = [d4[:, :128] + d4[:, 128:256], d4[:, 256:384] + d4[:, 384:]]
        dqr = [t * c + _rot(t * s) for t in dqr]
        dq = jnp.concatenate([dqn_ref[...]] + dqr, axis=1).astype(BF16)
        dq_ref[...] = dq
        dcq, dgq = _rms_bwd(cq_ref[...], gq_ref[...], _nt(dq, wq_ref[...]), Q_RANK)
        dcq_ref[...] = dcq
        dgq_ref[...] += dgq
        dkv = jnp.concatenate([dkn_ref[...], dvm_ref[...]], axis=1).astype(BF16)
        dkv_ref[...] = dkv
        ckvr = ckvr_ref[...]
        dckv, dgkv = _rms_bwd(ckvr[:, :KV_RANK], gkv_ref[...], _nt(dkv, wkv_ref[...]), KV_RANK)
        dgkv_ref[...] += dgkv
        k4 = dkrt4_ref[...]
        dkr = _fold4(k4[:, :128] + k4[:, 128:256] + k4[:, 256:384] + k4[:, 384:])
        dkr = dkr * c + _rot(dkr * s)
        lane = lax.broadcasted_iota(jnp.int32, dkr.shape, 1)
        dckvr_ref[...] = jnp.concatenate([dckv, jnp.where(lane < ROPE_DIM, dkr, 0.0)], axis=1)

    def row(w, j=0):
        return pl.BlockSpec((tm, w), lambda i: (i, j))

    def full(a):
        return pl.BlockSpec(a.shape, lambda i: (0, 0))

    return pl.pallas_call(
        body, name="mla_prep_bwd", grid=(T // tm,),
        in_specs=[row(Q_RANK, 4), row(Q_RANK, 5), row(LANES), row(LANES), full(g_cq), full(g_ckv),
                  full(w_uq_p), full(w_ukv_p), row(ATT_W), row(ATT_W), row(ATT_W), row(ATT_W), row(ATT_W)],
        out_specs=[row(Q_RANK), row(Q_RANK), row(ATT_W + ROPE_W), row(2 * ATT_W),
                   pl.BlockSpec((1, Q_RANK), lambda i: (0, 0)), pl.BlockSpec((1, KV_RANK), lambda i: (0, 0))],
        out_shape=[jax.ShapeDtypeStruct((T, Q_RANK), F32), jax.ShapeDtypeStruct((T, Q_RANK), F32),
                   jax.ShapeDtypeStruct((T, ATT_W + ROPE_W), BF16), jax.ShapeDtypeStruct((T, 2 * ATT_W), BF16),
                   jax.ShapeDtypeStruct((1, Q_RANK), F32), jax.ShapeDtypeStruct((1, KV_RANK), F32)],
        compiler_params=_params(("arbitrary",)),
    )(p, p, cos, sin, g_cq, g_ckv, w_uq_p, w_ukv_p, dqn, dqr4, dkn, dvm, dkrt4)


def _iota2(shape, axis):
    return lax.broadcasted_iota(jnp.int32, shape, axis)


def _head_masks():
    lane = _iota2((1, LANES), 1)
    return lane < HEAD_DIM, lane >= HEAD_DIM


def _pair(x, masks, dtype=BF16):
    return [jnp.where(m, x, 0.0).astype(dtype) for m in masks]


def _log_gates(z):
    soft = jnp.log1p(jnp.exp(-jnp.abs(z)))
    return jnp.minimum(z, 0.0) - soft, -jnp.maximum(z, 0.0) - soft


def _lane_selector(group):
    return jnp.where(_iota2((16, LANES), 1) // group == _iota2((16, LANES), 0), 1.0, 0.0).astype(BF16)


def _rows8(sel_t, x):
    hi = x.astype(BF16)
    r1 = x - hi.astype(F32)
    mid = r1.astype(BF16)
    lo = (r1 - mid.astype(F32)).astype(BF16)
    return _nt(sel_t, hi) + _nt(sel_t, mid) + _nt(sel_t, lo)


def _row_of(x8, j):
    return jnp.sum(jnp.where(_iota2(x8.shape, 0) == j, x8, 0.0), axis=0, keepdims=True)


def sb_fwd_call(p, B, S):
    T = B * S
    TQ, TK = ATT_TQ, ATT_TK
    nq = S // TQ

    def body(q_ref, k_ref, v_ref, o_ref, lt_ref):
        qi = pl.program_id(2)
        masks = _head_masks()
        qm = _pair(q_ref[...] * SB_SCALE, masks)
        row = _iota2((TQ, TK), 0)
        col = _iota2((TQ, TK), 1)
        tri = jnp.where(row > col, 1.0, 0.0).astype(BF16)
        tri2 = jnp.concatenate([tri, tri], axis=0)
        vis = col < row
        o_ref[...] = jnp.zeros_like(o_ref)

        def step(kb, carry, diag):
            k0 = pl.multiple_of(kb * TK, TK)
            k = k_ref[pl.ds(k0, TK), :].astype(BF16)
            vm = _pair(v_ref[pl.ds(k0, TK), :], masks)
            acc = jnp.zeros((TQ, LANES), F32)
            out = []
            for j in range(2):
                lb, lk = _log_gates(_nt(qm[j], k))
                if diag:
                    lk = jnp.where(vis, lk, 0.0)
                hi, lo = _split2(lk)
                tail = _nn(jnp.concatenate([hi, lo], axis=1), tri2) + carry[j]
                a = jnp.exp(lb + tail)
                if diag:
                    a = jnp.where(vis, a, 0.0)
                acc = acc + _nn(a.astype(BF16), vm[j])
                out.append(carry[j] + jnp.sum(lk, axis=1, keepdims=True))
            o_ref[...] += acc
            return tuple(out)

        zero = jnp.zeros((TQ, 1), F32)
        carry = step(qi, (zero, zero), True)
        c0, c1 = lax.fori_loop(0, qi, lambda i, c: step(qi - 1 - i, c, False), carry)
        lane = _iota2((TQ, LANES), 1)
        lt_ref[...] = jnp.where(lane == 0, c0, jnp.where(lane == 1, c1, 0.0))

    qspec = pl.BlockSpec((TQ, LANES), lambda b, h, i: (b * nq + i, h))
    return pl.pallas_call(
        body, name="sb_fwd", grid=(B, HEADS // 2, nq),
        in_specs=[qspec,
                  pl.BlockSpec((S, LANES), lambda b, h, i: (b, 4 + h)),
                  pl.BlockSpec((S, LANES), lambda b, h, i: (b, 8 + h))],
        out_specs=[qspec, qspec],
        out_shape=[jax.ShapeDtypeStruct((T, ATT_W), F32)] * 2,
        compiler_params=_params(("parallel", "parallel", "arbitrary")),
    )(p, p, p)


def sb_bwd_call(p, lt, do, B, S):
    T = B * S
    TQ, TK = ATT_TQ, ATT_TK
    nq = S // TQ

    def body(q_ref, k_ref, v_ref, lt_ref, do_ref, dq_ref, dk_ref, dv_ref):
        qi = pl.program_id(2)

        @pl.when(qi == 0)
        def _():
            dk_ref[...] = jnp.zeros_like(dk_ref)
            dv_ref[...] = jnp.zeros_like(dv_ref)

        masks = _head_masks()
        qm = _pair(q_ref[...] * SB_SCALE, masks)
        dom = _pair(do_ref[...], masks)
        l8 = _rows8(_lane_selector(1), lt_ref[...])
        ltot = [_row_of(l8, j) for j in range(2)]
        row = _iota2((TK, TQ), 0)
        col = _iota2((TK, TQ), 1)
        incl = jnp.where(col <= row, 1.0, 0.0).astype(BF16)
        incl3 = jnp.concatenate([incl, incl, incl], axis=1)
        excl = jnp.where(col < row, 1.0, 0.0).astype(BF16)
        excl2 = jnp.concatenate([excl, excl], axis=1)
        vis = row < col
        dq_ref[...] = jnp.zeros_like(dq_ref)

        def step(kb, carry, diag):
            k0 = pl.multiple_of(kb * TK, TK)
            kf = k_ref[pl.ds(k0, TK), :]
            k = kf.astype(BF16)
            km = _pair(kf, masks)
            v = v_ref[pl.ds(k0, TK), :].astype(BF16)
            dq = jnp.zeros((TQ, LANES), F32)
            dk = jnp.zeros((TK, LANES), F32)
            dv = jnp.zeros((TK, LANES), F32)
            out = []
            for j in range(2):
                c_j, e_j = carry[2 * j], carry[2 * j + 1]
                lb, lk = _log_gates(_nt(k, qm[j]))
                if diag:
                    lk = jnp.where(vis, lk, 0.0)
                left = _nn(incl3, jnp.concatenate(_split3(lk), axis=0)) + c_j
                a = jnp.exp(lb + (ltot[j] - left))
                if diag:
                    a = jnp.where(vis, a, 0.0)
                e = a * _nt(v, dom[j])
                before = _nn(excl2, jnp.concatenate(_split2(e), axis=0)) + e_j
                dz = e - jnp.exp(lb) * (e + before)
                if diag:
                    dz = jnp.where(vis, dz, 0.0)
                dzb = dz.astype(BF16)
                dk = dk + _nn(dzb, qm[j])
                dv = dv + _nn(a.astype(BF16), dom[j])
                dq = dq + _tn(dzb, km[j])
                out += [c_j + jnp.sum(lk, axis=0, keepdims=True), e_j + jnp.sum(e, axis=0, keepdims=True)]
            dq_ref[...] += dq
            dk_ref[pl.ds(k0, TK), :] += dk
            dv_ref[pl.ds(k0, TK), :] += dv
            return tuple(out)

        zero = jnp.zeros((1, TQ), F32)
        carry = lax.fori_loop(0, qi, lambda i, c: step(i, c, False), (zero,) * 4)
        step(qi, carry, True)
        dq_ref[...] *= SB_SCALE

    qspec = pl.BlockSpec((TQ, LANES), lambda b, h, i: (b * nq + i, h))
    sspec = pl.BlockSpec((S, LANES), lambda b, h, i: (b, h))
    return pl.pallas_call(
        body, name="sb_bwd", grid=(B, HEADS // 2, nq),
        in_specs=[qspec,
                  pl.BlockSpec((S, LANES), lambda b, h, i: (b, 4 + h)),
                  pl.BlockSpec((S, LANES), lambda b, h, i: (b, 8 + h)),
                  qspec, qspec],
        out_specs=[qspec, sspec, sspec],
        out_shape=[jax.ShapeDtypeStruct((T, ATT_W), F32)] * 3,
        compiler_params=_params(("parallel", "parallel", "arbitrary")),
    )(p, p, p, lt, do)


def _rope_masks(hp):
    grp = _iota2((1, LANES), 1) // ROPE_DIM
    return [grp == ((2 * hp + j) % 4) for j in range(2)]


def mla_fwd_call(qn, qr, kn, krt, vm, B, S):
    T = B * S
    TQ, TK = ATT_TQ, ATT_TK
    nq = S // TQ

    def body(qn_ref, qr_ref, kn_ref, kr_ref, v_ref, o_ref, lse_ref):
        hp = pl.program_id(1)
        qi = pl.program_id(2)
        masks = _head_masks()
        rmasks = _rope_masks(hp)
        qnv = qn_ref[...]
        qrv = qr_ref[...]
        qcat = [jnp.concatenate([jnp.where(masks[j], qnv, 0), jnp.where(rmasks[j], qrv, 0)], axis=1).astype(BF16)
                for j in range(2)]
        row = _iota2((TQ, TK), 0)
        col = _iota2((TQ, TK), 1)
        vis = col <= row

        def step(kb, carry, diag):
            k0 = pl.multiple_of(kb * TK, TK)
            kcat = jnp.concatenate([kn_ref[pl.ds(k0, TK), :], kr_ref[pl.ds(k0, TK), :]], axis=1)
            vmk = _pair(v_ref[pl.ds(k0, TK), :], masks)
            acc = carry[4]
            out = []
            scale = []
            for j in range(2):
                m_j, l_j = carry[2 * j], carry[2 * j + 1]
                s = _nt(qcat[j], kcat) * MLA_SCALE
                if diag:
                    s = jnp.where(vis, s, NEG_BIG)
                m_new = jnp.maximum(m_j, jnp.max(s, axis=1, keepdims=True))
                alpha = jnp.exp(m_j - m_new)
                pexp = jnp.exp(s - m_new)
                out += [m_new, alpha * l_j + jnp.sum(pexp, axis=1, keepdims=True)]
                scale.append(alpha)
                acc_add = _nn(pexp.astype(BF16), vmk[j])
                acc = acc * jnp.where(masks[j], alpha, 1.0) + acc_add
            return tuple(out) + (acc,)

        neg = jnp.full((TQ, 1), NEG_BIG, F32)
        zero = jnp.zeros((TQ, 1), F32)
        carry = step(qi, (neg, zero, neg, zero, jnp.zeros((TQ, LANES), F32)), True)
        m0, l0, m1, l1, acc = lax.fori_loop(0, qi, lambda i, c: step(qi - 1 - i, c, False), carry)
        o_ref[...] = acc * jnp.where(masks[0], 1.0 / l0, 1.0 / l1)
        lane = _iota2((TQ, LANES), 1)
        lse_ref[...] = jnp.where(lane == 0, m0 + jnp.log(l0), jnp.where(lane == 1, m1 + jnp.log(l1), 0.0))

    qspec = pl.BlockSpec((TQ, LANES), lambda b, h, i: (b * nq + i, h))
    sspec = pl.BlockSpec((S, LANES), lambda b, h, i: (b, h))
    return pl.pallas_call(
        body, name="mla_fwd", grid=(B, HEADS // 2, nq),
        in_specs=[qspec, pl.BlockSpec((TQ, LANES), lambda b, h, i: (b * nq + i, h // 2)), sspec,
                  pl.BlockSpec((S, LANES), lambda b, h, i: (b, 0)), sspec],
        out_specs=[qspec, qspec],
        out_shape=[jax.ShapeDtypeStruct((T, ATT_W), F32)] * 2,
        compiler_params=_params(("parallel", "parallel", "arbitrary")),
    )(qn, qr, kn, krt, vm)


def mla_bwd_call(qn, qr, kn, krt, vm, o, lse, do, B, S):
    T = B * S
    TQ, TK = ATT_TQ, ATT_TK
    nq = S // TQ

    def body(qn_ref, qr_ref, kn_ref, kr_ref, v_ref, o_ref, lse_ref, do_ref,
             dqn_ref, dqr_ref, dkn_ref, dv_ref, dkr_ref):
        hp = pl.program_id(1)
        qi = pl.program_id(2)

        @pl.when(qi == 0)
        def _():
            dkn_ref[...] = jnp.zeros_like(dkn_ref)
            dv_ref[...] = jnp.zeros_like(dv_ref)
            dkr_ref[...] = jnp.zeros_like(dkr_ref)

        masks = _head_masks()
        rmasks = _rope_masks(hp)
        qnv = qn_ref[...]
        qrv = qr_ref[...]
        qcat = [jnp.concatenate([jnp.where(masks[j], qnv, 0), jnp.where(rmasks[j], qrv, 0)], axis=1).astype(BF16)
                for j in range(2)]
        do = do_ref[...]
        dom = _pair(do, masks)
        d8 = _rows8(_lane_selector(HEAD_DIM), do * o_ref[...])
        l8 = _rows8(_lane_selector(1), lse_ref[...])
        dsum = [_row_of(d8, j) for j in range(2)]
        lse = [_row_of(l8, j) for j in range(2)]
        row = _iota2((TK, TQ), 0)
        col = _iota2((TK, TQ), 1)
        vis = row <= col
        dqn_ref[...] = jnp.zeros_like(dqn_ref)
        dqr_ref[...] = jnp.zeros_like(dqr_ref)

        def step(kb, diag):
            k0 = pl.multiple_of(kb * TK, TK)
            knv = kn_ref[pl.ds(k0, TK), :]
            krv = kr_ref[pl.ds(k0, TK), :]
            kcat = jnp.concatenate([knv, krv], axis=1)
            v = v_ref[pl.ds(k0, TK), :]
            dq = jnp.zeros((TQ, 2 * LANES), F32)
            dk = jnp.zeros((TK, 2 * LANES), F32)
            dv = jnp.zeros((TK, LANES), F32)
            for j in range(2):
                s = _nt(kcat, qcat[j]) * MLA_SCALE
                pr = jnp.exp(s - lse[j])
                if diag:
                    pr = jnp.where(vis, pr, 0.0)
                ds = (pr * (_nt(v, dom[j]) - dsum[j]) * MLA_SCALE).astype(BF16)
                dv = dv + _nn(pr.astype(BF16), dom[j])
                dk = dk + _nn(ds, qcat[j])
                kcat_j = jnp.concatenate([jnp.where(masks[j], knv, 0), jnp.where(rmasks[j], krv, 0)], axis=1)
                dq = dq + _tn(ds, kcat_j.astype(BF16))
            dqn_ref[...] += dq[:, :LANES]
            dqr_ref[...] += dq[:, LANES:]
            dkn_ref[pl.ds(k0, TK), :] += dk[:, :LANES]
            dkr_ref[pl.ds(k0, TK), :] += dk[:, LANES:]
            dv_ref[pl.ds(k0, TK), :] += dv

        step(qi, True)

        def loop(i, c):
            step(qi - 1 - i, False)
            return c

        lax.fori_loop(0, qi, loop, 0)

    qspec = pl.BlockSpec((TQ, LANES), lambda b, h, i: (b * nq + i, h))
    sspec = pl.BlockSpec((S, LANES), lambda b, h, i: (b, h))
    return pl.pallas_call(
        body, name="mla_bwd", grid=(B, HEADS // 2, nq),
        in_specs=[qspec, pl.BlockSpec((TQ, LANES), lambda b, h, i: (b * nq + i, h // 2)), sspec,
                  pl.BlockSpec((S, LANES), lambda b, h, i: (b, 0)), sspec, qspec, qspec, qspec],
        out_specs=[qspec, qspec, sspec, sspec, sspec],
        out_shape=[jax.ShapeDtypeStruct((T, ATT_W), F32)] * 5,
        compiler_params=_params(("parallel", "parallel", "arbitrary")),
    )(qn, qr, kn, krt, vm, o, lse, do)


CONV_TC = 256


def _shift_down(x, n):
    return jnp.where(_iota2(x.shape, 0) >= n, pltpu.roll(x, n, 0), 0.0)


def _shift_up(x, n):
    rows = x.shape[0]
    return jnp.where(_iota2(x.shape, 0) < rows - n, pltpu.roll(x, rows - n, 0), 0.0)


def _taps(w_ref):
    return [w_ref[k:k + 1, :] for k in range(3)]


def _conv3(u, w, b):
    return w[0] * _shift_down(u, 2) + w[1] * _shift_down(u, 1) + w[2] * u + b


def conv_act_fwd_call(u, conv_w, conv_b, B, S):
    T = B * S
    nc = D_FF // CONV_TC

    def body(ug_ref, uv_ref, wg_ref, wv_ref, bg_ref, bv_ref, a_ref):
        gate = _conv3(ug_ref[...], _taps(wg_ref), bg_ref[...])
        val = _conv3(uv_ref[...], _taps(wv_ref), bv_ref[...])
        a_ref[...] =(gate * (1.0 / (1.0 + jnp.exp(-gate))) * val).astype(BF16)

    def blk(rows, off):
        return pl.BlockSpec((rows, CONV_TC), lambda b, j: (b if rows == S else 0, off + j))

    return pl.pallas_call(
        body, name="conv_act_fwd", grid=(B, nc),
        in_specs=[blk(S, 0), blk(S, nc), blk(3, 0), blk(3, nc), blk(1, 0), blk(1, nc)],
        out_specs=blk(S, 0),
        out_shape=jax.ShapeDtypeStruct((T, D_FF), BF16),
        compiler_params=_params(("parallel", "parallel")),
    )(u, u, conv_w, conv_w, conv_b, conv_b)


def conv_act_bwd_call(u, da, conv_w, conv_b, B, S):
    T = B * S
    nc = D_FF // CONV_TC

    def body(ug_ref, uv_ref, da_ref, wg_ref, wv_ref, bg_ref, bv_ref,
             dug_ref, duv_ref, dwg_ref, dwv_ref, dbg_ref, dbv_ref):
        @pl.when(pl.program_id(1) == 0)
        def _():
            for r in (dwg_ref, dwv_ref, dbg_ref, dbv_ref):
                r[...] = jnp.zeros_like(r)

        ug = ug_ref[...]
        uv = uv_ref[...]
        wg = _taps(wg_ref)
        wv = _taps(wv_ref)
        gate = _conv3(ug, wg, bg_ref[...])
        val = _conv3(uv, wv, bv_ref[...])
        da = da_ref[...]
        sig = 1.0 / (1.0 + jnp.exp(-gate))
        dval = da * (gate * sig)
        dgate = da * val * (sig * (1.0 + gate * (1.0 - sig)))
        for u_, d, w, du_ref, dw_ref, db_ref in ((ug, dgate, wg, dug_ref, dwg_ref, dbg_ref),
                                                 (uv, dval, wv, duv_ref, dwv_ref, dbv_ref)):
            du_ref[...] = (w[2] * d + w[1] * _shift_up(d, 1) + w[0] * _shift_up(d, 2)).astype(BF16)
            db_ref[...] += jnp.sum(d, axis=0, keepdims=True)
            dw_ref[0:1, :] += jnp.sum(d * _shift_down(u_, 2), axis=0, keepdims=True)
            dw_ref[1:2, :] += jnp.sum(d * _shift_down(u_, 1), axis=0, keepdims=True)
            dw_ref[2:3, :] += jnp.sum(d * u_, axis=0, keepdims=True)

    def blk(rows, off):
        return pl.BlockSpec((rows, CONV_TC), lambda j, b: (b if rows == S else 0, off + j))

    return pl.pallas_call(
        body, name="conv_act_bwd", grid=(nc, B),
        in_specs=[blk(S, 0), blk(S, nc), blk(S, 0), blk(3, 0), blk(3, nc), blk(1, 0), blk(1, nc)],
        out_specs=[blk(S, 0), blk(S, 0), blk(3, 0), blk(3, 0), blk(1, 0), blk(1, 0)],
        out_shape=[jax.ShapeDtypeStruct((T, D_FF), BF16), jax.ShapeDtypeStruct((T, D_FF), BF16),
                   jax.ShapeDtypeStruct((3, D_FF), F32), jax.ShapeDtypeStruct((3, D_FF), F32),
                   jax.ShapeDtypeStruct((1, D_FF), F32), jax.ShapeDtypeStruct((1, D_FF), F32)],
        compiler_params=_params(("parallel", "arbitrary")),
    )(u, u, da, conv_w, conv_w, conv_b, conv_b)


CHIP_MASKS = ((1, 0), (0, 1), (1, 1))


def _place():
    return lax.axis_index("x"), lax.axis_index("y"), lax.axis_index("c")


HALF_ALIGN = 32


def _any_specs(n):
    return [pl.BlockSpec(memory_space=pl.ANY)] * n


def _half_rows(r, half):
    return pl.ds(pl.multiple_of(half * (r // 2), HALF_ALIGN // 2), r // 2)


def _remote(src, dst, send_sem, recv_sem, device):
    return pltpu.make_async_remote_copy(src_ref=src, dst_ref=dst, send_sem=send_sem, recv_sem=recv_sem,
                                        device_id=device, device_id_type=MESH)


def gather_group_call(name, shards):
    n = len(shards)
    split = [s.shape[0] % HALF_ALIGN == 0 for s in shards]

    def body(*refs):
        ins, outs = refs[:n], refs[n:2 * n]
        ici_s, ici_r, d2d_s, d2d_r, local_sems = refs[2 * n:]
        x, y, c = _place()
        chip = 2 * x + y
        sib = (x, y, 1 - c)
        started = []
        local = []
        for w in range(n):
            cp = pltpu.make_async_copy(ins[w], outs[w].at[chip], local_sems.at[w])
            cp.start()
            local.append(cp)

        def rows(w, half):
            return _half_rows(shards[w].shape[0], half) if split[w] else slice(None)

        for w in range(n):
            for k, (fx, fy) in enumerate(CHIP_MASKS):
                cp = _remote(ins[w].at[rows(w, c)], outs[w].at[chip, rows(w, c)], ici_s.at[w, k], ici_r.at[w, k],
                             (x ^ fx, y ^ fy, c))
                cp.start()
                started.append(cp)
        for w in range(n):
            for k, (fx, fy) in enumerate(CHIP_MASKS):
                landed = outs[w].at[2 * (x ^ fx) + (y ^ fy), rows(w, c)]
                _remote(landed, landed, ici_s.at[w, k], ici_r.at[w, k], sib).wait_recv()
                if split[w]:
                    cp = _remote(landed, landed, d2d_s.at[w, k], d2d_r.at[w, k], sib)
                    cp.start()
                    started.append(cp)
        for w in range(n):
            for k, (fx, fy) in enumerate(CHIP_MASKS):
                if split[w]:
                    other = outs[w].at[2 * (x ^ fx) + (y ^ fy), rows(w, 1 - c)]
                    _remote(other, other, d2d_s.at[w, k], d2d_r.at[w, k], sib).wait_recv()
        for cp in started:
            cp.wait_send()
        for cp in local:
            cp.wait()

    sems = pltpu.SemaphoreType.DMA((n, 3))
    return pl.pallas_call(
        body, name=name, in_specs=_any_specs(n), out_specs=_any_specs(n),
        out_shape=[jax.ShapeDtypeStruct((N_CHIPS,) + s.shape, s.dtype) for s in shards],
        scratch_shapes=[sems, sems, sems, sems, pltpu.SemaphoreType.DMA((n,))],
        compiler_params=_params(),
    )(*shards)


def swap_half_call(name, parts):
    n = len(parts)

    def body(*refs):
        ins, outs = refs[:n], refs[n:2 * n]
        send_sems, recv_sems = refs[2 * n:]
        x, y, c = _place()
        copies = []
        for w in range(n):
            r = parts[w].shape[1]
            cp = _remote(ins[w].at[:, _half_rows(r, 1 - c)], outs[w], send_sems.at[w], recv_sems.at[w], (x, y, 1 - c))
            cp.start()
            copies.append(cp)
        for cp in copies:
            cp.wait_recv()
        for cp in copies:
            cp.wait_send()

    return pl.pallas_call(
        body, name=name, in_specs=_any_specs(n), out_specs=_any_specs(n),
        out_shape=[jax.ShapeDtypeStruct((N_CHIPS, p.shape[1] // 2, p.shape[2]), F32) for p in parts],
        scratch_shapes=[pltpu.SemaphoreType.DMA((n,)), pltpu.SemaphoreType.DMA((n,))],
        compiler_params=_params(),
    )(*parts)


def scatter_half_call(name, halves):
    n = len(halves)

    def body(*refs):
        ins, outs = refs[:n], refs[n:2 * n]
        send_sems, recv_sems = refs[2 * n:]
        x, y, c = _place()
        copies = []
        for w in range(n):
            for k, (fx, fy) in enumerate(CHIP_MASKS):
                cp = _remote(ins[w].at[2 * (x ^ fx) + (y ^ fy)], outs[w].at[k], send_sems.at[w, k], recv_sems.at[w, k],
                             (x ^ fx, y ^ fy, c))
                cp.start()
                copies.append(cp)
        for cp in copies:
            cp.wait_recv()
        for cp in copies:
            cp.wait_send()

    return pl.pallas_call(
        body, name=name, in_specs=_any_specs(n), out_specs=_any_specs(n),
        out_shape=[jax.ShapeDtypeStruct((3,) + h.shape[1:], F32) for h in halves],
        scratch_shapes=[pltpu.SemaphoreType.DMA((n, 3)), pltpu.SemaphoreType.DMA((n, 3))],
        compiler_params=_params(),
    )(*halves)


def swap_final_call(name, finals):
    n = len(finals)

    def body(*refs):
        ins, outs = refs[:n], refs[n:2 * n]
        send_sems, recv_sems, local_sems = refs[2 * n:]
        x, y, c = _place()
        copies = []
        local = []
        for w in range(n):
            r = 2 * finals[w].shape[0]
            cp = pltpu.make_async_copy(ins[w], outs[w].at[_half_rows(r, c)], local_sems.at[w])
            cp.start()
            local.append(cp)
            cp = _remote(ins[w], outs[w].at[_half_rows(r, c)], send_sems.at[w], recv_sems.at[w], (x, y, 1 - c))
            cp.start()
            copies.append(cp)
        for w in range(n):
            r = 2 * finals[w].shape[0]
            got = outs[w].at[_half_rows(r, 1 - c)]
            _remote(got, got, send_sems.at[w], recv_sems.at[w], (x, y, 1 - c)).wait_recv()
        for cp in copies:
            cp.wait_send()
        for cp in local:
            cp.wait()

    return pl.pallas_call(
        body, name=name, in_specs=_any_specs(n), out_specs=_any_specs(n),
        out_shape=[jax.ShapeDtypeStruct((2 * f.shape[0], f.shape[1]), F32) for f in finals],
        scratch_shapes=[pltpu.SemaphoreType.DMA((n,))] * 3,
        compiler_params=_params(),
    )(*finals)


def _row_tile(rows, cap):
    return max(t for t in range(8, min(rows, cap) + 1, 8) if rows % t == 0)


def add_half_call(name, part, got, core):
    _, rh, cols = got.shape
    tr = _row_tile(rh, 128)
    nb = rh // tr

    def body(core_ref, p_ref, g_ref, o_ref):
        o_ref[...] = p_ref[...] + g_ref[...]

    blk = (N_CHIPS, tr, cols)
    return pl.pallas_call(
        body, name=name,
        grid_spec=pltpu.PrefetchScalarGridSpec(
            num_scalar_prefetch=1, grid=(nb,),
            in_specs=[pl.BlockSpec(blk, lambda i, core_ref: (0, core_ref[0] * nb + i, 0)),
                      pl.BlockSpec(blk, lambda i, core_ref: (0, i, 0))],
            out_specs=pl.BlockSpec(blk, lambda i, core_ref: (0, i, 0))),
        out_shape=jax.ShapeDtypeStruct(got.shape, F32),
        compiler_params=_params(("parallel",)),
    )(core, part, got)


def sum_chips_call(name, half, got, chip):
    _, rh, cols = got.shape
    tr = _row_tile(rh, 128)

    def body(chip_ref, h_ref, g_ref, o_ref):
        o_ref[...] = ((h_ref[0] + g_ref[0]) + g_ref[1]) + g_ref[2]

    return pl.pallas_call(
        body, name=name,
        grid_spec=pltpu.PrefetchScalarGridSpec(
            num_scalar_prefetch=1, grid=(rh // tr,),
            in_specs=[pl.BlockSpec((1, tr, cols), lambda i, chip_ref: (chip_ref[0], i, 0)),
                      pl.BlockSpec((3, tr, cols), lambda i, chip_ref: (0, i, 0))],
            out_specs=pl.BlockSpec((tr, cols), lambda i, chip_ref: (i, 0))),
        out_shape=jax.ShapeDtypeStruct((rh, cols), F32),
        compiler_params=_params(("parallel",)),
    )(chip, half, got)


def _adamw(w, g, m, v):
    m = ADAM_B1 * m + (1.0 - ADAM_B1) * g
    v = ADAM_B2 * v + (1.0 - ADAM_B2) * (g * g)
    m_hat = m / (1.0 - ADAM_B1 ** ADAM_STEP)
    v_hat = v / (1.0 - ADAM_B2 ** ADAM_STEP)
    delta = -ADAM_LR * (m_hat / (jnp.sqrt(v_hat) + ADAM_EPS) + ADAM_WD * w)
    return delta, m, v


def adamw_call(name, g, w, m, v):
    r, cols = w.shape
    tr = r if r % 8 else _row_tile(r, 256)

    def body(g_ref, w_ref, m_ref, v_ref, d_ref, nm_ref, nv_ref):
        d_ref[...], nm_ref[...], nv_ref[...] = _adamw(w_ref[...], g_ref[...], m_ref[...], v_ref[...])

    spec = pl.BlockSpec((tr, cols), lambda i: (i, 0))
    return pl.pallas_call(
        body, name=name, grid=(r // tr,), in_specs=[spec] * 4, out_specs=[spec] * 3,
        out_shape=[jax.ShapeDtypeStruct((r, cols), F32)] * 3,
        compiler_params=_params(("parallel",)),
    )(g, w, m, v)


def allsum_small_call(v):
    R = v.shape[0]

    def body(v_ref, out_ref, buf, send_sems, recv_sems):
        x, y, c = _place()
        me = 4 * x + 2 * y + c
        buf[me] = v_ref[...]
        sends = []
        for k in range(1, N_DEV):
            fx, fy, fc = (k >> 2) & 1, (k >> 1) & 1, k & 1
            cp = pltpu.make_async_remote_copy(
                src_ref=v_ref, dst_ref=buf.at[me], send_sem=send_sems.at[k - 1], recv_sem=recv_sems.at[k - 1],
                device_id=(x ^ fx, y ^ fy, c ^ fc), device_id_type=MESH)
            cp.start()
            sends.append(cp)
        for k in range(1, N_DEV):
            pltpu.make_async_remote_copy(
                src_ref=v_ref, dst_ref=buf.at[me ^ k], send_sem=send_sems.at[k - 1], recv_sem=recv_sems.at[k - 1],
                device_id=(x, y, c), device_id_type=MESH).wait_recv()
        acc = buf[0]
        for d in range(1, N_DEV):
            acc = acc + buf[d]
        out_ref[...] = acc
        for cp in sends:
            cp.wait_send()

    vm = pl.BlockSpec(memory_space=pltpu.VMEM)
    return pl.pallas_call(
        body, name="allsum_small", in_specs=[vm], out_specs=vm,
        out_shape=jax.ShapeDtypeStruct((R, LANES), F32),
        scratch_shapes=[pltpu.VMEM((N_DEV, R, LANES), F32), pltpu.SemaphoreType.DMA((N_DEV - 1,)),
                        pltpu.SemaphoreType.DMA((N_DEV - 1,))],
        compiler_params=_params(),
    )(v)


def _slab(flat, mult):
    n = flat.shape[-1]
    rows = -(-n // (LANES * mult)) * mult
    flat = jnp.pad(flat, [(0, 0)] * (flat.ndim - 1) + [(0, rows * LANES - n)])
    return flat.reshape(flat.shape[:-1] + (rows, LANES))


def full_from_chips(blocks, by_col):
    _, r, c = blocks.shape
    return blocks.transpose(1, 0, 2).reshape(r, N_CHIPS * c) if by_col else blocks.reshape(N_CHIPS * r, c)


def chips_from_full(full, by_col):
    if by_col:
        r, c = full.shape[0], full.shape[1] // N_CHIPS
        return full.reshape(r, N_CHIPS, c).transpose(1, 0, 2)
    return full.reshape(N_CHIPS, full.shape[0] // N_CHIPS, full.shape[1])


SMALL_PACK = SMALL_W + ("loss", "conv_w")
SMALL_PACK_N = {**SMALL_N, "loss": 1, "conv_w": 3 * 2 * D_FF}


def pack_small(vals):
    zero = jnp.zeros((1,), F32)
    return _slab(jnp.concatenate([vals[n].reshape(-1) if n in vals else jnp.tile(zero, SMALL_PACK_N[n])
                                  for n in SMALL_PACK]), 8)


def unpack_small(slab, shapes):
    flat = slab.reshape(-1)
    out, off = {}, 0
    for n in SMALL_PACK:
        out[n] = flat[off:off + SMALL_PACK_N[n]].reshape(shapes[n])
        off += SMALL_PACK_N[n]
    return out


def _split_heads(w, a, b):
    r = w.shape[0]
    w3 = w.reshape(r, HEADS, a + b)
    return w3[:, :, :a].reshape(r, HEADS * a), w3[:, :, a:].reshape(r, HEADS * b)


def _merge_heads(wa, wb, a, b):
    r = wa.shape[0]
    return jnp.concatenate([wa.reshape(r, HEADS, a), wb.reshape(r, HEADS, b)], axis=2).reshape(r, HEADS * (a + b))


def kernel(x, positions, g_mix, w_in, g_cq, w_uq, g_ckv, w_ukv, g_sb_out, g_mla_out, w_out, g_ffn, w_up, conv_w, conv_b, w_down, g_final, loss_target, m_g_mix, m_w_in, m_g_cq, m_w_uq, m_g_ckv, m_w_ukv, m_g_sb_out, m_g_mla_out, m_w_out, m_g_ffn, m_w_up, m_conv_w, m_conv_b, m_w_down, m_g_final, v_g_mix, v_w_in, v_g_cq, v_w_uq, v_g_ckv, v_w_ukv, v_g_sb_out, v_g_mla_out, v_w_out, v_g_ffn, v_w_up, v_conv_w, v_conv_b, v_w_down, v_g_final):
    given = dict(locals())
    B, S, _ = x.shape
    T = B * S
    w_big = {n: given[n][0] for n in BIG_W}
    m_big = {n: given["m_" + n][0] for n in BIG_W}
    v_big = {n: given["v_" + n][0] for n in BIG_W}

    first = ("w_in", "w_uq", "w_ukv")
    later = ("w_out", "w_up", "w_down", "conv_w")
    got_w = dict(zip(first, gather_group_call("gather_first", [w_big[n].astype(BF16) for n in first])))
    got_w.update(zip(later, gather_group_call(
        "gather_later", [w_big[n] if n == "conv_w" else w_big[n].astype(BF16) for n in later])))
    full = {n: full_from_chips(got_w[n], BIG_SHARD[n][2]) for n in BIG_W}
    conv_w_full = full["conv_w"]
    w_in_p = jnp.pad(full["w_in"], ((0, 0), (0, IN_COLS_PAD - IN_COLS)))
    w_uq_p = jnp.concatenate(_split_heads(full["w_uq"], HEAD_DIM, ROPE_DIM), axis=1)
    w_ukv_p = jnp.concatenate(_split_heads(full["w_ukv"], HEAD_DIM, HEAD_DIM), axis=1)

    x2d = x.reshape(T, D_MODEL)
    half = ROPE_DIM // 2
    inv_freq = 1.0 / (ROPE_BASE ** (jnp.arange(half, dtype=F32) * (2.0 / ROPE_DIM)))
    cos, sin = rope_tab_call(positions.reshape(T, 1), jnp.tile(inv_freq, LANES // half).reshape(1, LANES))
    h = rmsnorm_fwd_call("norm_mix", x2d, g_mix)
    p = matmul_call("proj_in", h, w_in_p, "nn", tn=IN_COLS_PAD // 2)
    qn, qr, kn, vm, krt, cqn, ckvn = mla_prep_fwd_call(p, cos, sin, g_cq, g_ckv, w_uq_p, w_ukv_p)
    o_sb, lt_sb = sb_fwd_call(p, B, S)
    o_mla, lse = mla_fwd_call(qn, qr, kn, krt, vm, B, S)
    o_cat = outnorm_fwd_call(o_sb, o_mla, g_sb_out, g_mla_out)
    x1 = matmul_call("proj_out", o_cat, full["w_out"], "nn", res=x2d)
    hn = rmsnorm_fwd_call("norm_ffn", x1, g_ffn)
    u = matmul_call("ffn_up", hn, full["w_up"], "nn", tn=2 * D_FF // 4)
    act = conv_act_fwd_call(u, conv_w_full, conv_b, B, S)
    x2 = matmul_call("ffn_down", act, full["w_down"], "nn", res=x1)
    dx2, loss_row, dg_final = final_loss_call(x2, g_final.reshape(1, D_MODEL), loss_target.reshape(T, D_MODEL))

    gw = {}
    gw["w_down"] = wgrad_call("wgrad_down", act, dx2, tn=512)
    da = matmul_call("ffn_down_bwd", dx2, full["w_down"], "nt", tn=D_FF // 2)
    du_g, du_v, dcw_g, dcw_v, dcb_g, dcb_v = conv_act_bwd_call(u, da, conv_w_full, conv_b, B, S)
    du = jnp.concatenate([du_g, du_v], axis=1)
    gw["w_up"] = wgrad_call("wgrad_up", hn, du, tn=2 * D_FF // 4, by_chip=True)
    dhn = matmul_call("ffn_up_bwd", du, full["w_up"], "nt", tn=512)
    dx1, dg_ffn = rmsnorm_bwd_call("norm_ffn_bwd", x1, g_ffn, dhn, dx2)
    gw["w_out"] = wgrad_call("wgrad_out", o_cat, dx1)
    do_cat = matmul_call("proj_out_bwd", dx1, full["w_out"], "nt")
    do_sb, do_mla, dg_sb_out, dg_mla_out = outnorm_bwd_call(o_sb, o_mla, g_sb_out, g_mla_out, do_cat)
    dq_sb, dk_sb, dv_sb = sb_bwd_call(p, lt_sb, do_sb, B, S)
    dqn, dqr4, dkn, dvm, dkrt4 = mla_bwd_call(qn, qr, kn, krt, vm, o_mla, lse, do_mla, B, S)
    dcq, dckvr, dq_cat, dkv_cat, dg_cq, dg_ckv = mla_prep_bwd_call(
        p, cos, sin, g_cq, g_ckv, w_uq_p, w_ukv_p, dqn, dqr4, dkn, dvm, dkrt4)
    dw_uq_p = wgrad_call("wgrad_uq", cqn, dq_cat)
    dw_ukv_p = wgrad_call("wgrad_ukv", ckvn, dkv_cat)
    gw["w_uq"] = _merge_heads(dw_uq_p[:, :ATT_W], dw_uq_p[:, ATT_W:], HEAD_DIM, ROPE_DIM)
    gw["w_ukv"] = _merge_heads(dw_ukv_p[:, :ATT_W], dw_ukv_p[:, ATT_W:], HEAD_DIM, HEAD_DIM)
    dp = jnp.concatenate([dq_sb, dk_sb, dv_sb, dcq, dckvr], axis=1)
    gw["w_in"] = wgrad_call("wgrad_in", h, dp, tn=IN_COLS_PAD // 2)[:, :IN_COLS]
    dh = matmul_call("proj_in_bwd", dp, w_in_p, "nt")
    grad_x, dg_mix = rmsnorm_bwd_call("norm_mix_bwd", x2d, g_mix, dh, dx1)

    xi, yi, ci = _place()
    chip = (2 * xi + yi).astype(jnp.int32).reshape(1)
    core = ci.astype(jnp.int32).reshape(1)
    moved = ("w_down", "w_up", "w_out", "w_uq", "w_ukv", "w_in")
    parts = [gw[n] if n == "w_up" else chips_from_full(gw[n], BIG_SHARD[n][2]) for n in moved]
    sib_rows = swap_half_call("swap_half", parts)
    halves = [add_half_call("add_half_" + n, p_, s_, core) for n, p_, s_ in zip(moved, parts, sib_rows)]
    from_chips = scatter_half_call("scatter_half", halves)
    finals = [sum_chips_call("sum_chips_" + n, h_, f_, chip) for n, h_, f_ in zip(moved, halves, from_chips)]
    grads = dict(zip(moved, swap_final_call("swap_final", finals)))

    shapes = {n: given[n].shape for n in SMALL_W}
    shapes.update(loss=(), conv_w=(3, 2 * D_FF))
    small_g = {"g_mix": dg_mix, "g_cq": dg_cq, "g_ckv": dg_ckv, "g_sb_out": dg_sb_out, "g_mla_out": dg_mla_out,
               "g_ffn": dg_ffn, "conv_b": jnp.concatenate([dcb_g, dcb_v], axis=1), "g_final": dg_final,
               "loss": loss_row[0, :1], "conv_w": jnp.concatenate([dcw_g, dcw_v], axis=1)}
    gs_slab = allsum_small_call(pack_small(small_g))
    small_in = [pack_small({n: given[pre + n] for n in SMALL_W}) for pre in ("", "m_", "v_")]
    small_out = [unpack_small(s, shapes) for s in (gs_slab,) + tuple(adamw_call("adamw_small", gs_slab, *small_in))]
    cw_cols = BIG_SHARD["conv_w"][1]
    grads["conv_w"] = lax.dynamic_slice_in_dim(small_out[0]["conv_w"], chip[0] * cw_cols, cw_cols, axis=1)

    big_out = {n: (grads[n],) + tuple(adamw_call("adamw_" + n, grads[n], w_big[n], m_big[n], v_big[n])) for n in BIG_W}
    weights = ("g_mix", "w_in", "g_cq", "w_uq", "g_ckv", "w_ukv", "g_sb_out", "g_mla_out", "w_out", "g_ffn",
               "w_up", "conv_w", "conv_b", "w_down", "g_final")
    outs = [small_out[0]["loss"], grad_x.reshape(B, S, D_MODEL)]
    for k in range(4):
        for n in weights:
            outs.append(big_out[n][k][None] if n in BIG_W else small_out[k][n])
    return tuple(outs)
```

```python
import functools

import jax
import jax.numpy as jnp
from jax import lax
from jax.experimental import pallas as pl
from jax.experimental.pallas import tpu as pltpu

F32 = jnp.float32
BF16 = jnp.bfloat16
MESH = pl.DeviceIdType.MESH

D_MODEL = 1024
HEADS = 8
HEAD_DIM = 64
ATT_W = HEADS * HEAD_DIM
ROPE_DIM = 32
ROPE_W = HEADS * ROPE_DIM
QK_DIM = HEAD_DIM + ROPE_DIM
Q_RANK = 384
KV_RANK = 256
D_FF = 2816
IN_COLS = 2208
IN_COLS_PAD = 2304
EPS = 1e-6
ROPE_BASE = 10000.0
SB_SCALE = HEAD_DIM ** -0.5
MLA_SCALE = QK_DIM ** -0.5
LANES = 128
N_CHIPS = 4
N_DEV = 8
VMEM_LIMIT = 48 * 1024 * 1024
ATT_TQ = 256
ATT_TK = 256
NEG_BIG = -1e30

ADAM_LR = 0.001
ADAM_B1 = 0.9
ADAM_B2 = 0.999
ADAM_EPS = 1e-08
ADAM_WD = 0.01
ADAM_STEP = 10

BIG_W = ("w_in", "w_uq", "w_ukv", "w_out", "w_up", "conv_w", "w_down")
BIG_SHARD = {
    "w_in": (D_MODEL, IN_COLS // 4, True),
    "w_uq": (Q_RANK, HEADS * QK_DIM // 4, True),
    "w_ukv": (KV_RANK, 2 * ATT_W // 4, True),
    "w_out": (2 * ATT_W // 4, D_MODEL, False),
    "w_up": (D_MODEL, 2 * D_FF // 4, True),
    "conv_w": (3, 2 * D_FF // 4, True),
    "w_down": (D_FF // 4, D_MODEL, False),
}
SMALL_W = ("g_mix", "g_cq", "g_ckv", "g_sb_out", "g_mla_out", "g_ffn", "conv_b", "g_final")
SMALL_N = {"g_mix": D_MODEL, "g_cq": Q_RANK, "g_ckv": KV_RANK, "g_sb_out": ATT_W, "g_mla_out": ATT_W,
           "g_ffn": D_MODEL, "conv_b": 2 * D_FF, "g_final": D_MODEL}


def _params(sem=None, **kw):
    return pltpu.CompilerParams(dimension_semantics=sem, vmem_limit_bytes=VMEM_LIMIT, **kw)


def _dot(a, b, dims):
    return lax.dot_general(a, b, (dims, ((), ())), preferred_element_type=F32)


def _nn(a, b):
    return _dot(a, b, ((1,), (0,)))


def _nt(a, b):
    return _dot(a, b, ((1,), (1,)))


def _tn(a, b):
    return _dot(a, b, ((0,), (0,)))


def _split2(x):
    hi = x.astype(BF16)
    lo = (x - hi.astype(F32)).astype(BF16)
    return hi, lo


def _split3(x):
    hi = x.astype(BF16)
    r1 = x - hi.astype(F32)
    mid = r1.astype(BF16)
    return hi, mid, (r1 - mid.astype(F32)).astype(BF16)


def _rms_r(x, d):
    return lax.rsqrt(jnp.sum(x * x, axis=-1, keepdims=True) * (1.0 / d) + EPS)


def _rms_bwd(x, g, dy, d):
    r = _rms_r(x, d)
    xhat = x * r
    gy = dy * g
    dx = r * (gy - xhat * (jnp.sum(xhat * gy, axis=-1, keepdims=True) * (1.0 / d)))
    return dx, jnp.sum(dy * xhat, axis=0, keepdims=True)


def _rot(x):
    lane = lax.broadcasted_iota(jnp.int32, x.shape, x.ndim - 1)
    n = x.shape[-1]
    return jnp.where((lane & 31) < 16, pltpu.roll(x, n - 16, x.ndim - 1), pltpu.roll(x, 16, x.ndim - 1))


def _fold4(x):
    return x + pltpu.roll(x, 32, 1) + pltpu.roll(x, 64, 1) + pltpu.roll(x, 96, 1)


def matmul_call(name, a, b, mode, out_dtype=F32, res=None, tm=512, tn=None, ex=None):
    M, K = a.shape
    N = b.shape[1] if mode == "nn" else b.shape[0]
    tn = N if tn is None else tn
    assert M % tm == 0 and N % tn == 0

    def body(*refs):
        if res is None:
            a_ref, b_ref, o_ref = refs
        else:
            a_ref, b_ref, r_ref, o_ref = refs
        av = a_ref[...].astype(BF16)
        bv = b_ref[...].astype(BF16)
        acc = _nn(av, bv) if mode == "nn" else _nt(av, bv)
        if res is not None:
            acc = r_ref[...] + acc
        o_ref[...] = acc.astype(out_dtype)

    in_specs = [pl.BlockSpec((tm, K), lambda j, i: (i, 0))]
    if mode == "nn":
        in_specs.append(pl.BlockSpec((K, tn), lambda j, i: (0, j)))
    else:
        in_specs.append(pl.BlockSpec((tn, K), lambda j, i: (j, 0)))
    args = [a, b]
    if res is not None:
        in_specs.append(pl.BlockSpec((tm, tn), lambda j, i: (i, j)))
        args.append(res)
    outs, moved = _call(body, ex, name=name, grid=(N // tn, M // tm), in_specs=in_specs,
                        out_specs=[pl.BlockSpec((tm, tn), lambda j, i: (i, j))],
                        out_shape=[jax.ShapeDtypeStruct((M, N), out_dtype)], args=args)
    return outs[0] if ex is None else (outs[0], moved)


def wgrad_call(name, a, b, tn=None, tt=512, by_chip=False):
    T, M = a.shape
    N = b.shape[1]
    tn = N if tn is None else tn
    assert T % tt == 0 and N % tn == 0
    if by_chip:
        out_spec = pl.BlockSpec((None, M, tn), lambda j, t: (j, 0, 0))
        out_shape = jax.ShapeDtypeStruct((N // tn, M, tn), F32)
    else:
        out_spec = pl.BlockSpec((M, tn), lambda j, t: (0, j))
        out_shape = jax.ShapeDtypeStruct((M, N), F32)

    def body(a_ref, b_ref, o_ref):
        @pl.when(pl.program_id(1) == 0)
        def _():
            o_ref[...] = jnp.zeros_like(o_ref)

        o_ref[...] += _tn(a_ref[...].astype(BF16), b_ref[...].astype(BF16))

    return pl.pallas_call(
        body, name=name, grid=(N // tn, T // tt),
        in_specs=[pl.BlockSpec((tt, M), lambda j, t: (t, 0)), pl.BlockSpec((tt, tn), lambda j, t: (t, j))],
        out_specs=out_spec, out_shape=out_shape,
        compiler_params=_params(("parallel", "arbitrary")),
    )(a, b)


def rmsnorm_fwd_call(name, x, g, tm=512):
    T, d = x.shape

    def body(x_ref, g_ref, o_ref):
        x = x_ref[...]
        o_ref[...] = ((x * _rms_r(x, d)) * g_ref[...]).astype(BF16)

    return pl.pallas_call(
        body, name=name, grid=(T // tm,),
        in_specs=[pl.BlockSpec((tm, d), lambda i: (i, 0)), pl.BlockSpec((1, d), lambda i: (0, 0))],
        out_specs=pl.BlockSpec((tm, d), lambda i: (i, 0)),
        out_shape=jax.ShapeDtypeStruct((T, d), BF16),
        compiler_params=_params(("parallel",)),
    )(x, g)


def rmsnorm_bwd_call(name, x, g, dy, res, tm=512, ex=None):
    T, d = x.shape

    def body(x_ref, g_ref, dy_ref, r_ref, dx_ref, dg_ref):
        @pl.when(pl.program_id(0) == 0)
        def _():
            dg_ref[...] = jnp.zeros_like(dg_ref)

        dx, dg = _rms_bwd(x_ref[...], g_ref[...], dy_ref[...], d)
        dx_ref[...] = r_ref[...] + dx
        dg_ref[...] += dg

    row = pl.BlockSpec((tm, d), lambda i: (i, 0))
    vec = pl.BlockSpec((1, d), lambda i: (0, 0))
    outs, moved = _call(body, ex, name=name, grid=(T // tm,), in_specs=[row, vec, row, row], out_specs=[row, vec],
                        out_shape=[jax.ShapeDtypeStruct((T, d), F32), jax.ShapeDtypeStruct((1, d), F32)],
                        args=(x, g, dy, res))
    return tuple(outs) if ex is None else tuple(outs) + (moved,)


def outnorm_fwd_call(o_sb, o_mla, g_sb, g_mla, tm=512):
    T = o_sb.shape[0]

    def body(a_ref, b_ref, ga_ref, gb_ref, o_ref):
        a = a_ref[...]
        b = b_ref[...]
        ya = (a * _rms_r(a, ATT_W)) * ga_ref[...]
        yb = (b * _rms_r(b, ATT_W)) * gb_ref[...]
        o_ref[...] = jnp.concatenate([ya, yb], axis=1).astype(BF16)

    row = pl.BlockSpec((tm, ATT_W), lambda i: (i, 0))
    vec = pl.BlockSpec((1, ATT_W), lambda i: (0, 0))
    return pl.pallas_call(
        body, name="outnorm_fwd", grid=(T // tm,), in_specs=[row, row, vec, vec],
        out_specs=pl.BlockSpec((tm, 2 * ATT_W), lambda i: (i, 0)),
        out_shape=jax.ShapeDtypeStruct((T, 2 * ATT_W), BF16),
        compiler_params=_params(("parallel",)),
    )(o_sb, o_mla, g_sb, g_mla)


def outnorm_bwd_call(o_sb, o_mla, g_sb, g_mla, do_cat, tm=512, ex=None):
    T = o_sb.shape[0]

    def body(a_ref, b_ref, ga_ref, gb_ref, d_ref, da_ref, db_ref, dga_ref, dgb_ref):
        @pl.when(pl.program_id(0) == 0)
        def _():
            dga_ref[...] = jnp.zeros_like(dga_ref)
            dgb_ref[...] = jnp.zeros_like(dgb_ref)

        d = d_ref[...]
        da, dga = _rms_bwd(a_ref[...], ga_ref[...], d[:, :ATT_W], ATT_W)
        db, dgb = _rms_bwd(b_ref[...], gb_ref[...], d[:, ATT_W:], ATT_W)
        da_ref[...] = da
        db_ref[...] = db
        dga_ref[...] += dga
        dgb_ref[...] += dgb

    row = pl.BlockSpec((tm, ATT_W), lambda i: (i, 0))
    vec = pl.BlockSpec((1, ATT_W), lambda i: (0, 0))
    outs, moved = _call(
        body, ex, name="outnorm_bwd", grid=(T // tm,),
        in_specs=[row, row, vec, vec, pl.BlockSpec((tm, 2 * ATT_W), lambda i: (i, 0))],
        out_specs=[row, row, vec, vec],
        out_shape=[jax.ShapeDtypeStruct((T, ATT_W), F32), jax.ShapeDtypeStruct((T, ATT_W), F32),
                   jax.ShapeDtypeStruct((1, ATT_W), F32), jax.ShapeDtypeStruct((1, ATT_W), F32)],
        args=(o_sb, o_mla, g_sb, g_mla, do_cat))
    return tuple(outs) if ex is None else tuple(outs) + (moved,)


def final_loss_call(x2, g, target, tm=512):
    T, d = x2.shape

    def body(x_ref, g_ref, t_ref, dx_ref, loss_ref, dg_ref):
        @pl.when(pl.program_id(0) == 0)
        def _():
            loss_ref[...] = jnp.zeros_like(loss_ref)
            dg_ref[...] = jnp.zeros_like(dg_ref)

        x = x_ref[...]
        g = g_ref[...]
        y = (x * _rms_r(x, d)) * g
        err = y - t_ref[...]
        loss_ref[...] += jnp.sum(jnp.sum(err * err, axis=1, keepdims=True), axis=0, keepdims=True) * (0.5 / d)
        dx, dg = _rms_bwd(x, g, err * (1.0 / d), d)
        dx_ref[...] = dx
        dg_ref[...] += dg

    row = pl.BlockSpec((tm, d), lambda i: (i, 0))
    vec = pl.BlockSpec((1, d), lambda i: (0, 0))
    return pl.pallas_call(
        body, name="final_loss", grid=(T // tm,), in_specs=[row, vec, row],
        out_specs=[row, pl.BlockSpec((1, LANES), lambda i: (0, 0)), vec],
        out_shape=[jax.ShapeDtypeStruct((T, d), F32), jax.ShapeDtypeStruct((1, LANES), F32),
                   jax.ShapeDtypeStruct((1, d), F32)],
        compiler_params=_params(("arbitrary",)),
    )(x2, g, target)


def rope_tab_call(pos, inv_freq, tm=512):
    T = pos.shape[0]

    def body(p_ref, f_ref, c_ref, s_ref):
        ang = p_ref[...].astype(F32) * f_ref[...]
        lane = lax.broadcasted_iota(jnp.int32, ang.shape, 1)
        sn = jnp.sin(ang)
        c_ref[...] = jnp.cos(ang)
        s_ref[...] = jnp.where((lane & 31) < 16, -sn, sn)

    row = pl.BlockSpec((tm, LANES), lambda i: (i, 0))
    return pl.pallas_call(
        body, name="rope_tab", grid=(T // tm,),
        in_specs=[pl.BlockSpec((tm, 1), lambda i: (i, 0)), pl.BlockSpec((1, LANES), lambda i: (0, 0))],
        out_specs=[row, row],
        out_shape=[jax.ShapeDtypeStruct((T, LANES), F32)] * 2,
        compiler_params=_params(("parallel",)),
    )(pos, inv_freq)


def mla_prep_fwd_call(p, cos, sin, g_cq, g_ckv, w_uq_p, w_ukv_p, tm=512):
    T = p.shape[0]

    def body(cq_ref, ckvr_ref, c_ref, s_ref, gq_ref, gkv_ref, wq_ref, wkv_ref,
             qn_ref, qr_ref, kn_ref, vm_ref, krt_ref, cqn_ref, ckvn_ref):
        c = c_ref[...]
        s = s_ref[...]
        cq = cq_ref[...]
        cqn = ((cq * _rms_r(cq, Q_RANK)) * gq_ref[...]).astype(BF16)
        cqn_ref[...] = cqn
        q = _nn(cqn, wq_ref[...])
        qn_ref[...] = q[:, :ATT_W].astype(BF16)
        for g in range(ROPE_W // LANES):
            qr = q[:, ATT_W + g * LANES:ATT_W + (g + 1) * LANES]
            qr_ref[:, g * LANES:(g + 1) * LANES] = (qr * c + _rot(qr) * s).astype(BF16)
        ckvr = ckvr_ref[...]
        ckv = ckvr[:, :KV_RANK]
        ckvn = ((ckv * _rms_r(ckv, KV_RANK)) * gkv_ref[...]).astype(BF16)
        ckvn_ref[...] = ckvn
        kv = _nn(ckvn, wkv_ref[...])
        kn_ref[...] = kv[:, :ATT_W].astype(BF16)
        vm_ref[...] = kv[:, ATT_W:].astype(BF16)
        kr = _fold4(ckvr[:, KV_RANK:])
        krt_ref[...] = (kr * c + _rot(kr) * s).astype(BF16)

    def row(w, j=0):
        return pl.BlockSpec((tm, w), lambda i: (i, j))

    def full(a):
        return pl.BlockSpec(a.shape, lambda i: (0, 0))

    return pl.pallas_call(
        body, name="mla_prep_fwd", grid=(T // tm,),
        in_specs=[row(Q_RANK, 4), row(Q_RANK, 5), row(LANES), row(LANES), full(g_cq), full(g_ckv),
                  full(w_uq_p), full(w_ukv_p)],
        out_specs=[row(ATT_W), row(ROPE_W), row(ATT_W), row(ATT_W), row(LANES), row(Q_RANK), row(KV_RANK)],
        out_shape=[jax.ShapeDtypeStruct((T, w), BF16) for w in (ATT_W, ROPE_W, ATT_W, ATT_W, LANES, Q_RANK, KV_RANK)],
        compiler_params=_params(("parallel",)),
    )(p, p, cos, sin, g_cq, g_ckv, w_uq_p, w_ukv_p)


def mla_prep_bwd_call(p, cos, sin, g_cq, g_ckv, w_uq_p, w_ukv_p, dqn, dqr4, dkn, dvm, dkrt4, tm=512):
    T = p.shape[0]

    def body(cq_ref, ckvr_ref, c_ref, s_ref, gq_ref, gkv_ref, wq_ref, wkv_ref,
             dqn_ref, dqr4_ref, dkn_ref, dvm_ref, dkrt4_ref,
             dcq_ref, dckvr_ref, dq_ref, dkv_ref, dgq_ref, dgkv_ref):
        @pl.when(pl.program_id(0) == 0)
        def _():
            dgq_ref[...] = jnp.zeros_like(dgq_ref)
            dgkv_ref[...] = jnp.zeros_like(dgkv_ref)

        c = c_ref[...]
        s = s_ref[...]
        d4 = dqr4_ref[...]
        dqr = [d4[:, :128] + d4[:, 128:256], d4[:, 256:384] + d4[:, 384:]]
        dqr = [t * c + _rot(t * s) for t in dqr]
        dq = jnp.concatenate([dqn_ref[...]] + dqr, axis=1).astype(BF16)
        dq_ref[...] = dq
        dcq, dgq = _rms_bwd(cq_ref[...], gq_ref[...], _nt(dq, wq_ref[...]), Q_RANK)
        dcq_ref[...] = dcq
        dgq_ref[...] += dgq
        dkv = jnp.concatenate([dkn_ref[...], dvm_ref[...]], axis=1).astype(BF16)
        dkv_ref[...] = dkv
        ckvr = ckvr_ref[...]
        dckv, dgkv = _rms_bwd(ckvr[:, :KV_RANK], gkv_ref[...], _nt(dkv, wkv_ref[...]), KV_RANK)
        dgkv_ref[...] += dgkv
        k4 = dkrt4_ref[...]
        dkr = _fold4(k4[:, :128] + k4[:, 128:256] + k4[:, 256:384] + k4[:, 384:])
        dkr = dkr * c + _rot(dkr * s)
        lane = lax.broadcasted_iota(jnp.int32, dkr.shape, 1)
        dckvr_ref[...] = jnp.concatenate([dckv, jnp.where(lane < ROPE_DIM, dkr, 0.0)], axis=1)

    def row(w, j=0):
        return pl.BlockSpec((tm, w), lambda i: (i, j))

    def full(a):
        return pl.BlockSpec(a.shape, lambda i: (0, 0))

    return pl.pallas_call(
        body, name="mla_prep_bwd", grid=(T // tm,),
        in_specs=[row(Q_RANK, 4), row(Q_RANK, 5), row(LANES), row(LANES), full(g_cq), full(g_ckv),
                  full(w_uq_p), full(w_ukv_p), row(ATT_W), row(ATT_W), row(ATT_W), row(ATT_W), row(ATT_W)],
        out_specs=[row(Q_RANK), row(Q_RANK), row(ATT_W + ROPE_W), row(2 * ATT_W),
                   pl.BlockSpec((1, Q_RANK), lambda i: (0, 0)), pl.BlockSpec((1, KV_RANK), lambda i: (0, 0))],
        out_shape=[jax.ShapeDtypeStruct((T, Q_RANK), F32), jax.ShapeDtypeStruct((T, Q_RANK), F32),
                   jax.ShapeDtypeStruct((T, ATT_W + ROPE_W), BF16), jax.ShapeDtypeStruct((T, 2 * ATT_W), BF16),
                   jax.ShapeDtypeStruct((1, Q_RANK), F32), jax.ShapeDtypeStruct((1, KV_RANK), F32)],
        compiler_params=_params(("arbitrary",)),
    )(p, p, cos, sin, g_cq, g_ckv, w_uq_p, w_ukv_p, dqn, dqr4, dkn, dvm, dkrt4)


def _iota2(shape, axis):
    return lax.broadcasted_iota(jnp.int32, shape, axis)


def _head_masks():
    lane = _iota2((1, LANES), 1)
    return lane < HEAD_DIM, lane >= HEAD_DIM


def _pair(x, masks, dtype=BF16):
    return [jnp.where(m, x, 0.0).astype(dtype) for m in masks]


def _log_gates(z):
    soft = jnp.log1p(jnp.exp(-jnp.abs(z)))
    return jnp.minimum(z, 0.0) - soft, -jnp.maximum(z, 0.0) - soft


def _lane_selector(group):
    return jnp.where(_iota2((16, LANES), 1) // group == _iota2((16, LANES), 0), 1.0, 0.0).astype(BF16)


def _rows8(sel_t, x):
    hi = x.astype(BF16)
    r1 = x - hi.astype(F32)
    mid = r1.astype(BF16)
    lo = (r1 - mid.astype(F32)).astype(BF16)
    return _nt(sel_t, hi) + _nt(sel_t, mid) + _nt(sel_t, lo)


def _row_of(x8, j):
    return jnp.sum(jnp.where(_iota2(x8.shape, 0) == j, x8, 0.0), axis=0, keepdims=True)


def sb_fwd_call(p, B, S, ex=None):
    T = B * S
    TQ, TK = ATT_TQ, ATT_TK
    nq = S // TQ

    def body(q_ref, k_ref, v_ref, o_ref, lt_ref):
        qi = pl.program_id(2)
        masks = _head_masks()
        qm = _pair(q_ref[...] * SB_SCALE, masks)
        row = _iota2((TQ, TK), 0)
        col = _iota2((TQ, TK), 1)
        tri = jnp.where(row > col, 1.0, 0.0).astype(BF16)
        tri2 = jnp.concatenate([tri, tri], axis=0)
        vis = col < row
        o_ref[...] = jnp.zeros_like(o_ref)

        def step(kb, carry, diag):
            k0 = pl.multiple_of(kb * TK, TK)
            k = k_ref[pl.ds(k0, TK), :].astype(BF16)
            vm = _pair(v_ref[pl.ds(k0, TK), :], masks)
            acc = jnp.zeros((TQ, LANES), F32)
            out = []
            for j in range(2):
                lb, lk = _log_gates(_nt(qm[j], k))
                if diag:
                    lk = jnp.where(vis, lk, 0.0)
                hi, lo = _split2(lk)
                tail = _nn(jnp.concatenate([hi, lo], axis=1), tri2) + carry[j]
                a = jnp.exp(lb + tail)
                if diag:
                    a = jnp.where(vis, a, 0.0)
                acc = acc + _nn(a.astype(BF16), vm[j])
                out.append(carry[j] + jnp.sum(lk, axis=1, keepdims=True))
            o_ref[...] += acc
            return tuple(out)

        zero = jnp.zeros((TQ, 1), F32)
        carry = step(qi, (zero, zero), True)
        c0, c1 = lax.fori_loop(0, qi, lambda i, c: step(qi - 1 - i, c, False), carry)
        lane = _iota2((TQ, LANES), 1)
        lt_ref[...] = jnp.where(lane == 0, c0, jnp.where(lane == 1, c1, 0.0))

    qspec = pl.BlockSpec((TQ, LANES), lambda b, h, i: (b * nq + i, h))
    outs, moved = _call(
        body, ex, name="sb_fwd", grid=(B, HEADS // 2, nq),
        in_specs=[qspec,
                  pl.BlockSpec((S, LANES), lambda b, h, i: (b, 4 + h)),
                  pl.BlockSpec((S, LANES), lambda b, h, i: (b, 8 + h))],
        out_specs=[qspec, qspec],
        out_shape=[jax.ShapeDtypeStruct((T, ATT_W), F32)] * 2, args=(p, p, p))
    return tuple(outs) if ex is None else tuple(outs) + (moved,)


def sb_bwd_call(p, lt, do, B, S, ex=None):
    T = B * S
    TQ, TK = ATT_TQ, ATT_TK
    nq = S // TQ

    def body(q_ref, k_ref, v_ref, lt_ref, do_ref, dq_ref, dk_ref, dv_ref):
        qi = pl.program_id(2)

        @pl.when(qi == 0)
        def _():
            dk_ref[...] = jnp.zeros_like(dk_ref)
            dv_ref[...] = jnp.zeros_like(dv_ref)

        masks = _head_masks()
        qm = _pair(q_ref[...] * SB_SCALE, masks)
        dom = _pair(do_ref[...], masks)
        l8 = _rows8(_lane_selector(1), lt_ref[...])
        ltot = [_row_of(l8, j) for j in range(2)]
        row = _iota2((TK, TQ), 0)
        col = _iota2((TK, TQ), 1)
        incl = jnp.where(col <= row, 1.0, 0.0).astype(BF16)
        incl3 = jnp.concatenate([incl, incl, incl], axis=1)
        excl = jnp.where(col < row, 1.0, 0.0).astype(BF16)
        excl2 = jnp.concatenate([excl, excl], axis=1)
        vis = row < col
        dq_ref[...] = jnp.zeros_like(dq_ref)

        def step(kb, carry, diag):
            k0 = pl.multiple_of(kb * TK, TK)
            kf = k_ref[pl.ds(k0, TK), :]
            k = kf.astype(BF16)
            km = _pair(kf, masks)
            v = v_ref[pl.ds(k0, TK), :].astype(BF16)
            dq = jnp.zeros((TQ, LANES), F32)
            dk = jnp.zeros((TK, LANES), F32)
            dv = jnp.zeros((TK, LANES), F32)
            out = []
            for j in range(2):
                c_j, e_j = carry[2 * j], carry[2 * j + 1]
                lb, lk = _log_gates(_nt(k, qm[j]))
                if diag:
                    lk = jnp.where(vis, lk, 0.0)
                left = _nn(incl3, jnp.concatenate(_split3(lk), axis=0)) + c_j
                a = jnp.exp(lb + (ltot[j] - left))
                if diag:
                    a = jnp.where(vis, a, 0.0)
                e = a * _nt(v, dom[j])
                before = _nn(excl2, jnp.concatenate(_split2(e), axis=0)) + e_j
                dz = e - jnp.exp(lb) * (e + before)
                if diag:
                    dz = jnp.where(vis, dz, 0.0)
                dzb = dz.astype(BF16)
                dk = dk + _nn(dzb, qm[j])
                dv = dv + _nn(a.astype(BF16), dom[j])
                dq = dq + _tn(dzb, km[j])
                out += [c_j + jnp.sum(lk, axis=0, keepdims=True), e_j + jnp.sum(e, axis=0, keepdims=True)]
            dq_ref[...] += dq
            dk_ref[pl.ds(k0, TK), :] += dk
            dv_ref[pl.ds(k0, TK), :] += dv
            return tuple(out)

        zero = jnp.zeros((1, TQ), F32)
        carry = lax.fori_loop(0, qi, lambda i, c: step(i, c, False), (zero,) * 4)
        step(qi, carry, True)
        dq_ref[...] *= SB_SCALE

    qspec = pl.BlockSpec((TQ, LANES), lambda b, h, i: (b * nq + i, h))
    sspec = pl.BlockSpec((S, LANES), lambda b, h, i: (b, h))
    outs, moved = _call(
        body, ex, name="sb_bwd", grid=(B, HEADS // 2, nq),
        in_specs=[qspec,
                  pl.BlockSpec((S, LANES), lambda b, h, i: (b, 4 + h)),
                  pl.BlockSpec((S, LANES), lambda b, h, i: (b, 8 + h)),
                  qspec, qspec],
        out_specs=[qspec, sspec, sspec],
        out_shape=[jax.ShapeDtypeStruct((T, ATT_W), F32)] * 3, args=(p, p, p, lt, do))
    return tuple(outs) if ex is None else tuple(outs) + (moved,)


def _rope_masks(hp):
    grp = _iota2((1, LANES), 1) // ROPE_DIM
    return [grp == ((2 * hp + j) % 4) for j in range(2)]


def mla_fwd_call(qn, qr, kn, krt, vm, B, S):
    T = B * S
    TQ, TK = ATT_TQ, ATT_TK
    nq = S // TQ

    def body(qn_ref, qr_ref, kn_ref, kr_ref, v_ref, o_ref, lse_ref):
        hp = pl.program_id(1)
        qi = pl.program_id(2)
        masks = _head_masks()
        rmasks = _rope_masks(hp)
        qnv = qn_ref[...]
        qrv = qr_ref[...]
        qcat = [jnp.concatenate([jnp.where(masks[j], qnv, 0), jnp.where(rmasks[j], qrv, 0)], axis=1).astype(BF16)
                for j in range(2)]
        row = _iota2((TQ, TK), 0)
        col = _iota2((TQ, TK), 1)
        vis = col <= row

        def step(kb, carry, diag):
            k0 = pl.multiple_of(kb * TK, TK)
            kcat = jnp.concatenate([kn_ref[pl.ds(k0, TK), :], kr_ref[pl.ds(k0, TK), :]], axis=1)
            vmk = _pair(v_ref[pl.ds(k0, TK), :], masks)
            acc = carry[4]
            out = []
            for j in range(2):
                m_j, l_j = carry[2 * j], carry[2 * j + 1]
                s = _nt(qcat[j], kcat) * MLA_SCALE
                if diag:
                    s = jnp.where(vis, s, NEG_BIG)
                m_new = jnp.maximum(m_j, jnp.max(s, axis=1, keepdims=True))
                alpha = jnp.exp(m_j - m_new)
                pexp = jnp.exp(s - m_new)
                out += [m_new, alpha * l_j + jnp.sum(pexp, axis=1, keepdims=True)]
                acc_add = _nn(pexp.astype(BF16), vmk[j])
                acc = acc * jnp.where(masks[j], alpha, 1.0) + acc_add
            return tuple(out) + (acc,)

        neg = jnp.full((TQ, 1), NEG_BIG, F32)
        zero = jnp.zeros((TQ, 1), F32)
        carry = step(qi, (neg, zero, neg, zero, jnp.zeros((TQ, LANES), F32)), True)
        m0, l0, m1, l1, acc = lax.fori_loop(0, qi, lambda i, c: step(qi - 1 - i, c, False), carry)
        o_ref[...] = acc * jnp.where(masks[0], 1.0 / l0, 1.0 / l1)
        lane = _iota2((TQ, LANES), 1)
        lse_ref[...] = jnp.where(lane == 0, m0 + jnp.log(l0), jnp.where(lane == 1, m1 + jnp.log(l1), 0.0))

    qspec = pl.BlockSpec((TQ, LANES), lambda b, h, i: (b * nq + i, h))
    sspec = pl.BlockSpec((S, LANES), lambda b, h, i: (b, h))
    return pl.pallas_call(
        body, name="mla_fwd", grid=(B, HEADS // 2, nq),
        in_specs=[qspec, pl.BlockSpec((TQ, LANES), lambda b, h, i: (b * nq + i, h // 2)), sspec,
                  pl.BlockSpec((S, LANES), lambda b, h, i: (b, 0)), sspec],
        out_specs=[qspec, qspec],
        out_shape=[jax.ShapeDtypeStruct((T, ATT_W), F32)] * 2,
        compiler_params=_params(("parallel", "parallel", "arbitrary")),
    )(qn, qr, kn, krt, vm)


def mla_bwd_call(qn, qr, kn, krt, vm, o, lse, do, B, S, ex=None):
    T = B * S
    TQ, TK = ATT_TQ, ATT_TK
    nq = S // TQ

    def body(qn_ref, qr_ref, kn_ref, kr_ref, v_ref, o_ref, lse_ref, do_ref,
             dqn_ref, dqr_ref, dkn_ref, dv_ref, dkr_ref):
        hp = pl.program_id(1)
        qi = pl.program_id(2)

        @pl.when(qi == 0)
        def _():
            dkn_ref[...] = jnp.zeros_like(dkn_ref)
            dv_ref[...] = jnp.zeros_like(dv_ref)
            dkr_ref[...] = jnp.zeros_like(dkr_ref)

        masks = _head_masks()
        rmasks = _rope_masks(hp)
        qnv = qn_ref[...]
        qrv = qr_ref[...]
        qcat = [jnp.concatenate([jnp.where(masks[j], qnv, 0), jnp.where(rmasks[j], qrv, 0)], axis=1).astype(BF16)
                for j in range(2)]
        do = do_ref[...]
        dom = _pair(do, masks)
        d8 = _rows8(_lane_selector(HEAD_DIM), do * o_ref[...])
        l8 = _rows8(_lane_selector(1), lse_ref[...])
        dsum = [_row_of(d8, j) for j in range(2)]
        lse = [_row_of(l8, j) for j in range(2)]
        row = _iota2((TK, TQ), 0)
        col = _iota2((TK, TQ), 1)
        vis = row <= col
        dqn_ref[...] = jnp.zeros_like(dqn_ref)
        dqr_ref[...] = jnp.zeros_like(dqr_ref)

        def step(kb, diag):
            k0 = pl.multiple_of(kb * TK, TK)
            knv = kn_ref[pl.ds(k0, TK), :]
            krv = kr_ref[pl.ds(k0, TK), :]
            kcat = jnp.concatenate([knv, krv], axis=1)
            v = v_ref[pl.ds(k0, TK), :]
            dq = jnp.zeros((TQ, 2 * LANES), F32)
            dk = jnp.zeros((TK, 2 * LANES), F32)
            dv = jnp.zeros((TK, LANES), F32)
            for j in range(2):
                s = _nt(kcat, qcat[j]) * MLA_SCALE
                pr = jnp.exp(s - lse[j])
                if diag:
                    pr = jnp.where(vis, pr, 0.0)
                ds = (pr * (_nt(v, dom[j]) - dsum[j]) * MLA_SCALE).astype(BF16)
                dv = dv + _nn(pr.astype(BF16), dom[j])
                dk = dk + _nn(ds, qcat[j])
                kcat_j = jnp.concatenate([jnp.where(masks[j], knv, 0), jnp.where(rmasks[j], krv, 0)], axis=1)
                dq = dq + _tn(ds, kcat_j.astype(BF16))
            dqn_ref[...] += dq[:, :LANES]
            dqr_ref[...] += dq[:, LANES:]
            dkn_ref[pl.ds(k0, TK), :] += dk[:, :LANES]
            dkr_ref[pl.ds(k0, TK), :] += dk[:, LANES:]
            dv_ref[pl.ds(k0, TK), :] += dv

        step(qi, True)

        def loop(i, c):
            step(qi - 1 - i, False)
            return c

        lax.fori_loop(0, qi, loop, 0)

    qspec = pl.BlockSpec((TQ, LANES), lambda b, h, i: (b * nq + i, h))
    sspec = pl.BlockSpec((S, LANES), lambda b, h, i: (b, h))
    outs, moved = _call(
        body, ex, name="mla_bwd", grid=(B, HEADS // 2, nq),
        in_specs=[qspec, pl.BlockSpec((TQ, LANES), lambda b, h, i: (b * nq + i, h // 2)), sspec,
                  pl.BlockSpec((S, LANES), lambda b, h, i: (b, 0)), sspec, qspec, qspec, qspec],
        out_specs=[qspec, qspec, sspec, sspec, sspec],
        out_shape=[jax.ShapeDtypeStruct((T, ATT_W), F32)] * 5, args=(qn, qr, kn, krt, vm, o, lse, do))
    return tuple(outs) if ex is None else tuple(outs) + (moved,)


CONV_TC = 256


def _shift_down(x, n):
    return jnp.where(_iota2(x.shape, 0) >= n, pltpu.roll(x, n, 0), 0.0)


def _shift_up(x, n):
    rows = x.shape[0]
    return jnp.where(_iota2(x.shape, 0) < rows - n, pltpu.roll(x, rows - n, 0), 0.0)


def _taps(w_ref):
    return [w_ref[k:k + 1, :] for k in range(3)]


def _conv3(u, w, b):
    return w[0] * _shift_down(u, 2) + w[1] * _shift_down(u, 1) + w[2] * u + b


def conv_act_fwd_call(u, conv_w, conv_b, B, S):
    T = B * S
    nc = D_FF // CONV_TC

    def body(ug_ref, uv_ref, wg_ref, wv_ref, bg_ref, bv_ref, a_ref):
        gate = _conv3(ug_ref[...], _taps(wg_ref), bg_ref[...])
        val = _conv3(uv_ref[...], _taps(wv_ref), bv_ref[...])
        a_ref[...] =(gate * (1.0 / (1.0 + jnp.exp(-gate))) * val).astype(BF16)

    def blk(rows, off):
        return pl.BlockSpec((rows, CONV_TC), lambda b, j: (b if rows == S else 0, off + j))

    return pl.pallas_call(
        body, name="conv_act_fwd", grid=(B, nc),
        in_specs=[blk(S, 0), blk(S, nc), blk(3, 0), blk(3, nc), blk(1, 0), blk(1, nc)],
        out_specs=blk(S, 0),
        out_shape=jax.ShapeDtypeStruct((T, D_FF), BF16),
        compiler_params=_params(("parallel", "parallel")),
    )(u, u, conv_w, conv_w, conv_b, conv_b)


def conv_act_bwd_call(u, da, conv_w, conv_b, B, S):
    T = B * S
    nc = D_FF // CONV_TC

    def body(ug_ref, uv_ref, da_ref, wg_ref, wv_ref, bg_ref, bv_ref,
             dug_ref, duv_ref, dwg_ref, dwv_ref, dbg_ref, dbv_ref):
        @pl.when(pl.program_id(1) == 0)
        def _():
            for r in (dwg_ref, dwv_ref, dbg_ref, dbv_ref):
                r[...] = jnp.zeros_like(r)

        ug = ug_ref[...]
        uv = uv_ref[...]
        wg = _taps(wg_ref)
        wv = _taps(wv_ref)
        gate = _conv3(ug, wg, bg_ref[...])
        val = _conv3(uv, wv, bv_ref[...])
        da = da_ref[...]
        sig = 1.0 / (1.0 + jnp.exp(-gate))
        dval = da * (gate * sig)
        dgate = da * val * (sig * (1.0 + gate * (1.0 - sig)))
        for u_, d, w, du_ref, dw_ref, db_ref in ((ug, dgate, wg, dug_ref, dwg_ref, dbg_ref),
                                                 (uv, dval, wv, duv_ref, dwv_ref, dbv_ref)):
            du_ref[...] = (w[2] * d + w[1] * _shift_up(d, 1) + w[0] * _shift_up(d, 2)).astype(BF16)
            db_ref[...] += jnp.sum(d, axis=0, keepdims=True)
            dw_ref[0:1, :] += jnp.sum(d * _shift_down(u_, 2), axis=0, keepdims=True)
            dw_ref[1:2, :] += jnp.sum(d * _shift_down(u_, 1), axis=0, keepdims=True)
            dw_ref[2:3, :] += jnp.sum(d * u_, axis=0, keepdims=True)

    def blk(rows, off):
        return pl.BlockSpec((rows, CONV_TC), lambda j, b: (b if rows == S else 0, off + j))

    return pl.pallas_call(
        body, name="conv_act_bwd", grid=(nc, B),
        in_specs=[blk(S, 0), blk(S, nc), blk(S, 0), blk(3, 0), blk(3, nc), blk(1, 0), blk(1, nc)],
        out_specs=[blk(S, 0), blk(S, 0), blk(3, 0), blk(3, 0), blk(1, 0), blk(1, 0)],
        out_shape=[jax.ShapeDtypeStruct((T, D_FF), BF16), jax.ShapeDtypeStruct((T, D_FF), BF16),
                   jax.ShapeDtypeStruct((3, D_FF), F32), jax.ShapeDtypeStruct((3, D_FF), F32),
                   jax.ShapeDtypeStruct((1, D_FF), F32), jax.ShapeDtypeStruct((1, D_FF), F32)],
        compiler_params=_params(("parallel", "arbitrary")),
    )(u, u, da, conv_w, conv_w, conv_b, conv_b)


CHIP_MASKS = ((1, 0), (0, 1), (1, 1))


def _place():
    return lax.axis_index("x"), lax.axis_index("y"), lax.axis_index("c")


HALF_ALIGN = 32


def _any_specs(n):
    return [pl.BlockSpec(memory_space=pl.ANY)] * n


def _half_rows(r, half):
    return pl.ds(pl.multiple_of(half * (r // 2), HALF_ALIGN // 2), r // 2)


def _remote(src, dst, send_sem, recv_sem, device):
    return pltpu.make_async_remote_copy(src_ref=src, dst_ref=dst, send_sem=send_sem, recv_sem=recv_sem,
                                        device_id=device, device_id_type=MESH)


class Exchange:
    def __init__(self, ins, out_shape, sems, start, finish):
        self.ins, self.out_shape, self.sems, self.start, self.finish = list(ins), list(out_shape), list(sems), start, finish


def gather_group(shards):
    n = len(shards)
    split = [s.shape[0] % HALF_ALIGN == 0 for s in shards]

    def rows(w, half):
        return _half_rows(shards[w].shape[0], half) if split[w] else slice(None)

    def copies(ins, outs, sems):
        ici_s, ici_r, _, _, local_sems = sems
        x, y, c = _place()
        chip = 2 * x + y
        local = [pltpu.make_async_copy(ins[w], outs[w].at[chip], local_sems.at[w]) for w in range(n)]
        sends = [_remote(ins[w].at[rows(w, c)], outs[w].at[chip, rows(w, c)], ici_s.at[w, k], ici_r.at[w, k],
                         (x ^ fx, y ^ fy, c))
                 for w in range(n) for k, (fx, fy) in enumerate(CHIP_MASKS)]
        return local, sends

    def start(ins, outs, sems):
        local, sends = copies(ins, outs, sems)
        for cp in local + sends:
            cp.start()

    def finish(ins, outs, sems):
        ici_s, ici_r, d2d_s, d2d_r, _ = sems
        x, y, c = _place()
        sib = (x, y, 1 - c)
        local, sends = copies(ins, outs, sems)
        for w in range(n):
            for k, (fx, fy) in enumerate(CHIP_MASKS):
                landed = outs[w].at[2 * (x ^ fx) + (y ^ fy), rows(w, c)]
                _remote(landed, landed, ici_s.at[w, k], ici_r.at[w, k], sib).wait_recv()
                if split[w]:
                    cp = _remote(landed, landed, d2d_s.at[w, k], d2d_r.at[w, k], sib)
                    cp.start()
                    sends.append(cp)
        for w in range(n):
            for k, (fx, fy) in enumerate(CHIP_MASKS):
                if split[w]:
                    other = outs[w].at[2 * (x ^ fx) + (y ^ fy), rows(w, 1 - c)]
                    _remote(other, other, d2d_s.at[w, k], d2d_r.at[w, k], sib).wait_recv()
        for cp in sends:
            cp.wait_send()
        for cp in local:
            cp.wait()

    sems = pltpu.SemaphoreType.DMA((n, 3))
    return Exchange(shards, [jax.ShapeDtypeStruct((N_CHIPS,) + s.shape, s.dtype) for s in shards],
                    [sems, sems, sems, sems, pltpu.SemaphoreType.DMA((n,))], start, finish)


def swap_half(parts):
    n = len(parts)

    def copies(ins, outs, sems):
        x, y, c = _place()
        return [_remote(ins[w].at[:, _half_rows(parts[w].shape[1], 1 - c)], outs[w], sems[0].at[w], sems[1].at[w],
                        (x, y, 1 - c)) for w in range(n)]

    def start(ins, outs, sems):
        for cp in copies(ins, outs, sems):
            cp.start()

    def finish(ins, outs, sems):
        for cp in copies(ins, outs, sems):
            cp.wait_recv()
            cp.wait_send()

    return Exchange(parts, [jax.ShapeDtypeStruct((N_CHIPS, p.shape[1] // 2, p.shape[2]), F32) for p in parts],
                    [pltpu.SemaphoreType.DMA((n,))] * 2, start, finish)


def scatter_half(halves):
    n = len(halves)

    def copies(ins, outs, sems):
        x, y, c = _place()
        return [_remote(ins[w].at[2 * (x ^ fx) + (y ^ fy)], outs[w].at[k], sems[0].at[w, k], sems[1].at[w, k],
                        (x ^ fx, y ^ fy, c))
                for w in range(n) for k, (fx, fy) in enumerate(CHIP_MASKS)]

    def start(ins, outs, sems):
        for cp in copies(ins, outs, sems):
            cp.start()

    def finish(ins, outs, sems):
        for cp in copies(ins, outs, sems):
            cp.wait_recv()
            cp.wait_send()

    return Exchange(halves, [jax.ShapeDtypeStruct((3,) + h.shape[1:], F32) for h in halves],
                    [pltpu.SemaphoreType.DMA((n, 3))] * 2, start, finish)


def swap_final(finals):
    n = len(finals)

    def copies(ins, outs, sems):
        x, y, c = _place()
        mine = [outs[w].at[_half_rows(2 * finals[w].shape[0], c)] for w in range(n)]
        local = [pltpu.make_async_copy(ins[w], mine[w], sems[2].at[w]) for w in range(n)]
        sends = [_remote(ins[w], mine[w], sems[0].at[w], sems[1].at[w], (x, y, 1 - c)) for w in range(n)]
        return local, sends

    def start(ins, outs, sems):
        local, sends = copies(ins, outs, sems)
        for cp in local + sends:
            cp.start()

    def finish(ins, outs, sems):
        x, y, c = _place()
        local, sends = copies(ins, outs, sems)
        for w in range(n):
            got = outs[w].at[_half_rows(2 * finals[w].shape[0], 1 - c)]
            _remote(got, got, sems[0].at[w], sems[1].at[w], (x, y, 1 - c)).wait_recv()
        for cp in sends:
            cp.wait_send()
        for cp in local:
            cp.wait()

    return Exchange(finals, [jax.ShapeDtypeStruct((2 * f.shape[0], f.shape[1]), F32) for f in finals],
                    [pltpu.SemaphoreType.DMA((n,))] * 3, start, finish)


def exchange_call(name, ex):
    n, m = len(ex.ins), len(ex.out_shape)

    def body(*refs):
        ins, outs, sems = refs[:n], refs[n:n + m], refs[n + m:]
        ex.start(ins, outs, sems)
        ex.finish(ins, outs, sems)

    return pl.pallas_call(body, name=name, in_specs=_any_specs(n), out_specs=_any_specs(m), out_shape=ex.out_shape,
                          scratch_shapes=ex.sems, compiler_params=_params())(*ex.ins)


def _call(body, ex, *, name, grid, in_specs, out_specs, out_shape, args, scratch_shapes=()):
    sem = ("arbitrary",) * len(grid)
    if ex is None:
        outs = pl.pallas_call(body, name=name, grid=grid, in_specs=in_specs, out_specs=out_specs, out_shape=out_shape,
                              scratch_shapes=list(scratch_shapes), compiler_params=_params(sem))(*args)
        return outs, None
    ni, no, ns = len(in_specs), len(out_specs), len(scratch_shapes)
    ne, me = len(ex.ins), len(ex.out_shape)

    def wrapped(*refs):
        own_in, ex_in = refs[:ni], refs[ni:ni + ne]
        own_out, ex_out = refs[ni + ne:ni + ne + no], refs[ni + ne + no:ni + ne + no + me]
        own_scr, ex_sems = refs[ni + ne + no + me:ni + ne + no + me + ns], refs[ni + ne + no + me + ns:]
        ids = [pl.program_id(a) for a in range(len(grid))]
        first = functools.reduce(jnp.logical_and, [i == 0 for i in ids])
        last = functools.reduce(jnp.logical_and, [i == g - 1 for i, g in zip(ids, grid)])

        @pl.when(first)
        def _():
            ex.start(ex_in, ex_out, ex_sems)

        body(*own_in, *own_out, *own_scr)

        @pl.when(last)
        def _():
            ex.finish(ex_in, ex_out, ex_sems)

    outs = pl.pallas_call(
        wrapped, name=name, grid=grid, in_specs=list(in_specs) + _any_specs(ne),
        out_specs=list(out_specs) + _any_specs(me), out_shape=list(out_shape) + ex.out_shape,
        scratch_shapes=list(scratch_shapes) + ex.sems, compiler_params=_params(sem))(*args, *ex.ins)
    return outs[:no], outs[no:]


def _row_tile(rows, cap):
    return max(t for t in range(8, min(rows, cap) + 1, 8) if rows % t == 0)


def add_half_call(name, part, got, core):
    _, rh, cols = got.shape
    tr = _row_tile(rh, 128)
    nb = rh // tr

    def body(core_ref, p_ref, g_ref, o_ref):
        o_ref[...] = p_ref[...] + g_ref[...]

    blk = (N_CHIPS, tr, cols)
    return pl.pallas_call(
        body, name=name,
        grid_spec=pltpu.PrefetchScalarGridSpec(
            num_scalar_prefetch=1, grid=(nb,),
            in_specs=[pl.BlockSpec(blk, lambda i, core_ref: (0, core_ref[0] * nb + i, 0)),
                      pl.BlockSpec(blk, lambda i, core_ref: (0, i, 0))],
            out_specs=pl.BlockSpec(blk, lambda i, core_ref: (0, i, 0))),
        out_shape=jax.ShapeDtypeStruct(got.shape, F32),
        compiler_params=_params(("parallel",)),
    )(core, part, got)


def sum_chips_call(name, half, got, chip):
    _, rh, cols = got.shape
    tr = _row_tile(rh, 128)

    def body(chip_ref, h_ref, g_ref, o_ref):
        o_ref[...] = ((h_ref[0] + g_ref[0]) + g_ref[1]) + g_ref[2]

    return pl.pallas_call(
        body, name=name,
        grid_spec=pltpu.PrefetchScalarGridSpec(
            num_scalar_prefetch=1, grid=(rh // tr,),
            in_specs=[pl.BlockSpec((1, tr, cols), lambda i, chip_ref: (chip_ref[0], i, 0)),
                      pl.BlockSpec((3, tr, cols), lambda i, chip_ref: (0, i, 0))],
            out_specs=pl.BlockSpec((tr, cols), lambda i, chip_ref: (i, 0))),
        out_shape=jax.ShapeDtypeStruct((rh, cols), F32),
        compiler_params=_params(("parallel",)),
    )(chip, half, got)


def _adamw(w, g, m, v):
    m = ADAM_B1 * m + (1.0 - ADAM_B1) * g
    v = ADAM_B2 * v + (1.0 - ADAM_B2) * (g * g)
    m_hat = m / (1.0 - ADAM_B1 ** ADAM_STEP)
    v_hat = v / (1.0 - ADAM_B2 ** ADAM_STEP)
    delta = -ADAM_LR * (m_hat / (jnp.sqrt(v_hat) + ADAM_EPS) + ADAM_WD * w)
    return delta, m, v


def adamw_call(name, g, w, m, v):
    r, cols = w.shape
    tr = r if r % 8 else _row_tile(r, 256)

    def body(g_ref, w_ref, m_ref, v_ref, d_ref, nm_ref, nv_ref):
        d_ref[...], nm_ref[...], nv_ref[...] = _adamw(w_ref[...], g_ref[...], m_ref[...], v_ref[...])

    spec = pl.BlockSpec((tr, cols), lambda i: (i, 0))
    return pl.pallas_call(
        body, name=name, grid=(r // tr,), in_specs=[spec] * 4, out_specs=[spec] * 3,
        out_shape=[jax.ShapeDtypeStruct((r, cols), F32)] * 3,
        compiler_params=_params(("parallel",)),
    )(g, w, m, v)


def allsum_small_call(v):
    R = v.shape[0]

    def body(v_ref, out_ref, buf, send_sems, recv_sems):
        x, y, c = _place()
        me = 4 * x + 2 * y + c
        buf[me] = v_ref[...]
        sends = []
        for k in range(1, N_DEV):
            fx, fy, fc = (k >> 2) & 1, (k >> 1) & 1, k & 1
            cp = pltpu.make_async_remote_copy(
                src_ref=v_ref, dst_ref=buf.at[me], send_sem=send_sems.at[k - 1], recv_sem=recv_sems.at[k - 1],
                device_id=(x ^ fx, y ^ fy, c ^ fc), device_id_type=MESH)
            cp.start()
            sends.append(cp)
        for k in range(1, N_DEV):
            pltpu.make_async_remote_copy(
                src_ref=v_ref, dst_ref=buf.at[me ^ k], send_sem=send_sems.at[k - 1], recv_sem=recv_sems.at[k - 1],
                device_id=(x, y, c), device_id_type=MESH).wait_recv()
        acc = buf[0]
        for d in range(1, N_DEV):
            acc = acc + buf[d]
        out_ref[...] = acc
        for cp in sends:
            cp.wait_send()

    vm = pl.BlockSpec(memory_space=pltpu.VMEM)
    return pl.pallas_call(
        body, name="allsum_small", in_specs=[vm], out_specs=vm,
        out_shape=jax.ShapeDtypeStruct((R, LANES), F32),
        scratch_shapes=[pltpu.VMEM((N_DEV, R, LANES), F32), pltpu.SemaphoreType.DMA((N_DEV - 1,)),
                        pltpu.SemaphoreType.DMA((N_DEV - 1,))],
        compiler_params=_params(),
    )(v)


def _slab(flat, mult):
    n = flat.shape[-1]
    rows = -(-n // (LANES * mult)) * mult
    flat = jnp.pad(flat, [(0, 0)] * (flat.ndim - 1) + [(0, rows * LANES - n)])
    return flat.reshape(flat.shape[:-1] + (rows, LANES))


def full_from_chips(blocks, by_col):
    _, r, c = blocks.shape
    return blocks.transpose(1, 0, 2).reshape(r, N_CHIPS * c) if by_col else blocks.reshape(N_CHIPS * r, c)


def chips_from_full(full, by_col):
    if by_col:
        r, c = full.shape[0], full.shape[1] // N_CHIPS
        return full.reshape(r, N_CHIPS, c).transpose(1, 0, 2)
    return full.reshape(N_CHIPS, full.shape[0] // N_CHIPS, full.shape[1])


SMALL_PACK = SMALL_W + ("loss", "conv_w")
SMALL_PACK_N = {**SMALL_N, "loss": 1, "conv_w": 3 * 2 * D_FF}


def pack_small(vals):
    zero = jnp.zeros((1,), F32)
    return _slab(jnp.concatenate([vals[n].reshape(-1) if n in vals else jnp.tile(zero, SMALL_PACK_N[n])
                                  for n in SMALL_PACK]), 8)


def unpack_small(slab, shapes):
    flat = slab.reshape(-1)
    out, off = {}, 0
    for n in SMALL_PACK:
        out[n] = flat[off:off + SMALL_PACK_N[n]].reshape(shapes[n])
        off += SMALL_PACK_N[n]
    return out


def _split_heads(w, a, b):
    r = w.shape[0]
    w3 = w.reshape(r, HEADS, a + b)
    return w3[:, :, :a].reshape(r, HEADS * a), w3[:, :, a:].reshape(r, HEADS * b)


def _merge_heads(wa, wb, a, b):
    r = wa.shape[0]
    return jnp.concatenate([wa.reshape(r, HEADS, a), wb.reshape(r, HEADS, b)], axis=2).reshape(r, HEADS * (a + b))


def kernel(x, positions, g_mix, w_in, g_cq, w_uq, g_ckv, w_ukv, g_sb_out, g_mla_out, w_out, g_ffn, w_up, conv_w, conv_b, w_down, g_final, loss_target, m_g_mix, m_w_in, m_g_cq, m_w_uq, m_g_ckv, m_w_ukv, m_g_sb_out, m_g_mla_out, m_w_out, m_g_ffn, m_w_up, m_conv_w, m_conv_b, m_w_down, m_g_final, v_g_mix, v_w_in, v_g_cq, v_w_uq, v_g_ckv, v_w_ukv, v_g_sb_out, v_g_mla_out, v_w_out, v_g_ffn, v_w_up, v_conv_w, v_conv_b, v_w_down, v_g_final):
    given = dict(locals())
    B, S, _ = x.shape
    T = B * S
    w_big = {n: given[n][0] for n in BIG_W}
    m_big = {n: given["m_" + n][0] for n in BIG_W}
    v_big = {n: given["v_" + n][0] for n in BIG_W}

    first = ("w_in", "w_uq", "w_ukv")
    later = ("w_out", "w_up", "w_down", "conv_w")
    got_w = exchange_call("gather_first", gather_group([w_big[n].astype(BF16) for n in first]))
    full = {n: full_from_chips(g_, BIG_SHARD[n][2]) for n, g_ in zip(first, got_w)}
    gather_later = gather_group([w_big[n] if n == "conv_w" else w_big[n].astype(BF16) for n in later])
    w_in_p = jnp.pad(full["w_in"], ((0, 0), (0, IN_COLS_PAD - IN_COLS)))
    w_uq_p = jnp.concatenate(_split_heads(full["w_uq"], HEAD_DIM, ROPE_DIM), axis=1)
    w_ukv_p = jnp.concatenate(_split_heads(full["w_ukv"], HEAD_DIM, HEAD_DIM), axis=1)

    x2d = x.reshape(T, D_MODEL)
    half = ROPE_DIM // 2
    inv_freq = 1.0 / (ROPE_BASE ** (jnp.arange(half, dtype=F32) * (2.0 / ROPE_DIM)))
    cos, sin = rope_tab_call(positions.reshape(T, 1), jnp.tile(inv_freq, LANES // half).reshape(1, LANES))
    h = rmsnorm_fwd_call("norm_mix", x2d, g_mix)
    p = matmul_call("proj_in", h, w_in_p, "nn", tn=IN_COLS_PAD // 2)
    qn, qr, kn, vm, krt, cqn, ckvn = mla_prep_fwd_call(p, cos, sin, g_cq, g_ckv, w_uq_p, w_ukv_p)
    o_sb, lt_sb, got_w = sb_fwd_call(p, B, S, ex=gather_later)
    full.update({n: full_from_chips(g_, BIG_SHARD[n][2]) for n, g_ in zip(later, got_w)})
    conv_w_full = full["conv_w"]
    o_mla, lse = mla_fwd_call(qn, qr, kn, krt, vm, B, S)
    o_cat = outnorm_fwd_call(o_sb, o_mla, g_sb_out, g_mla_out)
    x1 = matmul_call("proj_out", o_cat, full["w_out"], "nn", res=x2d)
    hn = rmsnorm_fwd_call("norm_ffn", x1, g_ffn)
    u = matmul_call("ffn_up", hn, full["w_up"], "nn", tn=2 * D_FF // 4)
    act = conv_act_fwd_call(u, conv_w_full, conv_b, B, S)
    x2 = matmul_call("ffn_down", act, full["w_down"], "nn", res=x1)
    dx2, loss_row, dg_final = final_loss_call(x2, g_final.reshape(1, D_MODEL), loss_target.reshape(T, D_MODEL))

    xi, yi, ci = _place()
    chip = (2 * xi + yi).astype(jnp.int32).reshape(1)
    core = ci.astype(jnp.int32).reshape(1)

    def add_halves(names, parts, sib_rows):
        return [add_half_call("add_half_" + n, p_, s_, core) for n, p_, s_ in zip(names, parts, sib_rows)]

    def sum_chips(names, halves, from_chips):
        return [sum_chips_call("sum_chips_" + n, h_, f_, chip) for n, h_, f_ in zip(names, halves, from_chips)]

    ffn_w = ("w_down", "w_up")
    parts_ffn = [chips_from_full(wgrad_call("wgrad_down", act, dx2, tn=512), False)]
    da = matmul_call("ffn_down_bwd", dx2, full["w_down"], "nt", tn=D_FF // 2)
    du_g, du_v, dcw_g, dcw_v, dcb_g, dcb_v = conv_act_bwd_call(u, da, conv_w_full, conv_b, B, S)
    du = jnp.concatenate([du_g, du_v], axis=1)
    parts_ffn.append(wgrad_call("wgrad_up", hn, du, tn=2 * D_FF // 4, by_chip=True))
    dhn, sib_ffn = matmul_call("ffn_up_bwd", du, full["w_up"], "nt", tn=512, ex=swap_half(parts_ffn))
    dx1, dg_ffn = rmsnorm_bwd_call("norm_ffn_bwd", x1, g_ffn, dhn, dx2)
    parts_out = [chips_from_full(wgrad_call("wgrad_out", o_cat, dx1), False)]
    do_cat = matmul_call("proj_out_bwd", dx1, full["w_out"], "nt")
    do_sb, do_mla, dg_sb_out, dg_mla_out, sib_out = outnorm_bwd_call(
        o_sb, o_mla, g_sb_out, g_mla_out, do_cat, ex=swap_half(parts_out))
    early = ffn_w + ("w_out",)
    halves = add_halves(early, parts_ffn + parts_out, list(sib_ffn) + list(sib_out))
    dq_sb, dk_sb, dv_sb, from_chips = sb_bwd_call(p, lt_sb, do_sb, B, S, ex=scatter_half(halves))
    finals = sum_chips(early, halves, from_chips)
    dqn, dqr4, dkn, dvm, dkrt4, done = mla_bwd_call(qn, qr, kn, krt, vm, o_mla, lse, do_mla, B, S, ex=swap_final(finals))
    grads = dict(zip(early, done))
    dcq, dckvr, dq_cat, dkv_cat, dg_cq, dg_ckv = mla_prep_bwd_call(
        p, cos, sin, g_cq, g_ckv, w_uq_p, w_ukv_p, dqn, dqr4, dkn, dvm, dkrt4)
    dw_uq_p = wgrad_call("wgrad_uq", cqn, dq_cat)
    dw_ukv_p = wgrad_call("wgrad_ukv", ckvn, dkv_cat)
    dp = jnp.concatenate([dq_sb, dk_sb, dv_sb, dcq, dckvr], axis=1)
    late = ("w_uq", "w_ukv", "w_in")
    parts_late = [chips_from_full(g_, True) for g_ in (
        _merge_heads(dw_uq_p[:, :ATT_W], dw_uq_p[:, ATT_W:], HEAD_DIM, ROPE_DIM),
        _merge_heads(dw_ukv_p[:, :ATT_W], dw_ukv_p[:, ATT_W:], HEAD_DIM, HEAD_DIM),
        wgrad_call("wgrad_in", h, dp, tn=IN_COLS_PAD // 2)[:, :IN_COLS])]
    dh, sib_late = matmul_call("proj_in_bwd", dp, w_in_p, "nt", ex=swap_half(parts_late))
    halves = add_halves(late, parts_late, sib_late)
    grad_x, dg_mix, from_chips = rmsnorm_bwd_call("norm_mix_bwd", x2d, g_mix, dh, dx1, ex=scatter_half(halves))
    finals = sum_chips(late, halves, from_chips)
    grads.update(zip(late, exchange_call("swap_final_late", swap_final(finals))))

    shapes = {n: given[n].shape for n in SMALL_W}
    shapes.update(loss=(), conv_w=(3, 2 * D_FF))
    small_g = {"g_mix": dg_mix, "g_cq": dg_cq, "g_ckv": dg_ckv, "g_sb_out": dg_sb_out, "g_mla_out": dg_mla_out,
               "g_ffn": dg_ffn, "conv_b": jnp.concatenate([dcb_g, dcb_v], axis=1), "g_final": dg_final,
               "loss": loss_row[0, :1], "conv_w": jnp.concatenate([dcw_g, dcw_v], axis=1)}
    gs_slab = allsum_small_call(pack_small(small_g))
    small_in = [pack_small({n: given[pre + n] for n in SMALL_W}) for pre in ("", "m_", "v_")]
    small_out = [unpack_small(s, shapes) for s in (gs_slab,) + tuple(adamw_call("adamw_small", gs_slab, *small_in))]
    cw_cols = BIG_SHARD["conv_w"][1]
    grads["conv_w"] = lax.dynamic_slice_in_dim(small_out[0]["conv_w"], chip[0] * cw_cols, cw_cols, axis=1)

    big_out = {n: (grads[n],) + tuple(adamw_call("adamw_" + n, grads[n], w_big[n], m_big[n], v_big[n])) for n in BIG_W}
    weights = ("g_mix", "w_in", "g_cq", "w_uq", "g_ckv", "w_ukv", "g_sb_out", "g_mla_out", "w_out", "g_ffn",
               "w_up", "conv_w", "conv_b", "w_down", "g_final")
    outs = [small_out[0]["loss"], grad_x.reshape(B, S, D_MODEL)]
    for k in range(4):
        for n in weights:
            outs.append(big_out[n][k][None] if n in BIG_W else small_out[k][n])
    return tuple(outs)
```

```python
import functools

import jax
import jax.numpy as jnp
from jax import lax
from jax.experimental import pallas as pl
from jax.experimental.pallas import tpu as pltpu

F32 = jnp.float32
BF16 = jnp.bfloat16
MESH = pl.DeviceIdType.MESH

D_MODEL = 1024
HEADS = 8
HEAD_DIM = 64
ATT_W = HEADS * HEAD_DIM
ROPE_DIM = 32
ROPE_W = HEADS * ROPE_DIM
QK_DIM = HEAD_DIM + ROPE_DIM
Q_RANK = 384
KV_RANK = 256
D_FF = 2816
IN_COLS = 2208
IN_COLS_PAD = 2304
EPS = 1e-6
ROPE_BASE = 10000.0
SB_SCALE = HEAD_DIM ** -0.5
MLA_SCALE = QK_DIM ** -0.5
LANES = 128
N_CHIPS = 4
N_DEV = 8
VMEM_LIMIT = 48 * 1024 * 1024
ATT_TQ = 256
ATT_TK = 256
ATT_PAIRS = 4
PAIR_LANES = [slice(i * LANES, (i + 1) * LANES) for i in range(ATT_PAIRS)]
NEG_BIG = -1e30

ADAM_LR = 0.001
ADAM_B1 = 0.9
ADAM_B2 = 0.999
ADAM_EPS = 1e-08
ADAM_WD = 0.01
ADAM_STEP = 10

BIG_W = ("w_in", "w_uq", "w_ukv", "w_out", "w_up", "conv_w", "w_down")
BIG_SHARD = {
    "w_in": (D_MODEL, IN_COLS // 4, True),
    "w_uq": (Q_RANK, HEADS * QK_DIM // 4, True),
    "w_ukv": (KV_RANK, 2 * ATT_W // 4, True),
    "w_out": (2 * ATT_W // 4, D_MODEL, False),
    "w_up": (D_MODEL, 2 * D_FF // 4, True),
    "conv_w": (3, 2 * D_FF // 4, True),
    "w_down": (D_FF // 4, D_MODEL, False),
}
SMALL_W = ("g_mix", "g_cq", "g_ckv", "g_sb_out", "g_mla_out", "g_ffn", "conv_b", "g_final")
SMALL_N = {"g_mix": D_MODEL, "g_cq": Q_RANK, "g_ckv": KV_RANK, "g_sb_out": ATT_W, "g_mla_out": ATT_W,
           "g_ffn": D_MODEL, "conv_b": 2 * D_FF, "g_final": D_MODEL}


def _params(sem=None, **kw):
    return pltpu.CompilerParams(dimension_semantics=sem, vmem_limit_bytes=VMEM_LIMIT, **kw)


def _dot(a, b, dims):
    return lax.dot_general(a, b, (dims, ((), ())), preferred_element_type=F32)


def _nn(a, b):
    return _dot(a, b, ((1,), (0,)))


def _nt(a, b):
    return _dot(a, b, ((1,), (1,)))


def _tn(a, b):
    return _dot(a, b, ((0,), (0,)))


def _split2(x):
    hi = x.astype(BF16)
    lo = (x - hi.astype(F32)).astype(BF16)
    return hi, lo


def _split3(x):
    hi = x.astype(BF16)
    r1 = x - hi.astype(F32)
    mid = r1.astype(BF16)
    return hi, mid, (r1 - mid.astype(F32)).astype(BF16)


def _rms_r(x, d):
    return lax.rsqrt(jnp.sum(x * x, axis=-1, keepdims=True) * (1.0 / d) + EPS)


def _rms_bwd(x, g, dy, d):
    r = _rms_r(x, d)
    xhat = x * r
    gy = dy * g
    dx = r * (gy - xhat * (jnp.sum(xhat * gy, axis=-1, keepdims=True) * (1.0 / d)))
    return dx, jnp.sum(dy * xhat, axis=0, keepdims=True)


def _rot(x):
    lane = lax.broadcasted_iota(jnp.int32, x.shape, x.ndim - 1)
    n = x.shape[-1]
    return jnp.where((lane & 31) < 16, pltpu.roll(x, n - 16, x.ndim - 1), pltpu.roll(x, 16, x.ndim - 1))


def _fold4(x):
    return x + pltpu.roll(x, 32, 1) + pltpu.roll(x, 64, 1) + pltpu.roll(x, 96, 1)


def matmul_call(name, a, b, mode, out_dtype=F32, res=None, tm=512, tn=None, ex=None):
    M, K = a.shape
    N = b.shape[1] if mode == "nn" else b.shape[0]
    tn = N if tn is None else tn
    assert M % tm == 0 and N % tn == 0

    def body(*refs):
        if res is None:
            a_ref, b_ref, o_ref = refs
        else:
            a_ref, b_ref, r_ref, o_ref = refs
        av = a_ref[...].astype(BF16)
        bv = b_ref[...].astype(BF16)
        acc = _nn(av, bv) if mode == "nn" else _nt(av, bv)
        if res is not None:
            acc = r_ref[...] + acc
        o_ref[...] = acc.astype(out_dtype)

    in_specs = [pl.BlockSpec((tm, K), lambda j, i: (i, 0))]
    if mode == "nn":
        in_specs.append(pl.BlockSpec((K, tn), lambda j, i: (0, j)))
    else:
        in_specs.append(pl.BlockSpec((tn, K), lambda j, i: (j, 0)))
    args = [a, b]
    if res is not None:
        in_specs.append(pl.BlockSpec((tm, tn), lambda j, i: (i, j)))
        args.append(res)
    outs, moved = _call(body, ex, name=name, grid=(N // tn, M // tm), in_specs=in_specs,
                        out_specs=[pl.BlockSpec((tm, tn), lambda j, i: (i, j))],
                        out_shape=[jax.ShapeDtypeStruct((M, N), out_dtype)], args=args)
    return outs[0] if ex is None else (outs[0], moved)


def wgrad_call(name, a, b, tn=None, tt=512, by_chip=False):
    T, M = a.shape
    N = b.shape[1]
    tn = N if tn is None else tn
    assert T % tt == 0 and N % tn == 0
    if by_chip:
        out_spec = pl.BlockSpec((None, M, tn), lambda j, t: (j, 0, 0))
        out_shape = jax.ShapeDtypeStruct((N // tn, M, tn), F32)
    else:
        out_spec = pl.BlockSpec((M, tn), lambda j, t: (0, j))
        out_shape = jax.ShapeDtypeStruct((M, N), F32)

    def body(a_ref, b_ref, o_ref):
        @pl.when(pl.program_id(1) == 0)
        def _():
            o_ref[...] = jnp.zeros_like(o_ref)

        o_ref[...] += _tn(a_ref[...].astype(BF16), b_ref[...].astype(BF16))

    return pl.pallas_call(
        body, name=name, grid=(N // tn, T // tt),
        in_specs=[pl.BlockSpec((tt, M), lambda j, t: (t, 0)), pl.BlockSpec((tt, tn), lambda j, t: (t, j))],
        out_specs=out_spec, out_shape=out_shape,
        compiler_params=_params(("parallel", "arbitrary")),
    )(a, b)


def rmsnorm_fwd_call(name, x, g, tm=512):
    T, d = x.shape

    def body(x_ref, g_ref, o_ref):
        x = x_ref[...]
        o_ref[...] = ((x * _rms_r(x, d)) * g_ref[...]).astype(BF16)

    return pl.pallas_call(
        body, name=name, grid=(T // tm,),
        in_specs=[pl.BlockSpec((tm, d), lambda i: (i, 0)), pl.BlockSpec((1, d), lambda i: (0, 0))],
        out_specs=pl.BlockSpec((tm, d), lambda i: (i, 0)),
        out_shape=jax.ShapeDtypeStruct((T, d), BF16),
        compiler_params=_params(("parallel",)),
    )(x, g)


def rmsnorm_bwd_call(name, x, g, dy, res, tm=512, ex=None):
    T, d = x.shape

    def body(x_ref, g_ref, dy_ref, r_ref, dx_ref, dg_ref):
        @pl.when(pl.program_id(0) == 0)
        def _():
            dg_ref[...] = jnp.zeros_like(dg_ref)

        dx, dg = _rms_bwd(x_ref[...], g_ref[...], dy_ref[...], d)
        dx_ref[...] = r_ref[...] + dx
        dg_ref[...] += dg

    row = pl.BlockSpec((tm, d), lambda i: (i, 0))
    vec = pl.BlockSpec((1, d), lambda i: (0, 0))
    outs, moved = _call(body, ex, name=name, grid=(T // tm,), in_specs=[row, vec, row, row], out_specs=[row, vec],
                        out_shape=[jax.ShapeDtypeStruct((T, d), F32), jax.ShapeDtypeStruct((1, d), F32)],
                        args=(x, g, dy, res))
    return tuple(outs) if ex is None else tuple(outs) + (moved,)


def outnorm_fwd_call(o_sb, o_mla, g_sb, g_mla, tm=512):
    T = o_sb.shape[0]

    def body(a_ref, b_ref, ga_ref, gb_ref, o_ref):
        a = a_ref[...]
        b = b_ref[...]
        ya = (a * _rms_r(a, ATT_W)) * ga_ref[...]
        yb = (b * _rms_r(b, ATT_W)) * gb_ref[...]
        o_ref[...] = jnp.concatenate([ya, yb], axis=1).astype(BF16)

    row = pl.BlockSpec((tm, ATT_W), lambda i: (i, 0))
    vec = pl.BlockSpec((1, ATT_W), lambda i: (0, 0))
    return pl.pallas_call(
        body, name="outnorm_fwd", grid=(T // tm,), in_specs=[row, row, vec, vec],
        out_specs=pl.BlockSpec((tm, 2 * ATT_W), lambda i: (i, 0)),
        out_shape=jax.ShapeDtypeStruct((T, 2 * ATT_W), BF16),
        compiler_params=_params(("parallel",)),
    )(o_sb, o_mla, g_sb, g_mla)


def outnorm_bwd_call(o_sb, o_mla, g_sb, g_mla, do_cat, tm=512, ex=None):
    T = o_sb.shape[0]

    def body(a_ref, b_ref, ga_ref, gb_ref, d_ref, da_ref, db_ref, dga_ref, dgb_ref):
        @pl.when(pl.program_id(0) == 0)
        def _():
            dga_ref[...] = jnp.zeros_like(dga_ref)
            dgb_ref[...] = jnp.zeros_like(dgb_ref)

        d = d_ref[...]
        da, dga = _rms_bwd(a_ref[...], ga_ref[...], d[:, :ATT_W], ATT_W)
        db, dgb = _rms_bwd(b_ref[...], gb_ref[...], d[:, ATT_W:], ATT_W)
        da_ref[...] = da
        db_ref[...] = db
        dga_ref[...] += dga
        dgb_ref[...] += dgb

    row = pl.BlockSpec((tm, ATT_W), lambda i: (i, 0))
    vec = pl.BlockSpec((1, ATT_W), lambda i: (0, 0))
    outs, moved = _call(
        body, ex, name="outnorm_bwd", grid=(T // tm,),
        in_specs=[row, row, vec, vec, pl.BlockSpec((tm, 2 * ATT_W), lambda i: (i, 0))],
        out_specs=[row, row, vec, vec],
        out_shape=[jax.ShapeDtypeStruct((T, ATT_W), F32), jax.ShapeDtypeStruct((T, ATT_W), F32),
                   jax.ShapeDtypeStruct((1, ATT_W), F32), jax.ShapeDtypeStruct((1, ATT_W), F32)],
        args=(o_sb, o_mla, g_sb, g_mla, do_cat))
    return tuple(outs) if ex is None else tuple(outs) + (moved,)


def final_loss_call(x2, g, target, tm=512):
    T, d = x2.shape

    def body(x_ref, g_ref, t_ref, dx_ref, loss_ref, dg_ref):
        @pl.when(pl.program_id(0) == 0)
        def _():
            loss_ref[...] = jnp.zeros_like(loss_ref)
            dg_ref[...] = jnp.zeros_like(dg_ref)

        x = x_ref[...]
        g = g_ref[...]
        y = (x * _rms_r(x, d)) * g
        err = y - t_ref[...]
        loss_ref[...] += jnp.sum(jnp.sum(err * err, axis=1, keepdims=True), axis=0, keepdims=True) * (0.5 / d)
        dx, dg = _rms_bwd(x, g, err * (1.0 / d), d)
        dx_ref[...] = dx
        dg_ref[...] += dg

    row = pl.BlockSpec((tm, d), lambda i: (i, 0))
    vec = pl.BlockSpec((1, d), lambda i: (0, 0))
    return pl.pallas_call(
        body, name="final_loss", grid=(T // tm,), in_specs=[row, vec, row],
        out_specs=[row, pl.BlockSpec((1, LANES), lambda i: (0, 0)), vec],
        out_shape=[jax.ShapeDtypeStruct((T, d), F32), jax.ShapeDtypeStruct((1, LANES), F32),
                   jax.ShapeDtypeStruct((1, d), F32)],
        compiler_params=_params(("arbitrary",)),
    )(x2, g, target)


def rope_tab_call(pos, inv_freq, tm=512):
    T = pos.shape[0]

    def body(p_ref, f_ref, c_ref, s_ref):
        ang = p_ref[...].astype(F32) * f_ref[...]
        lane = lax.broadcasted_iota(jnp.int32, ang.shape, 1)
        sn = jnp.sin(ang)
        c_ref[...] = jnp.cos(ang)
        s_ref[...] = jnp.where((lane & 31) < 16, -sn, sn)

    row = pl.BlockSpec((tm, LANES), lambda i: (i, 0))
    return pl.pallas_call(
        body, name="rope_tab", grid=(T // tm,),
        in_specs=[pl.BlockSpec((tm, 1), lambda i: (i, 0)), pl.BlockSpec((1, LANES), lambda i: (0, 0))],
        out_specs=[row, row],
        out_shape=[jax.ShapeDtypeStruct((T, LANES), F32)] * 2,
        compiler_params=_params(("parallel",)),
    )(pos, inv_freq)


def mla_prep_fwd_call(p, cos, sin, g_cq, g_ckv, w_uq_p, w_ukv_p, tm=512):
    T = p.shape[0]

    def body(cq_ref, ckvr_ref, c_ref, s_ref, gq_ref, gkv_ref, wq_ref, wkv_ref,
             qn_ref, qr_ref, kn_ref, vm_ref, krt_ref, cqn_ref, ckvn_ref):
        c = c_ref[...]
        s = s_ref[...]
        cq = cq_ref[...]
        cqn = ((cq * _rms_r(cq, Q_RANK)) * gq_ref[...]).astype(BF16)
        cqn_ref[...] = cqn
        q = _nn(cqn, wq_ref[...])
        qn_ref[...] = q[:, :ATT_W].astype(BF16)
        for g in range(ROPE_W // LANES):
            qr = q[:, ATT_W + g * LANES:ATT_W + (g + 1) * LANES]
            qr_ref[:, g * LANES:(g + 1) * LANES] = (qr * c + _rot(qr) * s).astype(BF16)
        ckvr = ckvr_ref[...]
        ckv = ckvr[:, :KV_RANK]
        ckvn = ((ckv * _rms_r(ckv, KV_RANK)) * gkv_ref[...]).astype(BF16)
        ckvn_ref[...] = ckvn
        kv = _nn(ckvn, wkv_ref[...])
        kn_ref[...] = kv[:, :ATT_W].astype(BF16)
        vm_ref[...] = kv[:, ATT_W:].astype(BF16)
        kr = _fold4(ckvr[:, KV_RANK:])
        krt_ref[...] = (kr * c + _rot(kr) * s).astype(BF16)

    def row(w, j=0):
        return pl.BlockSpec((tm, w), lambda i: (i, j))

    def full(a):
        return pl.BlockSpec(a.shape, lambda i: (0, 0))

    return pl.pallas_call(
        body, name="mla_prep_fwd", grid=(T // tm,),
        in_specs=[row(Q_RANK, 4), row(Q_RANK, 5), row(LANES), row(LANES), full(g_cq), full(g_ckv),
                  full(w_uq_p), full(w_ukv_p)],
        out_specs=[row(ATT_W), row(ROPE_W), row(ATT_W), row(ATT_W), row(LANES), row(Q_RANK), row(KV_RANK)],
        out_shape=[jax.ShapeDtypeStruct((T, w), BF16) for w in (ATT_W, ROPE_W, ATT_W, ATT_W, LANES, Q_RANK, KV_RANK)],
        compiler_params=_params(("parallel",)),
    )(p, p, cos, sin, g_cq, g_ckv, w_uq_p, w_ukv_p)


def mla_prep_bwd_call(p, cos, sin, g_cq, g_ckv, w_uq_p, w_ukv_p, dqn, dqr4, dkn, dvm, dkrt4, tm=512):
    T = p.shape[0]

    def body(cq_ref, ckvr_ref, c_ref, s_ref, gq_ref, gkv_ref, wq_ref, wkv_ref,
             dqn_ref, dqr4_ref, dkn_ref, dvm_ref, dkrt4_ref,
             dcq_ref, dckvr_ref, dq_ref, dkv_ref, dgq_ref, dgkv_ref):
        @pl.when(pl.program_id(0) == 0)
        def _():
            dgq_ref[...] = jnp.zeros_like(dgq_ref)
            dgkv_ref[...] = jnp.zeros_like(dgkv_ref)

        c = c_ref[...]
        s = s_ref[...]
        d4 = dqr4_ref[...]
        dqr = [d4[:, :128] + d4[:, 128:256], d4[:, 256:384] + d4[:, 384:]]
        dqr = [t * c + _rot(t * s) for t in dqr]
        dq = jnp.concatenate([dqn_ref[...]] + dqr, axis=1).astype(BF16)
        dq_ref[...] = dq
        dcq, dgq = _rms_bwd(cq_ref[...], gq_ref[...], _nt(dq, wq_ref[...]), Q_RANK)
        dcq_ref[...] = dcq
        dgq_ref[...] += dgq
        dkv = jnp.concatenate([dkn_ref[...], dvm_ref[...]], axis=1).astype(BF16)
        dkv_ref[...] = dkv
        ckvr = ckvr_ref[...]
        dckv, dgkv = _rms_bwd(ckvr[:, :KV_RANK], gkv_ref[...], _nt(dkv, wkv_ref[...]), KV_RANK)
        dgkv_ref[...] += dgkv
        k4 = dkrt4_ref[...]
        dkr = _fold4(k4[:, :128] + k4[:, 128:256] + k4[:, 256:384] + k4[:, 384:])
        dkr = dkr * c + _rot(dkr * s)
        lane = lax.broadcasted_iota(jnp.int32, dkr.shape, 1)
        dckvr_ref[...] = jnp.concatenate([dckv, jnp.where(lane < ROPE_DIM, dkr, 0.0)], axis=1)

    def row(w, j=0):
        return pl.BlockSpec((tm, w), lambda i: (i, j))

    def full(a):
        return pl.BlockSpec(a.shape, lambda i: (0, 0))

    return pl.pallas_call(
        body, name="mla_prep_bwd", grid=(T // tm,),
        in_specs=[row(Q_RANK, 4), row(Q_RANK, 5), row(LANES), row(LANES), full(g_cq), full(g_ckv),
                  full(w_uq_p), full(w_ukv_p), row(ATT_W), row(ATT_W), row(ATT_W), row(ATT_W), row(ATT_W)],
        out_specs=[row(Q_RANK), row(Q_RANK), row(ATT_W + ROPE_W), row(2 * ATT_W),
                   pl.BlockSpec((1, Q_RANK), lambda i: (0, 0)), pl.BlockSpec((1, KV_RANK), lambda i: (0, 0))],
        out_shape=[jax.ShapeDtypeStruct((T, Q_RANK), F32), jax.ShapeDtypeStruct((T, Q_RANK), F32),
                   jax.ShapeDtypeStruct((T, ATT_W + ROPE_W), BF16), jax.ShapeDtypeStruct((T, 2 * ATT_W), BF16),
                   jax.ShapeDtypeStruct((1, Q_RANK), F32), jax.ShapeDtypeStruct((1, KV_RANK), F32)],
        compiler_params=_params(("arbitrary",)),
    )(p, p, cos, sin, g_cq, g_ckv, w_uq_p, w_ukv_p, dqn, dqr4, dkn, dvm, dkrt4)


def _iota2(shape, axis):
    return lax.broadcasted_iota(jnp.int32, shape, axis)


def _head_masks():
    lane = _iota2((1, LANES), 1)
    return lane < HEAD_DIM, lane >= HEAD_DIM


def _pair(x, masks, dtype=BF16):
    return [jnp.where(m, x, 0.0).astype(dtype) for m in masks]


def _log_gates(z):
    keep = jnp.maximum(z, 0.0) + jnp.log(1.0 + jnp.exp(-jnp.abs(z)))
    return z - keep, keep


def _last_row(x):
    return _row_of(x[x.shape[0] - 8:, :], 7)


def _lane_selector(group):
    return jnp.where(_iota2((16, LANES), 1) // group == _iota2((16, LANES), 0), 1.0, 0.0).astype(BF16)


def _rows8(sel_t, x):
    hi = x.astype(BF16)
    r1 = x - hi.astype(F32)
    mid = r1.astype(BF16)
    lo = (r1 - mid.astype(F32)).astype(BF16)
    return _nt(sel_t, hi) + _nt(sel_t, mid) + _nt(sel_t, lo)


def _row_of(x8, j):
    return jnp.sum(jnp.where(_iota2(x8.shape, 0) == j, x8, 0.0), axis=0, keepdims=True)


def sb_fwd_call(p, B, S, ex=None):
    T = B * S
    TQ, TK = ATT_TQ, ATT_TK
    nq = S // TQ

    def body(q_ref, k_ref, v_ref, o_ref, lt_ref):
        qi = pl.program_id(2)
        masks = _head_masks()
        qm = [_pair(q_ref[:, sl] * SB_SCALE, masks) for sl in PAIR_LANES]
        row = _iota2((TQ, TK), 0)
        col = _iota2((TQ, TK), 1)
        tri = jnp.where(row > col, 1.0, 0.0).astype(BF16)
        tri2 = jnp.concatenate([tri, tri], axis=0)
        vis = col < row
        o_ref[...] = jnp.zeros_like(o_ref)

        def step(kb, carry, diag):
            k0 = pl.multiple_of(kb * TK, TK)
            out = []
            for pp, sl in enumerate(PAIR_LANES):
                k = k_ref[pl.ds(k0, TK), sl].astype(BF16)
                vm = _pair(v_ref[pl.ds(k0, TK), sl], masks)
                acc = jnp.zeros((TQ, LANES), F32)
                for j in range(2):
                    c_j = carry[2 * pp + j]
                    lb, keep = _log_gates(_nt(qm[pp][j], k))
                    if diag:
                        keep = jnp.where(vis, keep, 0.0)
                    hi, lo = _split2(keep)
                    tail = _nn(jnp.concatenate([hi, lo], axis=1), tri2) + c_j
                    a = jnp.exp(lb - tail)
                    if diag:
                        a = jnp.where(vis, a, 0.0)
                    acc = acc + _nn(a.astype(BF16), vm[j])
                    out.append(c_j + jnp.sum(keep, axis=1, keepdims=True))
                o_ref[:, sl] += acc
            return tuple(out)

        zero = jnp.zeros((TQ, 1), F32)
        carry = step(qi, (zero,) * (2 * ATT_PAIRS), True)
        carry = lax.fori_loop(0, qi, lambda i, c: step(qi - 1 - i, c, False), carry)
        lane = _iota2((TQ, LANES), 1)
        for pp, sl in enumerate(PAIR_LANES):
            lt_ref[:, sl] = jnp.where(lane == 0, carry[2 * pp], jnp.where(lane == 1, carry[2 * pp + 1], 0.0))

    W = ATT_PAIRS * LANES
    qspec = pl.BlockSpec((TQ, W), lambda b, h, i: (b * nq + i, h))
    outs, moved = _call(
        body, ex, name="sb_fwd", grid=(B, HEADS // 2 // ATT_PAIRS, nq),
        in_specs=[qspec,
                  pl.BlockSpec((S, W), lambda b, h, i: (b, ATT_W // W + h)),
                  pl.BlockSpec((S, W), lambda b, h, i: (b, 2 * ATT_W // W + h))],
        out_specs=[qspec, qspec],
        out_shape=[jax.ShapeDtypeStruct((T, ATT_W), F32)] * 2, args=(p, p, p))
    return tuple(outs) if ex is None else tuple(outs) + (moved,)


def sb_bwd_call(p, lt, do, B, S, ex=None):
    T = B * S
    TQ, TK = ATT_TQ, ATT_TK
    nq = S // TQ

    def body(q_ref, k_ref, v_ref, lt_ref, do_ref, dq_ref, dk_ref, dv_ref):
        qi = pl.program_id(2)

        @pl.when(qi == 0)
        def _():
            dk_ref[...] = jnp.zeros_like(dk_ref)
            dv_ref[...] = jnp.zeros_like(dv_ref)

        masks = _head_masks()
        qm = [_pair(q_ref[:, sl] * SB_SCALE, masks) for sl in PAIR_LANES]
        dom = [_pair(do_ref[:, sl], masks) for sl in PAIR_LANES]
        start = []
        for sl in PAIR_LANES:
            l8 = _rows8(_lane_selector(1), lt_ref[:, sl])
            start += [-_row_of(l8, 0), jnp.zeros((1, TQ), F32), -_row_of(l8, 1), jnp.zeros((1, TQ), F32)]
        row = _iota2((TK, TQ), 0)
        col = _iota2((TK, TQ), 1)
        incl = jnp.where(col <= row, 1.0, 0.0).astype(BF16)
        incl2 = jnp.concatenate([incl, incl], axis=1)
        excl = jnp.where(col < row, 1.0, 0.0).astype(BF16)
        excl2 = jnp.concatenate([excl, excl], axis=1)
        vis = row < col
        dq_ref[...] = jnp.zeros_like(dq_ref)

        def step(kb, carry, diag):
            k0 = pl.multiple_of(kb * TK, TK)
            out = []
            for pp, sl in enumerate(PAIR_LANES):
                kf = k_ref[pl.ds(k0, TK), sl]
                k = kf.astype(BF16)
                km = _pair(kf, masks)
                v = v_ref[pl.ds(k0, TK), sl].astype(BF16)
                dq = jnp.zeros((TQ, LANES), F32)
                dk = jnp.zeros((TK, LANES), F32)
                dv = jnp.zeros((TK, LANES), F32)
                for j in range(2):
                    c_j, e_j = carry[4 * pp + 2 * j], carry[4 * pp + 2 * j + 1]
                    lb, keep = _log_gates(_nt(k, qm[pp][j]))
                    if diag:
                        keep = jnp.where(vis, keep, 0.0)
                    left = _nn(incl2, jnp.concatenate(_split2(keep), axis=0)) + c_j
                    a = jnp.exp(lb + left)
                    if diag:
                        a = jnp.where(vis, a, 0.0)
                    e = a * _nt(v, dom[pp][j])
                    before = _nn(excl2, jnp.concatenate(_split2(e), axis=0)) + e_j
                    dz = e - jnp.exp(lb) * (e + before)
                    if diag:
                        dz = jnp.where(vis, dz, 0.0)
                    dzb = dz.astype(BF16)
                    dk = dk + _nn(dzb, qm[pp][j])
                    dv = dv + _nn(a.astype(BF16), dom[pp][j])
                    dq = dq + _tn(dzb, km[j])
                    out += [_last_row(left), _last_row(before) + _last_row(e)]
                dq_ref[:, sl] += dq
                dk_ref[pl.ds(k0, TK), sl] += dk
                dv_ref[pl.ds(k0, TK), sl] += dv
            return tuple(out)

        carry = lax.fori_loop(0, qi, lambda i, c: step(i, c, False), tuple(start))
        step(qi, carry, True)
        dq_ref[...] *= SB_SCALE

    W = ATT_PAIRS * LANES
    qspec = pl.BlockSpec((TQ, W), lambda b, h, i: (b * nq + i, h))
    sspec = pl.BlockSpec((S, W), lambda b, h, i: (b, h))
    outs, moved = _call(
        body, ex, name="sb_bwd", grid=(B, HEADS // 2 // ATT_PAIRS, nq),
        in_specs=[qspec,
                  pl.BlockSpec((S, W), lambda b, h, i: (b, ATT_W // W + h)),
                  pl.BlockSpec((S, W), lambda b, h, i: (b, 2 * ATT_W // W + h)),
                  qspec, qspec],
        out_specs=[qspec, sspec, sspec],
        out_shape=[jax.ShapeDtypeStruct((T, ATT_W), F32)] * 3, args=(p, p, p, lt, do))
    return tuple(outs) if ex is None else tuple(outs) + (moved,)


ALL_PAIRS = [slice(i * LANES, (i + 1) * LANES) for i in range(HEADS // 2)]


def _rope_masks(hp):
    grp = _iota2((1, LANES), 1) // ROPE_DIM
    return [grp == ((2 * hp + j) % 4) for j in range(2)]


def _mla_queries(qn_ref, qr_ref, masks):
    out = []
    for pp, sl in enumerate(ALL_PAIRS):
        qnv = qn_ref[:, sl]
        qrv = qr_ref[:, ALL_PAIRS[pp // 2]]
        rmasks = _rope_masks(pp)
        out.append([jnp.concatenate([jnp.where(masks[j], qnv, 0), jnp.where(rmasks[j], qrv, 0)], axis=1).astype(BF16)
                    for j in range(2)])
    return out


def mla_fwd_call(qn, qr, kn, krt, vm, B, S):
    T = B * S
    TQ, TK = ATT_TQ, ATT_TK
    nq = S // TQ

    def body(qn_ref, qr_ref, kn_ref, kr_ref, v_ref, o_ref, lse_ref):
        qi = pl.program_id(1)
        masks = _head_masks()
        qcat = _mla_queries(qn_ref, qr_ref, masks)
        row = _iota2((TQ, TK), 0)
        col = _iota2((TQ, TK), 1)
        vis = col <= row
        o_ref[...] = jnp.zeros_like(o_ref)

        def step(kb, carry, diag):
            k0 = pl.multiple_of(kb * TK, TK)
            krv = kr_ref[pl.ds(k0, TK), :]
            out = []
            for pp, sl in enumerate(ALL_PAIRS):
                kcat = jnp.concatenate([kn_ref[pl.ds(k0, TK), sl], krv], axis=1)
                vmk = _pair(v_ref[pl.ds(k0, TK), sl], masks)
                acc = o_ref[:, sl]
                for j in range(2):
                    m_j, l_j = carry[4 * pp + 2 * j], carry[4 * pp + 2 * j + 1]
                    s = _nt(qcat[pp][j], kcat) * MLA_SCALE
                    if diag:
                        s = jnp.where(vis, s, NEG_BIG)
                    m_new = jnp.maximum(m_j, jnp.max(s, axis=1, keepdims=True))
                    alpha = jnp.exp(m_j - m_new)
                    pexp = jnp.exp(s - m_new)
                    out += [m_new, alpha * l_j + jnp.sum(pexp, axis=1, keepdims=True)]
                    acc = acc * jnp.where(masks[j], alpha, 1.0) + _nn(pexp.astype(BF16), vmk[j])
                o_ref[:, sl] = acc
            return tuple(out)

        neg = jnp.full((TQ, 1), NEG_BIG, F32)
        zero = jnp.zeros((TQ, 1), F32)
        carry = step(qi, (neg, zero) * (2 * len(ALL_PAIRS)), True)
        carry = lax.fori_loop(0, qi, lambda i, c: step(qi - 1 - i, c, False), carry)
        lane = _iota2((TQ, LANES), 1)
        for pp, sl in enumerate(ALL_PAIRS):
            m0, l0, m1, l1 = carry[4 * pp:4 * pp + 4]
            o_ref[:, sl] = o_ref[:, sl] * jnp.where(masks[0], 1.0 / l0, 1.0 / l1)
            lse_ref[:, sl] = jnp.where(lane == 0, m0 + jnp.log(l0), jnp.where(lane == 1, m1 + jnp.log(l1), 0.0))

    def rows(w):
        return pl.BlockSpec((TQ, w), lambda b, i: (b * nq + i, 0))

    def seq(w):
        return pl.BlockSpec((S, w), lambda b, i: (b, 0))

    return pl.pallas_call(
        body, name="mla_fwd", grid=(B, nq),
        in_specs=[rows(ATT_W), rows(ROPE_W), seq(ATT_W), seq(LANES), seq(ATT_W)],
        out_specs=[rows(ATT_W), rows(ATT_W)],
        out_shape=[jax.ShapeDtypeStruct((T, ATT_W), F32)] * 2,
        compiler_params=_params(("arbitrary", "arbitrary")),
    )(qn, qr, kn, krt, vm)


def mla_bwd_call(qn, qr, kn, krt, vm, o, lse, do, B, S, ex=None):
    T = B * S
    TQ, TK = ATT_TQ, ATT_TK
    nq = S // TQ

    def body(qn_ref, qr_ref, kn_ref, kr_ref, v_ref, o_ref, lse_ref, do_ref,
             dqn_ref, dqr_ref, dkn_ref, dv_ref, dkr_ref):
        qi = pl.program_id(1)

        @pl.when(qi == 0)
        def _():
            dkn_ref[...] = jnp.zeros_like(dkn_ref)
            dv_ref[...] = jnp.zeros_like(dv_ref)
            dkr_ref[...] = jnp.zeros_like(dkr_ref)

        masks = _head_masks()
        qcat = _mla_queries(qn_ref, qr_ref, masks)
        dom, dsum, lse = [], [], []
        for sl in ALL_PAIRS:
            do = do_ref[:, sl]
            dom.append(_pair(do, masks))
            d8 = _rows8(_lane_selector(HEAD_DIM), do * o_ref[:, sl])
            l8 = _rows8(_lane_selector(1), lse_ref[:, sl])
            dsum.append([_row_of(d8, j) for j in range(2)])
            lse.append([_row_of(l8, j) for j in range(2)])
        row = _iota2((TK, TQ), 0)
        col = _iota2((TK, TQ), 1)
        vis = row <= col
        dqn_ref[...] = jnp.zeros_like(dqn_ref)
        dqr_ref[...] = jnp.zeros_like(dqr_ref)

        def step(kb, diag):
            k0 = pl.multiple_of(kb * TK, TK)
            krv = kr_ref[pl.ds(k0, TK), :]
            for pp, sl in enumerate(ALL_PAIRS):
                rmasks = _rope_masks(pp)
                knv = kn_ref[pl.ds(k0, TK), sl]
                kcat = jnp.concatenate([knv, krv], axis=1)
                v = v_ref[pl.ds(k0, TK), sl]
                dq = jnp.zeros((TQ, 2 * LANES), F32)
                dk = jnp.zeros((TK, 2 * LANES), F32)
                dv = jnp.zeros((TK, LANES), F32)
                for j in range(2):
                    s = _nt(kcat, qcat[pp][j]) * MLA_SCALE
                    pr = jnp.exp(s - lse[pp][j])
                    if diag:
                        pr = jnp.where(vis, pr, 0.0)
                    ds = (pr * (_nt(v, dom[pp][j]) - dsum[pp][j]) * MLA_SCALE).astype(BF16)
                    dv = dv + _nn(pr.astype(BF16), dom[pp][j])
                    dk = dk + _nn(ds, qcat[pp][j])
                    kcat_j = jnp.concatenate([jnp.where(masks[j], knv, 0), jnp.where(rmasks[j], krv, 0)], axis=1)
                    dq = dq + _tn(ds, kcat_j.astype(BF16))
                dqn_ref[:, sl] += dq[:, :LANES]
                dqr_ref[:, sl] += dq[:, LANES:]
                dkn_ref[pl.ds(k0, TK), sl] += dk[:, :LANES]
                dkr_ref[pl.ds(k0, TK), sl] += dk[:, LANES:]
                dv_ref[pl.ds(k0, TK), sl] += dv

        step(qi, True)

        def loop(i, c):
            step(qi - 1 - i, False)
            return c

        lax.fori_loop(0, qi, loop, 0)

    def rows(w):
        return pl.BlockSpec((TQ, w), lambda b, i: (b * nq + i, 0))

    def seq(w):
        return pl.BlockSpec((S, w), lambda b, i: (b, 0))

    outs, moved = _call(
        body, ex, name="mla_bwd", grid=(B, nq),
        in_specs=[rows(ATT_W), rows(ROPE_W), seq(ATT_W), seq(LANES), seq(ATT_W), rows(ATT_W), rows(ATT_W), rows(ATT_W)],
        out_specs=[rows(ATT_W), rows(ATT_W), seq(ATT_W), seq(ATT_W), seq(ATT_W)],
        out_shape=[jax.ShapeDtypeStruct((T, ATT_W), F32)] * 5, args=(qn, qr, kn, krt, vm, o, lse, do))
    return tuple(outs) if ex is None else tuple(outs) + (moved,)


CONV_TC = 256


def _shift_down(x, n):
    return jnp.where(_iota2(x.shape, 0) >= n, pltpu.roll(x, n, 0), 0.0)


def _shift_up(x, n):
    rows = x.shape[0]
    return jnp.where(_iota2(x.shape, 0) < rows - n, pltpu.roll(x, rows - n, 0), 0.0)


def _taps(w_ref):
    return [w_ref[k:k + 1, :] for k in range(3)]


def _conv3(u, w, b):
    return w[0] * _shift_down(u, 2) + w[1] * _shift_down(u, 1) + w[2] * u + b


def conv_act_fwd_call(u, conv_w, conv_b, B, S):
    T = B * S
    nc = D_FF // CONV_TC

    def body(ug_ref, uv_ref, wg_ref, wv_ref, bg_ref, bv_ref, a_ref):
        gate = _conv3(ug_ref[...], _taps(wg_ref), bg_ref[...])
        val = _conv3(uv_ref[...], _taps(wv_ref), bv_ref[...])
        a_ref[...] =(gate * (1.0 / (1.0 + jnp.exp(-gate))) * val).astype(BF16)

    def blk(rows, off):
        return pl.BlockSpec((rows, CONV_TC), lambda b, j: (b if rows == S else 0, off + j))

    return pl.pallas_call(
        body, name="conv_act_fwd", grid=(B, nc),
        in_specs=[blk(S, 0), blk(S, nc), blk(3, 0), blk(3, nc), blk(1, 0), blk(1, nc)],
        out_specs=blk(S, 0),
        out_shape=jax.ShapeDtypeStruct((T, D_FF), BF16),
        compiler_params=_params(("parallel", "parallel")),
    )(u, u, conv_w, conv_w, conv_b, conv_b)


def conv_act_bwd_call(u, da, conv_w, conv_b, B, S):
    T = B * S
    nc = D_FF // CONV_TC

    def body(ug_ref, uv_ref, da_ref, wg_ref, wv_ref, bg_ref, bv_ref,
             dug_ref, duv_ref, dwg_ref, dwv_ref, dbg_ref, dbv_ref):
        @pl.when(pl.program_id(1) == 0)
        def _():
            for r in (dwg_ref, dwv_ref, dbg_ref, dbv_ref):
                r[...] = jnp.zeros_like(r)

        ug = ug_ref[...]
        uv = uv_ref[...]
        wg = _taps(wg_ref)
        wv = _taps(wv_ref)
        gate = _conv3(ug, wg, bg_ref[...])
        val = _conv3(uv, wv, bv_ref[...])
        da = da_ref[...]
        sig = 1.0 / (1.0 + jnp.exp(-gate))
        dval = da * (gate * sig)
        dgate = da * val * (sig * (1.0 + gate * (1.0 - sig)))
        for u_, d, w, du_ref, dw_ref, db_ref in ((ug, dgate, wg, dug_ref, dwg_ref, dbg_ref),
                                                 (uv, dval, wv, duv_ref, dwv_ref, dbv_ref)):
            du_ref[...] = (w[2] * d + w[1] * _shift_up(d, 1) + w[0] * _shift_up(d, 2)).astype(BF16)
            db_ref[...] += jnp.sum(d, axis=0, keepdims=True)
            dw_ref[0:1, :] += jnp.sum(d * _shift_down(u_, 2), axis=0, keepdims=True)
            dw_ref[1:2, :] += jnp.sum(d * _shift_down(u_, 1), axis=0, keepdims=True)
            dw_ref[2:3, :] += jnp.sum(d * u_, axis=0, keepdims=True)

    def blk(rows, off):
        return pl.BlockSpec((rows, CONV_TC), lambda j, b: (b if rows == S else 0, off + j))

    return pl.pallas_call(
        body, name="conv_act_bwd", grid=(nc, B),
        in_specs=[blk(S, 0), blk(S, nc), blk(S, 0), blk(3, 0), blk(3, nc), blk(1, 0), blk(1, nc)],
        out_specs=[blk(S, 0), blk(S, 0), blk(3, 0), blk(3, 0), blk(1, 0), blk(1, 0)],
        out_shape=[jax.ShapeDtypeStruct((T, D_FF), BF16), jax.ShapeDtypeStruct((T, D_FF), BF16),
                   jax.ShapeDtypeStruct((3, D_FF), F32), jax.ShapeDtypeStruct((3, D_FF), F32),
                   jax.ShapeDtypeStruct((1, D_FF), F32), jax.ShapeDtypeStruct((1, D_FF), F32)],
        compiler_params=_params(("parallel", "arbitrary")),
    )(u, u, da, conv_w, conv_w, conv_b, conv_b)


CHIP_MASKS = ((1, 0), (0, 1), (1, 1))


def _place():
    return lax.axis_index("x"), lax.axis_index("y"), lax.axis_index("c")


HALF_ALIGN = 32


def _any_specs(n):
    return [pl.BlockSpec(memory_space=pl.ANY)] * n


def _half_rows(r, half):
    return pl.ds(pl.multiple_of(half * (r // 2), HALF_ALIGN // 2), r // 2)


def _remote(src, dst, send_sem, recv_sem, device):
    return pltpu.make_async_remote_copy(src_ref=src, dst_ref=dst, send_sem=send_sem, recv_sem=recv_sem,
                                        device_id=device, device_id_type=MESH)


class Exchange:
    def __init__(self, ins, out_shape, sems, start, finish):
        self.ins, self.out_shape, self.sems, self.start, self.finish = list(ins), list(out_shape), list(sems), start, finish


def gather_group(shards):
    n = len(shards)
    split = [s.shape[0] % HALF_ALIGN == 0 for s in shards]

    def rows(w, half):
        return _half_rows(shards[w].shape[0], half) if split[w] else slice(None)

    def copies(ins, outs, sems):
        ici_s, ici_r, _, _, local_sems = sems
        x, y, c = _place()
        chip = 2 * x + y
        local = [pltpu.make_async_copy(ins[w], outs[w].at[chip], local_sems.at[w]) for w in range(n)]
        sends = [_remote(ins[w].at[rows(w, c)], outs[w].at[chip, rows(w, c)], ici_s.at[w, k], ici_r.at[w, k],
                         (x ^ fx, y ^ fy, c))
                 for w in range(n) for k, (fx, fy) in enumerate(CHIP_MASKS)]
        return local, sends

    def start(ins, outs, sems):
        local, sends = copies(ins, outs, sems)
        for cp in local + sends:
            cp.start()

    def finish(ins, outs, sems):
        ici_s, ici_r, d2d_s, d2d_r, _ = sems
        x, y, c = _place()
        sib = (x, y, 1 - c)
        local, sends = copies(ins, outs, sems)
        for w in range(n):
            for k, (fx, fy) in enumerate(CHIP_MASKS):
                landed = outs[w].at[2 * (x ^ fx) + (y ^ fy), rows(w, c)]
                _remote(landed, landed, ici_s.at[w, k], ici_r.at[w, k], sib).wait_recv()
                if split[w]:
                    cp = _remote(landed, landed, d2d_s.at[w, k], d2d_r.at[w, k], sib)
                    cp.start()
                    sends.append(cp)
        for w in range(n):
            for k, (fx, fy) in enumerate(CHIP_MASKS):
                if split[w]:
                    other = outs[w].at[2 * (x ^ fx) + (y ^ fy), rows(w, 1 - c)]
                    _remote(other, other, d2d_s.at[w, k], d2d_r.at[w, k], sib).wait_recv()
        for cp in sends:
            cp.wait_send()
        for cp in local:
            cp.wait()

    sems = pltpu.SemaphoreType.DMA((n, 3))
    return Exchange(shards, [jax.ShapeDtypeStruct((N_CHIPS,) + s.shape, s.dtype) for s in shards],
                    [sems, sems, sems, sems, pltpu.SemaphoreType.DMA((n,))], start, finish)


def swap_half(parts):
    n = len(parts)

    def copies(ins, outs, sems):
        x, y, c = _place()
        return [_remote(ins[w].at[:, _half_rows(parts[w].shape[1], 1 - c)], outs[w], sems[0].at[w], sems[1].at[w],
                        (x, y, 1 - c)) for w in range(n)]

    def start(ins, outs, sems):
        for cp in copies(ins, outs, sems):
            cp.start()

    def finish(ins, outs, sems):
        for cp in copies(ins, outs, sems):
            cp.wait_recv()
            cp.wait_send()

    return Exchange(parts, [jax.ShapeDtypeStruct((N_CHIPS, p.shape[1] // 2, p.shape[2]), F32) for p in parts],
                    [pltpu.SemaphoreType.DMA((n,))] * 2, start, finish)


def scatter_half(halves):
    n = len(halves)

    def copies(ins, outs, sems):
        x, y, c = _place()
        return [_remote(ins[w].at[2 * (x ^ fx) + (y ^ fy)], outs[w].at[k], sems[0].at[w, k], sems[1].at[w, k],
                        (x ^ fx, y ^ fy, c))
                for w in range(n) for k, (fx, fy) in enumerate(CHIP_MASKS)]

    def start(ins, outs, sems):
        for cp in copies(ins, outs, sems):
            cp.start()

    def finish(ins, outs, sems):
        for cp in copies(ins, outs, sems):
            cp.wait_recv()
            cp.wait_send()

    return Exchange(halves, [jax.ShapeDtypeStruct((3,) + h.shape[1:], F32) for h in halves],
                    [pltpu.SemaphoreType.DMA((n, 3))] * 2, start, finish)


def swap_final(finals):
    n = len(finals)

    def copies(ins, outs, sems):
        x, y, c = _place()
        mine = [outs[w].at[_half_rows(2 * finals[w].shape[0], c)] for w in range(n)]
        local = [pltpu.make_async_copy(ins[w], mine[w], sems[2].at[w]) for w in range(n)]
        sends = [_remote(ins[w], mine[w], sems[0].at[w], sems[1].at[w], (x, y, 1 - c)) for w in range(n)]
        return local, sends

    def start(ins, outs, sems):
        local, sends = copies(ins, outs, sems)
        for cp in local + sends:
            cp.start()

    def finish(ins, outs, sems):
        x, y, c = _place()
        local, sends = copies(ins, outs, sems)
        for w in range(n):
            got = outs[w].at[_half_rows(2 * finals[w].shape[0], 1 - c)]
            _remote(got, got, sems[0].at[w], sems[1].at[w], (x, y, 1 - c)).wait_recv()
        for cp in sends:
            cp.wait_send()
        for cp in local:
            cp.wait()

    return Exchange(finals, [jax.ShapeDtypeStruct((2 * f.shape[0], f.shape[1]), F32) for f in finals],
                    [pltpu.SemaphoreType.DMA((n,))] * 3, start, finish)


def exchange_call(name, ex):
    n, m = len(ex.ins), len(ex.out_shape)

    def body(*refs):
        ins, outs, sems = refs[:n], refs[n:n + m], refs[n + m:]
        ex.start(ins, outs, sems)
        ex.finish(ins, outs, sems)

    return pl.pallas_call(body, name=name, in_specs=_any_specs(n), out_specs=_any_specs(m), out_shape=ex.out_shape,
                          scratch_shapes=ex.sems, compiler_params=_params())(*ex.ins)


def _call(body, ex, *, name, grid, in_specs, out_specs, out_shape, args, scratch_shapes=()):
    sem = ("arbitrary",) * len(grid)
    if ex is None:
        outs = pl.pallas_call(body, name=name, grid=grid, in_specs=in_specs, out_specs=out_specs, out_shape=out_shape,
                              scratch_shapes=list(scratch_shapes), compiler_params=_params(sem))(*args)
        return outs, None
    ni, no, ns = len(in_specs), len(out_specs), len(scratch_shapes)
    ne, me = len(ex.ins), len(ex.out_shape)

    def wrapped(*refs):
        own_in, ex_in = refs[:ni], refs[ni:ni + ne]
        own_out, ex_out = refs[ni + ne:ni + ne + no], refs[ni + ne + no:ni + ne + no + me]
        own_scr, ex_sems = refs[ni + ne + no + me:ni + ne + no + me + ns], refs[ni + ne + no + me + ns:]
        ids = [pl.program_id(a) for a in range(len(grid))]
        first = functools.reduce(jnp.logical_and, [i == 0 for i in ids])
        last = functools.reduce(jnp.logical_and, [i == g - 1 for i, g in zip(ids, grid)])

        @pl.when(first)
        def _():
            ex.start(ex_in, ex_out, ex_sems)

        body(*own_in, *own_out, *own_scr)

        @pl.when(last)
        def _():
            ex.finish(ex_in, ex_out, ex_sems)

    outs = pl.pallas_call(
        wrapped, name=name, grid=grid, in_specs=list(in_specs) + _any_specs(ne),
        out_specs=list(out_specs) + _any_specs(me), out_shape=list(out_shape) + ex.out_shape,
        scratch_shapes=list(scratch_shapes) + ex.sems, compiler_params=_params(sem))(*args, *ex.ins)
    return outs[:no], outs[no:]


def _row_tile(rows, cap):
    return max(t for t in range(8, min(rows, cap) + 1, 8) if rows % t == 0)


def add_half_call(name, part, got, core):
    _, rh, cols = got.shape
    tr = _row_tile(rh, 128)
    nb = rh // tr

    def body(core_ref, p_ref, g_ref, o_ref):
        o_ref[...] = p_ref[...] + g_ref[...]

    blk = (N_CHIPS, tr, cols)
    return pl.pallas_call(
        body, name=name,
        grid_spec=pltpu.PrefetchScalarGridSpec(
            num_scalar_prefetch=1, grid=(nb,),
            in_specs=[pl.BlockSpec(blk, lambda i, core_ref: (0, core_ref[0] * nb + i, 0)),
                      pl.BlockSpec(blk, lambda i, core_ref: (0, i, 0))],
            out_specs=pl.BlockSpec(blk, lambda i, core_ref: (0, i, 0))),
        out_shape=jax.ShapeDtypeStruct(got.shape, F32),
        compiler_params=_params(("parallel",)),
    )(core, part, got)


def sum_chips_call(name, half, got, chip):
    _, rh, cols = got.shape
    tr = _row_tile(rh, 128)

    def body(chip_ref, h_ref, g_ref, o_ref):
        o_ref[...] = ((h_ref[0] + g_ref[0]) + g_ref[1]) + g_ref[2]

    return pl.pallas_call(
        body, name=name,
        grid_spec=pltpu.PrefetchScalarGridSpec(
            num_scalar_prefetch=1, grid=(rh // tr,),
            in_specs=[pl.BlockSpec((1, tr, cols), lambda i, chip_ref: (chip_ref[0], i, 0)),
                      pl.BlockSpec((3, tr, cols), lambda i, chip_ref: (0, i, 0))],
            out_specs=pl.BlockSpec((tr, cols), lambda i, chip_ref: (i, 0))),
        out_shape=jax.ShapeDtypeStruct((rh, cols), F32),
        compiler_params=_params(("parallel",)),
    )(chip, half, got)


def _adamw(w, g, m, v):
    m = ADAM_B1 * m + (1.0 - ADAM_B1) * g
    v = ADAM_B2 * v + (1.0 - ADAM_B2) * (g * g)
    m_hat = m / (1.0 - ADAM_B1 ** ADAM_STEP)
    v_hat = v / (1.0 - ADAM_B2 ** ADAM_STEP)
    delta = -ADAM_LR * (m_hat / (jnp.sqrt(v_hat) + ADAM_EPS) + ADAM_WD * w)
    return delta, m, v


def adamw_call(name, g, w, m, v):
    r, cols = w.shape
    tr = r if r % 8 else _row_tile(r, 256)

    def body(g_ref, w_ref, m_ref, v_ref, d_ref, nm_ref, nv_ref):
        d_ref[...], nm_ref[...], nv_ref[...] = _adamw(w_ref[...], g_ref[...], m_ref[...], v_ref[...])

    spec = pl.BlockSpec((tr, cols), lambda i: (i, 0))
    return pl.pallas_call(
        body, name=name, grid=(r // tr,), in_specs=[spec] * 4, out_specs=[spec] * 3,
        out_shape=[jax.ShapeDtypeStruct((r, cols), F32)] * 3,
        compiler_params=_params(("parallel",)),
    )(g, w, m, v)


def allsum_small_call(v):
    R = v.shape[0]

    def body(v_ref, out_ref, buf, send_sems, recv_sems):
        x, y, c = _place()
        me = 4 * x + 2 * y + c
        buf[me] = v_ref[...]
        sends = []
        for k in range(1, N_DEV):
            fx, fy, fc = (k >> 2) & 1, (k >> 1) & 1, k & 1
            cp = pltpu.make_async_remote_copy(
                src_ref=v_ref, dst_ref=buf.at[me], send_sem=send_sems.at[k - 1], recv_sem=recv_sems.at[k - 1],
                device_id=(x ^ fx, y ^ fy, c ^ fc), device_id_type=MESH)
            cp.start()
            sends.append(cp)
        for k in range(1, N_DEV):
            pltpu.make_async_remote_copy(
                src_ref=v_ref, dst_ref=buf.at[me ^ k], send_sem=send_sems.at[k - 1], recv_sem=recv_sems.at[k - 1],
                device_id=(x, y, c), device_id_type=MESH).wait_recv()
        acc = buf[0]
        for d in range(1, N_DEV):
            acc = acc + buf[d]
        out_ref[...] = acc
        for cp in sends:
            cp.wait_send()

    vm = pl.BlockSpec(memory_space=pltpu.VMEM)
    return pl.pallas_call(
        body, name="allsum_small", in_specs=[vm], out_specs=vm,
        out_shape=jax.ShapeDtypeStruct((R, LANES), F32),
        scratch_shapes=[pltpu.VMEM((N_DEV, R, LANES), F32), pltpu.SemaphoreType.DMA((N_DEV - 1,)),
                        pltpu.SemaphoreType.DMA((N_DEV - 1,))],
        compiler_params=_params(),
    )(v)


def _slab(flat, mult):
    n = flat.shape[-1]
    rows = -(-n // (LANES * mult)) * mult
    flat = jnp.pad(flat, [(0, 0)] * (flat.ndim - 1) + [(0, rows * LANES - n)])
    return flat.reshape(flat.shape[:-1] + (rows, LANES))


def full_from_chips(blocks, by_col):
    _, r, c = blocks.shape
    return blocks.transpose(1, 0, 2).reshape(r, N_CHIPS * c) if by_col else blocks.reshape(N_CHIPS * r, c)


def chips_from_full(full, by_col):
    if by_col:
        r, c = full.shape[0], full.shape[1] // N_CHIPS
        return full.reshape(r, N_CHIPS, c).transpose(1, 0, 2)
    return full.reshape(N_CHIPS, full.shape[0] // N_CHIPS, full.shape[1])


SMALL_PACK = SMALL_W + ("loss", "conv_w")
SMALL_PACK_N = {**SMALL_N, "loss": 1, "conv_w": 3 * 2 * D_FF}


def pack_small(vals):
    zero = jnp.zeros((1,), F32)
    return _slab(jnp.concatenate([vals[n].reshape(-1) if n in vals else jnp.tile(zero, SMALL_PACK_N[n])
                                  for n in SMALL_PACK]), 8)


def unpack_small(slab, shapes):
    flat = slab.reshape(-1)
    out, off = {}, 0
    for n in SMALL_PACK:
        out[n] = flat[off:off + SMALL_PACK_N[n]].reshape(shapes[n])
        off += SMALL_PACK_N[n]
    return out


def _split_heads(w, a, b):
    r = w.shape[0]
    w3 = w.reshape(r, HEADS, a + b)
    return w3[:, :, :a].reshape(r, HEADS * a), w3[:, :, a:].reshape(r, HEADS * b)


def _merge_heads(wa, wb, a, b):
    r = wa.shape[0]
    return jnp.concatenate([wa.reshape(r, HEADS, a), wb.reshape(r, HEADS, b)], axis=2).reshape(r, HEADS * (a + b))


def kernel(x, positions, g_mix, w_in, g_cq, w_uq, g_ckv, w_ukv, g_sb_out, g_mla_out, w_out, g_ffn, w_up, conv_w, conv_b, w_down, g_final, loss_target, m_g_mix, m_w_in, m_g_cq, m_w_uq, m_g_ckv, m_w_ukv, m_g_sb_out, m_g_mla_out, m_w_out, m_g_ffn, m_w_up, m_conv_w, m_conv_b, m_w_down, m_g_final, v_g_mix, v_w_in, v_g_cq, v_w_uq, v_g_ckv, v_w_ukv, v_g_sb_out, v_g_mla_out, v_w_out, v_g_ffn, v_w_up, v_conv_w, v_conv_b, v_w_down, v_g_final):
    given = dict(locals())
    B, S, _ = x.shape
    T = B * S
    w_big = {n: given[n][0] for n in BIG_W}
    m_big = {n: given["m_" + n][0] for n in BIG_W}
    v_big = {n: given["v_" + n][0] for n in BIG_W}

    first = ("w_in", "w_uq", "w_ukv")
    later = ("w_out", "w_up", "w_down", "conv_w")
    got_w = exchange_call("gather_first", gather_group([w_big[n].astype(BF16) for n in first]))
    full = {n: full_from_chips(g_, BIG_SHARD[n][2]) for n, g_ in zip(first, got_w)}
    gather_later = gather_group([w_big[n] if n == "conv_w" else w_big[n].astype(BF16) for n in later])
    w_in_p = jnp.pad(full["w_in"], ((0, 0), (0, IN_COLS_PAD - IN_COLS)))
    w_uq_p = jnp.concatenate(_split_heads(full["w_uq"], HEAD_DIM, ROPE_DIM), axis=1)
    w_ukv_p = jnp.concatenate(_split_heads(full["w_ukv"], HEAD_DIM, HEAD_DIM), axis=1)

    x2d = x.reshape(T, D_MODEL)
    half = ROPE_DIM // 2
    inv_freq = 1.0 / (ROPE_BASE ** (jnp.arange(half, dtype=F32) * (2.0 / ROPE_DIM)))
    cos, sin = rope_tab_call(positions.reshape(T, 1), jnp.tile(inv_freq, LANES // half).reshape(1, LANES))
    h = rmsnorm_fwd_call("norm_mix", x2d, g_mix)
    p = matmul_call("proj_in", h, w_in_p, "nn", tn=IN_COLS_PAD // 2)
    qn, qr, kn, vm, krt, cqn, ckvn = mla_prep_fwd_call(p, cos, sin, g_cq, g_ckv, w_uq_p, w_ukv_p)
    o_sb, lt_sb, got_w = sb_fwd_call(p, B, S, ex=gather_later)
    full.update({n: full_from_chips(g_, BIG_SHARD[n][2]) for n, g_ in zip(later, got_w)})
    conv_w_full = full["conv_w"]
    o_mla, lse = mla_fwd_call(qn, qr, kn, krt, vm, B, S)
    o_cat = outnorm_fwd_call(o_sb, o_mla, g_sb_out, g_mla_out)
    x1 = matmul_call("proj_out", o_cat, full["w_out"], "nn", res=x2d)
    hn = rmsnorm_fwd_call("norm_ffn", x1, g_ffn)
    u = matmul_call("ffn_up", hn, full["w_up"], "nn", tn=2 * D_FF // 4)
    act = conv_act_fwd_call(u, conv_w_full, conv_b, B, S)
    x2 = matmul_call("ffn_down", act, full["w_down"], "nn", res=x1)
    dx2, loss_row, dg_final = final_loss_call(x2, g_final.reshape(1, D_MODEL), loss_target.reshape(T, D_MODEL))

    xi, yi, ci = _place()
    chip = (2 * xi + yi).astype(jnp.int32).reshape(1)
    core = ci.astype(jnp.int32).reshape(1)

    def add_halves(names, parts, sib_rows):
        return [add_half_call("add_half_" + n, p_, s_, core) for n, p_, s_ in zip(names, parts, sib_rows)]

    def sum_chips(names, halves, from_chips):
        return [sum_chips_call("sum_chips_" + n, h_, f_, chip) for n, h_, f_ in zip(names, halves, from_chips)]

    ffn_w = ("w_down", "w_up")
    parts_ffn = [chips_from_full(wgrad_call("wgrad_down", act, dx2, tn=512), False)]
    da = matmul_call("ffn_down_bwd", dx2, full["w_down"], "nt", tn=D_FF // 2)
    du_g, du_v, dcw_g, dcw_v, dcb_g, dcb_v = conv_act_bwd_call(u, da, conv_w_full, conv_b, B, S)
    du = jnp.concatenate([du_g, du_v], axis=1)
    parts_ffn.append(wgrad_call("wgrad_up", hn, du, tn=2 * D_FF // 4, by_chip=True))
    dhn, sib_ffn = matmul_call("ffn_up_bwd", du, full["w_up"], "nt", tn=512, ex=swap_half(parts_ffn))
    dx1, dg_ffn = rmsnorm_bwd_call("norm_ffn_bwd", x1, g_ffn, dhn, dx2)
    parts_out = [chips_from_full(wgrad_call("wgrad_out", o_cat, dx1), False)]
    do_cat = matmul_call("proj_out_bwd", dx1, full["w_out"], "nt")
    do_sb, do_mla, dg_sb_out, dg_mla_out, sib_out = outnorm_bwd_call(
        o_sb, o_mla, g_sb_out, g_mla_out, do_cat, ex=swap_half(parts_out))
    early = ffn_w + ("w_out",)
    halves = add_halves(early, parts_ffn + parts_out, list(sib_ffn) + list(sib_out))
    dq_sb, dk_sb, dv_sb, from_chips = sb_bwd_call(p, lt_sb, do_sb, B, S, ex=scatter_half(halves))
    finals = sum_chips(early, halves, from_chips)
    dqn, dqr4, dkn, dvm, dkrt4, done = mla_bwd_call(qn, qr, kn, krt, vm, o_mla, lse, do_mla, B, S, ex=swap_final(finals))
    grads = dict(zip(early, done))
    dcq, dckvr, dq_cat, dkv_cat, dg_cq, dg_ckv = mla_prep_bwd_call(
        p, cos, sin, g_cq, g_ckv, w_uq_p, w_ukv_p, dqn, dqr4, dkn, dvm, dkrt4)
    dw_uq_p = wgrad_call("wgrad_uq", cqn, dq_cat)
    dw_ukv_p = wgrad_call("wgrad_ukv", ckvn, dkv_cat)
    dp = jnp.concatenate([dq_sb, dk_sb, dv_sb, dcq, dckvr], axis=1)
    late = ("w_uq", "w_ukv", "w_in")
    parts_late = [chips_from_full(g_, True) for g_ in (
        _merge_heads(dw_uq_p[:, :ATT_W], dw_uq_p[:, ATT_W:], HEAD_DIM, ROPE_DIM),
        _merge_heads(dw_ukv_p[:, :ATT_W], dw_ukv_p[:, ATT_W:], HEAD_DIM, HEAD_DIM),
        wgrad_call("wgrad_in", h, dp, tn=IN_COLS_PAD // 2)[:, :IN_COLS])]
    dh, sib_late = matmul_call("proj_in_bwd", dp, w_in_p, "nt", ex=swap_half(parts_late))
    halves = add_halves(late, parts_late, sib_late)
    grad_x, dg_mix, from_chips = rmsnorm_bwd_call("norm_mix_bwd", x2d, g_mix, dh, dx1, ex=scatter_half(halves))
    finals = sum_chips(late, halves, from_chips)
    grads.update(zip(late, exchange_call("swap_final_late", swap_final(finals))))

    shapes = {n: given[n].shape for n in SMALL_W}
    shapes.update(loss=(), conv_w=(3, 2 * D_FF))
    small_g = {"g_mix": dg_mix, "g_cq": dg_cq, "g_ckv": dg_ckv, "g_sb_out": dg_sb_out, "g_mla_out": dg_mla_out,
               "g_ffn": dg_ffn, "conv_b": jnp.concatenate([dcb_g, dcb_v], axis=1), "g_final": dg_final,
               "loss": loss_row[0, :1], "conv_w": jnp.concatenate([dcw_g, dcw_v], axis=1)}
    gs_slab = allsum_small_call(pack_small(small_g))
    small_in = [pack_small({n: given[pre + n] for n in SMALL_W}) for pre in ("", "m_", "v_")]
    small_out = [unpack_small(s, shapes) for s in (gs_slab,) + tuple(adamw_call("adamw_small", gs_slab, *small_in))]
    cw_cols = BIG_SHARD["conv_w"][1]
    grads["conv_w"] = lax.dynamic_slice_in_dim(small_out[0]["conv_w"], chip[0] * cw_cols, cw_cols, axis=1)

    big_out = {n: (grads[n],) + tuple(adamw_call("adamw_" + n, grads[n], w_big[n], m_big[n], v_big[n])) for n in BIG_W}
    weights = ("g_mix", "w_in", "g_cq", "w_uq", "g_ckv", "w_ukv", "g_sb_out", "g_mla_out", "w_out", "g_ffn",
               "w_up", "conv_w", "conv_b", "w_down", "g_final")
    outs = [small_out[0]["loss"], grad_x.reshape(B, S, D_MODEL)]
    for k in range(4):
        for n in weights:
            outs.append(big_out[n][k][None] if n in BIG_W else small_out[k][n])
    return tuple(outs)
```

```python
import functools

import jax
import jax.numpy as jnp
from jax import lax
from jax.experimental import pallas as pl
from jax.experimental.pallas import tpu as pltpu

F32 = jnp.float32
BF16 = jnp.bfloat16
MESH = pl.DeviceIdType.MESH

D_MODEL = 1024
HEADS = 8
HEAD_DIM = 64
ATT_W = HEADS * HEAD_DIM
ROPE_DIM = 32
ROPE_W = HEADS * ROPE_DIM
QK_DIM = HEAD_DIM + ROPE_DIM
Q_RANK = 384
KV_RANK = 256
D_FF = 2816
IN_COLS = 2208
IN_COLS_PAD = 2304
EPS = 1e-6
ROPE_BASE = 10000.0
SB_SCALE = HEAD_DIM ** -0.5
MLA_SCALE = QK_DIM ** -0.5
LANES = 128
N_CHIPS = 4
N_DEV = 8
VMEM_LIMIT = 48 * 1024 * 1024
ATT_TQ = 256
ATT_TK = 256
ATT_PAIRS = 4
PAIR_LANES = [slice(i * LANES, (i + 1) * LANES) for i in range(ATT_PAIRS)]
SB_BWD_GROUP = 2
SB_FWD_GROUP = 4
MLA_GROUP = 4
NEG_BIG = -1e30

ADAM_LR = 0.001
ADAM_B1 = 0.9
ADAM_B2 = 0.999
ADAM_EPS = 1e-08
ADAM_WD = 0.01
ADAM_STEP = 10

BIG_W = ("w_in", "w_uq", "w_ukv", "w_out", "w_up", "conv_w", "w_down")
BIG_SHARD = {
    "w_in": (D_MODEL, IN_COLS // 4, True),
    "w_uq": (Q_RANK, HEADS * QK_DIM // 4, True),
    "w_ukv": (KV_RANK, 2 * ATT_W // 4, True),
    "w_out": (2 * ATT_W // 4, D_MODEL, False),
    "w_up": (D_MODEL, 2 * D_FF // 4, True),
    "conv_w": (3, 2 * D_FF // 4, True),
    "w_down": (D_FF // 4, D_MODEL, False),
}
SMALL_W = ("g_mix", "g_cq", "g_ckv", "g_sb_out", "g_mla_out", "g_ffn", "conv_b", "g_final")
SMALL_N = {"g_mix": D_MODEL, "g_cq": Q_RANK, "g_ckv": KV_RANK, "g_sb_out": ATT_W, "g_mla_out": ATT_W,
           "g_ffn": D_MODEL, "conv_b": 2 * D_FF, "g_final": D_MODEL}


def _params(sem=None, **kw):
    return pltpu.CompilerParams(dimension_semantics=sem, vmem_limit_bytes=VMEM_LIMIT, **kw)


def _dot(a, b, dims):
    return lax.dot_general(a, b, (dims, ((), ())), preferred_element_type=F32)


def _nn(a, b):
    return _dot(a, b, ((1,), (0,)))


def _nt(a, b):
    return _dot(a, b, ((1,), (1,)))


def _tn(a, b):
    return _dot(a, b, ((0,), (0,)))


def _split2(x):
    hi = x.astype(BF16)
    lo = (x - hi.astype(F32)).astype(BF16)
    return hi, lo


def _split3(x):
    hi = x.astype(BF16)
    r1 = x - hi.astype(F32)
    mid = r1.astype(BF16)
    return hi, mid, (r1 - mid.astype(F32)).astype(BF16)


def _rms_r(x, d):
    return lax.rsqrt(jnp.sum(x * x, axis=-1, keepdims=True) * (1.0 / d) + EPS)


def _rms_bwd(x, g, dy, d):
    r = _rms_r(x, d)
    xhat = x * r
    gy = dy * g
    dx = r * (gy - xhat * (jnp.sum(xhat * gy, axis=-1, keepdims=True) * (1.0 / d)))
    return dx, jnp.sum(dy * xhat, axis=0, keepdims=True)


def _rot(x):
    lane = lax.broadcasted_iota(jnp.int32, x.shape, x.ndim - 1)
    n = x.shape[-1]
    return jnp.where((lane & 31) < 16, pltpu.roll(x, n - 16, x.ndim - 1), pltpu.roll(x, 16, x.ndim - 1))


def _fold4(x):
    return x + pltpu.roll(x, 32, 1) + pltpu.roll(x, 64, 1) + pltpu.roll(x, 96, 1)


def matmul_call(name, a, b, mode, out_dtype=F32, res=None, tm=512, tn=None, ex=None):
    M, K = a.shape
    N = b.shape[1] if mode == "nn" else b.shape[0]
    tn = N if tn is None else tn
    assert M % tm == 0 and N % tn == 0

    def body(*refs):
        if res is None:
            a_ref, b_ref, o_ref = refs
        else:
            a_ref, b_ref, r_ref, o_ref = refs
        av = a_ref[...].astype(BF16)
        bv = b_ref[...].astype(BF16)
        acc = _nn(av, bv) if mode == "nn" else _nt(av, bv)
        if res is not None:
            acc = r_ref[...] + acc
        o_ref[...] = acc.astype(out_dtype)

    in_specs = [pl.BlockSpec((tm, K), lambda j, i: (i, 0))]
    if mode == "nn":
        in_specs.append(pl.BlockSpec((K, tn), lambda j, i: (0, j)))
    else:
        in_specs.append(pl.BlockSpec((tn, K), lambda j, i: (j, 0)))
    args = [a, b]
    if res is not None:
        in_specs.append(pl.BlockSpec((tm, tn), lambda j, i: (i, j)))
        args.append(res)
    outs, moved = _call(body, ex, name=name, grid=(N // tn, M // tm), in_specs=in_specs,
                        out_specs=[pl.BlockSpec((tm, tn), lambda j, i: (i, j))],
                        out_shape=[jax.ShapeDtypeStruct((M, N), out_dtype)], args=args)
    return outs[0] if ex is None else (outs[0], moved)


def wgrad_call(name, a, b, tn=None, tt=512, by_chip=False):
    T, M = a.shape
    N = b.shape[1]
    tn = N if tn is None else tn
    assert T % tt == 0 and N % tn == 0
    if by_chip:
        out_spec = pl.BlockSpec((None, M, tn), lambda j, t: (j, 0, 0))
        out_shape = jax.ShapeDtypeStruct((N // tn, M, tn), F32)
    else:
        out_spec = pl.BlockSpec((M, tn), lambda j, t: (0, j))
        out_shape = jax.ShapeDtypeStruct((M, N), F32)

    def body(a_ref, b_ref, o_ref):
        @pl.when(pl.program_id(1) == 0)
        def _():
            o_ref[...] = jnp.zeros_like(o_ref)

        o_ref[...] += _tn(a_ref[...].astype(BF16), b_ref[...].astype(BF16))

    return pl.pallas_call(
        body, name=name, grid=(N // tn, T // tt),
        in_specs=[pl.BlockSpec((tt, M), lambda j, t: (t, 0)), pl.BlockSpec((tt, tn), lambda j, t: (t, j))],
        out_specs=out_spec, out_shape=out_shape,
        compiler_params=_params(("parallel", "arbitrary")),
    )(a, b)


def rmsnorm_fwd_call(name, x, g, tm=512):
    T, d = x.shape

    def body(x_ref, g_ref, o_ref):
        x = x_ref[...]
        o_ref[...] = ((x * _rms_r(x, d)) * g_ref[...]).astype(BF16)

    return pl.pallas_call(
        body, name=name, grid=(T // tm,),
        in_specs=[pl.BlockSpec((tm, d), lambda i: (i, 0)), pl.BlockSpec((1, d), lambda i: (0, 0))],
        out_specs=pl.BlockSpec((tm, d), lambda i: (i, 0)),
        out_shape=jax.ShapeDtypeStruct((T, d), BF16),
        compiler_params=_params(("parallel",)),
    )(x, g)


def rmsnorm_bwd_call(name, x, g, dy, res, tm=512, ex=None):
    T, d = x.shape

    def body(x_ref, g_ref, dy_ref, r_ref, dx_ref, dg_ref):
        @pl.when(pl.program_id(0) == 0)
        def _():
            dg_ref[...] = jnp.zeros_like(dg_ref)

        dx, dg = _rms_bwd(x_ref[...], g_ref[...], dy_ref[...], d)
        dx_ref[...] = r_ref[...] + dx
        dg_ref[...] += dg

    row = pl.BlockSpec((tm, d), lambda i: (i, 0))
    vec = pl.BlockSpec((1, d), lambda i: (0, 0))
    outs, moved = _call(body, ex, name=name, grid=(T // tm,), in_specs=[row, vec, row, row], out_specs=[row, vec],
                        out_shape=[jax.ShapeDtypeStruct((T, d), F32), jax.ShapeDtypeStruct((1, d), F32)],
                        args=(x, g, dy, res))
    return tuple(outs) if ex is None else tuple(outs) + (moved,)


def outnorm_fwd_call(o_sb, o_mla, g_sb, g_mla, tm=512):
    T = o_sb.shape[0]

    def body(a_ref, b_ref, ga_ref, gb_ref, o_ref):
        a = a_ref[...]
        b = b_ref[...]
        ya = (a * _rms_r(a, ATT_W)) * ga_ref[...]
        yb = (b * _rms_r(b, ATT_W)) * gb_ref[...]
        o_ref[...] = jnp.concatenate([ya, yb], axis=1).astype(BF16)

    row = pl.BlockSpec((tm, ATT_W), lambda i: (i, 0))
    vec = pl.BlockSpec((1, ATT_W), lambda i: (0, 0))
    return pl.pallas_call(
        body, name="outnorm_fwd", grid=(T // tm,), in_specs=[row, row, vec, vec],
        out_specs=pl.BlockSpec((tm, 2 * ATT_W), lambda i: (i, 0)),
        out_shape=jax.ShapeDtypeStruct((T, 2 * ATT_W), BF16),
        compiler_params=_params(("parallel",)),
    )(o_sb, o_mla, g_sb, g_mla)


def outnorm_bwd_call(o_sb, o_mla, g_sb, g_mla, do_cat, tm=512, ex=None):
    T = o_sb.shape[0]

    def body(a_ref, b_ref, ga_ref, gb_ref, d_ref, da_ref, db_ref, dga_ref, dgb_ref):
        @pl.when(pl.program_id(0) == 0)
        def _():
            dga_ref[...] = jnp.zeros_like(dga_ref)
            dgb_ref[...] = jnp.zeros_like(dgb_ref)

        d = d_ref[...]
        da, dga = _rms_bwd(a_ref[...], ga_ref[...], d[:, :ATT_W], ATT_W)
        db, dgb = _rms_bwd(b_ref[...], gb_ref[...], d[:, ATT_W:], ATT_W)
        da_ref[...] = da
        db_ref[...] = db
        dga_ref[...] += dga
        dgb_ref[...] += dgb

    row = pl.BlockSpec((tm, ATT_W), lambda i: (i, 0))
    vec = pl.BlockSpec((1, ATT_W), lambda i: (0, 0))
    outs, moved = _call(
        body, ex, name="outnorm_bwd", grid=(T // tm,),
        in_specs=[row, row, vec, vec, pl.BlockSpec((tm, 2 * ATT_W), lambda i: (i, 0))],
        out_specs=[row, row, vec, vec],
        out_shape=[jax.ShapeDtypeStruct((T, ATT_W), F32), jax.ShapeDtypeStruct((T, ATT_W), F32),
                   jax.ShapeDtypeStruct((1, ATT_W), F32), jax.ShapeDtypeStruct((1, ATT_W), F32)],
        args=(o_sb, o_mla, g_sb, g_mla, do_cat))
    return tuple(outs) if ex is None else tuple(outs) + (moved,)


def final_loss_call(x2, g, target, tm=512):
    T, d = x2.shape

    def body(x_ref, g_ref, t_ref, dx_ref, loss_ref, dg_ref):
        @pl.when(pl.program_id(0) == 0)
        def _():
            loss_ref[...] = jnp.zeros_like(loss_ref)
            dg_ref[...] = jnp.zeros_like(dg_ref)

        x = x_ref[...]
        g = g_ref[...]
        y = (x * _rms_r(x, d)) * g
        err = y - t_ref[...]
        loss_ref[...] += jnp.sum(jnp.sum(err * err, axis=1, keepdims=True), axis=0, keepdims=True) * (0.5 / d)
        dx, dg = _rms_bwd(x, g, err * (1.0 / d), d)
        dx_ref[...] = dx
        dg_ref[...] += dg

    row = pl.BlockSpec((tm, d), lambda i: (i, 0))
    vec = pl.BlockSpec((1, d), lambda i: (0, 0))
    return pl.pallas_call(
        body, name="final_loss", grid=(T // tm,), in_specs=[row, vec, row],
        out_specs=[row, pl.BlockSpec((1, LANES), lambda i: (0, 0)), vec],
        out_shape=[jax.ShapeDtypeStruct((T, d), F32), jax.ShapeDtypeStruct((1, LANES), F32),
                   jax.ShapeDtypeStruct((1, d), F32)],
        compiler_params=_params(("arbitrary",)),
    )(x2, g, target)


def rope_tab_call(pos, inv_freq, tm=512):
    T = pos.shape[0]

    def body(p_ref, f_ref, c_ref, s_ref):
        ang = p_ref[...].astype(F32) * f_ref[...]
        lane = lax.broadcasted_iota(jnp.int32, ang.shape, 1)
        sn = jnp.sin(ang)
        c_ref[...] = jnp.cos(ang)
        s_ref[...] = jnp.where((lane & 31) < 16, -sn, sn)

    row = pl.BlockSpec((tm, LANES), lambda i: (i, 0))
    return pl.pallas_call(
        body, name="rope_tab", grid=(T // tm,),
        in_specs=[pl.BlockSpec((tm, 1), lambda i: (i, 0)), pl.BlockSpec((1, LANES), lambda i: (0, 0))],
        out_specs=[row, row],
        out_shape=[jax.ShapeDtypeStruct((T, LANES), F32)] * 2,
        compiler_params=_params(("parallel",)),
    )(pos, inv_freq)


def mla_prep_fwd_call(p, cos, sin, g_cq, g_ckv, w_uq_p, w_ukv_p, tm=512):
    T = p.shape[0]

    def body(cq_ref, ckvr_ref, c_ref, s_ref, gq_ref, gkv_ref, wq_ref, wkv_ref,
             qn_ref, qr_ref, kn_ref, vm_ref, krt_ref, cqn_ref, ckvn_ref):
        c = c_ref[...]
        s = s_ref[...]
        cq = cq_ref[...]
        cqn = ((cq * _rms_r(cq, Q_RANK)) * gq_ref[...]).astype(BF16)
        cqn_ref[...] = cqn
        q = _nn(cqn, wq_ref[...])
        qn_ref[...] = q[:, :ATT_W].astype(BF16)
        for g in range(ROPE_W // LANES):
            qr = q[:, ATT_W + g * LANES:ATT_W + (g + 1) * LANES]
            qr_ref[:, g * LANES:(g + 1) * LANES] = (qr * c + _rot(qr) * s).astype(BF16)
        ckvr = ckvr_ref[...]
        ckv = ckvr[:, :KV_RANK]
        ckvn = ((ckv * _rms_r(ckv, KV_RANK)) * gkv_ref[...]).astype(BF16)
        ckvn_ref[...] = ckvn
        kv = _nn(ckvn, wkv_ref[...])
        kn_ref[...] = kv[:, :ATT_W].astype(BF16)
        vm_ref[...] = kv[:, ATT_W:].astype(BF16)
        kr = _fold4(ckvr[:, KV_RANK:])
        krt_ref[...] = (kr * c + _rot(kr) * s).astype(BF16)

    def row(w, j=0):
        return pl.BlockSpec((tm, w), lambda i: (i, j))

    def full(a):
        return pl.BlockSpec(a.shape, lambda i: (0, 0))

    return pl.pallas_call(
        body, name="mla_prep_fwd", grid=(T // tm,),
        in_specs=[row(Q_RANK, 4), row(Q_RANK, 5), row(LANES), row(LANES), full(g_cq), full(g_ckv),
                  full(w_uq_p), full(w_ukv_p)],
        out_specs=[row(ATT_W), row(ROPE_W), row(ATT_W), row(ATT_W), row(LANES), row(Q_RANK), row(KV_RANK)],
        out_shape=[jax.ShapeDtypeStruct((T, w), BF16) for w in (ATT_W, ROPE_W, ATT_W, ATT_W, LANES, Q_RANK, KV_RANK)],
        compiler_params=_params(("parallel",)),
    )(p, p, cos, sin, g_cq, g_ckv, w_uq_p, w_ukv_p)


def mla_prep_bwd_call(p, cos, sin, g_cq, g_ckv, w_uq_p, w_ukv_p, dqn, dqr4, dkn, dvm, dkrt4, tm=512):
    T = p.shape[0]

    def body(cq_ref, ckvr_ref, c_ref, s_ref, gq_ref, gkv_ref, wq_ref, wkv_ref,
             dqn_ref, dqr4_ref, dkn_ref, dvm_ref, dkrt4_ref,
             dcq_ref, dckvr_ref, dq_ref, dkv_ref, dgq_ref, dgkv_ref):
        @pl.when(pl.program_id(0) == 0)
        def _():
            dgq_ref[...] = jnp.zeros_like(dgq_ref)
            dgkv_ref[...] = jnp.zeros_like(dgkv_ref)

        c = c_ref[...]
        s = s_ref[...]
        d4 = dqr4_ref[...]
        dqr = [d4[:, :128] + d4[:, 128:256], d4[:, 256:384] + d4[:, 384:]]
        dqr = [t * c + _rot(t * s) for t in dqr]
        dq = jnp.concatenate([dqn_ref[...]] + dqr, axis=1).astype(BF16)
        dq_ref[...] = dq
        dcq, dgq = _rms_bwd(cq_ref[...], gq_ref[...], _nt(dq, wq_ref[...]), Q_RANK)
        dcq_ref[...] = dcq
        dgq_ref[...] += dgq
        dkv = jnp.concatenate([dkn_ref[...], dvm_ref[...]], axis=1).astype(BF16)
        dkv_ref[...] = dkv
        ckvr = ckvr_ref[...]
        dckv, dgkv = _rms_bwd(ckvr[:, :KV_RANK], gkv_ref[...], _nt(dkv, wkv_ref[...]), KV_RANK)
        dgkv_ref[...] += dgkv
        k4 = dkrt4_ref[...]
        dkr = _fold4(k4[:, :128] + k4[:, 128:256] + k4[:, 256:384] + k4[:, 384:])
        dkr = dkr * c + _rot(dkr * s)
        lane = lax.broadcasted_iota(jnp.int32, dkr.shape, 1)
        dckvr_ref[...] = jnp.concatenate([dckv, jnp.where(lane < ROPE_DIM, dkr, 0.0)], axis=1)

    def row(w, j=0):
        return pl.BlockSpec((tm, w), lambda i: (i, j))

    def full(a):
        return pl.BlockSpec(a.shape, lambda i: (0, 0))

    return pl.pallas_call(
        body, name="mla_prep_bwd", grid=(T // tm,),
        in_specs=[row(Q_RANK, 4), row(Q_RANK, 5), row(LANES), row(LANES), full(g_cq), full(g_ckv),
                  full(w_uq_p), full(w_ukv_p), row(ATT_W), row(ATT_W), row(ATT_W), row(ATT_W), row(ATT_W)],
        out_specs=[row(Q_RANK), row(Q_RANK), row(ATT_W + ROPE_W), row(2 * ATT_W),
                   pl.BlockSpec((1, Q_RANK), lambda i: (0, 0)), pl.BlockSpec((1, KV_RANK), lambda i: (0, 0))],
        out_shape=[jax.ShapeDtypeStruct((T, Q_RANK), F32), jax.ShapeDtypeStruct((T, Q_RANK), F32),
                   jax.ShapeDtypeStruct((T, ATT_W + ROPE_W), BF16), jax.ShapeDtypeStruct((T, 2 * ATT_W), BF16),
                   jax.ShapeDtypeStruct((1, Q_RANK), F32), jax.ShapeDtypeStruct((1, KV_RANK), F32)],
        compiler_params=_params(("arbitrary",)),
    )(p, p, cos, sin, g_cq, g_ckv, w_uq_p, w_ukv_p, dqn, dqr4, dkn, dvm, dkrt4)


def _iota2(shape, axis):
    return lax.broadcasted_iota(jnp.int32, shape, axis)


def _head_masks():
    lane = _iota2((1, LANES), 1)
    return lane < HEAD_DIM, lane >= HEAD_DIM


def _pair(x, masks, dtype=BF16):
    return [jnp.where(m, x, 0.0).astype(dtype) for m in masks]


def _log_gates(z):
    keep = jnp.maximum(z, 0.0) + jnp.log(1.0 + jnp.exp(-jnp.abs(z)))
    return z - keep, keep


def _last_row(x):
    return _row_of(x[x.shape[0] - 8:, :], 7)


def _lane_selector(group):
    return jnp.where(_iota2((16, LANES), 1) // group == _iota2((16, LANES), 0), 1.0, 0.0).astype(BF16)


def _rows8(sel_t, x):
    hi = x.astype(BF16)
    r1 = x - hi.astype(F32)
    mid = r1.astype(BF16)
    lo = (r1 - mid.astype(F32)).astype(BF16)
    return _nt(sel_t, hi) + _nt(sel_t, mid) + _nt(sel_t, lo)


def _row_of(x8, j):
    return jnp.sum(jnp.where(_iota2(x8.shape, 0) == j, x8, 0.0), axis=0, keepdims=True)


def sb_fwd_call(p, B, S, ex=None):
    T = B * S
    TQ, TK = ATT_TQ, ATT_TK
    nq = S // TQ

    def body(q_ref, k_ref, v_ref, o_ref, lt_ref):
        qi = pl.program_id(2)
        masks = _head_masks()
        qm = [_pair(q_ref[:, sl] * SB_SCALE, masks) for sl in PAIR_LANES]
        row = _iota2((TQ, TK), 0)
        col = _iota2((TQ, TK), 1)
        tri = jnp.where(row > col, 1.0, 0.0).astype(BF16)
        tri2 = jnp.concatenate([tri, tri], axis=0)
        vis = col < row
        o_ref[...] = jnp.zeros_like(o_ref)

        def group(k0, pairs, carry, diag):
            heads = [(pp, j) for pp in pairs for j in range(2)]
            n = range(len(heads))
            k = {pp: k_ref[pl.ds(k0, TK), PAIR_LANES[pp]].astype(BF16) for pp in pairs}
            vm = {pp: _pair(v_ref[pl.ds(k0, TK), PAIR_LANES[pp]], masks) for pp in pairs}
            gates = [_log_gates(_nt(qm[pp][j], k[pp])) for pp, j in heads]
            lb = [g[0] for g in gates]
            keep = [jnp.where(vis, g[1], 0.0) if diag else g[1] for g in gates]
            tail = [_nn(jnp.concatenate(_split2(keep[h]), axis=1), tri2) + carry[h] for h in n]
            a = [jnp.exp(lb[h] - tail[h]) for h in n]
            if diag:
                a = [jnp.where(vis, x, 0.0) for x in a]
            ab = [x.astype(BF16) for x in a]
            for i, pp in enumerate(pairs):
                o_ref[:, PAIR_LANES[pp]] += _nn(ab[2 * i], vm[pp][0]) + _nn(ab[2 * i + 1], vm[pp][1])
            return [carry[h] + jnp.sum(keep[h], axis=1, keepdims=True) for h in n]

        def step(kb, carry, diag):
            k0 = pl.multiple_of(kb * TK, TK)
            out = []
            for g in range(0, ATT_PAIRS, SB_FWD_GROUP):
                out += group(k0, list(range(g, g + SB_FWD_GROUP)), carry[2 * g:2 * (g + SB_FWD_GROUP)], diag)
            return tuple(out)

        zero = jnp.zeros((TQ, 1), F32)
        carry = step(qi, (zero,) * (2 * ATT_PAIRS), True)
        carry = lax.fori_loop(0, qi, lambda i, c: step(qi - 1 - i, c, False), carry)
        lane = _iota2((TQ, LANES), 1)
        for pp, sl in enumerate(PAIR_LANES):
            lt_ref[:, sl] = jnp.where(lane == 0, carry[2 * pp], jnp.where(lane == 1, carry[2 * pp + 1], 0.0))

    W = ATT_PAIRS * LANES
    qspec = pl.BlockSpec((TQ, W), lambda b, h, i: (b * nq + i, h))
    outs, moved = _call(
        body, ex, name="sb_fwd", grid=(B, HEADS // 2 // ATT_PAIRS, nq),
        in_specs=[qspec,
                  pl.BlockSpec((S, W), lambda b, h, i: (b, ATT_W // W + h)),
                  pl.BlockSpec((S, W), lambda b, h, i: (b, 2 * ATT_W // W + h))],
        out_specs=[qspec, qspec],
        out_shape=[jax.ShapeDtypeStruct((T, ATT_W), F32)] * 2, args=(p, p, p))
    return tuple(outs) if ex is None else tuple(outs) + (moved,)


def sb_bwd_call(p, lt, do, B, S, ex=None):
    T = B * S
    TQ, TK = ATT_TQ, ATT_TK
    nq = S // TQ

    def body(q_ref, k_ref, v_ref, lt_ref, do_ref, dq_ref, dk_ref, dv_ref):
        qi = pl.program_id(2)

        @pl.when(qi == 0)
        def _():
            dk_ref[...] = jnp.zeros_like(dk_ref)
            dv_ref[...] = jnp.zeros_like(dv_ref)

        masks = _head_masks()
        qm = [_pair(q_ref[:, sl] * SB_SCALE, masks) for sl in PAIR_LANES]
        dom = [_pair(do_ref[:, sl], masks) for sl in PAIR_LANES]
        start = []
        for sl in PAIR_LANES:
            l8 = _rows8(_lane_selector(1), lt_ref[:, sl])
            start += [-_row_of(l8, 0), jnp.zeros((1, TQ), F32), -_row_of(l8, 1), jnp.zeros((1, TQ), F32)]
        row = _iota2((TK, TQ), 0)
        col = _iota2((TK, TQ), 1)
        incl = jnp.where(col <= row, 1.0, 0.0).astype(BF16)
        incl2 = jnp.concatenate([incl, incl], axis=1)
        excl = jnp.where(col < row, 1.0, 0.0).astype(BF16)
        excl2 = jnp.concatenate([excl, excl], axis=1)
        vis = row < col
        dq_ref[...] = jnp.zeros_like(dq_ref)

        def group(k0, pairs, carry, diag):
            heads = [(pp, j) for pp in pairs for j in range(2)]
            n = range(len(heads))
            kf = {pp: k_ref[pl.ds(k0, TK), PAIR_LANES[pp]] for pp in pairs}
            km = {pp: _pair(kf[pp], masks) for pp in pairs}
            v = {pp: v_ref[pl.ds(k0, TK), PAIR_LANES[pp]].astype(BF16) for pp in pairs}
            z = [_nt(kf[pp].astype(BF16), qm[pp][j]) for pp, j in heads]
            da = [_nt(v[pp], dom[pp][j]) for pp, j in heads]
            gates = [_log_gates(x) for x in z]
            lb = [g[0] for g in gates]
            keep = [jnp.where(vis, g[1], 0.0) if diag else g[1] for g in gates]
            left = [_nn(incl2, jnp.concatenate(_split2(keep[h]), axis=0)) + carry[2 * h] for h in n]
            a = [jnp.exp(lb[h] + left[h]) for h in n]
            if diag:
                a = [jnp.where(vis, x, 0.0) for x in a]
            e = [a[h] * da[h] for h in n]
            before = [_nn(excl2, jnp.concatenate(_split2(e[h]), axis=0)) + carry[2 * h + 1] for h in n]
            dz = [e[h] - jnp.exp(lb[h]) * (e[h] + before[h]) for h in n]
            if diag:
                dz = [jnp.where(vis, x, 0.0) for x in dz]
            dzb = [x.astype(BF16) for x in dz]
            ab = [x.astype(BF16) for x in a]
            out = []
            for h in n:
                out += [_last_row(left[h]), _last_row(before[h]) + _last_row(e[h])]
            for i, pp in enumerate(pairs):
                sl = PAIR_LANES[pp]
                dk_ref[pl.ds(k0, TK), sl] += _nn(dzb[2 * i], qm[pp][0]) + _nn(dzb[2 * i + 1], qm[pp][1])
                dv_ref[pl.ds(k0, TK), sl] += _nn(ab[2 * i], dom[pp][0]) + _nn(ab[2 * i + 1], dom[pp][1])
                dq_ref[:, sl] += _tn(dzb[2 * i], km[pp][0]) + _tn(dzb[2 * i + 1], km[pp][1])
            return out

        def step(kb, carry, diag):
            k0 = pl.multiple_of(kb * TK, TK)
            out = []
            for g in range(0, ATT_PAIRS, SB_BWD_GROUP):
                out += group(k0, list(range(g, g + SB_BWD_GROUP)), carry[4 * g:4 * (g + SB_BWD_GROUP)], diag)
            return tuple(out)

        carry = lax.fori_loop(0, qi, lambda i, c: step(i, c, False), tuple(start))
        step(qi, carry, True)
        dq_ref[...] *= SB_SCALE

    W = ATT_PAIRS * LANES
    qspec = pl.BlockSpec((TQ, W), lambda b, h, i: (b * nq + i, h))
    sspec = pl.BlockSpec((S, W), lambda b, h, i: (b, h))
    outs, moved = _call(
        body, ex, name="sb_bwd", grid=(B, HEADS // 2 // ATT_PAIRS, nq),
        in_specs=[qspec,
                  pl.BlockSpec((S, W), lambda b, h, i: (b, ATT_W // W + h)),
                  pl.BlockSpec((S, W), lambda b, h, i: (b, 2 * ATT_W // W + h)),
                  qspec, qspec],
        out_specs=[qspec, sspec, sspec],
        out_shape=[jax.ShapeDtypeStruct((T, ATT_W), F32)] * 3, args=(p, p, p, lt, do))
    return tuple(outs) if ex is None else tuple(outs) + (moved,)


ALL_PAIRS = [slice(i * LANES, (i + 1) * LANES) for i in range(HEADS // 2)]


def _rope_masks(hp):
    grp = _iota2((1, LANES), 1) // ROPE_DIM
    return [grp == ((2 * hp + j) % 4) for j in range(2)]


def _mla_queries(qn_ref, qr_ref, masks):
    out = []
    for pp, sl in enumerate(ALL_PAIRS):
        qnv = qn_ref[:, sl]
        qrv = qr_ref[:, ALL_PAIRS[pp // 2]]
        rmasks = _rope_masks(pp)
        out.append([jnp.concatenate([jnp.where(masks[j], qnv, 0), jnp.where(rmasks[j], qrv, 0)], axis=1).astype(BF16)
                    for j in range(2)])
    return out


def mla_fwd_call(qn, qr, kn, krt, vm, B, S):
    T = B * S
    TQ, TK = ATT_TQ, ATT_TK
    nq = S // TQ

    def body(qn_ref, qr_ref, kn_ref, kr_ref, v_ref, o_ref, lse_ref):
        qi = pl.program_id(1)
        masks = _head_masks()
        qcat = _mla_queries(qn_ref, qr_ref, masks)
        row = _iota2((TQ, TK), 0)
        col = _iota2((TQ, TK), 1)
        vis = col <= row
        o_ref[...] = jnp.zeros_like(o_ref)

        def group(k0, pairs, carry, diag):
            heads = [(pp, j) for pp in pairs for j in range(2)]
            n = range(len(heads))
            krv = kr_ref[pl.ds(k0, TK), :]
            kcat = {pp: jnp.concatenate([kn_ref[pl.ds(k0, TK), ALL_PAIRS[pp]], krv], axis=1) for pp in pairs}
            vmk = {pp: _pair(v_ref[pl.ds(k0, TK), ALL_PAIRS[pp]], masks) for pp in pairs}
            s = [_nt(qcat[pp][j], kcat[pp]) * MLA_SCALE for pp, j in heads]
            if diag:
                s = [jnp.where(vis, x, NEG_BIG) for x in s]
            m_new = [jnp.maximum(carry[2 * h], jnp.max(s[h], axis=1, keepdims=True)) for h in n]
            alpha = [jnp.exp(carry[2 * h] - m_new[h]) for h in n]
            pexp = [jnp.exp(s[h] - m_new[h]) for h in n]
            out = []
            for h in n:
                out += [m_new[h], alpha[h] * carry[2 * h + 1] + jnp.sum(pexp[h], axis=1, keepdims=True)]
            pb = [x.astype(BF16) for x in pexp]
            for i, pp in enumerate(pairs):
                sl = ALL_PAIRS[pp]
                scale = jnp.where(masks[0], alpha[2 * i], alpha[2 * i + 1])
                o_ref[:, sl] = o_ref[:, sl] * scale + (_nn(pb[2 * i], vmk[pp][0]) + _nn(pb[2 * i + 1], vmk[pp][1]))
            return out

        def step(kb, carry, diag):
            k0 = pl.multiple_of(kb * TK, TK)
            out = []
            for g in range(0, len(ALL_PAIRS), MLA_GROUP):
                out += group(k0, list(range(g, g + MLA_GROUP)), carry[4 * g:4 * (g + MLA_GROUP)], diag)
            return tuple(out)

        neg = jnp.full((TQ, 1), NEG_BIG, F32)
        zero = jnp.zeros((TQ, 1), F32)
        carry = step(qi, (neg, zero) * (2 * len(ALL_PAIRS)), True)
        carry = lax.fori_loop(0, qi, lambda i, c: step(qi - 1 - i, c, False), carry)
        lane = _iota2((TQ, LANES), 1)
        for pp, sl in enumerate(ALL_PAIRS):
            m0, l0, m1, l1 = carry[4 * pp:4 * pp + 4]
            o_ref[:, sl] = o_ref[:, sl] * jnp.where(masks[0], 1.0 / l0, 1.0 / l1)
            lse_ref[:, sl] = jnp.where(lane == 0, m0 + jnp.log(l0), jnp.where(lane == 1, m1 + jnp.log(l1), 0.0))

    def rows(w):
        return pl.BlockSpec((TQ, w), lambda b, i: (b * nq + i, 0))

    def seq(w):
        return pl.BlockSpec((S, w), lambda b, i: (b, 0))

    return pl.pallas_call(
        body, name="mla_fwd", grid=(B, nq),
        in_specs=[rows(ATT_W), rows(ROPE_W), seq(ATT_W), seq(LANES), seq(ATT_W)],
        out_specs=[rows(ATT_W), rows(ATT_W)],
        out_shape=[jax.ShapeDtypeStruct((T, ATT_W), F32)] * 2,
        compiler_params=_params(("arbitrary", "arbitrary")),
    )(qn, qr, kn, krt, vm)


def mla_bwd_call(qn, qr, kn, krt, vm, o, lse, do, B, S, ex=None):
    T = B * S
    TQ, TK = ATT_TQ, ATT_TK
    nq = S // TQ

    def body(qn_ref, qr_ref, kn_ref, kr_ref, v_ref, o_ref, lse_ref, do_ref,
             dqn_ref, dqr_ref, dkn_ref, dv_ref, dkr_ref):
        qi = pl.program_id(1)

        @pl.when(qi == 0)
        def _():
            dkn_ref[...] = jnp.zeros_like(dkn_ref)
            dv_ref[...] = jnp.zeros_like(dv_ref)
            dkr_ref[...] = jnp.zeros_like(dkr_ref)

        masks = _head_masks()
        qcat = _mla_queries(qn_ref, qr_ref, masks)
        dom, dsum, lse = [], [], []
        for sl in ALL_PAIRS:
            do = do_ref[:, sl]
            dom.append(_pair(do, masks))
            d8 = _rows8(_lane_selector(HEAD_DIM), do * o_ref[:, sl])
            l8 = _rows8(_lane_selector(1), lse_ref[:, sl])
            dsum.append([_row_of(d8, j) for j in range(2)])
            lse.append([_row_of(l8, j) for j in range(2)])
        row = _iota2((TK, TQ), 0)
        col = _iota2((TK, TQ), 1)
        vis = row <= col
        dqn_ref[...] = jnp.zeros_like(dqn_ref)
        dqr_ref[...] = jnp.zeros_like(dqr_ref)

        def group(k0, pairs, diag):
            heads = [(pp, j) for pp in pairs for j in range(2)]
            n = range(len(heads))
            krv = kr_ref[pl.ds(k0, TK), :]
            knv = {pp: kn_ref[pl.ds(k0, TK), ALL_PAIRS[pp]] for pp in pairs}
            kcat = {pp: jnp.concatenate([knv[pp], krv], axis=1) for pp in pairs}
            v = {pp: v_ref[pl.ds(k0, TK), ALL_PAIRS[pp]] for pp in pairs}
            s = [_nt(kcat[pp], qcat[pp][j]) * MLA_SCALE for pp, j in heads]
            dp_ = [_nt(v[pp], dom[pp][j]) for pp, j in heads]
            pr = [jnp.exp(s[h] - lse[pp][j]) for h, (pp, j) in enumerate(heads)]
            if diag:
                pr = [jnp.where(vis, x, 0.0) for x in pr]
            ds = [(pr[h] * (dp_[h] - dsum[pp][j]) * MLA_SCALE).astype(BF16) for h, (pp, j) in enumerate(heads)]
            pb = [x.astype(BF16) for x in pr]
            for i, pp in enumerate(pairs):
                sl = ALL_PAIRS[pp]
                rmasks = _rope_masks(pp)
                kcat_j = [jnp.concatenate([jnp.where(masks[j], knv[pp], 0), jnp.where(rmasks[j], krv, 0)],
                                          axis=1).astype(BF16) for j in range(2)]
                dv_ref[pl.ds(k0, TK), sl] += _nn(pb[2 * i], dom[pp][0]) + _nn(pb[2 * i + 1], dom[pp][1])
                dk = _nn(ds[2 * i], qcat[pp][0]) + _nn(ds[2 * i + 1], qcat[pp][1])
                dq = _tn(ds[2 * i], kcat_j[0]) + _tn(ds[2 * i + 1], kcat_j[1])
                dqn_ref[:, sl] += dq[:, :LANES]
                dqr_ref[:, sl] += dq[:, LANES:]
                dkn_ref[pl.ds(k0, TK), sl] += dk[:, :LANES]
                dkr_ref[pl.ds(k0, TK), sl] += dk[:, LANES:]

        def step(kb, diag):
            k0 = pl.multiple_of(kb * TK, TK)
            for g in range(0, len(ALL_PAIRS), MLA_GROUP):
                group(k0, list(range(g, g + MLA_GROUP)), diag)

        step(qi, True)

        def loop(i, c):
            step(qi - 1 - i, False)
            return c

        lax.fori_loop(0, qi, loop, 0)

    def rows(w):
        return pl.BlockSpec((TQ, w), lambda b, i: (b * nq + i, 0))

    def seq(w):
        return pl.BlockSpec((S, w), lambda b, i: (b, 0))

    outs, moved = _call(
        body, ex, name="mla_bwd", grid=(B, nq),
        in_specs=[rows(ATT_W), rows(ROPE_W), seq(ATT_W), seq(LANES), seq(ATT_W), rows(ATT_W), rows(ATT_W), rows(ATT_W)],
        out_specs=[rows(ATT_W), rows(ATT_W), seq(ATT_W), seq(ATT_W), seq(ATT_W)],
        out_shape=[jax.ShapeDtypeStruct((T, ATT_W), F32)] * 5, args=(qn, qr, kn, krt, vm, o, lse, do))
    return tuple(outs) if ex is None else tuple(outs) + (moved,)


CONV_TC = 256


def _shift_down(x, n):
    return jnp.where(_iota2(x.shape, 0) >= n, pltpu.roll(x, n, 0), 0.0)


def _shift_up(x, n):
    rows = x.shape[0]
    return jnp.where(_iota2(x.shape, 0) < rows - n, pltpu.roll(x, rows - n, 0), 0.0)


def _taps(w_ref):
    return [w_ref[k:k + 1, :] for k in range(3)]


def _conv3(u, w, b):
    return w[0] * _shift_down(u, 2) + w[1] * _shift_down(u, 1) + w[2] * u + b


def conv_act_fwd_call(u, conv_w, conv_b, B, S):
    T = B * S
    nc = D_FF // CONV_TC

    def body(ug_ref, uv_ref, wg_ref, wv_ref, bg_ref, bv_ref, a_ref):
        gate = _conv3(ug_ref[...], _taps(wg_ref), bg_ref[...])
        val = _conv3(uv_ref[...], _taps(wv_ref), bv_ref[...])
        a_ref[...] =(gate * (1.0 / (1.0 + jnp.exp(-gate))) * val).astype(BF16)

    def blk(rows, off):
        return pl.BlockSpec((rows, CONV_TC), lambda b, j: (b if rows == S else 0, off + j))

    return pl.pallas_call(
        body, name="conv_act_fwd", grid=(B, nc),
        in_specs=[blk(S, 0), blk(S, nc), blk(3, 0), blk(3, nc), blk(1, 0), blk(1, nc)],
        out_specs=blk(S, 0),
        out_shape=jax.ShapeDtypeStruct((T, D_FF), BF16),
        compiler_params=_params(("parallel", "parallel")),
    )(u, u, conv_w, conv_w, conv_b, conv_b)


def conv_act_bwd_call(u, da, conv_w, conv_b, B, S):
    T = B * S
    nc = D_FF // CONV_TC

    def body(ug_ref, uv_ref, da_ref, wg_ref, wv_ref, bg_ref, bv_ref,
             dug_ref, duv_ref, dwg_ref, dwv_ref, dbg_ref, dbv_ref):
        @pl.when(pl.program_id(1) == 0)
        def _():
            for r in (dwg_ref, dwv_ref, dbg_ref, dbv_ref):
                r[...] = jnp.zeros_like(r)

        ug = ug_ref[...]
        uv = uv_ref[...]
        wg = _taps(wg_ref)
        wv = _taps(wv_ref)
        gate = _conv3(ug, wg, bg_ref[...])
        val = _conv3(uv, wv, bv_ref[...])
        da = da_ref[...]
        sig = 1.0 / (1.0 + jnp.exp(-gate))
        dval = da * (gate * sig)
        dgate = da * val * (sig * (1.0 + gate * (1.0 - sig)))
        for u_, d, w, du_ref, dw_ref, db_ref in ((ug, dgate, wg, dug_ref, dwg_ref, dbg_ref),
                                                 (uv, dval, wv, duv_ref, dwv_ref, dbv_ref)):
            du_ref[...] = (w[2] * d + w[1] * _shift_up(d, 1) + w[0] * _shift_up(d, 2)).astype(BF16)
            db_ref[...] += jnp.sum(d, axis=0, keepdims=True)
            dw_ref[0:1, :] += jnp.sum(d * _shift_down(u_, 2), axis=0, keepdims=True)
            dw_ref[1:2, :] += jnp.sum(d * _shift_down(u_, 1), axis=0, keepdims=True)
            dw_ref[2:3, :] += jnp.sum(d * u_, axis=0, keepdims=True)

    def blk(rows, off):
        return pl.BlockSpec((rows, CONV_TC), lambda j, b: (b if rows == S else 0, off + j))

    return pl.pallas_call(
        body, name="conv_act_bwd", grid=(nc, B),
        in_specs=[blk(S, 0), blk(S, nc), blk(S, 0), blk(3, 0), blk(3, nc), blk(1, 0), blk(1, nc)],
        out_specs=[blk(S, 0), blk(S, 0), blk(3, 0), blk(3, 0), blk(1, 0), blk(1, 0)],
        out_shape=[jax.ShapeDtypeStruct((T, D_FF), BF16), jax.ShapeDtypeStruct((T, D_FF), BF16),
                   jax.ShapeDtypeStruct((3, D_FF), F32), jax.ShapeDtypeStruct((3, D_FF), F32),
                   jax.ShapeDtypeStruct((1, D_FF), F32), jax.ShapeDtypeStruct((1, D_FF), F32)],
        compiler_params=_params(("parallel", "arbitrary")),
    )(u, u, da, conv_w, conv_w, conv_b, conv_b)


CHIP_MASKS = ((1, 0), (0, 1), (1, 1))


def _place():
    return lax.axis_index("x"), lax.axis_index("y"), lax.axis_index("c")


HALF_ALIGN = 32


def _any_specs(n):
    return [pl.BlockSpec(memory_space=pl.ANY)] * n


def _half_rows(r, half):
    return pl.ds(pl.multiple_of(half * (r // 2), HALF_ALIGN // 2), r // 2)


def _remote(src, dst, send_sem, recv_sem, device):
    return pltpu.make_async_remote_copy(src_ref=src, dst_ref=dst, send_sem=send_sem, recv_sem=recv_sem,
                                        device_id=device, device_id_type=MESH)


class Exchange:
    def __init__(self, ins, out_shape, sems, start, finish):
        self.ins, self.out_shape, self.sems, self.start, self.finish = list(ins), list(out_shape), list(sems), start, finish


def gather_group(shards):
    n = len(shards)
    split = [s.shape[0] % HALF_ALIGN == 0 for s in shards]

    def rows(w, half):
        return _half_rows(shards[w].shape[0], half) if split[w] else slice(None)

    def copies(ins, outs, sems):
        ici_s, ici_r, _, _, local_sems = sems
        x, y, c = _place()
        chip = 2 * x + y
        local = [pltpu.make_async_copy(ins[w], outs[w].at[chip], local_sems.at[w]) for w in range(n)]
        sends = [_remote(ins[w].at[rows(w, c)], outs[w].at[chip, rows(w, c)], ici_s.at[w, k], ici_r.at[w, k],
                         (x ^ fx, y ^ fy, c))
                 for w in range(n) for k, (fx, fy) in enumerate(CHIP_MASKS)]
        return local, sends

    def start(ins, outs, sems):
        local, sends = copies(ins, outs, sems)
        for cp in local + sends:
            cp.start()

    def finish(ins, outs, sems):
        ici_s, ici_r, d2d_s, d2d_r, _ = sems
        x, y, c = _place()
        sib = (x, y, 1 - c)
        local, sends = copies(ins, outs, sems)
        for w in range(n):
            for k, (fx, fy) in enumerate(CHIP_MASKS):
                landed = outs[w].at[2 * (x ^ fx) + (y ^ fy), rows(w, c)]
                _remote(landed, landed, ici_s.at[w, k], ici_r.at[w, k], sib).wait_recv()
                if split[w]:
                    cp = _remote(landed, landed, d2d_s.at[w, k], d2d_r.at[w, k], sib)
                    cp.start()
                    sends.append(cp)
        for w in range(n):
            for k, (fx, fy) in enumerate(CHIP_MASKS):
                if split[w]:
                    other = outs[w].at[2 * (x ^ fx) + (y ^ fy), rows(w, 1 - c)]
                    _remote(other, other, d2d_s.at[w, k], d2d_r.at[w, k], sib).wait_recv()
        for cp in sends:
            cp.wait_send()
        for cp in local:
            cp.wait()

    sems = pltpu.SemaphoreType.DMA((n, 3))
    return Exchange(shards, [jax.ShapeDtypeStruct((N_CHIPS,) + s.shape, s.dtype) for s in shards],
                    [sems, sems, sems, sems, pltpu.SemaphoreType.DMA((n,))], start, finish)


def swap_half(parts):
    n = len(parts)

    def copies(ins, outs, sems):
        x, y, c = _place()
        return [_remote(ins[w].at[:, _half_rows(parts[w].shape[1], 1 - c)], outs[w], sems[0].at[w], sems[1].at[w],
                        (x, y, 1 - c)) for w in range(n)]

    def start(ins, outs, sems):
        for cp in copies(ins, outs, sems):
            cp.start()

    def finish(ins, outs, sems):
        for cp in copies(ins, outs, sems):
            cp.wait_recv()
            cp.wait_send()

    return Exchange(parts, [jax.ShapeDtypeStruct((N_CHIPS, p.shape[1] // 2, p.shape[2]), F32) for p in parts],
                    [pltpu.SemaphoreType.DMA((n,))] * 2, start, finish)


def scatter_half(halves):
    n = len(halves)

    def copies(ins, outs, sems):
        x, y, c = _place()
        return [_remote(ins[w].at[2 * (x ^ fx) + (y ^ fy)], outs[w].at[k], sems[0].at[w, k], sems[1].at[w, k],
                        (x ^ fx, y ^ fy, c))
                for w in range(n) for k, (fx, fy) in enumerate(CHIP_MASKS)]

    def start(ins, outs, sems):
        for cp in copies(ins, outs, sems):
            cp.start()

    def finish(ins, outs, sems):
        for cp in copies(ins, outs, sems):
            cp.wait_recv()
            cp.wait_send()

    return Exchange(halves, [jax.ShapeDtypeStruct((3,) + h.shape[1:], F32) for h in halves],
                    [pltpu.SemaphoreType.DMA((n, 3))] * 2, start, finish)


def swap_final(finals):
    n = len(finals)

    def copies(ins, outs, sems):
        x, y, c = _place()
        mine = [outs[w].at[_half_rows(2 * finals[w].shape[0], c)] for w in range(n)]
        local = [pltpu.make_async_copy(ins[w], mine[w], sems[2].at[w]) for w in range(n)]
        sends = [_remote(ins[w], mine[w], sems[0].at[w], sems[1].at[w], (x, y, 1 - c)) for w in range(n)]
        return local, sends

    def start(ins, outs, sems):
        local, sends = copies(ins, outs, sems)
        for cp in local + sends:
            cp.start()

    def finish(ins, outs, sems):
        x, y, c = _place()
        local, sends = copies(ins, outs, sems)
        for w in range(n):
            got = outs[w].at[_half_rows(2 * finals[w].shape[0], 1 - c)]
            _remote(got, got, sems[0].at[w], sems[1].at[w], (x, y, 1 - c)).wait_recv()
        for cp in sends:
            cp.wait_send()
        for cp in local:
            cp.wait()

    return Exchange(finals, [jax.ShapeDtypeStruct((2 * f.shape[0], f.shape[1]), F32) for f in finals],
                    [pltpu.SemaphoreType.DMA((n,))] * 3, start, finish)


def exchange_call(name, ex):
    n, m = len(ex.ins), len(ex.out_shape)

    def body(*refs):
        ins, outs, sems = refs[:n], refs[n:n + m], refs[n + m:]
        ex.start(ins, outs, sems)
        ex.finish(ins, outs, sems)

    return pl.pallas_call(body, name=name, in_specs=_any_specs(n), out_specs=_any_specs(m), out_shape=ex.out_shape,
                          scratch_shapes=ex.sems, compiler_params=_params())(*ex.ins)


def _call(body, ex, *, name, grid, in_specs, out_specs, out_shape, args, scratch_shapes=()):
    sem = ("arbitrary",) * len(grid)
    if ex is None:
        outs = pl.pallas_call(body, name=name, grid=grid, in_specs=in_specs, out_specs=out_specs, out_shape=out_shape,
                              scratch_shapes=list(scratch_shapes), compiler_params=_params(sem))(*args)
        return outs, None
    ni, no, ns = len(in_specs), len(out_specs), len(scratch_shapes)
    ne, me = len(ex.ins), len(ex.out_shape)

    def wrapped(*refs):
        own_in, ex_in = refs[:ni], refs[ni:ni + ne]
        own_out, ex_out = refs[ni + ne:ni + ne + no], refs[ni + ne + no:ni + ne + no + me]
        own_scr, ex_sems = refs[ni + ne + no + me:ni + ne + no + me + ns], refs[ni + ne + no + me + ns:]
        ids = [pl.program_id(a) for a in range(len(grid))]
        first = functools.reduce(jnp.logical_and, [i == 0 for i in ids])
        last = functools.reduce(jnp.logical_and, [i == g - 1 for i, g in zip(ids, grid)])

        @pl.when(first)
        def _():
            ex.start(ex_in, ex_out, ex_sems)

        body(*own_in, *own_out, *own_scr)

        @pl.when(last)
        def _():
            ex.finish(ex_in, ex_out, ex_sems)

    outs = pl.pallas_call(
        wrapped, name=name, grid=grid, in_specs=list(in_specs) + _any_specs(ne),
        out_specs=list(out_specs) + _any_specs(me), out_shape=list(out_shape) + ex.out_shape,
        scratch_shapes=list(scratch_shapes) + ex.sems, compiler_params=_params(sem))(*args, *ex.ins)
    return outs[:no], outs[no:]


def _row_tile(rows, cap):
    return max(t for t in range(8, min(rows, cap) + 1, 8) if rows % t == 0)


def add_half_call(name, part, got, core):
    _, rh, cols = got.shape
    tr = _row_tile(rh, 128)
    nb = rh // tr

    def body(core_ref, p_ref, g_ref, o_ref):
        o_ref[...] = p_ref[...] + g_ref[...]

    blk = (N_CHIPS, tr, cols)
    return pl.pallas_call(
        body, name=name,
        grid_spec=pltpu.PrefetchScalarGridSpec(
            num_scalar_prefetch=1, grid=(nb,),
            in_specs=[pl.BlockSpec(blk, lambda i, core_ref: (0, core_ref[0] * nb + i, 0)),
                      pl.BlockSpec(blk, lambda i, core_ref: (0, i, 0))],
            out_specs=pl.BlockSpec(blk, lambda i, core_ref: (0, i, 0))),
        out_shape=jax.ShapeDtypeStruct(got.shape, F32),
        compiler_params=_params(("parallel",)),
    )(core, part, got)


def sum_chips_call(name, half, got, chip):
    _, rh, cols = got.shape
    tr = _row_tile(rh, 128)

    def body(chip_ref, h_ref, g_ref, o_ref):
        o_ref[...] = ((h_ref[0] + g_ref[0]) + g_ref[1]) + g_ref[2]

    return pl.pallas_call(
        body, name=name,
        grid_spec=pltpu.PrefetchScalarGridSpec(
            num_scalar_prefetch=1, grid=(rh // tr,),
            in_specs=[pl.BlockSpec((1, tr, cols), lambda i, chip_ref: (chip_ref[0], i, 0)),
                      pl.BlockSpec((3, tr, cols), lambda i, chip_ref: (0, i, 0))],
            out_specs=pl.BlockSpec((tr, cols), lambda i, chip_ref: (i, 0))),
        out_shape=jax.ShapeDtypeStruct((rh, cols), F32),
        compiler_params=_params(("parallel",)),
    )(chip, half, got)


def _adamw(w, g, m, v):
    m = ADAM_B1 * m + (1.0 - ADAM_B1) * g
    v = ADAM_B2 * v + (1.0 - ADAM_B2) * (g * g)
    m_hat = m / (1.0 - ADAM_B1 ** ADAM_STEP)
    v_hat = v / (1.0 - ADAM_B2 ** ADAM_STEP)
    delta = -ADAM_LR * (m_hat / (jnp.sqrt(v_hat) + ADAM_EPS) + ADAM_WD * w)
    return delta, m, v


def adamw_call(name, g, w, m, v):
    r, cols = w.shape
    tr = r if r % 8 else _row_tile(r, 256)

    def body(g_ref, w_ref, m_ref, v_ref, d_ref, nm_ref, nv_ref):
        d_ref[...], nm_ref[...], nv_ref[...] = _adamw(w_ref[...], g_ref[...], m_ref[...], v_ref[...])

    spec = pl.BlockSpec((tr, cols), lambda i: (i, 0))
    return pl.pallas_call(
        body, name=name, grid=(r // tr,), in_specs=[spec] * 4, out_specs=[spec] * 3,
        out_shape=[jax.ShapeDtypeStruct((r, cols), F32)] * 3,
        compiler_params=_params(("parallel",)),
    )(g, w, m, v)


def allsum_small_call(v):
    R = v.shape[0]

    def body(v_ref, out_ref, buf, send_sems, recv_sems):
        x, y, c = _place()
        me = 4 * x + 2 * y + c
        buf[me] = v_ref[...]
        sends = []
        for k in range(1, N_DEV):
            fx, fy, fc = (k >> 2) & 1, (k >> 1) & 1, k & 1
            cp = pltpu.make_async_remote_copy(
                src_ref=v_ref, dst_ref=buf.at[me], send_sem=send_sems.at[k - 1], recv_sem=recv_sems.at[k - 1],
                device_id=(x ^ fx, y ^ fy, c ^ fc), device_id_type=MESH)
            cp.start()
            sends.append(cp)
        for k in range(1, N_DEV):
            pltpu.make_async_remote_copy(
                src_ref=v_ref, dst_ref=buf.at[me ^ k], send_sem=send_sems.at[k - 1], recv_sem=recv_sems.at[k - 1],
                device_id=(x, y, c), device_id_type=MESH).wait_recv()
        acc = buf[0]
        for d in range(1, N_DEV):
            acc = acc + buf[d]
        out_ref[...] = acc
        for cp in sends:
            cp.wait_send()

    vm = pl.BlockSpec(memory_space=pltpu.VMEM)
    return pl.pallas_call(
        body, name="allsum_small", in_specs=[vm], out_specs=vm,
        out_shape=jax.ShapeDtypeStruct((R, LANES), F32),
        scratch_shapes=[pltpu.VMEM((N_DEV, R, LANES), F32), pltpu.SemaphoreType.DMA((N_DEV - 1,)),
                        pltpu.SemaphoreType.DMA((N_DEV - 1,))],
        compiler_params=_params(),
    )(v)


def _slab(flat, mult):
    n = flat.shape[-1]
    rows = -(-n // (LANES * mult)) * mult
    flat = jnp.pad(flat, [(0, 0)] * (flat.ndim - 1) + [(0, rows * LANES - n)])
    return flat.reshape(flat.shape[:-1] + (rows, LANES))


def full_from_chips(blocks, by_col):
    _, r, c = blocks.shape
    return blocks.transpose(1, 0, 2).reshape(r, N_CHIPS * c) if by_col else blocks.reshape(N_CHIPS * r, c)


def chips_from_full(full, by_col):
    if by_col:
        r, c = full.shape[0], full.shape[1] // N_CHIPS
        return full.reshape(r, N_CHIPS, c).transpose(1, 0, 2)
    return full.reshape(N_CHIPS, full.shape[0] // N_CHIPS, full.shape[1])


SMALL_PACK = SMALL_W + ("loss", "conv_w")
SMALL_PACK_N = {**SMALL_N, "loss": 1, "conv_w": 3 * 2 * D_FF}


def pack_small(vals):
    zero = jnp.zeros((1,), F32)
    return _slab(jnp.concatenate([vals[n].reshape(-1) if n in vals else jnp.tile(zero, SMALL_PACK_N[n])
                                  for n in SMALL_PACK]), 8)


def unpack_small(slab, shapes):
    flat = slab.reshape(-1)
    out, off = {}, 0
    for n in SMALL_PACK:
        out[n] = flat[off:off + SMALL_PACK_N[n]].reshape(shapes[n])
        off += SMALL_PACK_N[n]
    return out


def _split_heads(w, a, b):
    r = w.shape[0]
    w3 = w.reshape(r, HEADS, a + b)
    return w3[:, :, :a].reshape(r, HEADS * a), w3[:, :, a:].reshape(r, HEADS * b)


def _merge_heads(wa, wb, a, b):
    r = wa.shape[0]
    return jnp.concatenate([wa.reshape(r, HEADS, a), wb.reshape(r, HEADS, b)], axis=2).reshape(r, HEADS * (a + b))


def kernel(x, positions, g_mix, w_in, g_cq, w_uq, g_ckv, w_ukv, g_sb_out, g_mla_out, w_out, g_ffn, w_up, conv_w, conv_b, w_down, g_final, loss_target, m_g_mix, m_w_in, m_g_cq, m_w_uq, m_g_ckv, m_w_ukv, m_g_sb_out, m_g_mla_out, m_w_out, m_g_ffn, m_w_up, m_conv_w, m_conv_b, m_w_down, m_g_final, v_g_mix, v_w_in, v_g_cq, v_w_uq, v_g_ckv, v_w_ukv, v_g_sb_out, v_g_mla_out, v_w_out, v_g_ffn, v_w_up, v_conv_w, v_conv_b, v_w_down, v_g_final):
    given = dict(locals())
    B, S, _ = x.shape
    T = B * S
    w_big = {n: given[n][0] for n in BIG_W}
    m_big = {n: given["m_" + n][0] for n in BIG_W}
    v_big = {n: given["v_" + n][0] for n in BIG_W}

    first = ("w_in", "w_uq", "w_ukv")
    later = ("w_out", "w_up", "w_down", "conv_w")
    got_w = exchange_call("gather_first", gather_group([w_big[n].astype(BF16) for n in first]))
    full = {n: full_from_chips(g_, BIG_SHARD[n][2]) for n, g_ in zip(first, got_w)}
    gather_later = gather_group([w_big[n] if n == "conv_w" else w_big[n].astype(BF16) for n in later])
    w_in_p = jnp.pad(full["w_in"], ((0, 0), (0, IN_COLS_PAD - IN_COLS)))
    w_uq_p = jnp.concatenate(_split_heads(full["w_uq"], HEAD_DIM, ROPE_DIM), axis=1)
    w_ukv_p = jnp.concatenate(_split_heads(full["w_ukv"], HEAD_DIM, HEAD_DIM), axis=1)

    x2d = x.reshape(T, D_MODEL)
    half = ROPE_DIM // 2
    inv_freq = 1.0 / (ROPE_BASE ** (jnp.arange(half, dtype=F32) * (2.0 / ROPE_DIM)))
    cos, sin = rope_tab_call(positions.reshape(T, 1), jnp.tile(inv_freq, LANES // half).reshape(1, LANES))
    h = rmsnorm_fwd_call("norm_mix", x2d, g_mix)
    p = matmul_call("proj_in", h, w_in_p, "nn", tn=IN_COLS_PAD // 2)
    qn, qr, kn, vm, krt, cqn, ckvn = mla_prep_fwd_call(p, cos, sin, g_cq, g_ckv, w_uq_p, w_ukv_p)
    o_sb, lt_sb, got_w = sb_fwd_call(p, B, S, ex=gather_later)
    full.update({n: full_from_chips(g_, BIG_SHARD[n][2]) for n, g_ in zip(later, got_w)})
    conv_w_full = full["conv_w"]
    o_mla, lse = mla_fwd_call(qn, qr, kn, krt, vm, B, S)
    o_cat = outnorm_fwd_call(o_sb, o_mla, g_sb_out, g_mla_out)
    x1 = matmul_call("proj_out", o_cat, full["w_out"], "nn", res=x2d)
    hn = rmsnorm_fwd_call("norm_ffn", x1, g_ffn)
    u = matmul_call("ffn_up", hn, full["w_up"], "nn", tn=2 * D_FF // 4)
    act = conv_act_fwd_call(u, conv_w_full, conv_b, B, S)
    x2 = matmul_call("ffn_down", act, full["w_down"], "nn", res=x1)
    dx2, loss_row, dg_final = final_loss_call(x2, g_final.reshape(1, D_MODEL), loss_target.reshape(T, D_MODEL))

    xi, yi, ci = _place()
    chip = (2 * xi + yi).astype(jnp.int32).reshape(1)
    core = ci.astype(jnp.int32).reshape(1)

    def add_halves(names, parts, sib_rows):
        return [add_half_call("add_half_" + n, p_, s_, core) for n, p_, s_ in zip(names, parts, sib_rows)]

    def sum_chips(names, halves, from_chips):
        return [sum_chips_call("sum_chips_" + n, h_, f_, chip) for n, h_, f_ in zip(names, halves, from_chips)]

    ffn_w = ("w_down", "w_up")
    parts_ffn = [chips_from_full(wgrad_call("wgrad_down", act, dx2, tn=512), False)]
    da = matmul_call("ffn_down_bwd", dx2, full["w_down"], "nt", tn=D_FF // 2)
    du_g, du_v, dcw_g, dcw_v, dcb_g, dcb_v = conv_act_bwd_call(u, da, conv_w_full, conv_b, B, S)
    du = jnp.concatenate([du_g, du_v], axis=1)
    parts_ffn.append(wgrad_call("wgrad_up", hn, du, tn=2 * D_FF // 4, by_chip=True))
    dhn, sib_ffn = matmul_call("ffn_up_bwd", du, full["w_up"], "nt", tn=512, ex=swap_half(parts_ffn))
    dx1, dg_ffn = rmsnorm_bwd_call("norm_ffn_bwd", x1, g_ffn, dhn, dx2)
    parts_out = [chips_from_full(wgrad_call("wgrad_out", o_cat, dx1), False)]
    do_cat = matmul_call("proj_out_bwd", dx1, full["w_out"], "nt")
    do_sb, do_mla, dg_sb_out, dg_mla_out, sib_out = outnorm_bwd_call(
        o_sb, o_mla, g_sb_out, g_mla_out, do_cat, ex=swap_half(parts_out))
    early = ffn_w + ("w_out",)
    halves = add_halves(early, parts_ffn + parts_out, list(sib_ffn) + list(sib_out))
    dq_sb, dk_sb, dv_sb, from_chips = sb_bwd_call(p, lt_sb, do_sb, B, S, ex=scatter_half(halves))
    finals = sum_chips(early, halves, from_chips)
    dqn, dqr4, dkn, dvm, dkrt4, done = mla_bwd_call(qn, qr, kn, krt, vm, o_mla, lse, do_mla, B, S, ex=swap_final(finals))
    grads = dict(zip(early, done))
    dcq, dckvr, dq_cat, dkv_cat, dg_cq, dg_ckv = mla_prep_bwd_call(
        p, cos, sin, g_cq, g_ckv, w_uq_p, w_ukv_p, dqn, dqr4, dkn, dvm, dkrt4)
    dw_uq_p = wgrad_call("wgrad_uq", cqn, dq_cat)
    dw_ukv_p = wgrad_call("wgrad_ukv", ckvn, dkv_cat)
    dp = jnp.concatenate([dq_sb, dk_sb, dv_sb, dcq, dckvr], axis=1)
    late = ("w_uq", "w_ukv", "w_in")
    parts_late = [chips_from_full(g_, True) for g_ in (
        _merge_heads(dw_uq_p[:, :ATT_W], dw_uq_p[:, ATT_W:], HEAD_DIM, ROPE_DIM),
        _merge_heads(dw_ukv_p[:, :ATT_W], dw_ukv_p[:, ATT_W:], HEAD_DIM, HEAD_DIM),
        wgrad_call("wgrad_in", h, dp, tn=IN_COLS_PAD // 2)[:, :IN_COLS])]
    dh, sib_late = matmul_call("proj_in_bwd", dp, w_in_p, "nt", ex=swap_half(parts_late))
    halves = add_halves(late, parts_late, sib_late)
    grad_x, dg_mix, from_chips = rmsnorm_bwd_call("norm_mix_bwd", x2d, g_mix, dh, dx1, ex=scatter_half(halves))
    finals = sum_chips(late, halves, from_chips)
    grads.update(zip(late, exchange_call("swap_final_late", swap_final(finals))))

    shapes = {n: given[n].shape for n in SMALL_W}
    shapes.update(loss=(), conv_w=(3, 2 * D_FF))
    small_g = {"g_mix": dg_mix, "g_cq": dg_cq, "g_ckv": dg_ckv, "g_sb_out": dg_sb_out, "g_mla_out": dg_mla_out,
               "g_ffn": dg_ffn, "conv_b": jnp.concatenate([dcb_g, dcb_v], axis=1), "g_final": dg_final,
               "loss": loss_row[0, :1], "conv_w": jnp.concatenate([dcw_g, dcw_v], axis=1)}
    gs_slab = allsum_small_call(pack_small(small_g))
    small_in = [pack_small({n: given[pre + n] for n in SMALL_W}) for pre in ("", "m_", "v_")]
    small_out = [unpack_small(s, shapes) for s in (gs_slab,) + tuple(adamw_call("adamw_small", gs_slab, *small_in))]
    cw_cols = BIG_SHARD["conv_w"][1]
    grads["conv_w"] = lax.dynamic_slice_in_dim(small_out[0]["conv_w"], chip[0] * cw_cols, cw_cols, axis=1)

    big_out = {n: (grads[n],) + tuple(adamw_call("adamw_" + n, grads[n], w_big[n], m_big[n], v_big[n])) for n in BIG_W}
    weights = ("g_mix", "w_in", "g_cq", "w_uq", "g_ckv", "w_ukv", "g_sb_out", "g_mla_out", "w_out", "g_ffn",
               "w_up", "conv_w", "conv_b", "w_down", "g_final")
    outs = [small_out[0]["loss"], grad_x.reshape(B, S, D_MODEL)]
    for k in range(4):
        for n in weights:
            outs.append(big_out[n][k][None] if n in BIG_W else small_out[k][n])
    return tuple(outs)
```

```python
import functools

import jax
import jax.numpy as jnp
from jax import lax
from jax.experimental import pallas as pl
from jax.experimental.pallas import tpu as pltpu

F32 = jnp.float32
BF16 = jnp.bfloat16
MESH = pl.DeviceIdType.MESH

D_MODEL = 1024
HEADS = 8
HEAD_DIM = 64
ATT_W = HEADS * HEAD_DIM
ROPE_DIM = 32
ROPE_W = HEADS * ROPE_DIM
QK_DIM = HEAD_DIM + ROPE_DIM
Q_RANK = 384
KV_RANK = 256
D_FF = 2816
IN_COLS = 2208
IN_COLS_PAD = 2304
EPS = 1e-6
ROPE_BASE = 10000.0
SB_SCALE = HEAD_DIM ** -0.5
MLA_SCALE = QK_DIM ** -0.5
LANES = 128
N_CHIPS = 4
N_DEV = 8
VMEM_LIMIT = 48 * 1024 * 1024
ATT_TQ = 256
ATT_TK = 256
ATT_PAIRS = 4
PAIR_LANES = [slice(i * LANES, (i + 1) * LANES) for i in range(ATT_PAIRS)]
SB_BWD_GROUP = 2
SB_FWD_GROUP = 4
MLA_GROUP = 4
NEG_BIG = -1e30

ADAM_LR = 0.001
ADAM_B1 = 0.9
ADAM_B2 = 0.999
ADAM_EPS = 1e-08
ADAM_WD = 0.01
ADAM_STEP = 10

BIG_W = ("w_in", "w_uq", "w_ukv", "w_out", "w_up", "conv_w", "w_down")
BIG_SHARD = {
    "w_in": (D_MODEL, IN_COLS // 4, True),
    "w_uq": (Q_RANK, HEADS * QK_DIM // 4, True),
    "w_ukv": (KV_RANK, 2 * ATT_W // 4, True),
    "w_out": (2 * ATT_W // 4, D_MODEL, False),
    "w_up": (D_MODEL, 2 * D_FF // 4, True),
    "conv_w": (3, 2 * D_FF // 4, True),
    "w_down": (D_FF // 4, D_MODEL, False),
}
SMALL_W = ("g_mix", "g_cq", "g_ckv", "g_sb_out", "g_mla_out", "g_ffn", "conv_b", "g_final")
SMALL_N = {"g_mix": D_MODEL, "g_cq": Q_RANK, "g_ckv": KV_RANK, "g_sb_out": ATT_W, "g_mla_out": ATT_W,
           "g_ffn": D_MODEL, "conv_b": 2 * D_FF, "g_final": D_MODEL}


def _params(sem=None, **kw):
    return pltpu.CompilerParams(dimension_semantics=sem, vmem_limit_bytes=VMEM_LIMIT, **kw)


def _dot(a, b, dims):
    return lax.dot_general(a, b, (dims, ((), ())), preferred_element_type=F32)


def _nn(a, b):
    return _dot(a, b, ((1,), (0,)))


def _nt(a, b):
    return _dot(a, b, ((1,), (1,)))


def _tn(a, b):
    return _dot(a, b, ((0,), (0,)))


def _split2(x):
    hi = x.astype(BF16)
    lo = (x - hi.astype(F32)).astype(BF16)
    return hi, lo


def _split3(x):
    hi = x.astype(BF16)
    r1 = x - hi.astype(F32)
    mid = r1.astype(BF16)
    return hi, mid, (r1 - mid.astype(F32)).astype(BF16)


def _rms_r(x, d):
    return lax.rsqrt(jnp.sum(x * x, axis=-1, keepdims=True) * (1.0 / d) + EPS)


def _rms_bwd(x, g, dy, d):
    r = _rms_r(x, d)
    xhat = x * r
    gy = dy * g
    dx = r * (gy - xhat * (jnp.sum(xhat * gy, axis=-1, keepdims=True) * (1.0 / d)))
    return dx, jnp.sum(dy * xhat, axis=0, keepdims=True)


def _rot(x):
    lane = lax.broadcasted_iota(jnp.int32, x.shape, x.ndim - 1)
    n = x.shape[-1]
    return jnp.where((lane & 31) < 16, pltpu.roll(x, n - 16, x.ndim - 1), pltpu.roll(x, 16, x.ndim - 1))


def _fold4(x):
    return x + pltpu.roll(x, 32, 1) + pltpu.roll(x, 64, 1) + pltpu.roll(x, 96, 1)


def matmul_call(name, a, b, mode, out_dtype=F32, res=None, tm=512, tn=None, ex=None):
    M, K = a.shape
    N = b.shape[1] if mode == "nn" else b.shape[0]
    tn = N if tn is None else tn
    assert M % tm == 0 and N % tn == 0

    def body(*refs):
        if res is None:
            a_ref, b_ref, o_ref = refs
        else:
            a_ref, b_ref, r_ref, o_ref = refs
        av = a_ref[...].astype(BF16)
        bv = b_ref[...].astype(BF16)
        acc = _nn(av, bv) if mode == "nn" else _nt(av, bv)
        if res is not None:
            acc = r_ref[...] + acc
        o_ref[...] = acc.astype(out_dtype)

    in_specs = [pl.BlockSpec((tm, K), lambda j, i: (i, 0))]
    if mode == "nn":
        in_specs.append(pl.BlockSpec((K, tn), lambda j, i: (0, j)))
    else:
        in_specs.append(pl.BlockSpec((tn, K), lambda j, i: (j, 0)))
    args = [a, b]
    if res is not None:
        in_specs.append(pl.BlockSpec((tm, tn), lambda j, i: (i, j)))
        args.append(res)
    outs, moved = _call(body, ex, name=name, grid=(N // tn, M // tm), in_specs=in_specs,
                        out_specs=[pl.BlockSpec((tm, tn), lambda j, i: (i, j))],
                        out_shape=[jax.ShapeDtypeStruct((M, N), out_dtype)], args=args)
    return outs[0] if ex is None else (outs[0], moved)


def wgrad_call(name, a, b, tn=None, tt=512, by_chip=False):
    T, M = a.shape
    N = b.shape[1]
    tn = N if tn is None else tn
    assert T % tt == 0 and N % tn == 0
    if by_chip:
        out_spec = pl.BlockSpec((None, M, tn), lambda j, t: (j, 0, 0))
        out_shape = jax.ShapeDtypeStruct((N // tn, M, tn), F32)
    else:
        out_spec = pl.BlockSpec((M, tn), lambda j, t: (0, j))
        out_shape = jax.ShapeDtypeStruct((M, N), F32)

    def body(a_ref, b_ref, o_ref):
        @pl.when(pl.program_id(1) == 0)
        def _():
            o_ref[...] = jnp.zeros_like(o_ref)

        o_ref[...] += _tn(a_ref[...].astype(BF16), b_ref[...].astype(BF16))

    return pl.pallas_call(
        body, name=name, grid=(N // tn, T // tt),
        in_specs=[pl.BlockSpec((tt, M), lambda j, t: (t, 0)), pl.BlockSpec((tt, tn), lambda j, t: (t, j))],
        out_specs=out_spec, out_shape=out_shape,
        compiler_params=_params(("parallel", "arbitrary")),
    )(a, b)


UP_COLS = 2 * D_FF // N_CHIPS


def ffn_up_call(hn, w4, tm=512):
    T, K = hn.shape

    def body(a_ref, wg_ref, wv_ref, ug_ref, uv_ref):
        a = a_ref[...]
        ug_ref[...] = _nn(a, wg_ref[...])
        uv_ref[...] = _nn(a, wv_ref[...])

    out = pl.BlockSpec((tm, UP_COLS), lambda j, i: (i, j))
    return pl.pallas_call(
        body, name="ffn_up", grid=(2, T // tm),
        in_specs=[pl.BlockSpec((tm, K), lambda j, i: (i, 0)),
                  pl.BlockSpec((None, K, UP_COLS), lambda j, i: (j, 0, 0)),
                  pl.BlockSpec((None, K, UP_COLS), lambda j, i: (2 + j, 0, 0))],
        out_specs=[out, out], out_shape=[jax.ShapeDtypeStruct((T, D_FF), F32)] * 2,
        compiler_params=_params(("parallel", "parallel")),
    )(hn, w4, w4)


def ffn_up_bwd_call(du_g, du_v, w4, ex, tm=512, tn=512):
    T = du_g.shape[0]
    N = w4.shape[1]

    def body(g_ref, v_ref, w_ref, o_ref):
        acc = _nt(g_ref[:, :UP_COLS], w_ref[0]) + _nt(g_ref[:, UP_COLS:], w_ref[1])
        o_ref[...] = acc + _nt(v_ref[:, :UP_COLS], w_ref[2]) + _nt(v_ref[:, UP_COLS:], w_ref[3])

    row = pl.BlockSpec((tm, D_FF), lambda j, i: (i, 0))
    outs, moved = _call(body, ex, name="ffn_up_bwd", grid=(N // tn, T // tm),
                        in_specs=[row, row, pl.BlockSpec((N_CHIPS, tn, UP_COLS), lambda j, i: (0, j, 0))],
                        out_specs=[pl.BlockSpec((tm, tn), lambda j, i: (i, j))],
                        out_shape=[jax.ShapeDtypeStruct((T, N), F32)], args=(du_g, du_v, w4))
    return outs[0], moved


def wgrad_up_call(hn, du_g, du_v, tt=512):
    T, M = hn.shape

    def body(a_ref, g_ref, v_ref, o_ref):
        @pl.when(pl.program_id(1) == 0)
        def _():
            o_ref[...] = jnp.zeros_like(o_ref)

        a = a_ref[...]
        o_ref[0] += _tn(a, g_ref[...])
        o_ref[1] += _tn(a, v_ref[...])

    col = pl.BlockSpec((tt, UP_COLS), lambda j, t: (t, j))
    out = pl.pallas_call(
        body, name="wgrad_up", grid=(2, T // tt),
        in_specs=[pl.BlockSpec((tt, M), lambda j, t: (t, 0)), col, col],
        out_specs=pl.BlockSpec((2, None, M, UP_COLS), lambda j, t: (0, j, 0, 0)),
        out_shape=jax.ShapeDtypeStruct((2, 2, M, UP_COLS), F32),
        compiler_params=_params(("parallel", "arbitrary")),
    )(hn, du_g, du_v)
    return out.reshape(N_CHIPS, M, UP_COLS)


IN_PIECES = ((0, ATT_W), (ATT_W, ATT_W), (2 * ATT_W, ATT_W), (3 * ATT_W, Q_RANK), (3 * ATT_W + Q_RANK, Q_RANK))


def _piece_specs(rows, index):
    return [pl.BlockSpec((rows, w), functools.partial(index, off // w)) for off, w in IN_PIECES]


def proj_in_bwd_call(pieces, w_in_p, ex, tm=512):
    T = pieces[0].shape[0]
    N = w_in_p.shape[0]
    n = len(pieces)

    def body(*refs):
        o_ref = refs[2 * n]
        acc = _nt(refs[0][...].astype(BF16), refs[n][...])
        for i in range(1, n):
            acc = acc + _nt(refs[i][...].astype(BF16), refs[n + i][...])
        o_ref[...] = acc

    outs, moved = _call(body, ex, name="proj_in_bwd", grid=(T // tm,),
                        in_specs=_piece_specs(tm, lambda c, i: (i, 0)) + _piece_specs(N, lambda c, i: (0, c)),
                        out_specs=[pl.BlockSpec((tm, N), lambda i: (i, 0))],
                        out_shape=[jax.ShapeDtypeStruct((T, N), F32)], args=tuple(pieces) + (w_in_p,) * n)
    return outs[0], moved


def wgrad_in_call(h, pieces, tt=512):
    T, M = h.shape
    n = len(pieces)

    def body(*refs):
        a_ref, o_ref = refs[0], refs[n + 1]

        @pl.when(pl.program_id(0) == 0)
        def _():
            o_ref[...] = jnp.zeros_like(o_ref)

        a = a_ref[...]
        for i, (off, w) in enumerate(IN_PIECES):
            o_ref[:, off:off + w] += _tn(a, refs[1 + i][...].astype(BF16))

    return pl.pallas_call(
        body, name="wgrad_in", grid=(T // tt,),
        in_specs=[pl.BlockSpec((tt, M), lambda t: (t, 0))] + [pl.BlockSpec((tt, w), lambda t: (t, 0)) for _, w in IN_PIECES],
        out_specs=pl.BlockSpec((M, IN_COLS_PAD), lambda t: (0, 0)),
        out_shape=jax.ShapeDtypeStruct((M, IN_COLS_PAD), F32),
        compiler_params=_params(("arbitrary",)),
    )(h, *pieces)


def rmsnorm_fwd_call(name, x, g, tm=512, ex=None):
    T, d = x.shape

    def body(x_ref, g_ref, o_ref):
        x = x_ref[...]
        o_ref[...] = ((x * _rms_r(x, d)) * g_ref[...]).astype(BF16)

    row = pl.BlockSpec((tm, d), lambda i: (i, 0))
    outs, moved = _call(body, ex, name=name, grid=(T // tm,), in_specs=[row, pl.BlockSpec((1, d), lambda i: (0, 0))],
                        out_specs=[row], out_shape=[jax.ShapeDtypeStruct((T, d), BF16)], args=(x, g))
    return outs[0] if ex is None else (outs[0], moved)


def rmsnorm_bwd_call(name, x, g, dy, res, tm=512, ex=None):
    T, d = x.shape

    def body(x_ref, g_ref, dy_ref, r_ref, dx_ref, dg_ref):
        @pl.when(pl.program_id(0) == 0)
        def _():
            dg_ref[...] = jnp.zeros_like(dg_ref)

        dx, dg = _rms_bwd(x_ref[...], g_ref[...], dy_ref[...], d)
        dx_ref[...] = r_ref[...] + dx
        dg_ref[...] += dg

    row = pl.BlockSpec((tm, d), lambda i: (i, 0))
    vec = pl.BlockSpec((1, d), lambda i: (0, 0))
    outs, moved = _call(body, ex, name=name, grid=(T // tm,), in_specs=[row, vec, row, row], out_specs=[row, vec],
                        out_shape=[jax.ShapeDtypeStruct((T, d), F32), jax.ShapeDtypeStruct((1, d), F32)],
                        args=(x, g, dy, res))
    return tuple(outs) if ex is None else tuple(outs) + (moved,)


def outnorm_fwd_call(o_sb, o_mla, g_sb, g_mla, tm=512):
    T = o_sb.shape[0]

    def body(a_ref, b_ref, ga_ref, gb_ref, o_ref):
        a = a_ref[...]
        b = b_ref[...]
        ya = (a * _rms_r(a, ATT_W)) * ga_ref[...]
        yb = (b * _rms_r(b, ATT_W)) * gb_ref[...]
        o_ref[...] = jnp.concatenate([ya, yb], axis=1).astype(BF16)

    row = pl.BlockSpec((tm, ATT_W), lambda i: (i, 0))
    vec = pl.BlockSpec((1, ATT_W), lambda i: (0, 0))
    return pl.pallas_call(
        body, name="outnorm_fwd", grid=(T // tm,), in_specs=[row, row, vec, vec],
        out_specs=pl.BlockSpec((tm, 2 * ATT_W), lambda i: (i, 0)),
        out_shape=jax.ShapeDtypeStruct((T, 2 * ATT_W), BF16),
        compiler_params=_params(("parallel",)),
    )(o_sb, o_mla, g_sb, g_mla)


def outnorm_bwd_call(o_sb, o_mla, g_sb, g_mla, do_cat, tm=512, ex=None):
    T = o_sb.shape[0]

    def body(a_ref, b_ref, ga_ref, gb_ref, d_ref, da_ref, db_ref, dga_ref, dgb_ref):
        @pl.when(pl.program_id(0) == 0)
        def _():
            dga_ref[...] = jnp.zeros_like(dga_ref)
            dgb_ref[...] = jnp.zeros_like(dgb_ref)

        d = d_ref[...]
        da, dga = _rms_bwd(a_ref[...], ga_ref[...], d[:, :ATT_W], ATT_W)
        db, dgb = _rms_bwd(b_ref[...], gb_ref[...], d[:, ATT_W:], ATT_W)
        da_ref[...] = da
        db_ref[...] = db
        dga_ref[...] += dga
        dgb_ref[...] += dgb

    row = pl.BlockSpec((tm, ATT_W), lambda i: (i, 0))
    vec = pl.BlockSpec((1, ATT_W), lambda i: (0, 0))
    outs, moved = _call(
        body, ex, name="outnorm_bwd", grid=(T // tm,),
        in_specs=[row, row, vec, vec, pl.BlockSpec((tm, 2 * ATT_W), lambda i: (i, 0))],
        out_specs=[row, row, vec, vec],
        out_shape=[jax.ShapeDtypeStruct((T, ATT_W), F32), jax.ShapeDtypeStruct((T, ATT_W), F32),
                   jax.ShapeDtypeStruct((1, ATT_W), F32), jax.ShapeDtypeStruct((1, ATT_W), F32)],
        args=(o_sb, o_mla, g_sb, g_mla, do_cat))
    return tuple(outs) if ex is None else tuple(outs) + (moved,)


def final_loss_call(x2, g, target, tm=512):
    T, d = x2.shape

    def body(x_ref, g_ref, t_ref, dx_ref, loss_ref, dg_ref):
        @pl.when(pl.program_id(0) == 0)
        def _():
            loss_ref[...] = jnp.zeros_like(loss_ref)
            dg_ref[...] = jnp.zeros_like(dg_ref)

        x = x_ref[...]
        g = g_ref[...]
        y = (x * _rms_r(x, d)) * g
        err = y - t_ref[...]
        loss_ref[...] += jnp.sum(jnp.sum(err * err, axis=1, keepdims=True), axis=0, keepdims=True) * (0.5 / d)
        dx, dg = _rms_bwd(x, g, err * (1.0 / d), d)
        dx_ref[...] = dx
        dg_ref[...] += dg

    row = pl.BlockSpec((tm, d), lambda i: (i, 0))
    vec = pl.BlockSpec((1, d), lambda i: (0, 0))
    return pl.pallas_call(
        body, name="final_loss", grid=(T // tm,), in_specs=[row, vec, row],
        out_specs=[row, pl.BlockSpec((1, LANES), lambda i: (0, 0)), vec],
        out_shape=[jax.ShapeDtypeStruct((T, d), F32), jax.ShapeDtypeStruct((1, LANES), F32),
                   jax.ShapeDtypeStruct((1, d), F32)],
        compiler_params=_params(("arbitrary",)),
    )(x2, g, target)


def rope_tab_call(pos, inv_freq, tm=512):
    T = pos.shape[0]

    def body(p_ref, f_ref, c_ref, s_ref):
        ang = p_ref[...].astype(F32) * f_ref[...]
        lane = lax.broadcasted_iota(jnp.int32, ang.shape, 1)
        sn = jnp.sin(ang)
        c_ref[...] = jnp.cos(ang)
        s_ref[...] = jnp.where((lane & 31) < 16, -sn, sn)

    row = pl.BlockSpec((tm, LANES), lambda i: (i, 0))
    return pl.pallas_call(
        body, name="rope_tab", grid=(T // tm,),
        in_specs=[pl.BlockSpec((tm, 1), lambda i: (i, 0)), pl.BlockSpec((1, LANES), lambda i: (0, 0))],
        out_specs=[row, row],
        out_shape=[jax.ShapeDtypeStruct((T, LANES), F32)] * 2,
        compiler_params=_params(("parallel",)),
    )(pos, inv_freq)


def mla_prep_fwd_call(p, cos, sin, g_cq, g_ckv, w_uq_p, w_ukv_p, tm=512):
    T = p.shape[0]

    def body(cq_ref, ckvr_ref, c_ref, s_ref, gq_ref, gkv_ref, wq_ref, wkv_ref,
             qn_ref, qr_ref, kn_ref, vm_ref, krt_ref, cqn_ref, ckvn_ref):
        c = c_ref[...]
        s = s_ref[...]
        cq = cq_ref[...]
        cqn = ((cq * _rms_r(cq, Q_RANK)) * gq_ref[...]).astype(BF16)
        cqn_ref[...] = cqn
        q = _nn(cqn, wq_ref[...])
        qn_ref[...] = q[:, :ATT_W].astype(BF16)
        for g in range(ROPE_W // LANES):
            qr = q[:, ATT_W + g * LANES:ATT_W + (g + 1) * LANES]
            qr_ref[:, g * LANES:(g + 1) * LANES] = (qr * c + _rot(qr) * s).astype(BF16)
        ckvr = ckvr_ref[...]
        ckv = ckvr[:, :KV_RANK]
        ckvn = ((ckv * _rms_r(ckv, KV_RANK)) * gkv_ref[...]).astype(BF16)
        ckvn_ref[...] = ckvn
        kv = _nn(ckvn, wkv_ref[...])
        kn_ref[...] = kv[:, :ATT_W].astype(BF16)
        vm_ref[...] = kv[:, ATT_W:].astype(BF16)
        kr = _fold4(ckvr[:, KV_RANK:])
        krt_ref[...] = (kr * c + _rot(kr) * s).astype(BF16)

    def row(w, j=0):
        return pl.BlockSpec((tm, w), lambda i: (i, j))

    def full(a):
        return pl.BlockSpec(a.shape, lambda i: (0, 0))

    return pl.pallas_call(
        body, name="mla_prep_fwd", grid=(T // tm,),
        in_specs=[row(Q_RANK, 4), row(Q_RANK, 5), row(LANES), row(LANES), full(g_cq), full(g_ckv),
                  full(w_uq_p), full(w_ukv_p)],
        out_specs=[row(ATT_W), row(ROPE_W), row(ATT_W), row(ATT_W), row(LANES), row(Q_RANK), row(KV_RANK)],
        out_shape=[jax.ShapeDtypeStruct((T, w), BF16) for w in (ATT_W, ROPE_W, ATT_W, ATT_W, LANES, Q_RANK, KV_RANK)],
        compiler_params=_params(("parallel",)),
    )(p, p, cos, sin, g_cq, g_ckv, w_uq_p, w_ukv_p)


def mla_prep_bwd_call(p, cos, sin, g_cq, g_ckv, w_uq_p, w_ukv_p, dqn, dqr4, dkn, dvm, dkrt4, tm=512):
    T = p.shape[0]

    def body(cq_ref, ckvr_ref, c_ref, s_ref, gq_ref, gkv_ref, wq_ref, wkv_ref,
             dqn_ref, dqr4_ref, dkn_ref, dvm_ref, dkrt4_ref,
             dcq_ref, dckvr_ref, dq_ref, dkv_ref, dgq_ref, dgkv_ref):
        @pl.when(pl.program_id(0) == 0)
        def _():
            dgq_ref[...] = jnp.zeros_like(dgq_ref)
            dgkv_ref[...] = jnp.zeros_like(dgkv_ref)

        c = c_ref[...]
        s = s_ref[...]
        d4 = dqr4_ref[...]
        dqr = [d4[:, :128] + d4[:, 128:256], d4[:, 256:384] + d4[:, 384:]]
        dqr = [t * c + _rot(t * s) for t in dqr]
        dq = jnp.concatenate([dqn_ref[...]] + dqr, axis=1).astype(BF16)
        dq_ref[...] = dq
        dcq, dgq = _rms_bwd(cq_ref[...], gq_ref[...], _nt(dq, wq_ref[...]), Q_RANK)
        dcq_ref[...] = dcq
        dgq_ref[...] += dgq
        dkv = jnp.concatenate([dkn_ref[...], dvm_ref[...]], axis=1).astype(BF16)
        dkv_ref[...] = dkv
        ckvr = ckvr_ref[...]
        dckv, dgkv = _rms_bwd(ckvr[:, :KV_RANK], gkv_ref[...], _nt(dkv, wkv_ref[...]), KV_RANK)
        dgkv_ref[...] += dgkv
        k4 = dkrt4_ref[...]
        dkr = _fold4(k4[:, :128] + k4[:, 128:256] + k4[:, 256:384] + k4[:, 384:])
        dkr = dkr * c + _rot(dkr * s)
        lane = lax.broadcasted_iota(jnp.int32, dkr.shape, 1)
        dckvr_ref[...] = jnp.concatenate([dckv, jnp.where(lane < ROPE_DIM, dkr, 0.0)], axis=1)

    def row(w, j=0):
        return pl.BlockSpec((tm, w), lambda i: (i, j))

    def full(a):
        return pl.BlockSpec(a.shape, lambda i: (0, 0))

    return pl.pallas_call(
        body, name="mla_prep_bwd", grid=(T // tm,),
        in_specs=[row(Q_RANK, 4), row(Q_RANK, 5), row(LANES), row(LANES), full(g_cq), full(g_ckv),
                  full(w_uq_p), full(w_ukv_p), row(ATT_W), row(ATT_W), row(ATT_W), row(ATT_W), row(ATT_W)],
        out_specs=[row(Q_RANK), row(Q_RANK), row(ATT_W + ROPE_W), row(2 * ATT_W),
                   pl.BlockSpec((1, Q_RANK), lambda i: (0, 0)), pl.BlockSpec((1, KV_RANK), lambda i: (0, 0))],
        out_shape=[jax.ShapeDtypeStruct((T, Q_RANK), F32), jax.ShapeDtypeStruct((T, Q_RANK), F32),
                   jax.ShapeDtypeStruct((T, ATT_W + ROPE_W), BF16), jax.ShapeDtypeStruct((T, 2 * ATT_W), BF16),
                   jax.ShapeDtypeStruct((1, Q_RANK), F32), jax.ShapeDtypeStruct((1, KV_RANK), F32)],
        compiler_params=_params(("arbitrary",)),
    )(p, p, cos, sin, g_cq, g_ckv, w_uq_p, w_ukv_p, dqn, dqr4, dkn, dvm, dkrt4)


def _iota2(shape, axis):
    return lax.broadcasted_iota(jnp.int32, shape, axis)


def _head_masks():
    lane = _iota2((1, LANES), 1)
    return lane < HEAD_DIM, lane >= HEAD_DIM


def _pair(x, masks, dtype=BF16):
    return [jnp.where(m, x, 0.0).astype(dtype) for m in masks]


def _log_gates(z):
    keep = jnp.maximum(z, 0.0) + jnp.log(1.0 + jnp.exp(-jnp.abs(z)))
    return z - keep, keep


def _last_row(x):
    return _row_of(x[x.shape[0] - 8:, :], 7)


def _lane_selector(group):
    return jnp.where(_iota2((16, LANES), 1) // group == _iota2((16, LANES), 0), 1.0, 0.0).astype(BF16)


def _rows8(sel_t, x):
    hi = x.astype(BF16)
    r1 = x - hi.astype(F32)
    mid = r1.astype(BF16)
    lo = (r1 - mid.astype(F32)).astype(BF16)
    return _nt(sel_t, hi) + _nt(sel_t, mid) + _nt(sel_t, lo)


def _row_of(x8, j):
    return jnp.sum(jnp.where(_iota2(x8.shape, 0) == j, x8, 0.0), axis=0, keepdims=True)


def sb_fwd_call(p, B, S, ex=None):
    T = B * S
    TQ, TK = ATT_TQ, ATT_TK
    nq = S // TQ

    def body(q_ref, k_ref, v_ref, o_ref, lt_ref):
        qi = pl.program_id(2)
        masks = _head_masks()
        qm = [_pair(q_ref[:, sl] * SB_SCALE, masks) for sl in PAIR_LANES]
        row = _iota2((TQ, TK), 0)
        col = _iota2((TQ, TK), 1)
        tri = jnp.where(row > col, 1.0, 0.0).astype(BF16)
        tri2 = jnp.concatenate([tri, tri], axis=0)
        vis = col < row
        o_ref[...] = jnp.zeros_like(o_ref)

        def group(k0, pairs, carry, diag):
            heads = [(pp, j) for pp in pairs for j in range(2)]
            n = range(len(heads))
            k = {pp: k_ref[pl.ds(k0, TK), PAIR_LANES[pp]].astype(BF16) for pp in pairs}
            vm = {pp: _pair(v_ref[pl.ds(k0, TK), PAIR_LANES[pp]], masks) for pp in pairs}
            gates = [_log_gates(_nt(qm[pp][j], k[pp])) for pp, j in heads]
            lb = [g[0] for g in gates]
            keep = [jnp.where(vis, g[1], 0.0) if diag else g[1] for g in gates]
            tail = [_nn(jnp.concatenate(_split2(keep[h]), axis=1), tri2) + carry[h] for h in n]
            a = [jnp.exp(lb[h] - tail[h]) for h in n]
            if diag:
                a = [jnp.where(vis, x, 0.0) for x in a]
            ab = [x.astype(BF16) for x in a]
            for i, pp in enumerate(pairs):
                o_ref[:, PAIR_LANES[pp]] += _nn(ab[2 * i], vm[pp][0]) + _nn(ab[2 * i + 1], vm[pp][1])
            return [carry[h] + jnp.sum(keep[h], axis=1, keepdims=True) for h in n]

        def step(kb, carry, diag):
            k0 = pl.multiple_of(kb * TK, TK)
            out = []
            for g in range(0, ATT_PAIRS, SB_FWD_GROUP):
                out += group(k0, list(range(g, g + SB_FWD_GROUP)), carry[2 * g:2 * (g + SB_FWD_GROUP)], diag)
            return tuple(out)

        zero = jnp.zeros((TQ, 1), F32)
        carry = step(qi, (zero,) * (2 * ATT_PAIRS), True)
        carry = lax.fori_loop(0, qi, lambda i, c: step(qi - 1 - i, c, False), carry)
        lane = _iota2((TQ, LANES), 1)
        for pp, sl in enumerate(PAIR_LANES):
            lt_ref[:, sl] = jnp.where(lane == 0, carry[2 * pp], jnp.where(lane == 1, carry[2 * pp + 1], 0.0))

    W = ATT_PAIRS * LANES
    qspec = pl.BlockSpec((TQ, W), lambda b, h, i: (b * nq + i, h))
    outs, moved = _call(
        body, ex, name="sb_fwd", grid=(B, HEADS // 2 // ATT_PAIRS, nq),
        in_specs=[qspec,
                  pl.BlockSpec((S, W), lambda b, h, i: (b, ATT_W // W + h)),
                  pl.BlockSpec((S, W), lambda b, h, i: (b, 2 * ATT_W // W + h))],
        out_specs=[qspec, qspec],
        out_shape=[jax.ShapeDtypeStruct((T, ATT_W), F32)] * 2, args=(p, p, p))
    return tuple(outs) if ex is None else tuple(outs) + (moved,)


def sb_bwd_call(p, lt, do, B, S, ex=None):
    T = B * S
    TQ, TK = ATT_TQ, ATT_TK
    nq = S // TQ

    def body(q_ref, k_ref, v_ref, lt_ref, do_ref, dq_ref, dk_ref, dv_ref):
        qi = pl.program_id(2)

        @pl.when(qi == 0)
        def _():
            dk_ref[...] = jnp.zeros_like(dk_ref)
            dv_ref[...] = jnp.zeros_like(dv_ref)

        masks = _head_masks()
        qm = [_pair(q_ref[:, sl] * SB_SCALE, masks) for sl in PAIR_LANES]
        dom = [_pair(do_ref[:, sl], masks) for sl in PAIR_LANES]
        start = []
        for sl in PAIR_LANES:
            l8 = _rows8(_lane_selector(1), lt_ref[:, sl])
            start += [-_row_of(l8, 0), jnp.zeros((1, TQ), F32), -_row_of(l8, 1), jnp.zeros((1, TQ), F32)]
        row = _iota2((TK, TQ), 0)
        col = _iota2((TK, TQ), 1)
        incl = jnp.where(col <= row, 1.0, 0.0).astype(BF16)
        incl2 = jnp.concatenate([incl, incl], axis=1)
        excl = jnp.where(col < row, 1.0, 0.0).astype(BF16)
        excl2 = jnp.concatenate([excl, excl], axis=1)
        vis = row < col
        dq_ref[...] = jnp.zeros_like(dq_ref)

        def group(k0, pairs, carry, diag):
            heads = [(pp, j) for pp in pairs for j in range(2)]
            n = range(len(heads))
            kf = {pp: k_ref[pl.ds(k0, TK), PAIR_LANES[pp]] for pp in pairs}
            km = {pp: _pair(kf[pp], masks) for pp in pairs}
            v = {pp: v_ref[pl.ds(k0, TK), PAIR_LANES[pp]].astype(BF16) for pp in pairs}
            z = [_nt(kf[pp].astype(BF16), qm[pp][j]) for pp, j in heads]
            da = [_nt(v[pp], dom[pp][j]) for pp, j in heads]
            gates = [_log_gates(x) for x in z]
            lb = [g[0] for g in gates]
            keep = [jnp.where(vis, g[1], 0.0) if diag else g[1] for g in gates]
            left = [_nn(incl2, jnp.concatenate(_split2(keep[h]), axis=0)) + carry[2 * h] for h in n]
            a = [jnp.exp(lb[h] + left[h]) for h in n]
            if diag:
                a = [jnp.where(vis, x, 0.0) for x in a]
            e = [a[h] * da[h] for h in n]
            before = [_nn(excl2, jnp.concatenate(_split2(e[h]), axis=0)) + carry[2 * h + 1] for h in n]
            dz = [e[h] - jnp.exp(lb[h]) * (e[h] + before[h]) for h in n]
            if diag:
                dz = [jnp.where(vis, x, 0.0) for x in dz]
            dzb = [x.astype(BF16) for x in dz]
            ab = [x.astype(BF16) for x in a]
            out = []
            for h in n:
                out += [_last_row(left[h]), _last_row(before[h]) + _last_row(e[h])]
            for i, pp in enumerate(pairs):
                sl = PAIR_LANES[pp]
                dk_ref[pl.ds(k0, TK), sl] += _nn(dzb[2 * i], qm[pp][0]) + _nn(dzb[2 * i + 1], qm[pp][1])
                dv_ref[pl.ds(k0, TK), sl] += _nn(ab[2 * i], dom[pp][0]) + _nn(ab[2 * i + 1], dom[pp][1])
                dq_ref[:, sl] += _tn(dzb[2 * i], km[pp][0]) + _tn(dzb[2 * i + 1], km[pp][1])
            return out

        def step(kb, carry, diag):
            k0 = pl.multiple_of(kb * TK, TK)
            out = []
            for g in range(0, ATT_PAIRS, SB_BWD_GROUP):
                out += group(k0, list(range(g, g + SB_BWD_GROUP)), carry[4 * g:4 * (g + SB_BWD_GROUP)], diag)
            return tuple(out)

        carry = lax.fori_loop(0, qi, lambda i, c: step(i, c, False), tuple(start))
        step(qi, carry, True)
        dq_ref[...] *= SB_SCALE

    W = ATT_PAIRS * LANES
    qspec = pl.BlockSpec((TQ, W), lambda b, h, i: (b * nq + i, h))
    sspec = pl.BlockSpec((S, W), lambda b, h, i: (b, h))
    outs, moved = _call(
        body, ex, name="sb_bwd", grid=(B, HEADS // 2 // ATT_PAIRS, nq),
        in_specs=[qspec,
                  pl.BlockSpec((S, W), lambda b, h, i: (b, ATT_W // W + h)),
                  pl.BlockSpec((S, W), lambda b, h, i: (b, 2 * ATT_W // W + h)),
                  qspec, qspec],
        out_specs=[qspec, sspec, sspec],
        out_shape=[jax.ShapeDtypeStruct((T, ATT_W), F32)] * 3, args=(p, p, p, lt, do))
    return tuple(outs) if ex is None else tuple(outs) + (moved,)


ALL_PAIRS = [slice(i * LANES, (i + 1) * LANES) for i in range(HEADS // 2)]


def _rope_masks(hp):
    grp = _iota2((1, LANES), 1) // ROPE_DIM
    return [grp == ((2 * hp + j) % 4) for j in range(2)]


def _mla_queries(qn_ref, qr_ref, masks):
    out = []
    for pp, sl in enumerate(ALL_PAIRS):
        qnv = qn_ref[:, sl]
        qrv = qr_ref[:, ALL_PAIRS[pp // 2]]
        rmasks = _rope_masks(pp)
        out.append([jnp.concatenate([jnp.where(masks[j], qnv, 0), jnp.where(rmasks[j], qrv, 0)], axis=1).astype(BF16)
                    for j in range(2)])
    return out


def mla_fwd_call(qn, qr, kn, krt, vm, B, S):
    T = B * S
    TQ, TK = ATT_TQ, ATT_TK
    nq = S // TQ

    def body(qn_ref, qr_ref, kn_ref, kr_ref, v_ref, o_ref, lse_ref):
        qi = pl.program_id(1)
        masks = _head_masks()
        qcat = _mla_queries(qn_ref, qr_ref, masks)
        row = _iota2((TQ, TK), 0)
        col = _iota2((TQ, TK), 1)
        vis = col <= row
        o_ref[...] = jnp.zeros_like(o_ref)

        def group(k0, pairs, carry, diag):
            heads = [(pp, j) for pp in pairs for j in range(2)]
            n = range(len(heads))
            krv = kr_ref[pl.ds(k0, TK), :]
            kcat = {pp: jnp.concatenate([kn_ref[pl.ds(k0, TK), ALL_PAIRS[pp]], krv], axis=1) for pp in pairs}
            vmk = {pp: _pair(v_ref[pl.ds(k0, TK), ALL_PAIRS[pp]], masks) for pp in pairs}
            s = [_nt(qcat[pp][j], kcat[pp]) * MLA_SCALE for pp, j in heads]
            if diag:
                s = [jnp.where(vis, x, NEG_BIG) for x in s]
            m_new = [jnp.maximum(carry[2 * h], jnp.max(s[h], axis=1, keepdims=True)) for h in n]
            alpha = [jnp.exp(carry[2 * h] - m_new[h]) for h in n]
            pexp = [jnp.exp(s[h] - m_new[h]) for h in n]
            out = []
            for h in n:
                out += [m_new[h], alpha[h] * carry[2 * h + 1] + jnp.sum(pexp[h], axis=1, keepdims=True)]
            pb = [x.astype(BF16) for x in pexp]
            for i, pp in enumerate(pairs):
                sl = ALL_PAIRS[pp]
                scale = jnp.where(masks[0], alpha[2 * i], alpha[2 * i + 1])
                o_ref[:, sl] = o_ref[:, sl] * scale + (_nn(pb[2 * i], vmk[pp][0]) + _nn(pb[2 * i + 1], vmk[pp][1]))
            return out

        def step(kb, carry, diag):
            k0 = pl.multiple_of(kb * TK, TK)
            out = []
            for g in range(0, len(ALL_PAIRS), MLA_GROUP):
                out += group(k0, list(range(g, g + MLA_GROUP)), carry[4 * g:4 * (g + MLA_GROUP)], diag)
            return tuple(out)

        neg = jnp.full((TQ, 1), NEG_BIG, F32)
        zero = jnp.zeros((TQ, 1), F32)
        carry = step(qi, (neg, zero) * (2 * len(ALL_PAIRS)), True)
        carry = lax.fori_loop(0, qi, lambda i, c: step(qi - 1 - i, c, False), carry)
        lane = _iota2((TQ, LANES), 1)
        for pp, sl in enumerate(ALL_PAIRS):
            m0, l0, m1, l1 = carry[4 * pp:4 * pp + 4]
            o_ref[:, sl] = o_ref[:, sl] * jnp.where(masks[0], 1.0 / l0, 1.0 / l1)
            lse_ref[:, sl] = jnp.where(lane == 0, m0 + jnp.log(l0), jnp.where(lane == 1, m1 + jnp.log(l1), 0.0))

    def rows(w):
        return pl.BlockSpec((TQ, w), lambda b, i: (b * nq + i, 0))

    def seq(w):
        return pl.BlockSpec((S, w), lambda b, i: (b, 0))

    return pl.pallas_call(
        body, name="mla_fwd", grid=(B, nq),
        in_specs=[rows(ATT_W), rows(ROPE_W), seq(ATT_W), seq(LANES), seq(ATT_W)],
        out_specs=[rows(ATT_W), rows(ATT_W)],
        out_shape=[jax.ShapeDtypeStruct((T, ATT_W), F32)] * 2,
        compiler_params=_params(("arbitrary", "arbitrary")),
    )(qn, qr, kn, krt, vm)


def mla_bwd_call(qn, qr, kn, krt, vm, o, lse, do, B, S, ex=None):
    T = B * S
    TQ, TK = ATT_TQ, ATT_TK
    nq = S // TQ

    def body(qn_ref, qr_ref, kn_ref, kr_ref, v_ref, o_ref, lse_ref, do_ref,
             dqn_ref, dqr_ref, dkn_ref, dv_ref, dkr_ref):
        qi = pl.program_id(1)

        @pl.when(qi == 0)
        def _():
            dkn_ref[...] = jnp.zeros_like(dkn_ref)
            dv_ref[...] = jnp.zeros_like(dv_ref)
            dkr_ref[...] = jnp.zeros_like(dkr_ref)

        masks = _head_masks()
        qcat = _mla_queries(qn_ref, qr_ref, masks)
        dom, dsum, lse = [], [], []
        for sl in ALL_PAIRS:
            do = do_ref[:, sl]
            dom.append(_pair(do, masks))
            d8 = _rows8(_lane_selector(HEAD_DIM), do * o_ref[:, sl])
            l8 = _rows8(_lane_selector(1), lse_ref[:, sl])
            dsum.append([_row_of(d8, j) for j in range(2)])
            lse.append([_row_of(l8, j) for j in range(2)])
        row = _iota2((TK, TQ), 0)
        col = _iota2((TK, TQ), 1)
        vis = row <= col
        dqn_ref[...] = jnp.zeros_like(dqn_ref)
        dqr_ref[...] = jnp.zeros_like(dqr_ref)

        def group(k0, pairs, diag):
            heads = [(pp, j) for pp in pairs for j in range(2)]
            n = range(len(heads))
            krv = kr_ref[pl.ds(k0, TK), :]
            knv = {pp: kn_ref[pl.ds(k0, TK), ALL_PAIRS[pp]] for pp in pairs}
            kcat = {pp: jnp.concatenate([knv[pp], krv], axis=1) for pp in pairs}
            v = {pp: v_ref[pl.ds(k0, TK), ALL_PAIRS[pp]] for pp in pairs}
            s = [_nt(kcat[pp], qcat[pp][j]) * MLA_SCALE for pp, j in heads]
            dp_ = [_nt(v[pp], dom[pp][j]) for pp, j in heads]
            pr = [jnp.exp(s[h] - lse[pp][j]) for h, (pp, j) in enumerate(heads)]
            if diag:
                pr = [jnp.where(vis, x, 0.0) for x in pr]
            ds = [(pr[h] * (dp_[h] - dsum[pp][j]) * MLA_SCALE).astype(BF16) for h, (pp, j) in enumerate(heads)]
            pb = [x.astype(BF16) for x in pr]
            for i, pp in enumerate(pairs):
                sl = ALL_PAIRS[pp]
                rmasks = _rope_masks(pp)
                kcat_j = [jnp.concatenate([jnp.where(masks[j], knv[pp], 0), jnp.where(rmasks[j], krv, 0)],
                                          axis=1).astype(BF16) for j in range(2)]
                dv_ref[pl.ds(k0, TK), sl] += _nn(pb[2 * i], dom[pp][0]) + _nn(pb[2 * i + 1], dom[pp][1])
                dk = _nn(ds[2 * i], qcat[pp][0]) + _nn(ds[2 * i + 1], qcat[pp][1])
                dq = _tn(ds[2 * i], kcat_j[0]) + _tn(ds[2 * i + 1], kcat_j[1])
                dqn_ref[:, sl] += dq[:, :LANES]
                dqr_ref[:, sl] += dq[:, LANES:]
                dkn_ref[pl.ds(k0, TK), sl] += dk[:, :LANES]
                dkr_ref[pl.ds(k0, TK), sl] += dk[:, LANES:]

        def step(kb, diag):
            k0 = pl.multiple_of(kb * TK, TK)
            for g in range(0, len(ALL_PAIRS), MLA_GROUP):
                group(k0, list(range(g, g + MLA_GROUP)), diag)

        step(qi, True)

        def loop(i, c):
            step(qi - 1 - i, False)
            return c

        lax.fori_loop(0, qi, loop, 0)

    def rows(w):
        return pl.BlockSpec((TQ, w), lambda b, i: (b * nq + i, 0))

    def seq(w):
        return pl.BlockSpec((S, w), lambda b, i: (b, 0))

    outs, moved = _call(
        body, ex, name="mla_bwd", grid=(B, nq),
        in_specs=[rows(ATT_W), rows(ROPE_W), seq(ATT_W), seq(LANES), seq(ATT_W), rows(ATT_W), rows(ATT_W), rows(ATT_W)],
        out_specs=[rows(ATT_W), rows(ATT_W), seq(ATT_W), seq(ATT_W), seq(ATT_W)],
        out_shape=[jax.ShapeDtypeStruct((T, ATT_W), F32)] * 5, args=(qn, qr, kn, krt, vm, o, lse, do))
    return tuple(outs) if ex is None else tuple(outs) + (moved,)


CONV_TC = 256


def _shift_down(x, n):
    return jnp.where(_iota2(x.shape, 0) >= n, pltpu.roll(x, n, 0), 0.0)


def _shift_up(x, n):
    rows = x.shape[0]
    return jnp.where(_iota2(x.shape, 0) < rows - n, pltpu.roll(x, rows - n, 0), 0.0)


def _taps(w_ref):
    return [w_ref[k:k + 1, :] for k in range(3)]


def _conv3(u, w, b):
    return w[0] * _shift_down(u, 2) + w[1] * _shift_down(u, 1) + w[2] * u + b


def conv_act_fwd_call(ug, uv, conv_w, conv_b, B, S):
    T = B * S
    nc = D_FF // CONV_TC

    def body(ug_ref, uv_ref, wg_ref, wv_ref, bg_ref, bv_ref, a_ref):
        gate = _conv3(ug_ref[...], _taps(wg_ref), bg_ref[...])
        val = _conv3(uv_ref[...], _taps(wv_ref), bv_ref[...])
        a_ref[...] =(gate * (1.0 / (1.0 + jnp.exp(-gate))) * val).astype(BF16)

    def blk(rows, off):
        return pl.BlockSpec((rows, CONV_TC), lambda b, j: (b if rows == S else 0, off + j))

    return pl.pallas_call(
        body, name="conv_act_fwd", grid=(B, nc),
        in_specs=[blk(S, 0), blk(S, 0), blk(3, 0), blk(3, nc), blk(1, 0), blk(1, nc)],
        out_specs=blk(S, 0),
        out_shape=jax.ShapeDtypeStruct((T, D_FF), BF16),
        compiler_params=_params(("parallel", "parallel")),
    )(ug, uv, conv_w, conv_w, conv_b, conv_b)


def conv_act_bwd_call(ug, uv, da, conv_w, conv_b, B, S):
    T = B * S
    nc = D_FF // CONV_TC

    def body(ug_ref, uv_ref, da_ref, wg_ref, wv_ref, bg_ref, bv_ref,
             dug_ref, duv_ref, dwg_ref, dwv_ref, dbg_ref, dbv_ref):
        @pl.when(pl.program_id(1) == 0)
        def _():
            for r in (dwg_ref, dwv_ref, dbg_ref, dbv_ref):
                r[...] = jnp.zeros_like(r)

        ug = ug_ref[...]
        uv = uv_ref[...]
        wg = _taps(wg_ref)
        wv = _taps(wv_ref)
        gate = _conv3(ug, wg, bg_ref[...])
        val = _conv3(uv, wv, bv_ref[...])
        da = da_ref[...]
        sig = 1.0 / (1.0 + jnp.exp(-gate))
        dval = da * (gate * sig)
        dgate = da * val * (sig * (1.0 + gate * (1.0 - sig)))
        for u_, d, w, du_ref, dw_ref, db_ref in ((ug, dgate, wg, dug_ref, dwg_ref, dbg_ref),
                                                 (uv, dval, wv, duv_ref, dwv_ref, dbv_ref)):
            du_ref[...] = (w[2] * d + w[1] * _shift_up(d, 1) + w[0] * _shift_up(d, 2)).astype(BF16)
            db_ref[...] += jnp.sum(d, axis=0, keepdims=True)
            dw_ref[0:1, :] += jnp.sum(d * _shift_down(u_, 2), axis=0, keepdims=True)
            dw_ref[1:2, :] += jnp.sum(d * _shift_down(u_, 1), axis=0, keepdims=True)
            dw_ref[2:3, :] += jnp.sum(d * u_, axis=0, keepdims=True)

    def blk(rows, off):
        return pl.BlockSpec((rows, CONV_TC), lambda j, b: (b if rows == S else 0, off + j))

    return pl.pallas_call(
        body, name="conv_act_bwd", grid=(nc, B),
        in_specs=[blk(S, 0), blk(S, 0), blk(S, 0), blk(3, 0), blk(3, nc), blk(1, 0), blk(1, nc)],
        out_specs=[blk(S, 0), blk(S, 0), blk(3, 0), blk(3, 0), blk(1, 0), blk(1, 0)],
        out_shape=[jax.ShapeDtypeStruct((T, D_FF), BF16), jax.ShapeDtypeStruct((T, D_FF), BF16),
                   jax.ShapeDtypeStruct((3, D_FF), F32), jax.ShapeDtypeStruct((3, D_FF), F32),
                   jax.ShapeDtypeStruct((1, D_FF), F32), jax.ShapeDtypeStruct((1, D_FF), F32)],
        compiler_params=_params(("parallel", "arbitrary")),
    )(ug, uv, da, conv_w, conv_w, conv_b, conv_b)


CHIP_MASKS = ((1, 0), (0, 1), (1, 1))


def _place():
    return lax.axis_index("x"), lax.axis_index("y"), lax.axis_index("c")


HALF_ALIGN = 32


def _any_specs(n):
    return [pl.BlockSpec(memory_space=pl.ANY)] * n


def _half_rows(r, half):
    return pl.ds(pl.multiple_of(half * (r // 2), HALF_ALIGN // 2), r // 2)


def _remote(src, dst, send_sem, recv_sem, device):
    return pltpu.make_async_remote_copy(src_ref=src, dst_ref=dst, send_sem=send_sem, recv_sem=recv_sem,
                                        device_id=device, device_id_type=MESH)


class Exchange:
    def __init__(self, ins, out_shape, sems, start, finish):
        self.ins, self.out_shape, self.sems, self.start, self.finish = list(ins), list(out_shape), list(sems), start, finish


def gather_group(shards):
    n = len(shards)
    split = [s.shape[0] % HALF_ALIGN == 0 for s in shards]

    def rows(w, half):
        return _half_rows(shards[w].shape[0], half) if split[w] else slice(None)

    def copies(ins, outs, sems):
        ici_s, ici_r, _, _, local_sems = sems
        x, y, c = _place()
        chip = 2 * x + y
        local = [pltpu.make_async_copy(ins[w], outs[w].at[chip], local_sems.at[w]) for w in range(n)]
        sends = [_remote(ins[w].at[rows(w, c)], outs[w].at[chip, rows(w, c)], ici_s.at[w, k], ici_r.at[w, k],
                         (x ^ fx, y ^ fy, c))
                 for w in range(n) for k, (fx, fy) in enumerate(CHIP_MASKS)]
        return local, sends

    def start(ins, outs, sems):
        local, sends = copies(ins, outs, sems)
        for cp in local + sends:
            cp.start()

    def finish(ins, outs, sems):
        ici_s, ici_r, d2d_s, d2d_r, _ = sems
        x, y, c = _place()
        sib = (x, y, 1 - c)
        local, sends = copies(ins, outs, sems)
        for w in range(n):
            for k, (fx, fy) in enumerate(CHIP_MASKS):
                landed = outs[w].at[2 * (x ^ fx) + (y ^ fy), rows(w, c)]
                _remote(landed, landed, ici_s.at[w, k], ici_r.at[w, k], sib).wait_recv()
                if split[w]:
                    cp = _remote(landed, landed, d2d_s.at[w, k], d2d_r.at[w, k], sib)
                    cp.start()
                    sends.append(cp)
        for w in range(n):
            for k, (fx, fy) in enumerate(CHIP_MASKS):
                if split[w]:
                    other = outs[w].at[2 * (x ^ fx) + (y ^ fy), rows(w, 1 - c)]
                    _remote(other, other, d2d_s.at[w, k], d2d_r.at[w, k], sib).wait_recv()
        for cp in sends:
            cp.wait_send()
        for cp in local:
            cp.wait()

    sems = pltpu.SemaphoreType.DMA((n, 3))
    return Exchange(shards, [jax.ShapeDtypeStruct((N_CHIPS,) + s.shape, s.dtype) for s in shards],
                    [sems, sems, sems, sems, pltpu.SemaphoreType.DMA((n,))], start, finish)


def swap_half(parts):
    n = len(parts)

    def copies(ins, outs, sems):
        x, y, c = _place()
        return [_remote(ins[w].at[:, _half_rows(parts[w].shape[1], 1 - c)], outs[w], sems[0].at[w], sems[1].at[w],
                        (x, y, 1 - c)) for w in range(n)]

    def start(ins, outs, sems):
        for cp in copies(ins, outs, sems):
            cp.start()

    def finish(ins, outs, sems):
        for cp in copies(ins, outs, sems):
            cp.wait_recv()
            cp.wait_send()

    return Exchange(parts, [jax.ShapeDtypeStruct((N_CHIPS, p.shape[1] // 2, p.shape[2]), F32) for p in parts],
                    [pltpu.SemaphoreType.DMA((n,))] * 2, start, finish)


def scatter_half(halves):
    n = len(halves)

    def copies(ins, outs, sems):
        x, y, c = _place()
        return [_remote(ins[w].at[2 * (x ^ fx) + (y ^ fy)], outs[w].at[k], sems[0].at[w, k], sems[1].at[w, k],
                        (x ^ fx, y ^ fy, c))
                for w in range(n) for k, (fx, fy) in enumerate(CHIP_MASKS)]

    def start(ins, outs, sems):
        for cp in copies(ins, outs, sems):
            cp.start()

    def finish(ins, outs, sems):
        for cp in copies(ins, outs, sems):
            cp.wait_recv()
            cp.wait_send()

    return Exchange(halves, [jax.ShapeDtypeStruct((3,) + h.shape[1:], h.dtype) for h in halves],
                    [pltpu.SemaphoreType.DMA((n, 3))] * 2, start, finish)


def swap_final(finals):
    n = len(finals)

    def copies(ins, outs, sems):
        x, y, c = _place()
        mine = [outs[w].at[_half_rows(2 * finals[w].shape[0], c)] for w in range(n)]
        local = [pltpu.make_async_copy(ins[w], mine[w], sems[2].at[w]) for w in range(n)]
        sends = [_remote(ins[w], mine[w], sems[0].at[w], sems[1].at[w], (x, y, 1 - c)) for w in range(n)]
        return local, sends

    def start(ins, outs, sems):
        local, sends = copies(ins, outs, sems)
        for cp in local + sends:
            cp.start()

    def finish(ins, outs, sems):
        x, y, c = _place()
        local, sends = copies(ins, outs, sems)
        for w in range(n):
            got = outs[w].at[_half_rows(2 * finals[w].shape[0], 1 - c)]
            _remote(got, got, sems[0].at[w], sems[1].at[w], (x, y, 1 - c)).wait_recv()
        for cp in sends:
            cp.wait_send()
        for cp in local:
            cp.wait()

    return Exchange(finals, [jax.ShapeDtypeStruct((2 * f.shape[0], f.shape[1]), F32) for f in finals],
                    [pltpu.SemaphoreType.DMA((n,))] * 3, start, finish)


def exchange_call(name, ex):
    n, m = len(ex.ins), len(ex.out_shape)

    def body(*refs):
        ins, outs, sems = refs[:n], refs[n:n + m], refs[n + m:]
        ex.start(ins, outs, sems)
        ex.finish(ins, outs, sems)

    return pl.pallas_call(body, name=name, in_specs=_any_specs(n), out_specs=_any_specs(m), out_shape=ex.out_shape,
                          scratch_shapes=ex.sems, compiler_params=_params())(*ex.ins)


def _call(body, ex, *, name, grid, in_specs, out_specs, out_shape, args, scratch_shapes=()):
    sem = ("arbitrary",) * len(grid)
    if ex is None:
        outs = pl.pallas_call(body, name=name, grid=grid, in_specs=in_specs, out_specs=out_specs, out_shape=out_shape,
                              scratch_shapes=list(scratch_shapes), compiler_params=_params(sem))(*args)
        return outs, None
    ni, no, ns = len(in_specs), len(out_specs), len(scratch_shapes)
    ne, me = len(ex.ins), len(ex.out_shape)

    def wrapped(*refs):
        own_in, ex_in = refs[:ni], refs[ni:ni + ne]
        own_out, ex_out = refs[ni + ne:ni + ne + no], refs[ni + ne + no:ni + ne + no + me]
        own_scr, ex_sems = refs[ni + ne + no + me:ni + ne + no + me + ns], refs[ni + ne + no + me + ns:]
        ids = [pl.program_id(a) for a in range(len(grid))]
        first = functools.reduce(jnp.logical_and, [i == 0 for i in ids])
        last = functools.reduce(jnp.logical_and, [i == g - 1 for i, g in zip(ids, grid)])

        @pl.when(first)
        def _():
            ex.start(ex_in, ex_out, ex_sems)

        body(*own_in, *own_out, *own_scr)

        @pl.when(last)
        def _():
            ex.finish(ex_in, ex_out, ex_sems)

    outs = pl.pallas_call(
        wrapped, name=name, grid=grid, in_specs=list(in_specs) + _any_specs(ne),
        out_specs=list(out_specs) + _any_specs(me), out_shape=list(out_shape) + ex.out_shape,
        scratch_shapes=list(scratch_shapes) + ex.sems, compiler_params=_params(sem))(*args, *ex.ins)
    return outs[:no], outs[no:]


def _row_tile(rows, cap, mult=8):
    return max(t for t in range(mult, min(rows, cap) + 1, mult) if rows % t == 0)


def add_half_call(name, part, got, where):
    _, rh, cols = got.shape
    tr = _row_tile(rh, 176, 16)
    nb = rh // tr

    def body(where_ref, p_ref, g_ref, own_ref, send_ref):
        t = p_ref[...] + g_ref[...]
        send_ref[...] = t.astype(BF16)
        chip = where_ref[1]
        own_ref[...] = p_ref[chip] + g_ref[chip]

    blk = (N_CHIPS, tr, cols)
    return pl.pallas_call(
        body, name=name,
        grid_spec=pltpu.PrefetchScalarGridSpec(
            num_scalar_prefetch=1, grid=(nb,),
            in_specs=[pl.BlockSpec(blk, lambda i, where_ref: (0, where_ref[0] * nb + i, 0)),
                      pl.BlockSpec(blk, lambda i, where_ref: (0, i, 0))],
            out_specs=[pl.BlockSpec((tr, cols), lambda i, where_ref: (i, 0)),
                       pl.BlockSpec(blk, lambda i, where_ref: (0, i, 0))]),
        out_shape=[jax.ShapeDtypeStruct((rh, cols), F32), jax.ShapeDtypeStruct(got.shape, BF16)],
        compiler_params=_params(("parallel",)),
    )(where, part, got)


def sum_chips_call(name, own, got):
    _, rh, cols = got.shape
    tr = _row_tile(rh, 176, 16)

    def body(h_ref, g_ref, o_ref):
        o_ref[...] = ((h_ref[...] + g_ref[0].astype(F32)) + g_ref[1].astype(F32)) + g_ref[2].astype(F32)

    return pl.pallas_call(
        body, name=name, grid=(rh // tr,),
        in_specs=[pl.BlockSpec((tr, cols), lambda i: (i, 0)), pl.BlockSpec((3, tr, cols), lambda i: (0, i, 0))],
        out_specs=pl.BlockSpec((tr, cols), lambda i: (i, 0)),
        out_shape=jax.ShapeDtypeStruct((rh, cols), F32),
        compiler_params=_params(("parallel",)),
    )(own, got)


def _adamw(w, g, m, v):
    m = ADAM_B1 * m + (1.0 - ADAM_B1) * g
    v = ADAM_B2 * v + (1.0 - ADAM_B2) * (g * g)
    m_hat = m / (1.0 - ADAM_B1 ** ADAM_STEP)
    v_hat = v / (1.0 - ADAM_B2 ** ADAM_STEP)
    delta = -ADAM_LR * (m_hat / (jnp.sqrt(v_hat) + ADAM_EPS) + ADAM_WD * w)
    return delta, m, v


def adamw_call(name, g, w, m, v):
    r, cols = w.shape
    tr = r if r % 8 else _row_tile(r, 256)

    def body(g_ref, w_ref, m_ref, v_ref, d_ref, nm_ref, nv_ref):
        d_ref[...], nm_ref[...], nv_ref[...] = _adamw(w_ref[...], g_ref[...], m_ref[...], v_ref[...])

    spec = pl.BlockSpec((tr, cols), lambda i: (i, 0))
    return pl.pallas_call(
        body, name=name, grid=(r // tr,), in_specs=[spec] * 4, out_specs=[spec] * 3,
        out_shape=[jax.ShapeDtypeStruct((r, cols), F32)] * 3,
        compiler_params=_params(("parallel",)),
    )(g, w, m, v)


def allsum_small_call(v):
    R = v.shape[0]

    def body(v_ref, out_ref, buf, send_sems, recv_sems):
        x, y, c = _place()
        me = 4 * x + 2 * y + c
        buf[me] = v_ref[...]
        sends = []
        for k in range(1, N_DEV):
            fx, fy, fc = (k >> 2) & 1, (k >> 1) & 1, k & 1
            cp = pltpu.make_async_remote_copy(
                src_ref=v_ref, dst_ref=buf.at[me], send_sem=send_sems.at[k - 1], recv_sem=recv_sems.at[k - 1],
                device_id=(x ^ fx, y ^ fy, c ^ fc), device_id_type=MESH)
            cp.start()
            sends.append(cp)
        for k in range(1, N_DEV):
            pltpu.make_async_remote_copy(
                src_ref=v_ref, dst_ref=buf.at[me ^ k], send_sem=send_sems.at[k - 1], recv_sem=recv_sems.at[k - 1],
                device_id=(x, y, c), device_id_type=MESH).wait_recv()
        acc = buf[0]
        for d in range(1, N_DEV):
            acc = acc + buf[d]
        out_ref[...] = acc
        for cp in sends:
            cp.wait_send()

    vm = pl.BlockSpec(memory_space=pltpu.VMEM)
    return pl.pallas_call(
        body, name="allsum_small", in_specs=[vm], out_specs=vm,
        out_shape=jax.ShapeDtypeStruct((R, LANES), F32),
        scratch_shapes=[pltpu.VMEM((N_DEV, R, LANES), F32), pltpu.SemaphoreType.DMA((N_DEV - 1,)),
                        pltpu.SemaphoreType.DMA((N_DEV - 1,))],
        compiler_params=_params(),
    )(v)


def _slab(flat, mult):
    n = flat.shape[-1]
    rows = -(-n // (LANES * mult)) * mult
    flat = jnp.pad(flat, [(0, 0)] * (flat.ndim - 1) + [(0, rows * LANES - n)])
    return flat.reshape(flat.shape[:-1] + (rows, LANES))


def full_from_chips(blocks, by_col):
    _, r, c = blocks.shape
    return blocks.transpose(1, 0, 2).reshape(r, N_CHIPS * c) if by_col else blocks.reshape(N_CHIPS * r, c)


def chips_from_full(full, by_col):
    if by_col:
        r, c = full.shape[0], full.shape[1] // N_CHIPS
        return full.reshape(r, N_CHIPS, c).transpose(1, 0, 2)
    return full.reshape(N_CHIPS, full.shape[0] // N_CHIPS, full.shape[1])


SMALL_PACK = SMALL_W + ("loss", "conv_w")
SMALL_PACK_N = {**SMALL_N, "loss": 1, "conv_w": 3 * 2 * D_FF}


def pack_small(vals):
    zero = jnp.zeros((1,), F32)
    return _slab(jnp.concatenate([vals[n].reshape(-1) if n in vals else jnp.tile(zero, SMALL_PACK_N[n])
                                  for n in SMALL_PACK]), 8)


def unpack_small(slab, shapes):
    flat = slab.reshape(-1)
    out, off = {}, 0
    for n in SMALL_PACK:
        out[n] = flat[off:off + SMALL_PACK_N[n]].reshape(shapes[n])
        off += SMALL_PACK_N[n]
    return out


def _split_heads(w, a, b):
    r = w.shape[0]
    w3 = w.reshape(r, HEADS, a + b)
    return w3[:, :, :a].reshape(r, HEADS * a), w3[:, :, a:].reshape(r, HEADS * b)


def _merge_heads(wa, wb, a, b):
    r = wa.shape[0]
    return jnp.concatenate([wa.reshape(r, HEADS, a), wb.reshape(r, HEADS, b)], axis=2).reshape(r, HEADS * (a + b))


def kernel(x, positions, g_mix, w_in, g_cq, w_uq, g_ckv, w_ukv, g_sb_out, g_mla_out, w_out, g_ffn, w_up, conv_w, conv_b, w_down, g_final, loss_target, m_g_mix, m_w_in, m_g_cq, m_w_uq, m_g_ckv, m_w_ukv, m_g_sb_out, m_g_mla_out, m_w_out, m_g_ffn, m_w_up, m_conv_w, m_conv_b, m_w_down, m_g_final, v_g_mix, v_w_in, v_g_cq, v_w_uq, v_g_ckv, v_w_ukv, v_g_sb_out, v_g_mla_out, v_w_out, v_g_ffn, v_w_up, v_conv_w, v_conv_b, v_w_down, v_g_final):
    given = dict(locals())
    B, S, _ = x.shape
    T = B * S
    w_big = {n: given[n][0] for n in BIG_W}
    m_big = {n: given["m_" + n][0] for n in BIG_W}
    v_big = {n: given["v_" + n][0] for n in BIG_W}

    first = ("w_in", "w_uq", "w_ukv")
    later = ("w_out", "w_up", "w_down", "conv_w")
    x2d = x.reshape(T, D_MODEL)
    h, got_w = rmsnorm_fwd_call("norm_mix", x2d, g_mix, ex=gather_group([w_big[n].astype(BF16) for n in first]))
    full = {n: full_from_chips(g_, BIG_SHARD[n][2]) for n, g_ in zip(first, got_w)}
    gather_later = gather_group([w_big[n] if n == "conv_w" else w_big[n].astype(BF16) for n in later])
    w_in_p = jnp.pad(full["w_in"], ((0, 0), (0, IN_COLS_PAD - IN_COLS)))
    w_uq_p = jnp.concatenate(_split_heads(full["w_uq"], HEAD_DIM, ROPE_DIM), axis=1)
    w_ukv_p = jnp.concatenate(_split_heads(full["w_ukv"], HEAD_DIM, HEAD_DIM), axis=1)

    half = ROPE_DIM // 2
    inv_freq = 1.0 / (ROPE_BASE ** (jnp.arange(half, dtype=F32) * (2.0 / ROPE_DIM)))
    cos, sin = rope_tab_call(positions.reshape(T, 1), jnp.tile(inv_freq, LANES // half).reshape(1, LANES))
    p = matmul_call("proj_in", h, w_in_p, "nn", tn=IN_COLS_PAD // 2)
    qn, qr, kn, vm, krt, cqn, ckvn = mla_prep_fwd_call(p, cos, sin, g_cq, g_ckv, w_uq_p, w_ukv_p)
    o_sb, lt_sb, got_w = sb_fwd_call(p, B, S, ex=gather_later)
    w_up4 = got_w[1]
    full.update({n: full_from_chips(g_, BIG_SHARD[n][2]) for n, g_ in zip(later, got_w) if n != "w_up"})
    conv_w_full = full["conv_w"]
    o_mla, lse = mla_fwd_call(qn, qr, kn, krt, vm, B, S)
    o_cat = outnorm_fwd_call(o_sb, o_mla, g_sb_out, g_mla_out)
    x1 = matmul_call("proj_out", o_cat, full["w_out"], "nn", res=x2d)
    hn = rmsnorm_fwd_call("norm_ffn", x1, g_ffn)
    u_g, u_v = ffn_up_call(hn, w_up4)
    act = conv_act_fwd_call(u_g, u_v, conv_w_full, conv_b, B, S)
    x2 = matmul_call("ffn_down", act, full["w_down"], "nn", res=x1)
    dx2, loss_row, dg_final = final_loss_call(x2, g_final.reshape(1, D_MODEL), loss_target.reshape(T, D_MODEL))

    xi, yi, ci = _place()
    chip = (2 * xi + yi).astype(jnp.int32).reshape(1)
    where = jnp.stack([ci, 2 * xi + yi]).astype(jnp.int32)

    def add_halves(names, parts, sib_rows):
        return [add_half_call("add_half_" + n, p_, s_, where) for n, p_, s_ in zip(names, parts, sib_rows)]

    def sum_chips(names, halves, from_chips):
        return [sum_chips_call("sum_chips_" + n, h_[0], f_) for n, h_, f_ in zip(names, halves, from_chips)]

    ffn_w = ("w_down", "w_up")
    parts_ffn = [chips_from_full(wgrad_call("wgrad_down", act, dx2, tn=512), False)]
    da = matmul_call("ffn_down_bwd", dx2, full["w_down"], "nt", tn=D_FF // 2)
    du_g, du_v, dcw_g, dcw_v, dcb_g, dcb_v = conv_act_bwd_call(u_g, u_v, da, conv_w_full, conv_b, B, S)
    parts_ffn.append(wgrad_up_call(hn, du_g, du_v))
    dhn, sib_ffn = ffn_up_bwd_call(du_g, du_v, w_up4, swap_half(parts_ffn))
    dx1, dg_ffn = rmsnorm_bwd_call("norm_ffn_bwd", x1, g_ffn, dhn, dx2)
    parts_out = [chips_from_full(wgrad_call("wgrad_out", o_cat, dx1), False)]
    do_cat = matmul_call("proj_out_bwd", dx1, full["w_out"], "nt")
    do_sb, do_mla, dg_sb_out, dg_mla_out, sib_out = outnorm_bwd_call(
        o_sb, o_mla, g_sb_out, g_mla_out, do_cat, ex=swap_half(parts_out))
    early = ffn_w + ("w_out",)
    halves = add_halves(early, parts_ffn + parts_out, list(sib_ffn) + list(sib_out))
    dq_sb, dk_sb, dv_sb, from_chips = sb_bwd_call(p, lt_sb, do_sb, B, S, ex=scatter_half([h_[1] for h_ in halves]))
    finals = sum_chips(early, halves, from_chips)
    dqn, dqr4, dkn, dvm, dkrt4, done = mla_bwd_call(qn, qr, kn, krt, vm, o_mla, lse, do_mla, B, S, ex=swap_final(finals))
    grads = dict(zip(early, done))
    dcq, dckvr, dq_cat, dkv_cat, dg_cq, dg_ckv = mla_prep_bwd_call(
        p, cos, sin, g_cq, g_ckv, w_uq_p, w_ukv_p, dqn, dqr4, dkn, dvm, dkrt4)
    dw_uq_p = wgrad_call("wgrad_uq", cqn, dq_cat)
    dw_ukv_p = wgrad_call("wgrad_ukv", ckvn, dkv_cat)
    dp = (dq_sb, dk_sb, dv_sb, dcq, dckvr)
    late = ("w_uq", "w_ukv", "w_in")
    parts_late = [chips_from_full(g_, True) for g_ in (
        _merge_heads(dw_uq_p[:, :ATT_W], dw_uq_p[:, ATT_W:], HEAD_DIM, ROPE_DIM),
        _merge_heads(dw_ukv_p[:, :ATT_W], dw_ukv_p[:, ATT_W:], HEAD_DIM, HEAD_DIM),
        wgrad_in_call(h, dp)[:, :IN_COLS])]
    dh, sib_late = proj_in_bwd_call(dp, w_in_p, swap_half(parts_late))
    halves = add_halves(late, parts_late, sib_late)
    grad_x, dg_mix, from_chips = rmsnorm_bwd_call(
        "norm_mix_bwd", x2d, g_mix, dh, dx1, ex=scatter_half([h_[1] for h_ in halves]))
    finals = sum_chips(late, halves, from_chips)
    grads.update(zip(late, exchange_call("swap_final_late", swap_final(finals))))

    shapes = {n: given[n].shape for n in SMALL_W}
    shapes.update(loss=(), conv_w=(3, 2 * D_FF))
    small_g = {"g_mix": dg_mix, "g_cq": dg_cq, "g_ckv": dg_ckv, "g_sb_out": dg_sb_out, "g_mla_out": dg_mla_out,
               "g_ffn": dg_ffn, "conv_b": jnp.concatenate([dcb_g, dcb_v], axis=1), "g_final": dg_final,
               "loss": loss_row[0, :1], "conv_w": jnp.concatenate([dcw_g, dcw_v], axis=1)}
    gs_slab = allsum_small_call(pack_small(small_g))
    small_in = [pack_small({n: given[pre + n] for n in SMALL_W}) for pre in ("", "m_", "v_")]
    small_out = [unpack_small(s, shapes) for s in (gs_slab,) + tuple(adamw_call("adamw_small", gs_slab, *small_in))]
    cw_cols = BIG_SHARD["conv_w"][1]
    grads["conv_w"] = lax.dynamic_slice_in_dim(small_out[0]["conv_w"], chip[0] * cw_cols, cw_cols, axis=1)

    big_out = {n: (grads[n],) + tuple(adamw_call("adamw_" + n, grads[n], w_big[n], m_big[n], v_big[n])) for n in BIG_W}
    weights = ("g_mix", "w_in", "g_cq", "w_uq", "g_ckv", "w_ukv", "g_sb_out", "g_mla_out", "w_out", "g_ffn",
               "w_up", "conv_w", "conv_b", "w_down", "g_final")
    outs = [small_out[0]["loss"], grad_x.reshape(B, S, D_MODEL)]
    for k in range(4):
        for n in weights:
            outs.append(big_out[n][k][None] if n in BIG_W else small_out[k][n])
    return tuple(outs)
```

```python
import functools

import jax
import jax.numpy as jnp
from jax import lax
from jax.experimental import pallas as pl
from jax.experimental.pallas import tpu as pltpu

F32 = jnp.float32
BF16 = jnp.bfloat16
MESH = pl.DeviceIdType.MESH

D_MODEL = 1024
HEADS = 8
HEAD_DIM = 64
ATT_W = HEADS * HEAD_DIM
ROPE_DIM = 32
ROPE_W = HEADS * ROPE_DIM
QK_DIM = HEAD_DIM + ROPE_DIM
Q_RANK = 384
KV_RANK = 256
D_FF = 2816
IN_COLS = 2208
IN_COLS_PAD = 2304
EPS = 1e-6
ROPE_BASE = 10000.0
SB_SCALE = HEAD_DIM ** -0.5
MLA_SCALE = QK_DIM ** -0.5
LANES = 128
N_CHIPS = 4
N_DEV = 8
VMEM_LIMIT = 48 * 1024 * 1024
ATT_TQ = 256
ATT_TK = 256
ATT_PAIRS = 4
PAIR_LANES = [slice(i * LANES, (i + 1) * LANES) for i in range(ATT_PAIRS)]
SB_BWD_GROUP = 2
SB_FWD_GROUP = 4
MLA_GROUP = 4
NEG_BIG = -1e30

ADAM_LR = 0.001
ADAM_B1 = 0.9
ADAM_B2 = 0.999
ADAM_EPS = 1e-08
ADAM_WD = 0.01
ADAM_STEP = 10

BIG_W = ("w_in", "w_uq", "w_ukv", "w_out", "w_up", "conv_w", "w_down")
BIG_SHARD = {
    "w_in": (D_MODEL, IN_COLS // 4, True),
    "w_uq": (Q_RANK, HEADS * QK_DIM // 4, True),
    "w_ukv": (KV_RANK, 2 * ATT_W // 4, True),
    "w_out": (2 * ATT_W // 4, D_MODEL, False),
    "w_up": (D_MODEL, 2 * D_FF // 4, True),
    "conv_w": (3, 2 * D_FF // 4, True),
    "w_down": (D_FF // 4, D_MODEL, False),
}
SMALL_W = ("g_mix", "g_cq", "g_ckv", "g_sb_out", "g_mla_out", "g_ffn", "conv_b", "g_final")
SMALL_N = {"g_mix": D_MODEL, "g_cq": Q_RANK, "g_ckv": KV_RANK, "g_sb_out": ATT_W, "g_mla_out": ATT_W,
           "g_ffn": D_MODEL, "conv_b": 2 * D_FF, "g_final": D_MODEL}


def _params(sem=None, **kw):
    return pltpu.CompilerParams(dimension_semantics=sem, vmem_limit_bytes=VMEM_LIMIT, **kw)


def _dot(a, b, dims):
    return lax.dot_general(a, b, (dims, ((), ())), preferred_element_type=F32)


def _nn(a, b):
    return _dot(a, b, ((1,), (0,)))


def _nt(a, b):
    return _dot(a, b, ((1,), (1,)))


def _tn(a, b):
    return _dot(a, b, ((0,), (0,)))


def _split2(x):
    hi = x.astype(BF16)
    lo = (x - hi.astype(F32)).astype(BF16)
    return hi, lo


def _split3(x):
    hi = x.astype(BF16)
    r1 = x - hi.astype(F32)
    mid = r1.astype(BF16)
    return hi, mid, (r1 - mid.astype(F32)).astype(BF16)


def _rms_r(x, d):
    return lax.rsqrt(jnp.sum(x * x, axis=-1, keepdims=True) * (1.0 / d) + EPS)


def _rms_bwd(x, g, dy, d):
    r = _rms_r(x, d)
    xhat = x * r
    gy = dy * g
    dx = r * (gy - xhat * (jnp.sum(xhat * gy, axis=-1, keepdims=True) * (1.0 / d)))
    return dx, jnp.sum(dy * xhat, axis=0, keepdims=True)


def _rot(x):
    lane = lax.broadcasted_iota(jnp.int32, x.shape, x.ndim - 1)
    n = x.shape[-1]
    return jnp.where((lane & 31) < 16, pltpu.roll(x, n - 16, x.ndim - 1), pltpu.roll(x, 16, x.ndim - 1))


def _fold4(x):
    return x + pltpu.roll(x, 32, 1) + pltpu.roll(x, 64, 1) + pltpu.roll(x, 96, 1)


def matmul_call(name, a, b, mode, out_dtype=F32, res=None, tm=512, tn=None, ex=None):
    M, K = a.shape
    N = b.shape[1] if mode == "nn" else b.shape[0]
    tn = N if tn is None else tn
    assert M % tm == 0 and N % tn == 0

    def body(*refs):
        if res is None:
            a_ref, b_ref, o_ref = refs
        else:
            a_ref, b_ref, r_ref, o_ref = refs
        av = a_ref[...].astype(BF16)
        bv = b_ref[...].astype(BF16)
        acc = _nn(av, bv) if mode == "nn" else _nt(av, bv)
        if res is not None:
            acc = r_ref[...] + acc
        o_ref[...] = acc.astype(out_dtype)

    in_specs = [pl.BlockSpec((tm, K), lambda j, i: (i, 0))]
    if mode == "nn":
        in_specs.append(pl.BlockSpec((K, tn), lambda j, i: (0, j)))
    else:
        in_specs.append(pl.BlockSpec((tn, K), lambda j, i: (j, 0)))
    args = [a, b]
    if res is not None:
        in_specs.append(pl.BlockSpec((tm, tn), lambda j, i: (i, j)))
        args.append(res)
    outs, moved = _call(body, ex, name=name, grid=(N // tn, M // tm), in_specs=in_specs,
                        out_specs=[pl.BlockSpec((tm, tn), lambda j, i: (i, j))],
                        out_shape=[jax.ShapeDtypeStruct((M, N), out_dtype)], args=args)
    return outs[0] if ex is None else (outs[0], moved)


def wgrad_call(name, a, b, tn=None, tt=512, by_chip=False):
    T, M = a.shape
    N = b.shape[1]
    tn = N if tn is None else tn
    assert T % tt == 0 and N % tn == 0
    if by_chip:
        out_spec = pl.BlockSpec((None, M, tn), lambda j, t: (j, 0, 0))
        out_shape = jax.ShapeDtypeStruct((N // tn, M, tn), F32)
    else:
        out_spec = pl.BlockSpec((M, tn), lambda j, t: (0, j))
        out_shape = jax.ShapeDtypeStruct((M, N), F32)

    def body(a_ref, b_ref, o_ref):
        @pl.when(pl.program_id(1) == 0)
        def _():
            o_ref[...] = jnp.zeros_like(o_ref)

        o_ref[...] += _tn(a_ref[...].astype(BF16), b_ref[...].astype(BF16))

    return pl.pallas_call(
        body, name=name, grid=(N // tn, T // tt),
        in_specs=[pl.BlockSpec((tt, M), lambda j, t: (t, 0)), pl.BlockSpec((tt, tn), lambda j, t: (t, j))],
        out_specs=out_spec, out_shape=out_shape,
        compiler_params=_params(("parallel", "arbitrary")),
    )(a, b)


UP_COLS = 2 * D_FF // N_CHIPS


def ffn_up_call(hn, w4, tm=512):
    T, K = hn.shape

    def body(a_ref, wg_ref, wv_ref, ug_ref, uv_ref):
        a = a_ref[...]
        ug_ref[...] = _nn(a, wg_ref[...])
        uv_ref[...] = _nn(a, wv_ref[...])

    out = pl.BlockSpec((tm, UP_COLS), lambda j, i: (i, j))
    return pl.pallas_call(
        body, name="ffn_up", grid=(2, T // tm),
        in_specs=[pl.BlockSpec((tm, K), lambda j, i: (i, 0)),
                  pl.BlockSpec((None, K, UP_COLS), lambda j, i: (j, 0, 0)),
                  pl.BlockSpec((None, K, UP_COLS), lambda j, i: (2 + j, 0, 0))],
        out_specs=[out, out], out_shape=[jax.ShapeDtypeStruct((T, D_FF), F32)] * 2,
        compiler_params=_params(("parallel", "parallel")),
    )(hn, w4, w4)


def ffn_up_bwd_call(du_g, du_v, w4, ex, tm=512, tn=512):
    T = du_g.shape[0]
    N = w4.shape[1]

    def body(g_ref, v_ref, w_ref, o_ref):
        acc = _nt(g_ref[:, :UP_COLS], w_ref[0]) + _nt(g_ref[:, UP_COLS:], w_ref[1])
        o_ref[...] = acc + _nt(v_ref[:, :UP_COLS], w_ref[2]) + _nt(v_ref[:, UP_COLS:], w_ref[3])

    row = pl.BlockSpec((tm, D_FF), lambda j, i: (i, 0))
    outs, moved = _call(body, ex, name="ffn_up_bwd", grid=(N // tn, T // tm),
                        in_specs=[row, row, pl.BlockSpec((N_CHIPS, tn, UP_COLS), lambda j, i: (0, j, 0))],
                        out_specs=[pl.BlockSpec((tm, tn), lambda j, i: (i, j))],
                        out_shape=[jax.ShapeDtypeStruct((T, N), F32)], args=(du_g, du_v, w4))
    return outs[0], moved


def wgrad_up_call(hn, du_g, du_v, tt=512):
    T, M = hn.shape

    def body(a_ref, g_ref, v_ref, o_ref):
        @pl.when(pl.program_id(1) == 0)
        def _():
            o_ref[...] = jnp.zeros_like(o_ref)

        a = a_ref[...]
        o_ref[0] += _tn(a, g_ref[...])
        o_ref[1] += _tn(a, v_ref[...])

    col = pl.BlockSpec((tt, UP_COLS), lambda j, t: (t, j))
    out = pl.pallas_call(
        body, name="wgrad_up", grid=(2, T // tt),
        in_specs=[pl.BlockSpec((tt, M), lambda j, t: (t, 0)), col, col],
        out_specs=pl.BlockSpec((2, None, M, UP_COLS), lambda j, t: (0, j, 0, 0)),
        out_shape=jax.ShapeDtypeStruct((2, 2, M, UP_COLS), F32),
        compiler_params=_params(("parallel", "arbitrary")),
    )(hn, du_g, du_v)
    return out.reshape(N_CHIPS, M, UP_COLS)


IN_PIECES = ((0, ATT_W), (ATT_W, ATT_W), (2 * ATT_W, ATT_W), (3 * ATT_W, Q_RANK), (3 * ATT_W + Q_RANK, Q_RANK))


def _piece_specs(rows, index):
    return [pl.BlockSpec((rows, w), functools.partial(index, off // w)) for off, w in IN_PIECES]


def proj_in_bwd_call(pieces, w_in_p, ex, tm=512):
    T = pieces[0].shape[0]
    N = w_in_p.shape[0]
    n = len(pieces)

    def body(*refs):
        o_ref = refs[2 * n]
        acc = _nt(refs[0][...].astype(BF16), refs[n][...])
        for i in range(1, n):
            acc = acc + _nt(refs[i][...].astype(BF16), refs[n + i][...])
        o_ref[...] = acc

    outs, moved = _call(body, ex, name="proj_in_bwd", grid=(T // tm,),
                        in_specs=_piece_specs(tm, lambda c, i: (i, 0)) + _piece_specs(N, lambda c, i: (0, c)),
                        out_specs=[pl.BlockSpec((tm, N), lambda i: (i, 0))],
                        out_shape=[jax.ShapeDtypeStruct((T, N), F32)], args=tuple(pieces) + (w_in_p,) * n)
    return outs[0], moved


def wgrad_in_call(h, pieces, tt=512):
    T, M = h.shape
    n = len(pieces)

    def body(*refs):
        a_ref, o_ref = refs[0], refs[n + 1]

        @pl.when(pl.program_id(0) == 0)
        def _():
            o_ref[...] = jnp.zeros_like(o_ref)

        a = a_ref[...]
        for i, (off, w) in enumerate(IN_PIECES):
            o_ref[:, off:off + w] += _tn(a, refs[1 + i][...].astype(BF16))

    return pl.pallas_call(
        body, name="wgrad_in", grid=(T // tt,),
        in_specs=[pl.BlockSpec((tt, M), lambda t: (t, 0))] + [pl.BlockSpec((tt, w), lambda t: (t, 0)) for _, w in IN_PIECES],
        out_specs=pl.BlockSpec((M, IN_COLS_PAD), lambda t: (0, 0)),
        out_shape=jax.ShapeDtypeStruct((M, IN_COLS_PAD), F32),
        compiler_params=_params(("arbitrary",)),
    )(h, *pieces)


def rmsnorm_fwd_call(name, x, g, tm=512, ex=None):
    T, d = x.shape

    def body(x_ref, g_ref, o_ref):
        x = x_ref[...]
        o_ref[...] = ((x * _rms_r(x, d)) * g_ref[...]).astype(BF16)

    row = pl.BlockSpec((tm, d), lambda i: (i, 0))
    outs, moved = _call(body, ex, name=name, grid=(T // tm,), in_specs=[row, pl.BlockSpec((1, d), lambda i: (0, 0))],
                        out_specs=[row], out_shape=[jax.ShapeDtypeStruct((T, d), BF16)], args=(x, g))
    return outs[0] if ex is None else (outs[0], moved)


def rmsnorm_bwd_call(name, x, g, dy, res, tm=512, ex=None):
    T, d = x.shape

    def body(x_ref, g_ref, dy_ref, r_ref, dx_ref, dg_ref):
        @pl.when(pl.program_id(0) == 0)
        def _():
            dg_ref[...] = jnp.zeros_like(dg_ref)

        dx, dg = _rms_bwd(x_ref[...], g_ref[...], dy_ref[...], d)
        dx_ref[...] = r_ref[...] + dx
        dg_ref[...] += dg

    row = pl.BlockSpec((tm, d), lambda i: (i, 0))
    vec = pl.BlockSpec((1, d), lambda i: (0, 0))
    outs, moved = _call(body, ex, name=name, grid=(T // tm,), in_specs=[row, vec, row, row], out_specs=[row, vec],
                        out_shape=[jax.ShapeDtypeStruct((T, d), F32), jax.ShapeDtypeStruct((1, d), F32)],
                        args=(x, g, dy, res))
    return tuple(outs) if ex is None else tuple(outs) + (moved,)


def outnorm_fwd_call(o_sb, o_mla, g_sb, g_mla, tm=512):
    T = o_sb.shape[0]

    def body(a_ref, b_ref, ga_ref, gb_ref, o_ref):
        a = a_ref[...]
        b = b_ref[...]
        ya = (a * _rms_r(a, ATT_W)) * ga_ref[...]
        yb = (b * _rms_r(b, ATT_W)) * gb_ref[...]
        o_ref[...] = jnp.concatenate([ya, yb], axis=1).astype(BF16)

    row = pl.BlockSpec((tm, ATT_W), lambda i: (i, 0))
    vec = pl.BlockSpec((1, ATT_W), lambda i: (0, 0))
    return pl.pallas_call(
        body, name="outnorm_fwd", grid=(T // tm,), in_specs=[row, row, vec, vec],
        out_specs=pl.BlockSpec((tm, 2 * ATT_W), lambda i: (i, 0)),
        out_shape=jax.ShapeDtypeStruct((T, 2 * ATT_W), BF16),
        compiler_params=_params(("parallel",)),
    )(o_sb, o_mla, g_sb, g_mla)


def outnorm_bwd_call(o_sb, o_mla, g_sb, g_mla, do_cat, tm=512, ex=None):
    T = o_sb.shape[0]

    def body(a_ref, b_ref, ga_ref, gb_ref, d_ref, da_ref, db_ref, dga_ref, dgb_ref):
        @pl.when(pl.program_id(0) == 0)
        def _():
            dga_ref[...] = jnp.zeros_like(dga_ref)
            dgb_ref[...] = jnp.zeros_like(dgb_ref)

        d = d_ref[...]
        da, dga = _rms_bwd(a_ref[...], ga_ref[...], d[:, :ATT_W], ATT_W)
        db, dgb = _rms_bwd(b_ref[...], gb_ref[...], d[:, ATT_W:], ATT_W)
        da_ref[...] = da
        db_ref[...] = db
        dga_ref[...] += dga
        dgb_ref[...] += dgb

    row = pl.BlockSpec((tm, ATT_W), lambda i: (i, 0))
    vec = pl.BlockSpec((1, ATT_W), lambda i: (0, 0))
    outs, moved = _call(
        body, ex, name="outnorm_bwd", grid=(T // tm,),
        in_specs=[row, row, vec, vec, pl.BlockSpec((tm, 2 * ATT_W), lambda i: (i, 0))],
        out_specs=[row, row, vec, vec],
        out_shape=[jax.ShapeDtypeStruct((T, ATT_W), F32), jax.ShapeDtypeStruct((T, ATT_W), F32),
                   jax.ShapeDtypeStruct((1, ATT_W), F32), jax.ShapeDtypeStruct((1, ATT_W), F32)],
        args=(o_sb, o_mla, g_sb, g_mla, do_cat))
    return tuple(outs) if ex is None else tuple(outs) + (moved,)


def final_loss_call(x2, g, target, tm=512):
    T, d = x2.shape

    def body(x_ref, g_ref, t_ref, dx_ref, dxb_ref, loss_ref, dg_ref):
        @pl.when(pl.program_id(0) == 0)
        def _():
            loss_ref[...] = jnp.zeros_like(loss_ref)
            dg_ref[...] = jnp.zeros_like(dg_ref)

        x = x_ref[...]
        g = g_ref[...]
        y = (x * _rms_r(x, d)) * g
        err = y - t_ref[...]
        loss_ref[...] += jnp.sum(jnp.sum(err * err, axis=1, keepdims=True), axis=0, keepdims=True) * (0.5 / d)
        dx, dg = _rms_bwd(x, g, err * (1.0 / d), d)
        dx_ref[...] = dx
        dxb_ref[...] = dx.astype(BF16)
        dg_ref[...] += dg

    row = pl.BlockSpec((tm, d), lambda i: (i, 0))
    vec = pl.BlockSpec((1, d), lambda i: (0, 0))
    return pl.pallas_call(
        body, name="final_loss", grid=(T // tm,), in_specs=[row, vec, row],
        out_specs=[row, row, pl.BlockSpec((1, LANES), lambda i: (0, 0)), vec],
        out_shape=[jax.ShapeDtypeStruct((T, d), F32), jax.ShapeDtypeStruct((T, d), BF16),
                   jax.ShapeDtypeStruct((1, LANES), F32), jax.ShapeDtypeStruct((1, d), F32)],
        compiler_params=_params(("arbitrary",)),
    )(x2, g, target)


def rope_tab_call(pos, inv_freq, tm=512):
    T = pos.shape[0]

    def body(p_ref, f_ref, c_ref, s_ref):
        ang = p_ref[...].astype(F32) * f_ref[...]
        lane = lax.broadcasted_iota(jnp.int32, ang.shape, 1)
        sn = jnp.sin(ang)
        c_ref[...] = jnp.cos(ang)
        s_ref[...] = jnp.where((lane & 31) < 16, -sn, sn)

    row = pl.BlockSpec((tm, LANES), lambda i: (i, 0))
    return pl.pallas_call(
        body, name="rope_tab", grid=(T // tm,),
        in_specs=[pl.BlockSpec((tm, 1), lambda i: (i, 0)), pl.BlockSpec((1, LANES), lambda i: (0, 0))],
        out_specs=[row, row],
        out_shape=[jax.ShapeDtypeStruct((T, LANES), F32)] * 2,
        compiler_params=_params(("parallel",)),
    )(pos, inv_freq)


def mla_prep_fwd_call(p, cos, sin, g_cq, g_ckv, w_uq_p, w_ukv_p, tm=512):
    T = p.shape[0]

    def body(cq_ref, ckvr_ref, c_ref, s_ref, gq_ref, gkv_ref, wq_ref, wkv_ref,
             qn_ref, qr_ref, kn_ref, vm_ref, krt_ref, cqn_ref, ckvn_ref):
        c = c_ref[...]
        s = s_ref[...]
        cq = cq_ref[...]
        cqn = ((cq * _rms_r(cq, Q_RANK)) * gq_ref[...]).astype(BF16)
        cqn_ref[...] = cqn
        q = _nn(cqn, wq_ref[...])
        qn_ref[...] = q[:, :ATT_W].astype(BF16)
        for g in range(ROPE_W // LANES):
            qr = q[:, ATT_W + g * LANES:ATT_W + (g + 1) * LANES]
            qr_ref[:, g * LANES:(g + 1) * LANES] = (qr * c + _rot(qr) * s).astype(BF16)
        ckvr = ckvr_ref[...]
        ckv = ckvr[:, :KV_RANK]
        ckvn = ((ckv * _rms_r(ckv, KV_RANK)) * gkv_ref[...]).astype(BF16)
        ckvn_ref[...] = ckvn
        kv = _nn(ckvn, wkv_ref[...])
        kn_ref[...] = kv[:, :ATT_W].astype(BF16)
        vm_ref[...] = kv[:, ATT_W:].astype(BF16)
        kr = _fold4(ckvr[:, KV_RANK:])
        krt_ref[...] = (kr * c + _rot(kr) * s).astype(BF16)

    def row(w, j=0):
        return pl.BlockSpec((tm, w), lambda i: (i, j))

    def full(a):
        return pl.BlockSpec(a.shape, lambda i: (0, 0))

    return pl.pallas_call(
        body, name="mla_prep_fwd", grid=(T // tm,),
        in_specs=[row(Q_RANK, 4), row(Q_RANK, 5), row(LANES), row(LANES), full(g_cq), full(g_ckv),
                  full(w_uq_p), full(w_ukv_p)],
        out_specs=[row(ATT_W), row(ROPE_W), row(ATT_W), row(ATT_W), row(LANES), row(Q_RANK), row(KV_RANK)],
        out_shape=[jax.ShapeDtypeStruct((T, w), BF16) for w in (ATT_W, ROPE_W, ATT_W, ATT_W, LANES, Q_RANK, KV_RANK)],
        compiler_params=_params(("parallel",)),
    )(p, p, cos, sin, g_cq, g_ckv, w_uq_p, w_ukv_p)


def mla_prep_bwd_call(p, cos, sin, g_cq, g_ckv, w_uq_p, w_ukv_p, dqn, dqr4, dkn, dvm, dkrt4, tm=512):
    T = p.shape[0]

    def body(cq_ref, ckvr_ref, c_ref, s_ref, gq_ref, gkv_ref, wq_ref, wkv_ref,
             dqn_ref, dqr4_ref, dkn_ref, dvm_ref, dkrt4_ref,
             dcq_ref, dckvr_ref, dq_ref, dkv_ref, dgq_ref, dgkv_ref):
        @pl.when(pl.program_id(0) == 0)
        def _():
            dgq_ref[...] = jnp.zeros_like(dgq_ref)
            dgkv_ref[...] = jnp.zeros_like(dgkv_ref)

        c = c_ref[...]
        s = s_ref[...]
        d4 = dqr4_ref[...]
        dqr = [d4[:, :128] + d4[:, 128:256], d4[:, 256:384] + d4[:, 384:]]
        dqr = [t * c + _rot(t * s) for t in dqr]
        dq = jnp.concatenate([dqn_ref[...]] + dqr, axis=1).astype(BF16)
        dq_ref[...] = dq
        dcq, dgq = _rms_bwd(cq_ref[...], gq_ref[...], _nt(dq, wq_ref[...]), Q_RANK)
        dcq_ref[...] = dcq
        dgq_ref[...] += dgq
        dkv = jnp.concatenate([dkn_ref[...], dvm_ref[...]], axis=1).astype(BF16)
        dkv_ref[...] = dkv
        ckvr = ckvr_ref[...]
        dckv, dgkv = _rms_bwd(ckvr[:, :KV_RANK], gkv_ref[...], _nt(dkv, wkv_ref[...]), KV_RANK)
        dgkv_ref[...] += dgkv
        k4 = dkrt4_ref[...]
        dkr = _fold4(k4[:, :128] + k4[:, 128:256] + k4[:, 256:384] + k4[:, 384:])
        dkr = dkr * c + _rot(dkr * s)
        lane = lax.broadcasted_iota(jnp.int32, dkr.shape, 1)
        dckvr_ref[...] = jnp.concatenate([dckv, jnp.where(lane < ROPE_DIM, dkr, 0.0)], axis=1)

    def row(w, j=0):
        return pl.BlockSpec((tm, w), lambda i: (i, j))

    def full(a):
        return pl.BlockSpec(a.shape, lambda i: (0, 0))

    return pl.pallas_call(
        body, name="mla_prep_bwd", grid=(T // tm,),
        in_specs=[row(Q_RANK, 4), row(Q_RANK, 5), row(LANES), row(LANES), full(g_cq), full(g_ckv),
                  full(w_uq_p), full(w_ukv_p), row(ATT_W), row(ATT_W), row(ATT_W), row(ATT_W), row(ATT_W)],
        out_specs=[row(Q_RANK), row(Q_RANK), row(ATT_W + ROPE_W), row(2 * ATT_W),
                   pl.BlockSpec((1, Q_RANK), lambda i: (0, 0)), pl.BlockSpec((1, KV_RANK), lambda i: (0, 0))],
        out_shape=[jax.ShapeDtypeStruct((T, Q_RANK), F32), jax.ShapeDtypeStruct((T, Q_RANK), F32),
                   jax.ShapeDtypeStruct((T, ATT_W + ROPE_W), BF16), jax.ShapeDtypeStruct((T, 2 * ATT_W), BF16),
                   jax.ShapeDtypeStruct((1, Q_RANK), F32), jax.ShapeDtypeStruct((1, KV_RANK), F32)],
        compiler_params=_params(("arbitrary",)),
    )(p, p, cos, sin, g_cq, g_ckv, w_uq_p, w_ukv_p, dqn, dqr4, dkn, dvm, dkrt4)


def _iota2(shape, axis):
    return lax.broadcasted_iota(jnp.int32, shape, axis)


def _head_masks():
    lane = _iota2((1, LANES), 1)
    return lane < HEAD_DIM, lane >= HEAD_DIM


def _pair(x, masks, dtype=BF16):
    return [jnp.where(m, x, 0.0).astype(dtype) for m in masks]


def _log_gates(z):
    keep = jnp.maximum(z, 0.0) + jnp.log(1.0 + jnp.exp(-jnp.abs(z)))
    return z - keep, keep


def _last_row(x):
    return _row_of(x[x.shape[0] - 8:, :], 7)


def _lane_selector(group):
    return jnp.where(_iota2((16, LANES), 1) // group == _iota2((16, LANES), 0), 1.0, 0.0).astype(BF16)


def _rows8(sel_t, x):
    hi = x.astype(BF16)
    r1 = x - hi.astype(F32)
    mid = r1.astype(BF16)
    lo = (r1 - mid.astype(F32)).astype(BF16)
    return _nt(sel_t, hi) + _nt(sel_t, mid) + _nt(sel_t, lo)


def _row_of(x8, j):
    return jnp.sum(jnp.where(_iota2(x8.shape, 0) == j, x8, 0.0), axis=0, keepdims=True)


def sb_fwd_call(p, B, S, ex=None):
    T = B * S
    TQ, TK = ATT_TQ, ATT_TK
    nq = S // TQ

    def body(q_ref, k_ref, v_ref, o_ref, lt_ref):
        qi = pl.program_id(2)
        masks = _head_masks()
        qm = [_pair(q_ref[:, sl] * SB_SCALE, masks) for sl in PAIR_LANES]
        row = _iota2((TQ, TK), 0)
        col = _iota2((TQ, TK), 1)
        tri = jnp.where(row > col, 1.0, 0.0).astype(BF16)
        tri2 = jnp.concatenate([tri, tri], axis=0)
        vis = col < row
        o_ref[...] = jnp.zeros_like(o_ref)

        def group(k0, pairs, carry, diag):
            heads = [(pp, j) for pp in pairs for j in range(2)]
            n = range(len(heads))
            k = {pp: k_ref[pl.ds(k0, TK), PAIR_LANES[pp]].astype(BF16) for pp in pairs}
            vm = {pp: _pair(v_ref[pl.ds(k0, TK), PAIR_LANES[pp]], masks) for pp in pairs}
            gates = [_log_gates(_nt(qm[pp][j], k[pp])) for pp, j in heads]
            lb = [g[0] for g in gates]
            keep = [jnp.where(vis, g[1], 0.0) if diag else g[1] for g in gates]
            tail = [_nn(jnp.concatenate(_split2(keep[h]), axis=1), tri2) + carry[h] for h in n]
            a = [jnp.exp(lb[h] - tail[h]) for h in n]
            if diag:
                a = [jnp.where(vis, x, 0.0) for x in a]
            ab = [x.astype(BF16) for x in a]
            for i, pp in enumerate(pairs):
                o_ref[:, PAIR_LANES[pp]] += _nn(ab[2 * i], vm[pp][0]) + _nn(ab[2 * i + 1], vm[pp][1])
            return [carry[h] + jnp.sum(keep[h], axis=1, keepdims=True) for h in n]

        def step(kb, carry, diag):
            k0 = pl.multiple_of(kb * TK, TK)
            out = []
            for g in range(0, ATT_PAIRS, SB_FWD_GROUP):
                out += group(k0, list(range(g, g + SB_FWD_GROUP)), carry[2 * g:2 * (g + SB_FWD_GROUP)], diag)
            return tuple(out)

        zero = jnp.zeros((TQ, 1), F32)
        carry = step(qi, (zero,) * (2 * ATT_PAIRS), True)
        carry = lax.fori_loop(0, qi, lambda i, c: step(qi - 1 - i, c, False), carry)
        lane = _iota2((TQ, LANES), 1)
        for pp, sl in enumerate(PAIR_LANES):
            lt_ref[:, sl] = jnp.where(lane == 0, carry[2 * pp], jnp.where(lane == 1, carry[2 * pp + 1], 0.0))

    W = ATT_PAIRS * LANES
    qspec = pl.BlockSpec((TQ, W), lambda b, h, i: (b * nq + i, h))
    outs, moved = _call(
        body, ex, name="sb_fwd", grid=(B, HEADS // 2 // ATT_PAIRS, nq),
        in_specs=[qspec,
                  pl.BlockSpec((S, W), lambda b, h, i: (b, ATT_W // W + h)),
                  pl.BlockSpec((S, W), lambda b, h, i: (b, 2 * ATT_W // W + h))],
        out_specs=[qspec, qspec],
        out_shape=[jax.ShapeDtypeStruct((T, ATT_W), F32)] * 2, args=(p, p, p))
    return tuple(outs) if ex is None else tuple(outs) + (moved,)


def sb_bwd_call(p, lt, do, B, S, ex=None):
    T = B * S
    TQ, TK = ATT_TQ, ATT_TK
    nq = S // TQ

    def body(q_ref, k_ref, v_ref, lt_ref, do_ref, dq_ref, dk_ref, dv_ref):
        qi = pl.program_id(2)

        @pl.when(qi == 0)
        def _():
            dk_ref[...] = jnp.zeros_like(dk_ref)
            dv_ref[...] = jnp.zeros_like(dv_ref)

        masks = _head_masks()
        qm = [_pair(q_ref[:, sl] * SB_SCALE, masks) for sl in PAIR_LANES]
        dom = [_pair(do_ref[:, sl], masks) for sl in PAIR_LANES]
        start = []
        for sl in PAIR_LANES:
            l8 = _rows8(_lane_selector(1), lt_ref[:, sl])
            start += [-_row_of(l8, 0), jnp.zeros((1, TQ), F32), -_row_of(l8, 1), jnp.zeros((1, TQ), F32)]
        row = _iota2((TK, TQ), 0)
        col = _iota2((TK, TQ), 1)
        incl = jnp.where(col <= row, 1.0, 0.0).astype(BF16)
        incl2 = jnp.concatenate([incl, incl], axis=1)
        excl = jnp.where(col < row, 1.0, 0.0).astype(BF16)
        vis = row < col
        dq_ref[...] = jnp.zeros_like(dq_ref)

        def group(k0, pairs, carry, diag):
            heads = [(pp, j) for pp in pairs for j in range(2)]
            n = range(len(heads))
            kf = {pp: k_ref[pl.ds(k0, TK), PAIR_LANES[pp]] for pp in pairs}
            km = {pp: _pair(kf[pp], masks) for pp in pairs}
            v = {pp: v_ref[pl.ds(k0, TK), PAIR_LANES[pp]].astype(BF16) for pp in pairs}
            z = [_nt(kf[pp].astype(BF16), qm[pp][j]) for pp, j in heads]
            da = [_nt(v[pp], dom[pp][j]) for pp, j in heads]
            gates = [_log_gates(x) for x in z]
            lb = [g[0] for g in gates]
            keep = [jnp.where(vis, g[1], 0.0) if diag else g[1] for g in gates]
            left = [_nn(incl2, jnp.concatenate(_split2(keep[h]), axis=0)) + carry[2 * h] for h in n]
            a = [jnp.exp(lb[h] + left[h]) for h in n]
            if diag:
                a = [jnp.where(vis, x, 0.0) for x in a]
            e = [a[h] * da[h] for h in n]
            before = [_nn(excl, e[h].astype(BF16)) + carry[2 * h + 1] for h in n]
            dz = [e[h] - jnp.exp(lb[h]) * (e[h] + before[h]) for h in n]
            if diag:
                dz = [jnp.where(vis, x, 0.0) for x in dz]
            dzb = [x.astype(BF16) for x in dz]
            ab = [x.astype(BF16) for x in a]
            out = []
            for h in n:
                out += [_last_row(left[h]), _last_row(before[h]) + _last_row(e[h])]
            for i, pp in enumerate(pairs):
                sl = PAIR_LANES[pp]
                dk_ref[pl.ds(k0, TK), sl] += _nn(dzb[2 * i], qm[pp][0]) + _nn(dzb[2 * i + 1], qm[pp][1])
                dv_ref[pl.ds(k0, TK), sl] += _nn(ab[2 * i], dom[pp][0]) + _nn(ab[2 * i + 1], dom[pp][1])
                dq_ref[:, sl] += _tn(dzb[2 * i], km[pp][0]) + _tn(dzb[2 * i + 1], km[pp][1])
            return out

        def step(kb, carry, diag):
            k0 = pl.multiple_of(kb * TK, TK)
            out = []
            for g in range(0, ATT_PAIRS, SB_BWD_GROUP):
                out += group(k0, list(range(g, g + SB_BWD_GROUP)), carry[4 * g:4 * (g + SB_BWD_GROUP)], diag)
            return tuple(out)

        carry = lax.fori_loop(0, qi, lambda i, c: step(i, c, False), tuple(start))
        step(qi, carry, True)
        dq_ref[...] *= SB_SCALE

    W = ATT_PAIRS * LANES
    qspec = pl.BlockSpec((TQ, W), lambda b, h, i: (b * nq + i, h))
    sspec = pl.BlockSpec((S, W), lambda b, h, i: (b, h))
    outs, moved = _call(
        body, ex, name="sb_bwd", grid=(B, HEADS // 2 // ATT_PAIRS, nq),
        in_specs=[qspec,
                  pl.BlockSpec((S, W), lambda b, h, i: (b, ATT_W // W + h)),
                  pl.BlockSpec((S, W), lambda b, h, i: (b, 2 * ATT_W // W + h)),
                  qspec, qspec],
        out_specs=[qspec, sspec, sspec],
        out_shape=[jax.ShapeDtypeStruct((T, ATT_W), F32)] * 3, args=(p, p, p, lt, do))
    return tuple(outs) if ex is None else tuple(outs) + (moved,)


ALL_PAIRS = [slice(i * LANES, (i + 1) * LANES) for i in range(HEADS // 2)]


def _rope_masks(hp):
    grp = _iota2((1, LANES), 1) // ROPE_DIM
    return [grp == ((2 * hp + j) % 4) for j in range(2)]


def _mla_queries(qn_ref, qr_ref, masks):
    out = []
    for pp, sl in enumerate(ALL_PAIRS):
        qnv = qn_ref[:, sl]
        qrv = qr_ref[:, ALL_PAIRS[pp // 2]]
        rmasks = _rope_masks(pp)
        out.append([jnp.concatenate([jnp.where(masks[j], qnv, 0), jnp.where(rmasks[j], qrv, 0)], axis=1).astype(BF16)
                    for j in range(2)])
    return out


def mla_fwd_call(qn, qr, kn, krt, vm, B, S):
    T = B * S
    TQ, TK = ATT_TQ, ATT_TK
    nq = S // TQ

    def body(qn_ref, qr_ref, kn_ref, kr_ref, v_ref, o_ref, lse_ref):
        qi = pl.program_id(1)
        masks = _head_masks()
        qcat = _mla_queries(qn_ref, qr_ref, masks)
        row = _iota2((TQ, TK), 0)
        col = _iota2((TQ, TK), 1)
        vis = col <= row
        o_ref[...] = jnp.zeros_like(o_ref)

        def group(k0, pairs, carry, diag):
            heads = [(pp, j) for pp in pairs for j in range(2)]
            n = range(len(heads))
            krv = kr_ref[pl.ds(k0, TK), :]
            kcat = {pp: jnp.concatenate([kn_ref[pl.ds(k0, TK), ALL_PAIRS[pp]], krv], axis=1) for pp in pairs}
            vmk = {pp: _pair(v_ref[pl.ds(k0, TK), ALL_PAIRS[pp]], masks) for pp in pairs}
            s = [_nt(qcat[pp][j], kcat[pp]) * MLA_SCALE for pp, j in heads]
            if diag:
                s = [jnp.where(vis, x, NEG_BIG) for x in s]
            m_new = [jnp.maximum(carry[2 * h], jnp.max(s[h], axis=1, keepdims=True)) for h in n]
            alpha = [jnp.exp(carry[2 * h] - m_new[h]) for h in n]
            pexp = [jnp.exp(s[h] - m_new[h]) for h in n]
            out = []
            for h in n:
                out += [m_new[h], alpha[h] * carry[2 * h + 1] + jnp.sum(pexp[h], axis=1, keepdims=True)]
            pb = [x.astype(BF16) for x in pexp]
            for i, pp in enumerate(pairs):
                sl = ALL_PAIRS[pp]
                scale = jnp.where(masks[0], alpha[2 * i], alpha[2 * i + 1])
                o_ref[:, sl] = o_ref[:, sl] * scale + (_nn(pb[2 * i], vmk[pp][0]) + _nn(pb[2 * i + 1], vmk[pp][1]))
            return out

        def step(kb, carry, diag):
            k0 = pl.multiple_of(kb * TK, TK)
            out = []
            for g in range(0, len(ALL_PAIRS), MLA_GROUP):
                out += group(k0, list(range(g, g + MLA_GROUP)), carry[4 * g:4 * (g + MLA_GROUP)], diag)
            return tuple(out)

        neg = jnp.full((TQ, 1), NEG_BIG, F32)
        zero = jnp.zeros((TQ, 1), F32)
        carry = step(qi, (neg, zero) * (2 * len(ALL_PAIRS)), True)
        carry = lax.fori_loop(0, qi, lambda i, c: step(qi - 1 - i, c, False), carry)
        lane = _iota2((TQ, LANES), 1)
        for pp, sl in enumerate(ALL_PAIRS):
            m0, l0, m1, l1 = carry[4 * pp:4 * pp + 4]
            o_ref[:, sl] = o_ref[:, sl] * jnp.where(masks[0], 1.0 / l0, 1.0 / l1)
            lse_ref[:, sl] = jnp.where(lane == 0, m0 + jnp.log(l0), jnp.where(lane == 1, m1 + jnp.log(l1), 0.0))

    def rows(w):
        return pl.BlockSpec((TQ, w), lambda b, i: (b * nq + i, 0))

    def seq(w):
        return pl.BlockSpec((S, w), lambda b, i: (b, 0))

    return pl.pallas_call(
        body, name="mla_fwd", grid=(B, nq),
        in_specs=[rows(ATT_W), rows(ROPE_W), seq(ATT_W), seq(LANES), seq(ATT_W)],
        out_specs=[rows(ATT_W), rows(ATT_W)],
        out_shape=[jax.ShapeDtypeStruct((T, ATT_W), F32)] * 2,
        compiler_params=_params(("arbitrary", "arbitrary")),
    )(qn, qr, kn, krt, vm)


def mla_bwd_call(qn, qr, kn, krt, vm, o, lse, do, B, S, ex=None):
    T = B * S
    TQ, TK = ATT_TQ, ATT_TK
    nq = S // TQ

    def body(qn_ref, qr_ref, kn_ref, kr_ref, v_ref, o_ref, lse_ref, do_ref,
             dqn_ref, dqr_ref, dkn_ref, dv_ref, dkr_ref):
        qi = pl.program_id(1)

        @pl.when(qi == 0)
        def _():
            dkn_ref[...] = jnp.zeros_like(dkn_ref)
            dv_ref[...] = jnp.zeros_like(dv_ref)
            dkr_ref[...] = jnp.zeros_like(dkr_ref)

        masks = _head_masks()
        qcat = _mla_queries(qn_ref, qr_ref, masks)
        dom, dsum, lse = [], [], []
        for sl in ALL_PAIRS:
            do = do_ref[:, sl]
            dom.append(_pair(do, masks))
            d8 = _rows8(_lane_selector(HEAD_DIM), do * o_ref[:, sl])
            l8 = _rows8(_lane_selector(1), lse_ref[:, sl])
            dsum.append([_row_of(d8, j) for j in range(2)])
            lse.append([_row_of(l8, j) for j in range(2)])
        row = _iota2((TK, TQ), 0)
        col = _iota2((TK, TQ), 1)
        vis = row <= col
        dqn_ref[...] = jnp.zeros_like(dqn_ref)
        dqr_ref[...] = jnp.zeros_like(dqr_ref)

        def group(k0, pairs, diag):
            heads = [(pp, j) for pp in pairs for j in range(2)]
            n = range(len(heads))
            krv = kr_ref[pl.ds(k0, TK), :]
            knv = {pp: kn_ref[pl.ds(k0, TK), ALL_PAIRS[pp]] for pp in pairs}
            kcat = {pp: jnp.concatenate([knv[pp], krv], axis=1) for pp in pairs}
            v = {pp: v_ref[pl.ds(k0, TK), ALL_PAIRS[pp]] for pp in pairs}
            s = [_nt(kcat[pp], qcat[pp][j]) * MLA_SCALE for pp, j in heads]
            dp_ = [_nt(v[pp], dom[pp][j]) for pp, j in heads]
            pr = [jnp.exp(s[h] - lse[pp][j]) for h, (pp, j) in enumerate(heads)]
            if diag:
                pr = [jnp.where(vis, x, 0.0) for x in pr]
            ds = [(pr[h] * (dp_[h] - dsum[pp][j]) * MLA_SCALE).astype(BF16) for h, (pp, j) in enumerate(heads)]
            pb = [x.astype(BF16) for x in pr]
            for i, pp in enumerate(pairs):
                sl = ALL_PAIRS[pp]
                rmasks = _rope_masks(pp)
                kcat_j = [jnp.concatenate([jnp.where(masks[j], knv[pp], 0), jnp.where(rmasks[j], krv, 0)],
                                          axis=1).astype(BF16) for j in range(2)]
                dv_ref[pl.ds(k0, TK), sl] += _nn(pb[2 * i], dom[pp][0]) + _nn(pb[2 * i + 1], dom[pp][1])
                dk = _nn(ds[2 * i], qcat[pp][0]) + _nn(ds[2 * i + 1], qcat[pp][1])
                dq = _tn(ds[2 * i], kcat_j[0]) + _tn(ds[2 * i + 1], kcat_j[1])
                dqn_ref[:, sl] += dq[:, :LANES]
                dqr_ref[:, sl] += dq[:, LANES:]
                dkn_ref[pl.ds(k0, TK), sl] += dk[:, :LANES]
                dkr_ref[pl.ds(k0, TK), sl] += dk[:, LANES:]

        def step(kb, diag):
            k0 = pl.multiple_of(kb * TK, TK)
            for g in range(0, len(ALL_PAIRS), MLA_GROUP):
                group(k0, list(range(g, g + MLA_GROUP)), diag)

        step(qi, True)

        def loop(i, c):
            step(qi - 1 - i, False)
            return c

        lax.fori_loop(0, qi, loop, 0)

    def rows(w):
        return pl.BlockSpec((TQ, w), lambda b, i: (b * nq + i, 0))

    def seq(w):
        return pl.BlockSpec((S, w), lambda b, i: (b, 0))

    outs, moved = _call(
        body, ex, name="mla_bwd", grid=(B, nq),
        in_specs=[rows(ATT_W), rows(ROPE_W), seq(ATT_W), seq(LANES), seq(ATT_W), rows(ATT_W), rows(ATT_W), rows(ATT_W)],
        out_specs=[rows(ATT_W), rows(ATT_W), seq(ATT_W), seq(ATT_W), seq(ATT_W)],
        out_shape=[jax.ShapeDtypeStruct((T, ATT_W), F32)] * 5, args=(qn, qr, kn, krt, vm, o, lse, do))
    return tuple(outs) if ex is None else tuple(outs) + (moved,)


CONV_TC = 256


def _shift_down(x, n):
    return jnp.where(_iota2(x.shape, 0) >= n, pltpu.roll(x, n, 0), 0.0)


def _shift_up(x, n):
    rows = x.shape[0]
    return jnp.where(_iota2(x.shape, 0) < rows - n, pltpu.roll(x, rows - n, 0), 0.0)


def _taps(w_ref):
    return [w_ref[k:k + 1, :] for k in range(3)]


def _conv3(u, w, b):
    return w[0] * _shift_down(u, 2) + w[1] * _shift_down(u, 1) + w[2] * u + b


def conv_act_fwd_call(ug, uv, conv_w, conv_b, B, S):
    T = B * S
    nc = D_FF // CONV_TC

    def body(ug_ref, uv_ref, wg_ref, wv_ref, bg_ref, bv_ref, a_ref):
        gate = _conv3(ug_ref[...], _taps(wg_ref), bg_ref[...])
        val = _conv3(uv_ref[...], _taps(wv_ref), bv_ref[...])
        a_ref[...] =(gate * (1.0 / (1.0 + jnp.exp(-gate))) * val).astype(BF16)

    def blk(rows, off):
        return pl.BlockSpec((rows, CONV_TC), lambda b, j: (b if rows == S else 0, off + j))

    return pl.pallas_call(
        body, name="conv_act_fwd", grid=(B, nc),
        in_specs=[blk(S, 0), blk(S, 0), blk(3, 0), blk(3, nc), blk(1, 0), blk(1, nc)],
        out_specs=blk(S, 0),
        out_shape=jax.ShapeDtypeStruct((T, D_FF), BF16),
        compiler_params=_params(("parallel", "parallel")),
    )(ug, uv, conv_w, conv_w, conv_b, conv_b)


def conv_act_bwd_call(ug, uv, dx2, w_down, conv_w, conv_b, B, S):
    T = B * S
    nc = D_FF // CONV_TC

    def body(ug_ref, uv_ref, dx_ref, wd_ref, wg_ref, wv_ref, bg_ref, bv_ref,
             dug_ref, duv_ref, dwg_ref, dwv_ref, dbg_ref, dbv_ref):
        @pl.when(pl.program_id(1) == 0)
        def _():
            for r in (dwg_ref, dwv_ref, dbg_ref, dbv_ref):
                r[...] = jnp.zeros_like(r)

        ug = ug_ref[...]
        uv = uv_ref[...]
        wg = _taps(wg_ref)
        wv = _taps(wv_ref)
        gate = _conv3(ug, wg, bg_ref[...])
        val = _conv3(uv, wv, bv_ref[...])
        da = _nt(dx_ref[...], wd_ref[...])
        sig = 1.0 / (1.0 + jnp.exp(-gate))
        dval = da * (gate * sig)
        dgate = da * val * (sig * (1.0 + gate * (1.0 - sig)))
        for u_, d, w, du_ref, dw_ref, db_ref in ((ug, dgate, wg, dug_ref, dwg_ref, dbg_ref),
                                                 (uv, dval, wv, duv_ref, dwv_ref, dbv_ref)):
            du_ref[...] = (w[2] * d + w[1] * _shift_up(d, 1) + w[0] * _shift_up(d, 2)).astype(BF16)
            db_ref[...] += jnp.sum(d, axis=0, keepdims=True)
            dw_ref[0:1, :] += jnp.sum(d * _shift_down(u_, 2), axis=0, keepdims=True)
            dw_ref[1:2, :] += jnp.sum(d * _shift_down(u_, 1), axis=0, keepdims=True)
            dw_ref[2:3, :] += jnp.sum(d * u_, axis=0, keepdims=True)

    def blk(rows, off):
        return pl.BlockSpec((rows, CONV_TC), lambda j, b: (b if rows == S else 0, off + j))

    return pl.pallas_call(
        body, name="conv_act_bwd", grid=(nc, B),
        in_specs=[blk(S, 0), blk(S, 0), pl.BlockSpec((S, D_MODEL), lambda j, b: (b, 0)),
                  pl.BlockSpec((CONV_TC, D_MODEL), lambda j, b: (j, 0)), blk(3, 0), blk(3, nc), blk(1, 0), blk(1, nc)],
        out_specs=[blk(S, 0), blk(S, 0), blk(3, 0), blk(3, 0), blk(1, 0), blk(1, 0)],
        out_shape=[jax.ShapeDtypeStruct((T, D_FF), BF16), jax.ShapeDtypeStruct((T, D_FF), BF16),
                   jax.ShapeDtypeStruct((3, D_FF), F32), jax.ShapeDtypeStruct((3, D_FF), F32),
                   jax.ShapeDtypeStruct((1, D_FF), F32), jax.ShapeDtypeStruct((1, D_FF), F32)],
        compiler_params=_params(("parallel", "arbitrary")),
    )(ug, uv, dx2, w_down, conv_w, conv_w, conv_b, conv_b)


CHIP_MASKS = ((1, 0), (0, 1), (1, 1))


def _place():
    return lax.axis_index("x"), lax.axis_index("y"), lax.axis_index("c")


HALF_ALIGN = 32


def _any_specs(n):
    return [pl.BlockSpec(memory_space=pl.ANY)] * n


def _half_rows(r, half):
    return pl.ds(pl.multiple_of(half * (r // 2), HALF_ALIGN // 2), r // 2)


def _remote(src, dst, send_sem, recv_sem, device):
    return pltpu.make_async_remote_copy(src_ref=src, dst_ref=dst, send_sem=send_sem, recv_sem=recv_sem,
                                        device_id=device, device_id_type=MESH)


class Exchange:
    def __init__(self, ins, out_shape, sems, start, finish):
        self.ins, self.out_shape, self.sems, self.start, self.finish = list(ins), list(out_shape), list(sems), start, finish


def gather_group(shards):
    n = len(shards)
    split = [s.shape[0] % HALF_ALIGN == 0 for s in shards]

    def rows(w, half):
        return _half_rows(shards[w].shape[0], half) if split[w] else slice(None)

    def copies(ins, outs, sems):
        ici_s, ici_r, _, _, local_sems = sems
        x, y, c = _place()
        chip = 2 * x + y
        local = [pltpu.make_async_copy(ins[w], outs[w].at[chip], local_sems.at[w]) for w in range(n)]
        sends = [_remote(ins[w].at[rows(w, c)], outs[w].at[chip, rows(w, c)], ici_s.at[w, k], ici_r.at[w, k],
                         (x ^ fx, y ^ fy, c))
                 for w in range(n) for k, (fx, fy) in enumerate(CHIP_MASKS)]
        return local, sends

    def start(ins, outs, sems):
        local, sends = copies(ins, outs, sems)
        for cp in local + sends:
            cp.start()

    def finish(ins, outs, sems):
        ici_s, ici_r, d2d_s, d2d_r, _ = sems
        x, y, c = _place()
        sib = (x, y, 1 - c)
        local, sends = copies(ins, outs, sems)
        for w in range(n):
            for k, (fx, fy) in enumerate(CHIP_MASKS):
                landed = outs[w].at[2 * (x ^ fx) + (y ^ fy), rows(w, c)]
                _remote(landed, landed, ici_s.at[w, k], ici_r.at[w, k], sib).wait_recv()
                if split[w]:
                    cp = _remote(landed, landed, d2d_s.at[w, k], d2d_r.at[w, k], sib)
                    cp.start()
                    sends.append(cp)
        for w in range(n):
            for k, (fx, fy) in enumerate(CHIP_MASKS):
                if split[w]:
                    other = outs[w].at[2 * (x ^ fx) + (y ^ fy), rows(w, 1 - c)]
                    _remote(other, other, d2d_s.at[w, k], d2d_r.at[w, k], sib).wait_recv()
        for cp in sends:
            cp.wait_send()
        for cp in local:
            cp.wait()

    sems = pltpu.SemaphoreType.DMA((n, 3))
    return Exchange(shards, [jax.ShapeDtypeStruct((N_CHIPS,) + s.shape, s.dtype) for s in shards],
                    [sems, sems, sems, sems, pltpu.SemaphoreType.DMA((n,))], start, finish)


def swap_half(parts):
    n = len(parts)

    def copies(ins, outs, sems):
        x, y, c = _place()
        return [_remote(ins[w].at[:, _half_rows(parts[w].shape[1], 1 - c)], outs[w], sems[0].at[w], sems[1].at[w],
                        (x, y, 1 - c)) for w in range(n)]

    def start(ins, outs, sems):
        for cp in copies(ins, outs, sems):
            cp.start()

    def finish(ins, outs, sems):
        for cp in copies(ins, outs, sems):
            cp.wait_recv()
            cp.wait_send()

    return Exchange(parts, [jax.ShapeDtypeStruct((N_CHIPS, p.shape[1] // 2, p.shape[2]), F32) for p in parts],
                    [pltpu.SemaphoreType.DMA((n,))] * 2, start, finish)


def scatter_half(halves):
    n = len(halves)

    def copies(ins, outs, sems):
        x, y, c = _place()
        return [_remote(ins[w].at[2 * (x ^ fx) + (y ^ fy)], outs[w].at[k], sems[0].at[w, k], sems[1].at[w, k],
                        (x ^ fx, y ^ fy, c))
                for w in range(n) for k, (fx, fy) in enumerate(CHIP_MASKS)]

    def start(ins, outs, sems):
        for cp in copies(ins, outs, sems):
            cp.start()

    def finish(ins, outs, sems):
        for cp in copies(ins, outs, sems):
            cp.wait_recv()
            cp.wait_send()

    return Exchange(halves, [jax.ShapeDtypeStruct((3,) + h.shape[1:], h.dtype) for h in halves],
                    [pltpu.SemaphoreType.DMA((n, 3))] * 2, start, finish)


def swap_final(finals):
    n = len(finals)

    def copies(ins, outs, sems):
        x, y, c = _place()
        mine = [outs[w].at[_half_rows(2 * finals[w].shape[0], c)] for w in range(n)]
        local = [pltpu.make_async_copy(ins[w], mine[w], sems[2].at[w]) for w in range(n)]
        sends = [_remote(ins[w], mine[w], sems[0].at[w], sems[1].at[w], (x, y, 1 - c)) for w in range(n)]
        return local, sends

    def start(ins, outs, sems):
        local, sends = copies(ins, outs, sems)
        for cp in local + sends:
            cp.start()

    def finish(ins, outs, sems):
        x, y, c = _place()
        local, sends = copies(ins, outs, sems)
        for w in range(n):
            got = outs[w].at[_half_rows(2 * finals[w].shape[0], 1 - c)]
            _remote(got, got, sems[0].at[w], sems[1].at[w], (x, y, 1 - c)).wait_recv()
        for cp in sends:
            cp.wait_send()
        for cp in local:
            cp.wait()

    return Exchange(finals, [jax.ShapeDtypeStruct((2 * f.shape[0], f.shape[1]), F32) for f in finals],
                    [pltpu.SemaphoreType.DMA((n,))] * 3, start, finish)


def exchange_call(name, ex):
    n, m = len(ex.ins), len(ex.out_shape)

    def body(*refs):
        ins, outs, sems = refs[:n], refs[n:n + m], refs[n + m:]
        ex.start(ins, outs, sems)
        ex.finish(ins, outs, sems)

    return pl.pallas_call(body, name=name, in_specs=_any_specs(n), out_specs=_any_specs(m), out_shape=ex.out_shape,
                          scratch_shapes=ex.sems, compiler_params=_params())(*ex.ins)


def _call(body, ex, *, name, grid, in_specs, out_specs, out_shape, args, scratch_shapes=()):
    sem = ("arbitrary",) * len(grid)
    if ex is None:
        outs = pl.pallas_call(body, name=name, grid=grid, in_specs=in_specs, out_specs=out_specs, out_shape=out_shape,
                              scratch_shapes=list(scratch_shapes), compiler_params=_params(sem))(*args)
        return outs, None
    ni, no, ns = len(in_specs), len(out_specs), len(scratch_shapes)
    ne, me = len(ex.ins), len(ex.out_shape)

    def wrapped(*refs):
        own_in, ex_in = refs[:ni], refs[ni:ni + ne]
        own_out, ex_out = refs[ni + ne:ni + ne + no], refs[ni + ne + no:ni + ne + no + me]
        own_scr, ex_sems = refs[ni + ne + no + me:ni + ne + no + me + ns], refs[ni + ne + no + me + ns:]
        ids = [pl.program_id(a) for a in range(len(grid))]
        first = functools.reduce(jnp.logical_and, [i == 0 for i in ids])
        last = functools.reduce(jnp.logical_and, [i == g - 1 for i, g in zip(ids, grid)])

        @pl.when(first)
        def _():
            ex.start(ex_in, ex_out, ex_sems)

        body(*own_in, *own_out, *own_scr)

        @pl.when(last)
        def _():
            ex.finish(ex_in, ex_out, ex_sems)

    outs = pl.pallas_call(
        wrapped, name=name, grid=grid, in_specs=list(in_specs) + _any_specs(ne),
        out_specs=list(out_specs) + _any_specs(me), out_shape=list(out_shape) + ex.out_shape,
        scratch_shapes=list(scratch_shapes) + ex.sems, compiler_params=_params(sem))(*args, *ex.ins)
    return outs[:no], outs[no:]


def _row_tile(rows, cap, mult=8):
    return max(t for t in range(mult, min(rows, cap) + 1, mult) if rows % t == 0)


def add_half_call(name, part, got, where):
    _, rh, cols = got.shape
    tr = _row_tile(rh, 176, 16)
    nb = rh // tr

    def body(where_ref, p_ref, g_ref, own_ref, send_ref):
        t = p_ref[...] + g_ref[...]
        send_ref[...] = t.astype(BF16)
        chip = where_ref[1]
        own_ref[...] = p_ref[chip] + g_ref[chip]

    blk = (N_CHIPS, tr, cols)
    return pl.pallas_call(
        body, name=name,
        grid_spec=pltpu.PrefetchScalarGridSpec(
            num_scalar_prefetch=1, grid=(nb,),
            in_specs=[pl.BlockSpec(blk, lambda i, where_ref: (0, where_ref[0] * nb + i, 0)),
                      pl.BlockSpec(blk, lambda i, where_ref: (0, i, 0))],
            out_specs=[pl.BlockSpec((tr, cols), lambda i, where_ref: (i, 0)),
                       pl.BlockSpec(blk, lambda i, where_ref: (0, i, 0))]),
        out_shape=[jax.ShapeDtypeStruct((rh, cols), F32), jax.ShapeDtypeStruct(got.shape, BF16)],
        compiler_params=_params(("parallel",)),
    )(where, part, got)


def sum_chips_call(name, own, got):
    _, rh, cols = got.shape
    tr = _row_tile(rh, 176, 16)

    def body(h_ref, g_ref, o_ref):
        o_ref[...] = ((h_ref[...] + g_ref[0].astype(F32)) + g_ref[1].astype(F32)) + g_ref[2].astype(F32)

    return pl.pallas_call(
        body, name=name, grid=(rh // tr,),
        in_specs=[pl.BlockSpec((tr, cols), lambda i: (i, 0)), pl.BlockSpec((3, tr, cols), lambda i: (0, i, 0))],
        out_specs=pl.BlockSpec((tr, cols), lambda i: (i, 0)),
        out_shape=jax.ShapeDtypeStruct((rh, cols), F32),
        compiler_params=_params(("parallel",)),
    )(own, got)


def _adamw(w, g, m, v):
    m = ADAM_B1 * m + (1.0 - ADAM_B1) * g
    v = ADAM_B2 * v + (1.0 - ADAM_B2) * (g * g)
    m_hat = m / (1.0 - ADAM_B1 ** ADAM_STEP)
    v_hat = v / (1.0 - ADAM_B2 ** ADAM_STEP)
    delta = -ADAM_LR * (m_hat / (jnp.sqrt(v_hat) + ADAM_EPS) + ADAM_WD * w)
    return delta, m, v


def adamw_call(name, g, w, m, v):
    r, cols = w.shape
    tr = r if r % 8 else _row_tile(r, 256)

    def body(g_ref, w_ref, m_ref, v_ref, d_ref, nm_ref, nv_ref):
        d_ref[...], nm_ref[...], nv_ref[...] = _adamw(w_ref[...], g_ref[...], m_ref[...], v_ref[...])

    spec = pl.BlockSpec((tr, cols), lambda i: (i, 0))
    return pl.pallas_call(
        body, name=name, grid=(r // tr,), in_specs=[spec] * 4, out_specs=[spec] * 3,
        out_shape=[jax.ShapeDtypeStruct((r, cols), F32)] * 3,
        compiler_params=_params(("parallel",)),
    )(g, w, m, v)


def allsum_small_call(v):
    R = v.shape[0]

    def body(v_ref, out_ref, buf, send_sems, recv_sems):
        x, y, c = _place()
        me = 4 * x + 2 * y + c
        buf[me] = v_ref[...]
        sends = []
        for k in range(1, N_DEV):
            fx, fy, fc = (k >> 2) & 1, (k >> 1) & 1, k & 1
            cp = pltpu.make_async_remote_copy(
                src_ref=v_ref, dst_ref=buf.at[me], send_sem=send_sems.at[k - 1], recv_sem=recv_sems.at[k - 1],
                device_id=(x ^ fx, y ^ fy, c ^ fc), device_id_type=MESH)
            cp.start()
            sends.append(cp)
        for k in range(1, N_DEV):
            pltpu.make_async_remote_copy(
                src_ref=v_ref, dst_ref=buf.at[me ^ k], send_sem=send_sems.at[k - 1], recv_sem=recv_sems.at[k - 1],
                device_id=(x, y, c), device_id_type=MESH).wait_recv()
        acc = buf[0]
        for d in range(1, N_DEV):
            acc = acc + buf[d]
        out_ref[...] = acc
        for cp in sends:
            cp.wait_send()

    vm = pl.BlockSpec(memory_space=pltpu.VMEM)
    return pl.pallas_call(
        body, name="allsum_small", in_specs=[vm], out_specs=vm,
        out_shape=jax.ShapeDtypeStruct((R, LANES), F32),
        scratch_shapes=[pltpu.VMEM((N_DEV, R, LANES), F32), pltpu.SemaphoreType.DMA((N_DEV - 1,)),
                        pltpu.SemaphoreType.DMA((N_DEV - 1,))],
        compiler_params=_params(),
    )(v)


def _slab(flat, mult):
    n = flat.shape[-1]
    rows = -(-n // (LANES * mult)) * mult
    flat = jnp.pad(flat, [(0, 0)] * (flat.ndim - 1) + [(0, rows * LANES - n)])
    return flat.reshape(flat.shape[:-1] + (rows, LANES))


def full_from_chips(blocks, by_col):
    _, r, c = blocks.shape
    return blocks.transpose(1, 0, 2).reshape(r, N_CHIPS * c) if by_col else blocks.reshape(N_CHIPS * r, c)


def chips_from_full(full, by_col):
    if by_col:
        r, c = full.shape[0], full.shape[1] // N_CHIPS
        return full.reshape(r, N_CHIPS, c).transpose(1, 0, 2)
    return full.reshape(N_CHIPS, full.shape[0] // N_CHIPS, full.shape[1])


SMALL_PACK = SMALL_W + ("loss", "conv_w")
SMALL_PACK_N = {**SMALL_N, "loss": 1, "conv_w": 3 * 2 * D_FF}


def pack_small(vals):
    zero = jnp.zeros((1,), F32)
    return _slab(jnp.concatenate([vals[n].reshape(-1) if n in vals else jnp.tile(zero, SMALL_PACK_N[n])
                                  for n in SMALL_PACK]), 8)


def unpack_small(slab, shapes):
    flat = slab.reshape(-1)
    out, off = {}, 0
    for n in SMALL_PACK:
        out[n] = flat[off:off + SMALL_PACK_N[n]].reshape(shapes[n])
        off += SMALL_PACK_N[n]
    return out


def _split_heads(w, a, b):
    r = w.shape[0]
    w3 = w.reshape(r, HEADS, a + b)
    return w3[:, :, :a].reshape(r, HEADS * a), w3[:, :, a:].reshape(r, HEADS * b)


def _merge_heads(wa, wb, a, b):
    r = wa.shape[0]
    return jnp.concatenate([wa.reshape(r, HEADS, a), wb.reshape(r, HEADS, b)], axis=2).reshape(r, HEADS * (a + b))


def kernel(x, positions, g_mix, w_in, g_cq, w_uq, g_ckv, w_ukv, g_sb_out, g_mla_out, w_out, g_ffn, w_up, conv_w, conv_b, w_down, g_final, loss_target, m_g_mix, m_w_in, m_g_cq, m_w_uq, m_g_ckv, m_w_ukv, m_g_sb_out, m_g_mla_out, m_w_out, m_g_ffn, m_w_up, m_conv_w, m_conv_b, m_w_down, m_g_final, v_g_mix, v_w_in, v_g_cq, v_w_uq, v_g_ckv, v_w_ukv, v_g_sb_out, v_g_mla_out, v_w_out, v_g_ffn, v_w_up, v_conv_w, v_conv_b, v_w_down, v_g_final):
    given = dict(locals())
    B, S, _ = x.shape
    T = B * S
    w_big = {n: given[n][0] for n in BIG_W}
    m_big = {n: given["m_" + n][0] for n in BIG_W}
    v_big = {n: given["v_" + n][0] for n in BIG_W}

    first = ("w_in", "w_uq", "w_ukv")
    later = ("w_out", "w_up", "w_down", "conv_w")
    x2d = x.reshape(T, D_MODEL)
    h, got_w = rmsnorm_fwd_call("norm_mix", x2d, g_mix, ex=gather_group([w_big[n].astype(BF16) for n in first]))
    full = {n: full_from_chips(g_, BIG_SHARD[n][2]) for n, g_ in zip(first, got_w)}
    gather_later = gather_group([w_big[n] if n == "conv_w" else w_big[n].astype(BF16) for n in later])
    w_in_p = jnp.pad(full["w_in"], ((0, 0), (0, IN_COLS_PAD - IN_COLS)))
    w_uq_p = jnp.concatenate(_split_heads(full["w_uq"], HEAD_DIM, ROPE_DIM), axis=1)
    w_ukv_p = jnp.concatenate(_split_heads(full["w_ukv"], HEAD_DIM, HEAD_DIM), axis=1)

    half = ROPE_DIM // 2
    inv_freq = 1.0 / (ROPE_BASE ** (jnp.arange(half, dtype=F32) * (2.0 / ROPE_DIM)))
    cos, sin = rope_tab_call(positions.reshape(T, 1), jnp.tile(inv_freq, LANES // half).reshape(1, LANES))
    p = matmul_call("proj_in", h, w_in_p, "nn", tn=IN_COLS_PAD // 2)
    qn, qr, kn, vm, krt, cqn, ckvn = mla_prep_fwd_call(p, cos, sin, g_cq, g_ckv, w_uq_p, w_ukv_p)
    o_sb, lt_sb, got_w = sb_fwd_call(p, B, S, ex=gather_later)
    w_up4 = got_w[1]
    full.update({n: full_from_chips(g_, BIG_SHARD[n][2]) for n, g_ in zip(later, got_w) if n != "w_up"})
    conv_w_full = full["conv_w"]
    o_mla, lse = mla_fwd_call(qn, qr, kn, krt, vm, B, S)
    o_cat = outnorm_fwd_call(o_sb, o_mla, g_sb_out, g_mla_out)
    x1 = matmul_call("proj_out", o_cat, full["w_out"], "nn", res=x2d)
    hn = rmsnorm_fwd_call("norm_ffn", x1, g_ffn)
    u_g, u_v = ffn_up_call(hn, w_up4)
    act = conv_act_fwd_call(u_g, u_v, conv_w_full, conv_b, B, S)
    x2 = matmul_call("ffn_down", act, full["w_down"], "nn", res=x1)
    dx2, dx2b, loss_row, dg_final = final_loss_call(x2, g_final.reshape(1, D_MODEL), loss_target.reshape(T, D_MODEL))

    xi, yi, ci = _place()
    chip = (2 * xi + yi).astype(jnp.int32).reshape(1)
    where = jnp.stack([ci, 2 * xi + yi]).astype(jnp.int32)

    def add_halves(names, parts, sib_rows):
        return [add_half_call("add_half_" + n, p_, s_, where) for n, p_, s_ in zip(names, parts, sib_rows)]

    def sum_chips(names, halves, from_chips):
        return [sum_chips_call("sum_chips_" + n, h_[0], f_) for n, h_, f_ in zip(names, halves, from_chips)]

    ffn_w = ("w_down", "w_up")
    parts_ffn = [chips_from_full(wgrad_call("wgrad_down", act, dx2b, tn=512), False)]
    du_g, du_v, dcw_g, dcw_v, dcb_g, dcb_v = conv_act_bwd_call(
        u_g, u_v, dx2b, full["w_down"], conv_w_full, conv_b, B, S)
    parts_ffn.append(wgrad_up_call(hn, du_g, du_v))
    dhn, sib_ffn = ffn_up_bwd_call(du_g, du_v, w_up4, swap_half(parts_ffn))
    dx1, dg_ffn = rmsnorm_bwd_call("norm_ffn_bwd", x1, g_ffn, dhn, dx2)
    parts_out = [chips_from_full(wgrad_call("wgrad_out", o_cat, dx1), False)]
    do_cat = matmul_call("proj_out_bwd", dx1, full["w_out"], "nt")
    do_sb, do_mla, dg_sb_out, dg_mla_out, sib_out = outnorm_bwd_call(
        o_sb, o_mla, g_sb_out, g_mla_out, do_cat, ex=swap_half(parts_out))
    early = ffn_w + ("w_out",)
    halves = add_halves(early, parts_ffn + parts_out, list(sib_ffn) + list(sib_out))
    dq_sb, dk_sb, dv_sb, from_chips = sb_bwd_call(p, lt_sb, do_sb, B, S, ex=scatter_half([h_[1] for h_ in halves]))
    finals = sum_chips(early, halves, from_chips)
    dqn, dqr4, dkn, dvm, dkrt4, done = mla_bwd_call(qn, qr, kn, krt, vm, o_mla, lse, do_mla, B, S, ex=swap_final(finals))
    grads = dict(zip(early, done))
    dcq, dckvr, dq_cat, dkv_cat, dg_cq, dg_ckv = mla_prep_bwd_call(
        p, cos, sin, g_cq, g_ckv, w_uq_p, w_ukv_p, dqn, dqr4, dkn, dvm, dkrt4)
    dw_uq_p = wgrad_call("wgrad_uq", cqn, dq_cat)
    dw_ukv_p = wgrad_call("wgrad_ukv", ckvn, dkv_cat)
    dp = (dq_sb, dk_sb, dv_sb, dcq, dckvr)
    late = ("w_uq", "w_ukv", "w_in")
    parts_late = [chips_from_full(g_, True) for g_ in (
        _merge_heads(dw_uq_p[:, :ATT_W], dw_uq_p[:, ATT_W:], HEAD_DIM, ROPE_DIM),
        _merge_heads(dw_ukv_p[:, :ATT_W], dw_ukv_p[:, ATT_W:], HEAD_DIM, HEAD_DIM),
        wgrad_in_call(h, dp)[:, :IN_COLS])]
    dh, sib_late = proj_in_bwd_call(dp, w_in_p, swap_half(parts_late))
    halves = add_halves(late, parts_late, sib_late)
    grad_x, dg_mix, from_chips = rmsnorm_bwd_call(
        "norm_mix_bwd", x2d, g_mix, dh, dx1, ex=scatter_half([h_[1] for h_ in halves]))
    finals = sum_chips(late, halves, from_chips)
    grads.update(zip(late, exchange_call("swap_final_late", swap_final(finals))))

    shapes = {n: given[n].shape for n in SMALL_W}
    shapes.update(loss=(), conv_w=(3, 2 * D_FF))
    small_g = {"g_mix": dg_mix, "g_cq": dg_cq, "g_ckv": dg_ckv, "g_sb_out": dg_sb_out, "g_mla_out": dg_mla_out,
               "g_ffn": dg_ffn, "conv_b": jnp.concatenate([dcb_g, dcb_v], axis=1), "g_final": dg_final,
               "loss": loss_row[0, :1], "conv_w": jnp.concatenate([dcw_g, dcw_v], axis=1)}
    gs_slab = allsum_small_call(pack_small(small_g))
    small_in = [pack_small({n: given[pre + n] for n in SMALL_W}) for pre in ("", "m_", "v_")]
    small_out = [unpack_small(s, shapes) for s in (gs_slab,) + tuple(adamw_call("adamw_small", gs_slab, *small_in))]
    cw_cols = BIG_SHARD["conv_w"][1]
    grads["conv_w"] = lax.dynamic_slice_in_dim(small_out[0]["conv_w"], chip[0] * cw_cols, cw_cols, axis=1)

    big_out = {n: (grads[n],) + tuple(adamw_call("adamw_" + n, grads[n], w_big[n], m_big[n], v_big[n])) for n in BIG_W}
    weights = ("g_mix", "w_in", "g_cq", "w_uq", "g_ckv", "w_ukv", "g_sb_out", "g_mla_out", "w_out", "g_ffn",
               "w_up", "conv_w", "conv_b", "w_down", "g_final")
    outs = [small_out[0]["loss"], grad_x.reshape(B, S, D_MODEL)]
    for k in range(4):
        for n in weights:
            outs.append(big_out[n][k][None] if n in BIG_W else small_out[k][n])
    return tuple(outs)
```

```python
import functools

import jax
import jax.numpy as jnp
from jax import lax
from jax.experimental import pallas as pl
from jax.experimental.pallas import tpu as pltpu

F32 = jnp.float32
BF16 = jnp.bfloat16
MESH = pl.DeviceIdType.MESH

D_MODEL = 1024
HEADS = 8
HEAD_DIM = 64
ATT_W = HEADS * HEAD_DIM
ROPE_DIM = 32
ROPE_W = HEADS * ROPE_DIM
QK_DIM = HEAD_DIM + ROPE_DIM
Q_RANK = 384
KV_RANK = 256
D_FF = 2816
IN_COLS = 2208
IN_COLS_PAD = 2304
EPS = 1e-6
ROPE_BASE = 10000.0
SB_SCALE = HEAD_DIM ** -0.5
MLA_SCALE = QK_DIM ** -0.5
LOG2E = 1.4426950408889634
LN2 = 0.6931471805599453
MLA_SCALE2 = MLA_SCALE * LOG2E
LANES = 128
N_CHIPS = 4
N_DEV = 8
VMEM_LIMIT = 48 * 1024 * 1024
ATT_TQ = 256
ATT_TK = 256
ATT_PAIRS = 4
PAIR_LANES = [slice(i * LANES, (i + 1) * LANES) for i in range(ATT_PAIRS)]
SB_BWD_GROUP = 2
SB_FWD_GROUP = 4
MLA_GROUP = 4
NEG_BIG = -1e30

ADAM_LR = 0.001
ADAM_B1 = 0.9
ADAM_B2 = 0.999
ADAM_EPS = 1e-08
ADAM_WD = 0.01
ADAM_STEP = 10

BIG_W = ("w_in", "w_uq", "w_ukv", "w_out", "w_up", "conv_w", "w_down")
BIG_SHARD = {
    "w_in": (D_MODEL, IN_COLS // 4, True),
    "w_uq": (Q_RANK, HEADS * QK_DIM // 4, True),
    "w_ukv": (KV_RANK, 2 * ATT_W // 4, True),
    "w_out": (2 * ATT_W // 4, D_MODEL, False),
    "w_up": (D_MODEL, 2 * D_FF // 4, True),
    "conv_w": (3, 2 * D_FF // 4, True),
    "w_down": (D_FF // 4, D_MODEL, False),
}
SMALL_W = ("g_mix", "g_cq", "g_ckv", "g_sb_out", "g_mla_out", "g_ffn", "conv_b", "g_final")
SMALL_N = {"g_mix": D_MODEL, "g_cq": Q_RANK, "g_ckv": KV_RANK, "g_sb_out": ATT_W, "g_mla_out": ATT_W,
           "g_ffn": D_MODEL, "conv_b": 2 * D_FF, "g_final": D_MODEL}


def _params(sem=None, **kw):
    return pltpu.CompilerParams(dimension_semantics=sem, vmem_limit_bytes=VMEM_LIMIT, **kw)


def _dot(a, b, dims):
    return lax.dot_general(a, b, (dims, ((), ())), preferred_element_type=F32)


def _nn(a, b):
    return _dot(a, b, ((1,), (0,)))


def _nt(a, b):
    return _dot(a, b, ((1,), (1,)))


def _tn(a, b):
    return _dot(a, b, ((0,), (0,)))


def _split2(x):
    hi = x.astype(BF16)
    lo = (x - hi.astype(F32)).astype(BF16)
    return hi, lo


def _split3(x):
    hi = x.astype(BF16)
    r1 = x - hi.astype(F32)
    mid = r1.astype(BF16)
    return hi, mid, (r1 - mid.astype(F32)).astype(BF16)


def _rms_r(x, d):
    return lax.rsqrt(jnp.sum(x * x, axis=-1, keepdims=True) * (1.0 / d) + EPS)


def _rms_bwd(x, g, dy, d):
    r = _rms_r(x, d)
    xhat = x * r
    gy = dy * g
    dx = r * (gy - xhat * (jnp.sum(xhat * gy, axis=-1, keepdims=True) * (1.0 / d)))
    return dx, jnp.sum(dy * xhat, axis=0, keepdims=True)


def _rot(x):
    lane = lax.broadcasted_iota(jnp.int32, x.shape, x.ndim - 1)
    n = x.shape[-1]
    return jnp.where((lane & 31) < 16, pltpu.roll(x, n - 16, x.ndim - 1), pltpu.roll(x, 16, x.ndim - 1))


def _fold4(x):
    return x + pltpu.roll(x, 32, 1) + pltpu.roll(x, 64, 1) + pltpu.roll(x, 96, 1)


def matmul_call(name, a, b, mode, out_dtype=F32, res=None, tm=512, tn=None, ex=None):
    M, K = a.shape
    N = b.shape[1] if mode == "nn" else b.shape[0]
    tn = N if tn is None else tn
    assert M % tm == 0 and N % tn == 0

    def body(*refs):
        if res is None:
            a_ref, b_ref, o_ref = refs
        else:
            a_ref, b_ref, r_ref, o_ref = refs
        av = a_ref[...].astype(BF16)
        bv = b_ref[...].astype(BF16)
        acc = _nn(av, bv) if mode == "nn" else _nt(av, bv)
        if res is not None:
            acc = r_ref[...] + acc
        o_ref[...] = acc.astype(out_dtype)

    in_specs = [pl.BlockSpec((tm, K), lambda j, i: (i, 0))]
    if mode == "nn":
        in_specs.append(pl.BlockSpec((K, tn), lambda j, i: (0, j)))
    else:
        in_specs.append(pl.BlockSpec((tn, K), lambda j, i: (j, 0)))
    args = [a, b]
    if res is not None:
        in_specs.append(pl.BlockSpec((tm, tn), lambda j, i: (i, j)))
        args.append(res)
    outs, moved = _call(body, ex, name=name, grid=(N // tn, M // tm), in_specs=in_specs,
                        out_specs=[pl.BlockSpec((tm, tn), lambda j, i: (i, j))],
                        out_shape=[jax.ShapeDtypeStruct((M, N), out_dtype)], args=args)
    return outs[0] if ex is None else (outs[0], moved)


def wgrad_call(name, a, b, tn=None, tt=512, by_chip=False):
    T, M = a.shape
    N = b.shape[1]
    tn = N if tn is None else tn
    assert T % tt == 0 and N % tn == 0
    if by_chip:
        out_spec = pl.BlockSpec((None, M, tn), lambda j, t: (j, 0, 0))
        out_shape = jax.ShapeDtypeStruct((N // tn, M, tn), F32)
    else:
        out_spec = pl.BlockSpec((M, tn), lambda j, t: (0, j))
        out_shape = jax.ShapeDtypeStruct((M, N), F32)

    def body(a_ref, b_ref, o_ref):
        @pl.when(pl.program_id(1) == 0)
        def _():
            o_ref[...] = jnp.zeros_like(o_ref)

        o_ref[...] += _tn(a_ref[...].astype(BF16), b_ref[...].astype(BF16))

    return pl.pallas_call(
        body, name=name, grid=(N // tn, T // tt),
        in_specs=[pl.BlockSpec((tt, M), lambda j, t: (t, 0)), pl.BlockSpec((tt, tn), lambda j, t: (t, j))],
        out_specs=out_spec, out_shape=out_shape,
        compiler_params=_params(("parallel", "arbitrary")),
    )(a, b)


UP_COLS = 2 * D_FF // N_CHIPS


def ffn_up_call(hn, w4, tm=512):
    T, K = hn.shape

    def body(a_ref, wg_ref, wv_ref, ug_ref, uv_ref):
        a = a_ref[...]
        ug_ref[...] = _nn(a, wg_ref[...])
        uv_ref[...] = _nn(a, wv_ref[...])

    out = pl.BlockSpec((tm, UP_COLS), lambda j, i: (i, j))
    return pl.pallas_call(
        body, name="ffn_up", grid=(2, T // tm),
        in_specs=[pl.BlockSpec((tm, K), lambda j, i: (i, 0)),
                  pl.BlockSpec((None, K, UP_COLS), lambda j, i: (j, 0, 0)),
                  pl.BlockSpec((None, K, UP_COLS), lambda j, i: (2 + j, 0, 0))],
        out_specs=[out, out], out_shape=[jax.ShapeDtypeStruct((T, D_FF), F32)] * 2,
        compiler_params=_params(("parallel", "parallel")),
    )(hn, w4, w4)


def ffn_up_bwd_call(du_g, du_v, w4, ex, tm=512, tn=512):
    T = du_g.shape[0]
    N = w4.shape[1]

    def body(g_ref, v_ref, w_ref, o_ref):
        acc = _nt(g_ref[:, :UP_COLS], w_ref[0]) + _nt(g_ref[:, UP_COLS:], w_ref[1])
        o_ref[...] = acc + _nt(v_ref[:, :UP_COLS], w_ref[2]) + _nt(v_ref[:, UP_COLS:], w_ref[3])

    row = pl.BlockSpec((tm, D_FF), lambda j, i: (i, 0))
    outs, moved = _call(body, ex, name="ffn_up_bwd", grid=(N // tn, T // tm),
                        in_specs=[row, row, pl.BlockSpec((N_CHIPS, tn, UP_COLS), lambda j, i: (0, j, 0))],
                        out_specs=[pl.BlockSpec((tm, tn), lambda j, i: (i, j))],
                        out_shape=[jax.ShapeDtypeStruct((T, N), F32)], args=(du_g, du_v, w4))
    return outs[0], moved


def wgrad_up_call(hn, du_g, du_v, tt=512):
    T, M = hn.shape

    def body(a_ref, g_ref, v_ref, o_ref):
        @pl.when(pl.program_id(1) == 0)
        def _():
            o_ref[...] = jnp.zeros_like(o_ref)

        a = a_ref[...]
        o_ref[0] += _tn(a, g_ref[...])
        o_ref[1] += _tn(a, v_ref[...])

    col = pl.BlockSpec((tt, UP_COLS), lambda j, t: (t, j))
    out = pl.pallas_call(
        body, name="wgrad_up", grid=(2, T // tt),
        in_specs=[pl.BlockSpec((tt, M), lambda j, t: (t, 0)), col, col],
        out_specs=pl.BlockSpec((2, None, M, UP_COLS), lambda j, t: (0, j, 0, 0)),
        out_shape=jax.ShapeDtypeStruct((2, 2, M, UP_COLS), F32),
        compiler_params=_params(("parallel", "arbitrary")),
    )(hn, du_g, du_v)
    return out.reshape(N_CHIPS, M, UP_COLS)


IN_PIECES = ((0, ATT_W), (ATT_W, ATT_W), (2 * ATT_W, ATT_W), (3 * ATT_W, Q_RANK), (3 * ATT_W + Q_RANK, Q_RANK))


def _piece_specs(rows, index):
    return [pl.BlockSpec((rows, w), functools.partial(index, off // w)) for off, w in IN_PIECES]


def proj_in_bwd_call(pieces, w_in_p, ex, tm=512):
    T = pieces[0].shape[0]
    N = w_in_p.shape[0]
    n = len(pieces)

    def body(*refs):
        o_ref = refs[2 * n]
        acc = _nt(refs[0][...].astype(BF16), refs[n][...])
        for i in range(1, n):
            acc = acc + _nt(refs[i][...].astype(BF16), refs[n + i][...])
        o_ref[...] = acc

    outs, moved = _call(body, ex, name="proj_in_bwd", grid=(T // tm,),
                        in_specs=_piece_specs(tm, lambda c, i: (i, 0)) + _piece_specs(N, lambda c, i: (0, c)),
                        out_specs=[pl.BlockSpec((tm, N), lambda i: (i, 0))],
                        out_shape=[jax.ShapeDtypeStruct((T, N), F32)], args=tuple(pieces) + (w_in_p,) * n)
    return outs[0], moved


def wgrad_in_call(h, pieces, tt=512):
    T, M = h.shape
    n = len(pieces)

    def body(*refs):
        a_ref, o_ref = refs[0], refs[n + 1]

        @pl.when(pl.program_id(0) == 0)
        def _():
            o_ref[...] = jnp.zeros_like(o_ref)

        a = a_ref[...]
        for i, (off, w) in enumerate(IN_PIECES):
            o_ref[:, off:off + w] += _tn(a, refs[1 + i][...].astype(BF16))

    return pl.pallas_call(
        body, name="wgrad_in", grid=(T // tt,),
        in_specs=[pl.BlockSpec((tt, M), lambda t: (t, 0))] + [pl.BlockSpec((tt, w), lambda t: (t, 0)) for _, w in IN_PIECES],
        out_specs=pl.BlockSpec((M, IN_COLS_PAD), lambda t: (0, 0)),
        out_shape=jax.ShapeDtypeStruct((M, IN_COLS_PAD), F32),
        compiler_params=_params(("arbitrary",)),
    )(h, *pieces)


def rmsnorm_fwd_call(name, x, g, tm=512, ex=None):
    T, d = x.shape

    def body(x_ref, g_ref, o_ref):
        x = x_ref[...]
        o_ref[...] = ((x * _rms_r(x, d)) * g_ref[...]).astype(BF16)

    row = pl.BlockSpec((tm, d), lambda i: (i, 0))
    outs, moved = _call(body, ex, name=name, grid=(T // tm,), in_specs=[row, pl.BlockSpec((1, d), lambda i: (0, 0))],
                        out_specs=[row], out_shape=[jax.ShapeDtypeStruct((T, d), BF16)], args=(x, g))
    return outs[0] if ex is None else (outs[0], moved)


def rmsnorm_bwd_call(name, x, g, dy, res, tm=512, ex=None):
    T, d = x.shape

    def body(x_ref, g_ref, dy_ref, r_ref, dx_ref, dg_ref):
        @pl.when(pl.program_id(0) == 0)
        def _():
            dg_ref[...] = jnp.zeros_like(dg_ref)

        dx, dg = _rms_bwd(x_ref[...], g_ref[...], dy_ref[...], d)
        dx_ref[...] = r_ref[...] + dx
        dg_ref[...] += dg

    row = pl.BlockSpec((tm, d), lambda i: (i, 0))
    vec = pl.BlockSpec((1, d), lambda i: (0, 0))
    outs, moved = _call(body, ex, name=name, grid=(T // tm,), in_specs=[row, vec, row, row], out_specs=[row, vec],
                        out_shape=[jax.ShapeDtypeStruct((T, d), F32), jax.ShapeDtypeStruct((1, d), F32)],
                        args=(x, g, dy, res))
    return tuple(outs) if ex is None else tuple(outs) + (moved,)


def outnorm_fwd_call(o_sb, o_mla, g_sb, g_mla, tm=512):
    T = o_sb.shape[0]

    def body(a_ref, b_ref, ga_ref, gb_ref, o_ref):
        a = a_ref[...]
        b = b_ref[...]
        ya = (a * _rms_r(a, ATT_W)) * ga_ref[...]
        yb = (b * _rms_r(b, ATT_W)) * gb_ref[...]
        o_ref[...] = jnp.concatenate([ya, yb], axis=1).astype(BF16)

    row = pl.BlockSpec((tm, ATT_W), lambda i: (i, 0))
    vec = pl.BlockSpec((1, ATT_W), lambda i: (0, 0))
    return pl.pallas_call(
        body, name="outnorm_fwd", grid=(T // tm,), in_specs=[row, row, vec, vec],
        out_specs=pl.BlockSpec((tm, 2 * ATT_W), lambda i: (i, 0)),
        out_shape=jax.ShapeDtypeStruct((T, 2 * ATT_W), BF16),
        compiler_params=_params(("parallel",)),
    )(o_sb, o_mla, g_sb, g_mla)


def outnorm_bwd_call(o_sb, o_mla, g_sb, g_mla, do_cat, tm=512, ex=None):
    T = o_sb.shape[0]

    def body(a_ref, b_ref, ga_ref, gb_ref, d_ref, da_ref, db_ref, dga_ref, dgb_ref):
        @pl.when(pl.program_id(0) == 0)
        def _():
            dga_ref[...] = jnp.zeros_like(dga_ref)
            dgb_ref[...] = jnp.zeros_like(dgb_ref)

        d = d_ref[...]
        da, dga = _rms_bwd(a_ref[...], ga_ref[...], d[:, :ATT_W], ATT_W)
        db, dgb = _rms_bwd(b_ref[...], gb_ref[...], d[:, ATT_W:], ATT_W)
        da_ref[...] = da
        db_ref[...] = db
        dga_ref[...] += dga
        dgb_ref[...] += dgb

    row = pl.BlockSpec((tm, ATT_W), lambda i: (i, 0))
    vec = pl.BlockSpec((1, ATT_W), lambda i: (0, 0))
    outs, moved = _call(
        body, ex, name="outnorm_bwd", grid=(T // tm,),
        in_specs=[row, row, vec, vec, pl.BlockSpec((tm, 2 * ATT_W), lambda i: (i, 0))],
        out_specs=[row, row, vec, vec],
        out_shape=[jax.ShapeDtypeStruct((T, ATT_W), F32), jax.ShapeDtypeStruct((T, ATT_W), F32),
                   jax.ShapeDtypeStruct((1, ATT_W), F32), jax.ShapeDtypeStruct((1, ATT_W), F32)],
        args=(o_sb, o_mla, g_sb, g_mla, do_cat))
    return tuple(outs) if ex is None else tuple(outs) + (moved,)


def final_loss_call(x2, g, target, tm=512):
    T, d = x2.shape

    def body(x_ref, g_ref, t_ref, dx_ref, dxb_ref, loss_ref, dg_ref):
        @pl.when(pl.program_id(0) == 0)
        def _():
            loss_ref[...] = jnp.zeros_like(loss_ref)
            dg_ref[...] = jnp.zeros_like(dg_ref)

        x = x_ref[...]
        g = g_ref[...]
        y = (x * _rms_r(x, d)) * g
        err = y - t_ref[...]
        loss_ref[...] += jnp.sum(jnp.sum(err * err, axis=1, keepdims=True), axis=0, keepdims=True) * (0.5 / d)
        dx, dg = _rms_bwd(x, g, err * (1.0 / d), d)
        dx_ref[...] = dx
        dxb_ref[...] = dx.astype(BF16)
        dg_ref[...] += dg

    row = pl.BlockSpec((tm, d), lambda i: (i, 0))
    vec = pl.BlockSpec((1, d), lambda i: (0, 0))
    return pl.pallas_call(
        body, name="final_loss", grid=(T // tm,), in_specs=[row, vec, row],
        out_specs=[row, row, pl.BlockSpec((1, LANES), lambda i: (0, 0)), vec],
        out_shape=[jax.ShapeDtypeStruct((T, d), F32), jax.ShapeDtypeStruct((T, d), BF16),
                   jax.ShapeDtypeStruct((1, LANES), F32), jax.ShapeDtypeStruct((1, d), F32)],
        compiler_params=_params(("arbitrary",)),
    )(x2, g, target)


def rope_tab_call(pos, inv_freq, tm=512):
    T = pos.shape[0]

    def body(p_ref, f_ref, c_ref, s_ref):
        ang = p_ref[...].astype(F32) * f_ref[...]
        lane = lax.broadcasted_iota(jnp.int32, ang.shape, 1)
        sn = jnp.sin(ang)
        c_ref[...] = jnp.cos(ang)
        s_ref[...] = jnp.where((lane & 31) < 16, -sn, sn)

    row = pl.BlockSpec((tm, LANES), lambda i: (i, 0))
    return pl.pallas_call(
        body, name="rope_tab", grid=(T // tm,),
        in_specs=[pl.BlockSpec((tm, 1), lambda i: (i, 0)), pl.BlockSpec((1, LANES), lambda i: (0, 0))],
        out_specs=[row, row],
        out_shape=[jax.ShapeDtypeStruct((T, LANES), F32)] * 2,
        compiler_params=_params(("parallel",)),
    )(pos, inv_freq)


def mla_prep_fwd_call(p, cos, sin, g_cq, g_ckv, w_uq_p, w_ukv_p, tm=512):
    T = p.shape[0]

    def body(cq_ref, ckvr_ref, c_ref, s_ref, gq_ref, gkv_ref, wq_ref, wkv_ref,
             qn_ref, qr_ref, kn_ref, vm_ref, krt_ref, cqn_ref, ckvn_ref):
        c = c_ref[...]
        s = s_ref[...]
        cq = cq_ref[...]
        cqn = ((cq * _rms_r(cq, Q_RANK)) * gq_ref[...]).astype(BF16)
        cqn_ref[...] = cqn
        q = _nn(cqn, wq_ref[...])
        qn_ref[...] = q[:, :ATT_W].astype(BF16)
        for g in range(ROPE_W // LANES):
            qr = q[:, ATT_W + g * LANES:ATT_W + (g + 1) * LANES]
            qr_ref[:, g * LANES:(g + 1) * LANES] = (qr * c + _rot(qr) * s).astype(BF16)
        ckvr = ckvr_ref[...]
        ckv = ckvr[:, :KV_RANK]
        ckvn = ((ckv * _rms_r(ckv, KV_RANK)) * gkv_ref[...]).astype(BF16)
        ckvn_ref[...] = ckvn
        kv = _nn(ckvn, wkv_ref[...])
        kn_ref[...] = kv[:, :ATT_W].astype(BF16)
        vm_ref[...] = kv[:, ATT_W:].astype(BF16)
        kr = _fold4(ckvr[:, KV_RANK:])
        krt_ref[...] = (kr * c + _rot(kr) * s).astype(BF16)

    def row(w, j=0):
        return pl.BlockSpec((tm, w), lambda i: (i, j))

    def full(a):
        return pl.BlockSpec(a.shape, lambda i: (0, 0))

    return pl.pallas_call(
        body, name="mla_prep_fwd", grid=(T // tm,),
        in_specs=[row(Q_RANK, 4), row(Q_RANK, 5), row(LANES), row(LANES), full(g_cq), full(g_ckv),
                  full(w_uq_p), full(w_ukv_p)],
        out_specs=[row(ATT_W), row(ROPE_W), row(ATT_W), row(ATT_W), row(LANES), row(Q_RANK), row(KV_RANK)],
        out_shape=[jax.ShapeDtypeStruct((T, w), BF16) for w in (ATT_W, ROPE_W, ATT_W, ATT_W, LANES, Q_RANK, KV_RANK)],
        compiler_params=_params(("parallel",)),
    )(p, p, cos, sin, g_cq, g_ckv, w_uq_p, w_ukv_p)


def mla_prep_bwd_call(p, cos, sin, g_cq, g_ckv, w_uq_p, w_ukv_p, dqn, dqr4, dkn, dvm, dkrt4, tm=512):
    T = p.shape[0]

    def body(cq_ref, ckvr_ref, c_ref, s_ref, gq_ref, gkv_ref, wq_ref, wkv_ref,
             dqn_ref, dqr4_ref, dkn_ref, dvm_ref, dkrt4_ref,
             dcq_ref, dckvr_ref, dq_ref, dkv_ref, dgq_ref, dgkv_ref):
        @pl.when(pl.program_id(0) == 0)
        def _():
            dgq_ref[...] = jnp.zeros_like(dgq_ref)
            dgkv_ref[...] = jnp.zeros_like(dgkv_ref)

        c = c_ref[...]
        s = s_ref[...]
        d4 = dqr4_ref[...]
        dqr = [d4[:, :128] + d4[:, 128:256], d4[:, 256:384] + d4[:, 384:]]
        dqr = [t * c + _rot(t * s) for t in dqr]
        dq = jnp.concatenate([dqn_ref[...]] + dqr, axis=1).astype(BF16)
        dq_ref[...] = dq
        dcq, dgq = _rms_bwd(cq_ref[...], gq_ref[...], _nt(dq, wq_ref[...]), Q_RANK)
        dcq_ref[...] = dcq
        dgq_ref[...] += dgq
        dkv = jnp.concatenate([dkn_ref[...], dvm_ref[...]], axis=1).astype(BF16)
        dkv_ref[...] = dkv
        ckvr = ckvr_ref[...]
        dckv, dgkv = _rms_bwd(ckvr[:, :KV_RANK], gkv_ref[...], _nt(dkv, wkv_ref[...]), KV_RANK)
        dgkv_ref[...] += dgkv
        k4 = dkrt4_ref[...]
        dkr = _fold4(k4[:, :128] + k4[:, 128:256] + k4[:, 256:384] + k4[:, 384:])
        dkr = dkr * c + _rot(dkr * s)
        lane = lax.broadcasted_iota(jnp.int32, dkr.shape, 1)
        dckvr_ref[...] = jnp.concatenate([dckv, jnp.where(lane < ROPE_DIM, dkr, 0.0)], axis=1)

    def row(w, j=0):
        return pl.BlockSpec((tm, w), lambda i: (i, j))

    def full(a):
        return pl.BlockSpec(a.shape, lambda i: (0, 0))

    return pl.pallas_call(
        body, name="mla_prep_bwd", grid=(T // tm,),
        in_specs=[row(Q_RANK, 4), row(Q_RANK, 5), row(LANES), row(LANES), full(g_cq), full(g_ckv),
                  full(w_uq_p), full(w_ukv_p), row(ATT_W), row(ATT_W), row(ATT_W), row(ATT_W), row(ATT_W)],
        out_specs=[row(Q_RANK), row(Q_RANK), row(ATT_W + ROPE_W), row(2 * ATT_W),
                   pl.BlockSpec((1, Q_RANK), lambda i: (0, 0)), pl.BlockSpec((1, KV_RANK), lambda i: (0, 0))],
        out_shape=[jax.ShapeDtypeStruct((T, Q_RANK), F32), jax.ShapeDtypeStruct((T, Q_RANK), F32),
                   jax.ShapeDtypeStruct((T, ATT_W + ROPE_W), BF16), jax.ShapeDtypeStruct((T, 2 * ATT_W), BF16),
                   jax.ShapeDtypeStruct((1, Q_RANK), F32), jax.ShapeDtypeStruct((1, KV_RANK), F32)],
        compiler_params=_params(("arbitrary",)),
    )(p, p, cos, sin, g_cq, g_ckv, w_uq_p, w_ukv_p, dqn, dqr4, dkn, dvm, dkrt4)


def _iota2(shape, axis):
    return lax.broadcasted_iota(jnp.int32, shape, axis)


def _head_masks():
    lane = _iota2((1, LANES), 1)
    return lane < HEAD_DIM, lane >= HEAD_DIM


def _pair(x, masks, dtype=BF16):
    return [jnp.where(m, x, 0.0).astype(dtype) for m in masks]


def _log_gates(z):
    keep = jnp.maximum(z, 0.0) + jnp.log(1.0 + jnp.exp(-jnp.abs(z)))
    return z - keep, keep


def _last_row(x):
    return _row_of(x[x.shape[0] - 8:, :], 7)


def _lane_selector(group):
    return jnp.where(_iota2((16, LANES), 1) // group == _iota2((16, LANES), 0), 1.0, 0.0).astype(BF16)


def _rows8(sel_t, x):
    hi = x.astype(BF16)
    r1 = x - hi.astype(F32)
    mid = r1.astype(BF16)
    lo = (r1 - mid.astype(F32)).astype(BF16)
    return _nt(sel_t, hi) + _nt(sel_t, mid) + _nt(sel_t, lo)


def _row_of(x8, j):
    return jnp.sum(jnp.where(_iota2(x8.shape, 0) == j, x8, 0.0), axis=0, keepdims=True)


def sb_fwd_call(p, B, S, ex=None):
    T = B * S
    TQ, TK = ATT_TQ, ATT_TK
    nq = S // TQ

    def body(q_ref, k_ref, v_ref, o_ref, lt_ref):
        qi = pl.program_id(2)
        masks = _head_masks()
        qm = [_pair(q_ref[:, sl] * SB_SCALE, masks) for sl in PAIR_LANES]
        row = _iota2((TQ, TK), 0)
        col = _iota2((TQ, TK), 1)
        tri = jnp.where(row > col, 1.0, 0.0).astype(BF16)
        tri2 = jnp.concatenate([tri, tri], axis=0)
        vis = col < row
        o_ref[...] = jnp.zeros_like(o_ref)

        def group(k0, pairs, carry, diag):
            heads = [(pp, j) for pp in pairs for j in range(2)]
            n = range(len(heads))
            k = {pp: k_ref[pl.ds(k0, TK), PAIR_LANES[pp]].astype(BF16) for pp in pairs}
            vm = {pp: _pair(v_ref[pl.ds(k0, TK), PAIR_LANES[pp]], masks) for pp in pairs}
            gates = [_log_gates(_nt(qm[pp][j], k[pp])) for pp, j in heads]
            lb = [g[0] for g in gates]
            keep = [jnp.where(vis, g[1], 0.0) if diag else g[1] for g in gates]
            tail = [_nn(jnp.concatenate(_split2(keep[h]), axis=1), tri2) + carry[h] for h in n]
            a = [jnp.exp(lb[h] - tail[h]) for h in n]
            if diag:
                a = [jnp.where(vis, x, 0.0) for x in a]
            ab = [x.astype(BF16) for x in a]
            for i, pp in enumerate(pairs):
                o_ref[:, PAIR_LANES[pp]] += _nn(ab[2 * i], vm[pp][0]) + _nn(ab[2 * i + 1], vm[pp][1])
            return [carry[h] + jnp.sum(keep[h], axis=1, keepdims=True) for h in n]

        def step(kb, carry, diag):
            k0 = pl.multiple_of(kb * TK, TK)
            out = []
            for g in range(0, ATT_PAIRS, SB_FWD_GROUP):
                out += group(k0, list(range(g, g + SB_FWD_GROUP)), carry[2 * g:2 * (g + SB_FWD_GROUP)], diag)
            return tuple(out)

        zero = jnp.zeros((TQ, 1), F32)
        carry = step(qi, (zero,) * (2 * ATT_PAIRS), True)
        carry = lax.fori_loop(0, qi, lambda i, c: step(qi - 1 - i, c, False), carry)
        lane = _iota2((TQ, LANES), 1)
        for pp, sl in enumerate(PAIR_LANES):
            lt_ref[:, sl] = jnp.where(lane == 0, carry[2 * pp], jnp.where(lane == 1, carry[2 * pp + 1], 0.0))

    W = ATT_PAIRS * LANES
    qspec = pl.BlockSpec((TQ, W), lambda b, h, i: (b * nq + i, h))
    outs, moved = _call(
        body, ex, name="sb_fwd", grid=(B, HEADS // 2 // ATT_PAIRS, nq),
        in_specs=[qspec,
                  pl.BlockSpec((S, W), lambda b, h, i: (b, ATT_W // W + h)),
                  pl.BlockSpec((S, W), lambda b, h, i: (b, 2 * ATT_W // W + h))],
        out_specs=[qspec, qspec],
        out_shape=[jax.ShapeDtypeStruct((T, ATT_W), F32)] * 2, args=(p, p, p))
    return tuple(outs) if ex is None else tuple(outs) + (moved,)


def sb_bwd_call(p, lt, do, B, S, ex=None):
    T = B * S
    TQ, TK = ATT_TQ, ATT_TK
    nq = S // TQ

    def body(q_ref, k_ref, v_ref, lt_ref, do_ref, dq_ref, dk_ref, dv_ref):
        qi = pl.program_id(2)

        @pl.when(qi == 0)
        def _():
            dk_ref[...] = jnp.zeros_like(dk_ref)
            dv_ref[...] = jnp.zeros_like(dv_ref)

        masks = _head_masks()
        qm = [_pair(q_ref[:, sl] * SB_SCALE, masks) for sl in PAIR_LANES]
        dom = [_pair(do_ref[:, sl], masks) for sl in PAIR_LANES]
        start = []
        for sl in PAIR_LANES:
            l8 = _rows8(_lane_selector(1), lt_ref[:, sl])
            start += [-_row_of(l8, 0), jnp.zeros((1, TQ), F32), -_row_of(l8, 1), jnp.zeros((1, TQ), F32)]
        row = _iota2((TK, TQ), 0)
        col = _iota2((TK, TQ), 1)
        incl = jnp.where(col <= row, 1.0, 0.0).astype(BF16)
        incl2 = jnp.concatenate([incl, incl], axis=1)
        excl = jnp.where(col < row, 1.0, 0.0).astype(BF16)
        vis = row < col
        dq_ref[...] = jnp.zeros_like(dq_ref)

        def group(k0, pairs, carry, diag):
            heads = [(pp, j) for pp in pairs for j in range(2)]
            n = range(len(heads))
            kf = {pp: k_ref[pl.ds(k0, TK), PAIR_LANES[pp]] for pp in pairs}
            km = {pp: _pair(kf[pp], masks) for pp in pairs}
            v = {pp: v_ref[pl.ds(k0, TK), PAIR_LANES[pp]].astype(BF16) for pp in pairs}
            z = [_nt(kf[pp].astype(BF16), qm[pp][j]) for pp, j in heads]
            da = [_nt(v[pp], dom[pp][j]) for pp, j in heads]
            gates = [_log_gates(x) for x in z]
            lb = [g[0] for g in gates]
            keep = [jnp.where(vis, g[1], 0.0) if diag else g[1] for g in gates]
            left = [_nn(incl2, jnp.concatenate(_split2(keep[h]), axis=0)) + carry[2 * h] for h in n]
            a = [jnp.exp(lb[h] + left[h]) for h in n]
            if diag:
                a = [jnp.where(vis, x, 0.0) for x in a]
            e = [a[h] * da[h] for h in n]
            before = [_nn(excl, e[h].astype(BF16)) + carry[2 * h + 1] for h in n]
            dz = [e[h] - jnp.exp(lb[h]) * (e[h] + before[h]) for h in n]
            if diag:
                dz = [jnp.where(vis, x, 0.0) for x in dz]
            dzb = [x.astype(BF16) for x in dz]
            ab = [x.astype(BF16) for x in a]
            out = []
            for h in n:
                out += [_last_row(left[h]), _last_row(before[h]) + _last_row(e[h])]
            for i, pp in enumerate(pairs):
                sl = PAIR_LANES[pp]
                dk_ref[pl.ds(k0, TK), sl] += _nn(dzb[2 * i], qm[pp][0]) + _nn(dzb[2 * i + 1], qm[pp][1])
                dv_ref[pl.ds(k0, TK), sl] += _nn(ab[2 * i], dom[pp][0]) + _nn(ab[2 * i + 1], dom[pp][1])
                dq_ref[:, sl] += _tn(dzb[2 * i], km[pp][0]) + _tn(dzb[2 * i + 1], km[pp][1])
            return out

        def step(kb, carry, diag):
            k0 = pl.multiple_of(kb * TK, TK)
            out = []
            for g in range(0, ATT_PAIRS, SB_BWD_GROUP):
                out += group(k0, list(range(g, g + SB_BWD_GROUP)), carry[4 * g:4 * (g + SB_BWD_GROUP)], diag)
            return tuple(out)

        carry = lax.fori_loop(0, qi, lambda i, c: step(i, c, False), tuple(start))
        step(qi, carry, True)
        dq_ref[...] *= SB_SCALE

    W = ATT_PAIRS * LANES
    qspec = pl.BlockSpec((TQ, W), lambda b, h, i: (b * nq + i, h))
    sspec = pl.BlockSpec((S, W), lambda b, h, i: (b, h))
    outs, moved = _call(
        body, ex, name="sb_bwd", grid=(B, HEADS // 2 // ATT_PAIRS, nq),
        in_specs=[qspec,
                  pl.BlockSpec((S, W), lambda b, h, i: (b, ATT_W // W + h)),
                  pl.BlockSpec((S, W), lambda b, h, i: (b, 2 * ATT_W // W + h)),
                  qspec, qspec],
        out_specs=[qspec, sspec, sspec],
        out_shape=[jax.ShapeDtypeStruct((T, ATT_W), F32)] * 3, args=(p, p, p, lt, do))
    return tuple(outs) if ex is None else tuple(outs) + (moved,)


ALL_PAIRS = [slice(i * LANES, (i + 1) * LANES) for i in range(HEADS // 2)]


def _rope_masks(hp):
    grp = _iota2((1, LANES), 1) // ROPE_DIM
    return [grp == ((2 * hp + j) % 4) for j in range(2)]


def _mla_queries(qn_ref, qr_ref, masks):
    out = []
    for pp, sl in enumerate(ALL_PAIRS):
        qnv = qn_ref[:, sl]
        qrv = qr_ref[:, ALL_PAIRS[pp // 2]]
        rmasks = _rope_masks(pp)
        out.append([jnp.concatenate([jnp.where(masks[j], qnv, 0), jnp.where(rmasks[j], qrv, 0)], axis=1).astype(BF16)
                    for j in range(2)])
    return out


def mla_fwd_call(qn, qr, kn, krt, vm, B, S):
    T = B * S
    TQ, TK = ATT_TQ, ATT_TK
    nq = S // TQ

    def body(qn_ref, qr_ref, kn_ref, kr_ref, v_ref, o_ref, lse_ref):
        qi = pl.program_id(1)
        masks = _head_masks()
        qcat = _mla_queries(qn_ref, qr_ref, masks)
        row = _iota2((TQ, TK), 0)
        col = _iota2((TQ, TK), 1)
        vis = col <= row
        o_ref[...] = jnp.zeros_like(o_ref)

        def group(k0, pairs, carry, diag):
            heads = [(pp, j) for pp in pairs for j in range(2)]
            n = range(len(heads))
            krv = kr_ref[pl.ds(k0, TK), :]
            kcat = {pp: jnp.concatenate([kn_ref[pl.ds(k0, TK), ALL_PAIRS[pp]], krv], axis=1) for pp in pairs}
            vmk = {pp: _pair(v_ref[pl.ds(k0, TK), ALL_PAIRS[pp]], masks) for pp in pairs}
            s = [_nt(qcat[pp][j], kcat[pp]) * MLA_SCALE2 for pp, j in heads]
            if diag:
                s = [jnp.where(vis, x, NEG_BIG) for x in s]
            m_new = [jnp.maximum(carry[2 * h], jnp.max(s[h], axis=1, keepdims=True)) for h in n]
            alpha = [jnp.exp2(carry[2 * h] - m_new[h]) for h in n]
            pexp = [jnp.exp2(s[h] - m_new[h]) for h in n]
            out = []
            for h in n:
                out += [m_new[h], alpha[h] * carry[2 * h + 1] + jnp.sum(pexp[h], axis=1, keepdims=True)]
            pb = [x.astype(BF16) for x in pexp]
            for i, pp in enumerate(pairs):
                sl = ALL_PAIRS[pp]
                scale = jnp.where(masks[0], alpha[2 * i], alpha[2 * i + 1])
                o_ref[:, sl] = o_ref[:, sl] * scale + (_nn(pb[2 * i], vmk[pp][0]) + _nn(pb[2 * i + 1], vmk[pp][1]))
            return out

        def step(kb, carry, diag):
            k0 = pl.multiple_of(kb * TK, TK)
            out = []
            for g in range(0, len(ALL_PAIRS), MLA_GROUP):
                out += group(k0, list(range(g, g + MLA_GROUP)), carry[4 * g:4 * (g + MLA_GROUP)], diag)
            return tuple(out)

        neg = jnp.full((TQ, 1), NEG_BIG, F32)
        zero = jnp.zeros((TQ, 1), F32)
        carry = step(qi, (neg, zero) * (2 * len(ALL_PAIRS)), True)
        carry = lax.fori_loop(0, qi, lambda i, c: step(qi - 1 - i, c, False), carry)
        lane = _iota2((TQ, LANES), 1)
        for pp, sl in enumerate(ALL_PAIRS):
            m0, l0, m1, l1 = carry[4 * pp:4 * pp + 4]
            o_ref[:, sl] = o_ref[:, sl] * jnp.where(masks[0], 1.0 / l0, 1.0 / l1)
            lse_ref[:, sl] = jnp.where(lane == 0, m0 * LN2 + jnp.log(l0), jnp.where(lane == 1, m1 * LN2 + jnp.log(l1), 0.0))

    def rows(w):
        return pl.BlockSpec((TQ, w), lambda b, i: (b * nq + i, 0))

    def seq(w):
        return pl.BlockSpec((S, w), lambda b, i: (b, 0))

    return pl.pallas_call(
        body, name="mla_fwd", grid=(B, nq),
        in_specs=[rows(ATT_W), rows(ROPE_W), seq(ATT_W), seq(LANES), seq(ATT_W)],
        out_specs=[rows(ATT_W), rows(ATT_W)],
        out_shape=[jax.ShapeDtypeStruct((T, ATT_W), F32)] * 2,
        compiler_params=_params(("arbitrary", "arbitrary")),
    )(qn, qr, kn, krt, vm)


def mla_bwd_call(qn, qr, kn, krt, vm, o, lse, do, B, S, ex=None):
    T = B * S
    TQ, TK = ATT_TQ, ATT_TK
    nq = S // TQ

    def body(qn_ref, qr_ref, kn_ref, kr_ref, v_ref, o_ref, lse_ref, do_ref,
             dqn_ref, dqr_ref, dkn_ref, dv_ref, dkr_ref):
        qi = pl.program_id(1)

        @pl.when(qi == 0)
        def _():
            dkn_ref[...] = jnp.zeros_like(dkn_ref)
            dv_ref[...] = jnp.zeros_like(dv_ref)
            dkr_ref[...] = jnp.zeros_like(dkr_ref)

        masks = _head_masks()
        qcat = _mla_queries(qn_ref, qr_ref, masks)
        dom, dsum, lse = [], [], []
        for sl in ALL_PAIRS:
            do = do_ref[:, sl]
            dom.append(_pair(do, masks))
            d8 = _rows8(_lane_selector(HEAD_DIM), do * o_ref[:, sl])
            l8 = _rows8(_lane_selector(1), lse_ref[:, sl])
            dsum.append([_row_of(d8, j) for j in range(2)])
            lse.append([_row_of(l8, j) * LOG2E for j in range(2)])
        row = _iota2((TK, TQ), 0)
        col = _iota2((TK, TQ), 1)
        vis = row <= col
        dqn_ref[...] = jnp.zeros_like(dqn_ref)
        dqr_ref[...] = jnp.zeros_like(dqr_ref)

        def group(k0, pairs, diag):
            heads = [(pp, j) for pp in pairs for j in range(2)]
            n = range(len(heads))
            krv = kr_ref[pl.ds(k0, TK), :]
            knv = {pp: kn_ref[pl.ds(k0, TK), ALL_PAIRS[pp]] for pp in pairs}
            kcat = {pp: jnp.concatenate([knv[pp], krv], axis=1) for pp in pairs}
            v = {pp: v_ref[pl.ds(k0, TK), ALL_PAIRS[pp]] for pp in pairs}
            s = [_nt(kcat[pp], qcat[pp][j]) * MLA_SCALE2 for pp, j in heads]
            dp_ = [_nt(v[pp], dom[pp][j]) for pp, j in heads]
            pr = [jnp.exp2(s[h] - lse[pp][j]) for h, (pp, j) in enumerate(heads)]
            if diag:
                pr = [jnp.where(vis, x, 0.0) for x in pr]
            ds = [(pr[h] * (dp_[h] - dsum[pp][j]) * MLA_SCALE).astype(BF16) for h, (pp, j) in enumerate(heads)]
            pb = [x.astype(BF16) for x in pr]
            for i, pp in enumerate(pairs):
                sl = ALL_PAIRS[pp]
                rmasks = _rope_masks(pp)
                kcat_j = [jnp.concatenate([jnp.where(masks[j], knv[pp], 0), jnp.where(rmasks[j], krv, 0)],
                                          axis=1).astype(BF16) for j in range(2)]
                dv_ref[pl.ds(k0, TK), sl] += _nn(pb[2 * i], dom[pp][0]) + _nn(pb[2 * i + 1], dom[pp][1])
                dk = _nn(ds[2 * i], qcat[pp][0]) + _nn(ds[2 * i + 1], qcat[pp][1])
                dq = _tn(ds[2 * i], kcat_j[0]) + _tn(ds[2 * i + 1], kcat_j[1])
                dqn_ref[:, sl] += dq[:, :LANES]
                dqr_ref[:, sl] += dq[:, LANES:]
                dkn_ref[pl.ds(k0, TK), sl] += dk[:, :LANES]
                dkr_ref[pl.ds(k0, TK), sl] += dk[:, LANES:]

        def step(kb, diag):
            k0 = pl.multiple_of(kb * TK, TK)
            for g in range(0, len(ALL_PAIRS), MLA_GROUP):
                group(k0, list(range(g, g + MLA_GROUP)), diag)

        step(qi, True)

        def loop(i, c):
            step(qi - 1 - i, False)
            return c

        lax.fori_loop(0, qi, loop, 0)

    def rows(w):
        return pl.BlockSpec((TQ, w), lambda b, i: (b * nq + i, 0))

    def seq(w):
        return pl.BlockSpec((S, w), lambda b, i: (b, 0))

    outs, moved = _call(
        body, ex, name="mla_bwd", grid=(B, nq),
        in_specs=[rows(ATT_W), rows(ROPE_W), seq(ATT_W), seq(LANES), seq(ATT_W), rows(ATT_W), rows(ATT_W), rows(ATT_W)],
        out_specs=[rows(ATT_W), rows(ATT_W), seq(ATT_W), seq(ATT_W), seq(ATT_W)],
        out_shape=[jax.ShapeDtypeStruct((T, ATT_W), F32)] * 5, args=(qn, qr, kn, krt, vm, o, lse, do))
    return tuple(outs) if ex is None else tuple(outs) + (moved,)


CONV_TC = 256


def _shift_down(x, n):
    return jnp.where(_iota2(x.shape, 0) >= n, pltpu.roll(x, n, 0), 0.0)


def _shift_up(x, n):
    rows = x.shape[0]
    return jnp.where(_iota2(x.shape, 0) < rows - n, pltpu.roll(x, rows - n, 0), 0.0)


def _taps(w_ref):
    return [w_ref[k:k + 1, :] for k in range(3)]


def _conv3(u, w, b):
    return w[0] * _shift_down(u, 2) + w[1] * _shift_down(u, 1) + w[2] * u + b


def _ref_shift_down(ref, n):
    rows = ref.shape[0]
    return jnp.concatenate([_shift_down(ref[0:8, :], n), ref[8 - n:rows - n, :]], axis=0)


def _conv3_ref(u_ref, w, b):
    return w[0] * _ref_shift_down(u_ref, 2) + w[1] * _ref_shift_down(u_ref, 1) + w[2] * u_ref[...] + b


def conv_act_fwd_call(ug, uv, conv_w, conv_b, B, S):
    T = B * S
    nc = D_FF // CONV_TC

    def body(ug_ref, uv_ref, wg_ref, wv_ref, bg_ref, bv_ref, a_ref, cg_ref, cv_ref):
        gate = _conv3_ref(ug_ref, _taps(wg_ref), bg_ref[...])
        val = _conv3_ref(uv_ref, _taps(wv_ref), bv_ref[...])
        a_ref[...] = (gate * (1.0 / (1.0 + jnp.exp(-gate))) * val).astype(BF16)
        cg_ref[...] = gate.astype(BF16)
        cv_ref[...] = val.astype(BF16)

    def blk(rows, off):
        return pl.BlockSpec((rows, CONV_TC), lambda b, j: (b if rows == S else 0, off + j))

    return pl.pallas_call(
        body, name="conv_act_fwd", grid=(B, nc),
        in_specs=[blk(S, 0), blk(S, 0), blk(3, 0), blk(3, nc), blk(1, 0), blk(1, nc)],
        out_specs=[blk(S, 0)] * 3,
        out_shape=[jax.ShapeDtypeStruct((T, D_FF), BF16)] * 3,
        compiler_params=_params(("parallel", "parallel")),
    )(ug, uv, conv_w, conv_w, conv_b, conv_b)


def conv_act_bwd_call(ug, uv, cg, cv, dx2, w_down, conv_w, B, S):
    T = B * S
    nc = D_FF // CONV_TC

    def body(ug_ref, uv_ref, cg_ref, cv_ref, dx_ref, wd_ref, wg_ref, wv_ref,
             dug_ref, duv_ref, dwg_ref, dwv_ref, dbg_ref, dbv_ref):
        @pl.when(pl.program_id(1) == 0)
        def _():
            for r in (dwg_ref, dwv_ref, dbg_ref, dbv_ref):
                r[...] = jnp.zeros_like(r)

        gate = cg_ref[...].astype(F32)
        val = cv_ref[...].astype(F32)
        da = _nt(dx_ref[...], wd_ref[...])
        sig = 1.0 / (1.0 + jnp.exp(-gate))
        dval = da * (gate * sig)
        dgate = da * val * (sig * (1.0 + gate * (1.0 - sig)))
        for u_ref, d, w, du_ref, dw_ref, db_ref in ((ug_ref, dgate, _taps(wg_ref), dug_ref, dwg_ref, dbg_ref),
                                                   (uv_ref, dval, _taps(wv_ref), duv_ref, dwv_ref, dbv_ref)):
            u_ = u_ref[...]
            d1 = _shift_up(d, 1)
            d2 = _shift_up(d, 2)
            du_ref[...] = (w[2] * d + w[1] * d1 + w[0] * d2).astype(BF16)
            db_ref[...] += jnp.sum(d, axis=0, keepdims=True)
            dw_ref[0:1, :] += jnp.sum(d2 * u_, axis=0, keepdims=True)
            dw_ref[1:2, :] += jnp.sum(d1 * u_, axis=0, keepdims=True)
            dw_ref[2:3, :] += jnp.sum(d * u_, axis=0, keepdims=True)

    def blk(rows, off):
        return pl.BlockSpec((rows, CONV_TC), lambda j, b: (b if rows == S else 0, off + j))

    return pl.pallas_call(
        body, name="conv_act_bwd", grid=(nc, B),
        in_specs=[blk(S, 0), blk(S, 0), blk(S, 0), blk(S, 0), pl.BlockSpec((S, D_MODEL), lambda j, b: (b, 0)),
                  pl.BlockSpec((CONV_TC, D_MODEL), lambda j, b: (j, 0)), blk(3, 0), blk(3, nc)],
        out_specs=[blk(S, 0), blk(S, 0), blk(3, 0), blk(3, 0), blk(1, 0), blk(1, 0)],
        out_shape=[jax.ShapeDtypeStruct((T, D_FF), BF16), jax.ShapeDtypeStruct((T, D_FF), BF16),
                   jax.ShapeDtypeStruct((3, D_FF), F32), jax.ShapeDtypeStruct((3, D_FF), F32),
                   jax.ShapeDtypeStruct((1, D_FF), F32), jax.ShapeDtypeStruct((1, D_FF), F32)],
        compiler_params=_params(("parallel", "arbitrary")),
    )(ug, uv, cg, cv, dx2, w_down, conv_w, conv_w)


CHIP_MASKS = ((1, 0), (0, 1), (1, 1))


def _place():
    return lax.axis_index("x"), lax.axis_index("y"), lax.axis_index("c")


HALF_ALIGN = 32


def _any_specs(n):
    return [pl.BlockSpec(memory_space=pl.ANY)] * n


def _half_rows(r, half):
    return pl.ds(pl.multiple_of(half * (r // 2), HALF_ALIGN // 2), r // 2)


def _remote(src, dst, send_sem, recv_sem, device):
    return pltpu.make_async_remote_copy(src_ref=src, dst_ref=dst, send_sem=send_sem, recv_sem=recv_sem,
                                        device_id=device, device_id_type=MESH)


class Exchange:
    def __init__(self, ins, out_shape, sems, start, finish):
        self.ins, self.out_shape, self.sems, self.start, self.finish = list(ins), list(out_shape), list(sems), start, finish


def gather_group(shards):
    n = len(shards)
    split = [s.shape[0] % HALF_ALIGN == 0 for s in shards]

    def rows(w, half):
        return _half_rows(shards[w].shape[0], half) if split[w] else slice(None)

    def copies(ins, outs, sems):
        ici_s, ici_r, _, _, local_sems = sems
        x, y, c = _place()
        chip = 2 * x + y
        local = [pltpu.make_async_copy(ins[w], outs[w].at[chip], local_sems.at[w]) for w in range(n)]
        sends = [_remote(ins[w].at[rows(w, c)], outs[w].at[chip, rows(w, c)], ici_s.at[w, k], ici_r.at[w, k],
                         (x ^ fx, y ^ fy, c))
                 for w in range(n) for k, (fx, fy) in enumerate(CHIP_MASKS)]
        return local, sends

    def start(ins, outs, sems):
        local, sends = copies(ins, outs, sems)
        for cp in local + sends:
            cp.start()

    def finish(ins, outs, sems):
        ici_s, ici_r, d2d_s, d2d_r, _ = sems
        x, y, c = _place()
        sib = (x, y, 1 - c)
        local, sends = copies(ins, outs, sems)
        for w in range(n):
            for k, (fx, fy) in enumerate(CHIP_MASKS):
                landed = outs[w].at[2 * (x ^ fx) + (y ^ fy), rows(w, c)]
                _remote(landed, landed, ici_s.at[w, k], ici_r.at[w, k], sib).wait_recv()
                if split[w]:
                    cp = _remote(landed, landed, d2d_s.at[w, k], d2d_r.at[w, k], sib)
                    cp.start()
                    sends.append(cp)
        for w in range(n):
            for k, (fx, fy) in enumerate(CHIP_MASKS):
                if split[w]:
                    other = outs[w].at[2 * (x ^ fx) + (y ^ fy), rows(w, 1 - c)]
                    _remote(other, other, d2d_s.at[w, k], d2d_r.at[w, k], sib).wait_recv()
        for cp in sends:
            cp.wait_send()
        for cp in local:
            cp.wait()

    sems = pltpu.SemaphoreType.DMA((n, 3))
    return Exchange(shards, [jax.ShapeDtypeStruct((N_CHIPS,) + s.shape, s.dtype) for s in shards],
                    [sems, sems, sems, sems, pltpu.SemaphoreType.DMA((n,))], start, finish)


def swap_half(parts):
    n = len(parts)

    def copies(ins, outs, sems):
        x, y, c = _place()
        return [_remote(ins[w].at[:, _half_rows(parts[w].shape[1], 1 - c)], outs[w], sems[0].at[w], sems[1].at[w],
                        (x, y, 1 - c)) for w in range(n)]

    def start(ins, outs, sems):
        for cp in copies(ins, outs, sems):
            cp.start()

    def finish(ins, outs, sems):
        for cp in copies(ins, outs, sems):
            cp.wait_recv()
            cp.wait_send()

    return Exchange(parts, [jax.ShapeDtypeStruct((N_CHIPS, p.shape[1] // 2, p.shape[2]), F32) for p in parts],
                    [pltpu.SemaphoreType.DMA((n,))] * 2, start, finish)


def scatter_half(halves):
    n = len(halves)

    def copies(ins, outs, sems):
        x, y, c = _place()
        return [_remote(ins[w].at[2 * (x ^ fx) + (y ^ fy)], outs[w].at[k], sems[0].at[w, k], sems[1].at[w, k],
                        (x ^ fx, y ^ fy, c))
                for w in range(n) for k, (fx, fy) in enumerate(CHIP_MASKS)]

    def start(ins, outs, sems):
        for cp in copies(ins, outs, sems):
            cp.start()

    def finish(ins, outs, sems):
        for cp in copies(ins, outs, sems):
            cp.wait_recv()
            cp.wait_send()

    return Exchange(halves, [jax.ShapeDtypeStruct((3,) + h.shape[1:], h.dtype) for h in halves],
                    [pltpu.SemaphoreType.DMA((n, 3))] * 2, start, finish)


def swap_final(finals):
    n = len(finals)

    def copies(ins, outs, sems):
        x, y, c = _place()
        mine = [outs[w].at[_half_rows(2 * finals[w].shape[0], c)] for w in range(n)]
        local = [pltpu.make_async_copy(ins[w], mine[w], sems[2].at[w]) for w in range(n)]
        sends = [_remote(ins[w], mine[w], sems[0].at[w], sems[1].at[w], (x, y, 1 - c)) for w in range(n)]
        return local, sends

    def start(ins, outs, sems):
        local, sends = copies(ins, outs, sems)
        for cp in local + sends:
            cp.start()

    def finish(ins, outs, sems):
        x, y, c = _place()
        local, sends = copies(ins, outs, sems)
        for w in range(n):
            got = outs[w].at[_half_rows(2 * finals[w].shape[0], 1 - c)]
            _remote(got, got, sems[0].at[w], sems[1].at[w], (x, y, 1 - c)).wait_recv()
        for cp in sends:
            cp.wait_send()
        for cp in local:
            cp.wait()

    return Exchange(finals, [jax.ShapeDtypeStruct((2 * f.shape[0], f.shape[1]), F32) for f in finals],
                    [pltpu.SemaphoreType.DMA((n,))] * 3, start, finish)


def exchange_call(name, ex):
    n, m = len(ex.ins), len(ex.out_shape)

    def body(*refs):
        ins, outs, sems = refs[:n], refs[n:n + m], refs[n + m:]
        ex.start(ins, outs, sems)
        ex.finish(ins, outs, sems)

    return pl.pallas_call(body, name=name, in_specs=_any_specs(n), out_specs=_any_specs(m), out_shape=ex.out_shape,
                          scratch_shapes=ex.sems, compiler_params=_params())(*ex.ins)


def _call(body, ex, *, name, grid, in_specs, out_specs, out_shape, args, scratch_shapes=()):
    sem = ("arbitrary",) * len(grid)
    if ex is None:
        outs = pl.pallas_call(body, name=name, grid=grid, in_specs=in_specs, out_specs=out_specs, out_shape=out_shape,
                              scratch_shapes=list(scratch_shapes), compiler_params=_params(sem))(*args)
        return outs, None
    ni, no, ns = len(in_specs), len(out_specs), len(scratch_shapes)
    ne, me = len(ex.ins), len(ex.out_shape)

    def wrapped(*refs):
        own_in, ex_in = refs[:ni], refs[ni:ni + ne]
        own_out, ex_out = refs[ni + ne:ni + ne + no], refs[ni + ne + no:ni + ne + no + me]
        own_scr, ex_sems = refs[ni + ne + no + me:ni + ne + no + me + ns], refs[ni + ne + no + me + ns:]
        ids = [pl.program_id(a) for a in range(len(grid))]
        first = functools.reduce(jnp.logical_and, [i == 0 for i in ids])
        last = functools.reduce(jnp.logical_and, [i == g - 1 for i, g in zip(ids, grid)])

        @pl.when(first)
        def _():
            ex.start(ex_in, ex_out, ex_sems)

        body(*own_in, *own_out, *own_scr)

        @pl.when(last)
        def _():
            ex.finish(ex_in, ex_out, ex_sems)

    outs = pl.pallas_call(
        wrapped, name=name, grid=grid, in_specs=list(in_specs) + _any_specs(ne),
        out_specs=list(out_specs) + _any_specs(me), out_shape=list(out_shape) + ex.out_shape,
        scratch_shapes=list(scratch_shapes) + ex.sems, compiler_params=_params(sem))(*args, *ex.ins)
    return outs[:no], outs[no:]


def _row_tile(rows, cap, mult=8):
    return max(t for t in range(mult, min(rows, cap) + 1, mult) if rows % t == 0)


def add_half_call(name, part, got, where):
    _, rh, cols = got.shape
    tr = _row_tile(rh, 176, 16)
    nb = rh // tr

    def body(where_ref, p_ref, g_ref, own_ref, send_ref):
        t = p_ref[...] + g_ref[...]
        send_ref[...] = t.astype(BF16)
        chip = where_ref[1]
        own_ref[...] = p_ref[chip] + g_ref[chip]

    blk = (N_CHIPS, tr, cols)
    return pl.pallas_call(
        body, name=name,
        grid_spec=pltpu.PrefetchScalarGridSpec(
            num_scalar_prefetch=1, grid=(nb,),
            in_specs=[pl.BlockSpec(blk, lambda i, where_ref: (0, where_ref[0] * nb + i, 0)),
                      pl.BlockSpec(blk, lambda i, where_ref: (0, i, 0))],
            out_specs=[pl.BlockSpec((tr, cols), lambda i, where_ref: (i, 0)),
                       pl.BlockSpec(blk, lambda i, where_ref: (0, i, 0))]),
        out_shape=[jax.ShapeDtypeStruct((rh, cols), F32), jax.ShapeDtypeStruct(got.shape, BF16)],
        compiler_params=_params(("parallel",)),
    )(where, part, got)


def sum_chips_call(name, own, got):
    _, rh, cols = got.shape
    tr = _row_tile(rh, 176, 16)

    def body(h_ref, g_ref, o_ref):
        o_ref[...] = ((h_ref[...] + g_ref[0].astype(F32)) + g_ref[1].astype(F32)) + g_ref[2].astype(F32)

    return pl.pallas_call(
        body, name=name, grid=(rh // tr,),
        in_specs=[pl.BlockSpec((tr, cols), lambda i: (i, 0)), pl.BlockSpec((3, tr, cols), lambda i: (0, i, 0))],
        out_specs=pl.BlockSpec((tr, cols), lambda i: (i, 0)),
        out_shape=jax.ShapeDtypeStruct((rh, cols), F32),
        compiler_params=_params(("parallel",)),
    )(own, got)


def _adamw(w, g, m, v):
    m = ADAM_B1 * m + (1.0 - ADAM_B1) * g
    v = ADAM_B2 * v + (1.0 - ADAM_B2) * (g * g)
    m_hat = m / (1.0 - ADAM_B1 ** ADAM_STEP)
    v_hat = v / (1.0 - ADAM_B2 ** ADAM_STEP)
    delta = -ADAM_LR * (m_hat / (jnp.sqrt(v_hat) + ADAM_EPS) + ADAM_WD * w)
    return delta, m, v


def adamw_call(name, g, w, m, v):
    r, cols = w.shape
    tr = r if r % 8 else _row_tile(r, 256)

    def body(g_ref, w_ref, m_ref, v_ref, d_ref, nm_ref, nv_ref):
        d_ref[...], nm_ref[...], nv_ref[...] = _adamw(w_ref[...], g_ref[...], m_ref[...], v_ref[...])

    spec = pl.BlockSpec((tr, cols), lambda i: (i, 0))
    return pl.pallas_call(
        body, name=name, grid=(r // tr,), in_specs=[spec] * 4, out_specs=[spec] * 3,
        out_shape=[jax.ShapeDtypeStruct((r, cols), F32)] * 3,
        compiler_params=_params(("parallel",)),
    )(g, w, m, v)


def allsum_small_call(v):
    R = v.shape[0]

    def body(v_ref, out_ref, buf, send_sems, recv_sems):
        x, y, c = _place()
        me = 4 * x + 2 * y + c
        buf[me] = v_ref[...]
        sends = []
        for k in range(1, N_DEV):
            fx, fy, fc = (k >> 2) & 1, (k >> 1) & 1, k & 1
            cp = pltpu.make_async_remote_copy(
                src_ref=v_ref, dst_ref=buf.at[me], send_sem=send_sems.at[k - 1], recv_sem=recv_sems.at[k - 1],
                device_id=(x ^ fx, y ^ fy, c ^ fc), device_id_type=MESH)
            cp.start()
            sends.append(cp)
        for k in range(1, N_DEV):
            pltpu.make_async_remote_copy(
                src_ref=v_ref, dst_ref=buf.at[me ^ k], send_sem=send_sems.at[k - 1], recv_sem=recv_sems.at[k - 1],
                device_id=(x, y, c), device_id_type=MESH).wait_recv()
        acc = buf[0]
        for d in range(1, N_DEV):
            acc = acc + buf[d]
        out_ref[...] = acc
        for cp in sends:
            cp.wait_send()

    vm = pl.BlockSpec(memory_space=pltpu.VMEM)
    return pl.pallas_call(
        body, name="allsum_small", in_specs=[vm], out_specs=vm,
        out_shape=jax.ShapeDtypeStruct((R, LANES), F32),
        scratch_shapes=[pltpu.VMEM((N_DEV, R, LANES), F32), pltpu.SemaphoreType.DMA((N_DEV - 1,)),
                        pltpu.SemaphoreType.DMA((N_DEV - 1,))],
        compiler_params=_params(),
    )(v)


def _slab(flat, mult):
    n = flat.shape[-1]
    rows = -(-n // (LANES * mult)) * mult
    flat = jnp.pad(flat, [(0, 0)] * (flat.ndim - 1) + [(0, rows * LANES - n)])
    return flat.reshape(flat.shape[:-1] + (rows, LANES))


def full_from_chips(blocks, by_col):
    _, r, c = blocks.shape
    return blocks.transpose(1, 0, 2).reshape(r, N_CHIPS * c) if by_col else blocks.reshape(N_CHIPS * r, c)


def chips_from_full(full, by_col):
    if by_col:
        r, c = full.shape[0], full.shape[1] // N_CHIPS
        return full.reshape(r, N_CHIPS, c).transpose(1, 0, 2)
    return full.reshape(N_CHIPS, full.shape[0] // N_CHIPS, full.shape[1])


SMALL_PACK = SMALL_W + ("loss", "conv_w")
SMALL_PACK_N = {**SMALL_N, "loss": 1, "conv_w": 3 * 2 * D_FF}


def pack_small(vals):
    zero = jnp.zeros((1,), F32)
    return _slab(jnp.concatenate([vals[n].reshape(-1) if n in vals else jnp.tile(zero, SMALL_PACK_N[n])
                                  for n in SMALL_PACK]), 8)


def unpack_small(slab, shapes):
    flat = slab.reshape(-1)
    out, off = {}, 0
    for n in SMALL_PACK:
        out[n] = flat[off:off + SMALL_PACK_N[n]].reshape(shapes[n])
        off += SMALL_PACK_N[n]
    return out


def _split_heads(w, a, b):
    r = w.shape[0]
    w3 = w.reshape(r, HEADS, a + b)
    return w3[:, :, :a].reshape(r, HEADS * a), w3[:, :, a:].reshape(r, HEADS * b)


def _merge_heads(wa, wb, a, b):
    r = wa.shape[0]
    return jnp.concatenate([wa.reshape(r, HEADS, a), wb.reshape(r, HEADS, b)], axis=2).reshape(r, HEADS * (a + b))


def kernel(x, positions, g_mix, w_in, g_cq, w_uq, g_ckv, w_ukv, g_sb_out, g_mla_out, w_out, g_ffn, w_up, conv_w, conv_b, w_down, g_final, loss_target, m_g_mix, m_w_in, m_g_cq, m_w_uq, m_g_ckv, m_w_ukv, m_g_sb_out, m_g_mla_out, m_w_out, m_g_ffn, m_w_up, m_conv_w, m_conv_b, m_w_down, m_g_final, v_g_mix, v_w_in, v_g_cq, v_w_uq, v_g_ckv, v_w_ukv, v_g_sb_out, v_g_mla_out, v_w_out, v_g_ffn, v_w_up, v_conv_w, v_conv_b, v_w_down, v_g_final):
    given = dict(locals())
    B, S, _ = x.shape
    T = B * S
    w_big = {n: given[n][0] for n in BIG_W}
    m_big = {n: given["m_" + n][0] for n in BIG_W}
    v_big = {n: given["v_" + n][0] for n in BIG_W}

    first = ("w_in", "w_uq", "w_ukv")
    later = ("w_out", "w_up", "w_down", "conv_w")
    x2d = x.reshape(T, D_MODEL)
    h, got_w = rmsnorm_fwd_call("norm_mix", x2d, g_mix, ex=gather_group([w_big[n].astype(BF16) for n in first]))
    full = {n: full_from_chips(g_, BIG_SHARD[n][2]) for n, g_ in zip(first, got_w)}
    gather_later = gather_group([w_big[n] if n == "conv_w" else w_big[n].astype(BF16) for n in later])
    w_in_p = jnp.pad(full["w_in"], ((0, 0), (0, IN_COLS_PAD - IN_COLS)))
    w_uq_p = jnp.concatenate(_split_heads(full["w_uq"], HEAD_DIM, ROPE_DIM), axis=1)
    w_ukv_p = jnp.concatenate(_split_heads(full["w_ukv"], HEAD_DIM, HEAD_DIM), axis=1)

    half = ROPE_DIM // 2
    inv_freq = 1.0 / (ROPE_BASE ** (jnp.arange(half, dtype=F32) * (2.0 / ROPE_DIM)))
    cos, sin = rope_tab_call(positions.reshape(T, 1), jnp.tile(inv_freq, LANES // half).reshape(1, LANES))
    p = matmul_call("proj_in", h, w_in_p, "nn", tn=IN_COLS_PAD // 2)
    qn, qr, kn, vm, krt, cqn, ckvn = mla_prep_fwd_call(p, cos, sin, g_cq, g_ckv, w_uq_p, w_ukv_p)
    o_sb, lt_sb, got_w = sb_fwd_call(p, B, S, ex=gather_later)
    w_up4 = got_w[1]
    full.update({n: full_from_chips(g_, BIG_SHARD[n][2]) for n, g_ in zip(later, got_w) if n != "w_up"})
    conv_w_full = full["conv_w"]
    o_mla, lse = mla_fwd_call(qn, qr, kn, krt, vm, B, S)
    o_cat = outnorm_fwd_call(o_sb, o_mla, g_sb_out, g_mla_out)
    x1 = matmul_call("proj_out", o_cat, full["w_out"], "nn", res=x2d)
    hn = rmsnorm_fwd_call("norm_ffn", x1, g_ffn)
    u_g, u_v = ffn_up_call(hn, w_up4)
    act, c_g, c_v = conv_act_fwd_call(u_g, u_v, conv_w_full, conv_b, B, S)
    x2 = matmul_call("ffn_down", act, full["w_down"], "nn", res=x1)
    dx2, dx2b, loss_row, dg_final = final_loss_call(x2, g_final.reshape(1, D_MODEL), loss_target.reshape(T, D_MODEL))

    xi, yi, ci = _place()
    chip = (2 * xi + yi).astype(jnp.int32).reshape(1)
    where = jnp.stack([ci, 2 * xi + yi]).astype(jnp.int32)

    def add_halves(names, parts, sib_rows):
        return [add_half_call("add_half_" + n, p_, s_, where) for n, p_, s_ in zip(names, parts, sib_rows)]

    def sum_chips(names, halves, from_chips):
        return [sum_chips_call("sum_chips_" + n, h_[0], f_) for n, h_, f_ in zip(names, halves, from_chips)]

    ffn_w = ("w_down", "w_up")
    parts_ffn = [chips_from_full(wgrad_call("wgrad_down", act, dx2b, tn=512), False)]
    du_g, du_v, dcw_g, dcw_v, dcb_g, dcb_v = conv_act_bwd_call(
        u_g, u_v, c_g, c_v, dx2b, full["w_down"], conv_w_full, B, S)
    parts_ffn.append(wgrad_up_call(hn, du_g, du_v))
    dhn, sib_ffn = ffn_up_bwd_call(du_g, du_v, w_up4, swap_half(parts_ffn))
    dx1, dg_ffn = rmsnorm_bwd_call("norm_ffn_bwd", x1, g_ffn, dhn, dx2)
    parts_out = [chips_from_full(wgrad_call("wgrad_out", o_cat, dx1), False)]
    do_cat = matmul_call("proj_out_bwd", dx1, full["w_out"], "nt")
    do_sb, do_mla, dg_sb_out, dg_mla_out, sib_out = outnorm_bwd_call(
        o_sb, o_mla, g_sb_out, g_mla_out, do_cat, ex=swap_half(parts_out))
    early = ffn_w + ("w_out",)
    halves = add_halves(early, parts_ffn + parts_out, list(sib_ffn) + list(sib_out))
    dq_sb, dk_sb, dv_sb, from_chips = sb_bwd_call(p, lt_sb, do_sb, B, S, ex=scatter_half([h_[1] for h_ in halves]))
    finals = sum_chips(early, halves, from_chips)
    dqn, dqr4, dkn, dvm, dkrt4, done = mla_bwd_call(qn, qr, kn, krt, vm, o_mla, lse, do_mla, B, S, ex=swap_final(finals))
    grads = dict(zip(early, done))
    dcq, dckvr, dq_cat, dkv_cat, dg_cq, dg_ckv = mla_prep_bwd_call(
        p, cos, sin, g_cq, g_ckv, w_uq_p, w_ukv_p, dqn, dqr4, dkn, dvm, dkrt4)
    dw_uq_p = wgrad_call("wgrad_uq", cqn, dq_cat)
    dw_ukv_p = wgrad_call("wgrad_ukv", ckvn, dkv_cat)
    dp = (dq_sb, dk_sb, dv_sb, dcq, dckvr)
    late = ("w_uq", "w_ukv", "w_in")
    parts_late = [chips_from_full(g_, True) for g_ in (
        _merge_heads(dw_uq_p[:, :ATT_W], dw_uq_p[:, ATT_W:], HEAD_DIM, ROPE_DIM),
        _merge_heads(dw_ukv_p[:, :ATT_W], dw_ukv_p[:, ATT_W:], HEAD_DIM, HEAD_DIM),
        wgrad_in_call(h, dp)[:, :IN_COLS])]
    dh, sib_late = proj_in_bwd_call(dp, w_in_p, swap_half(parts_late))
    halves = add_halves(late, parts_late, sib_late)
    grad_x, dg_mix, from_chips = rmsnorm_bwd_call(
        "norm_mix_bwd", x2d, g_mix, dh, dx1, ex=scatter_half([h_[1] for h_ in halves]))
    finals = sum_chips(late, halves, from_chips)
    grads.update(zip(late, exchange_call("swap_final_late", swap_final(finals))))

    shapes = {n: given[n].shape for n in SMALL_W}
    shapes.update(loss=(), conv_w=(3, 2 * D_FF))
    small_g = {"g_mix": dg_mix, "g_cq": dg_cq, "g_ckv": dg_ckv, "g_sb_out": dg_sb_out, "g_mla_out": dg_mla_out,
               "g_ffn": dg_ffn, "conv_b": jnp.concatenate([dcb_g, dcb_v], axis=1), "g_final": dg_final,
               "loss": loss_row[0, :1], "conv_w": jnp.concatenate([dcw_g, dcw_v], axis=1)}
    gs_slab = allsum_small_call(pack_small(small_g))
    small_in = [pack_small({n: given[pre + n] for n in SMALL_W}) for pre in ("", "m_", "v_")]
    small_out = [unpack_small(s, shapes) for s in (gs_slab,) + tuple(adamw_call("adamw_small", gs_slab, *small_in))]
    cw_cols = BIG_SHARD["conv_w"][1]
    grads["conv_w"] = lax.dynamic_slice_in_dim(small_out[0]["conv_w"], chip[0] * cw_cols, cw_cols, axis=1)

    big_out = {n: (grads[n],) + tuple(adamw_call("adamw_" + n, grads[n], w_big[n], m_big[n], v_big[n])) for n in BIG_W}
    weights = ("g_mix", "w_in", "g_cq", "w_uq", "g_ckv", "w_ukv", "g_sb_out", "g_mla_out", "w_out", "g_ffn",
               "w_up", "conv_w", "conv_b", "w_down", "g_final")
    outs = [small_out[0]["loss"], grad_x.reshape(B, S, D_MODEL)]
    for k in range(4):
        for n in weights:
            outs.append(big_out[n][k][None] if n in BIG_W else small_out[k][n])
    return tuple(outs)
```

```python
import functools

import jax
import jax.numpy as jnp
from jax import lax
from jax.experimental import pallas as pl
from jax.experimental.pallas import tpu as pltpu

F32 = jnp.float32
BF16 = jnp.bfloat16
MESH = pl.DeviceIdType.MESH

D_MODEL = 1024
HEADS = 8
HEAD_DIM = 64
ATT_W = HEADS * HEAD_DIM
ROPE_DIM = 32
ROPE_W = HEADS * ROPE_DIM
QK_DIM = HEAD_DIM + ROPE_DIM
Q_RANK = 384
KV_RANK = 256
D_FF = 2816
IN_COLS = 2208
IN_COLS_PAD = 2304
EPS = 1e-6
ROPE_BASE = 10000.0
SB_SCALE = HEAD_DIM ** -0.5
SB_SCALE2 = SB_SCALE * 1.4426950408889634
MLA_SCALE = QK_DIM ** -0.5
LOG2E = 1.4426950408889634
LN2 = 0.6931471805599453
MLA_SCALE2 = MLA_SCALE * LOG2E
LANES = 128
N_CHIPS = 4
N_DEV = 8
VMEM_LIMIT = 48 * 1024 * 1024
ATT_TQ = 256
ATT_TK = 256
ATT_PAIRS = 4
PAIR_LANES = [slice(i * LANES, (i + 1) * LANES) for i in range(ATT_PAIRS)]
SB_BWD_GROUP = 2
SB_FWD_GROUP = 4
MLA_GROUP = 4
NEG_BIG = -1e30

ADAM_LR = 0.001
ADAM_B1 = 0.9
ADAM_B2 = 0.999
ADAM_EPS = 1e-08
ADAM_WD = 0.01
ADAM_STEP = 10

BIG_W = ("w_in", "w_uq", "w_ukv", "w_out", "w_up", "conv_w", "w_down")
BIG_SHARD = {
    "w_in": (D_MODEL, IN_COLS // 4, True),
    "w_uq": (Q_RANK, HEADS * QK_DIM // 4, True),
    "w_ukv": (KV_RANK, 2 * ATT_W // 4, True),
    "w_out": (2 * ATT_W // 4, D_MODEL, False),
    "w_up": (D_MODEL, 2 * D_FF // 4, True),
    "conv_w": (3, 2 * D_FF // 4, True),
    "w_down": (D_FF // 4, D_MODEL, False),
}
SMALL_W = ("g_mix", "g_cq", "g_ckv", "g_sb_out", "g_mla_out", "g_ffn", "conv_b", "g_final")
SMALL_N = {"g_mix": D_MODEL, "g_cq": Q_RANK, "g_ckv": KV_RANK, "g_sb_out": ATT_W, "g_mla_out": ATT_W,
           "g_ffn": D_MODEL, "conv_b": 2 * D_FF, "g_final": D_MODEL}


def _params(sem=None, **kw):
    return pltpu.CompilerParams(dimension_semantics=sem, vmem_limit_bytes=VMEM_LIMIT, **kw)


def _dot(a, b, dims):
    return lax.dot_general(a, b, (dims, ((), ())), preferred_element_type=F32)


def _nn(a, b):
    return _dot(a, b, ((1,), (0,)))


def _nt(a, b):
    return _dot(a, b, ((1,), (1,)))


def _tn(a, b):
    return _dot(a, b, ((0,), (0,)))


def _split2(x):
    hi = x.astype(BF16)
    lo = (x - hi.astype(F32)).astype(BF16)
    return hi, lo


def _split3(x):
    hi = x.astype(BF16)
    r1 = x - hi.astype(F32)
    mid = r1.astype(BF16)
    return hi, mid, (r1 - mid.astype(F32)).astype(BF16)


def _rms_r(x, d):
    return lax.rsqrt(jnp.sum(x * x, axis=-1, keepdims=True) * (1.0 / d) + EPS)


def _rms_bwd(x, g, dy, d):
    r = _rms_r(x, d)
    xhat = x * r
    gy = dy * g
    dx = r * (gy - xhat * (jnp.sum(xhat * gy, axis=-1, keepdims=True) * (1.0 / d)))
    return dx, jnp.sum(dy * xhat, axis=0, keepdims=True)


def _rot(x):
    lane = lax.broadcasted_iota(jnp.int32, x.shape, x.ndim - 1)
    n = x.shape[-1]
    return jnp.where((lane & 31) < 16, pltpu.roll(x, n - 16, x.ndim - 1), pltpu.roll(x, 16, x.ndim - 1))


def _fold4(x):
    return x + pltpu.roll(x, 32, 1) + pltpu.roll(x, 64, 1) + pltpu.roll(x, 96, 1)


def matmul_call(name, a, b, mode, out_dtype=F32, res=None, tm=512, tn=None, ex=None):
    M, K = a.shape
    N = b.shape[1] if mode == "nn" else b.shape[0]
    tn = N if tn is None else tn
    assert M % tm == 0 and N % tn == 0

    def body(*refs):
        if res is None:
            a_ref, b_ref, o_ref = refs
        else:
            a_ref, b_ref, r_ref, o_ref = refs
        av = a_ref[...].astype(BF16)
        bv = b_ref[...].astype(BF16)
        acc = _nn(av, bv) if mode == "nn" else _nt(av, bv)
        if res is not None:
            acc = r_ref[...] + acc
        o_ref[...] = acc.astype(out_dtype)

    in_specs = [pl.BlockSpec((tm, K), lambda j, i: (i, 0))]
    if mode == "nn":
        in_specs.append(pl.BlockSpec((K, tn), lambda j, i: (0, j)))
    else:
        in_specs.append(pl.BlockSpec((tn, K), lambda j, i: (j, 0)))
    args = [a, b]
    if res is not None:
        in_specs.append(pl.BlockSpec((tm, tn), lambda j, i: (i, j)))
        args.append(res)
    outs, moved = _call(body, ex, name=name, grid=(N // tn, M // tm), in_specs=in_specs,
                        out_specs=[pl.BlockSpec((tm, tn), lambda j, i: (i, j))],
                        out_shape=[jax.ShapeDtypeStruct((M, N), out_dtype)], args=args)
    return outs[0] if ex is None else (outs[0], moved)


def wgrad_call(name, a, b, tn=None, tt=512, by_chip=False):
    T, M = a.shape
    N = b.shape[1]
    tn = N if tn is None else tn
    assert T % tt == 0 and N % tn == 0
    if by_chip:
        out_spec = pl.BlockSpec((None, M, tn), lambda j, t: (j, 0, 0))
        out_shape = jax.ShapeDtypeStruct((N // tn, M, tn), F32)
    else:
        out_spec = pl.BlockSpec((M, tn), lambda j, t: (0, j))
        out_shape = jax.ShapeDtypeStruct((M, N), F32)

    def body(a_ref, b_ref, o_ref):
        @pl.when(pl.program_id(1) == 0)
        def _():
            o_ref[...] = jnp.zeros_like(o_ref)

        o_ref[...] += _tn(a_ref[...].astype(BF16), b_ref[...].astype(BF16))

    return pl.pallas_call(
        body, name=name, grid=(N // tn, T // tt),
        in_specs=[pl.BlockSpec((tt, M), lambda j, t: (t, 0)), pl.BlockSpec((tt, tn), lambda j, t: (t, j))],
        out_specs=out_spec, out_shape=out_shape,
        compiler_params=_params(("parallel", "arbitrary")),
    )(a, b)


UP_COLS = 2 * D_FF // N_CHIPS


def ffn_up_call(hn, w4, tm=512):
    T, K = hn.shape

    def body(a_ref, wg_ref, wv_ref, ug_ref, uv_ref):
        a = a_ref[...]
        ug_ref[...] = _nn(a, wg_ref[...])
        uv_ref[...] = _nn(a, wv_ref[...])

    out = pl.BlockSpec((tm, UP_COLS), lambda j, i: (i, j))
    return pl.pallas_call(
        body, name="ffn_up", grid=(2, T // tm),
        in_specs=[pl.BlockSpec((tm, K), lambda j, i: (i, 0)),
                  pl.BlockSpec((None, K, UP_COLS), lambda j, i: (j, 0, 0)),
                  pl.BlockSpec((None, K, UP_COLS), lambda j, i: (2 + j, 0, 0))],
        out_specs=[out, out], out_shape=[jax.ShapeDtypeStruct((T, D_FF), F32)] * 2,
        compiler_params=_params(("parallel", "parallel")),
    )(hn, w4, w4)


def ffn_up_bwd_call(du_g, du_v, w4, ex, tm=512, tn=512):
    T = du_g.shape[0]
    N = w4.shape[1]

    def body(g_ref, v_ref, w_ref, o_ref):
        acc = _nt(g_ref[:, :UP_COLS], w_ref[0]) + _nt(g_ref[:, UP_COLS:], w_ref[1])
        o_ref[...] = acc + _nt(v_ref[:, :UP_COLS], w_ref[2]) + _nt(v_ref[:, UP_COLS:], w_ref[3])

    row = pl.BlockSpec((tm, D_FF), lambda j, i: (i, 0))
    outs, moved = _call(body, ex, name="ffn_up_bwd", grid=(N // tn, T // tm),
                        in_specs=[row, row, pl.BlockSpec((N_CHIPS, tn, UP_COLS), lambda j, i: (0, j, 0))],
                        out_specs=[pl.BlockSpec((tm, tn), lambda j, i: (i, j))],
                        out_shape=[jax.ShapeDtypeStruct((T, N), F32)], args=(du_g, du_v, w4))
    return outs[0], moved


def wgrad_up_call(hn, du_g, du_v, tt=512):
    T, M = hn.shape

    def body(a_ref, g_ref, v_ref, o_ref):
        @pl.when(pl.program_id(1) == 0)
        def _():
            o_ref[...] = jnp.zeros_like(o_ref)

        a = a_ref[...]
        o_ref[0] += _tn(a, g_ref[...])
        o_ref[1] += _tn(a, v_ref[...])

    col = pl.BlockSpec((tt, UP_COLS), lambda j, t: (t, j))
    out = pl.pallas_call(
        body, name="wgrad_up", grid=(2, T // tt),
        in_specs=[pl.BlockSpec((tt, M), lambda j, t: (t, 0)), col, col],
        out_specs=pl.BlockSpec((2, None, M, UP_COLS), lambda j, t: (0, j, 0, 0)),
        out_shape=jax.ShapeDtypeStruct((2, 2, M, UP_COLS), F32),
        compiler_params=_params(("parallel", "arbitrary")),
    )(hn, du_g, du_v)
    return out.reshape(N_CHIPS, M, UP_COLS)


IN_PIECES = ((0, ATT_W), (ATT_W, ATT_W), (2 * ATT_W, ATT_W), (3 * ATT_W, Q_RANK), (3 * ATT_W + Q_RANK, Q_RANK))


def _piece_specs(rows, index):
    return [pl.BlockSpec((rows, w), functools.partial(index, off // w)) for off, w in IN_PIECES]


def proj_in_bwd_call(pieces, w_in_p, ex, tm=512):
    T = pieces[0].shape[0]
    N = w_in_p.shape[0]
    n = len(pieces)

    def body(*refs):
        o_ref = refs[2 * n]
        acc = _nt(refs[0][...].astype(BF16), refs[n][...])
        for i in range(1, n):
            acc = acc + _nt(refs[i][...].astype(BF16), refs[n + i][...])
        o_ref[...] = acc

    outs, moved = _call(body, ex, name="proj_in_bwd", grid=(T // tm,),
                        in_specs=_piece_specs(tm, lambda c, i: (i, 0)) + _piece_specs(N, lambda c, i: (0, c)),
                        out_specs=[pl.BlockSpec((tm, N), lambda i: (i, 0))],
                        out_shape=[jax.ShapeDtypeStruct((T, N), F32)], args=tuple(pieces) + (w_in_p,) * n)
    return outs[0], moved


def wgrad_in_call(h, pieces, tt=512):
    T, M = h.shape
    n = len(pieces)

    def body(*refs):
        a_ref, o_ref = refs[0], refs[n + 1]

        @pl.when(pl.program_id(0) == 0)
        def _():
            o_ref[...] = jnp.zeros_like(o_ref)

        a = a_ref[...]
        for i, (off, w) in enumerate(IN_PIECES):
            o_ref[:, off:off + w] += _tn(a, refs[1 + i][...].astype(BF16))

    return pl.pallas_call(
        body, name="wgrad_in", grid=(T // tt,),
        in_specs=[pl.BlockSpec((tt, M), lambda t: (t, 0))] + [pl.BlockSpec((tt, w), lambda t: (t, 0)) for _, w in IN_PIECES],
        out_specs=pl.BlockSpec((M, IN_COLS_PAD), lambda t: (0, 0)),
        out_shape=jax.ShapeDtypeStruct((M, IN_COLS_PAD), F32),
        compiler_params=_params(("arbitrary",)),
    )(h, *pieces)


def rmsnorm_fwd_call(name, x, g, tm=512, ex=None):
    T, d = x.shape

    def body(x_ref, g_ref, o_ref):
        x = x_ref[...]
        o_ref[...] = ((x * _rms_r(x, d)) * g_ref[...]).astype(BF16)

    row = pl.BlockSpec((tm, d), lambda i: (i, 0))
    outs, moved = _call(body, ex, name=name, grid=(T // tm,), in_specs=[row, pl.BlockSpec((1, d), lambda i: (0, 0))],
                        out_specs=[row], out_shape=[jax.ShapeDtypeStruct((T, d), BF16)], args=(x, g))
    return outs[0] if ex is None else (outs[0], moved)


def rmsnorm_bwd_call(name, x, g, dy, res, tm=512, ex=None):
    T, d = x.shape

    def body(x_ref, g_ref, dy_ref, r_ref, dx_ref, dg_ref):
        @pl.when(pl.program_id(0) == 0)
        def _():
            dg_ref[...] = jnp.zeros_like(dg_ref)

        dx, dg = _rms_bwd(x_ref[...], g_ref[...], dy_ref[...], d)
        dx_ref[...] = r_ref[...] + dx
        dg_ref[...] += dg

    row = pl.BlockSpec((tm, d), lambda i: (i, 0))
    vec = pl.BlockSpec((1, d), lambda i: (0, 0))
    outs, moved = _call(body, ex, name=name, grid=(T // tm,), in_specs=[row, vec, row, row], out_specs=[row, vec],
                        out_shape=[jax.ShapeDtypeStruct((T, d), F32), jax.ShapeDtypeStruct((1, d), F32)],
                        args=(x, g, dy, res))
    return tuple(outs) if ex is None else tuple(outs) + (moved,)


def outnorm_fwd_call(o_sb, o_mla, g_sb, g_mla, tm=512):
    T = o_sb.shape[0]

    def body(a_ref, b_ref, ga_ref, gb_ref, o_ref):
        a = a_ref[...]
        b = b_ref[...]
        ya = (a * _rms_r(a, ATT_W)) * ga_ref[...]
        yb = (b * _rms_r(b, ATT_W)) * gb_ref[...]
        o_ref[...] = jnp.concatenate([ya, yb], axis=1).astype(BF16)

    row = pl.BlockSpec((tm, ATT_W), lambda i: (i, 0))
    vec = pl.BlockSpec((1, ATT_W), lambda i: (0, 0))
    return pl.pallas_call(
        body, name="outnorm_fwd", grid=(T // tm,), in_specs=[row, row, vec, vec],
        out_specs=pl.BlockSpec((tm, 2 * ATT_W), lambda i: (i, 0)),
        out_shape=jax.ShapeDtypeStruct((T, 2 * ATT_W), BF16),
        compiler_params=_params(("parallel",)),
    )(o_sb, o_mla, g_sb, g_mla)


def outnorm_bwd_call(o_sb, o_mla, g_sb, g_mla, do_cat, tm=512, ex=None):
    T = o_sb.shape[0]

    def body(a_ref, b_ref, ga_ref, gb_ref, d_ref, da_ref, db_ref, dga_ref, dgb_ref):
        @pl.when(pl.program_id(0) == 0)
        def _():
            dga_ref[...] = jnp.zeros_like(dga_ref)
            dgb_ref[...] = jnp.zeros_like(dgb_ref)

        d = d_ref[...]
        da, dga = _rms_bwd(a_ref[...], ga_ref[...], d[:, :ATT_W], ATT_W)
        db, dgb = _rms_bwd(b_ref[...], gb_ref[...], d[:, ATT_W:], ATT_W)
        da_ref[...] = da
        db_ref[...] = db
        dga_ref[...] += dga
        dgb_ref[...] += dgb

    row = pl.BlockSpec((tm, ATT_W), lambda i: (i, 0))
    vec = pl.BlockSpec((1, ATT_W), lambda i: (0, 0))
    outs, moved = _call(
        body, ex, name="outnorm_bwd", grid=(T // tm,),
        in_specs=[row, row, vec, vec, pl.BlockSpec((tm, 2 * ATT_W), lambda i: (i, 0))],
        out_specs=[row, row, vec, vec],
        out_shape=[jax.ShapeDtypeStruct((T, ATT_W), F32), jax.ShapeDtypeStruct((T, ATT_W), F32),
                   jax.ShapeDtypeStruct((1, ATT_W), F32), jax.ShapeDtypeStruct((1, ATT_W), F32)],
        args=(o_sb, o_mla, g_sb, g_mla, do_cat))
    return tuple(outs) if ex is None else tuple(outs) + (moved,)


def final_loss_call(x2, g, target, tm=512):
    T, d = x2.shape

    def body(x_ref, g_ref, t_ref, dx_ref, dxb_ref, loss_ref, dg_ref):
        @pl.when(pl.program_id(0) == 0)
        def _():
            loss_ref[...] = jnp.zeros_like(loss_ref)
            dg_ref[...] = jnp.zeros_like(dg_ref)

        x = x_ref[...]
        g = g_ref[...]
        y = (x * _rms_r(x, d)) * g
        err = y - t_ref[...]
        loss_ref[...] += jnp.sum(jnp.sum(err * err, axis=1, keepdims=True), axis=0, keepdims=True) * (0.5 / d)
        dx, dg = _rms_bwd(x, g, err * (1.0 / d), d)
        dx_ref[...] = dx
        dxb_ref[...] = dx.astype(BF16)
        dg_ref[...] += dg

    row = pl.BlockSpec((tm, d), lambda i: (i, 0))
    vec = pl.BlockSpec((1, d), lambda i: (0, 0))
    return pl.pallas_call(
        body, name="final_loss", grid=(T // tm,), in_specs=[row, vec, row],
        out_specs=[row, row, pl.BlockSpec((1, LANES), lambda i: (0, 0)), vec],
        out_shape=[jax.ShapeDtypeStruct((T, d), F32), jax.ShapeDtypeStruct((T, d), BF16),
                   jax.ShapeDtypeStruct((1, LANES), F32), jax.ShapeDtypeStruct((1, d), F32)],
        compiler_params=_params(("arbitrary",)),
    )(x2, g, target)


def rope_tab_call(pos, inv_freq, tm=512):
    T = pos.shape[0]

    def body(p_ref, f_ref, c_ref, s_ref):
        ang = p_ref[...].astype(F32) * f_ref[...]
        lane = lax.broadcasted_iota(jnp.int32, ang.shape, 1)
        sn = jnp.sin(ang)
        c_ref[...] = jnp.cos(ang)
        s_ref[...] = jnp.where((lane & 31) < 16, -sn, sn)

    row = pl.BlockSpec((tm, LANES), lambda i: (i, 0))
    return pl.pallas_call(
        body, name="rope_tab", grid=(T // tm,),
        in_specs=[pl.BlockSpec((tm, 1), lambda i: (i, 0)), pl.BlockSpec((1, LANES), lambda i: (0, 0))],
        out_specs=[row, row],
        out_shape=[jax.ShapeDtypeStruct((T, LANES), F32)] * 2,
        compiler_params=_params(("parallel",)),
    )(pos, inv_freq)


def mla_prep_fwd_call(p, cos, sin, g_cq, g_ckv, w_uq_p, w_ukv_p, tm=512):
    T = p.shape[0]

    def body(cq_ref, ckvr_ref, c_ref, s_ref, gq_ref, gkv_ref, wq_ref, wkv_ref,
             qn_ref, qr_ref, kn_ref, vm_ref, krt_ref, cqn_ref, ckvn_ref):
        c = c_ref[...]
        s = s_ref[...]
        cq = cq_ref[...]
        cqn = ((cq * _rms_r(cq, Q_RANK)) * gq_ref[...]).astype(BF16)
        cqn_ref[...] = cqn
        q = _nn(cqn, wq_ref[...])
        qn_ref[...] = q[:, :ATT_W].astype(BF16)
        for g in range(ROPE_W // LANES):
            qr = q[:, ATT_W + g * LANES:ATT_W + (g + 1) * LANES]
            qr_ref[:, g * LANES:(g + 1) * LANES] = (qr * c + _rot(qr) * s).astype(BF16)
        ckvr = ckvr_ref[...]
        ckv = ckvr[:, :KV_RANK]
        ckvn = ((ckv * _rms_r(ckv, KV_RANK)) * gkv_ref[...]).astype(BF16)
        ckvn_ref[...] = ckvn
        kv = _nn(ckvn, wkv_ref[...])
        kn_ref[...] = kv[:, :ATT_W].astype(BF16)
        vm_ref[...] = kv[:, ATT_W:].astype(BF16)
        kr = _fold4(ckvr[:, KV_RANK:])
        krt_ref[...] = (kr * c + _rot(kr) * s).astype(BF16)

    def row(w, j=0):
        return pl.BlockSpec((tm, w), lambda i: (i, j))

    def full(a):
        return pl.BlockSpec(a.shape, lambda i: (0, 0))

    return pl.pallas_call(
        body, name="mla_prep_fwd", grid=(T // tm,),
        in_specs=[row(Q_RANK, 4), row(Q_RANK, 5), row(LANES), row(LANES), full(g_cq), full(g_ckv),
                  full(w_uq_p), full(w_ukv_p)],
        out_specs=[row(ATT_W), row(ROPE_W), row(ATT_W), row(ATT_W), row(LANES), row(Q_RANK), row(KV_RANK)],
        out_shape=[jax.ShapeDtypeStruct((T, w), BF16) for w in (ATT_W, ROPE_W, ATT_W, ATT_W, LANES, Q_RANK, KV_RANK)],
        compiler_params=_params(("parallel",)),
    )(p, p, cos, sin, g_cq, g_ckv, w_uq_p, w_ukv_p)


def mla_prep_bwd_call(p, cos, sin, g_cq, g_ckv, w_uq_p, w_ukv_p, dqn, dqr4, dkn, dvm, dkrt4, tm=512):
    T = p.shape[0]

    def body(cq_ref, ckvr_ref, c_ref, s_ref, gq_ref, gkv_ref, wq_ref, wkv_ref,
             dqn_ref, dqr4_ref, dkn_ref, dvm_ref, dkrt4_ref,
             dcq_ref, dckvr_ref, dq_ref, dkv_ref, dgq_ref, dgkv_ref):
        @pl.when(pl.program_id(0) == 0)
        def _():
            dgq_ref[...] = jnp.zeros_like(dgq_ref)
            dgkv_ref[...] = jnp.zeros_like(dgkv_ref)

        c = c_ref[...]
        s = s_ref[...]
        d4 = dqr4_ref[...]
        dqr = [d4[:, :128] + d4[:, 128:256], d4[:, 256:384] + d4[:, 384:]]
        dqr = [t * c + _rot(t * s) for t in dqr]
        dq = jnp.concatenate([dqn_ref[...]] + dqr, axis=1).astype(BF16)
        dq_ref[...] = dq
        dcq, dgq = _rms_bwd(cq_ref[...], gq_ref[...], _nt(dq, wq_ref[...]), Q_RANK)
        dcq_ref[...] = dcq
        dgq_ref[...] += dgq
        dkv = jnp.concatenate([dkn_ref[...], dvm_ref[...]], axis=1).astype(BF16)
        dkv_ref[...] = dkv
        ckvr = ckvr_ref[...]
        dckv, dgkv = _rms_bwd(ckvr[:, :KV_RANK], gkv_ref[...], _nt(dkv, wkv_ref[...]), KV_RANK)
        dgkv_ref[...] += dgkv
        k4 = dkrt4_ref[...]
        dkr = _fold4(k4[:, :128] + k4[:, 128:256] + k4[:, 256:384] + k4[:, 384:])
        dkr = dkr * c + _rot(dkr * s)
        lane = lax.broadcasted_iota(jnp.int32, dkr.shape, 1)
        dckvr_ref[...] = jnp.concatenate([dckv, jnp.where(lane < ROPE_DIM, dkr, 0.0)], axis=1)

    def row(w, j=0):
        return pl.BlockSpec((tm, w), lambda i: (i, j))

    def full(a):
        return pl.BlockSpec(a.shape, lambda i: (0, 0))

    return pl.pallas_call(
        body, name="mla_prep_bwd", grid=(T // tm,),
        in_specs=[row(Q_RANK, 4), row(Q_RANK, 5), row(LANES), row(LANES), full(g_cq), full(g_ckv),
                  full(w_uq_p), full(w_ukv_p), row(ATT_W), row(ATT_W), row(ATT_W), row(ATT_W), row(ATT_W)],
        out_specs=[row(Q_RANK), row(Q_RANK), row(ATT_W + ROPE_W), row(2 * ATT_W),
                   pl.BlockSpec((1, Q_RANK), lambda i: (0, 0)), pl.BlockSpec((1, KV_RANK), lambda i: (0, 0))],
        out_shape=[jax.ShapeDtypeStruct((T, Q_RANK), F32), jax.ShapeDtypeStruct((T, Q_RANK), F32),
                   jax.ShapeDtypeStruct((T, ATT_W + ROPE_W), BF16), jax.ShapeDtypeStruct((T, 2 * ATT_W), BF16),
                   jax.ShapeDtypeStruct((1, Q_RANK), F32), jax.ShapeDtypeStruct((1, KV_RANK), F32)],
        compiler_params=_params(("arbitrary",)),
    )(p, p, cos, sin, g_cq, g_ckv, w_uq_p, w_ukv_p, dqn, dqr4, dkn, dvm, dkrt4)


def _iota2(shape, axis):
    return lax.broadcasted_iota(jnp.int32, shape, axis)


def _head_masks():
    lane = _iota2((1, LANES), 1)
    return lane < HEAD_DIM, lane >= HEAD_DIM


def _pair(x, masks, dtype=BF16):
    return [jnp.where(m, x, 0.0).astype(dtype) for m in masks]


def _log_gates(z):
    keep = jnp.maximum(z, 0.0) + jnp.log2(1.0 + jnp.exp2(-jnp.abs(z)))
    return z - keep, keep


def _last_row(x):
    return _row_of(x[x.shape[0] - 8:, :], 7)


def _lane_selector(group):
    return jnp.where(_iota2((16, LANES), 1) // group == _iota2((16, LANES), 0), 1.0, 0.0).astype(BF16)


def _rows8(sel_t, x):
    hi = x.astype(BF16)
    r1 = x - hi.astype(F32)
    mid = r1.astype(BF16)
    lo = (r1 - mid.astype(F32)).astype(BF16)
    return _nt(sel_t, hi) + _nt(sel_t, mid) + _nt(sel_t, lo)


def _row_of(x8, j):
    return jnp.sum(jnp.where(_iota2(x8.shape, 0) == j, x8, 0.0), axis=0, keepdims=True)


def sb_fwd_call(p, B, S, ex=None):
    T = B * S
    TQ, TK = ATT_TQ, ATT_TK
    nq = S // TQ

    def body(q_ref, k_ref, v_ref, o_ref, lt_ref):
        qi = pl.program_id(2)
        masks = _head_masks()
        qm = [_pair(q_ref[:, sl] * SB_SCALE2, masks) for sl in PAIR_LANES]
        row = _iota2((TQ, TK), 0)
        col = _iota2((TQ, TK), 1)
        tri = jnp.where(row > col, 1.0, 0.0).astype(BF16)
        tri2 = jnp.concatenate([tri, tri], axis=0)
        vis = col < row
        o_ref[...] = jnp.zeros_like(o_ref)

        def group(k0, pairs, carry, diag):
            heads = [(pp, j) for pp in pairs for j in range(2)]
            n = range(len(heads))
            k = {pp: k_ref[pl.ds(k0, TK), PAIR_LANES[pp]].astype(BF16) for pp in pairs}
            vm = {pp: _pair(v_ref[pl.ds(k0, TK), PAIR_LANES[pp]], masks) for pp in pairs}
            gates = [_log_gates(_nt(qm[pp][j], k[pp])) for pp, j in heads]
            lb = [g[0] for g in gates]
            keep = [jnp.where(vis, g[1], 0.0) if diag else g[1] for g in gates]
            tail = [_nn(jnp.concatenate(_split2(keep[h]), axis=1), tri2) + carry[h] for h in n]
            a = [jnp.exp2(lb[h] - tail[h]) for h in n]
            if diag:
                a = [jnp.where(vis, x, 0.0) for x in a]
            ab = [x.astype(BF16) for x in a]
            for i, pp in enumerate(pairs):
                o_ref[:, PAIR_LANES[pp]] += _nn(ab[2 * i], vm[pp][0]) + _nn(ab[2 * i + 1], vm[pp][1])
            return [tail[h][:, 0:1] + keep[h][:, 0:1] for h in n]

        def step(kb, carry, diag):
            k0 = pl.multiple_of(kb * TK, TK)
            out = []
            for g in range(0, ATT_PAIRS, SB_FWD_GROUP):
                out += group(k0, list(range(g, g + SB_FWD_GROUP)), carry[2 * g:2 * (g + SB_FWD_GROUP)], diag)
            return tuple(out)

        zero = jnp.zeros((TQ, 1), F32)
        carry = step(qi, (zero,) * (2 * ATT_PAIRS), True)
        carry = lax.fori_loop(0, qi, lambda i, c: step(qi - 1 - i, c, False), carry)
        lane = _iota2((TQ, LANES), 1)
        for pp, sl in enumerate(PAIR_LANES):
            lt_ref[:, sl] = jnp.where(lane == 0, carry[2 * pp], jnp.where(lane == 1, carry[2 * pp + 1], 0.0))

    W = ATT_PAIRS * LANES
    qspec = pl.BlockSpec((TQ, W), lambda b, h, i: (b * nq + i, h))
    outs, moved = _call(
        body, ex, name="sb_fwd", grid=(B, HEADS // 2 // ATT_PAIRS, nq),
        in_specs=[qspec,
                  pl.BlockSpec((S, W), lambda b, h, i: (b, ATT_W // W + h)),
                  pl.BlockSpec((S, W), lambda b, h, i: (b, 2 * ATT_W // W + h))],
        out_specs=[qspec, qspec],
        out_shape=[jax.ShapeDtypeStruct((T, ATT_W), F32)] * 2, args=(p, p, p))
    return tuple(outs) if ex is None else tuple(outs) + (moved,)


def sb_bwd_call(p, lt, do, B, S, ex=None):
    T = B * S
    TQ, TK = ATT_TQ, ATT_TK
    nq = S // TQ

    def body(q_ref, k_ref, v_ref, lt_ref, do_ref, dq_ref, dk_ref, dv_ref):
        qi = pl.program_id(2)

        @pl.when(qi == 0)
        def _():
            dk_ref[...] = jnp.zeros_like(dk_ref)
            dv_ref[...] = jnp.zeros_like(dv_ref)

        masks = _head_masks()
        qm = [_pair(q_ref[:, sl] * SB_SCALE2, masks) for sl in PAIR_LANES]
        dom = [_pair(do_ref[:, sl], masks) for sl in PAIR_LANES]
        start = []
        for sl in PAIR_LANES:
            l8 = _rows8(_lane_selector(1), lt_ref[:, sl])
            start += [-_row_of(l8, 0), jnp.zeros((1, TQ), F32), -_row_of(l8, 1), jnp.zeros((1, TQ), F32)]
        row = _iota2((TK, TQ), 0)
        col = _iota2((TK, TQ), 1)
        incl = jnp.where(col <= row, 1.0, 0.0).astype(BF16)
        incl2 = jnp.concatenate([incl, incl], axis=1)
        excl = jnp.where(col < row, 1.0, 0.0).astype(BF16)
        vis = row < col
        dq_ref[...] = jnp.zeros_like(dq_ref)

        def group(k0, pairs, carry, diag):
            heads = [(pp, j) for pp in pairs for j in range(2)]
            n = range(len(heads))
            kf = {pp: k_ref[pl.ds(k0, TK), PAIR_LANES[pp]] for pp in pairs}
            km = {pp: _pair(kf[pp], masks) for pp in pairs}
            v = {pp: v_ref[pl.ds(k0, TK), PAIR_LANES[pp]].astype(BF16) for pp in pairs}
            z = [_nt(kf[pp].astype(BF16), qm[pp][j]) for pp, j in heads]
            da = [_nt(v[pp], dom[pp][j]) for pp, j in heads]
            gates = [_log_gates(x) for x in z]
            lb = [g[0] for g in gates]
            keep = [jnp.where(vis, g[1], 0.0) if diag else g[1] for g in gates]
            left = [_nn(incl2, jnp.concatenate(_split2(keep[h]), axis=0)) + carry[2 * h] for h in n]
            a = [jnp.exp2(lb[h] + left[h]) for h in n]
            if diag:
                a = [jnp.where(vis, x, 0.0) for x in a]
            e = [a[h] * da[h] for h in n]
            before = [_nn(excl, e[h].astype(BF16)) + carry[2 * h + 1] for h in n]
            dz = [e[h] - jnp.exp2(lb[h]) * (e[h] + before[h]) for h in n]
            if diag:
                dz = [jnp.where(vis, x, 0.0) for x in dz]
            dzb = [x.astype(BF16) for x in dz]
            ab = [x.astype(BF16) for x in a]
            out = []
            for h in n:
                out += [_last_row(left[h]), _last_row(before[h]) + _last_row(e[h])]
            for i, pp in enumerate(pairs):
                sl = PAIR_LANES[pp]
                dk_ref[pl.ds(k0, TK), sl] += _nn(dzb[2 * i], qm[pp][0]) + _nn(dzb[2 * i + 1], qm[pp][1])
                dv_ref[pl.ds(k0, TK), sl] += _nn(ab[2 * i], dom[pp][0]) + _nn(ab[2 * i + 1], dom[pp][1])
                dq_ref[:, sl] += _tn(dzb[2 * i], km[pp][0]) + _tn(dzb[2 * i + 1], km[pp][1])
            return out

        def step(kb, carry, diag):
            k0 = pl.multiple_of(kb * TK, TK)
            out = []
            for g in range(0, ATT_PAIRS, SB_BWD_GROUP):
                out += group(k0, list(range(g, g + SB_BWD_GROUP)), carry[4 * g:4 * (g + SB_BWD_GROUP)], diag)
            return tuple(out)

        carry = lax.fori_loop(0, qi, lambda i, c: step(i, c, False), tuple(start))
        step(qi, carry, True)
        dq_ref[...] *= SB_SCALE

        @pl.when(qi == nq - 1)
        def _():
            dk_ref[...] *= LN2

    W = ATT_PAIRS * LANES
    qspec = pl.BlockSpec((TQ, W), lambda b, h, i: (b * nq + i, h))
    sspec = pl.BlockSpec((S, W), lambda b, h, i: (b, h))
    outs, moved = _call(
        body, ex, name="sb_bwd", grid=(B, HEADS // 2 // ATT_PAIRS, nq),
        in_specs=[qspec,
                  pl.BlockSpec((S, W), lambda b, h, i: (b, ATT_W // W + h)),
                  pl.BlockSpec((S, W), lambda b, h, i: (b, 2 * ATT_W // W + h)),
                  qspec, qspec],
        out_specs=[qspec, sspec, sspec],
        out_shape=[jax.ShapeDtypeStruct((T, ATT_W), F32)] * 3, args=(p, p, p, lt, do))
    return tuple(outs) if ex is None else tuple(outs) + (moved,)


ALL_PAIRS = [slice(i * LANES, (i + 1) * LANES) for i in range(HEADS // 2)]


def _rope_masks(hp):
    grp = _iota2((1, LANES), 1) // ROPE_DIM
    return [grp == ((2 * hp + j) % 4) for j in range(2)]


def _mla_queries(qn_ref, qr_ref, masks):
    out = []
    for pp, sl in enumerate(ALL_PAIRS):
        qnv = qn_ref[:, sl]
        qrv = qr_ref[:, ALL_PAIRS[pp // 2]]
        rmasks = _rope_masks(pp)
        out.append([jnp.concatenate([jnp.where(masks[j], qnv, 0), jnp.where(rmasks[j], qrv, 0)], axis=1).astype(BF16)
                    for j in range(2)])
    return out


def mla_fwd_call(qn, qr, kn, krt, vm, B, S):
    T = B * S
    TQ, TK = ATT_TQ, ATT_TK
    nq = S // TQ

    def body(qn_ref, qr_ref, kn_ref, kr_ref, v_ref, o_ref, lse_ref):
        qi = pl.program_id(1)
        masks = _head_masks()
        qcat = _mla_queries(qn_ref, qr_ref, masks)
        row = _iota2((TQ, TK), 0)
        col = _iota2((TQ, TK), 1)
        vis = col <= row
        o_ref[...] = jnp.zeros_like(o_ref)

        def group(k0, pairs, carry, diag):
            heads = [(pp, j) for pp in pairs for j in range(2)]
            n = range(len(heads))
            krv = kr_ref[pl.ds(k0, TK), :]
            kcat = {pp: jnp.concatenate([kn_ref[pl.ds(k0, TK), ALL_PAIRS[pp]], krv], axis=1) for pp in pairs}
            vmk = {pp: _pair(v_ref[pl.ds(k0, TK), ALL_PAIRS[pp]], masks) for pp in pairs}
            s = [_nt(qcat[pp][j], kcat[pp]) * MLA_SCALE2 for pp, j in heads]
            if diag:
                s = [jnp.where(vis, x, NEG_BIG) for x in s]
            m_new = [jnp.maximum(carry[2 * h], jnp.max(s[h], axis=1, keepdims=True)) for h in n]
            alpha = [jnp.exp2(carry[2 * h] - m_new[h]) for h in n]
            pexp = [jnp.exp2(s[h] - m_new[h]) for h in n]
            out = []
            for h in n:
                out += [m_new[h], alpha[h] * carry[2 * h + 1] + jnp.sum(pexp[h], axis=1, keepdims=True)]
            pb = [x.astype(BF16) for x in pexp]
            for i, pp in enumerate(pairs):
                sl = ALL_PAIRS[pp]
                scale = jnp.where(masks[0], alpha[2 * i], alpha[2 * i + 1])
                o_ref[:, sl] = o_ref[:, sl] * scale + (_nn(pb[2 * i], vmk[pp][0]) + _nn(pb[2 * i + 1], vmk[pp][1]))
            return out

        def step(kb, carry, diag):
            k0 = pl.multiple_of(kb * TK, TK)
            out = []
            for g in range(0, len(ALL_PAIRS), MLA_GROUP):
                out += group(k0, list(range(g, g + MLA_GROUP)), carry[4 * g:4 * (g + MLA_GROUP)], diag)
            return tuple(out)

        neg = jnp.full((TQ, 1), NEG_BIG, F32)
        zero = jnp.zeros((TQ, 1), F32)
        carry = step(qi, (neg, zero) * (2 * len(ALL_PAIRS)), True)
        carry = lax.fori_loop(0, qi, lambda i, c: step(qi - 1 - i, c, False), carry)
        lane = _iota2((TQ, LANES), 1)
        for pp, sl in enumerate(ALL_PAIRS):
            m0, l0, m1, l1 = carry[4 * pp:4 * pp + 4]
            o_ref[:, sl] = o_ref[:, sl] * jnp.where(masks[0], 1.0 / l0, 1.0 / l1)
            lse_ref[:, sl] = jnp.where(lane == 0, m0 * LN2 + jnp.log(l0), jnp.where(lane == 1, m1 * LN2 + jnp.log(l1), 0.0))

    def rows(w):
        return pl.BlockSpec((TQ, w), lambda b, i: (b * nq + i, 0))

    def seq(w):
        return pl.BlockSpec((S, w), lambda b, i: (b, 0))

    return pl.pallas_call(
        body, name="mla_fwd", grid=(B, nq),
        in_specs=[rows(ATT_W), rows(ROPE_W), seq(ATT_W), seq(LANES), seq(ATT_W)],
        out_specs=[rows(ATT_W), rows(ATT_W)],
        out_shape=[jax.ShapeDtypeStruct((T, ATT_W), F32)] * 2,
        compiler_params=_params(("arbitrary", "arbitrary")),
    )(qn, qr, kn, krt, vm)


def mla_bwd_call(qn, qr, kn, krt, vm, o, lse, do, B, S, ex=None):
    T = B * S
    TQ, TK = ATT_TQ, ATT_TK
    nq = S // TQ

    def body(qn_ref, qr_ref, kn_ref, kr_ref, v_ref, o_ref, lse_ref, do_ref,
             dqn_ref, dqr_ref, dkn_ref, dv_ref, dkr_ref):
        qi = pl.program_id(1)

        @pl.when(qi == 0)
        def _():
            dkn_ref[...] = jnp.zeros_like(dkn_ref)
            dv_ref[...] = jnp.zeros_like(dv_ref)
            dkr_ref[...] = jnp.zeros_like(dkr_ref)

        masks = _head_masks()
        qcat = _mla_queries(qn_ref, qr_ref, masks)
        dom, dsum, lse = [], [], []
        for sl in ALL_PAIRS:
            do = do_ref[:, sl]
            dom.append(_pair(do, masks))
            d8 = _rows8(_lane_selector(HEAD_DIM), do * o_ref[:, sl])
            l8 = _rows8(_lane_selector(1), lse_ref[:, sl])
            dsum.append([_row_of(d8, j) for j in range(2)])
            lse.append([_row_of(l8, j) * LOG2E for j in range(2)])
        row = _iota2((TK, TQ), 0)
        col = _iota2((TK, TQ), 1)
        vis = row <= col
        dqn_ref[...] = jnp.zeros_like(dqn_ref)
        dqr_ref[...] = jnp.zeros_like(dqr_ref)

        def group(k0, pairs, diag):
            heads = [(pp, j) for pp in pairs for j in range(2)]
            n = range(len(heads))
            krv = kr_ref[pl.ds(k0, TK), :]
            knv = {pp: kn_ref[pl.ds(k0, TK), ALL_PAIRS[pp]] for pp in pairs}
            kcat = {pp: jnp.concatenate([knv[pp], krv], axis=1) for pp in pairs}
            v = {pp: v_ref[pl.ds(k0, TK), ALL_PAIRS[pp]] for pp in pairs}
            s = [_nt(kcat[pp], qcat[pp][j]) * MLA_SCALE2 for pp, j in heads]
            dp_ = [_nt(v[pp], dom[pp][j]) for pp, j in heads]
            pr = [jnp.exp2(s[h] - lse[pp][j]) for h, (pp, j) in enumerate(heads)]
            if diag:
                pr = [jnp.where(vis, x, 0.0) for x in pr]
            ds = [(pr[h] * (dp_[h] - dsum[pp][j]) * MLA_SCALE).astype(BF16) for h, (pp, j) in enumerate(heads)]
            pb = [x.astype(BF16) for x in pr]
            for i, pp in enumerate(pairs):
                sl = ALL_PAIRS[pp]
                rmasks = _rope_masks(pp)
                kcat_j = [jnp.concatenate([jnp.where(masks[j], knv[pp], 0), jnp.where(rmasks[j], krv, 0)],
                                          axis=1).astype(BF16) for j in range(2)]
                dv_ref[pl.ds(k0, TK), sl] += _nn(pb[2 * i], dom[pp][0]) + _nn(pb[2 * i + 1], dom[pp][1])
                dk = _nn(ds[2 * i], qcat[pp][0]) + _nn(ds[2 * i + 1], qcat[pp][1])
                dq = _tn(ds[2 * i], kcat_j[0]) + _tn(ds[2 * i + 1], kcat_j[1])
                dqn_ref[:, sl] += dq[:, :LANES]
                dqr_ref[:, sl] += dq[:, LANES:]
                dkn_ref[pl.ds(k0, TK), sl] += dk[:, :LANES]
                dkr_ref[pl.ds(k0, TK), sl] += dk[:, LANES:]

        def step(kb, diag):
            k0 = pl.multiple_of(kb * TK, TK)
            for g in range(0, len(ALL_PAIRS), MLA_GROUP):
                group(k0, list(range(g, g + MLA_GROUP)), diag)

        step(qi, True)

        def loop(i, c):
            step(qi - 1 - i, False)
            return c

        lax.fori_loop(0, qi, loop, 0)

    def rows(w):
        return pl.BlockSpec((TQ, w), lambda b, i: (b * nq + i, 0))

    def seq(w):
        return pl.BlockSpec((S, w), lambda b, i: (b, 0))

    outs, moved = _call(
        body, ex, name="mla_bwd", grid=(B, nq),
        in_specs=[rows(ATT_W), rows(ROPE_W), seq(ATT_W), seq(LANES), seq(ATT_W), rows(ATT_W), rows(ATT_W), rows(ATT_W)],
        out_specs=[rows(ATT_W), rows(ATT_W), seq(ATT_W), seq(ATT_W), seq(ATT_W)],
        out_shape=[jax.ShapeDtypeStruct((T, ATT_W), F32)] * 5, args=(qn, qr, kn, krt, vm, o, lse, do))
    return tuple(outs) if ex is None else tuple(outs) + (moved,)


CONV_TC = 256


def _shift_down(x, n):
    return jnp.where(_iota2(x.shape, 0) >= n, pltpu.roll(x, n, 0), 0.0)


def _shift_up(x, n):
    rows = x.shape[0]
    return jnp.where(_iota2(x.shape, 0) < rows - n, pltpu.roll(x, rows - n, 0), 0.0)


def _taps(w_ref):
    return [w_ref[k:k + 1, :] for k in range(3)]


def _conv3(u, w, b):
    return w[0] * _shift_down(u, 2) + w[1] * _shift_down(u, 1) + w[2] * u + b


def _ref_shift_down(ref, n):
    rows = ref.shape[0]
    return jnp.concatenate([_shift_down(ref[0:8, :], n), ref[8 - n:rows - n, :]], axis=0)


def _conv3_ref(u_ref, w, b):
    return w[0] * _ref_shift_down(u_ref, 2) + w[1] * _ref_shift_down(u_ref, 1) + w[2] * u_ref[...] + b


def conv_act_fwd_call(ug, uv, conv_w, conv_b, B, S):
    T = B * S
    nc = D_FF // CONV_TC

    def body(ug_ref, uv_ref, wg_ref, wv_ref, bg_ref, bv_ref, a_ref, cg_ref, cv_ref):
        gate = _conv3_ref(ug_ref, _taps(wg_ref), bg_ref[...])
        val = _conv3_ref(uv_ref, _taps(wv_ref), bv_ref[...])
        a_ref[...] = (gate * (1.0 / (1.0 + jnp.exp(-gate))) * val).astype(BF16)
        cg_ref[...] = gate.astype(BF16)
        cv_ref[...] = val.astype(BF16)

    def blk(rows, off):
        return pl.BlockSpec((rows, CONV_TC), lambda b, j: (b if rows == S else 0, off + j))

    return pl.pallas_call(
        body, name="conv_act_fwd", grid=(B, nc),
        in_specs=[blk(S, 0), blk(S, 0), blk(3, 0), blk(3, nc), blk(1, 0), blk(1, nc)],
        out_specs=[blk(S, 0)] * 3,
        out_shape=[jax.ShapeDtypeStruct((T, D_FF), BF16)] * 3,
        compiler_params=_params(("parallel", "parallel")),
    )(ug, uv, conv_w, conv_w, conv_b, conv_b)


def conv_act_bwd_call(ug, uv, cg, cv, dx2, w_down, conv_w, B, S):
    T = B * S
    nc = D_FF // CONV_TC

    def body(ug_ref, uv_ref, cg_ref, cv_ref, dx_ref, wd_ref, wg_ref, wv_ref,
             dug_ref, duv_ref, dwg_ref, dwv_ref, dbg_ref, dbv_ref):
        @pl.when(pl.program_id(1) == 0)
        def _():
            for r in (dwg_ref, dwv_ref, dbg_ref, dbv_ref):
                r[...] = jnp.zeros_like(r)

        gate = cg_ref[...].astype(F32)
        val = cv_ref[...].astype(F32)
        da = _nt(dx_ref[...], wd_ref[...])
        sig = 1.0 / (1.0 + jnp.exp(-gate))
        dval = da * (gate * sig)
        dgate = da * val * (sig * (1.0 + gate * (1.0 - sig)))
        for u_ref, d, w, du_ref, dw_ref, db_ref in ((ug_ref, dgate, _taps(wg_ref), dug_ref, dwg_ref, dbg_ref),
                                                   (uv_ref, dval, _taps(wv_ref), duv_ref, dwv_ref, dbv_ref)):
            u_ = u_ref[...]
            d1 = _shift_up(d, 1)
            d2 = _shift_up(d, 2)
            du_ref[...] = (w[2] * d + w[1] * d1 + w[0] * d2).astype(BF16)
            db_ref[...] += jnp.sum(d, axis=0, keepdims=True)
            dw_ref[0:1, :] += jnp.sum(d2 * u_, axis=0, keepdims=True)
            dw_ref[1:2, :] += jnp.sum(d1 * u_, axis=0, keepdims=True)
            dw_ref[2:3, :] += jnp.sum(d * u_, axis=0, keepdims=True)

    def blk(rows, off):
        return pl.BlockSpec((rows, CONV_TC), lambda j, b: (b if rows == S else 0, off + j))

    return pl.pallas_call(
        body, name="conv_act_bwd", grid=(nc, B),
        in_specs=[blk(S, 0), blk(S, 0), blk(S, 0), blk(S, 0), pl.BlockSpec((S, D_MODEL), lambda j, b: (b, 0)),
                  pl.BlockSpec((CONV_TC, D_MODEL), lambda j, b: (j, 0)), blk(3, 0), blk(3, nc)],
        out_specs=[blk(S, 0), blk(S, 0), blk(3, 0), blk(3, 0), blk(1, 0), blk(1, 0)],
        out_shape=[jax.ShapeDtypeStruct((T, D_FF), BF16), jax.ShapeDtypeStruct((T, D_FF), BF16),
                   jax.ShapeDtypeStruct((3, D_FF), F32), jax.ShapeDtypeStruct((3, D_FF), F32),
                   jax.ShapeDtypeStruct((1, D_FF), F32), jax.ShapeDtypeStruct((1, D_FF), F32)],
        compiler_params=_params(("parallel", "arbitrary")),
    )(ug, uv, cg, cv, dx2, w_down, conv_w, conv_w)


CHIP_MASKS = ((1, 0), (0, 1), (1, 1))


def _place():
    return lax.axis_index("x"), lax.axis_index("y"), lax.axis_index("c")


HALF_ALIGN = 32


def _any_specs(n):
    return [pl.BlockSpec(memory_space=pl.ANY)] * n


def _half_rows(r, half):
    return pl.ds(pl.multiple_of(half * (r // 2), HALF_ALIGN // 2), r // 2)


def _remote(src, dst, send_sem, recv_sem, device):
    return pltpu.make_async_remote_copy(src_ref=src, dst_ref=dst, send_sem=send_sem, recv_sem=recv_sem,
                                        device_id=device, device_id_type=MESH)


class Exchange:
    def __init__(self, ins, out_shape, sems, start, finish):
        self.ins, self.out_shape, self.sems, self.start, self.finish = list(ins), list(out_shape), list(sems), start, finish


def gather_group(shards):
    n = len(shards)
    split = [s.shape[0] % HALF_ALIGN == 0 for s in shards]

    def rows(w, half):
        return _half_rows(shards[w].shape[0], half) if split[w] else slice(None)

    def copies(ins, outs, sems):
        ici_s, ici_r, _, _, local_sems = sems
        x, y, c = _place()
        chip = 2 * x + y
        local = [pltpu.make_async_copy(ins[w], outs[w].at[chip], local_sems.at[w]) for w in range(n)]
        sends = [_remote(ins[w].at[rows(w, c)], outs[w].at[chip, rows(w, c)], ici_s.at[w, k], ici_r.at[w, k],
                         (x ^ fx, y ^ fy, c))
                 for w in range(n) for k, (fx, fy) in enumerate(CHIP_MASKS)]
        return local, sends

    def start(ins, outs, sems):
        local, sends = copies(ins, outs, sems)
        for cp in local + sends:
            cp.start()

    def finish(ins, outs, sems):
        ici_s, ici_r, d2d_s, d2d_r, _ = sems
        x, y, c = _place()
        sib = (x, y, 1 - c)
        local, sends = copies(ins, outs, sems)
        for w in range(n):
            for k, (fx, fy) in enumerate(CHIP_MASKS):
                landed = outs[w].at[2 * (x ^ fx) + (y ^ fy), rows(w, c)]
                _remote(landed, landed, ici_s.at[w, k], ici_r.at[w, k], sib).wait_recv()
                if split[w]:
                    cp = _remote(landed, landed, d2d_s.at[w, k], d2d_r.at[w, k], sib)
                    cp.start()
                    sends.append(cp)
        for w in range(n):
            for k, (fx, fy) in enumerate(CHIP_MASKS):
                if split[w]:
                    other = outs[w].at[2 * (x ^ fx) + (y ^ fy), rows(w, 1 - c)]
                    _remote(other, other, d2d_s.at[w, k], d2d_r.at[w, k], sib).wait_recv()
        for cp in sends:
            cp.wait_send()
        for cp in local:
            cp.wait()

    sems = pltpu.SemaphoreType.DMA((n, 3))
    return Exchange(shards, [jax.ShapeDtypeStruct((N_CHIPS,) + s.shape, s.dtype) for s in shards],
                    [sems, sems, sems, sems, pltpu.SemaphoreType.DMA((n,))], start, finish)


def swap_half(parts):
    n = len(parts)

    def copies(ins, outs, sems):
        x, y, c = _place()
        return [_remote(ins[w].at[:, _half_rows(parts[w].shape[1], 1 - c)], outs[w], sems[0].at[w], sems[1].at[w],
                        (x, y, 1 - c)) for w in range(n)]

    def start(ins, outs, sems):
        for cp in copies(ins, outs, sems):
            cp.start()

    def finish(ins, outs, sems):
        for cp in copies(ins, outs, sems):
            cp.wait_recv()
            cp.wait_send()

    return Exchange(parts, [jax.ShapeDtypeStruct((N_CHIPS, p.shape[1] // 2, p.shape[2]), F32) for p in parts],
                    [pltpu.SemaphoreType.DMA((n,))] * 2, start, finish)


def scatter_half(halves):
    n = len(halves)

    def copies(ins, outs, sems):
        x, y, c = _place()
        return [_remote(ins[w].at[2 * (x ^ fx) + (y ^ fy)], outs[w].at[k], sems[0].at[w, k], sems[1].at[w, k],
                        (x ^ fx, y ^ fy, c))
                for w in range(n) for k, (fx, fy) in enumerate(CHIP_MASKS)]

    def start(ins, outs, sems):
        for cp in copies(ins, outs, sems):
            cp.start()

    def finish(ins, outs, sems):
        for cp in copies(ins, outs, sems):
            cp.wait_recv()
            cp.wait_send()

    return Exchange(halves, [jax.ShapeDtypeStruct((3,) + h.shape[1:], h.dtype) for h in halves],
                    [pltpu.SemaphoreType.DMA((n, 3))] * 2, start, finish)


def swap_final(finals):
    n = len(finals)

    def copies(ins, outs, sems):
        x, y, c = _place()
        mine = [outs[w].at[_half_rows(2 * finals[w].shape[0], c)] for w in range(n)]
        local = [pltpu.make_async_copy(ins[w], mine[w], sems[2].at[w]) for w in range(n)]
        sends = [_remote(ins[w], mine[w], sems[0].at[w], sems[1].at[w], (x, y, 1 - c)) for w in range(n)]
        return local, sends

    def start(ins, outs, sems):
        local, sends = copies(ins, outs, sems)
        for cp in local + sends:
            cp.start()

    def finish(ins, outs, sems):
        x, y, c = _place()
        local, sends = copies(ins, outs, sems)
        for w in range(n):
            got = outs[w].at[_half_rows(2 * finals[w].shape[0], 1 - c)]
            _remote(got, got, sems[0].at[w], sems[1].at[w], (x, y, 1 - c)).wait_recv()
        for cp in sends:
            cp.wait_send()
        for cp in local:
            cp.wait()

    return Exchange(finals, [jax.ShapeDtypeStruct((2 * f.shape[0], f.shape[1]), F32) for f in finals],
                    [pltpu.SemaphoreType.DMA((n,))] * 3, start, finish)


def exchange_call(name, ex):
    n, m = len(ex.ins), len(ex.out_shape)

    def body(*refs):
        ins, outs, sems = refs[:n], refs[n:n + m], refs[n + m:]
        ex.start(ins, outs, sems)
        ex.finish(ins, outs, sems)

    return pl.pallas_call(body, name=name, in_specs=_any_specs(n), out_specs=_any_specs(m), out_shape=ex.out_shape,
                          scratch_shapes=ex.sems, compiler_params=_params())(*ex.ins)


def _call(body, ex, *, name, grid, in_specs, out_specs, out_shape, args, scratch_shapes=()):
    sem = ("arbitrary",) * len(grid)
    if ex is None:
        outs = pl.pallas_call(body, name=name, grid=grid, in_specs=in_specs, out_specs=out_specs, out_shape=out_shape,
                              scratch_shapes=list(scratch_shapes), compiler_params=_params(sem))(*args)
        return outs, None
    ni, no, ns = len(in_specs), len(out_specs), len(scratch_shapes)
    ne, me = len(ex.ins), len(ex.out_shape)

    def wrapped(*refs):
        own_in, ex_in = refs[:ni], refs[ni:ni + ne]
        own_out, ex_out = refs[ni + ne:ni + ne + no], refs[ni + ne + no:ni + ne + no + me]
        own_scr, ex_sems = refs[ni + ne + no + me:ni + ne + no + me + ns], refs[ni + ne + no + me + ns:]
        ids = [pl.program_id(a) for a in range(len(grid))]
        first = functools.reduce(jnp.logical_and, [i == 0 for i in ids])
        last = functools.reduce(jnp.logical_and, [i == g - 1 for i, g in zip(ids, grid)])

        @pl.when(first)
        def _():
            ex.start(ex_in, ex_out, ex_sems)

        body(*own_in, *own_out, *own_scr)

        @pl.when(last)
        def _():
            ex.finish(ex_in, ex_out, ex_sems)

    outs = pl.pallas_call(
        wrapped, name=name, grid=grid, in_specs=list(in_specs) + _any_specs(ne),
        out_specs=list(out_specs) + _any_specs(me), out_shape=list(out_shape) + ex.out_shape,
        scratch_shapes=list(scratch_shapes) + ex.sems, compiler_params=_params(sem))(*args, *ex.ins)
    return outs[:no], outs[no:]


def _row_tile(rows, cap, mult=8):
    return max(t for t in range(mult, min(rows, cap) + 1, mult) if rows % t == 0)


def add_half_call(name, part, got, where):
    _, rh, cols = got.shape
    tr = _row_tile(rh, 176, 16)
    nb = rh // tr

    def body(where_ref, p_ref, g_ref, own_ref, send_ref):
        t = p_ref[...] + g_ref[...]
        send_ref[...] = t.astype(BF16)
        chip = where_ref[1]
        own_ref[...] = p_ref[chip] + g_ref[chip]

    blk = (N_CHIPS, tr, cols)
    return pl.pallas_call(
        body, name=name,
        grid_spec=pltpu.PrefetchScalarGridSpec(
            num_scalar_prefetch=1, grid=(nb,),
            in_specs=[pl.BlockSpec(blk, lambda i, where_ref: (0, where_ref[0] * nb + i, 0)),
                      pl.BlockSpec(blk, lambda i, where_ref: (0, i, 0))],
            out_specs=[pl.BlockSpec((tr, cols), lambda i, where_ref: (i, 0)),
                       pl.BlockSpec(blk, lambda i, where_ref: (0, i, 0))]),
        out_shape=[jax.ShapeDtypeStruct((rh, cols), F32), jax.ShapeDtypeStruct(got.shape, BF16)],
        compiler_params=_params(("parallel",)),
    )(where, part, got)


def sum_chips_call(name, own, got):
    _, rh, cols = got.shape
    tr = _row_tile(rh, 176, 16)

    def body(h_ref, g_ref, o_ref):
        o_ref[...] = ((h_ref[...] + g_ref[0].astype(F32)) + g_ref[1].astype(F32)) + g_ref[2].astype(F32)

    return pl.pallas_call(
        body, name=name, grid=(rh // tr,),
        in_specs=[pl.BlockSpec((tr, cols), lambda i: (i, 0)), pl.BlockSpec((3, tr, cols), lambda i: (0, i, 0))],
        out_specs=pl.BlockSpec((tr, cols), lambda i: (i, 0)),
        out_shape=jax.ShapeDtypeStruct((rh, cols), F32),
        compiler_params=_params(("parallel",)),
    )(own, got)


def _adamw(w, g, m, v):
    m = ADAM_B1 * m + (1.0 - ADAM_B1) * g
    v = ADAM_B2 * v + (1.0 - ADAM_B2) * (g * g)
    m_hat = m / (1.0 - ADAM_B1 ** ADAM_STEP)
    v_hat = v / (1.0 - ADAM_B2 ** ADAM_STEP)
    delta = -ADAM_LR * (m_hat / (jnp.sqrt(v_hat) + ADAM_EPS) + ADAM_WD * w)
    return delta, m, v


def adamw_call(name, g, w, m, v):
    r, cols = w.shape
    tr = r if r % 8 else _row_tile(r, 256)

    def body(g_ref, w_ref, m_ref, v_ref, go_ref, d_ref, nm_ref, nv_ref):
        g = g_ref[...]
        go_ref[...] = g
        d_ref[...], nm_ref[...], nv_ref[...] = _adamw(w_ref[...], g, m_ref[...], v_ref[...])

    spec = pl.BlockSpec((tr, cols), lambda i: (i, 0))
    return pl.pallas_call(
        body, name=name, grid=(r // tr,), in_specs=[spec] * 4, out_specs=[spec] * 4,
        out_shape=[jax.ShapeDtypeStruct((r, cols), F32)] * 4,
        compiler_params=_params(("parallel",)),
    )(g, w, m, v)


def allsum_small_call(v):
    R = v.shape[0]

    def body(v_ref, out_ref, buf, send_sems, recv_sems):
        x, y, c = _place()
        me = 4 * x + 2 * y + c
        buf[me] = v_ref[...]
        sends = []
        for k in range(1, N_DEV):
            fx, fy, fc = (k >> 2) & 1, (k >> 1) & 1, k & 1
            cp = pltpu.make_async_remote_copy(
                src_ref=v_ref, dst_ref=buf.at[me], send_sem=send_sems.at[k - 1], recv_sem=recv_sems.at[k - 1],
                device_id=(x ^ fx, y ^ fy, c ^ fc), device_id_type=MESH)
            cp.start()
            sends.append(cp)
        for k in range(1, N_DEV):
            pltpu.make_async_remote_copy(
                src_ref=v_ref, dst_ref=buf.at[me ^ k], send_sem=send_sems.at[k - 1], recv_sem=recv_sems.at[k - 1],
                device_id=(x, y, c), device_id_type=MESH).wait_recv()
        acc = buf[0]
        for d in range(1, N_DEV):
            acc = acc + buf[d]
        out_ref[...] = acc
        for cp in sends:
            cp.wait_send()

    vm = pl.BlockSpec(memory_space=pltpu.VMEM)
    return pl.pallas_call(
        body, name="allsum_small", in_specs=[vm], out_specs=vm,
        out_shape=jax.ShapeDtypeStruct((R, LANES), F32),
        scratch_shapes=[pltpu.VMEM((N_DEV, R, LANES), F32), pltpu.SemaphoreType.DMA((N_DEV - 1,)),
                        pltpu.SemaphoreType.DMA((N_DEV - 1,))],
        compiler_params=_params(),
    )(v)


def _slab(flat, mult):
    n = flat.shape[-1]
    rows = -(-n // (LANES * mult)) * mult
    flat = jnp.pad(flat, [(0, 0)] * (flat.ndim - 1) + [(0, rows * LANES - n)])
    return flat.reshape(flat.shape[:-1] + (rows, LANES))


def full_from_chips(blocks, by_col):
    _, r, c = blocks.shape
    return blocks.transpose(1, 0, 2).reshape(r, N_CHIPS * c) if by_col else blocks.reshape(N_CHIPS * r, c)


def chips_from_full(full, by_col):
    if by_col:
        r, c = full.shape[0], full.shape[1] // N_CHIPS
        return full.reshape(r, N_CHIPS, c).transpose(1, 0, 2)
    return full.reshape(N_CHIPS, full.shape[0] // N_CHIPS, full.shape[1])


SMALL_PACK = SMALL_W + ("loss", "conv_w")
SMALL_PACK_N = {**SMALL_N, "loss": 1, "conv_w": 3 * 2 * D_FF}


def pack_small(vals):
    zero = jnp.zeros((1,), F32)
    return _slab(jnp.concatenate([vals[n].reshape(-1) if n in vals else jnp.tile(zero, SMALL_PACK_N[n])
                                  for n in SMALL_PACK]), 8)


def unpack_small(slab, shapes):
    flat = slab.reshape(-1)
    out, off = {}, 0
    for n in SMALL_PACK:
        out[n] = flat[off:off + SMALL_PACK_N[n]].reshape(shapes[n])
        off += SMALL_PACK_N[n]
    return out


def _split_heads(w, a, b):
    r = w.shape[0]
    w3 = w.reshape(r, HEADS, a + b)
    return w3[:, :, :a].reshape(r, HEADS * a), w3[:, :, a:].reshape(r, HEADS * b)


def _merge_heads(wa, wb, a, b):
    r = wa.shape[0]
    return jnp.concatenate([wa.reshape(r, HEADS, a), wb.reshape(r, HEADS, b)], axis=2).reshape(r, HEADS * (a + b))


def kernel(x, positions, g_mix, w_in, g_cq, w_uq, g_ckv, w_ukv, g_sb_out, g_mla_out, w_out, g_ffn, w_up, conv_w, conv_b, w_down, g_final, loss_target, m_g_mix, m_w_in, m_g_cq, m_w_uq, m_g_ckv, m_w_ukv, m_g_sb_out, m_g_mla_out, m_w_out, m_g_ffn, m_w_up, m_conv_w, m_conv_b, m_w_down, m_g_final, v_g_mix, v_w_in, v_g_cq, v_w_uq, v_g_ckv, v_w_ukv, v_g_sb_out, v_g_mla_out, v_w_out, v_g_ffn, v_w_up, v_conv_w, v_conv_b, v_w_down, v_g_final):
    given = dict(locals())
    B, S, _ = x.shape
    T = B * S
    w_big = {n: given[n][0] for n in BIG_W}
    m_big = {n: given["m_" + n][0] for n in BIG_W}
    v_big = {n: given["v_" + n][0] for n in BIG_W}

    first = ("w_in", "w_uq", "w_ukv")
    later = ("w_out", "w_up", "w_down", "conv_w")
    x2d = x.reshape(T, D_MODEL)
    h, got_w = rmsnorm_fwd_call("norm_mix", x2d, g_mix, ex=gather_group([w_big[n].astype(BF16) for n in first]))
    full = {n: full_from_chips(g_, BIG_SHARD[n][2]) for n, g_ in zip(first, got_w)}
    gather_later = gather_group([w_big[n] if n == "conv_w" else w_big[n].astype(BF16) for n in later])
    w_in_p = jnp.pad(full["w_in"], ((0, 0), (0, IN_COLS_PAD - IN_COLS)))
    w_uq_p = jnp.concatenate(_split_heads(full["w_uq"], HEAD_DIM, ROPE_DIM), axis=1)
    w_ukv_p = jnp.concatenate(_split_heads(full["w_ukv"], HEAD_DIM, HEAD_DIM), axis=1)

    half = ROPE_DIM // 2
    inv_freq = 1.0 / (ROPE_BASE ** (jnp.arange(half, dtype=F32) * (2.0 / ROPE_DIM)))
    cos, sin = rope_tab_call(positions.reshape(T, 1), jnp.tile(inv_freq, LANES // half).reshape(1, LANES))
    p = matmul_call("proj_in", h, w_in_p, "nn", tn=IN_COLS_PAD // 2)
    qn, qr, kn, vm, krt, cqn, ckvn = mla_prep_fwd_call(p, cos, sin, g_cq, g_ckv, w_uq_p, w_ukv_p)
    o_sb, lt_sb, got_w = sb_fwd_call(p, B, S, ex=gather_later)
    w_up4 = got_w[1]
    full.update({n: full_from_chips(g_, BIG_SHARD[n][2]) for n, g_ in zip(later, got_w) if n != "w_up"})
    conv_w_full = full["conv_w"]
    o_mla, lse = mla_fwd_call(qn, qr, kn, krt, vm, B, S)
    o_cat = outnorm_fwd_call(o_sb, o_mla, g_sb_out, g_mla_out)
    x1 = matmul_call("proj_out", o_cat, full["w_out"], "nn", res=x2d)
    hn = rmsnorm_fwd_call("norm_ffn", x1, g_ffn)
    u_g, u_v = ffn_up_call(hn, w_up4)
    act, c_g, c_v = conv_act_fwd_call(u_g, u_v, conv_w_full, conv_b, B, S)
    x2 = matmul_call("ffn_down", act, full["w_down"], "nn", res=x1)
    dx2, dx2b, loss_row, dg_final = final_loss_call(x2, g_final.reshape(1, D_MODEL), loss_target.reshape(T, D_MODEL))

    xi, yi, ci = _place()
    chip = (2 * xi + yi).astype(jnp.int32).reshape(1)
    where = jnp.stack([ci, 2 * xi + yi]).astype(jnp.int32)

    def add_halves(names, parts, sib_rows):
        return [add_half_call("add_half_" + n, p_, s_, where) for n, p_, s_ in zip(names, parts, sib_rows)]

    def sum_chips(names, halves, from_chips):
        return [sum_chips_call("sum_chips_" + n, h_[0], f_) for n, h_, f_ in zip(names, halves, from_chips)]

    ffn_w = ("w_down", "w_up")
    parts_ffn = [chips_from_full(wgrad_call("wgrad_down", act, dx2b, tn=512), False)]
    du_g, du_v, dcw_g, dcw_v, dcb_g, dcb_v = conv_act_bwd_call(
        u_g, u_v, c_g, c_v, dx2b, full["w_down"], conv_w_full, B, S)
    parts_ffn.append(wgrad_up_call(hn, du_g, du_v))
    dhn, sib_ffn = ffn_up_bwd_call(du_g, du_v, w_up4, swap_half(parts_ffn))
    dx1, dg_ffn = rmsnorm_bwd_call("norm_ffn_bwd", x1, g_ffn, dhn, dx2)
    parts_out = [chips_from_full(wgrad_call("wgrad_out", o_cat, dx1), False)]
    do_cat = matmul_call("proj_out_bwd", dx1, full["w_out"], "nt")
    do_sb, do_mla, dg_sb_out, dg_mla_out, sib_out = outnorm_bwd_call(
        o_sb, o_mla, g_sb_out, g_mla_out, do_cat, ex=swap_half(parts_out))
    early = ffn_w + ("w_out",)
    halves = add_halves(early, parts_ffn + parts_out, list(sib_ffn) + list(sib_out))
    dq_sb, dk_sb, dv_sb, from_chips = sb_bwd_call(p, lt_sb, do_sb, B, S, ex=scatter_half([h_[1] for h_ in halves]))
    finals = sum_chips(early, halves, from_chips)
    dqn, dqr4, dkn, dvm, dkrt4, done = mla_bwd_call(qn, qr, kn, krt, vm, o_mla, lse, do_mla, B, S, ex=swap_final(finals))
    grads = dict(zip(early, done))
    dcq, dckvr, dq_cat, dkv_cat, dg_cq, dg_ckv = mla_prep_bwd_call(
        p, cos, sin, g_cq, g_ckv, w_uq_p, w_ukv_p, dqn, dqr4, dkn, dvm, dkrt4)
    dw_uq_p = wgrad_call("wgrad_uq", cqn, dq_cat)
    dw_ukv_p = wgrad_call("wgrad_ukv", ckvn, dkv_cat)
    dp = (dq_sb, dk_sb, dv_sb, dcq, dckvr)
    late = ("w_uq", "w_ukv", "w_in")
    parts_late = [chips_from_full(g_, True) for g_ in (
        _merge_heads(dw_uq_p[:, :ATT_W], dw_uq_p[:, ATT_W:], HEAD_DIM, ROPE_DIM),
        _merge_heads(dw_ukv_p[:, :ATT_W], dw_ukv_p[:, ATT_W:], HEAD_DIM, HEAD_DIM),
        wgrad_in_call(h, dp)[:, :IN_COLS])]
    dh, sib_late = proj_in_bwd_call(dp, w_in_p, swap_half(parts_late))
    halves = add_halves(late, parts_late, sib_late)
    grad_x, dg_mix, from_chips = rmsnorm_bwd_call(
        "norm_mix_bwd", x2d, g_mix, dh, dx1, ex=scatter_half([h_[1] for h_ in halves]))
    finals = sum_chips(late, halves, from_chips)
    grads.update(zip(late, exchange_call("swap_final_late", swap_final(finals))))

    shapes = {n: given[n].shape for n in SMALL_W}
    shapes.update(loss=(), conv_w=(3, 2 * D_FF))
    small_g = {"g_mix": dg_mix, "g_cq": dg_cq, "g_ckv": dg_ckv, "g_sb_out": dg_sb_out, "g_mla_out": dg_mla_out,
               "g_ffn": dg_ffn, "conv_b": jnp.concatenate([dcb_g, dcb_v], axis=1), "g_final": dg_final,
               "loss": loss_row[0, :1], "conv_w": jnp.concatenate([dcw_g, dcw_v], axis=1)}
    gs_slab = allsum_small_call(pack_small(small_g))
    small_in = [pack_small({n: given[pre + n] for n in SMALL_W}) for pre in ("", "m_", "v_")]
    small_out = [unpack_small(s, shapes) for s in adamw_call("adamw_small", gs_slab, *small_in)]
    cw_cols = BIG_SHARD["conv_w"][1]
    grads["conv_w"] = lax.dynamic_slice_in_dim(small_out[0]["conv_w"], chip[0] * cw_cols, cw_cols, axis=1)

    big_out = {n: adamw_call("adamw_" + n, grads[n], w_big[n], m_big[n], v_big[n]) for n in BIG_W}
    weights = ("g_mix", "w_in", "g_cq", "w_uq", "g_ckv", "w_ukv", "g_sb_out", "g_mla_out", "w_out", "g_ffn",
               "w_up", "conv_w", "conv_b", "w_down", "g_final")
    outs = [small_out[0]["loss"], grad_x.reshape(B, S, D_MODEL)]
    for k in range(4):
        for n in weights:
            outs.append(big_out[n][k][None] if n in BIG_W else small_out[k][n])
    return tuple(outs)
```

```python
import functools

import jax
import jax.numpy as jnp
from jax import lax
from jax.experimental import pallas as pl
from jax.experimental.pallas import tpu as pltpu

F32 = jnp.float32
BF16 = jnp.bfloat16
MESH = pl.DeviceIdType.MESH

D_MODEL = 1024
HEADS = 8
HEAD_DIM = 64
ATT_W = HEADS * HEAD_DIM
ROPE_DIM = 32
ROPE_W = HEADS * ROPE_DIM
QK_DIM = HEAD_DIM + ROPE_DIM
Q_RANK = 384
KV_RANK = 256
D_FF = 2816
IN_COLS = 2208
IN_COLS_PAD = 2304
EPS = 1e-6
ROPE_BASE = 10000.0
SB_SCALE = HEAD_DIM ** -0.5
SB_SCALE2 = SB_SCALE * 1.4426950408889634
MLA_SCALE = QK_DIM ** -0.5
LOG2E = 1.4426950408889634
LN2 = 0.6931471805599453
MLA_SCALE2 = MLA_SCALE * LOG2E
LANES = 128
N_CHIPS = 4
N_DEV = 8
VMEM_LIMIT = 48 * 1024 * 1024
ATT_TQ = 256
ATT_TK = 256
ATT_PAIRS = 4
PAIR_LANES = [slice(i * LANES, (i + 1) * LANES) for i in range(ATT_PAIRS)]
SB_BWD_GROUP = 2
SB_FWD_GROUP = 4
MLA_GROUP = 4
NEG_BIG = -1e30

ADAM_LR = 0.001
ADAM_B1 = 0.9
ADAM_B2 = 0.999
ADAM_EPS = 1e-08
ADAM_WD = 0.01
ADAM_STEP = 10

BIG_W = ("w_in", "w_uq", "w_ukv", "w_out", "w_up", "conv_w", "w_down")
BIG_SHARD = {
    "w_in": (D_MODEL, IN_COLS // 4, True),
    "w_uq": (Q_RANK, HEADS * QK_DIM // 4, True),
    "w_ukv": (KV_RANK, 2 * ATT_W // 4, True),
    "w_out": (2 * ATT_W // 4, D_MODEL, False),
    "w_up": (D_MODEL, 2 * D_FF // 4, True),
    "conv_w": (3, 2 * D_FF // 4, True),
    "w_down": (D_FF // 4, D_MODEL, False),
}
SMALL_W = ("g_mix", "g_cq", "g_ckv", "g_sb_out", "g_mla_out", "g_ffn", "conv_b", "g_final")
SMALL_N = {"g_mix": D_MODEL, "g_cq": Q_RANK, "g_ckv": KV_RANK, "g_sb_out": ATT_W, "g_mla_out": ATT_W,
           "g_ffn": D_MODEL, "conv_b": 2 * D_FF, "g_final": D_MODEL}


def _params(sem=None, **kw):
    return pltpu.CompilerParams(dimension_semantics=sem, vmem_limit_bytes=VMEM_LIMIT, **kw)


def _dot(a, b, dims):
    return lax.dot_general(a, b, (dims, ((), ())), preferred_element_type=F32)


def _nn(a, b):
    return _dot(a, b, ((1,), (0,)))


def _nt(a, b):
    return _dot(a, b, ((1,), (1,)))


def _tn(a, b):
    return _dot(a, b, ((0,), (0,)))


def _split2(x):
    hi = x.astype(BF16)
    lo = (x - hi.astype(F32)).astype(BF16)
    return hi, lo


def _split3(x):
    hi = x.astype(BF16)
    r1 = x - hi.astype(F32)
    mid = r1.astype(BF16)
    return hi, mid, (r1 - mid.astype(F32)).astype(BF16)


def _rms_r(x, d):
    return lax.rsqrt(jnp.sum(x * x, axis=-1, keepdims=True) * (1.0 / d) + EPS)


def _rms_bwd(x, g, dy, d):
    r = _rms_r(x, d)
    xhat = x * r
    gy = dy * g
    dx = r * (gy - xhat * (jnp.sum(xhat * gy, axis=-1, keepdims=True) * (1.0 / d)))
    return dx, jnp.sum(dy * xhat, axis=0, keepdims=True)


def _rot(x):
    lane = lax.broadcasted_iota(jnp.int32, x.shape, x.ndim - 1)
    n = x.shape[-1]
    return jnp.where((lane & 31) < 16, pltpu.roll(x, n - 16, x.ndim - 1), pltpu.roll(x, 16, x.ndim - 1))


def _fold4(x):
    return x + pltpu.roll(x, 32, 1) + pltpu.roll(x, 64, 1) + pltpu.roll(x, 96, 1)


def matmul_call(name, a, b, mode, out_dtype=F32, res=None, tm=512, tn=None, ex=None):
    M, K = a.shape
    N = b.shape[1] if mode == "nn" else b.shape[0]
    tn = N if tn is None else tn
    assert M % tm == 0 and N % tn == 0

    def body(*refs):
        if res is None:
            a_ref, b_ref, o_ref = refs
        else:
            a_ref, b_ref, r_ref, o_ref = refs
        av = a_ref[...].astype(BF16)
        bv = b_ref[...].astype(BF16)
        acc = _nn(av, bv) if mode == "nn" else _nt(av, bv)
        if res is not None:
            acc = r_ref[...] + acc
        o_ref[...] = acc.astype(out_dtype)

    in_specs = [pl.BlockSpec((tm, K), lambda j, i: (i, 0))]
    if mode == "nn":
        in_specs.append(pl.BlockSpec((K, tn), lambda j, i: (0, j)))
    else:
        in_specs.append(pl.BlockSpec((tn, K), lambda j, i: (j, 0)))
    args = [a, b]
    if res is not None:
        in_specs.append(pl.BlockSpec((tm, tn), lambda j, i: (i, j)))
        args.append(res)
    outs, moved = _call(body, ex, name=name, grid=(N // tn, M // tm), in_specs=in_specs,
                        out_specs=[pl.BlockSpec((tm, tn), lambda j, i: (i, j))],
                        out_shape=[jax.ShapeDtypeStruct((M, N), out_dtype)], args=args)
    return outs[0] if ex is None else (outs[0], moved)


def wgrad_call(name, a, b, tn=None, tt=512, by_chip=False):
    T, M = a.shape
    N = b.shape[1]
    tn = N if tn is None else tn
    assert T % tt == 0 and N % tn == 0
    if by_chip:
        out_spec = pl.BlockSpec((None, M, tn), lambda j, t: (j, 0, 0))
        out_shape = jax.ShapeDtypeStruct((N // tn, M, tn), F32)
    else:
        out_spec = pl.BlockSpec((M, tn), lambda j, t: (0, j))
        out_shape = jax.ShapeDtypeStruct((M, N), F32)

    def body(a_ref, b_ref, o_ref):
        @pl.when(pl.program_id(1) == 0)
        def _():
            o_ref[...] = jnp.zeros_like(o_ref)

        o_ref[...] += _tn(a_ref[...].astype(BF16), b_ref[...].astype(BF16))

    return pl.pallas_call(
        body, name=name, grid=(N // tn, T // tt),
        in_specs=[pl.BlockSpec((tt, M), lambda j, t: (t, 0)), pl.BlockSpec((tt, tn), lambda j, t: (t, j))],
        out_specs=out_spec, out_shape=out_shape,
        compiler_params=_params(("parallel", "arbitrary")),
    )(a, b)


UP_COLS = 2 * D_FF // N_CHIPS


def ffn_up_call(hn, w4, tm=512):
    T, K = hn.shape

    def body(a_ref, wg_ref, wv_ref, ug_ref, uv_ref):
        a = a_ref[...]
        ug_ref[...] = _nn(a, wg_ref[...])
        uv_ref[...] = _nn(a, wv_ref[...])

    out = pl.BlockSpec((tm, UP_COLS), lambda j, i: (i, j))
    return pl.pallas_call(
        body, name="ffn_up", grid=(2, T // tm),
        in_specs=[pl.BlockSpec((tm, K), lambda j, i: (i, 0)),
                  pl.BlockSpec((None, K, UP_COLS), lambda j, i: (j, 0, 0)),
                  pl.BlockSpec((None, K, UP_COLS), lambda j, i: (2 + j, 0, 0))],
        out_specs=[out, out], out_shape=[jax.ShapeDtypeStruct((T, D_FF), F32)] * 2,
        compiler_params=_params(("parallel", "parallel")),
    )(hn, w4, w4)


def ffn_up_bwd_call(du_g, du_v, w4, ex, tm=512, tn=512):
    T = du_g.shape[0]
    N = w4.shape[1]

    def body(g_ref, v_ref, w_ref, o_ref):
        acc = _nt(g_ref[:, :UP_COLS], w_ref[0]) + _nt(g_ref[:, UP_COLS:], w_ref[1])
        o_ref[...] = acc + _nt(v_ref[:, :UP_COLS], w_ref[2]) + _nt(v_ref[:, UP_COLS:], w_ref[3])

    row = pl.BlockSpec((tm, D_FF), lambda j, i: (i, 0))
    outs, moved = _call(body, ex, name="ffn_up_bwd", grid=(N // tn, T // tm),
                        in_specs=[row, row, pl.BlockSpec((N_CHIPS, tn, UP_COLS), lambda j, i: (0, j, 0))],
                        out_specs=[pl.BlockSpec((tm, tn), lambda j, i: (i, j))],
                        out_shape=[jax.ShapeDtypeStruct((T, N), F32)], args=(du_g, du_v, w4))
    return outs[0], moved


def wgrad_up_call(hn, du_g, du_v, tt=512):
    T, M = hn.shape

    def body(a_ref, g_ref, v_ref, o_ref):
        @pl.when(pl.program_id(1) == 0)
        def _():
            o_ref[...] = jnp.zeros_like(o_ref)

        a = a_ref[...]
        o_ref[0] += _tn(a, g_ref[...])
        o_ref[1] += _tn(a, v_ref[...])

    col = pl.BlockSpec((tt, UP_COLS), lambda j, t: (t, j))
    out = pl.pallas_call(
        body, name="wgrad_up", grid=(2, T // tt),
        in_specs=[pl.BlockSpec((tt, M), lambda j, t: (t, 0)), col, col],
        out_specs=pl.BlockSpec((2, None, M, UP_COLS), lambda j, t: (0, j, 0, 0)),
        out_shape=jax.ShapeDtypeStruct((2, 2, M, UP_COLS), F32),
        compiler_params=_params(("parallel", "arbitrary")),
    )(hn, du_g, du_v)
    return out.reshape(N_CHIPS, M, UP_COLS)


IN_PIECES = ((0, ATT_W), (ATT_W, ATT_W), (2 * ATT_W, ATT_W), (3 * ATT_W, Q_RANK), (3 * ATT_W + Q_RANK, Q_RANK))


def proj_in_bwd_call(pieces, w_in_t, ex, tm=512):
    T = pieces[0].shape[0]
    N = w_in_t.shape[1]
    n = len(pieces)

    def body(*refs):
        o_ref = refs[2 * n]
        acc = _nn(refs[0][...].astype(BF16), refs[n][...])
        for i in range(1, n):
            acc = acc + _nn(refs[i][...].astype(BF16), refs[n + i][...])
        o_ref[...] = acc

    outs, moved = _call(
        body, ex, name="proj_in_bwd", grid=(T // tm,),
        in_specs=[pl.BlockSpec((tm, w), lambda i: (i, 0)) for _, w in IN_PIECES]
        + [pl.BlockSpec((w, N), functools.partial(lambda c, i: (c, 0), off // w)) for off, w in IN_PIECES],
        out_specs=[pl.BlockSpec((tm, N), lambda i: (i, 0))],
        out_shape=[jax.ShapeDtypeStruct((T, N), F32)], args=tuple(pieces) + (w_in_t,) * n)
    return outs[0], moved


def wgrad_in_call(h, pieces, tt=512):
    T, M = h.shape
    n = len(pieces)

    def body(*refs):
        a_ref, o_ref = refs[0], refs[n + 1]

        @pl.when(pl.program_id(0) == 0)
        def _():
            o_ref[...] = jnp.zeros_like(o_ref)

        a = a_ref[...]
        for i, (off, w) in enumerate(IN_PIECES):
            o_ref[off:off + w, :] += _tn(refs[1 + i][...].astype(BF16), a)

    return pl.pallas_call(
        body, name="wgrad_in", grid=(T // tt,),
        in_specs=[pl.BlockSpec((tt, M), lambda t: (t, 0))] + [pl.BlockSpec((tt, w), lambda t: (t, 0)) for _, w in IN_PIECES],
        out_specs=pl.BlockSpec((IN_COLS_PAD, M), lambda t: (0, 0)),
        out_shape=jax.ShapeDtypeStruct((IN_COLS_PAD, M), F32),
        compiler_params=_params(("arbitrary",)),
    )(h, *pieces)


def rmsnorm_fwd_call(name, x, g, tm=512, ex=None):
    T, d = x.shape

    def body(x_ref, g_ref, o_ref):
        x = x_ref[...]
        o_ref[...] = ((x * _rms_r(x, d)) * g_ref[...]).astype(BF16)

    row = pl.BlockSpec((tm, d), lambda i: (i, 0))
    outs, moved = _call(body, ex, name=name, grid=(T // tm,), in_specs=[row, pl.BlockSpec((1, d), lambda i: (0, 0))],
                        out_specs=[row], out_shape=[jax.ShapeDtypeStruct((T, d), BF16)], args=(x, g))
    return outs[0] if ex is None else (outs[0], moved)


def rmsnorm_bwd_call(name, x, g, dy, res, tm=512, ex=None):
    T, d = x.shape

    def body(x_ref, g_ref, dy_ref, r_ref, dx_ref, dg_ref):
        @pl.when(pl.program_id(0) == 0)
        def _():
            dg_ref[...] = jnp.zeros_like(dg_ref)

        dx, dg = _rms_bwd(x_ref[...], g_ref[...], dy_ref[...], d)
        dx_ref[...] = r_ref[...] + dx
        dg_ref[...] += dg

    row = pl.BlockSpec((tm, d), lambda i: (i, 0))
    vec = pl.BlockSpec((1, d), lambda i: (0, 0))
    outs, moved = _call(body, ex, name=name, grid=(T // tm,), in_specs=[row, vec, row, row], out_specs=[row, vec],
                        out_shape=[jax.ShapeDtypeStruct((T, d), F32), jax.ShapeDtypeStruct((1, d), F32)],
                        args=(x, g, dy, res))
    return tuple(outs) if ex is None else tuple(outs) + (moved,)


def outnorm_fwd_call(o_sb, o_mla, g_sb, g_mla, tm=512):
    T = o_sb.shape[0]

    def body(a_ref, b_ref, ga_ref, gb_ref, o_ref):
        a = a_ref[...]
        b = b_ref[...]
        ya = (a * _rms_r(a, ATT_W)) * ga_ref[...]
        yb = (b * _rms_r(b, ATT_W)) * gb_ref[...]
        o_ref[...] = jnp.concatenate([ya, yb], axis=1).astype(BF16)

    row = pl.BlockSpec((tm, ATT_W), lambda i: (i, 0))
    vec = pl.BlockSpec((1, ATT_W), lambda i: (0, 0))
    return pl.pallas_call(
        body, name="outnorm_fwd", grid=(T // tm,), in_specs=[row, row, vec, vec],
        out_specs=pl.BlockSpec((tm, 2 * ATT_W), lambda i: (i, 0)),
        out_shape=jax.ShapeDtypeStruct((T, 2 * ATT_W), BF16),
        compiler_params=_params(("parallel",)),
    )(o_sb, o_mla, g_sb, g_mla)


def outnorm_bwd_call(o_sb, o_mla, g_sb, g_mla, do_cat, tm=512, ex=None):
    T = o_sb.shape[0]

    def body(a_ref, b_ref, ga_ref, gb_ref, d_ref, da_ref, db_ref, dga_ref, dgb_ref):
        @pl.when(pl.program_id(0) == 0)
        def _():
            dga_ref[...] = jnp.zeros_like(dga_ref)
            dgb_ref[...] = jnp.zeros_like(dgb_ref)

        d = d_ref[...]
        da, dga = _rms_bwd(a_ref[...], ga_ref[...], d[:, :ATT_W], ATT_W)
        db, dgb = _rms_bwd(b_ref[...], gb_ref[...], d[:, ATT_W:], ATT_W)
        da_ref[...] = da
        db_ref[...] = db
        dga_ref[...] += dga
        dgb_ref[...] += dgb

    row = pl.BlockSpec((tm, ATT_W), lambda i: (i, 0))
    vec = pl.BlockSpec((1, ATT_W), lambda i: (0, 0))
    outs, moved = _call(
        body, ex, name="outnorm_bwd", grid=(T // tm,),
        in_specs=[row, row, vec, vec, pl.BlockSpec((tm, 2 * ATT_W), lambda i: (i, 0))],
        out_specs=[row, row, vec, vec],
        out_shape=[jax.ShapeDtypeStruct((T, ATT_W), F32), jax.ShapeDtypeStruct((T, ATT_W), F32),
                   jax.ShapeDtypeStruct((1, ATT_W), F32), jax.ShapeDtypeStruct((1, ATT_W), F32)],
        args=(o_sb, o_mla, g_sb, g_mla, do_cat))
    return tuple(outs) if ex is None else tuple(outs) + (moved,)


def final_loss_call(x2, g, target, tm=512):
    T, d = x2.shape

    def body(x_ref, g_ref, t_ref, dx_ref, dxb_ref, loss_ref, dg_ref):
        @pl.when(pl.program_id(0) == 0)
        def _():
            loss_ref[...] = jnp.zeros_like(loss_ref)
            dg_ref[...] = jnp.zeros_like(dg_ref)

        x = x_ref[...]
        g = g_ref[...]
        y = (x * _rms_r(x, d)) * g
        err = y - t_ref[...]
        loss_ref[...] += jnp.sum(jnp.sum(err * err, axis=1, keepdims=True), axis=0, keepdims=True) * (0.5 / d)
        dx, dg = _rms_bwd(x, g, err * (1.0 / d), d)
        dx_ref[...] = dx
        dxb_ref[...] = dx.astype(BF16)
        dg_ref[...] += dg

    row = pl.BlockSpec((tm, d), lambda i: (i, 0))
    vec = pl.BlockSpec((1, d), lambda i: (0, 0))
    return pl.pallas_call(
        body, name="final_loss", grid=(T // tm,), in_specs=[row, vec, row],
        out_specs=[row, row, pl.BlockSpec((1, LANES), lambda i: (0, 0)), vec],
        out_shape=[jax.ShapeDtypeStruct((T, d), F32), jax.ShapeDtypeStruct((T, d), BF16),
                   jax.ShapeDtypeStruct((1, LANES), F32), jax.ShapeDtypeStruct((1, d), F32)],
        compiler_params=_params(("arbitrary",)),
    )(x2, g, target)


def rope_tab_call(pos, inv_freq, tm=512):
    T = pos.shape[0]

    def body(p_ref, f_ref, c_ref, s_ref):
        ang = p_ref[...].astype(F32) * f_ref[...]
        lane = lax.broadcasted_iota(jnp.int32, ang.shape, 1)
        sn = jnp.sin(ang)
        c_ref[...] = jnp.cos(ang)
        s_ref[...] = jnp.where((lane & 31) < 16, -sn, sn)

    row = pl.BlockSpec((tm, LANES), lambda i: (i, 0))
    return pl.pallas_call(
        body, name="rope_tab", grid=(T // tm,),
        in_specs=[pl.BlockSpec((tm, 1), lambda i: (i, 0)), pl.BlockSpec((1, LANES), lambda i: (0, 0))],
        out_specs=[row, row],
        out_shape=[jax.ShapeDtypeStruct((T, LANES), F32)] * 2,
        compiler_params=_params(("parallel",)),
    )(pos, inv_freq)


def mla_prep_fwd_call(p, cos, sin, g_cq, g_ckv, w_uq_p, w_ukv_p, tm=512):
    T = p.shape[0]

    def body(cq_ref, ckvr_ref, c_ref, s_ref, gq_ref, gkv_ref, wq_ref, wkv_ref,
             qn_ref, qr_ref, kn_ref, vm_ref, krt_ref, cqn_ref, ckvn_ref):
        c = c_ref[...]
        s = s_ref[...]
        cq = cq_ref[...]
        cqn = ((cq * _rms_r(cq, Q_RANK)) * gq_ref[...]).astype(BF16)
        cqn_ref[...] = cqn
        q = _nn(cqn, wq_ref[...])
        qn_ref[...] = q[:, :ATT_W].astype(BF16)
        for g in range(ROPE_W // LANES):
            qr = q[:, ATT_W + g * LANES:ATT_W + (g + 1) * LANES]
            qr_ref[:, g * LANES:(g + 1) * LANES] = (qr * c + _rot(qr) * s).astype(BF16)
        ckvr = ckvr_ref[...]
        ckv = ckvr[:, :KV_RANK]
        ckvn = ((ckv * _rms_r(ckv, KV_RANK)) * gkv_ref[...]).astype(BF16)
        ckvn_ref[...] = ckvn
        kv = _nn(ckvn, wkv_ref[...])
        kn_ref[...] = kv[:, :ATT_W].astype(BF16)
        vm_ref[...] = kv[:, ATT_W:].astype(BF16)
        kr = _fold4(ckvr[:, KV_RANK:])
        krt_ref[...] = (kr * c + _rot(kr) * s).astype(BF16)

    def row(w, j=0):
        return pl.BlockSpec((tm, w), lambda i: (i, j))

    def full(a):
        return pl.BlockSpec(a.shape, lambda i: (0, 0))

    return pl.pallas_call(
        body, name="mla_prep_fwd", grid=(T // tm,),
        in_specs=[row(Q_RANK, 4), row(Q_RANK, 5), row(LANES), row(LANES), full(g_cq), full(g_ckv),
                  full(w_uq_p), full(w_ukv_p)],
        out_specs=[row(ATT_W), row(ROPE_W), row(ATT_W), row(ATT_W), row(LANES), row(Q_RANK), row(KV_RANK)],
        out_shape=[jax.ShapeDtypeStruct((T, w), BF16) for w in (ATT_W, ROPE_W, ATT_W, ATT_W, LANES, Q_RANK, KV_RANK)],
        compiler_params=_params(("parallel",)),
    )(p, p, cos, sin, g_cq, g_ckv, w_uq_p, w_ukv_p)


def mla_prep_bwd_call(p, cos, sin, g_cq, g_ckv, w_uq_p, w_ukv_p, dqn, dqr4, dkn, dvm, dkrt4, tm=512):
    T = p.shape[0]

    def body(cq_ref, ckvr_ref, c_ref, s_ref, gq_ref, gkv_ref, wq_ref, wkv_ref,
             dqn_ref, dqr4_ref, dkn_ref, dvm_ref, dkrt4_ref,
             dcq_ref, dckvr_ref, dq_ref, dkv_ref, dgq_ref, dgkv_ref):
        @pl.when(pl.program_id(0) == 0)
        def _():
            dgq_ref[...] = jnp.zeros_like(dgq_ref)
            dgkv_ref[...] = jnp.zeros_like(dgkv_ref)

        c = c_ref[...]
        s = s_ref[...]
        d4 = dqr4_ref[...]
        dqr = [d4[:, :128] + d4[:, 128:256], d4[:, 256:384] + d4[:, 384:]]
        dqr = [t * c + _rot(t * s) for t in dqr]
        dq = jnp.concatenate([dqn_ref[...]] + dqr, axis=1).astype(BF16)
        dq_ref[...] = dq
        dcq, dgq = _rms_bwd(cq_ref[...], gq_ref[...], _nt(dq, wq_ref[...]), Q_RANK)
        dcq_ref[...] = dcq
        dgq_ref[...] += dgq
        dkv = jnp.concatenate([dkn_ref[...], dvm_ref[...]], axis=1).astype(BF16)
        dkv_ref[...] = dkv
        ckvr = ckvr_ref[...]
        dckv, dgkv = _rms_bwd(ckvr[:, :KV_RANK], gkv_ref[...], _nt(dkv, wkv_ref[...]), KV_RANK)
        dgkv_ref[...] += dgkv
        k4 = dkrt4_ref[...]
        dkr = _fold4(k4[:, :128] + k4[:, 128:256] + k4[:, 256:384] + k4[:, 384:])
        dkr = dkr * c + _rot(dkr * s)
        lane = lax.broadcasted_iota(jnp.int32, dkr.shape, 1)
        dckvr_ref[...] = jnp.concatenate([dckv, jnp.where(lane < ROPE_DIM, dkr, 0.0)], axis=1)

    def row(w, j=0):
        return pl.BlockSpec((tm, w), lambda i: (i, j))

    def full(a):
        return pl.BlockSpec(a.shape, lambda i: (0, 0))

    return pl.pallas_call(
        body, name="mla_prep_bwd", grid=(T // tm,),
        in_specs=[row(Q_RANK, 4), row(Q_RANK, 5), row(LANES), row(LANES), full(g_cq), full(g_ckv),
                  full(w_uq_p), full(w_ukv_p), row(ATT_W), row(ATT_W), row(ATT_W), row(ATT_W), row(ATT_W)],
        out_specs=[row(Q_RANK), row(Q_RANK), row(ATT_W + ROPE_W), row(2 * ATT_W),
                   pl.BlockSpec((1, Q_RANK), lambda i: (0, 0)), pl.BlockSpec((1, KV_RANK), lambda i: (0, 0))],
        out_shape=[jax.ShapeDtypeStruct((T, Q_RANK), F32), jax.ShapeDtypeStruct((T, Q_RANK), F32),
                   jax.ShapeDtypeStruct((T, ATT_W + ROPE_W), BF16), jax.ShapeDtypeStruct((T, 2 * ATT_W), BF16),
                   jax.ShapeDtypeStruct((1, Q_RANK), F32), jax.ShapeDtypeStruct((1, KV_RANK), F32)],
        compiler_params=_params(("arbitrary",)),
    )(p, p, cos, sin, g_cq, g_ckv, w_uq_p, w_ukv_p, dqn, dqr4, dkn, dvm, dkrt4)


def _iota2(shape, axis):
    return lax.broadcasted_iota(jnp.int32, shape, axis)


def _head_masks():
    lane = _iota2((1, LANES), 1)
    return lane < HEAD_DIM, lane >= HEAD_DIM


def _pair(x, masks, dtype=BF16):
    return [jnp.where(m, x, 0.0).astype(dtype) for m in masks]


def _log_gates(z):
    keep = jnp.maximum(z, 0.0) + jnp.log2(1.0 + jnp.exp2(-jnp.abs(z)))
    return z - keep, keep


def _last_row(x):
    return _row_of(x[x.shape[0] - 8:, :], 7)


def _lane_selector(group):
    return jnp.where(_iota2((16, LANES), 1) // group == _iota2((16, LANES), 0), 1.0, 0.0).astype(BF16)


def _rows8(sel_t, x):
    hi = x.astype(BF16)
    r1 = x - hi.astype(F32)
    mid = r1.astype(BF16)
    lo = (r1 - mid.astype(F32)).astype(BF16)
    return _nt(sel_t, hi) + _nt(sel_t, mid) + _nt(sel_t, lo)


def _row_of(x8, j):
    return jnp.sum(jnp.where(_iota2(x8.shape, 0) == j, x8, 0.0), axis=0, keepdims=True)


def sb_fwd_call(p, B, S, ex=None):
    T = B * S
    TQ, TK = ATT_TQ, ATT_TK
    nq = S // TQ

    def body(q_ref, k_ref, v_ref, o_ref, lt_ref):
        qi = pl.program_id(2)
        masks = _head_masks()
        qm = [_pair(q_ref[:, sl] * SB_SCALE2, masks) for sl in PAIR_LANES]
        row = _iota2((TQ, TK), 0)
        col = _iota2((TQ, TK), 1)
        tri = jnp.where(row > col, 1.0, 0.0).astype(BF16)
        tri2 = jnp.concatenate([tri, tri], axis=0)
        vis = col < row
        o_ref[...] = jnp.zeros_like(o_ref)

        def group(k0, pairs, carry, diag):
            heads = [(pp, j) for pp in pairs for j in range(2)]
            n = range(len(heads))
            k = {pp: k_ref[pl.ds(k0, TK), PAIR_LANES[pp]].astype(BF16) for pp in pairs}
            vm = {pp: _pair(v_ref[pl.ds(k0, TK), PAIR_LANES[pp]], masks) for pp in pairs}
            gates = [_log_gates(_nt(qm[pp][j], k[pp])) for pp, j in heads]
            lb = [g[0] for g in gates]
            keep = [jnp.where(vis, g[1], 0.0) if diag else g[1] for g in gates]
            tail = [_nn(jnp.concatenate(_split2(keep[h]), axis=1), tri2) + carry[h] for h in n]
            a = [jnp.exp2(lb[h] - tail[h]) for h in n]
            if diag:
                a = [jnp.where(vis, x, 0.0) for x in a]
            ab = [x.astype(BF16) for x in a]
            for i, pp in enumerate(pairs):
                o_ref[:, PAIR_LANES[pp]] += _nn(ab[2 * i], vm[pp][0]) + _nn(ab[2 * i + 1], vm[pp][1])
            return [tail[h][:, 0:1] + keep[h][:, 0:1] for h in n]

        def step(kb, carry, diag):
            k0 = pl.multiple_of(kb * TK, TK)
            out = []
            for g in range(0, ATT_PAIRS, SB_FWD_GROUP):
                out += group(k0, list(range(g, g + SB_FWD_GROUP)), carry[2 * g:2 * (g + SB_FWD_GROUP)], diag)
            return tuple(out)

        zero = jnp.zeros((TQ, 1), F32)
        carry = step(qi, (zero,) * (2 * ATT_PAIRS), True)
        carry = lax.fori_loop(0, qi, lambda i, c: step(qi - 1 - i, c, False), carry)
        lane = _iota2((TQ, LANES), 1)
        for pp, sl in enumerate(PAIR_LANES):
            lt_ref[:, sl] = jnp.where(lane == 0, carry[2 * pp], jnp.where(lane == 1, carry[2 * pp + 1], 0.0))

    W = ATT_PAIRS * LANES
    qspec = pl.BlockSpec((TQ, W), lambda b, h, i: (b * nq + i, h))
    outs, moved = _call(
        body, ex, name="sb_fwd", grid=(B, HEADS // 2 // ATT_PAIRS, nq),
        in_specs=[qspec,
                  pl.BlockSpec((S, W), lambda b, h, i: (b, ATT_W // W + h)),
                  pl.BlockSpec((S, W), lambda b, h, i: (b, 2 * ATT_W // W + h))],
        out_specs=[qspec, qspec],
        out_shape=[jax.ShapeDtypeStruct((T, ATT_W), F32)] * 2, args=(p, p, p))
    return tuple(outs) if ex is None else tuple(outs) + (moved,)


def sb_bwd_call(p, lt, do, B, S, ex=None):
    T = B * S
    TQ, TK = ATT_TQ, ATT_TK
    nq = S // TQ

    def body(q_ref, k_ref, v_ref, lt_ref, do_ref, dq_ref, dk_ref, dv_ref):
        qi = pl.program_id(2)

        @pl.when(qi == 0)
        def _():
            dk_ref[...] = jnp.zeros_like(dk_ref)
            dv_ref[...] = jnp.zeros_like(dv_ref)

        masks = _head_masks()
        qm = [_pair(q_ref[:, sl] * SB_SCALE2, masks) for sl in PAIR_LANES]
        dom = [_pair(do_ref[:, sl], masks) for sl in PAIR_LANES]
        start = []
        for sl in PAIR_LANES:
            l8 = _rows8(_lane_selector(1), lt_ref[:, sl])
            start += [-_row_of(l8, 0), jnp.zeros((1, TQ), F32), -_row_of(l8, 1), jnp.zeros((1, TQ), F32)]
        row = _iota2((TK, TQ), 0)
        col = _iota2((TK, TQ), 1)
        incl = jnp.where(col <= row, 1.0, 0.0).astype(BF16)
        incl2 = jnp.concatenate([incl, incl], axis=1)
        excl = jnp.where(col < row, 1.0, 0.0).astype(BF16)
        vis = row < col
        dq_ref[...] = jnp.zeros_like(dq_ref)

        def group(k0, pairs, carry, diag):
            heads = [(pp, j) for pp in pairs for j in range(2)]
            n = range(len(heads))
            kf = {pp: k_ref[pl.ds(k0, TK), PAIR_LANES[pp]] for pp in pairs}
            km = {pp: _pair(kf[pp], masks) for pp in pairs}
            v = {pp: v_ref[pl.ds(k0, TK), PAIR_LANES[pp]].astype(BF16) for pp in pairs}
            z = [_nt(kf[pp].astype(BF16), qm[pp][j]) for pp, j in heads]
            da = [_nt(v[pp], dom[pp][j]) for pp, j in heads]
            gates = [_log_gates(x) for x in z]
            lb = [g[0] for g in gates]
            keep = [jnp.where(vis, g[1], 0.0) if diag else g[1] for g in gates]
            left = [_nn(incl2, jnp.concatenate(_split2(keep[h]), axis=0)) + carry[2 * h] for h in n]
            a = [jnp.exp2(lb[h] + left[h]) for h in n]
            if diag:
                a = [jnp.where(vis, x, 0.0) for x in a]
            e = [a[h] * da[h] for h in n]
            before = [_nn(excl, e[h].astype(BF16)) + carry[2 * h + 1] for h in n]
            dz = [e[h] - jnp.exp2(lb[h]) * (e[h] + before[h]) for h in n]
            if diag:
                dz = [jnp.where(vis, x, 0.0) for x in dz]
            dzb = [x.astype(BF16) for x in dz]
            ab = [x.astype(BF16) for x in a]
            out = []
            for h in n:
                out += [_last_row(left[h]), _last_row(before[h]) + _last_row(e[h])]
            for i, pp in enumerate(pairs):
                sl = PAIR_LANES[pp]
                dk_ref[pl.ds(k0, TK), sl] += _nn(dzb[2 * i], qm[pp][0]) + _nn(dzb[2 * i + 1], qm[pp][1])
                dv_ref[pl.ds(k0, TK), sl] += _nn(ab[2 * i], dom[pp][0]) + _nn(ab[2 * i + 1], dom[pp][1])
                dq_ref[:, sl] += _tn(dzb[2 * i], km[pp][0]) + _tn(dzb[2 * i + 1], km[pp][1])
            return out

        def step(kb, carry, diag):
            k0 = pl.multiple_of(kb * TK, TK)
            out = []
            for g in range(0, ATT_PAIRS, SB_BWD_GROUP):
                out += group(k0, list(range(g, g + SB_BWD_GROUP)), carry[4 * g:4 * (g + SB_BWD_GROUP)], diag)
            return tuple(out)

        carry = lax.fori_loop(0, qi, lambda i, c: step(i, c, False), tuple(start))
        step(qi, carry, True)
        dq_ref[...] *= SB_SCALE

        @pl.when(qi == nq - 1)
        def _():
            dk_ref[...] *= LN2

    W = ATT_PAIRS * LANES
    qspec = pl.BlockSpec((TQ, W), lambda b, h, i: (b * nq + i, h))
    sspec = pl.BlockSpec((S, W), lambda b, h, i: (b, h))
    outs, moved = _call(
        body, ex, name="sb_bwd", grid=(B, HEADS // 2 // ATT_PAIRS, nq),
        in_specs=[qspec,
                  pl.BlockSpec((S, W), lambda b, h, i: (b, ATT_W // W + h)),
                  pl.BlockSpec((S, W), lambda b, h, i: (b, 2 * ATT_W // W + h)),
                  qspec, qspec],
        out_specs=[qspec, sspec, sspec],
        out_shape=[jax.ShapeDtypeStruct((T, ATT_W), F32)] * 3, args=(p, p, p, lt, do))
    return tuple(outs) if ex is None else tuple(outs) + (moved,)


ALL_PAIRS = [slice(i * LANES, (i + 1) * LANES) for i in range(HEADS // 2)]


def _rope_masks(hp):
    grp = _iota2((1, LANES), 1) // ROPE_DIM
    return [grp == ((2 * hp + j) % 4) for j in range(2)]


def _mla_queries(qn_ref, qr_ref, masks):
    out = []
    for pp, sl in enumerate(ALL_PAIRS):
        qnv = qn_ref[:, sl]
        qrv = qr_ref[:, ALL_PAIRS[pp // 2]]
        rmasks = _rope_masks(pp)
        out.append([jnp.concatenate([jnp.where(masks[j], qnv, 0), jnp.where(rmasks[j], qrv, 0)], axis=1).astype(BF16)
                    for j in range(2)])
    return out


def mla_fwd_call(qn, qr, kn, krt, vm, B, S):
    T = B * S
    TQ, TK = ATT_TQ, ATT_TK
    nq = S // TQ

    def body(qn_ref, qr_ref, kn_ref, kr_ref, v_ref, o_ref, lse_ref):
        qi = pl.program_id(1)
        masks = _head_masks()
        qcat = _mla_queries(qn_ref, qr_ref, masks)
        row = _iota2((TQ, TK), 0)
        col = _iota2((TQ, TK), 1)
        vis = col <= row
        o_ref[...] = jnp.zeros_like(o_ref)

        def group(k0, pairs, carry, diag):
            heads = [(pp, j) for pp in pairs for j in range(2)]
            n = range(len(heads))
            krv = kr_ref[pl.ds(k0, TK), :]
            kcat = {pp: jnp.concatenate([kn_ref[pl.ds(k0, TK), ALL_PAIRS[pp]], krv], axis=1) for pp in pairs}
            vmk = {pp: _pair(v_ref[pl.ds(k0, TK), ALL_PAIRS[pp]], masks) for pp in pairs}
            s = [_nt(qcat[pp][j], kcat[pp]) * MLA_SCALE2 for pp, j in heads]
            if diag:
                s = [jnp.where(vis, x, NEG_BIG) for x in s]
            m_new = [jnp.maximum(carry[2 * h], jnp.max(s[h], axis=1, keepdims=True)) for h in n]
            alpha = [jnp.exp2(carry[2 * h] - m_new[h]) for h in n]
            pexp = [jnp.exp2(s[h] - m_new[h]) for h in n]
            out = []
            for h in n:
                out += [m_new[h], alpha[h] * carry[2 * h + 1] + jnp.sum(pexp[h], axis=1, keepdims=True)]
            pb = [x.astype(BF16) for x in pexp]
            for i, pp in enumerate(pairs):
                sl = ALL_PAIRS[pp]
                scale = jnp.where(masks[0], alpha[2 * i], alpha[2 * i + 1])
                o_ref[:, sl] = o_ref[:, sl] * scale + (_nn(pb[2 * i], vmk[pp][0]) + _nn(pb[2 * i + 1], vmk[pp][1]))
            return out

        def step(kb, carry, diag):
            k0 = pl.multiple_of(kb * TK, TK)
            out = []
            for g in range(0, len(ALL_PAIRS), MLA_GROUP):
                out += group(k0, list(range(g, g + MLA_GROUP)), carry[4 * g:4 * (g + MLA_GROUP)], diag)
            return tuple(out)

        neg = jnp.full((TQ, 1), NEG_BIG, F32)
        zero = jnp.zeros((TQ, 1), F32)
        carry = step(qi, (neg, zero) * (2 * len(ALL_PAIRS)), True)
        carry = lax.fori_loop(0, qi, lambda i, c: step(qi - 1 - i, c, False), carry)
        lane = _iota2((TQ, LANES), 1)
        for pp, sl in enumerate(ALL_PAIRS):
            m0, l0, m1, l1 = carry[4 * pp:4 * pp + 4]
            o_ref[:, sl] = o_ref[:, sl] * jnp.where(masks[0], 1.0 / l0, 1.0 / l1)
            lse_ref[:, sl] = jnp.where(lane == 0, m0 * LN2 + jnp.log(l0), jnp.where(lane == 1, m1 * LN2 + jnp.log(l1), 0.0))

    def rows(w):
        return pl.BlockSpec((TQ, w), lambda b, i: (b * nq + i, 0))

    def seq(w):
        return pl.BlockSpec((S, w), lambda b, i: (b, 0))

    return pl.pallas_call(
        body, name="mla_fwd", grid=(B, nq),
        in_specs=[rows(ATT_W), rows(ROPE_W), seq(ATT_W), seq(LANES), seq(ATT_W)],
        out_specs=[rows(ATT_W), rows(ATT_W)],
        out_shape=[jax.ShapeDtypeStruct((T, ATT_W), F32)] * 2,
        compiler_params=_params(("arbitrary", "arbitrary")),
    )(qn, qr, kn, krt, vm)


def mla_bwd_call(qn, qr, kn, krt, vm, o, lse, do, B, S, ex=None):
    T = B * S
    TQ, TK = ATT_TQ, ATT_TK
    nq = S // TQ

    def body(qn_ref, qr_ref, kn_ref, kr_ref, v_ref, o_ref, lse_ref, do_ref,
             dqn_ref, dqr_ref, dkn_ref, dv_ref, dkr_ref):
        qi = pl.program_id(1)

        @pl.when(qi == 0)
        def _():
            dkn_ref[...] = jnp.zeros_like(dkn_ref)
            dv_ref[...] = jnp.zeros_like(dv_ref)
            dkr_ref[...] = jnp.zeros_like(dkr_ref)

        masks = _head_masks()
        qcat = _mla_queries(qn_ref, qr_ref, masks)
        dom, dsum, lse = [], [], []
        for sl in ALL_PAIRS:
            do = do_ref[:, sl]
            dom.append(_pair(do, masks))
            d8 = _rows8(_lane_selector(HEAD_DIM), do * o_ref[:, sl])
            l8 = _rows8(_lane_selector(1), lse_ref[:, sl])
            dsum.append([_row_of(d8, j) for j in range(2)])
            lse.append([_row_of(l8, j) * LOG2E for j in range(2)])
        row = _iota2((TK, TQ), 0)
        col = _iota2((TK, TQ), 1)
        vis = row <= col
        dqn_ref[...] = jnp.zeros_like(dqn_ref)
        dqr_ref[...] = jnp.zeros_like(dqr_ref)

        def group(k0, pairs, diag):
            heads = [(pp, j) for pp in pairs for j in range(2)]
            n = range(len(heads))
            krv = kr_ref[pl.ds(k0, TK), :]
            knv = {pp: kn_ref[pl.ds(k0, TK), ALL_PAIRS[pp]] for pp in pairs}
            kcat = {pp: jnp.concatenate([knv[pp], krv], axis=1) for pp in pairs}
            v = {pp: v_ref[pl.ds(k0, TK), ALL_PAIRS[pp]] for pp in pairs}
            s = [_nt(kcat[pp], qcat[pp][j]) * MLA_SCALE2 for pp, j in heads]
            dp_ = [_nt(v[pp], dom[pp][j]) for pp, j in heads]
            pr = [jnp.exp2(s[h] - lse[pp][j]) for h, (pp, j) in enumerate(heads)]
            if diag:
                pr = [jnp.where(vis, x, 0.0) for x in pr]
            ds = [(pr[h] * (dp_[h] - dsum[pp][j]) * MLA_SCALE).astype(BF16) for h, (pp, j) in enumerate(heads)]
            pb = [x.astype(BF16) for x in pr]
            for i, pp in enumerate(pairs):
                sl = ALL_PAIRS[pp]
                rmasks = _rope_masks(pp)
                kcat_j = [jnp.concatenate([jnp.where(masks[j], knv[pp], 0), jnp.where(rmasks[j], krv, 0)],
                                          axis=1).astype(BF16) for j in range(2)]
                dv_ref[pl.ds(k0, TK), sl] += _nn(pb[2 * i], dom[pp][0]) + _nn(pb[2 * i + 1], dom[pp][1])
                dk = _nn(ds[2 * i], qcat[pp][0]) + _nn(ds[2 * i + 1], qcat[pp][1])
                dq = _tn(ds[2 * i], kcat_j[0]) + _tn(ds[2 * i + 1], kcat_j[1])
                dqn_ref[:, sl] += dq[:, :LANES]
                dqr_ref[:, sl] += dq[:, LANES:]
                dkn_ref[pl.ds(k0, TK), sl] += dk[:, :LANES]
                dkr_ref[pl.ds(k0, TK), sl] += dk[:, LANES:]

        def step(kb, diag):
            k0 = pl.multiple_of(kb * TK, TK)
            for g in range(0, len(ALL_PAIRS), MLA_GROUP):
                group(k0, list(range(g, g + MLA_GROUP)), diag)

        step(qi, True)

        def loop(i, c):
            step(qi - 1 - i, False)
            return c

        lax.fori_loop(0, qi, loop, 0)

    def rows(w):
        return pl.BlockSpec((TQ, w), lambda b, i: (b * nq + i, 0))

    def seq(w):
        return pl.BlockSpec((S, w), lambda b, i: (b, 0))

    outs, moved = _call(
        body, ex, name="mla_bwd", grid=(B, nq),
        in_specs=[rows(ATT_W), rows(ROPE_W), seq(ATT_W), seq(LANES), seq(ATT_W), rows(ATT_W), rows(ATT_W), rows(ATT_W)],
        out_specs=[rows(ATT_W), rows(ATT_W), seq(ATT_W), seq(ATT_W), seq(ATT_W)],
        out_shape=[jax.ShapeDtypeStruct((T, ATT_W), F32)] * 5, args=(qn, qr, kn, krt, vm, o, lse, do))
    return tuple(outs) if ex is None else tuple(outs) + (moved,)


CONV_TC = 256


def _shift_down(x, n):
    return jnp.where(_iota2(x.shape, 0) >= n, pltpu.roll(x, n, 0), 0.0)


def _shift_up(x, n):
    rows = x.shape[0]
    return jnp.where(_iota2(x.shape, 0) < rows - n, pltpu.roll(x, rows - n, 0), 0.0)


def _taps(w_ref):
    return [w_ref[k:k + 1, :] for k in range(3)]


def _conv3(u, w, b):
    return w[0] * _shift_down(u, 2) + w[1] * _shift_down(u, 1) + w[2] * u + b


def _ref_shift_down(ref, n):
    rows = ref.shape[0]
    return jnp.concatenate([_shift_down(ref[0:8, :], n), ref[8 - n:rows - n, :]], axis=0)


def _conv3_ref(u_ref, w, b):
    return w[0] * _ref_shift_down(u_ref, 2) + w[1] * _ref_shift_down(u_ref, 1) + w[2] * u_ref[...] + b


def conv_act_fwd_call(ug, uv, conv_w, conv_b, B, S):
    T = B * S
    nc = D_FF // CONV_TC

    def body(ug_ref, uv_ref, wg_ref, wv_ref, bg_ref, bv_ref, a_ref, cg_ref, cv_ref):
        gate = _conv3_ref(ug_ref, _taps(wg_ref), bg_ref[...])
        val = _conv3_ref(uv_ref, _taps(wv_ref), bv_ref[...])
        a_ref[...] = (gate * (1.0 / (1.0 + jnp.exp(-gate))) * val).astype(BF16)
        cg_ref[...] = gate.astype(BF16)
        cv_ref[...] = val.astype(BF16)

    def blk(rows, off):
        return pl.BlockSpec((rows, CONV_TC), lambda b, j: (b if rows == S else 0, off + j))

    return pl.pallas_call(
        body, name="conv_act_fwd", grid=(B, nc),
        in_specs=[blk(S, 0), blk(S, 0), blk(3, 0), blk(3, nc), blk(1, 0), blk(1, nc)],
        out_specs=[blk(S, 0)] * 3,
        out_shape=[jax.ShapeDtypeStruct((T, D_FF), BF16)] * 3,
        compiler_params=_params(("parallel", "parallel")),
    )(ug, uv, conv_w, conv_w, conv_b, conv_b)


def conv_act_bwd_call(ug, uv, cg, cv, dx2, w_down, conv_w, B, S):
    T = B * S
    nc = D_FF // CONV_TC

    def body(ug_ref, uv_ref, cg_ref, cv_ref, dx_ref, wd_ref, wg_ref, wv_ref,
             dug_ref, duv_ref, dwg_ref, dwv_ref, dbg_ref, dbv_ref):
        @pl.when(pl.program_id(1) == 0)
        def _():
            for r in (dwg_ref, dwv_ref, dbg_ref, dbv_ref):
                r[...] = jnp.zeros_like(r)

        gate = cg_ref[...].astype(F32)
        val = cv_ref[...].astype(F32)
        da = _nt(dx_ref[...], wd_ref[...])
        sig = 1.0 / (1.0 + jnp.exp(-gate))
        dval = da * (gate * sig)
        dgate = da * val * (sig * (1.0 + gate * (1.0 - sig)))
        for u_ref, d, w, du_ref, dw_ref, db_ref in ((ug_ref, dgate, _taps(wg_ref), dug_ref, dwg_ref, dbg_ref),
                                                   (uv_ref, dval, _taps(wv_ref), duv_ref, dwv_ref, dbv_ref)):
            u_ = u_ref[...]
            d1 = _shift_up(d, 1)
            d2 = _shift_up(d, 2)
            du_ref[...] = (w[2] * d + w[1] * d1 + w[0] * d2).astype(BF16)
            db_ref[...] += jnp.sum(d, axis=0, keepdims=True)
            dw_ref[0:1, :] += jnp.sum(d2 * u_, axis=0, keepdims=True)
            dw_ref[1:2, :] += jnp.sum(d1 * u_, axis=0, keepdims=True)
            dw_ref[2:3, :] += jnp.sum(d * u_, axis=0, keepdims=True)

    def blk(rows, off):
        return pl.BlockSpec((rows, CONV_TC), lambda j, b: (b if rows == S else 0, off + j))

    return pl.pallas_call(
        body, name="conv_act_bwd", grid=(nc, B),
        in_specs=[blk(S, 0), blk(S, 0), blk(S, 0), blk(S, 0), pl.BlockSpec((S, D_MODEL), lambda j, b: (b, 0)),
                  pl.BlockSpec((CONV_TC, D_MODEL), lambda j, b: (j, 0)), blk(3, 0), blk(3, nc)],
        out_specs=[blk(S, 0), blk(S, 0), blk(3, 0), blk(3, 0), blk(1, 0), blk(1, 0)],
        out_shape=[jax.ShapeDtypeStruct((T, D_FF), BF16), jax.ShapeDtypeStruct((T, D_FF), BF16),
                   jax.ShapeDtypeStruct((3, D_FF), F32), jax.ShapeDtypeStruct((3, D_FF), F32),
                   jax.ShapeDtypeStruct((1, D_FF), F32), jax.ShapeDtypeStruct((1, D_FF), F32)],
        compiler_params=_params(("parallel", "arbitrary")),
    )(ug, uv, cg, cv, dx2, w_down, conv_w, conv_w)


CHIP_MASKS = ((1, 0), (0, 1), (1, 1))


def _place():
    return lax.axis_index("x"), lax.axis_index("y"), lax.axis_index("c")


HALF_ALIGN = 32


def _any_specs(n):
    return [pl.BlockSpec(memory_space=pl.ANY)] * n


def _splits(shape):
    r, c = shape
    return "rows" if r % HALF_ALIGN == 0 else "cols" if c % (2 * LANES) == 0 else None


def _half(shape, half):
    r, c = shape
    how = _splits(shape)
    if how == "rows":
        return (pl.ds(pl.multiple_of(half * (r // 2), HALF_ALIGN // 2), r // 2), slice(None))
    if how == "cols":
        return (slice(None), pl.ds(pl.multiple_of(half * (c // 2), LANES), c // 2))
    return (slice(None), slice(None))


def _half_shape(shape):
    r, c = shape
    return {"rows": (r // 2, c), "cols": (r, c // 2)}[_splits(shape)]


def _remote(src, dst, send_sem, recv_sem, device):
    return pltpu.make_async_remote_copy(src_ref=src, dst_ref=dst, send_sem=send_sem, recv_sem=recv_sem,
                                        device_id=device, device_id_type=MESH)


class Exchange:
    def __init__(self, ins, out_shape, sems, start, finish):
        self.ins, self.out_shape, self.sems, self.start, self.finish = list(ins), list(out_shape), list(sems), start, finish


def gather_group(shards):
    n = len(shards)
    split = [_splits(s.shape) is not None for s in shards]

    def part(w, half):
        return _half(shards[w].shape, half)

    def copies(ins, outs, sems):
        ici_s, ici_r, _, _, local_sems = sems
        x, y, c = _place()
        chip = 2 * x + y
        local = [pltpu.make_async_copy(ins[w], outs[w].at[chip], local_sems.at[w]) for w in range(n)]
        sends = [_remote(ins[w].at[part(w, c)], outs[w].at[(chip,) + part(w, c)], ici_s.at[w, k], ici_r.at[w, k],
                         (x ^ fx, y ^ fy, c))
                 for w in range(n) for k, (fx, fy) in enumerate(CHIP_MASKS)]
        return local, sends

    def start(ins, outs, sems):
        local, sends = copies(ins, outs, sems)
        for cp in local + sends:
            cp.start()

    def finish(ins, outs, sems):
        ici_s, ici_r, d2d_s, d2d_r, _ = sems
        x, y, c = _place()
        sib = (x, y, 1 - c)
        local, sends = copies(ins, outs, sems)
        for w in range(n):
            for k, (fx, fy) in enumerate(CHIP_MASKS):
                landed = outs[w].at[(2 * (x ^ fx) + (y ^ fy),) + part(w, c)]
                _remote(landed, landed, ici_s.at[w, k], ici_r.at[w, k], sib).wait_recv()
                if split[w]:
                    cp = _remote(landed, landed, d2d_s.at[w, k], d2d_r.at[w, k], sib)
                    cp.start()
                    sends.append(cp)
        for w in range(n):
            for k, (fx, fy) in enumerate(CHIP_MASKS):
                if split[w]:
                    other = outs[w].at[(2 * (x ^ fx) + (y ^ fy),) + part(w, 1 - c)]
                    _remote(other, other, d2d_s.at[w, k], d2d_r.at[w, k], sib).wait_recv()
        for cp in sends:
            cp.wait_send()
        for cp in local:
            cp.wait()

    sems = pltpu.SemaphoreType.DMA((n, 3))
    return Exchange(shards, [jax.ShapeDtypeStruct((N_CHIPS,) + s.shape, s.dtype) for s in shards],
                    [sems, sems, sems, sems, pltpu.SemaphoreType.DMA((n,))], start, finish)


def swap_half(parts):
    n = len(parts)

    def copies(ins, outs, sems):
        x, y, c = _place()
        return [_remote(ins[w].at[(slice(None),) + _half(parts[w].shape[1:], 1 - c)], outs[w], sems[0].at[w], sems[1].at[w],
                        (x, y, 1 - c)) for w in range(n)]

    def start(ins, outs, sems):
        for cp in copies(ins, outs, sems):
            cp.start()

    def finish(ins, outs, sems):
        for cp in copies(ins, outs, sems):
            cp.wait_recv()
            cp.wait_send()

    return Exchange(parts, [jax.ShapeDtypeStruct((N_CHIPS,) + _half_shape(p.shape[1:]), F32) for p in parts],
                    [pltpu.SemaphoreType.DMA((n,))] * 2, start, finish)


def scatter_half(halves):
    n = len(halves)

    def copies(ins, outs, sems):
        x, y, c = _place()
        return [_remote(ins[w].at[2 * (x ^ fx) + (y ^ fy)], outs[w].at[k], sems[0].at[w, k], sems[1].at[w, k],
                        (x ^ fx, y ^ fy, c))
                for w in range(n) for k, (fx, fy) in enumerate(CHIP_MASKS)]

    def start(ins, outs, sems):
        for cp in copies(ins, outs, sems):
            cp.start()

    def finish(ins, outs, sems):
        for cp in copies(ins, outs, sems):
            cp.wait_recv()
            cp.wait_send()

    return Exchange(halves, [jax.ShapeDtypeStruct((3,) + h.shape[1:], h.dtype) for h in halves],
                    [pltpu.SemaphoreType.DMA((n, 3))] * 2, start, finish)


def swap_final(finals, shapes):
    n = len(finals)

    def copies(ins, outs, sems):
        x, y, c = _place()
        mine = [outs[w].at[_half(shapes[w], c)] for w in range(n)]
        local = [pltpu.make_async_copy(ins[w], mine[w], sems[2].at[w]) for w in range(n)]
        sends = [_remote(ins[w], mine[w], sems[0].at[w], sems[1].at[w], (x, y, 1 - c)) for w in range(n)]
        return local, sends

    def start(ins, outs, sems):
        local, sends = copies(ins, outs, sems)
        for cp in local + sends:
            cp.start()

    def finish(ins, outs, sems):
        x, y, c = _place()
        local, sends = copies(ins, outs, sems)
        for w in range(n):
            got = outs[w].at[_half(shapes[w], 1 - c)]
            _remote(got, got, sems[0].at[w], sems[1].at[w], (x, y, 1 - c)).wait_recv()
        for cp in sends:
            cp.wait_send()
        for cp in local:
            cp.wait()

    return Exchange(finals, [jax.ShapeDtypeStruct(tuple(s), F32) for s in shapes],
                    [pltpu.SemaphoreType.DMA((n,))] * 3, start, finish)


def exchange_call(name, ex):
    n, m = len(ex.ins), len(ex.out_shape)

    def body(*refs):
        ins, outs, sems = refs[:n], refs[n:n + m], refs[n + m:]
        ex.start(ins, outs, sems)
        ex.finish(ins, outs, sems)

    return pl.pallas_call(body, name=name, in_specs=_any_specs(n), out_specs=_any_specs(m), out_shape=ex.out_shape,
                          scratch_shapes=ex.sems, compiler_params=_params())(*ex.ins)


def _call(body, ex, *, name, grid, in_specs, out_specs, out_shape, args, scratch_shapes=()):
    sem = ("arbitrary",) * len(grid)
    if ex is None:
        outs = pl.pallas_call(body, name=name, grid=grid, in_specs=in_specs, out_specs=out_specs, out_shape=out_shape,
                              scratch_shapes=list(scratch_shapes), compiler_params=_params(sem))(*args)
        return outs, None
    ni, no, ns = len(in_specs), len(out_specs), len(scratch_shapes)
    ne, me = len(ex.ins), len(ex.out_shape)

    def wrapped(*refs):
        own_in, ex_in = refs[:ni], refs[ni:ni + ne]
        own_out, ex_out = refs[ni + ne:ni + ne + no], refs[ni + ne + no:ni + ne + no + me]
        own_scr, ex_sems = refs[ni + ne + no + me:ni + ne + no + me + ns], refs[ni + ne + no + me + ns:]
        ids = [pl.program_id(a) for a in range(len(grid))]
        first = functools.reduce(jnp.logical_and, [i == 0 for i in ids])
        last = functools.reduce(jnp.logical_and, [i == g - 1 for i, g in zip(ids, grid)])

        @pl.when(first)
        def _():
            ex.start(ex_in, ex_out, ex_sems)

        body(*own_in, *own_out, *own_scr)

        @pl.when(last)
        def _():
            ex.finish(ex_in, ex_out, ex_sems)

    outs = pl.pallas_call(
        wrapped, name=name, grid=grid, in_specs=list(in_specs) + _any_specs(ne),
        out_specs=list(out_specs) + _any_specs(me), out_shape=list(out_shape) + ex.out_shape,
        scratch_shapes=list(scratch_shapes) + ex.sems, compiler_params=_params(sem))(*args, *ex.ins)
    return outs[:no], outs[no:]


def _row_tile(rows, cap, mult=8):
    return max([t for t in range(mult, min(rows, cap) + 1, mult) if rows % t == 0] or [rows])


def add_half_call(name, part, got, where):
    _, rh, cols = got.shape
    tr = _row_tile(rh, 176, 16)
    nb = rh // tr
    by_rows = _splits(part.shape[1:]) == "rows"

    def body(where_ref, p_ref, g_ref, own_ref, send_ref):
        t = p_ref[...] + g_ref[...]
        send_ref[...] = t.astype(BF16)
        chip = where_ref[1]
        own_ref[...] = p_ref[chip] + g_ref[chip]

    blk = (N_CHIPS, tr, cols)
    return pl.pallas_call(
        body, name=name,
        grid_spec=pltpu.PrefetchScalarGridSpec(
            num_scalar_prefetch=1, grid=(nb,),
            in_specs=[pl.BlockSpec(blk, (lambda i, where_ref: (0, where_ref[0] * nb + i, 0)) if by_rows
                                   else (lambda i, where_ref: (0, i, where_ref[0]))),
                      pl.BlockSpec(blk, lambda i, where_ref: (0, i, 0))],
            out_specs=[pl.BlockSpec((tr, cols), lambda i, where_ref: (i, 0)),
                       pl.BlockSpec(blk, lambda i, where_ref: (0, i, 0))]),
        out_shape=[jax.ShapeDtypeStruct((rh, cols), F32), jax.ShapeDtypeStruct(got.shape, BF16)],
        compiler_params=_params(("parallel",)),
    )(where, part, got)


def sum_chips_call(name, own, got):
    _, rh, cols = got.shape
    tr = _row_tile(rh, 176, 16)

    def body(h_ref, g_ref, o_ref):
        o_ref[...] = ((h_ref[...] + g_ref[0].astype(F32)) + g_ref[1].astype(F32)) + g_ref[2].astype(F32)

    return pl.pallas_call(
        body, name=name, grid=(rh // tr,),
        in_specs=[pl.BlockSpec((tr, cols), lambda i: (i, 0)), pl.BlockSpec((3, tr, cols), lambda i: (0, i, 0))],
        out_specs=pl.BlockSpec((tr, cols), lambda i: (i, 0)),
        out_shape=jax.ShapeDtypeStruct((rh, cols), F32),
        compiler_params=_params(("parallel",)),
    )(own, got)


def _adamw(w, g, m, v):
    m = ADAM_B1 * m + (1.0 - ADAM_B1) * g
    v = ADAM_B2 * v + (1.0 - ADAM_B2) * (g * g)
    m_hat = m / (1.0 - ADAM_B1 ** ADAM_STEP)
    v_hat = v / (1.0 - ADAM_B2 ** ADAM_STEP)
    delta = -ADAM_LR * (m_hat / (jnp.sqrt(v_hat) + ADAM_EPS) + ADAM_WD * w)
    return delta, m, v


def adamw_call(name, g, w, m, v):
    r, cols = w.shape
    tr = r if r % 8 else _row_tile(r, 256)

    def body(g_ref, w_ref, m_ref, v_ref, go_ref, d_ref, nm_ref, nv_ref):
        g = g_ref[...]
        go_ref[...] = g
        d_ref[...], nm_ref[...], nv_ref[...] = _adamw(w_ref[...], g, m_ref[...], v_ref[...])

    spec = pl.BlockSpec((tr, cols), lambda i: (i, 0))
    return pl.pallas_call(
        body, name=name, grid=(r // tr,), in_specs=[spec] * 4, out_specs=[spec] * 4,
        out_shape=[jax.ShapeDtypeStruct((r, cols), F32)] * 4,
        compiler_params=_params(("parallel",)),
    )(g, w, m, v)


def allsum_small_call(v):
    R = v.shape[0]

    def body(v_ref, out_ref, buf, send_sems, recv_sems):
        x, y, c = _place()
        me = 4 * x + 2 * y + c
        buf[me] = v_ref[...]
        sends = []
        for k in range(1, N_DEV):
            fx, fy, fc = (k >> 2) & 1, (k >> 1) & 1, k & 1
            cp = pltpu.make_async_remote_copy(
                src_ref=v_ref, dst_ref=buf.at[me], send_sem=send_sems.at[k - 1], recv_sem=recv_sems.at[k - 1],
                device_id=(x ^ fx, y ^ fy, c ^ fc), device_id_type=MESH)
            cp.start()
            sends.append(cp)
        for k in range(1, N_DEV):
            pltpu.make_async_remote_copy(
                src_ref=v_ref, dst_ref=buf.at[me ^ k], send_sem=send_sems.at[k - 1], recv_sem=recv_sems.at[k - 1],
                device_id=(x, y, c), device_id_type=MESH).wait_recv()
        acc = buf[0]
        for d in range(1, N_DEV):
            acc = acc + buf[d]
        out_ref[...] = acc
        for cp in sends:
            cp.wait_send()

    vm = pl.BlockSpec(memory_space=pltpu.VMEM)
    return pl.pallas_call(
        body, name="allsum_small", in_specs=[vm], out_specs=vm,
        out_shape=jax.ShapeDtypeStruct((R, LANES), F32),
        scratch_shapes=[pltpu.VMEM((N_DEV, R, LANES), F32), pltpu.SemaphoreType.DMA((N_DEV - 1,)),
                        pltpu.SemaphoreType.DMA((N_DEV - 1,))],
        compiler_params=_params(),
    )(v)


def _slab(flat, mult):
    n = flat.shape[-1]
    rows = -(-n // (LANES * mult)) * mult
    flat = jnp.pad(flat, [(0, 0)] * (flat.ndim - 1) + [(0, rows * LANES - n)])
    return flat.reshape(flat.shape[:-1] + (rows, LANES))


def full_from_chips(blocks, by_col):
    _, r, c = blocks.shape
    return blocks.transpose(1, 0, 2).reshape(r, N_CHIPS * c) if by_col else blocks.reshape(N_CHIPS * r, c)


def chips_from_full(full, by_col):
    if by_col:
        r, c = full.shape[0], full.shape[1] // N_CHIPS
        return full.reshape(r, N_CHIPS, c).transpose(1, 0, 2)
    return full.reshape(N_CHIPS, full.shape[0] // N_CHIPS, full.shape[1])


SMALL_PACK = SMALL_W + ("loss", "conv_w")
SMALL_PACK_N = {**SMALL_N, "loss": 1, "conv_w": 3 * 2 * D_FF}


def pack_small(vals):
    zero = jnp.zeros((1,), F32)
    return _slab(jnp.concatenate([vals[n].reshape(-1) if n in vals else jnp.tile(zero, SMALL_PACK_N[n])
                                  for n in SMALL_PACK]), 8)


def unpack_small(slab, shapes):
    flat = slab.reshape(-1)
    out, off = {}, 0
    for n in SMALL_PACK:
        out[n] = flat[off:off + SMALL_PACK_N[n]].reshape(shapes[n])
        off += SMALL_PACK_N[n]
    return out


def _split_heads(w, a, b):
    r = w.shape[0]
    w3 = w.reshape(r, HEADS, a + b)
    return w3[:, :, :a].reshape(r, HEADS * a), w3[:, :, a:].reshape(r, HEADS * b)


def _merge_heads(wa, wb, a, b):
    r = wa.shape[0]
    return jnp.concatenate([wa.reshape(r, HEADS, a), wb.reshape(r, HEADS, b)], axis=2).reshape(r, HEADS * (a + b))


def kernel(x, positions, g_mix, w_in, g_cq, w_uq, g_ckv, w_ukv, g_sb_out, g_mla_out, w_out, g_ffn, w_up, conv_w, conv_b, w_down, g_final, loss_target, m_g_mix, m_w_in, m_g_cq, m_w_uq, m_g_ckv, m_w_ukv, m_g_sb_out, m_g_mla_out, m_w_out, m_g_ffn, m_w_up, m_conv_w, m_conv_b, m_w_down, m_g_final, v_g_mix, v_w_in, v_g_cq, v_w_uq, v_g_ckv, v_w_ukv, v_g_sb_out, v_g_mla_out, v_w_out, v_g_ffn, v_w_up, v_conv_w, v_conv_b, v_w_down, v_g_final):
    given = dict(locals())
    B, S, _ = x.shape
    T = B * S
    w_big = {n: given[n][0].T if n == "w_in" else given[n][0] for n in BIG_W}
    m_big = {n: given["m_" + n][0].T if n == "w_in" else given["m_" + n][0] for n in BIG_W}
    v_big = {n: given["v_" + n][0].T if n == "w_in" else given["v_" + n][0] for n in BIG_W}
    shard_shape = {n: w_big[n].shape for n in BIG_W}

    first = ("w_in", "w_uq", "w_ukv")
    later = ("w_out", "w_up", "w_down", "conv_w")
    x2d = x.reshape(T, D_MODEL)
    h, got_w = rmsnorm_fwd_call("norm_mix", x2d, g_mix, ex=gather_group([w_big[n].astype(BF16) for n in first]))
    full = {n: full_from_chips(g_, BIG_SHARD[n][2]) for n, g_ in zip(first, got_w) if n != "w_in"}
    gather_later = gather_group([w_big[n] if n == "conv_w" else w_big[n].astype(BF16) for n in later])
    w_in_t = jnp.pad(got_w[0].reshape(IN_COLS, D_MODEL), ((0, IN_COLS_PAD - IN_COLS), (0, 0)))
    w_uq_p = jnp.concatenate(_split_heads(full["w_uq"], HEAD_DIM, ROPE_DIM), axis=1)
    w_ukv_p = jnp.concatenate(_split_heads(full["w_ukv"], HEAD_DIM, HEAD_DIM), axis=1)

    half = ROPE_DIM // 2
    inv_freq = 1.0 / (ROPE_BASE ** (jnp.arange(half, dtype=F32) * (2.0 / ROPE_DIM)))
    cos, sin = rope_tab_call(positions.reshape(T, 1), jnp.tile(inv_freq, LANES // half).reshape(1, LANES))
    p = matmul_call("proj_in", h, w_in_t, "nt", tn=IN_COLS_PAD // 2)
    qn, qr, kn, vm, krt, cqn, ckvn = mla_prep_fwd_call(p, cos, sin, g_cq, g_ckv, w_uq_p, w_ukv_p)
    o_sb, lt_sb, got_w = sb_fwd_call(p, B, S, ex=gather_later)
    w_up4 = got_w[1]
    full.update({n: full_from_chips(g_, BIG_SHARD[n][2]) for n, g_ in zip(later, got_w) if n != "w_up"})
    conv_w_full = full["conv_w"]
    o_mla, lse = mla_fwd_call(qn, qr, kn, krt, vm, B, S)
    o_cat = outnorm_fwd_call(o_sb, o_mla, g_sb_out, g_mla_out)
    x1 = matmul_call("proj_out", o_cat, full["w_out"], "nn", res=x2d)
    hn = rmsnorm_fwd_call("norm_ffn", x1, g_ffn)
    u_g, u_v = ffn_up_call(hn, w_up4)
    act, c_g, c_v = conv_act_fwd_call(u_g, u_v, conv_w_full, conv_b, B, S)
    x2 = matmul_call("ffn_down", act, full["w_down"], "nn", res=x1)
    dx2, dx2b, loss_row, dg_final = final_loss_call(x2, g_final.reshape(1, D_MODEL), loss_target.reshape(T, D_MODEL))

    xi, yi, ci = _place()
    chip = (2 * xi + yi).astype(jnp.int32).reshape(1)
    where = jnp.stack([ci, 2 * xi + yi]).astype(jnp.int32)

    def add_halves(names, parts, sib_rows):
        return [add_half_call("add_half_" + n, p_, s_, where) for n, p_, s_ in zip(names, parts, sib_rows)]

    def sum_chips(names, halves, from_chips):
        return [sum_chips_call("sum_chips_" + n, h_[0], f_) for n, h_, f_ in zip(names, halves, from_chips)]

    ffn_w = ("w_down", "w_up")
    parts_ffn = [chips_from_full(wgrad_call("wgrad_down", act, dx2b, tn=512), False)]
    du_g, du_v, dcw_g, dcw_v, dcb_g, dcb_v = conv_act_bwd_call(
        u_g, u_v, c_g, c_v, dx2b, full["w_down"], conv_w_full, B, S)
    parts_ffn.append(wgrad_up_call(hn, du_g, du_v))
    dhn, sib_ffn = ffn_up_bwd_call(du_g, du_v, w_up4, swap_half(parts_ffn))
    dx1, dg_ffn = rmsnorm_bwd_call("norm_ffn_bwd", x1, g_ffn, dhn, dx2)
    parts_out = [chips_from_full(wgrad_call("wgrad_out", o_cat, dx1), False)]
    do_cat = matmul_call("proj_out_bwd", dx1, full["w_out"], "nt")
    do_sb, do_mla, dg_sb_out, dg_mla_out, sib_out = outnorm_bwd_call(
        o_sb, o_mla, g_sb_out, g_mla_out, do_cat, ex=swap_half(parts_out))
    early = ffn_w + ("w_out",)
    halves = add_halves(early, parts_ffn + parts_out, list(sib_ffn) + list(sib_out))
    dq_sb, dk_sb, dv_sb, from_chips = sb_bwd_call(p, lt_sb, do_sb, B, S, ex=scatter_half([h_[1] for h_ in halves]))
    finals = sum_chips(early, halves, from_chips)
    dqn, dqr4, dkn, dvm, dkrt4, done = mla_bwd_call(qn, qr, kn, krt, vm, o_mla, lse, do_mla, B, S,
        ex=swap_final(finals, [shard_shape[n] for n in early]))
    grads = dict(zip(early, done))
    dcq, dckvr, dq_cat, dkv_cat, dg_cq, dg_ckv = mla_prep_bwd_call(
        p, cos, sin, g_cq, g_ckv, w_uq_p, w_ukv_p, dqn, dqr4, dkn, dvm, dkrt4)
    dw_uq_p = wgrad_call("wgrad_uq", cqn, dq_cat)
    dw_ukv_p = wgrad_call("wgrad_ukv", ckvn, dkv_cat)
    dp = (dq_sb, dk_sb, dv_sb, dcq, dckvr)
    late = ("w_uq", "w_ukv", "w_in")
    parts_late = [chips_from_full(g_, True) for g_ in (
        _merge_heads(dw_uq_p[:, :ATT_W], dw_uq_p[:, ATT_W:], HEAD_DIM, ROPE_DIM),
        _merge_heads(dw_ukv_p[:, :ATT_W], dw_ukv_p[:, ATT_W:], HEAD_DIM, HEAD_DIM))]
    parts_late.append(chips_from_full(wgrad_in_call(h, dp)[:IN_COLS], False))
    dh, sib_late = proj_in_bwd_call(dp, w_in_t, swap_half(parts_late))
    halves = add_halves(late, parts_late, sib_late)
    grad_x, dg_mix, from_chips = rmsnorm_bwd_call(
        "norm_mix_bwd", x2d, g_mix, dh, dx1, ex=scatter_half([h_[1] for h_ in halves]))
    finals = sum_chips(late, halves, from_chips)
    grads.update(zip(late, exchange_call("swap_final_late", swap_final(finals, [shard_shape[n] for n in late]))))

    shapes = {n: given[n].shape for n in SMALL_W}
    shapes.update(loss=(), conv_w=(3, 2 * D_FF))
    small_g = {"g_mix": dg_mix, "g_cq": dg_cq, "g_ckv": dg_ckv, "g_sb_out": dg_sb_out, "g_mla_out": dg_mla_out,
               "g_ffn": dg_ffn, "conv_b": jnp.concatenate([dcb_g, dcb_v], axis=1), "g_final": dg_final,
               "loss": loss_row[0, :1], "conv_w": jnp.concatenate([dcw_g, dcw_v], axis=1)}
    gs_slab = allsum_small_call(pack_small(small_g))
    small_in = [pack_small({n: given[pre + n] for n in SMALL_W}) for pre in ("", "m_", "v_")]
    small_out = [unpack_small(s, shapes) for s in adamw_call("adamw_small", gs_slab, *small_in)]
    cw_cols = BIG_SHARD["conv_w"][1]
    grads["conv_w"] = lax.dynamic_slice_in_dim(small_out[0]["conv_w"], chip[0] * cw_cols, cw_cols, axis=1)

    big_out = {n: adamw_call("adamw_" + n, grads[n], w_big[n], m_big[n], v_big[n]) for n in BIG_W}
    weights = ("g_mix", "w_in", "g_cq", "w_uq", "g_ckv", "w_ukv", "g_sb_out", "g_mla_out", "w_out", "g_ffn",
               "w_up", "conv_w", "conv_b", "w_down", "g_final")
    outs = [small_out[0]["loss"], grad_x.reshape(B, S, D_MODEL)]
    for k in range(4):
        for n in weights:
            if n in BIG_W:
                outs.append((big_out[n][k].T if n == "w_in" else big_out[n][k])[None])
            else:
                outs.append(small_out[k][n])
    return tuple(outs)
```

```python
import functools

import jax
import jax.numpy as jnp
from jax import lax
from jax.experimental import pallas as pl
from jax.experimental.pallas import tpu as pltpu

F32 = jnp.float32
BF16 = jnp.bfloat16
MESH = pl.DeviceIdType.MESH

D_MODEL = 1024
HEADS = 8
HEAD_DIM = 64
ATT_W = HEADS * HEAD_DIM
ROPE_DIM = 32
ROPE_W = HEADS * ROPE_DIM
QK_DIM = HEAD_DIM + ROPE_DIM
Q_RANK = 384
KV_RANK = 256
D_FF = 2816
IN_COLS = 2208
IN_COLS_PAD = 2304
EPS = 1e-6
ROPE_BASE = 10000.0
SB_SCALE = HEAD_DIM ** -0.5
SB_SCALE2 = SB_SCALE * 1.4426950408889634
MLA_SCALE = QK_DIM ** -0.5
LOG2E = 1.4426950408889634
LN2 = 0.6931471805599453
MLA_SCALE2 = MLA_SCALE * LOG2E
LANES = 128
N_CHIPS = 4
N_DEV = 8
VMEM_LIMIT = 48 * 1024 * 1024
ATT_TQ = 256
ATT_TK = 256
ATT_PAIRS = 4
PAIR_LANES = [slice(i * LANES, (i + 1) * LANES) for i in range(ATT_PAIRS)]
SB_BWD_GROUP = 2
SB_FWD_GROUP = 4
MLA_GROUP = 4
NEG_BIG = -1e30

ADAM_LR = 0.001
ADAM_B1 = 0.9
ADAM_B2 = 0.999
ADAM_EPS = 1e-08
ADAM_WD = 0.01
ADAM_STEP = 10

BIG_W = ("w_in", "w_uq", "w_ukv", "w_out", "w_up", "conv_w", "w_down")
BIG_SHARD = {
    "w_in": (D_MODEL, IN_COLS // 4, True),
    "w_uq": (Q_RANK, HEADS * QK_DIM // 4, True),
    "w_ukv": (KV_RANK, 2 * ATT_W // 4, True),
    "w_out": (2 * ATT_W // 4, D_MODEL, False),
    "w_up": (D_MODEL, 2 * D_FF // 4, True),
    "conv_w": (3, 2 * D_FF // 4, True),
    "w_down": (D_FF // 4, D_MODEL, False),
}
SMALL_W = ("g_mix", "g_cq", "g_ckv", "g_sb_out", "g_mla_out", "g_ffn", "conv_b", "g_final")
SMALL_N = {"g_mix": D_MODEL, "g_cq": Q_RANK, "g_ckv": KV_RANK, "g_sb_out": ATT_W, "g_mla_out": ATT_W,
           "g_ffn": D_MODEL, "conv_b": 2 * D_FF, "g_final": D_MODEL}


def _params(sem=None, **kw):
    return pltpu.CompilerParams(dimension_semantics=sem, vmem_limit_bytes=VMEM_LIMIT, **kw)


def _dot(a, b, dims):
    return lax.dot_general(a, b, (dims, ((), ())), preferred_element_type=F32)


def _nn(a, b):
    return _dot(a, b, ((1,), (0,)))


def _nt(a, b):
    return _dot(a, b, ((1,), (1,)))


def _tn(a, b):
    return _dot(a, b, ((0,), (0,)))


def _split2(x):
    hi = x.astype(BF16)
    lo = (x - hi.astype(F32)).astype(BF16)
    return hi, lo


def _split3(x):
    hi = x.astype(BF16)
    r1 = x - hi.astype(F32)
    mid = r1.astype(BF16)
    return hi, mid, (r1 - mid.astype(F32)).astype(BF16)


def _rms_r(x, d):
    return lax.rsqrt(jnp.sum(x * x, axis=-1, keepdims=True) * (1.0 / d) + EPS)


def _rms_bwd(x, g, dy, d):
    r = _rms_r(x, d)
    xhat = x * r
    gy = dy * g
    dx = r * (gy - xhat * (jnp.sum(xhat * gy, axis=-1, keepdims=True) * (1.0 / d)))
    return dx, jnp.sum(dy * xhat, axis=0, keepdims=True)


def _rot(x):
    lane = lax.broadcasted_iota(jnp.int32, x.shape, x.ndim - 1)
    n = x.shape[-1]
    return jnp.where((lane & 31) < 16, pltpu.roll(x, n - 16, x.ndim - 1), pltpu.roll(x, 16, x.ndim - 1))


def _fold4(x):
    return x + pltpu.roll(x, 32, 1) + pltpu.roll(x, 64, 1) + pltpu.roll(x, 96, 1)


def matmul_call(name, a, b, mode, out_dtype=F32, res=None, tm=512, tn=None, ex=None):
    M, K = a.shape
    N = b.shape[1] if mode == "nn" else b.shape[0]
    tn = N if tn is None else tn
    assert M % tm == 0 and N % tn == 0

    def body(*refs):
        if res is None:
            a_ref, b_ref, o_ref = refs
        else:
            a_ref, b_ref, r_ref, o_ref = refs
        av = a_ref[...].astype(BF16)
        bv = b_ref[...].astype(BF16)
        acc = _nn(av, bv) if mode == "nn" else _nt(av, bv)
        if res is not None:
            acc = r_ref[...] + acc
        o_ref[...] = acc.astype(out_dtype)

    in_specs = [pl.BlockSpec((tm, K), lambda j, i: (i, 0))]
    if mode == "nn":
        in_specs.append(pl.BlockSpec((K, tn), lambda j, i: (0, j)))
    else:
        in_specs.append(pl.BlockSpec((tn, K), lambda j, i: (j, 0)))
    args = [a, b]
    if res is not None:
        in_specs.append(pl.BlockSpec((tm, tn), lambda j, i: (i, j)))
        args.append(res)
    outs, moved = _call(body, ex, name=name, grid=(N // tn, M // tm), in_specs=in_specs,
                        out_specs=[pl.BlockSpec((tm, tn), lambda j, i: (i, j))],
                        out_shape=[jax.ShapeDtypeStruct((M, N), out_dtype)], args=args)
    return outs[0] if ex is None else (outs[0], moved)


def _rows(tm, width):
    return pl.BlockSpec((tm, width), lambda i: (i, 0))


def _whole(a, **kw):
    return pl.BlockSpec(a.shape, lambda i: (0,) * a.ndim, **kw)


def proj_out_norm_call(o_cat, w_out, x, g_ffn, tm=512):
    T, K = o_cat.shape
    N = w_out.shape[1]

    def body(a_ref, w_ref, x_ref, g_ref, x1_ref, hn_ref):
        x1 = x_ref[...] + _nn(a_ref[...], w_ref[...])
        x1_ref[...] = x1
        hn_ref[...] = ((x1 * _rms_r(x1, N)) * g_ref[...]).astype(BF16)

    return pl.pallas_call(
        body, name="proj_out", grid=(T // tm,),
        in_specs=[_rows(tm, K), _whole(w_out), _rows(tm, N), _whole(g_ffn)], out_specs=[_rows(tm, N)] * 2,
        out_shape=[jax.ShapeDtypeStruct((T, N), F32), jax.ShapeDtypeStruct((T, N), BF16)],
        compiler_params=_params(("parallel",)),
    )(o_cat, w_out, x, g_ffn)


def ffn_down_loss_call(act, w_down, x1, g, target, tm=512):
    T, K = act.shape
    d = w_down.shape[1]

    def body(a_ref, w_ref, x1_ref, g_ref, t_ref, dx_ref, dxb_ref, loss_ref, dg_ref):
        @pl.when(pl.program_id(0) == 0)
        def _():
            loss_ref[...] = jnp.zeros_like(loss_ref)
            dg_ref[...] = jnp.zeros_like(dg_ref)

        x = x1_ref[...] + _nn(a_ref[...], w_ref[...])
        g = g_ref[...]
        y = (x * _rms_r(x, d)) * g
        err = y - t_ref[...]
        loss_ref[...] += jnp.sum(jnp.sum(err * err, axis=1, keepdims=True), axis=0, keepdims=True) * (0.5 / d)
        dx, dg = _rms_bwd(x, g, err * (1.0 / d), d)
        dx_ref[...] = dx
        dxb_ref[...] = dx.astype(BF16)
        dg_ref[...] += dg

    return pl.pallas_call(
        body, name="ffn_down_loss", grid=(T // tm,),
        in_specs=[_rows(tm, K), _whole(w_down), _rows(tm, d), _whole(g), _rows(tm, d)],
        out_specs=[_rows(tm, d), _rows(tm, d), pl.BlockSpec((1, LANES), lambda i: (0, 0)), _whole(g)],
        out_shape=[jax.ShapeDtypeStruct((T, d), F32), jax.ShapeDtypeStruct((T, d), BF16),
                   jax.ShapeDtypeStruct((1, LANES), F32), jax.ShapeDtypeStruct((1, d), F32)],
        compiler_params=_params(("arbitrary",)),
    )(act, w_down, x1, g, target)


def proj_out_bwd_call(dx1, w_out, o_sb, o_mla, g_sb, g_mla, ex, tm=512):
    T, N = dx1.shape

    def body(d_ref, w_ref, a_ref, b_ref, ga_ref, gb_ref, da_ref, db_ref, dga_ref, dgb_ref):
        @pl.when(pl.program_id(0) == 0)
        def _():
            dga_ref[...] = jnp.zeros_like(dga_ref)
            dgb_ref[...] = jnp.zeros_like(dgb_ref)

        d = _nt(d_ref[...].astype(BF16), w_ref[...])
        da, dga = _rms_bwd(a_ref[...], ga_ref[...], d[:, :ATT_W], ATT_W)
        db, dgb = _rms_bwd(b_ref[...], gb_ref[...], d[:, ATT_W:], ATT_W)
        da_ref[...] = da
        db_ref[...] = db
        dga_ref[...] += dga
        dgb_ref[...] += dgb

    outs, moved = _call(
        body, ex, name="proj_out_bwd", grid=(T // tm,),
        in_specs=[_rows(tm, N), _whole(w_out), _rows(tm, ATT_W), _rows(tm, ATT_W), _whole(g_sb), _whole(g_mla)],
        out_specs=[_rows(tm, ATT_W), _rows(tm, ATT_W), _whole(g_sb), _whole(g_mla)],
        out_shape=[jax.ShapeDtypeStruct((T, ATT_W), F32), jax.ShapeDtypeStruct((T, ATT_W), F32),
                   jax.ShapeDtypeStruct((1, ATT_W), F32), jax.ShapeDtypeStruct((1, ATT_W), F32)],
        args=(dx1, w_out, o_sb, o_mla, g_sb, g_mla))
    return tuple(outs) + (moved,)


def wgrad_call(name, a, b, tn=None, tt=512, by_chip=False):
    T, M = a.shape
    N = b.shape[1]
    tn = N if tn is None else tn
    assert T % tt == 0 and N % tn == 0
    if by_chip:
        out_spec = pl.BlockSpec((None, M, tn), lambda j, t: (j, 0, 0))
        out_shape = jax.ShapeDtypeStruct((N // tn, M, tn), F32)
    else:
        out_spec = pl.BlockSpec((M, tn), lambda j, t: (0, j))
        out_shape = jax.ShapeDtypeStruct((M, N), F32)

    def body(a_ref, b_ref, o_ref):
        @pl.when(pl.program_id(1) == 0)
        def _():
            o_ref[...] = jnp.zeros_like(o_ref)

        o_ref[...] += _tn(a_ref[...].astype(BF16), b_ref[...].astype(BF16))

    return pl.pallas_call(
        body, name=name, grid=(N // tn, T // tt),
        in_specs=[pl.BlockSpec((tt, M), lambda j, t: (t, 0)), pl.BlockSpec((tt, tn), lambda j, t: (t, j))],
        out_specs=out_spec, out_shape=out_shape,
        compiler_params=_params(("parallel", "arbitrary")),
    )(a, b)


UP_COLS = 2 * D_FF // N_CHIPS


def ffn_up_call(hn, w4, tm=512):
    T, K = hn.shape

    def body(a_ref, wg_ref, wv_ref, ug_ref, uv_ref):
        a = a_ref[...]
        ug_ref[...] = _nn(a, wg_ref[...])
        uv_ref[...] = _nn(a, wv_ref[...])

    out = pl.BlockSpec((tm, UP_COLS), lambda j, i: (i, j))
    return pl.pallas_call(
        body, name="ffn_up", grid=(2, T // tm),
        in_specs=[pl.BlockSpec((tm, K), lambda j, i: (i, 0)),
                  pl.BlockSpec((None, K, UP_COLS), lambda j, i: (j, 0, 0)),
                  pl.BlockSpec((None, K, UP_COLS), lambda j, i: (2 + j, 0, 0))],
        out_specs=[out, out], out_shape=[jax.ShapeDtypeStruct((T, D_FF), F32)] * 2,
        compiler_params=_params(("parallel", "parallel")),
    )(hn, w4, w4)


def ffn_up_bwd_call(du_g, du_v, w4, x1, g_ffn, dx2, ex, tm=512):
    T = du_g.shape[0]
    N = w4.shape[1]

    def body(g_ref, v_ref, w_ref, x1_ref, gf_ref, dx2_ref, dx1_ref, dg_ref):
        @pl.when(pl.program_id(0) == 0)
        def _():
            dg_ref[...] = jnp.zeros_like(dg_ref)

        dhn = _nt(g_ref[:, :UP_COLS], w_ref[0]) + _nt(g_ref[:, UP_COLS:], w_ref[1])
        dhn = dhn + _nt(v_ref[:, :UP_COLS], w_ref[2]) + _nt(v_ref[:, UP_COLS:], w_ref[3])
        dx, dg = _rms_bwd(x1_ref[...], gf_ref[...], dhn, N)
        dx1_ref[...] = dx2_ref[...] + dx
        dg_ref[...] += dg

    outs, moved = _call(
        body, ex, name="ffn_up_bwd", grid=(T // tm,),
        in_specs=[_rows(tm, D_FF), _rows(tm, D_FF), _whole(w4, pipeline_mode=pl.Buffered(1)), _rows(tm, N),
                  _whole(g_ffn), _rows(tm, N)],
        out_specs=[_rows(tm, N), _whole(g_ffn)],
        out_shape=[jax.ShapeDtypeStruct((T, N), F32), jax.ShapeDtypeStruct((1, N), F32)],
        args=(du_g, du_v, w4, x1, g_ffn, dx2))
    return outs[0], outs[1], moved


def wgrad_up_call(hn, du_g, du_v, tt=512):
    T, M = hn.shape

    def body(a_ref, g_ref, v_ref, o_ref):
        @pl.when(pl.program_id(1) == 0)
        def _():
            o_ref[...] = jnp.zeros_like(o_ref)

        a = a_ref[...]
        o_ref[0] += _tn(a, g_ref[...])
        o_ref[1] += _tn(a, v_ref[...])

    col = pl.BlockSpec((tt, UP_COLS), lambda j, t: (t, j))
    out = pl.pallas_call(
        body, name="wgrad_up", grid=(2, T // tt),
        in_specs=[pl.BlockSpec((tt, M), lambda j, t: (t, 0)), col, col],
        out_specs=pl.BlockSpec((2, None, M, UP_COLS), lambda j, t: (0, j, 0, 0)),
        out_shape=jax.ShapeDtypeStruct((2, 2, M, UP_COLS), F32),
        compiler_params=_params(("parallel", "arbitrary")),
    )(hn, du_g, du_v)
    return out.reshape(N_CHIPS, M, UP_COLS)


IN_PIECES = ((0, ATT_W), (ATT_W, ATT_W), (2 * ATT_W, ATT_W), (3 * ATT_W, Q_RANK), (3 * ATT_W + Q_RANK, Q_RANK))


def proj_in_bwd_call(pieces, w_in_t, ex, tm=512):
    T = pieces[0].shape[0]
    N = w_in_t.shape[1]
    n = len(pieces)

    def body(*refs):
        o_ref = refs[2 * n]
        acc = _nn(refs[0][...].astype(BF16), refs[n][...])
        for i in range(1, n):
            acc = acc + _nn(refs[i][...].astype(BF16), refs[n + i][...])
        o_ref[...] = acc

    outs, moved = _call(
        body, ex, name="proj_in_bwd", grid=(T // tm,),
        in_specs=[pl.BlockSpec((tm, w), lambda i: (i, 0)) for _, w in IN_PIECES]
        + [pl.BlockSpec((w, N), functools.partial(lambda c, i: (c, 0), off // w)) for off, w in IN_PIECES],
        out_specs=[pl.BlockSpec((tm, N), lambda i: (i, 0))],
        out_shape=[jax.ShapeDtypeStruct((T, N), F32)], args=tuple(pieces) + (w_in_t,) * n)
    return outs[0], moved


def wgrad_in_call(h, pieces, tt=512):
    T, M = h.shape
    n = len(pieces)

    def body(*refs):
        a_ref, o_ref = refs[0], refs[n + 1]

        @pl.when(pl.program_id(0) == 0)
        def _():
            o_ref[...] = jnp.zeros_like(o_ref)

        a = a_ref[...]
        for i, (off, w) in enumerate(IN_PIECES):
            o_ref[off:off + w, :] += _tn(refs[1 + i][...].astype(BF16), a)

    return pl.pallas_call(
        body, name="wgrad_in", grid=(T // tt,),
        in_specs=[pl.BlockSpec((tt, M), lambda t: (t, 0))] + [pl.BlockSpec((tt, w), lambda t: (t, 0)) for _, w in IN_PIECES],
        out_specs=pl.BlockSpec((IN_COLS_PAD, M), lambda t: (0, 0)),
        out_shape=jax.ShapeDtypeStruct((IN_COLS_PAD, M), F32),
        compiler_params=_params(("arbitrary",)),
    )(h, *pieces)


def rmsnorm_fwd_call(name, x, g, tm=512, ex=None):
    T, d = x.shape

    def body(x_ref, g_ref, o_ref):
        x = x_ref[...]
        o_ref[...] = ((x * _rms_r(x, d)) * g_ref[...]).astype(BF16)

    row = pl.BlockSpec((tm, d), lambda i: (i, 0))
    outs, moved = _call(body, ex, name=name, grid=(T // tm,), in_specs=[row, pl.BlockSpec((1, d), lambda i: (0, 0))],
                        out_specs=[row], out_shape=[jax.ShapeDtypeStruct((T, d), BF16)], args=(x, g))
    return outs[0] if ex is None else (outs[0], moved)


def rmsnorm_bwd_call(name, x, g, dy, res, tm=512, ex=None):
    T, d = x.shape

    def body(x_ref, g_ref, dy_ref, r_ref, dx_ref, dg_ref):
        @pl.when(pl.program_id(0) == 0)
        def _():
            dg_ref[...] = jnp.zeros_like(dg_ref)

        dx, dg = _rms_bwd(x_ref[...], g_ref[...], dy_ref[...], d)
        dx_ref[...] = r_ref[...] + dx
        dg_ref[...] += dg

    row = pl.BlockSpec((tm, d), lambda i: (i, 0))
    vec = pl.BlockSpec((1, d), lambda i: (0, 0))
    outs, moved = _call(body, ex, name=name, grid=(T // tm,), in_specs=[row, vec, row, row], out_specs=[row, vec],
                        out_shape=[jax.ShapeDtypeStruct((T, d), F32), jax.ShapeDtypeStruct((1, d), F32)],
                        args=(x, g, dy, res))
    return tuple(outs) if ex is None else tuple(outs) + (moved,)


def outnorm_fwd_call(o_sb, o_mla, g_sb, g_mla, tm=512):
    T = o_sb.shape[0]

    def body(a_ref, b_ref, ga_ref, gb_ref, o_ref):
        a = a_ref[...]
        b = b_ref[...]
        ya = (a * _rms_r(a, ATT_W)) * ga_ref[...]
        yb = (b * _rms_r(b, ATT_W)) * gb_ref[...]
        o_ref[...] = jnp.concatenate([ya, yb], axis=1).astype(BF16)

    row = pl.BlockSpec((tm, ATT_W), lambda i: (i, 0))
    vec = pl.BlockSpec((1, ATT_W), lambda i: (0, 0))
    return pl.pallas_call(
        body, name="outnorm_fwd", grid=(T // tm,), in_specs=[row, row, vec, vec],
        out_specs=pl.BlockSpec((tm, 2 * ATT_W), lambda i: (i, 0)),
        out_shape=jax.ShapeDtypeStruct((T, 2 * ATT_W), BF16),
        compiler_params=_params(("parallel",)),
    )(o_sb, o_mla, g_sb, g_mla)


def outnorm_bwd_call(o_sb, o_mla, g_sb, g_mla, do_cat, tm=512, ex=None):
    T = o_sb.shape[0]

    def body(a_ref, b_ref, ga_ref, gb_ref, d_ref, da_ref, db_ref, dga_ref, dgb_ref):
        @pl.when(pl.program_id(0) == 0)
        def _():
            dga_ref[...] = jnp.zeros_like(dga_ref)
            dgb_ref[...] = jnp.zeros_like(dgb_ref)

        d = d_ref[...]
        da, dga = _rms_bwd(a_ref[...], ga_ref[...], d[:, :ATT_W], ATT_W)
        db, dgb = _rms_bwd(b_ref[...], gb_ref[...], d[:, ATT_W:], ATT_W)
        da_ref[...] = da
        db_ref[...] = db
        dga_ref[...] += dga
        dgb_ref[...] += dgb

    row = pl.BlockSpec((tm, ATT_W), lambda i: (i, 0))
    vec = pl.BlockSpec((1, ATT_W), lambda i: (0, 0))
    outs, moved = _call(
        body, ex, name="outnorm_bwd", grid=(T // tm,),
        in_specs=[row, row, vec, vec, pl.BlockSpec((tm, 2 * ATT_W), lambda i: (i, 0))],
        out_specs=[row, row, vec, vec],
        out_shape=[jax.ShapeDtypeStruct((T, ATT_W), F32), jax.ShapeDtypeStruct((T, ATT_W), F32),
                   jax.ShapeDtypeStruct((1, ATT_W), F32), jax.ShapeDtypeStruct((1, ATT_W), F32)],
        args=(o_sb, o_mla, g_sb, g_mla, do_cat))
    return tuple(outs) if ex is None else tuple(outs) + (moved,)


def final_loss_call(x2, g, target, tm=512):
    T, d = x2.shape

    def body(x_ref, g_ref, t_ref, dx_ref, dxb_ref, loss_ref, dg_ref):
        @pl.when(pl.program_id(0) == 0)
        def _():
            loss_ref[...] = jnp.zeros_like(loss_ref)
            dg_ref[...] = jnp.zeros_like(dg_ref)

        x = x_ref[...]
        g = g_ref[...]
        y = (x * _rms_r(x, d)) * g
        err = y - t_ref[...]
        loss_ref[...] += jnp.sum(jnp.sum(err * err, axis=1, keepdims=True), axis=0, keepdims=True) * (0.5 / d)
        dx, dg = _rms_bwd(x, g, err * (1.0 / d), d)
        dx_ref[...] = dx
        dxb_ref[...] = dx.astype(BF16)
        dg_ref[...] += dg

    row = pl.BlockSpec((tm, d), lambda i: (i, 0))
    vec = pl.BlockSpec((1, d), lambda i: (0, 0))
    return pl.pallas_call(
        body, name="final_loss", grid=(T // tm,), in_specs=[row, vec, row],
        out_specs=[row, row, pl.BlockSpec((1, LANES), lambda i: (0, 0)), vec],
        out_shape=[jax.ShapeDtypeStruct((T, d), F32), jax.ShapeDtypeStruct((T, d), BF16),
                   jax.ShapeDtypeStruct((1, LANES), F32), jax.ShapeDtypeStruct((1, d), F32)],
        compiler_params=_params(("arbitrary",)),
    )(x2, g, target)


def rope_tab_call(pos, inv_freq, tm=512):
    T = pos.shape[0]

    def body(p_ref, f_ref, c_ref, s_ref):
        ang = p_ref[...].astype(F32) * f_ref[...]
        lane = lax.broadcasted_iota(jnp.int32, ang.shape, 1)
        sn = jnp.sin(ang)
        c_ref[...] = jnp.cos(ang)
        s_ref[...] = jnp.where((lane & 31) < 16, -sn, sn)

    row = pl.BlockSpec((tm, LANES), lambda i: (i, 0))
    return pl.pallas_call(
        body, name="rope_tab", grid=(T // tm,),
        in_specs=[pl.BlockSpec((tm, 1), lambda i: (i, 0)), pl.BlockSpec((1, LANES), lambda i: (0, 0))],
        out_specs=[row, row],
        out_shape=[jax.ShapeDtypeStruct((T, LANES), F32)] * 2,
        compiler_params=_params(("parallel",)),
    )(pos, inv_freq)


def mla_prep_fwd_call(p, cos, sin, g_cq, g_ckv, w_uq_p, w_ukv_p, tm=512):
    T = p.shape[0]

    def body(cq_ref, ckvr_ref, c_ref, s_ref, gq_ref, gkv_ref, wq_ref, wkv_ref,
             qn_ref, qr_ref, kn_ref, vm_ref, krt_ref, cqn_ref, ckvn_ref):
        c = c_ref[...]
        s = s_ref[...]
        cq = cq_ref[...]
        cqn = ((cq * _rms_r(cq, Q_RANK)) * gq_ref[...]).astype(BF16)
        cqn_ref[...] = cqn
        q = _nn(cqn, wq_ref[...])
        qn_ref[...] = q[:, :ATT_W].astype(BF16)
        for g in range(ROPE_W // LANES):
            qr = q[:, ATT_W + g * LANES:ATT_W + (g + 1) * LANES]
            qr_ref[:, g * LANES:(g + 1) * LANES] = (qr * c + _rot(qr) * s).astype(BF16)
        ckvr = ckvr_ref[...]
        ckv = ckvr[:, :KV_RANK]
        ckvn = ((ckv * _rms_r(ckv, KV_RANK)) * gkv_ref[...]).astype(BF16)
        ckvn_ref[...] = ckvn
        kv = _nn(ckvn, wkv_ref[...])
        kn_ref[...] = kv[:, :ATT_W].astype(BF16)
        vm_ref[...] = kv[:, ATT_W:].astype(BF16)
        kr = _fold4(ckvr[:, KV_RANK:])
        krt_ref[...] = (kr * c + _rot(kr) * s).astype(BF16)

    def row(w, j=0):
        return pl.BlockSpec((tm, w), lambda i: (i, j))

    def full(a):
        return pl.BlockSpec(a.shape, lambda i: (0, 0))

    return pl.pallas_call(
        body, name="mla_prep_fwd", grid=(T // tm,),
        in_specs=[row(Q_RANK, 4), row(Q_RANK, 5), row(LANES), row(LANES), full(g_cq), full(g_ckv),
                  full(w_uq_p), full(w_ukv_p)],
        out_specs=[row(ATT_W), row(ROPE_W), row(ATT_W), row(ATT_W), row(LANES), row(Q_RANK), row(KV_RANK)],
        out_shape=[jax.ShapeDtypeStruct((T, w), BF16) for w in (ATT_W, ROPE_W, ATT_W, ATT_W, LANES, Q_RANK, KV_RANK)],
        compiler_params=_params(("parallel",)),
    )(p, p, cos, sin, g_cq, g_ckv, w_uq_p, w_ukv_p)


def mla_prep_bwd_call(p, cos, sin, g_cq, g_ckv, w_uq_p, w_ukv_p, dqn, dqr4, dkn, dvm, dkrt4, tm=512):
    T = p.shape[0]

    def body(cq_ref, ckvr_ref, c_ref, s_ref, gq_ref, gkv_ref, wq_ref, wkv_ref,
             dqn_ref, dqr4_ref, dkn_ref, dvm_ref, dkrt4_ref,
             dcq_ref, dckvr_ref, dq_ref, dkv_ref, dgq_ref, dgkv_ref):
        @pl.when(pl.program_id(0) == 0)
        def _():
            dgq_ref[...] = jnp.zeros_like(dgq_ref)
            dgkv_ref[...] = jnp.zeros_like(dgkv_ref)

        c = c_ref[...]
        s = s_ref[...]
        d4 = dqr4_ref[...]
        dqr = [d4[:, :128] + d4[:, 128:256], d4[:, 256:384] + d4[:, 384:]]
        dqr = [t * c + _rot(t * s) for t in dqr]
        dq = jnp.concatenate([dqn_ref[...]] + dqr, axis=1).astype(BF16)
        dq_ref[...] = dq
        dcq, dgq = _rms_bwd(cq_ref[...], gq_ref[...], _nt(dq, wq_ref[...]), Q_RANK)
        dcq_ref[...] = dcq
        dgq_ref[...] += dgq
        dkv = jnp.concatenate([dkn_ref[...], dvm_ref[...]], axis=1).astype(BF16)
        dkv_ref[...] = dkv
        ckvr = ckvr_ref[...]
        dckv, dgkv = _rms_bwd(ckvr[:, :KV_RANK], gkv_ref[...], _nt(dkv, wkv_ref[...]), KV_RANK)
        dgkv_ref[...] += dgkv
        k4 = dkrt4_ref[...]
        dkr = _fold4(k4[:, :128] + k4[:, 128:256] + k4[:, 256:384] + k4[:, 384:])
        dkr = dkr * c + _rot(dkr * s)
        lane = lax.broadcasted_iota(jnp.int32, dkr.shape, 1)
        dckvr_ref[...] = jnp.concatenate([dckv, jnp.where(lane < ROPE_DIM, dkr, 0.0)], axis=1)

    def row(w, j=0):
        return pl.BlockSpec((tm, w), lambda i: (i, j))

    def full(a):
        return pl.BlockSpec(a.shape, lambda i: (0, 0))

    return pl.pallas_call(
        body, name="mla_prep_bwd", grid=(T // tm,),
        in_specs=[row(Q_RANK, 4), row(Q_RANK, 5), row(LANES), row(LANES), full(g_cq), full(g_ckv),
                  full(w_uq_p), full(w_ukv_p), row(ATT_W), row(ATT_W), row(ATT_W), row(ATT_W), row(ATT_W)],
        out_specs=[row(Q_RANK), row(Q_RANK), row(ATT_W + ROPE_W), row(2 * ATT_W),
                   pl.BlockSpec((1, Q_RANK), lambda i: (0, 0)), pl.BlockSpec((1, KV_RANK), lambda i: (0, 0))],
        out_shape=[jax.ShapeDtypeStruct((T, Q_RANK), F32), jax.ShapeDtypeStruct((T, Q_RANK), F32),
                   jax.ShapeDtypeStruct((T, ATT_W + ROPE_W), BF16), jax.ShapeDtypeStruct((T, 2 * ATT_W), BF16),
                   jax.ShapeDtypeStruct((1, Q_RANK), F32), jax.ShapeDtypeStruct((1, KV_RANK), F32)],
        compiler_params=_params(("arbitrary",)),
    )(p, p, cos, sin, g_cq, g_ckv, w_uq_p, w_ukv_p, dqn, dqr4, dkn, dvm, dkrt4)


def _iota2(shape, axis):
    return lax.broadcasted_iota(jnp.int32, shape, axis)


def _head_masks():
    lane = _iota2((1, LANES), 1)
    return lane < HEAD_DIM, lane >= HEAD_DIM


def _pair(x, masks, dtype=BF16):
    return [jnp.where(m, x, 0.0).astype(dtype) for m in masks]


def _log_gates(z):
    keep = jnp.maximum(z, 0.0) + jnp.log2(1.0 + jnp.exp2(-jnp.abs(z)))
    return z - keep, keep


def _last_row(x):
    return _row_of(x[x.shape[0] - 8:, :], 7)


def _lane_selector(group):
    return jnp.where(_iota2((16, LANES), 1) // group == _iota2((16, LANES), 0), 1.0, 0.0).astype(BF16)


def _rows8(sel_t, x):
    hi = x.astype(BF16)
    r1 = x - hi.astype(F32)
    mid = r1.astype(BF16)
    lo = (r1 - mid.astype(F32)).astype(BF16)
    return _nt(sel_t, hi) + _nt(sel_t, mid) + _nt(sel_t, lo)


def _row_of(x8, j):
    return jnp.sum(jnp.where(_iota2(x8.shape, 0) == j, x8, 0.0), axis=0, keepdims=True)


def sb_fwd_call(p, B, S, ex=None):
    T = B * S
    TQ, TK = ATT_TQ, ATT_TK
    nq = S // TQ

    def body(q_ref, k_ref, v_ref, o_ref, lt_ref):
        qi = pl.program_id(2)
        masks = _head_masks()
        qm = [_pair(q_ref[:, sl] * SB_SCALE2, masks) for sl in PAIR_LANES]
        row = _iota2((TQ, TK), 0)
        col = _iota2((TQ, TK), 1)
        tri = jnp.where(row > col, 1.0, 0.0).astype(BF16)
        tri2 = jnp.concatenate([tri, tri], axis=0)
        vis = col < row
        o_ref[...] = jnp.zeros_like(o_ref)

        def group(k0, pairs, carry, diag):
            heads = [(pp, j) for pp in pairs for j in range(2)]
            n = range(len(heads))
            k = {pp: k_ref[pl.ds(k0, TK), PAIR_LANES[pp]].astype(BF16) for pp in pairs}
            vm = {pp: _pair(v_ref[pl.ds(k0, TK), PAIR_LANES[pp]], masks) for pp in pairs}
            gates = [_log_gates(_nt(qm[pp][j], k[pp])) for pp, j in heads]
            lb = [g[0] for g in gates]
            keep = [jnp.where(vis, g[1], 0.0) if diag else g[1] for g in gates]
            tail = [_nn(jnp.concatenate(_split2(keep[h]), axis=1), tri2) + carry[h] for h in n]
            a = [jnp.exp2(lb[h] - tail[h]) for h in n]
            if diag:
                a = [jnp.where(vis, x, 0.0) for x in a]
            ab = [x.astype(BF16) for x in a]
            for i, pp in enumerate(pairs):
                o_ref[:, PAIR_LANES[pp]] += _nn(ab[2 * i], vm[pp][0]) + _nn(ab[2 * i + 1], vm[pp][1])
            return [tail[h][:, 0:1] + keep[h][:, 0:1] for h in n]

        def step(kb, carry, diag):
            k0 = pl.multiple_of(kb * TK, TK)
            out = []
            for g in range(0, ATT_PAIRS, SB_FWD_GROUP):
                out += group(k0, list(range(g, g + SB_FWD_GROUP)), carry[2 * g:2 * (g + SB_FWD_GROUP)], diag)
            return tuple(out)

        zero = jnp.zeros((TQ, 1), F32)
        carry = step(qi, (zero,) * (2 * ATT_PAIRS), True)
        carry = lax.fori_loop(0, qi, lambda i, c: step(qi - 1 - i, c, False), carry)
        lane = _iota2((TQ, LANES), 1)
        for pp, sl in enumerate(PAIR_LANES):
            lt_ref[:, sl] = jnp.where(lane == 0, carry[2 * pp], jnp.where(lane == 1, carry[2 * pp + 1], 0.0))

    W = ATT_PAIRS * LANES
    qspec = pl.BlockSpec((TQ, W), lambda b, h, i: (b * nq + i, h))
    outs, moved = _call(
        body, ex, name="sb_fwd", grid=(B, HEADS // 2 // ATT_PAIRS, nq),
        in_specs=[qspec,
                  pl.BlockSpec((S, W), lambda b, h, i: (b, ATT_W // W + h)),
                  pl.BlockSpec((S, W), lambda b, h, i: (b, 2 * ATT_W // W + h))],
        out_specs=[qspec, qspec],
        out_shape=[jax.ShapeDtypeStruct((T, ATT_W), F32)] * 2, args=(p, p, p))
    return tuple(outs) if ex is None else tuple(outs) + (moved,)


def sb_bwd_call(p, lt, do, B, S, ex=None):
    T = B * S
    TQ, TK = ATT_TQ, ATT_TK
    nq = S // TQ

    def body(q_ref, k_ref, v_ref, lt_ref, do_ref, dq_ref, dk_ref, dv_ref):
        qi = pl.program_id(2)

        @pl.when(qi == 0)
        def _():
            dk_ref[...] = jnp.zeros_like(dk_ref)
            dv_ref[...] = jnp.zeros_like(dv_ref)

        masks = _head_masks()
        qm = [_pair(q_ref[:, sl] * SB_SCALE2, masks) for sl in PAIR_LANES]
        dom = [_pair(do_ref[:, sl], masks) for sl in PAIR_LANES]
        start = []
        for sl in PAIR_LANES:
            l8 = _rows8(_lane_selector(1), lt_ref[:, sl])
            start += [-_row_of(l8, 0), jnp.zeros((1, TQ), F32), -_row_of(l8, 1), jnp.zeros((1, TQ), F32)]
        row = _iota2((TK, TQ), 0)
        col = _iota2((TK, TQ), 1)
        incl = jnp.where(col <= row, 1.0, 0.0).astype(BF16)
        incl2 = jnp.concatenate([incl, incl], axis=1)
        excl = jnp.where(col < row, 1.0, 0.0).astype(BF16)
        vis = row < col
        dq_ref[...] = jnp.zeros_like(dq_ref)

        def group(k0, pairs, carry, diag):
            heads = [(pp, j) for pp in pairs for j in range(2)]
            n = range(len(heads))
            kf = {pp: k_ref[pl.ds(k0, TK), PAIR_LANES[pp]] for pp in pairs}
            km = {pp: _pair(kf[pp], masks) for pp in pairs}
            v = {pp: v_ref[pl.ds(k0, TK), PAIR_LANES[pp]].astype(BF16) for pp in pairs}
            z = [_nt(kf[pp].astype(BF16), qm[pp][j]) for pp, j in heads]
            da = [_nt(v[pp], dom[pp][j]) for pp, j in heads]
            gates = [_log_gates(x) for x in z]
            lb = [g[0] for g in gates]
            keep = [jnp.where(vis, g[1], 0.0) if diag else g[1] for g in gates]
            left = [_nn(incl2, jnp.concatenate(_split2(keep[h]), axis=0)) + carry[2 * h] for h in n]
            a = [jnp.exp2(lb[h] + left[h]) for h in n]
            if diag:
                a = [jnp.where(vis, x, 0.0) for x in a]
            e = [a[h] * da[h] for h in n]
            before = [_nn(excl, e[h].astype(BF16)) + carry[2 * h + 1] for h in n]
            dz = [e[h] - jnp.exp2(lb[h]) * (e[h] + before[h]) for h in n]
            if diag:
                dz = [jnp.where(vis, x, 0.0) for x in dz]
            dzb = [x.astype(BF16) for x in dz]
            ab = [x.astype(BF16) for x in a]
            out = []
            for h in n:
                out += [_last_row(left[h]), _last_row(before[h]) + _last_row(e[h])]
            for i, pp in enumerate(pairs):
                sl = PAIR_LANES[pp]
                dk_ref[pl.ds(k0, TK), sl] += _nn(dzb[2 * i], qm[pp][0]) + _nn(dzb[2 * i + 1], qm[pp][1])
                dv_ref[pl.ds(k0, TK), sl] += _nn(ab[2 * i], dom[pp][0]) + _nn(ab[2 * i + 1], dom[pp][1])
                dq_ref[:, sl] += _tn(dzb[2 * i], km[pp][0]) + _tn(dzb[2 * i + 1], km[pp][1])
            return out

        def step(kb, carry, diag):
            k0 = pl.multiple_of(kb * TK, TK)
            out = []
            for g in range(0, ATT_PAIRS, SB_BWD_GROUP):
                out += group(k0, list(range(g, g + SB_BWD_GROUP)), carry[4 * g:4 * (g + SB_BWD_GROUP)], diag)
            return tuple(out)

        carry = lax.fori_loop(0, qi, lambda i, c: step(i, c, False), tuple(start))
        step(qi, carry, True)
        dq_ref[...] *= SB_SCALE

        @pl.when(qi == nq - 1)
        def _():
            dk_ref[...] *= LN2

    W = ATT_PAIRS * LANES
    qspec = pl.BlockSpec((TQ, W), lambda b, h, i: (b * nq + i, h))
    sspec = pl.BlockSpec((S, W), lambda b, h, i: (b, h))
    outs, moved = _call(
        body, ex, name="sb_bwd", grid=(B, HEADS // 2 // ATT_PAIRS, nq),
        in_specs=[qspec,
                  pl.BlockSpec((S, W), lambda b, h, i: (b, ATT_W // W + h)),
                  pl.BlockSpec((S, W), lambda b, h, i: (b, 2 * ATT_W // W + h)),
                  qspec, qspec],
        out_specs=[qspec, sspec, sspec],
        out_shape=[jax.ShapeDtypeStruct((T, ATT_W), F32)] * 3, args=(p, p, p, lt, do))
    return tuple(outs) if ex is None else tuple(outs) + (moved,)


ALL_PAIRS = [slice(i * LANES, (i + 1) * LANES) for i in range(HEADS // 2)]


def _rope_masks(hp):
    grp = _iota2((1, LANES), 1) // ROPE_DIM
    return [grp == ((2 * hp + j) % 4) for j in range(2)]


def _mla_queries(qn_ref, qr_ref, masks):
    out = []
    for pp, sl in enumerate(ALL_PAIRS):
        qnv = qn_ref[:, sl]
        qrv = qr_ref[:, ALL_PAIRS[pp // 2]]
        rmasks = _rope_masks(pp)
        out.append([jnp.concatenate([jnp.where(masks[j], qnv, 0), jnp.where(rmasks[j], qrv, 0)], axis=1).astype(BF16)
                    for j in range(2)])
    return out


def mla_fwd_call(qn, qr, kn, krt, vm, B, S):
    T = B * S
    TQ, TK = ATT_TQ, ATT_TK
    nq = S // TQ

    def body(qn_ref, qr_ref, kn_ref, kr_ref, v_ref, o_ref, lse_ref):
        qi = pl.program_id(1)
        masks = _head_masks()
        qcat = _mla_queries(qn_ref, qr_ref, masks)
        row = _iota2((TQ, TK), 0)
        col = _iota2((TQ, TK), 1)
        vis = col <= row
        o_ref[...] = jnp.zeros_like(o_ref)

        def group(k0, pairs, carry, diag):
            heads = [(pp, j) for pp in pairs for j in range(2)]
            n = range(len(heads))
            krv = kr_ref[pl.ds(k0, TK), :]
            kcat = {pp: jnp.concatenate([kn_ref[pl.ds(k0, TK), ALL_PAIRS[pp]], krv], axis=1) for pp in pairs}
            vmk = {pp: _pair(v_ref[pl.ds(k0, TK), ALL_PAIRS[pp]], masks) for pp in pairs}
            s = [_nt(qcat[pp][j], kcat[pp]) * MLA_SCALE2 for pp, j in heads]
            if diag:
                s = [jnp.where(vis, x, NEG_BIG) for x in s]
            m_new = [jnp.maximum(carry[2 * h], jnp.max(s[h], axis=1, keepdims=True)) for h in n]
            alpha = [jnp.exp2(carry[2 * h] - m_new[h]) for h in n]
            pexp = [jnp.exp2(s[h] - m_new[h]) for h in n]
            out = []
            for h in n:
                out += [m_new[h], alpha[h] * carry[2 * h + 1] + jnp.sum(pexp[h], axis=1, keepdims=True)]
            pb = [x.astype(BF16) for x in pexp]
            for i, pp in enumerate(pairs):
                sl = ALL_PAIRS[pp]
                scale = jnp.where(masks[0], alpha[2 * i], alpha[2 * i + 1])
                o_ref[:, sl] = o_ref[:, sl] * scale + (_nn(pb[2 * i], vmk[pp][0]) + _nn(pb[2 * i + 1], vmk[pp][1]))
            return out

        def step(kb, carry, diag):
            k0 = pl.multiple_of(kb * TK, TK)
            out = []
            for g in range(0, len(ALL_PAIRS), MLA_GROUP):
                out += group(k0, list(range(g, g + MLA_GROUP)), carry[4 * g:4 * (g + MLA_GROUP)], diag)
            return tuple(out)

        neg = jnp.full((TQ, 1), NEG_BIG, F32)
        zero = jnp.zeros((TQ, 1), F32)
        carry = step(qi, (neg, zero) * (2 * len(ALL_PAIRS)), True)
        carry = lax.fori_loop(0, qi, lambda i, c: step(qi - 1 - i, c, False), carry)
        lane = _iota2((TQ, LANES), 1)
        for pp, sl in enumerate(ALL_PAIRS):
            m0, l0, m1, l1 = carry[4 * pp:4 * pp + 4]
            o_ref[:, sl] = o_ref[:, sl] * jnp.where(masks[0], 1.0 / l0, 1.0 / l1)
            lse_ref[:, sl] = jnp.where(lane == 0, m0 * LN2 + jnp.log(l0), jnp.where(lane == 1, m1 * LN2 + jnp.log(l1), 0.0))

    def rows(w):
        return pl.BlockSpec((TQ, w), lambda b, i: (b * nq + i, 0))

    def seq(w):
        return pl.BlockSpec((S, w), lambda b, i: (b, 0))

    return pl.pallas_call(
        body, name="mla_fwd", grid=(B, nq),
        in_specs=[rows(ATT_W), rows(ROPE_W), seq(ATT_W), seq(LANES), seq(ATT_W)],
        out_specs=[rows(ATT_W), rows(ATT_W)],
        out_shape=[jax.ShapeDtypeStruct((T, ATT_W), F32)] * 2,
        compiler_params=_params(("arbitrary", "arbitrary")),
    )(qn, qr, kn, krt, vm)


def mla_bwd_call(qn, qr, kn, krt, vm, o, lse, do, B, S, ex=None):
    T = B * S
    TQ, TK = ATT_TQ, ATT_TK
    nq = S // TQ

    def body(qn_ref, qr_ref, kn_ref, kr_ref, v_ref, o_ref, lse_ref, do_ref,
             dqn_ref, dqr_ref, dkn_ref, dv_ref, dkr_ref):
        qi = pl.program_id(1)

        @pl.when(qi == 0)
        def _():
            dkn_ref[...] = jnp.zeros_like(dkn_ref)
            dv_ref[...] = jnp.zeros_like(dv_ref)
            dkr_ref[...] = jnp.zeros_like(dkr_ref)

        masks = _head_masks()
        qcat = _mla_queries(qn_ref, qr_ref, masks)
        dom, dsum, lse = [], [], []
        for sl in ALL_PAIRS:
            do = do_ref[:, sl]
            dom.append(_pair(do, masks))
            d8 = _rows8(_lane_selector(HEAD_DIM), do * o_ref[:, sl])
            l8 = _rows8(_lane_selector(1), lse_ref[:, sl])
            dsum.append([_row_of(d8, j) for j in range(2)])
            lse.append([_row_of(l8, j) * LOG2E for j in range(2)])
        row = _iota2((TK, TQ), 0)
        col = _iota2((TK, TQ), 1)
        vis = row <= col
        dqn_ref[...] = jnp.zeros_like(dqn_ref)
        dqr_ref[...] = jnp.zeros_like(dqr_ref)

        def group(k0, pairs, diag):
            heads = [(pp, j) for pp in pairs for j in range(2)]
            n = range(len(heads))
            krv = kr_ref[pl.ds(k0, TK), :]
            knv = {pp: kn_ref[pl.ds(k0, TK), ALL_PAIRS[pp]] for pp in pairs}
            kcat = {pp: jnp.concatenate([knv[pp], krv], axis=1) for pp in pairs}
            v = {pp: v_ref[pl.ds(k0, TK), ALL_PAIRS[pp]] for pp in pairs}
            s = [_nt(kcat[pp], qcat[pp][j]) * MLA_SCALE2 for pp, j in heads]
            dp_ = [_nt(v[pp], dom[pp][j]) for pp, j in heads]
            pr = [jnp.exp2(s[h] - lse[pp][j]) for h, (pp, j) in enumerate(heads)]
            if diag:
                pr = [jnp.where(vis, x, 0.0) for x in pr]
            ds = [(pr[h] * (dp_[h] - dsum[pp][j]) * MLA_SCALE).astype(BF16) for h, (pp, j) in enumerate(heads)]
            pb = [x.astype(BF16) for x in pr]
            for i, pp in enumerate(pairs):
                sl = ALL_PAIRS[pp]
                rmasks = _rope_masks(pp)
                kcat_j = [jnp.concatenate([jnp.where(masks[j], knv[pp], 0), jnp.where(rmasks[j], krv, 0)],
                                          axis=1).astype(BF16) for j in range(2)]
                dv_ref[pl.ds(k0, TK), sl] += _nn(pb[2 * i], dom[pp][0]) + _nn(pb[2 * i + 1], dom[pp][1])
                dk = _nn(ds[2 * i], qcat[pp][0]) + _nn(ds[2 * i + 1], qcat[pp][1])
                dq = _tn(ds[2 * i], kcat_j[0]) + _tn(ds[2 * i + 1], kcat_j[1])
                dqn_ref[:, sl] += dq[:, :LANES]
                dqr_ref[:, sl] += dq[:, LANES:]
                dkn_ref[pl.ds(k0, TK), sl] += dk[:, :LANES]
                dkr_ref[pl.ds(k0, TK), sl] += dk[:, LANES:]

        def step(kb, diag):
            k0 = pl.multiple_of(kb * TK, TK)
            for g in range(0, len(ALL_PAIRS), MLA_GROUP):
                group(k0, list(range(g, g + MLA_GROUP)), diag)

        step(qi, True)

        def loop(i, c):
            step(qi - 1 - i, False)
            return c

        lax.fori_loop(0, qi, loop, 0)

    def rows(w):
        return pl.BlockSpec((TQ, w), lambda b, i: (b * nq + i, 0))

    def seq(w):
        return pl.BlockSpec((S, w), lambda b, i: (b, 0))

    outs, moved = _call(
        body, ex, name="mla_bwd", grid=(B, nq),
        in_specs=[rows(ATT_W), rows(ROPE_W), seq(ATT_W), seq(LANES), seq(ATT_W), rows(ATT_W), rows(ATT_W), rows(ATT_W)],
        out_specs=[rows(ATT_W), rows(ATT_W), seq(ATT_W), seq(ATT_W), seq(ATT_W)],
        out_shape=[jax.ShapeDtypeStruct((T, ATT_W), F32)] * 5, args=(qn, qr, kn, krt, vm, o, lse, do))
    return tuple(outs) if ex is None else tuple(outs) + (moved,)


CONV_TC = 256


def _shift_down(x, n):
    return jnp.where(_iota2(x.shape, 0) >= n, pltpu.roll(x, n, 0), 0.0)


def _shift_up(x, n):
    rows = x.shape[0]
    return jnp.where(_iota2(x.shape, 0) < rows - n, pltpu.roll(x, rows - n, 0), 0.0)


def _taps(w_ref):
    return [w_ref[k:k + 1, :] for k in range(3)]


def _conv3(u, w, b):
    return w[0] * _shift_down(u, 2) + w[1] * _shift_down(u, 1) + w[2] * u + b


def _ref_shift_down(ref, n):
    rows = ref.shape[0]
    return jnp.concatenate([_shift_down(ref[0:8, :], n), ref[8 - n:rows - n, :]], axis=0)


def _conv3_ref(u_ref, w, b):
    return w[0] * _ref_shift_down(u_ref, 2) + w[1] * _ref_shift_down(u_ref, 1) + w[2] * u_ref[...] + b


def conv_act_fwd_call(ug, uv, conv_w, conv_b, B, S):
    T = B * S
    nc = D_FF // CONV_TC

    def body(ug_ref, uv_ref, wg_ref, wv_ref, bg_ref, bv_ref, a_ref, cg_ref, cv_ref):
        gate = _conv3_ref(ug_ref, _taps(wg_ref), bg_ref[...])
        val = _conv3_ref(uv_ref, _taps(wv_ref), bv_ref[...])
        a_ref[...] = (gate * (1.0 / (1.0 + jnp.exp(-gate))) * val).astype(BF16)
        cg_ref[...] = gate.astype(BF16)
        cv_ref[...] = val.astype(BF16)

    def blk(rows, off):
        return pl.BlockSpec((rows, CONV_TC), lambda b, j: (b if rows == S else 0, off + j))

    return pl.pallas_call(
        body, name="conv_act_fwd", grid=(B, nc),
        in_specs=[blk(S, 0), blk(S, 0), blk(3, 0), blk(3, nc), blk(1, 0), blk(1, nc)],
        out_specs=[blk(S, 0)] * 3,
        out_shape=[jax.ShapeDtypeStruct((T, D_FF), BF16)] * 3,
        compiler_params=_params(("parallel", "parallel")),
    )(ug, uv, conv_w, conv_w, conv_b, conv_b)


def conv_act_bwd_call(ug, uv, cg, cv, dx2, w_down, conv_w, B, S):
    T = B * S
    nc = D_FF // CONV_TC

    def body(ug_ref, uv_ref, cg_ref, cv_ref, dx_ref, wd_ref, wg_ref, wv_ref,
             dug_ref, duv_ref, dwg_ref, dwv_ref, dbg_ref, dbv_ref):
        @pl.when(pl.program_id(1) == 0)
        def _():
            for r in (dwg_ref, dwv_ref, dbg_ref, dbv_ref):
                r[...] = jnp.zeros_like(r)

        gate = cg_ref[...].astype(F32)
        val = cv_ref[...].astype(F32)
        da = _nt(dx_ref[...], wd_ref[...])
        sig = 1.0 / (1.0 + jnp.exp(-gate))
        dval = da * (gate * sig)
        dgate = da * val * (sig * (1.0 + gate * (1.0 - sig)))
        for u_ref, d, w, du_ref, dw_ref, db_ref in ((ug_ref, dgate, _taps(wg_ref), dug_ref, dwg_ref, dbg_ref),
                                                   (uv_ref, dval, _taps(wv_ref), duv_ref, dwv_ref, dbv_ref)):
            u_ = u_ref[...]
            d1 = _shift_up(d, 1)
            d2 = _shift_up(d, 2)
            du_ref[...] = (w[2] * d + w[1] * d1 + w[0] * d2).astype(BF16)
            db_ref[...] += jnp.sum(d, axis=0, keepdims=True)
            dw_ref[0:1, :] += jnp.sum(d2 * u_, axis=0, keepdims=True)
            dw_ref[1:2, :] += jnp.sum(d1 * u_, axis=0, keepdims=True)
            dw_ref[2:3, :] += jnp.sum(d * u_, axis=0, keepdims=True)

    def blk(rows, off):
        return pl.BlockSpec((rows, CONV_TC), lambda j, b: (b if rows == S else 0, off + j))

    return pl.pallas_call(
        body, name="conv_act_bwd", grid=(nc, B),
        in_specs=[blk(S, 0), blk(S, 0), blk(S, 0), blk(S, 0), pl.BlockSpec((S, D_MODEL), lambda j, b: (b, 0)),
                  pl.BlockSpec((CONV_TC, D_MODEL), lambda j, b: (j, 0)), blk(3, 0), blk(3, nc)],
        out_specs=[blk(S, 0), blk(S, 0), blk(3, 0), blk(3, 0), blk(1, 0), blk(1, 0)],
        out_shape=[jax.ShapeDtypeStruct((T, D_FF), BF16), jax.ShapeDtypeStruct((T, D_FF), BF16),
                   jax.ShapeDtypeStruct((3, D_FF), F32), jax.ShapeDtypeStruct((3, D_FF), F32),
                   jax.ShapeDtypeStruct((1, D_FF), F32), jax.ShapeDtypeStruct((1, D_FF), F32)],
        compiler_params=_params(("parallel", "arbitrary")),
    )(ug, uv, cg, cv, dx2, w_down, conv_w, conv_w)


CHIP_MASKS = ((1, 0), (0, 1), (1, 1))


def _place():
    return lax.axis_index("x"), lax.axis_index("y"), lax.axis_index("c")


HALF_ALIGN = 32


def _any_specs(n):
    return [pl.BlockSpec(memory_space=pl.ANY)] * n


def _splits(shape):
    r, c = shape
    return "rows" if r % HALF_ALIGN == 0 else "cols" if c % (2 * LANES) == 0 else None


def _half(shape, half):
    r, c = shape
    how = _splits(shape)
    if how == "rows":
        return (pl.ds(pl.multiple_of(half * (r // 2), HALF_ALIGN // 2), r // 2), slice(None))
    if how == "cols":
        return (slice(None), pl.ds(pl.multiple_of(half * (c // 2), LANES), c // 2))
    return (slice(None), slice(None))


def _half_shape(shape):
    r, c = shape
    return {"rows": (r // 2, c), "cols": (r, c // 2)}[_splits(shape)]


def _remote(src, dst, send_sem, recv_sem, device):
    return pltpu.make_async_remote_copy(src_ref=src, dst_ref=dst, send_sem=send_sem, recv_sem=recv_sem,
                                        device_id=device, device_id_type=MESH)


class Exchange:
    def __init__(self, ins, out_shape, sems, start, finish):
        self.ins, self.out_shape, self.sems, self.start, self.finish = list(ins), list(out_shape), list(sems), start, finish


def gather_group(shards):
    n = len(shards)
    split = [_splits(s.shape) is not None for s in shards]

    def part(w, half):
        return _half(shards[w].shape, half)

    def copies(ins, outs, sems):
        ici_s, ici_r, _, _, local_sems = sems
        x, y, c = _place()
        chip = 2 * x + y
        local = [pltpu.make_async_copy(ins[w], outs[w].at[chip], local_sems.at[w]) for w in range(n)]
        sends = [_remote(ins[w].at[part(w, c)], outs[w].at[(chip,) + part(w, c)], ici_s.at[w, k], ici_r.at[w, k],
                         (x ^ fx, y ^ fy, c))
                 for w in range(n) for k, (fx, fy) in enumerate(CHIP_MASKS)]
        return local, sends

    def start(ins, outs, sems):
        local, sends = copies(ins, outs, sems)
        for cp in local + sends:
            cp.start()

    def finish(ins, outs, sems):
        ici_s, ici_r, d2d_s, d2d_r, _ = sems
        x, y, c = _place()
        sib = (x, y, 1 - c)
        local, sends = copies(ins, outs, sems)
        for w in range(n):
            for k, (fx, fy) in enumerate(CHIP_MASKS):
                landed = outs[w].at[(2 * (x ^ fx) + (y ^ fy),) + part(w, c)]
                _remote(landed, landed, ici_s.at[w, k], ici_r.at[w, k], sib).wait_recv()
                if split[w]:
                    cp = _remote(landed, landed, d2d_s.at[w, k], d2d_r.at[w, k], sib)
                    cp.start()
                    sends.append(cp)
        for w in range(n):
            for k, (fx, fy) in enumerate(CHIP_MASKS):
                if split[w]:
                    other = outs[w].at[(2 * (x ^ fx) + (y ^ fy),) + part(w, 1 - c)]
                    _remote(other, other, d2d_s.at[w, k], d2d_r.at[w, k], sib).wait_recv()
        for cp in sends:
            cp.wait_send()
        for cp in local:
            cp.wait()

    sems = pltpu.SemaphoreType.DMA((n, 3))
    return Exchange(shards, [jax.ShapeDtypeStruct((N_CHIPS,) + s.shape, s.dtype) for s in shards],
                    [sems, sems, sems, sems, pltpu.SemaphoreType.DMA((n,))], start, finish)


def swap_half(parts):
    n = len(parts)

    def copies(ins, outs, sems):
        x, y, c = _place()
        return [_remote(ins[w].at[(slice(None),) + _half(parts[w].shape[1:], 1 - c)], outs[w], sems[0].at[w], sems[1].at[w],
                        (x, y, 1 - c)) for w in range(n)]

    def start(ins, outs, sems):
        for cp in copies(ins, outs, sems):
            cp.start()

    def finish(ins, outs, sems):
        for cp in copies(ins, outs, sems):
            cp.wait_recv()
            cp.wait_send()

    return Exchange(parts, [jax.ShapeDtypeStruct((N_CHIPS,) + _half_shape(p.shape[1:]), F32) for p in parts],
                    [pltpu.SemaphoreType.DMA((n,))] * 2, start, finish)


def scatter_half(halves):
    n = len(halves)

    def copies(ins, outs, sems):
        x, y, c = _place()
        return [_remote(ins[w].at[2 * (x ^ fx) + (y ^ fy)], outs[w].at[k], sems[0].at[w, k], sems[1].at[w, k],
                        (x ^ fx, y ^ fy, c))
                for w in range(n) for k, (fx, fy) in enumerate(CHIP_MASKS)]

    def start(ins, outs, sems):
        for cp in copies(ins, outs, sems):
            cp.start()

    def finish(ins, outs, sems):
        for cp in copies(ins, outs, sems):
            cp.wait_recv()
            cp.wait_send()

    return Exchange(halves, [jax.ShapeDtypeStruct((3,) + h.shape[1:], h.dtype) for h in halves],
                    [pltpu.SemaphoreType.DMA((n, 3))] * 2, start, finish)


def swap_final(finals, shapes):
    n = len(finals)

    def copies(ins, outs, sems):
        x, y, c = _place()
        mine = [outs[w].at[_half(shapes[w], c)] for w in range(n)]
        local = [pltpu.make_async_copy(ins[w], mine[w], sems[2].at[w]) for w in range(n)]
        sends = [_remote(ins[w], mine[w], sems[0].at[w], sems[1].at[w], (x, y, 1 - c)) for w in range(n)]
        return local, sends

    def start(ins, outs, sems):
        local, sends = copies(ins, outs, sems)
        for cp in local + sends:
            cp.start()

    def finish(ins, outs, sems):
        x, y, c = _place()
        local, sends = copies(ins, outs, sems)
        for w in range(n):
            got = outs[w].at[_half(shapes[w], 1 - c)]
            _remote(got, got, sems[0].at[w], sems[1].at[w], (x, y, 1 - c)).wait_recv()
        for cp in sends:
            cp.wait_send()
        for cp in local:
            cp.wait()

    return Exchange(finals, [jax.ShapeDtypeStruct(tuple(s), F32) for s in shapes],
                    [pltpu.SemaphoreType.DMA((n,))] * 3, start, finish)


def exchange_call(name, ex):
    n, m = len(ex.ins), len(ex.out_shape)

    def body(*refs):
        ins, outs, sems = refs[:n], refs[n:n + m], refs[n + m:]
        ex.start(ins, outs, sems)
        ex.finish(ins, outs, sems)

    return pl.pallas_call(body, name=name, in_specs=_any_specs(n), out_specs=_any_specs(m), out_shape=ex.out_shape,
                          scratch_shapes=ex.sems, compiler_params=_params())(*ex.ins)


def _call(body, ex, *, name, grid, in_specs, out_specs, out_shape, args, scratch_shapes=()):
    sem = ("arbitrary",) * len(grid)
    if ex is None:
        outs = pl.pallas_call(body, name=name, grid=grid, in_specs=in_specs, out_specs=out_specs, out_shape=out_shape,
                              scratch_shapes=list(scratch_shapes), compiler_params=_params(sem))(*args)
        return outs, None
    ni, no, ns = len(in_specs), len(out_specs), len(scratch_shapes)
    ne, me = len(ex.ins), len(ex.out_shape)

    def wrapped(*refs):
        own_in, ex_in = refs[:ni], refs[ni:ni + ne]
        own_out, ex_out = refs[ni + ne:ni + ne + no], refs[ni + ne + no:ni + ne + no + me]
        own_scr, ex_sems = refs[ni + ne + no + me:ni + ne + no + me + ns], refs[ni + ne + no + me + ns:]
        ids = [pl.program_id(a) for a in range(len(grid))]
        first = functools.reduce(jnp.logical_and, [i == 0 for i in ids])
        last = functools.reduce(jnp.logical_and, [i == g - 1 for i, g in zip(ids, grid)])

        @pl.when(first)
        def _():
            ex.start(ex_in, ex_out, ex_sems)

        body(*own_in, *own_out, *own_scr)

        @pl.when(last)
        def _():
            ex.finish(ex_in, ex_out, ex_sems)

    outs = pl.pallas_call(
        wrapped, name=name, grid=grid, in_specs=list(in_specs) + _any_specs(ne),
        out_specs=list(out_specs) + _any_specs(me), out_shape=list(out_shape) + ex.out_shape,
        scratch_shapes=list(scratch_shapes) + ex.sems, compiler_params=_params(sem))(*args, *ex.ins)
    return outs[:no], outs[no:]


def _row_tile(rows, cap, mult=8):
    return max([t for t in range(mult, min(rows, cap) + 1, mult) if rows % t == 0] or [rows])


def add_half_call(name, part, got, where):
    _, rh, cols = got.shape
    tr = _row_tile(rh, 176, 16)
    nb = rh // tr
    by_rows = _splits(part.shape[1:]) == "rows"

    def body(where_ref, p_ref, g_ref, own_ref, send_ref):
        t = p_ref[...] + g_ref[...]
        send_ref[...] = t.astype(BF16)
        chip = where_ref[1]
        own_ref[...] = p_ref[chip] + g_ref[chip]

    blk = (N_CHIPS, tr, cols)
    return pl.pallas_call(
        body, name=name,
        grid_spec=pltpu.PrefetchScalarGridSpec(
            num_scalar_prefetch=1, grid=(nb,),
            in_specs=[pl.BlockSpec(blk, (lambda i, where_ref: (0, where_ref[0] * nb + i, 0)) if by_rows
                                   else (lambda i, where_ref: (0, i, where_ref[0]))),
                      pl.BlockSpec(blk, lambda i, where_ref: (0, i, 0))],
            out_specs=[pl.BlockSpec((tr, cols), lambda i, where_ref: (i, 0)),
                       pl.BlockSpec(blk, lambda i, where_ref: (0, i, 0))]),
        out_shape=[jax.ShapeDtypeStruct((rh, cols), F32), jax.ShapeDtypeStruct(got.shape, BF16)],
        compiler_params=_params(("parallel",)),
    )(where, part, got)


def sum_chips_call(name, own, got):
    _, rh, cols = got.shape
    tr = _row_tile(rh, 176, 16)

    def body(h_ref, g_ref, o_ref):
        o_ref[...] = ((h_ref[...] + g_ref[0].astype(F32)) + g_ref[1].astype(F32)) + g_ref[2].astype(F32)

    return pl.pallas_call(
        body, name=name, grid=(rh // tr,),
        in_specs=[pl.BlockSpec((tr, cols), lambda i: (i, 0)), pl.BlockSpec((3, tr, cols), lambda i: (0, i, 0))],
        out_specs=pl.BlockSpec((tr, cols), lambda i: (i, 0)),
        out_shape=jax.ShapeDtypeStruct((rh, cols), F32),
        compiler_params=_params(("parallel",)),
    )(own, got)


def _adamw(w, g, m, v):
    m = ADAM_B1 * m + (1.0 - ADAM_B1) * g
    v = ADAM_B2 * v + (1.0 - ADAM_B2) * (g * g)
    m_hat = m / (1.0 - ADAM_B1 ** ADAM_STEP)
    v_hat = v / (1.0 - ADAM_B2 ** ADAM_STEP)
    delta = -ADAM_LR * (m_hat / (jnp.sqrt(v_hat) + ADAM_EPS) + ADAM_WD * w)
    return delta, m, v


def adamw_call(name, g, w, m, v):
    r, cols = w.shape
    tr = r if r % 8 else _row_tile(r, 256)

    def body(g_ref, w_ref, m_ref, v_ref, go_ref, d_ref, nm_ref, nv_ref):
        g = g_ref[...]
        go_ref[...] = g
        d_ref[...], nm_ref[...], nv_ref[...] = _adamw(w_ref[...], g, m_ref[...], v_ref[...])

    spec = pl.BlockSpec((tr, cols), lambda i: (i, 0))
    return pl.pallas_call(
        body, name=name, grid=(r // tr,), in_specs=[spec] * 4, out_specs=[spec] * 4,
        out_shape=[jax.ShapeDtypeStruct((r, cols), F32)] * 4,
        compiler_params=_params(("parallel",)),
    )(g, w, m, v)


def allsum_small_call(v):
    R = v.shape[0]

    def body(v_ref, out_ref, buf, send_sems, recv_sems):
        x, y, c = _place()
        me = 4 * x + 2 * y + c
        buf[me] = v_ref[...]
        sends = []
        for k in range(1, N_DEV):
            fx, fy, fc = (k >> 2) & 1, (k >> 1) & 1, k & 1
            cp = pltpu.make_async_remote_copy(
                src_ref=v_ref, dst_ref=buf.at[me], send_sem=send_sems.at[k - 1], recv_sem=recv_sems.at[k - 1],
                device_id=(x ^ fx, y ^ fy, c ^ fc), device_id_type=MESH)
            cp.start()
            sends.append(cp)
        for k in range(1, N_DEV):
            pltpu.make_async_remote_copy(
                src_ref=v_ref, dst_ref=buf.at[me ^ k], send_sem=send_sems.at[k - 1], recv_sem=recv_sems.at[k - 1],
                device_id=(x, y, c), device_id_type=MESH).wait_recv()
        acc = buf[0]
        for d in range(1, N_DEV):
            acc = acc + buf[d]
        out_ref[...] = acc
        for cp in sends:
            cp.wait_send()

    vm = pl.BlockSpec(memory_space=pltpu.VMEM)
    return pl.pallas_call(
        body, name="allsum_small", in_specs=[vm], out_specs=vm,
        out_shape=jax.ShapeDtypeStruct((R, LANES), F32),
        scratch_shapes=[pltpu.VMEM((N_DEV, R, LANES), F32), pltpu.SemaphoreType.DMA((N_DEV - 1,)),
                        pltpu.SemaphoreType.DMA((N_DEV - 1,))],
        compiler_params=_params(),
    )(v)


def _slab(flat, mult):
    n = flat.shape[-1]
    rows = -(-n // (LANES * mult)) * mult
    flat = jnp.pad(flat, [(0, 0)] * (flat.ndim - 1) + [(0, rows * LANES - n)])
    return flat.reshape(flat.shape[:-1] + (rows, LANES))


def full_from_chips(blocks, by_col):
    _, r, c = blocks.shape
    return blocks.transpose(1, 0, 2).reshape(r, N_CHIPS * c) if by_col else blocks.reshape(N_CHIPS * r, c)


def chips_from_full(full, by_col):
    if by_col:
        r, c = full.shape[0], full.shape[1] // N_CHIPS
        return full.reshape(r, N_CHIPS, c).transpose(1, 0, 2)
    return full.reshape(N_CHIPS, full.shape[0] // N_CHIPS, full.shape[1])


SMALL_PACK = SMALL_W + ("loss", "conv_w")
SMALL_PACK_N = {**SMALL_N, "loss": 1, "conv_w": 3 * 2 * D_FF}


def pack_small(vals):
    zero = jnp.zeros((1,), F32)
    return _slab(jnp.concatenate([vals[n].reshape(-1) if n in vals else jnp.tile(zero, SMALL_PACK_N[n])
                                  for n in SMALL_PACK]), 8)


def unpack_small(slab, shapes):
    flat = slab.reshape(-1)
    out, off = {}, 0
    for n in SMALL_PACK:
        out[n] = flat[off:off + SMALL_PACK_N[n]].reshape(shapes[n])
        off += SMALL_PACK_N[n]
    return out


def _split_heads(w, a, b):
    r = w.shape[0]
    w3 = w.reshape(r, HEADS, a + b)
    return w3[:, :, :a].reshape(r, HEADS * a), w3[:, :, a:].reshape(r, HEADS * b)


def _merge_heads(wa, wb, a, b):
    r = wa.shape[0]
    return jnp.concatenate([wa.reshape(r, HEADS, a), wb.reshape(r, HEADS, b)], axis=2).reshape(r, HEADS * (a + b))


def kernel(x, positions, g_mix, w_in, g_cq, w_uq, g_ckv, w_ukv, g_sb_out, g_mla_out, w_out, g_ffn, w_up, conv_w, conv_b, w_down, g_final, loss_target, m_g_mix, m_w_in, m_g_cq, m_w_uq, m_g_ckv, m_w_ukv, m_g_sb_out, m_g_mla_out, m_w_out, m_g_ffn, m_w_up, m_conv_w, m_conv_b, m_w_down, m_g_final, v_g_mix, v_w_in, v_g_cq, v_w_uq, v_g_ckv, v_w_ukv, v_g_sb_out, v_g_mla_out, v_w_out, v_g_ffn, v_w_up, v_conv_w, v_conv_b, v_w_down, v_g_final):
    given = dict(locals())
    B, S, _ = x.shape
    T = B * S
    w_big = {n: given[n][0].T if n == "w_in" else given[n][0] for n in BIG_W}
    m_big = {n: given["m_" + n][0].T if n == "w_in" else given["m_" + n][0] for n in BIG_W}
    v_big = {n: given["v_" + n][0].T if n == "w_in" else given["v_" + n][0] for n in BIG_W}
    shard_shape = {n: w_big[n].shape for n in BIG_W}

    first = ("w_in", "w_uq", "w_ukv")
    later = ("w_out", "w_up", "w_down", "conv_w")
    x2d = x.reshape(T, D_MODEL)
    h, got_w = rmsnorm_fwd_call("norm_mix", x2d, g_mix, ex=gather_group([w_big[n].astype(BF16) for n in first]))
    full = {n: full_from_chips(g_, BIG_SHARD[n][2]) for n, g_ in zip(first, got_w) if n != "w_in"}
    gather_later = gather_group([w_big[n] if n == "conv_w" else w_big[n].astype(BF16) for n in later])
    w_in_t = jnp.pad(got_w[0].reshape(IN_COLS, D_MODEL), ((0, IN_COLS_PAD - IN_COLS), (0, 0)))
    w_uq_p = jnp.concatenate(_split_heads(full["w_uq"], HEAD_DIM, ROPE_DIM), axis=1)
    w_ukv_p = jnp.concatenate(_split_heads(full["w_ukv"], HEAD_DIM, HEAD_DIM), axis=1)

    half = ROPE_DIM // 2
    inv_freq = 1.0 / (ROPE_BASE ** (jnp.arange(half, dtype=F32) * (2.0 / ROPE_DIM)))
    cos, sin = rope_tab_call(positions.reshape(T, 1), jnp.tile(inv_freq, LANES // half).reshape(1, LANES))
    p = matmul_call("proj_in", h, w_in_t, "nt", tn=IN_COLS_PAD // 2)
    qn, qr, kn, vm, krt, cqn, ckvn = mla_prep_fwd_call(p, cos, sin, g_cq, g_ckv, w_uq_p, w_ukv_p)
    o_sb, lt_sb, got_w = sb_fwd_call(p, B, S, ex=gather_later)
    w_up4 = got_w[1]
    full.update({n: full_from_chips(g_, BIG_SHARD[n][2]) for n, g_ in zip(later, got_w) if n != "w_up"})
    conv_w_full = full["conv_w"]
    o_mla, lse = mla_fwd_call(qn, qr, kn, krt, vm, B, S)
    o_cat = outnorm_fwd_call(o_sb, o_mla, g_sb_out, g_mla_out)
    x1, hn = proj_out_norm_call(o_cat, full["w_out"], x2d, g_ffn)
    u_g, u_v = ffn_up_call(hn, w_up4)
    act, c_g, c_v = conv_act_fwd_call(u_g, u_v, conv_w_full, conv_b, B, S)
    dx2, dx2b, loss_row, dg_final = ffn_down_loss_call(
        act, full["w_down"], x1, g_final.reshape(1, D_MODEL), loss_target.reshape(T, D_MODEL))

    xi, yi, ci = _place()
    chip = (2 * xi + yi).astype(jnp.int32).reshape(1)
    where = jnp.stack([ci, 2 * xi + yi]).astype(jnp.int32)

    def add_halves(names, parts, sib_rows):
        return [add_half_call("add_half_" + n, p_, s_, where) for n, p_, s_ in zip(names, parts, sib_rows)]

    def sum_chips(names, halves, from_chips):
        return [sum_chips_call("sum_chips_" + n, h_[0], f_) for n, h_, f_ in zip(names, halves, from_chips)]

    ffn_w = ("w_down", "w_up")
    parts_ffn = [chips_from_full(wgrad_call("wgrad_down", act, dx2b, tn=512), False)]
    du_g, du_v, dcw_g, dcw_v, dcb_g, dcb_v = conv_act_bwd_call(
        u_g, u_v, c_g, c_v, dx2b, full["w_down"], conv_w_full, B, S)
    parts_ffn.append(wgrad_up_call(hn, du_g, du_v))
    dx1, dg_ffn, sib_ffn = ffn_up_bwd_call(du_g, du_v, w_up4, x1, g_ffn, dx2, swap_half(parts_ffn))
    parts_out = [chips_from_full(wgrad_call("wgrad_out", o_cat, dx1), False)]
    do_sb, do_mla, dg_sb_out, dg_mla_out, sib_out = proj_out_bwd_call(
        dx1, full["w_out"], o_sb, o_mla, g_sb_out, g_mla_out, swap_half(parts_out))
    early = ffn_w + ("w_out",)
    halves = add_halves(early, parts_ffn + parts_out, list(sib_ffn) + list(sib_out))
    dq_sb, dk_sb, dv_sb, from_chips = sb_bwd_call(p, lt_sb, do_sb, B, S, ex=scatter_half([h_[1] for h_ in halves]))
    finals = sum_chips(early, halves, from_chips)
    dqn, dqr4, dkn, dvm, dkrt4, done = mla_bwd_call(qn, qr, kn, krt, vm, o_mla, lse, do_mla, B, S,
        ex=swap_final(finals, [shard_shape[n] for n in early]))
    grads = dict(zip(early, done))
    dcq, dckvr, dq_cat, dkv_cat, dg_cq, dg_ckv = mla_prep_bwd_call(
        p, cos, sin, g_cq, g_ckv, w_uq_p, w_ukv_p, dqn, dqr4, dkn, dvm, dkrt4)
    dw_uq_p = wgrad_call("wgrad_uq", cqn, dq_cat)
    dw_ukv_p = wgrad_call("wgrad_ukv", ckvn, dkv_cat)
    dp = (dq_sb, dk_sb, dv_sb, dcq, dckvr)
    late = ("w_uq", "w_ukv", "w_in")
    parts_late = [chips_from_full(g_, True) for g_ in (
        _merge_heads(dw_uq_p[:, :ATT_W], dw_uq_p[:, ATT_W:], HEAD_DIM, ROPE_DIM),
        _merge_heads(dw_ukv_p[:, :ATT_W], dw_ukv_p[:, ATT_W:], HEAD_DIM, HEAD_DIM))]
    parts_late.append(chips_from_full(wgrad_in_call(h, dp)[:IN_COLS], False))
    dh, sib_late = proj_in_bwd_call(dp, w_in_t, swap_half(parts_late))
    halves = add_halves(late, parts_late, sib_late)
    grad_x, dg_mix, from_chips = rmsnorm_bwd_call(
        "norm_mix_bwd", x2d, g_mix, dh, dx1, ex=scatter_half([h_[1] for h_ in halves]))
    finals = sum_chips(late, halves, from_chips)
    grads.update(zip(late, exchange_call("swap_final_late", swap_final(finals, [shard_shape[n] for n in late]))))

    shapes = {n: given[n].shape for n in SMALL_W}
    shapes.update(loss=(), conv_w=(3, 2 * D_FF))
    small_g = {"g_mix": dg_mix, "g_cq": dg_cq, "g_ckv": dg_ckv, "g_sb_out": dg_sb_out, "g_mla_out": dg_mla_out,
               "g_ffn": dg_ffn, "conv_b": jnp.concatenate([dcb_g, dcb_v], axis=1), "g_final": dg_final,
               "loss": loss_row[0, :1], "conv_w": jnp.concatenate([dcw_g, dcw_v], axis=1)}
    gs_slab = allsum_small_call(pack_small(small_g))
    small_in = [pack_small({n: given[pre + n] for n in SMALL_W}) for pre in ("", "m_", "v_")]
    small_out = [unpack_small(s, shapes) for s in adamw_call("adamw_small", gs_slab, *small_in)]
    cw_cols = BIG_SHARD["conv_w"][1]
    grads["conv_w"] = lax.dynamic_slice_in_dim(small_out[0]["conv_w"], chip[0] * cw_cols, cw_cols, axis=1)

    big_out = {n: adamw_call("adamw_" + n, grads[n], w_big[n], m_big[n], v_big[n]) for n in BIG_W}
    weights = ("g_mix", "w_in", "g_cq", "w_uq", "g_ckv", "w_ukv", "g_sb_out", "g_mla_out", "w_out", "g_ffn",
               "w_up", "conv_w", "conv_b", "w_down", "g_final")
    outs = [small_out[0]["loss"], grad_x.reshape(B, S, D_MODEL)]
    for k in range(4):
        for n in weights:
            if n in BIG_W:
                outs.append((big_out[n][k].T if n == "w_in" else big_out[n][k])[None])
            else:
                outs.append(small_out[k][n])
    return tuple(outs)
```

```python
import functools

import jax
import jax.numpy as jnp
from jax import lax
from jax.experimental import pallas as pl
from jax.experimental.pallas import tpu as pltpu

F32 = jnp.float32
BF16 = jnp.bfloat16
MESH = pl.DeviceIdType.MESH

D_MODEL = 1024
HEADS = 8
HEAD_DIM = 64
ATT_W = HEADS * HEAD_DIM
ROPE_DIM = 32
ROPE_W = HEADS * ROPE_DIM
QK_DIM = HEAD_DIM + ROPE_DIM
Q_RANK = 384
KV_RANK = 256
D_FF = 2816
IN_COLS = 2208
IN_COLS_PAD = 2304
EPS = 1e-6
ROPE_BASE = 10000.0
SB_SCALE = HEAD_DIM ** -0.5
SB_SCALE2 = SB_SCALE * 1.4426950408889634
MLA_SCALE = QK_DIM ** -0.5
LOG2E = 1.4426950408889634
LN2 = 0.6931471805599453
MLA_SCALE2 = MLA_SCALE * LOG2E
LANES = 128
N_CHIPS = 4
N_DEV = 8
VMEM_LIMIT = 48 * 1024 * 1024
ATT_TQ = 256
ATT_TK = 256
ATT_PAIRS = 4
PAIR_LANES = [slice(i * LANES, (i + 1) * LANES) for i in range(ATT_PAIRS)]
SB_BWD_GROUP = 2
SB_FWD_GROUP = 4
MLA_GROUP = 4
NEG_BIG = -1e30

ADAM_LR = 0.001
ADAM_B1 = 0.9
ADAM_B2 = 0.999
ADAM_EPS = 1e-08
ADAM_WD = 0.01
ADAM_STEP = 10

BIG_W = ("w_in", "w_uq", "w_ukv", "w_out", "w_up", "conv_w", "w_down")
BIG_SHARD = {
    "w_in": (D_MODEL, IN_COLS // 4, True),
    "w_uq": (Q_RANK, HEADS * QK_DIM // 4, True),
    "w_ukv": (KV_RANK, 2 * ATT_W // 4, True),
    "w_out": (2 * ATT_W // 4, D_MODEL, False),
    "w_up": (D_MODEL, 2 * D_FF // 4, True),
    "conv_w": (3, 2 * D_FF // 4, True),
    "w_down": (D_FF // 4, D_MODEL, False),
}
SMALL_W = ("g_mix", "g_cq", "g_ckv", "g_sb_out", "g_mla_out", "g_ffn", "conv_b", "g_final")
SMALL_N = {"g_mix": D_MODEL, "g_cq": Q_RANK, "g_ckv": KV_RANK, "g_sb_out": ATT_W, "g_mla_out": ATT_W,
           "g_ffn": D_MODEL, "conv_b": 2 * D_FF, "g_final": D_MODEL}


def _params(sem=None, **kw):
    return pltpu.CompilerParams(dimension_semantics=sem, vmem_limit_bytes=VMEM_LIMIT, **kw)


def _dot(a, b, dims):
    return lax.dot_general(a, b, (dims, ((), ())), preferred_element_type=F32)


def _nn(a, b):
    return _dot(a, b, ((1,), (0,)))


def _nt(a, b):
    return _dot(a, b, ((1,), (1,)))


def _tn(a, b):
    return _dot(a, b, ((0,), (0,)))


def _split2(x):
    hi = x.astype(BF16)
    lo = (x - hi.astype(F32)).astype(BF16)
    return hi, lo


def _split3(x):
    hi = x.astype(BF16)
    r1 = x - hi.astype(F32)
    mid = r1.astype(BF16)
    return hi, mid, (r1 - mid.astype(F32)).astype(BF16)


def _rms_r(x, d):
    return lax.rsqrt(jnp.sum(x * x, axis=-1, keepdims=True) * (1.0 / d) + EPS)


def _rms_bwd(x, g, dy, d):
    r = _rms_r(x, d)
    xhat = x * r
    gy = dy * g
    dx = r * (gy - xhat * (jnp.sum(xhat * gy, axis=-1, keepdims=True) * (1.0 / d)))
    return dx, jnp.sum(dy * xhat, axis=0, keepdims=True)


def _rot(x):
    lane = lax.broadcasted_iota(jnp.int32, x.shape, x.ndim - 1)
    n = x.shape[-1]
    return jnp.where((lane & 31) < 16, pltpu.roll(x, n - 16, x.ndim - 1), pltpu.roll(x, 16, x.ndim - 1))


def _fold4(x):
    return x + pltpu.roll(x, 32, 1) + pltpu.roll(x, 64, 1) + pltpu.roll(x, 96, 1)


def matmul_call(name, a, b, mode, out_dtype=F32, res=None, tm=512, tn=None, ex=None):
    M, K = a.shape
    N = b.shape[1] if mode == "nn" else b.shape[0]
    tn = N if tn is None else tn
    assert M % tm == 0 and N % tn == 0

    def body(*refs):
        if res is None:
            a_ref, b_ref, o_ref = refs
        else:
            a_ref, b_ref, r_ref, o_ref = refs
        av = a_ref[...].astype(BF16)
        bv = b_ref[...].astype(BF16)
        acc = _nn(av, bv) if mode == "nn" else _nt(av, bv)
        if res is not None:
            acc = r_ref[...] + acc
        o_ref[...] = acc.astype(out_dtype)

    in_specs = [pl.BlockSpec((tm, K), lambda j, i: (i, 0))]
    if mode == "nn":
        in_specs.append(pl.BlockSpec((K, tn), lambda j, i: (0, j)))
    else:
        in_specs.append(pl.BlockSpec((tn, K), lambda j, i: (j, 0)))
    args = [a, b]
    if res is not None:
        in_specs.append(pl.BlockSpec((tm, tn), lambda j, i: (i, j)))
        args.append(res)
    outs, moved = _call(body, ex, name=name, grid=(N // tn, M // tm), in_specs=in_specs,
                        out_specs=[pl.BlockSpec((tm, tn), lambda j, i: (i, j))],
                        out_shape=[jax.ShapeDtypeStruct((M, N), out_dtype)], args=args)
    return outs[0] if ex is None else (outs[0], moved)


def _rows(tm, width):
    return pl.BlockSpec((tm, width), lambda i: (i, 0))


def _whole(a, **kw):
    return pl.BlockSpec(a.shape, lambda i: (0,) * a.ndim, **kw)


def proj_out_norm_call(o_cat, w_out, x, g_ffn, tm=512):
    T, K = o_cat.shape
    N = w_out.shape[1]

    def body(a_ref, w_ref, x_ref, g_ref, x1_ref, hn_ref):
        x1 = x_ref[...] + _nn(a_ref[...], w_ref[...])
        x1_ref[...] = x1
        hn_ref[...] = ((x1 * _rms_r(x1, N)) * g_ref[...]).astype(BF16)

    return pl.pallas_call(
        body, name="proj_out", grid=(T // tm,),
        in_specs=[_rows(tm, K), _whole(w_out), _rows(tm, N), _whole(g_ffn)], out_specs=[_rows(tm, N)] * 2,
        out_shape=[jax.ShapeDtypeStruct((T, N), F32), jax.ShapeDtypeStruct((T, N), BF16)],
        compiler_params=_params(("parallel",)),
    )(o_cat, w_out, x, g_ffn)


def ffn_down_loss_call(act, w_down, x1, g, target, tm=512):
    T, K = act.shape
    d = w_down.shape[1]

    def body(a_ref, w_ref, x1_ref, g_ref, t_ref, dx_ref, dxb_ref, loss_ref, dg_ref):
        @pl.when(pl.program_id(0) == 0)
        def _():
            loss_ref[...] = jnp.zeros_like(loss_ref)
            dg_ref[...] = jnp.zeros_like(dg_ref)

        x = x1_ref[...] + _nn(a_ref[...], w_ref[...])
        g = g_ref[...]
        y = (x * _rms_r(x, d)) * g
        err = y - t_ref[...]
        loss_ref[...] += jnp.sum(jnp.sum(err * err, axis=1, keepdims=True), axis=0, keepdims=True) * (0.5 / d)
        dx, dg = _rms_bwd(x, g, err * (1.0 / d), d)
        dx_ref[...] = dx
        dxb_ref[...] = dx.astype(BF16)
        dg_ref[...] += dg

    return pl.pallas_call(
        body, name="ffn_down_loss", grid=(T // tm,),
        in_specs=[_rows(tm, K), _whole(w_down), _rows(tm, d), _whole(g), _rows(tm, d)],
        out_specs=[_rows(tm, d), _rows(tm, d), pl.BlockSpec((1, LANES), lambda i: (0, 0)), _whole(g)],
        out_shape=[jax.ShapeDtypeStruct((T, d), F32), jax.ShapeDtypeStruct((T, d), BF16),
                   jax.ShapeDtypeStruct((1, LANES), F32), jax.ShapeDtypeStruct((1, d), F32)],
        compiler_params=_params(("arbitrary",)),
    )(act, w_down, x1, g, target)


def proj_out_bwd_call(dx1, w_out, o_sb, o_mla, g_sb, g_mla, ex, tm=512):
    T, N = dx1.shape

    def body(d_ref, w_ref, a_ref, b_ref, ga_ref, gb_ref, da_ref, db_ref, dga_ref, dgb_ref):
        @pl.when(pl.program_id(0) == 0)
        def _():
            dga_ref[...] = jnp.zeros_like(dga_ref)
            dgb_ref[...] = jnp.zeros_like(dgb_ref)

        d = _nt(d_ref[...].astype(BF16), w_ref[...])
        da, dga = _rms_bwd(a_ref[...], ga_ref[...], d[:, :ATT_W], ATT_W)
        db, dgb = _rms_bwd(b_ref[...], gb_ref[...], d[:, ATT_W:], ATT_W)
        da_ref[...] = da
        db_ref[...] = db
        dga_ref[...] += dga
        dgb_ref[...] += dgb

    outs, moved = _call(
        body, ex, name="proj_out_bwd", grid=(T // tm,),
        in_specs=[_rows(tm, N), _whole(w_out), _rows(tm, ATT_W), _rows(tm, ATT_W), _whole(g_sb), _whole(g_mla)],
        out_specs=[_rows(tm, ATT_W), _rows(tm, ATT_W), _whole(g_sb), _whole(g_mla)],
        out_shape=[jax.ShapeDtypeStruct((T, ATT_W), F32), jax.ShapeDtypeStruct((T, ATT_W), F32),
                   jax.ShapeDtypeStruct((1, ATT_W), F32), jax.ShapeDtypeStruct((1, ATT_W), F32)],
        args=(dx1, w_out, o_sb, o_mla, g_sb, g_mla))
    return tuple(outs) + (moved,)


def wgrad_call(name, a, b, tn=None, tt=512, by_chip=False):
    T, M = a.shape
    N = b.shape[1]
    tn = N if tn is None else tn
    assert T % tt == 0 and N % tn == 0
    if by_chip:
        out_spec = pl.BlockSpec((None, M, tn), lambda j, t: (j, 0, 0))
        out_shape = jax.ShapeDtypeStruct((N // tn, M, tn), F32)
    else:
        out_spec = pl.BlockSpec((M, tn), lambda j, t: (0, j))
        out_shape = jax.ShapeDtypeStruct((M, N), F32)

    def body(a_ref, b_ref, o_ref):
        @pl.when(pl.program_id(1) == 0)
        def _():
            o_ref[...] = jnp.zeros_like(o_ref)

        o_ref[...] += _tn(a_ref[...].astype(BF16), b_ref[...].astype(BF16))

    return pl.pallas_call(
        body, name=name, grid=(N // tn, T // tt),
        in_specs=[pl.BlockSpec((tt, M), lambda j, t: (t, 0)), pl.BlockSpec((tt, tn), lambda j, t: (t, j))],
        out_specs=out_spec, out_shape=out_shape,
        compiler_params=_params(("parallel", "arbitrary")),
    )(a, b)


UP_COLS = 2 * D_FF // N_CHIPS


def _shift_down_after(u, prev8, n):
    top = pltpu.roll(jnp.concatenate([prev8, u[0:8]], axis=0), n, 0)[8:16]
    return jnp.concatenate([top, pltpu.roll(u, n, 0)[8:]], axis=0)


def ffn_up_conv_call(hn, w4, conv_w, conv_b, S, tm=512):
    T, K = hn.shape
    per_seq = S // tm

    def body(a_ref, wg_ref, wv_ref, cg_ref, cv_ref, bg_ref, bv_ref, ug_ref, uv_ref, act_ref, og_ref, ov_ref, pg_ref, pv_ref):
        @pl.when(pl.program_id(1) % per_seq == 0)
        def _():
            pg_ref[...] = jnp.zeros_like(pg_ref)
            pv_ref[...] = jnp.zeros_like(pv_ref)

        a = a_ref[...]
        outs = []
        for w_ref, c_ref, b_ref, u_ref, prev_ref in ((wg_ref, cg_ref, bg_ref, ug_ref, pg_ref),
                                                     (wv_ref, cv_ref, bv_ref, uv_ref, pv_ref)):
            u = _nn(a, w_ref[...])
            u_ref[...] = u
            prev8 = prev_ref[...]
            taps = _taps(c_ref)
            outs.append(taps[0] * _shift_down_after(u, prev8, 2) + taps[1] * _shift_down_after(u, prev8, 1)
                        + taps[2] * u + b_ref[...])
            prev_ref[...] = u[tm - 8:, :]
        gate, val = outs
        act_ref[...] = (gate * (1.0 / (1.0 + jnp.exp(-gate))) * val).astype(BF16)
        og_ref[...] = gate.astype(BF16)
        ov_ref[...] = val.astype(BF16)

    out = pl.BlockSpec((tm, UP_COLS), lambda j, i: (i, j))

    def cols(rows, off):
        return pl.BlockSpec((rows, UP_COLS), lambda j, i: (0, off + j))

    return pl.pallas_call(
        body, name="ffn_up_conv", grid=(2, T // tm),
        in_specs=[pl.BlockSpec((tm, K), lambda j, i: (i, 0)),
                  pl.BlockSpec((None, K, UP_COLS), lambda j, i: (j, 0, 0)),
                  pl.BlockSpec((None, K, UP_COLS), lambda j, i: (2 + j, 0, 0)),
                  cols(3, 0), cols(3, 2), cols(1, 0), cols(1, 2)],
        out_specs=[out] * 5,
        out_shape=[jax.ShapeDtypeStruct((T, D_FF), F32)] * 2 + [jax.ShapeDtypeStruct((T, D_FF), BF16)] * 3,
        scratch_shapes=[pltpu.VMEM((8, UP_COLS), F32)] * 2,
        compiler_params=_params(("arbitrary", "arbitrary")),
    )(hn, w4, w4, conv_w, conv_w, conv_b, conv_b)


def ffn_up_bwd_call(du_g, du_v, w4, x1, g_ffn, dx2, ex, tm=512):
    T = du_g.shape[0]
    N = w4.shape[1]

    def body(g_ref, v_ref, w_ref, x1_ref, gf_ref, dx2_ref, dx1_ref, dg_ref):
        @pl.when(pl.program_id(0) == 0)
        def _():
            dg_ref[...] = jnp.zeros_like(dg_ref)

        dhn = _nt(g_ref[:, :UP_COLS], w_ref[0]) + _nt(g_ref[:, UP_COLS:], w_ref[1])
        dhn = dhn + _nt(v_ref[:, :UP_COLS], w_ref[2]) + _nt(v_ref[:, UP_COLS:], w_ref[3])
        dx, dg = _rms_bwd(x1_ref[...], gf_ref[...], dhn, N)
        dx1_ref[...] = dx2_ref[...] + dx
        dg_ref[...] += dg

    outs, moved = _call(
        body, ex, name="ffn_up_bwd", grid=(T // tm,),
        in_specs=[_rows(tm, D_FF), _rows(tm, D_FF), _whole(w4, pipeline_mode=pl.Buffered(1)), _rows(tm, N),
                  _whole(g_ffn), _rows(tm, N)],
        out_specs=[_rows(tm, N), _whole(g_ffn)],
        out_shape=[jax.ShapeDtypeStruct((T, N), F32), jax.ShapeDtypeStruct((1, N), F32)],
        args=(du_g, du_v, w4, x1, g_ffn, dx2))
    return outs[0], outs[1], moved


def wgrad_up_call(hn, du_g, du_v, tt=512):
    T, M = hn.shape

    def body(a_ref, g_ref, v_ref, o_ref):
        @pl.when(pl.program_id(1) == 0)
        def _():
            o_ref[...] = jnp.zeros_like(o_ref)

        a = a_ref[...]
        o_ref[0] += _tn(a, g_ref[...])
        o_ref[1] += _tn(a, v_ref[...])

    col = pl.BlockSpec((tt, UP_COLS), lambda j, t: (t, j))
    out = pl.pallas_call(
        body, name="wgrad_up", grid=(2, T // tt),
        in_specs=[pl.BlockSpec((tt, M), lambda j, t: (t, 0)), col, col],
        out_specs=pl.BlockSpec((2, None, M, UP_COLS), lambda j, t: (0, j, 0, 0)),
        out_shape=jax.ShapeDtypeStruct((2, 2, M, UP_COLS), F32),
        compiler_params=_params(("parallel", "arbitrary")),
    )(hn, du_g, du_v)
    return out.reshape(N_CHIPS, M, UP_COLS)


IN_PIECES = ((0, ATT_W), (ATT_W, ATT_W), (2 * ATT_W, ATT_W), (3 * ATT_W, Q_RANK), (3 * ATT_W + Q_RANK, Q_RANK))


def proj_in_bwd_call(pieces, w_in_t, ex, tm=512):
    T = pieces[0].shape[0]
    N = w_in_t.shape[1]
    n = len(pieces)

    def body(*refs):
        o_ref = refs[2 * n]
        acc = _nn(refs[0][...].astype(BF16), refs[n][...])
        for i in range(1, n):
            acc = acc + _nn(refs[i][...].astype(BF16), refs[n + i][...])
        o_ref[...] = acc

    outs, moved = _call(
        body, ex, name="proj_in_bwd", grid=(T // tm,),
        in_specs=[pl.BlockSpec((tm, w), lambda i: (i, 0)) for _, w in IN_PIECES]
        + [pl.BlockSpec((w, N), functools.partial(lambda c, i: (c, 0), off // w)) for off, w in IN_PIECES],
        out_specs=[pl.BlockSpec((tm, N), lambda i: (i, 0))],
        out_shape=[jax.ShapeDtypeStruct((T, N), F32)], args=tuple(pieces) + (w_in_t,) * n)
    return outs[0], moved


def wgrad_in_call(h, pieces, tt=512):
    T, M = h.shape
    n = len(pieces)

    def body(*refs):
        a_ref, o_ref = refs[0], refs[n + 1]

        @pl.when(pl.program_id(0) == 0)
        def _():
            o_ref[...] = jnp.zeros_like(o_ref)

        a = a_ref[...]
        for i, (off, w) in enumerate(IN_PIECES):
            o_ref[off:off + w, :] += _tn(refs[1 + i][...].astype(BF16), a)

    return pl.pallas_call(
        body, name="wgrad_in", grid=(T // tt,),
        in_specs=[pl.BlockSpec((tt, M), lambda t: (t, 0))] + [pl.BlockSpec((tt, w), lambda t: (t, 0)) for _, w in IN_PIECES],
        out_specs=pl.BlockSpec((IN_COLS_PAD, M), lambda t: (0, 0)),
        out_shape=jax.ShapeDtypeStruct((IN_COLS_PAD, M), F32),
        compiler_params=_params(("arbitrary",)),
    )(h, *pieces)


def rmsnorm_fwd_call(name, x, g, tm=512, ex=None):
    T, d = x.shape

    def body(x_ref, g_ref, o_ref):
        x = x_ref[...]
        o_ref[...] = ((x * _rms_r(x, d)) * g_ref[...]).astype(BF16)

    row = pl.BlockSpec((tm, d), lambda i: (i, 0))
    outs, moved = _call(body, ex, name=name, grid=(T // tm,), in_specs=[row, pl.BlockSpec((1, d), lambda i: (0, 0))],
                        out_specs=[row], out_shape=[jax.ShapeDtypeStruct((T, d), BF16)], args=(x, g))
    return outs[0] if ex is None else (outs[0], moved)


def rmsnorm_bwd_call(name, x, g, dy, res, tm=512, ex=None):
    T, d = x.shape

    def body(x_ref, g_ref, dy_ref, r_ref, dx_ref, dg_ref):
        @pl.when(pl.program_id(0) == 0)
        def _():
            dg_ref[...] = jnp.zeros_like(dg_ref)

        dx, dg = _rms_bwd(x_ref[...], g_ref[...], dy_ref[...], d)
        dx_ref[...] = r_ref[...] + dx
        dg_ref[...] += dg

    row = pl.BlockSpec((tm, d), lambda i: (i, 0))
    vec = pl.BlockSpec((1, d), lambda i: (0, 0))
    outs, moved = _call(body, ex, name=name, grid=(T // tm,), in_specs=[row, vec, row, row], out_specs=[row, vec],
                        out_shape=[jax.ShapeDtypeStruct((T, d), F32), jax.ShapeDtypeStruct((1, d), F32)],
                        args=(x, g, dy, res))
    return tuple(outs) if ex is None else tuple(outs) + (moved,)


def outnorm_fwd_call(o_sb, o_mla, g_sb, g_mla, tm=512):
    T = o_sb.shape[0]

    def body(a_ref, b_ref, ga_ref, gb_ref, o_ref):
        a = a_ref[...]
        b = b_ref[...]
        ya = (a * _rms_r(a, ATT_W)) * ga_ref[...]
        yb = (b * _rms_r(b, ATT_W)) * gb_ref[...]
        o_ref[...] = jnp.concatenate([ya, yb], axis=1).astype(BF16)

    row = pl.BlockSpec((tm, ATT_W), lambda i: (i, 0))
    vec = pl.BlockSpec((1, ATT_W), lambda i: (0, 0))
    return pl.pallas_call(
        body, name="outnorm_fwd", grid=(T // tm,), in_specs=[row, row, vec, vec],
        out_specs=pl.BlockSpec((tm, 2 * ATT_W), lambda i: (i, 0)),
        out_shape=jax.ShapeDtypeStruct((T, 2 * ATT_W), BF16),
        compiler_params=_params(("parallel",)),
    )(o_sb, o_mla, g_sb, g_mla)


def outnorm_bwd_call(o_sb, o_mla, g_sb, g_mla, do_cat, tm=512, ex=None):
    T = o_sb.shape[0]

    def body(a_ref, b_ref, ga_ref, gb_ref, d_ref, da_ref, db_ref, dga_ref, dgb_ref):
        @pl.when(pl.program_id(0) == 0)
        def _():
            dga_ref[...] = jnp.zeros_like(dga_ref)
            dgb_ref[...] = jnp.zeros_like(dgb_ref)

        d = d_ref[...]
        da, dga = _rms_bwd(a_ref[...], ga_ref[...], d[:, :ATT_W], ATT_W)
        db, dgb = _rms_bwd(b_ref[...], gb_ref[...], d[:, ATT_W:], ATT_W)
        da_ref[...] = da
        db_ref[...] = db
        dga_ref[...] += dga
        dgb_ref[...] += dgb

    row = pl.BlockSpec((tm, ATT_W), lambda i: (i, 0))
    vec = pl.BlockSpec((1, ATT_W), lambda i: (0, 0))
    outs, moved = _call(
        body, ex, name="outnorm_bwd", grid=(T // tm,),
        in_specs=[row, row, vec, vec, pl.BlockSpec((tm, 2 * ATT_W), lambda i: (i, 0))],
        out_specs=[row, row, vec, vec],
        out_shape=[jax.ShapeDtypeStruct((T, ATT_W), F32), jax.ShapeDtypeStruct((T, ATT_W), F32),
                   jax.ShapeDtypeStruct((1, ATT_W), F32), jax.ShapeDtypeStruct((1, ATT_W), F32)],
        args=(o_sb, o_mla, g_sb, g_mla, do_cat))
    return tuple(outs) if ex is None else tuple(outs) + (moved,)


def final_loss_call(x2, g, target, tm=512):
    T, d = x2.shape

    def body(x_ref, g_ref, t_ref, dx_ref, dxb_ref, loss_ref, dg_ref):
        @pl.when(pl.program_id(0) == 0)
        def _():
            loss_ref[...] = jnp.zeros_like(loss_ref)
            dg_ref[...] = jnp.zeros_like(dg_ref)

        x = x_ref[...]
        g = g_ref[...]
        y = (x * _rms_r(x, d)) * g
        err = y - t_ref[...]
        loss_ref[...] += jnp.sum(jnp.sum(err * err, axis=1, keepdims=True), axis=0, keepdims=True) * (0.5 / d)
        dx, dg = _rms_bwd(x, g, err * (1.0 / d), d)
        dx_ref[...] = dx
        dxb_ref[...] = dx.astype(BF16)
        dg_ref[...] += dg

    row = pl.BlockSpec((tm, d), lambda i: (i, 0))
    vec = pl.BlockSpec((1, d), lambda i: (0, 0))
    return pl.pallas_call(
        body, name="final_loss", grid=(T // tm,), in_specs=[row, vec, row],
        out_specs=[row, row, pl.BlockSpec((1, LANES), lambda i: (0, 0)), vec],
        out_shape=[jax.ShapeDtypeStruct((T, d), F32), jax.ShapeDtypeStruct((T, d), BF16),
                   jax.ShapeDtypeStruct((1, LANES), F32), jax.ShapeDtypeStruct((1, d), F32)],
        compiler_params=_params(("arbitrary",)),
    )(x2, g, target)


def rope_tab_call(pos, inv_freq, tm=512):
    T = pos.shape[0]

    def body(p_ref, f_ref, c_ref, s_ref):
        ang = p_ref[...].astype(F32) * f_ref[...]
        lane = lax.broadcasted_iota(jnp.int32, ang.shape, 1)
        sn = jnp.sin(ang)
        c_ref[...] = jnp.cos(ang)
        s_ref[...] = jnp.where((lane & 31) < 16, -sn, sn)

    row = pl.BlockSpec((tm, LANES), lambda i: (i, 0))
    return pl.pallas_call(
        body, name="rope_tab", grid=(T // tm,),
        in_specs=[pl.BlockSpec((tm, 1), lambda i: (i, 0)), pl.BlockSpec((1, LANES), lambda i: (0, 0))],
        out_specs=[row, row],
        out_shape=[jax.ShapeDtypeStruct((T, LANES), F32)] * 2,
        compiler_params=_params(("parallel",)),
    )(pos, inv_freq)


def mla_prep_fwd_call(p, cos, sin, g_cq, g_ckv, w_uq_p, w_ukv_p, tm=512):
    T = p.shape[0]

    def body(cq_ref, ckvr_ref, c_ref, s_ref, gq_ref, gkv_ref, wq_ref, wkv_ref,
             qn_ref, qr_ref, kn_ref, vm_ref, krt_ref, cqn_ref, ckvn_ref):
        c = c_ref[...]
        s = s_ref[...]
        cq = cq_ref[...]
        cqn = ((cq * _rms_r(cq, Q_RANK)) * gq_ref[...]).astype(BF16)
        cqn_ref[...] = cqn
        q = _nn(cqn, wq_ref[...])
        qn_ref[...] = q[:, :ATT_W].astype(BF16)
        for g in range(ROPE_W // LANES):
            qr = q[:, ATT_W + g * LANES:ATT_W + (g + 1) * LANES]
            qr_ref[:, g * LANES:(g + 1) * LANES] = (qr * c + _rot(qr) * s).astype(BF16)
        ckvr = ckvr_ref[...]
        ckv = ckvr[:, :KV_RANK]
        ckvn = ((ckv * _rms_r(ckv, KV_RANK)) * gkv_ref[...]).astype(BF16)
        ckvn_ref[...] = ckvn
        kv = _nn(ckvn, wkv_ref[...])
        kn_ref[...] = kv[:, :ATT_W].astype(BF16)
        vm_ref[...] = kv[:, ATT_W:].astype(BF16)
        kr = _fold4(ckvr[:, KV_RANK:])
        krt_ref[...] = (kr * c + _rot(kr) * s).astype(BF16)

    def row(w, j=0):
        return pl.BlockSpec((tm, w), lambda i: (i, j))

    def full(a):
        return pl.BlockSpec(a.shape, lambda i: (0, 0))

    return pl.pallas_call(
        body, name="mla_prep_fwd", grid=(T // tm,),
        in_specs=[row(Q_RANK, 4), row(Q_RANK, 5), row(LANES), row(LANES), full(g_cq), full(g_ckv),
                  full(w_uq_p), full(w_ukv_p)],
        out_specs=[row(ATT_W), row(ROPE_W), row(ATT_W), row(ATT_W), row(LANES), row(Q_RANK), row(KV_RANK)],
        out_shape=[jax.ShapeDtypeStruct((T, w), BF16) for w in (ATT_W, ROPE_W, ATT_W, ATT_W, LANES, Q_RANK, KV_RANK)],
        compiler_params=_params(("parallel",)),
    )(p, p, cos, sin, g_cq, g_ckv, w_uq_p, w_ukv_p)


def mla_prep_bwd_call(p, cos, sin, g_cq, g_ckv, w_uq_p, w_ukv_p, dqn, dqr4, dkn, dvm, dkrt4, tm=512):
    T = p.shape[0]

    def body(cq_ref, ckvr_ref, c_ref, s_ref, gq_ref, gkv_ref, wq_ref, wkv_ref,
             dqn_ref, dqr4_ref, dkn_ref, dvm_ref, dkrt4_ref,
             dcq_ref, dckvr_ref, dq_ref, dkv_ref, dgq_ref, dgkv_ref):
        @pl.when(pl.program_id(0) == 0)
        def _():
            dgq_ref[...] = jnp.zeros_like(dgq_ref)
            dgkv_ref[...] = jnp.zeros_like(dgkv_ref)

        c = c_ref[...]
        s = s_ref[...]
        d4 = dqr4_ref[...]
        dqr = [d4[:, :128] + d4[:, 128:256], d4[:, 256:384] + d4[:, 384:]]
        dqr = [t * c + _rot(t * s) for t in dqr]
        dq = jnp.concatenate([dqn_ref[...]] + dqr, axis=1).astype(BF16)
        dq_ref[...] = dq
        dcq, dgq = _rms_bwd(cq_ref[...], gq_ref[...], _nt(dq, wq_ref[...]), Q_RANK)
        dcq_ref[...] = dcq
        dgq_ref[...] += dgq
        dkv = jnp.concatenate([dkn_ref[...], dvm_ref[...]], axis=1).astype(BF16)
        dkv_ref[...] = dkv
        ckvr = ckvr_ref[...]
        dckv, dgkv = _rms_bwd(ckvr[:, :KV_RANK], gkv_ref[...], _nt(dkv, wkv_ref[...]), KV_RANK)
        dgkv_ref[...] += dgkv
        k4 = dkrt4_ref[...]
        dkr = _fold4(k4[:, :128] + k4[:, 128:256] + k4[:, 256:384] + k4[:, 384:])
        dkr = dkr * c + _rot(dkr * s)
        lane = lax.broadcasted_iota(jnp.int32, dkr.shape, 1)
        dckvr_ref[...] = jnp.concatenate([dckv, jnp.where(lane < ROPE_DIM, dkr, 0.0)], axis=1)

    def row(w, j=0):
        return pl.BlockSpec((tm, w), lambda i: (i, j))

    def full(a):
        return pl.BlockSpec(a.shape, lambda i: (0, 0))

    return pl.pallas_call(
        body, name="mla_prep_bwd", grid=(T // tm,),
        in_specs=[row(Q_RANK, 4), row(Q_RANK, 5), row(LANES), row(LANES), full(g_cq), full(g_ckv),
                  full(w_uq_p), full(w_ukv_p), row(ATT_W), row(ATT_W), row(ATT_W), row(ATT_W), row(ATT_W)],
        out_specs=[row(Q_RANK), row(Q_RANK), row(ATT_W + ROPE_W), row(2 * ATT_W),
                   pl.BlockSpec((1, Q_RANK), lambda i: (0, 0)), pl.BlockSpec((1, KV_RANK), lambda i: (0, 0))],
        out_shape=[jax.ShapeDtypeStruct((T, Q_RANK), F32), jax.ShapeDtypeStruct((T, Q_RANK), F32),
                   jax.ShapeDtypeStruct((T, ATT_W + ROPE_W), BF16), jax.ShapeDtypeStruct((T, 2 * ATT_W), BF16),
                   jax.ShapeDtypeStruct((1, Q_RANK), F32), jax.ShapeDtypeStruct((1, KV_RANK), F32)],
        compiler_params=_params(("arbitrary",)),
    )(p, p, cos, sin, g_cq, g_ckv, w_uq_p, w_ukv_p, dqn, dqr4, dkn, dvm, dkrt4)


def _iota2(shape, axis):
    return lax.broadcasted_iota(jnp.int32, shape, axis)


def _head_masks():
    lane = _iota2((1, LANES), 1)
    return lane < HEAD_DIM, lane >= HEAD_DIM


def _pair(x, masks, dtype=BF16):
    return [jnp.where(m, x, 0.0).astype(dtype) for m in masks]


def _log_gates(z):
    keep = jnp.maximum(z, 0.0) + jnp.log2(1.0 + jnp.exp2(-jnp.abs(z)))
    return z - keep, keep


def _last_row(x):
    return _row_of(x[x.shape[0] - 8:, :], 7)


def _lane_selector(group):
    return jnp.where(_iota2((16, LANES), 1) // group == _iota2((16, LANES), 0), 1.0, 0.0).astype(BF16)


def _rows8(sel_t, x):
    hi = x.astype(BF16)
    r1 = x - hi.astype(F32)
    mid = r1.astype(BF16)
    lo = (r1 - mid.astype(F32)).astype(BF16)
    return _nt(sel_t, hi) + _nt(sel_t, mid) + _nt(sel_t, lo)


def _row_of(x8, j):
    return jnp.sum(jnp.where(_iota2(x8.shape, 0) == j, x8, 0.0), axis=0, keepdims=True)


def sb_fwd_call(p, B, S, ex=None):
    T = B * S
    TQ, TK = ATT_TQ, ATT_TK
    nq = S // TQ

    def body(q_ref, k_ref, v_ref, o_ref, lt_ref):
        qi = pl.program_id(2)
        masks = _head_masks()
        qm = [_pair(q_ref[:, sl] * SB_SCALE2, masks) for sl in PAIR_LANES]
        row = _iota2((TQ, TK), 0)
        col = _iota2((TQ, TK), 1)
        tri = jnp.where(row > col, 1.0, 0.0).astype(BF16)
        tri2 = jnp.concatenate([tri, tri], axis=0)
        vis = col < row
        o_ref[...] = jnp.zeros_like(o_ref)

        def group(k0, pairs, carry, diag):
            heads = [(pp, j) for pp in pairs for j in range(2)]
            n = range(len(heads))
            k = {pp: k_ref[pl.ds(k0, TK), PAIR_LANES[pp]].astype(BF16) for pp in pairs}
            vm = {pp: _pair(v_ref[pl.ds(k0, TK), PAIR_LANES[pp]], masks) for pp in pairs}
            z = [_nt(qm[pp][j], k[pp]) for pp, j in heads]
            if diag:
                z = [jnp.where(vis, x, NEG_BIG) for x in z]
            gates = [_log_gates(x) for x in z]
            lb = [g[0] for g in gates]
            keep = [g[1] for g in gates]
            tail = [_nn(jnp.concatenate(_split2(keep[h]), axis=1), tri2) + carry[h] for h in n]
            a = [jnp.exp2(lb[h] - tail[h]) for h in n]
            ab = [x.astype(BF16) for x in a]
            for i, pp in enumerate(pairs):
                o_ref[:, PAIR_LANES[pp]] += _nn(ab[2 * i], vm[pp][0]) + _nn(ab[2 * i + 1], vm[pp][1])
            return [tail[h][:, 0:1] + keep[h][:, 0:1] for h in n]

        def step(kb, carry, diag):
            k0 = pl.multiple_of(kb * TK, TK)
            out = []
            for g in range(0, ATT_PAIRS, SB_FWD_GROUP):
                out += group(k0, list(range(g, g + SB_FWD_GROUP)), carry[2 * g:2 * (g + SB_FWD_GROUP)], diag)
            return tuple(out)

        zero = jnp.zeros((TQ, 1), F32)
        carry = step(qi, (zero,) * (2 * ATT_PAIRS), True)
        carry = lax.fori_loop(0, qi, lambda i, c: step(qi - 1 - i, c, False), carry)
        lane = _iota2((TQ, LANES), 1)
        for pp, sl in enumerate(PAIR_LANES):
            lt_ref[:, sl] = jnp.where(lane == 0, carry[2 * pp], jnp.where(lane == 1, carry[2 * pp + 1], 0.0))

    W = ATT_PAIRS * LANES
    qspec = pl.BlockSpec((TQ, W), lambda b, h, i: (b * nq + i, h))
    outs, moved = _call(
        body, ex, name="sb_fwd", grid=(B, HEADS // 2 // ATT_PAIRS, nq),
        in_specs=[qspec,
                  pl.BlockSpec((S, W), lambda b, h, i: (b, ATT_W // W + h)),
                  pl.BlockSpec((S, W), lambda b, h, i: (b, 2 * ATT_W // W + h))],
        out_specs=[qspec, qspec],
        out_shape=[jax.ShapeDtypeStruct((T, ATT_W), F32)] * 2, args=(p, p, p))
    return tuple(outs) if ex is None else tuple(outs) + (moved,)


def sb_bwd_call(p, lt, do, B, S, ex=None):
    T = B * S
    TQ, TK = ATT_TQ, ATT_TK
    nq = S // TQ

    def body(q_ref, k_ref, v_ref, lt_ref, do_ref, dq_ref, dk_ref, dv_ref):
        qi = pl.program_id(2)

        @pl.when(qi == 0)
        def _():
            dk_ref[...] = jnp.zeros_like(dk_ref)
            dv_ref[...] = jnp.zeros_like(dv_ref)

        masks = _head_masks()
        qm = [_pair(q_ref[:, sl] * SB_SCALE2, masks) for sl in PAIR_LANES]
        dom = [_pair(do_ref[:, sl], masks) for sl in PAIR_LANES]
        start = []
        for sl in PAIR_LANES:
            l8 = _rows8(_lane_selector(1), lt_ref[:, sl])
            start += [-_row_of(l8, 0), jnp.zeros((1, TQ), F32), -_row_of(l8, 1), jnp.zeros((1, TQ), F32)]
        row = _iota2((TK, TQ), 0)
        col = _iota2((TK, TQ), 1)
        incl = jnp.where(col <= row, 1.0, 0.0).astype(BF16)
        incl2 = jnp.concatenate([incl, incl], axis=1)
        excl = jnp.where(col < row, 1.0, 0.0).astype(BF16)
        vis = row < col
        dq_ref[...] = jnp.zeros_like(dq_ref)

        def group(k0, pairs, carry, diag):
            heads = [(pp, j) for pp in pairs for j in range(2)]
            n = range(len(heads))
            kf = {pp: k_ref[pl.ds(k0, TK), PAIR_LANES[pp]] for pp in pairs}
            km = {pp: _pair(kf[pp], masks) for pp in pairs}
            v = {pp: v_ref[pl.ds(k0, TK), PAIR_LANES[pp]].astype(BF16) for pp in pairs}
            z = [_nt(kf[pp].astype(BF16), qm[pp][j]) for pp, j in heads]
            da = [_nt(v[pp], dom[pp][j]) for pp, j in heads]
            if diag:
                z = [jnp.where(vis, x, NEG_BIG) for x in z]
            gates = [_log_gates(x) for x in z]
            lb = [g[0] for g in gates]
            keep = [g[1] for g in gates]
            left = [_nn(incl2, jnp.concatenate(_split2(keep[h]), axis=0)) + carry[2 * h] for h in n]
            a = [jnp.exp2(lb[h] + left[h]) for h in n]
            e = [a[h] * da[h] for h in n]
            before = [_nn(excl, e[h].astype(BF16)) + carry[2 * h + 1] for h in n]
            dz = [e[h] - jnp.exp2(lb[h]) * (e[h] + before[h]) for h in n]
            dzb = [x.astype(BF16) for x in dz]
            ab = [x.astype(BF16) for x in a]
            out = []
            for h in n:
                out += [_last_row(left[h]), _last_row(before[h]) + _last_row(e[h])]
            for i, pp in enumerate(pairs):
                sl = PAIR_LANES[pp]
                dk_ref[pl.ds(k0, TK), sl] += _nn(dzb[2 * i], qm[pp][0]) + _nn(dzb[2 * i + 1], qm[pp][1])
                dv_ref[pl.ds(k0, TK), sl] += _nn(ab[2 * i], dom[pp][0]) + _nn(ab[2 * i + 1], dom[pp][1])
                dq_ref[:, sl] += _tn(dzb[2 * i], km[pp][0]) + _tn(dzb[2 * i + 1], km[pp][1])
            return out

        def step(kb, carry, diag):
            k0 = pl.multiple_of(kb * TK, TK)
            out = []
            for g in range(0, ATT_PAIRS, SB_BWD_GROUP):
                out += group(k0, list(range(g, g + SB_BWD_GROUP)), carry[4 * g:4 * (g + SB_BWD_GROUP)], diag)
            return tuple(out)

        carry = lax.fori_loop(0, qi, lambda i, c: step(i, c, False), tuple(start))
        step(qi, carry, True)
        dq_ref[...] *= SB_SCALE

        @pl.when(qi == nq - 1)
        def _():
            dk_ref[...] *= LN2

    W = ATT_PAIRS * LANES
    qspec = pl.BlockSpec((TQ, W), lambda b, h, i: (b * nq + i, h))
    sspec = pl.BlockSpec((S, W), lambda b, h, i: (b, h))
    outs, moved = _call(
        body, ex, name="sb_bwd", grid=(B, HEADS // 2 // ATT_PAIRS, nq),
        in_specs=[qspec,
                  pl.BlockSpec((S, W), lambda b, h, i: (b, ATT_W // W + h)),
                  pl.BlockSpec((S, W), lambda b, h, i: (b, 2 * ATT_W // W + h)),
                  qspec, qspec],
        out_specs=[qspec, sspec, sspec],
        out_shape=[jax.ShapeDtypeStruct((T, ATT_W), F32)] * 3, args=(p, p, p, lt, do))
    return tuple(outs) if ex is None else tuple(outs) + (moved,)


ALL_PAIRS = [slice(i * LANES, (i + 1) * LANES) for i in range(HEADS // 2)]


def _rope_masks(hp):
    grp = _iota2((1, LANES), 1) // ROPE_DIM
    return [grp == ((2 * hp + j) % 4) for j in range(2)]


def _mla_queries(qn_ref, qr_ref, masks):
    out = []
    for pp, sl in enumerate(ALL_PAIRS):
        qnv = qn_ref[:, sl]
        qrv = qr_ref[:, ALL_PAIRS[pp // 2]]
        rmasks = _rope_masks(pp)
        out.append([jnp.concatenate([jnp.where(masks[j], qnv, 0), jnp.where(rmasks[j], qrv, 0)], axis=1).astype(BF16)
                    for j in range(2)])
    return out


def mla_fwd_call(qn, qr, kn, krt, vm, B, S):
    T = B * S
    TQ, TK = ATT_TQ, ATT_TK
    nq = S // TQ

    def body(qn_ref, qr_ref, kn_ref, kr_ref, v_ref, o_ref, lse_ref):
        qi = pl.program_id(1)
        masks = _head_masks()
        qcat = _mla_queries(qn_ref, qr_ref, masks)
        row = _iota2((TQ, TK), 0)
        col = _iota2((TQ, TK), 1)
        vis = col <= row
        o_ref[...] = jnp.zeros_like(o_ref)

        def group(k0, pairs, carry, diag):
            heads = [(pp, j) for pp in pairs for j in range(2)]
            n = range(len(heads))
            krv = kr_ref[pl.ds(k0, TK), :]
            kcat = {pp: jnp.concatenate([kn_ref[pl.ds(k0, TK), ALL_PAIRS[pp]], krv], axis=1) for pp in pairs}
            vmk = {pp: _pair(v_ref[pl.ds(k0, TK), ALL_PAIRS[pp]], masks) for pp in pairs}
            s = [_nt(qcat[pp][j], kcat[pp]) * MLA_SCALE2 for pp, j in heads]
            if diag:
                s = [jnp.where(vis, x, NEG_BIG) for x in s]
            m_new = [jnp.maximum(carry[2 * h], jnp.max(s[h], axis=1, keepdims=True)) for h in n]
            alpha = [jnp.exp2(carry[2 * h] - m_new[h]) for h in n]
            pexp = [jnp.exp2(s[h] - m_new[h]) for h in n]
            out = []
            for h in n:
                out += [m_new[h], alpha[h] * carry[2 * h + 1] + jnp.sum(pexp[h], axis=1, keepdims=True)]
            pb = [x.astype(BF16) for x in pexp]
            for i, pp in enumerate(pairs):
                sl = ALL_PAIRS[pp]
                scale = jnp.where(masks[0], alpha[2 * i], alpha[2 * i + 1])
                o_ref[:, sl] = o_ref[:, sl] * scale + (_nn(pb[2 * i], vmk[pp][0]) + _nn(pb[2 * i + 1], vmk[pp][1]))
            return out

        def step(kb, carry, diag):
            k0 = pl.multiple_of(kb * TK, TK)
            out = []
            for g in range(0, len(ALL_PAIRS), MLA_GROUP):
                out += group(k0, list(range(g, g + MLA_GROUP)), carry[4 * g:4 * (g + MLA_GROUP)], diag)
            return tuple(out)

        neg = jnp.full((TQ, 1), NEG_BIG, F32)
        zero = jnp.zeros((TQ, 1), F32)
        carry = step(qi, (neg, zero) * (2 * len(ALL_PAIRS)), True)
        carry = lax.fori_loop(0, qi, lambda i, c: step(qi - 1 - i, c, False), carry)
        lane = _iota2((TQ, LANES), 1)
        for pp, sl in enumerate(ALL_PAIRS):
            m0, l0, m1, l1 = carry[4 * pp:4 * pp + 4]
            o_ref[:, sl] = o_ref[:, sl] * jnp.where(masks[0], 1.0 / l0, 1.0 / l1)
            lse_ref[:, sl] = jnp.where(lane == 0, m0 * LN2 + jnp.log(l0), jnp.where(lane == 1, m1 * LN2 + jnp.log(l1), 0.0))

    def rows(w):
        return pl.BlockSpec((TQ, w), lambda b, i: (b * nq + i, 0))

    def seq(w):
        return pl.BlockSpec((S, w), lambda b, i: (b, 0))

    return pl.pallas_call(
        body, name="mla_fwd", grid=(B, nq),
        in_specs=[rows(ATT_W), rows(ROPE_W), seq(ATT_W), seq(LANES), seq(ATT_W)],
        out_specs=[rows(ATT_W), rows(ATT_W)],
        out_shape=[jax.ShapeDtypeStruct((T, ATT_W), F32)] * 2,
        compiler_params=_params(("arbitrary", "arbitrary")),
    )(qn, qr, kn, krt, vm)


def mla_bwd_call(qn, qr, kn, krt, vm, o, lse, do, B, S, ex=None):
    T = B * S
    TQ, TK = ATT_TQ, ATT_TK
    nq = S // TQ

    def body(qn_ref, qr_ref, kn_ref, kr_ref, v_ref, o_ref, lse_ref, do_ref,
             dqn_ref, dqr_ref, dkn_ref, dv_ref, dkr_ref):
        qi = pl.program_id(1)

        @pl.when(qi == 0)
        def _():
            dkn_ref[...] = jnp.zeros_like(dkn_ref)
            dv_ref[...] = jnp.zeros_like(dv_ref)
            dkr_ref[...] = jnp.zeros_like(dkr_ref)

        masks = _head_masks()
        qcat = _mla_queries(qn_ref, qr_ref, masks)
        dom, dsum, lse = [], [], []
        for sl in ALL_PAIRS:
            do = do_ref[:, sl]
            dom.append(_pair(do, masks))
            d8 = _rows8(_lane_selector(HEAD_DIM), do * o_ref[:, sl])
            l8 = _rows8(_lane_selector(1), lse_ref[:, sl])
            dsum.append([_row_of(d8, j) for j in range(2)])
            lse.append([_row_of(l8, j) * LOG2E for j in range(2)])
        row = _iota2((TK, TQ), 0)
        col = _iota2((TK, TQ), 1)
        vis = row <= col
        dqn_ref[...] = jnp.zeros_like(dqn_ref)
        dqr_ref[...] = jnp.zeros_like(dqr_ref)

        def group(k0, pairs, diag):
            heads = [(pp, j) for pp in pairs for j in range(2)]
            n = range(len(heads))
            krv = kr_ref[pl.ds(k0, TK), :]
            knv = {pp: kn_ref[pl.ds(k0, TK), ALL_PAIRS[pp]] for pp in pairs}
            kcat = {pp: jnp.concatenate([knv[pp], krv], axis=1) for pp in pairs}
            v = {pp: v_ref[pl.ds(k0, TK), ALL_PAIRS[pp]] for pp in pairs}
            s = [_nt(kcat[pp], qcat[pp][j]) * MLA_SCALE2 for pp, j in heads]
            dp_ = [_nt(v[pp], dom[pp][j]) for pp, j in heads]
            pr = [jnp.exp2(s[h] - lse[pp][j]) for h, (pp, j) in enumerate(heads)]
            if diag:
                pr = [jnp.where(vis, x, 0.0) for x in pr]
            ds = [(pr[h] * (dp_[h] - dsum[pp][j]) * MLA_SCALE).astype(BF16) for h, (pp, j) in enumerate(heads)]
            pb = [x.astype(BF16) for x in pr]
            for i, pp in enumerate(pairs):
                sl = ALL_PAIRS[pp]
                rmasks = _rope_masks(pp)
                kcat_j = [jnp.concatenate([jnp.where(masks[j], knv[pp], 0), jnp.where(rmasks[j], krv, 0)],
                                          axis=1).astype(BF16) for j in range(2)]
                dv_ref[pl.ds(k0, TK), sl] += _nn(pb[2 * i], dom[pp][0]) + _nn(pb[2 * i + 1], dom[pp][1])
                dk = _nn(ds[2 * i], qcat[pp][0]) + _nn(ds[2 * i + 1], qcat[pp][1])
                dq = _tn(ds[2 * i], kcat_j[0]) + _tn(ds[2 * i + 1], kcat_j[1])
                dqn_ref[:, sl] += dq[:, :LANES]
                dqr_ref[:, sl] += dq[:, LANES:]
                dkn_ref[pl.ds(k0, TK), sl] += dk[:, :LANES]
                dkr_ref[pl.ds(k0, TK), sl] += dk[:, LANES:]

        def step(kb, diag):
            k0 = pl.multiple_of(kb * TK, TK)
            for g in range(0, len(ALL_PAIRS), MLA_GROUP):
                group(k0, list(range(g, g + MLA_GROUP)), diag)

        step(qi, True)

        def loop(i, c):
            step(qi - 1 - i, False)
            return c

        lax.fori_loop(0, qi, loop, 0)

    def rows(w):
        return pl.BlockSpec((TQ, w), lambda b, i: (b * nq + i, 0))

    def seq(w):
        return pl.BlockSpec((S, w), lambda b, i: (b, 0))

    outs, moved = _call(
        body, ex, name="mla_bwd", grid=(B, nq),
        in_specs=[rows(ATT_W), rows(ROPE_W), seq(ATT_W), seq(LANES), seq(ATT_W), rows(ATT_W), rows(ATT_W), rows(ATT_W)],
        out_specs=[rows(ATT_W), rows(ATT_W), seq(ATT_W), seq(ATT_W), seq(ATT_W)],
        out_shape=[jax.ShapeDtypeStruct((T, ATT_W), F32)] * 5, args=(qn, qr, kn, krt, vm, o, lse, do))
    return tuple(outs) if ex is None else tuple(outs) + (moved,)


CONV_TC = 256


def _shift_down(x, n):
    return jnp.where(_iota2(x.shape, 0) >= n, pltpu.roll(x, n, 0), 0.0)


def _shift_up(x, n):
    rows = x.shape[0]
    return jnp.where(_iota2(x.shape, 0) < rows - n, pltpu.roll(x, rows - n, 0), 0.0)


def _taps(w_ref):
    return [w_ref[k:k + 1, :] for k in range(3)]


def _conv3(u, w, b):
    return w[0] * _shift_down(u, 2) + w[1] * _shift_down(u, 1) + w[2] * u + b


def _ref_shift_down(ref, n):
    rows = ref.shape[0]
    return jnp.concatenate([_shift_down(ref[0:8, :], n), ref[8 - n:rows - n, :]], axis=0)


def _conv3_ref(u_ref, w, b):
    return w[0] * _ref_shift_down(u_ref, 2) + w[1] * _ref_shift_down(u_ref, 1) + w[2] * u_ref[...] + b


def conv_act_fwd_call(ug, uv, conv_w, conv_b, B, S):
    T = B * S
    nc = D_FF // CONV_TC

    def body(ug_ref, uv_ref, wg_ref, wv_ref, bg_ref, bv_ref, a_ref, cg_ref, cv_ref):
        gate = _conv3_ref(ug_ref, _taps(wg_ref), bg_ref[...])
        val = _conv3_ref(uv_ref, _taps(wv_ref), bv_ref[...])
        a_ref[...] = (gate * (1.0 / (1.0 + jnp.exp(-gate))) * val).astype(BF16)
        cg_ref[...] = gate.astype(BF16)
        cv_ref[...] = val.astype(BF16)

    def blk(rows, off):
        return pl.BlockSpec((rows, CONV_TC), lambda b, j: (b if rows == S else 0, off + j))

    return pl.pallas_call(
        body, name="conv_act_fwd", grid=(B, nc),
        in_specs=[blk(S, 0), blk(S, 0), blk(3, 0), blk(3, nc), blk(1, 0), blk(1, nc)],
        out_specs=[blk(S, 0)] * 3,
        out_shape=[jax.ShapeDtypeStruct((T, D_FF), BF16)] * 3,
        compiler_params=_params(("parallel", "parallel")),
    )(ug, uv, conv_w, conv_w, conv_b, conv_b)


def conv_act_bwd_call(ug, uv, cg, cv, dx2, w_down, conv_w, B, S):
    T = B * S
    nc = D_FF // CONV_TC

    def body(ug_ref, uv_ref, cg_ref, cv_ref, dx_ref, wd_ref, wg_ref, wv_ref,
             dug_ref, duv_ref, dwg_ref, dwv_ref, dbg_ref, dbv_ref):
        @pl.when(pl.program_id(1) == 0)
        def _():
            for r in (dwg_ref, dwv_ref, dbg_ref, dbv_ref):
                r[...] = jnp.zeros_like(r)

        gate = cg_ref[...].astype(F32)
        val = cv_ref[...].astype(F32)
        da = _nt(dx_ref[...], wd_ref[...])
        sig = 1.0 / (1.0 + jnp.exp(-gate))
        dval = da * (gate * sig)
        dgate = da * val * (sig * (1.0 + gate * (1.0 - sig)))
        for u_ref, d, w, du_ref, dw_ref, db_ref in ((ug_ref, dgate, _taps(wg_ref), dug_ref, dwg_ref, dbg_ref),
                                                   (uv_ref, dval, _taps(wv_ref), duv_ref, dwv_ref, dbv_ref)):
            u_ = u_ref[...]
            d1 = _shift_up(d, 1)
            d2 = _shift_up(d, 2)
            du_ref[...] = (w[2] * d + w[1] * d1 + w[0] * d2).astype(BF16)
            db_ref[...] += jnp.sum(d, axis=0, keepdims=True)
            dw_ref[0:1, :] += jnp.sum(d2 * u_, axis=0, keepdims=True)
            dw_ref[1:2, :] += jnp.sum(d1 * u_, axis=0, keepdims=True)
            dw_ref[2:3, :] += jnp.sum(d * u_, axis=0, keepdims=True)

    def blk(rows, off):
        return pl.BlockSpec((rows, CONV_TC), lambda j, b: (b if rows == S else 0, off + j))

    return pl.pallas_call(
        body, name="conv_act_bwd", grid=(nc, B),
        in_specs=[blk(S, 0), blk(S, 0), blk(S, 0), blk(S, 0), pl.BlockSpec((S, D_MODEL), lambda j, b: (b, 0)),
                  pl.BlockSpec((CONV_TC, D_MODEL), lambda j, b: (j, 0)), blk(3, 0), blk(3, nc)],
        out_specs=[blk(S, 0), blk(S, 0), blk(3, 0), blk(3, 0), blk(1, 0), blk(1, 0)],
        out_shape=[jax.ShapeDtypeStruct((T, D_FF), BF16), jax.ShapeDtypeStruct((T, D_FF), BF16),
                   jax.ShapeDtypeStruct((3, D_FF), F32), jax.ShapeDtypeStruct((3, D_FF), F32),
                   jax.ShapeDtypeStruct((1, D_FF), F32), jax.ShapeDtypeStruct((1, D_FF), F32)],
        compiler_params=_params(("parallel", "arbitrary")),
    )(ug, uv, cg, cv, dx2, w_down, conv_w, conv_w)


CHIP_MASKS = ((1, 0), (0, 1), (1, 1))


def _place():
    return lax.axis_index("x"), lax.axis_index("y"), lax.axis_index("c")


HALF_ALIGN = 32


def _any_specs(n):
    return [pl.BlockSpec(memory_space=pl.ANY)] * n


def _splits(shape):
    r, c = shape
    return "rows" if r % HALF_ALIGN == 0 else "cols" if c % (2 * LANES) == 0 else None


def _half(shape, half):
    r, c = shape
    how = _splits(shape)
    if how == "rows":
        return (pl.ds(pl.multiple_of(half * (r // 2), HALF_ALIGN // 2), r // 2), slice(None))
    if how == "cols":
        return (slice(None), pl.ds(pl.multiple_of(half * (c // 2), LANES), c // 2))
    return (slice(None), slice(None))


def _half_shape(shape):
    r, c = shape
    return {"rows": (r // 2, c), "cols": (r, c // 2)}[_splits(shape)]


def _remote(src, dst, send_sem, recv_sem, device):
    return pltpu.make_async_remote_copy(src_ref=src, dst_ref=dst, send_sem=send_sem, recv_sem=recv_sem,
                                        device_id=device, device_id_type=MESH)


class Exchange:
    def __init__(self, ins, out_shape, sems, start, finish):
        self.ins, self.out_shape, self.sems, self.start, self.finish = list(ins), list(out_shape), list(sems), start, finish


def gather_group(shards):
    n = len(shards)
    split = [_splits(s.shape) is not None for s in shards]

    def part(w, half):
        return _half(shards[w].shape, half)

    def copies(ins, outs, sems):
        ici_s, ici_r, _, _, local_sems = sems
        x, y, c = _place()
        chip = 2 * x + y
        local = [pltpu.make_async_copy(ins[w], outs[w].at[chip], local_sems.at[w]) for w in range(n)]
        sends = [_remote(ins[w].at[part(w, c)], outs[w].at[(chip,) + part(w, c)], ici_s.at[w, k], ici_r.at[w, k],
                         (x ^ fx, y ^ fy, c))
                 for w in range(n) for k, (fx, fy) in enumerate(CHIP_MASKS)]
        return local, sends

    def start(ins, outs, sems):
        local, sends = copies(ins, outs, sems)
        for cp in local + sends:
            cp.start()

    def finish(ins, outs, sems):
        ici_s, ici_r, d2d_s, d2d_r, _ = sems
        x, y, c = _place()
        sib = (x, y, 1 - c)
        local, sends = copies(ins, outs, sems)
        for w in range(n):
            for k, (fx, fy) in enumerate(CHIP_MASKS):
                landed = outs[w].at[(2 * (x ^ fx) + (y ^ fy),) + part(w, c)]
                _remote(landed, landed, ici_s.at[w, k], ici_r.at[w, k], sib).wait_recv()
                if split[w]:
                    cp = _remote(landed, landed, d2d_s.at[w, k], d2d_r.at[w, k], sib)
                    cp.start()
                    sends.append(cp)
        for w in range(n):
            for k, (fx, fy) in enumerate(CHIP_MASKS):
                if split[w]:
                    other = outs[w].at[(2 * (x ^ fx) + (y ^ fy),) + part(w, 1 - c)]
                    _remote(other, other, d2d_s.at[w, k], d2d_r.at[w, k], sib).wait_recv()
        for cp in sends:
            cp.wait_send()
        for cp in local:
            cp.wait()

    sems = pltpu.SemaphoreType.DMA((n, 3))
    return Exchange(shards, [jax.ShapeDtypeStruct((N_CHIPS,) + s.shape, s.dtype) for s in shards],
                    [sems, sems, sems, sems, pltpu.SemaphoreType.DMA((n,))], start, finish)


def swap_half(parts):
    n = len(parts)

    def copies(ins, outs, sems):
        x, y, c = _place()
        return [_remote(ins[w].at[(slice(None),) + _half(parts[w].shape[1:], 1 - c)], outs[w], sems[0].at[w], sems[1].at[w],
                        (x, y, 1 - c)) for w in range(n)]

    def start(ins, outs, sems):
        for cp in copies(ins, outs, sems):
            cp.start()

    def finish(ins, outs, sems):
        for cp in copies(ins, outs, sems):
            cp.wait_recv()
            cp.wait_send()

    return Exchange(parts, [jax.ShapeDtypeStruct((N_CHIPS,) + _half_shape(p.shape[1:]), F32) for p in parts],
                    [pltpu.SemaphoreType.DMA((n,))] * 2, start, finish)


def scatter_half(halves):
    n = len(halves)

    def copies(ins, outs, sems):
        x, y, c = _place()
        return [_remote(ins[w].at[2 * (x ^ fx) + (y ^ fy)], outs[w].at[k], sems[0].at[w, k], sems[1].at[w, k],
                        (x ^ fx, y ^ fy, c))
                for w in range(n) for k, (fx, fy) in enumerate(CHIP_MASKS)]

    def start(ins, outs, sems):
        for cp in copies(ins, outs, sems):
            cp.start()

    def finish(ins, outs, sems):
        for cp in copies(ins, outs, sems):
            cp.wait_recv()
            cp.wait_send()

    return Exchange(halves, [jax.ShapeDtypeStruct((3,) + h.shape[1:], h.dtype) for h in halves],
                    [pltpu.SemaphoreType.DMA((n, 3))] * 2, start, finish)


def swap_final(finals, shapes):
    n = len(finals)

    def copies(ins, outs, sems):
        x, y, c = _place()
        mine = [outs[w].at[_half(shapes[w], c)] for w in range(n)]
        local = [pltpu.make_async_copy(ins[w], mine[w], sems[2].at[w]) for w in range(n)]
        sends = [_remote(ins[w], mine[w], sems[0].at[w], sems[1].at[w], (x, y, 1 - c)) for w in range(n)]
        return local, sends

    def start(ins, outs, sems):
        local, sends = copies(ins, outs, sems)
        for cp in local + sends:
            cp.start()

    def finish(ins, outs, sems):
        x, y, c = _place()
        local, sends = copies(ins, outs, sems)
        for w in range(n):
            got = outs[w].at[_half(shapes[w], 1 - c)]
            _remote(got, got, sems[0].at[w], sems[1].at[w], (x, y, 1 - c)).wait_recv()
        for cp in sends:
            cp.wait_send()
        for cp in local:
            cp.wait()

    return Exchange(finals, [jax.ShapeDtypeStruct(tuple(s), F32) for s in shapes],
                    [pltpu.SemaphoreType.DMA((n,))] * 3, start, finish)


def exchange_call(name, ex):
    n, m = len(ex.ins), len(ex.out_shape)

    def body(*refs):
        ins, outs, sems = refs[:n], refs[n:n + m], refs[n + m:]
        ex.start(ins, outs, sems)
        ex.finish(ins, outs, sems)

    return pl.pallas_call(body, name=name, in_specs=_any_specs(n), out_specs=_any_specs(m), out_shape=ex.out_shape,
                          scratch_shapes=ex.sems, compiler_params=_params())(*ex.ins)


def _call(body, ex, *, name, grid, in_specs, out_specs, out_shape, args, scratch_shapes=()):
    sem = ("arbitrary",) * len(grid)
    if ex is None:
        outs = pl.pallas_call(body, name=name, grid=grid, in_specs=in_specs, out_specs=out_specs, out_shape=out_shape,
                              scratch_shapes=list(scratch_shapes), compiler_params=_params(sem))(*args)
        return outs, None
    ni, no, ns = len(in_specs), len(out_specs), len(scratch_shapes)
    ne, me = len(ex.ins), len(ex.out_shape)

    def wrapped(*refs):
        own_in, ex_in = refs[:ni], refs[ni:ni + ne]
        own_out, ex_out = refs[ni + ne:ni + ne + no], refs[ni + ne + no:ni + ne + no + me]
        own_scr, ex_sems = refs[ni + ne + no + me:ni + ne + no + me + ns], refs[ni + ne + no + me + ns:]
        ids = [pl.program_id(a) for a in range(len(grid))]
        first = functools.reduce(jnp.logical_and, [i == 0 for i in ids])
        last = functools.reduce(jnp.logical_and, [i == g - 1 for i, g in zip(ids, grid)])

        @pl.when(first)
        def _():
            ex.start(ex_in, ex_out, ex_sems)

        body(*own_in, *own_out, *own_scr)

        @pl.when(last)
        def _():
            ex.finish(ex_in, ex_out, ex_sems)

    outs = pl.pallas_call(
        wrapped, name=name, grid=grid, in_specs=list(in_specs) + _any_specs(ne),
        out_specs=list(out_specs) + _any_specs(me), out_shape=list(out_shape) + ex.out_shape,
        scratch_shapes=list(scratch_shapes) + ex.sems, compiler_params=_params(sem))(*args, *ex.ins)
    return outs[:no], outs[no:]


def _row_tile(rows, cap, mult=8):
    return max([t for t in range(mult, min(rows, cap) + 1, mult) if rows % t == 0] or [rows])


def add_half_call(name, part, got, where):
    _, rh, cols = got.shape
    tr = _row_tile(rh, 176, 16)
    nb = rh // tr
    by_rows = _splits(part.shape[1:]) == "rows"

    def body(where_ref, p_ref, g_ref, own_ref, send_ref):
        t = p_ref[...] + g_ref[...]
        send_ref[...] = t.astype(BF16)
        chip = where_ref[1]
        own_ref[...] = p_ref[chip] + g_ref[chip]

    blk = (N_CHIPS, tr, cols)
    return pl.pallas_call(
        body, name=name,
        grid_spec=pltpu.PrefetchScalarGridSpec(
            num_scalar_prefetch=1, grid=(nb,),
            in_specs=[pl.BlockSpec(blk, (lambda i, where_ref: (0, where_ref[0] * nb + i, 0)) if by_rows
                                   else (lambda i, where_ref: (0, i, where_ref[0]))),
                      pl.BlockSpec(blk, lambda i, where_ref: (0, i, 0))],
            out_specs=[pl.BlockSpec((tr, cols), lambda i, where_ref: (i, 0)),
                       pl.BlockSpec(blk, lambda i, where_ref: (0, i, 0))]),
        out_shape=[jax.ShapeDtypeStruct((rh, cols), F32), jax.ShapeDtypeStruct(got.shape, BF16)],
        compiler_params=_params(("parallel",)),
    )(where, part, got)


def sum_chips_call(name, own, got):
    _, rh, cols = got.shape
    tr = _row_tile(rh, 176, 16)

    def body(h_ref, g_ref, o_ref):
        o_ref[...] = ((h_ref[...] + g_ref[0].astype(F32)) + g_ref[1].astype(F32)) + g_ref[2].astype(F32)

    return pl.pallas_call(
        body, name=name, grid=(rh // tr,),
        in_specs=[pl.BlockSpec((tr, cols), lambda i: (i, 0)), pl.BlockSpec((3, tr, cols), lambda i: (0, i, 0))],
        out_specs=pl.BlockSpec((tr, cols), lambda i: (i, 0)),
        out_shape=jax.ShapeDtypeStruct((rh, cols), F32),
        compiler_params=_params(("parallel",)),
    )(own, got)


def _adamw(w, g, m, v):
    m = ADAM_B1 * m + (1.0 - ADAM_B1) * g
    v = ADAM_B2 * v + (1.0 - ADAM_B2) * (g * g)
    m_hat = m / (1.0 - ADAM_B1 ** ADAM_STEP)
    v_hat = v / (1.0 - ADAM_B2 ** ADAM_STEP)
    delta = -ADAM_LR * (m_hat / (jnp.sqrt(v_hat) + ADAM_EPS) + ADAM_WD * w)
    return delta, m, v


def adamw_call(name, g, w, m, v):
    r, cols = w.shape
    tr = r if r % 8 else _row_tile(r, 256)

    def body(g_ref, w_ref, m_ref, v_ref, go_ref, d_ref, nm_ref, nv_ref):
        g = g_ref[...]
        go_ref[...] = g
        d_ref[...], nm_ref[...], nv_ref[...] = _adamw(w_ref[...], g, m_ref[...], v_ref[...])

    spec = pl.BlockSpec((tr, cols), lambda i: (i, 0))
    return pl.pallas_call(
        body, name=name, grid=(r // tr,), in_specs=[spec] * 4, out_specs=[spec] * 4,
        out_shape=[jax.ShapeDtypeStruct((r, cols), F32)] * 4,
        compiler_params=_params(("parallel",)),
    )(g, w, m, v)


def allsum_small_call(v):
    R = v.shape[0]

    def body(v_ref, out_ref, buf, send_sems, recv_sems):
        x, y, c = _place()
        me = 4 * x + 2 * y + c
        buf[me] = v_ref[...]
        sends = []
        for k in range(1, N_DEV):
            fx, fy, fc = (k >> 2) & 1, (k >> 1) & 1, k & 1
            cp = pltpu.make_async_remote_copy(
                src_ref=v_ref, dst_ref=buf.at[me], send_sem=send_sems.at[k - 1], recv_sem=recv_sems.at[k - 1],
                device_id=(x ^ fx, y ^ fy, c ^ fc), device_id_type=MESH)
            cp.start()
            sends.append(cp)
        for k in range(1, N_DEV):
            pltpu.make_async_remote_copy(
                src_ref=v_ref, dst_ref=buf.at[me ^ k], send_sem=send_sems.at[k - 1], recv_sem=recv_sems.at[k - 1],
                device_id=(x, y, c), device_id_type=MESH).wait_recv()
        acc = buf[0]
        for d in range(1, N_DEV):
            acc = acc + buf[d]
        out_ref[...] = acc
        for cp in sends:
            cp.wait_send()

    vm = pl.BlockSpec(memory_space=pltpu.VMEM)
    return pl.pallas_call(
        body, name="allsum_small", in_specs=[vm], out_specs=vm,
        out_shape=jax.ShapeDtypeStruct((R, LANES), F32),
        scratch_shapes=[pltpu.VMEM((N_DEV, R, LANES), F32), pltpu.SemaphoreType.DMA((N_DEV - 1,)),
                        pltpu.SemaphoreType.DMA((N_DEV - 1,))],
        compiler_params=_params(),
    )(v)


def _slab(flat, mult):
    n = flat.shape[-1]
    rows = -(-n // (LANES * mult)) * mult
    flat = jnp.pad(flat, [(0, 0)] * (flat.ndim - 1) + [(0, rows * LANES - n)])
    return flat.reshape(flat.shape[:-1] + (rows, LANES))


def full_from_chips(blocks, by_col):
    _, r, c = blocks.shape
    return blocks.transpose(1, 0, 2).reshape(r, N_CHIPS * c) if by_col else blocks.reshape(N_CHIPS * r, c)


def chips_from_full(full, by_col):
    if by_col:
        r, c = full.shape[0], full.shape[1] // N_CHIPS
        return full.reshape(r, N_CHIPS, c).transpose(1, 0, 2)
    return full.reshape(N_CHIPS, full.shape[0] // N_CHIPS, full.shape[1])


SMALL_PACK = SMALL_W + ("loss", "conv_w")
SMALL_PACK_N = {**SMALL_N, "loss": 1, "conv_w": 3 * 2 * D_FF}


def pack_small(vals):
    zero = jnp.zeros((1,), F32)
    return _slab(jnp.concatenate([vals[n].reshape(-1) if n in vals else jnp.tile(zero, SMALL_PACK_N[n])
                                  for n in SMALL_PACK]), 8)


def unpack_small(slab, shapes):
    flat = slab.reshape(-1)
    out, off = {}, 0
    for n in SMALL_PACK:
        out[n] = flat[off:off + SMALL_PACK_N[n]].reshape(shapes[n])
        off += SMALL_PACK_N[n]
    return out


def _split_heads(w, a, b):
    r = w.shape[0]
    w3 = w.reshape(r, HEADS, a + b)
    return w3[:, :, :a].reshape(r, HEADS * a), w3[:, :, a:].reshape(r, HEADS * b)


def _merge_heads(wa, wb, a, b):
    r = wa.shape[0]
    return jnp.concatenate([wa.reshape(r, HEADS, a), wb.reshape(r, HEADS, b)], axis=2).reshape(r, HEADS * (a + b))


def kernel(x, positions, g_mix, w_in, g_cq, w_uq, g_ckv, w_ukv, g_sb_out, g_mla_out, w_out, g_ffn, w_up, conv_w, conv_b, w_down, g_final, loss_target, m_g_mix, m_w_in, m_g_cq, m_w_uq, m_g_ckv, m_w_ukv, m_g_sb_out, m_g_mla_out, m_w_out, m_g_ffn, m_w_up, m_conv_w, m_conv_b, m_w_down, m_g_final, v_g_mix, v_w_in, v_g_cq, v_w_uq, v_g_ckv, v_w_ukv, v_g_sb_out, v_g_mla_out, v_w_out, v_g_ffn, v_w_up, v_conv_w, v_conv_b, v_w_down, v_g_final):
    given = dict(locals())
    B, S, _ = x.shape
    T = B * S
    w_big = {n: given[n][0].T if n == "w_in" else given[n][0] for n in BIG_W}
    m_big = {n: given["m_" + n][0].T if n == "w_in" else given["m_" + n][0] for n in BIG_W}
    v_big = {n: given["v_" + n][0].T if n == "w_in" else given["v_" + n][0] for n in BIG_W}
    shard_shape = {n: w_big[n].shape for n in BIG_W}

    first = ("w_in", "w_uq", "w_ukv")
    later = ("w_out", "w_up", "w_down", "conv_w")
    x2d = x.reshape(T, D_MODEL)
    h, got_w = rmsnorm_fwd_call("norm_mix", x2d, g_mix, ex=gather_group([w_big[n].astype(BF16) for n in first]))
    full = {n: full_from_chips(g_, BIG_SHARD[n][2]) for n, g_ in zip(first, got_w) if n != "w_in"}
    gather_later = gather_group([w_big[n] if n == "conv_w" else w_big[n].astype(BF16) for n in later])
    w_in_t = jnp.pad(got_w[0].reshape(IN_COLS, D_MODEL), ((0, IN_COLS_PAD - IN_COLS), (0, 0)))
    w_uq_p = jnp.concatenate(_split_heads(full["w_uq"], HEAD_DIM, ROPE_DIM), axis=1)
    w_ukv_p = jnp.concatenate(_split_heads(full["w_ukv"], HEAD_DIM, HEAD_DIM), axis=1)

    half = ROPE_DIM // 2
    inv_freq = 1.0 / (ROPE_BASE ** (jnp.arange(half, dtype=F32) * (2.0 / ROPE_DIM)))
    cos, sin = rope_tab_call(positions.reshape(T, 1), jnp.tile(inv_freq, LANES // half).reshape(1, LANES))
    p = matmul_call("proj_in", h, w_in_t, "nt", tn=IN_COLS_PAD // 2)
    qn, qr, kn, vm, krt, cqn, ckvn = mla_prep_fwd_call(p, cos, sin, g_cq, g_ckv, w_uq_p, w_ukv_p)
    o_sb, lt_sb, got_w = sb_fwd_call(p, B, S, ex=gather_later)
    w_up4 = got_w[1]
    full.update({n: full_from_chips(g_, BIG_SHARD[n][2]) for n, g_ in zip(later, got_w) if n != "w_up"})
    conv_w_full = full["conv_w"]
    o_mla, lse = mla_fwd_call(qn, qr, kn, krt, vm, B, S)
    o_cat = outnorm_fwd_call(o_sb, o_mla, g_sb_out, g_mla_out)
    x1, hn = proj_out_norm_call(o_cat, full["w_out"], x2d, g_ffn)
    u_g, u_v, act, c_g, c_v = ffn_up_conv_call(hn, w_up4, conv_w_full, conv_b, S)
    dx2, dx2b, loss_row, dg_final = ffn_down_loss_call(
        act, full["w_down"], x1, g_final.reshape(1, D_MODEL), loss_target.reshape(T, D_MODEL))

    xi, yi, ci = _place()
    chip = (2 * xi + yi).astype(jnp.int32).reshape(1)
    where = jnp.stack([ci, 2 * xi + yi]).astype(jnp.int32)

    def add_halves(names, parts, sib_rows):
        return [add_half_call("add_half_" + n, p_, s_, where) for n, p_, s_ in zip(names, parts, sib_rows)]

    def sum_chips(names, halves, from_chips):
        return [sum_chips_call("sum_chips_" + n, h_[0], f_) for n, h_, f_ in zip(names, halves, from_chips)]

    ffn_w = ("w_down", "w_up")
    parts_ffn = [chips_from_full(wgrad_call("wgrad_down", act, dx2b, tn=512), False)]
    du_g, du_v, dcw_g, dcw_v, dcb_g, dcb_v = conv_act_bwd_call(
        u_g, u_v, c_g, c_v, dx2b, full["w_down"], conv_w_full, B, S)
    parts_ffn.append(wgrad_up_call(hn, du_g, du_v))
    dx1, dg_ffn, sib_ffn = ffn_up_bwd_call(du_g, du_v, w_up4, x1, g_ffn, dx2, swap_half(parts_ffn))
    parts_out = [chips_from_full(wgrad_call("wgrad_out", o_cat, dx1), False)]
    do_sb, do_mla, dg_sb_out, dg_mla_out, sib_out = proj_out_bwd_call(
        dx1, full["w_out"], o_sb, o_mla, g_sb_out, g_mla_out, swap_half(parts_out))
    early = ffn_w + ("w_out",)
    halves = add_halves(early, parts_ffn + parts_out, list(sib_ffn) + list(sib_out))
    dq_sb, dk_sb, dv_sb, from_chips = sb_bwd_call(p, lt_sb, do_sb, B, S, ex=scatter_half([h_[1] for h_ in halves]))
    finals = sum_chips(early, halves, from_chips)
    dqn, dqr4, dkn, dvm, dkrt4, done = mla_bwd_call(qn, qr, kn, krt, vm, o_mla, lse, do_mla, B, S,
        ex=swap_final(finals, [shard_shape[n] for n in early]))
    grads = dict(zip(early, done))
    dcq, dckvr, dq_cat, dkv_cat, dg_cq, dg_ckv = mla_prep_bwd_call(
        p, cos, sin, g_cq, g_ckv, w_uq_p, w_ukv_p, dqn, dqr4, dkn, dvm, dkrt4)
    dw_uq_p = wgrad_call("wgrad_uq", cqn, dq_cat)
    dw_ukv_p = wgrad_call("wgrad_ukv", ckvn, dkv_cat)
    dp = (dq_sb, dk_sb, dv_sb, dcq, dckvr)
    late = ("w_uq", "w_ukv", "w_in")
    parts_late = [chips_from_full(g_, True) for g_ in (
        _merge_heads(dw_uq_p[:, :ATT_W], dw_uq_p[:, ATT_W:], HEAD_DIM, ROPE_DIM),
        _merge_heads(dw_ukv_p[:, :ATT_W], dw_ukv_p[:, ATT_W:], HEAD_DIM, HEAD_DIM))]
    parts_late.append(chips_from_full(wgrad_in_call(h, dp)[:IN_COLS], False))
    dh, sib_late = proj_in_bwd_call(dp, w_in_t, swap_half(parts_late))
    halves = add_halves(late, parts_late, sib_late)
    grad_x, dg_mix, from_chips = rmsnorm_bwd_call(
        "norm_mix_bwd", x2d, g_mix, dh, dx1, ex=scatter_half([h_[1] for h_ in halves]))
    finals = sum_chips(late, halves, from_chips)
    grads.update(zip(late, exchange_call("swap_final_late", swap_final(finals, [shard_shape[n] for n in late]))))

    shapes = {n: given[n].shape for n in SMALL_W}
    shapes.update(loss=(), conv_w=(3, 2 * D_FF))
    small_g = {"g_mix": dg_mix, "g_cq": dg_cq, "g_ckv": dg_ckv, "g_sb_out": dg_sb_out, "g_mla_out": dg_mla_out,
               "g_ffn": dg_ffn, "conv_b": jnp.concatenate([dcb_g, dcb_v], axis=1), "g_final": dg_final,
               "loss": loss_row[0, :1], "conv_w": jnp.concatenate([dcw_g, dcw_v], axis=1)}
    gs_slab = allsum_small_call(pack_small(small_g))
    small_in = [pack_small({n: given[pre + n] for n in SMALL_W}) for pre in ("", "m_", "v_")]
    small_out = [unpack_small(s, shapes) for s in adamw_call("adamw_small", gs_slab, *small_in)]
    cw_cols = BIG_SHARD["conv_w"][1]
    grads["conv_w"] = lax.dynamic_slice_in_dim(small_out[0]["conv_w"], chip[0] * cw_cols, cw_cols, axis=1)

    big_out = {n: adamw_call("adamw_" + n, grads[n], w_big[n], m_big[n], v_big[n]) for n in BIG_W}
    weights = ("g_mix", "w_in", "g_cq", "w_uq", "g_ckv", "w_ukv", "g_sb_out", "g_mla_out", "w_out", "g_ffn",
               "w_up", "conv_w", "conv_b", "w_down", "g_final")
    outs = [small_out[0]["loss"], grad_x.reshape(B, S, D_MODEL)]
    for k in range(4):
        for n in weights:
            if n in BIG_W:
                outs.append((big_out[n][k].T if n == "w_in" else big_out[n][k])[None])
            else:
                outs.append(small_out[k][n])
    return tuple(outs)
```

```python
import functools

import jax
import jax.numpy as jnp
from jax import lax
from jax.experimental import pallas as pl
from jax.experimental.pallas import tpu as pltpu

F32 = jnp.float32
BF16 = jnp.bfloat16
MESH = pl.DeviceIdType.MESH

D_MODEL = 1024
HEADS = 8
HEAD_DIM = 64
ATT_W = HEADS * HEAD_DIM
ROPE_DIM = 32
ROPE_W = HEADS * ROPE_DIM
QK_DIM = HEAD_DIM + ROPE_DIM
Q_RANK = 384
KV_RANK = 256
D_FF = 2816
IN_COLS = 2208
IN_COLS_PAD = 2304
EPS = 1e-6
ROPE_BASE = 10000.0
SB_SCALE = HEAD_DIM ** -0.5
SB_SCALE2 = SB_SCALE * 1.4426950408889634
MLA_SCALE = QK_DIM ** -0.5
LOG2E = 1.4426950408889634
LN2 = 0.6931471805599453
MLA_SCALE2 = MLA_SCALE * LOG2E
LANES = 128
N_CHIPS = 4
N_DEV = 8
VMEM_LIMIT = 48 * 1024 * 1024
ATT_TQ = 256
ATT_TK = 256
ATT_PAIRS = 4
PAIR_LANES = [slice(i * LANES, (i + 1) * LANES) for i in range(ATT_PAIRS)]
SB_BWD_GROUP = 2
SB_FWD_GROUP = 4
MLA_GROUP = 4
NEG_BIG = -1e30

ADAM_LR = 0.001
ADAM_B1 = 0.9
ADAM_B2 = 0.999
ADAM_EPS = 1e-08
ADAM_WD = 0.01
ADAM_STEP = 10

BIG_W = ("w_in", "w_uq", "w_ukv", "w_out", "w_up", "conv_w", "w_down")
BIG_SHARD = {
    "w_in": (D_MODEL, IN_COLS // 4, True),
    "w_uq": (Q_RANK, HEADS * QK_DIM // 4, True),
    "w_ukv": (KV_RANK, 2 * ATT_W // 4, True),
    "w_out": (2 * ATT_W // 4, D_MODEL, False),
    "w_up": (D_MODEL, 2 * D_FF // 4, True),
    "conv_w": (3, 2 * D_FF // 4, True),
    "w_down": (D_FF // 4, D_MODEL, False),
}
SMALL_W = ("g_mix", "g_cq", "g_ckv", "g_sb_out", "g_mla_out", "g_ffn", "conv_b", "g_final")
SMALL_N = {"g_mix": D_MODEL, "g_cq": Q_RANK, "g_ckv": KV_RANK, "g_sb_out": ATT_W, "g_mla_out": ATT_W,
           "g_ffn": D_MODEL, "conv_b": 2 * D_FF, "g_final": D_MODEL}


def _params(sem=None, **kw):
    return pltpu.CompilerParams(dimension_semantics=sem, vmem_limit_bytes=VMEM_LIMIT, **kw)


def _dot(a, b, dims):
    return lax.dot_general(a, b, (dims, ((), ())), preferred_element_type=F32)


def _nn(a, b):
    return _dot(a, b, ((1,), (0,)))


def _nt(a, b):
    return _dot(a, b, ((1,), (1,)))


def _tn(a, b):
    return _dot(a, b, ((0,), (0,)))


def _split2(x):
    hi = x.astype(BF16)
    lo = (x - hi.astype(F32)).astype(BF16)
    return hi, lo


def _split3(x):
    hi = x.astype(BF16)
    r1 = x - hi.astype(F32)
    mid = r1.astype(BF16)
    return hi, mid, (r1 - mid.astype(F32)).astype(BF16)


def _rms_r(x, d):
    return lax.rsqrt(jnp.sum(x * x, axis=-1, keepdims=True) * (1.0 / d) + EPS)


def _rms_bwd(x, g, dy, d):
    r = _rms_r(x, d)
    xhat = x * r
    gy = dy * g
    dx = r * (gy - xhat * (jnp.sum(xhat * gy, axis=-1, keepdims=True) * (1.0 / d)))
    return dx, jnp.sum(dy * xhat, axis=0, keepdims=True)


def _rot(x):
    lane = lax.broadcasted_iota(jnp.int32, x.shape, x.ndim - 1)
    n = x.shape[-1]
    return jnp.where((lane & 31) < 16, pltpu.roll(x, n - 16, x.ndim - 1), pltpu.roll(x, 16, x.ndim - 1))


def _fold4(x):
    return x + pltpu.roll(x, 32, 1) + pltpu.roll(x, 64, 1) + pltpu.roll(x, 96, 1)


def matmul_call(name, a, b, mode, out_dtype=F32, res=None, tm=512, tn=None, ex=None):
    M, K = a.shape
    N = b.shape[1] if mode == "nn" else b.shape[0]
    tn = N if tn is None else tn
    assert M % tm == 0 and N % tn == 0

    def body(*refs):
        if res is None:
            a_ref, b_ref, o_ref = refs
        else:
            a_ref, b_ref, r_ref, o_ref = refs
        av = a_ref[...].astype(BF16)
        bv = b_ref[...].astype(BF16)
        acc = _nn(av, bv) if mode == "nn" else _nt(av, bv)
        if res is not None:
            acc = r_ref[...] + acc
        o_ref[...] = acc.astype(out_dtype)

    in_specs = [pl.BlockSpec((tm, K), lambda j, i: (i, 0))]
    if mode == "nn":
        in_specs.append(pl.BlockSpec((K, tn), lambda j, i: (0, j)))
    else:
        in_specs.append(pl.BlockSpec((tn, K), lambda j, i: (j, 0)))
    args = [a, b]
    if res is not None:
        in_specs.append(pl.BlockSpec((tm, tn), lambda j, i: (i, j)))
        args.append(res)
    outs, moved = _call(body, ex, name=name, grid=(N // tn, M // tm), in_specs=in_specs,
                        out_specs=[pl.BlockSpec((tm, tn), lambda j, i: (i, j))],
                        out_shape=[jax.ShapeDtypeStruct((M, N), out_dtype)], args=args)
    return outs[0] if ex is None else (outs[0], moved)


def _rows(tm, width):
    return pl.BlockSpec((tm, width), lambda i: (i, 0))


def _whole(a, **kw):
    return pl.BlockSpec(a.shape, lambda i: (0,) * a.ndim, **kw)


def proj_out_norm_call(o_cat, w_out, x, g_ffn, tm=512):
    T, K = o_cat.shape
    N = w_out.shape[1]

    def body(a_ref, w_ref, x_ref, g_ref, x1_ref, hn_ref):
        x1 = x_ref[...] + _nn(a_ref[...], w_ref[...])
        x1_ref[...] = x1
        hn_ref[...] = ((x1 * _rms_r(x1, N)) * g_ref[...]).astype(BF16)

    return pl.pallas_call(
        body, name="proj_out", grid=(T // tm,),
        in_specs=[_rows(tm, K), _whole(w_out), _rows(tm, N), _whole(g_ffn)], out_specs=[_rows(tm, N)] * 2,
        out_shape=[jax.ShapeDtypeStruct((T, N), F32), jax.ShapeDtypeStruct((T, N), BF16)],
        compiler_params=_params(("parallel",)),
    )(o_cat, w_out, x, g_ffn)


def ffn_down_loss_call(act, w_down, x1, g, target, tm=512):
    T, K = act.shape
    d = w_down.shape[1]

    def body(a_ref, w_ref, x1_ref, g_ref, t_ref, dx_ref, dxb_ref, loss_ref, dg_ref):
        @pl.when(pl.program_id(0) == 0)
        def _():
            loss_ref[...] = jnp.zeros_like(loss_ref)
            dg_ref[...] = jnp.zeros_like(dg_ref)

        x = x1_ref[...] + _nn(a_ref[...], w_ref[...])
        g = g_ref[...]
        y = (x * _rms_r(x, d)) * g
        err = y - t_ref[...]
        loss_ref[...] += jnp.sum(jnp.sum(err * err, axis=1, keepdims=True), axis=0, keepdims=True) * (0.5 / d)
        dx, dg = _rms_bwd(x, g, err * (1.0 / d), d)
        dx_ref[...] = dx
        dxb_ref[...] = dx.astype(BF16)
        dg_ref[...] += dg

    return pl.pallas_call(
        body, name="ffn_down_loss", grid=(T // tm,),
        in_specs=[_rows(tm, K), _whole(w_down), _rows(tm, d), _whole(g), _rows(tm, d)],
        out_specs=[_rows(tm, d), _rows(tm, d), pl.BlockSpec((1, LANES), lambda i: (0, 0)), _whole(g)],
        out_shape=[jax.ShapeDtypeStruct((T, d), F32), jax.ShapeDtypeStruct((T, d), BF16),
                   jax.ShapeDtypeStruct((1, LANES), F32), jax.ShapeDtypeStruct((1, d), F32)],
        compiler_params=_params(("arbitrary",)),
    )(act, w_down, x1, g, target)


def proj_out_bwd_call(dx1, w_out, o_sb, o_mla, g_sb, g_mla, ex, tm=512):
    T, N = dx1.shape

    def body(d_ref, w_ref, a_ref, b_ref, ga_ref, gb_ref, da_ref, db_ref, dga_ref, dgb_ref):
        @pl.when(pl.program_id(0) == 0)
        def _():
            dga_ref[...] = jnp.zeros_like(dga_ref)
            dgb_ref[...] = jnp.zeros_like(dgb_ref)

        d = _nt(d_ref[...].astype(BF16), w_ref[...])
        da, dga = _rms_bwd(a_ref[...], ga_ref[...], d[:, :ATT_W], ATT_W)
        db, dgb = _rms_bwd(b_ref[...], gb_ref[...], d[:, ATT_W:], ATT_W)
        da_ref[...] = da
        db_ref[...] = db
        dga_ref[...] += dga
        dgb_ref[...] += dgb

    outs, moved = _call(
        body, ex, name="proj_out_bwd", grid=(T // tm,),
        in_specs=[_rows(tm, N), _whole(w_out), _rows(tm, ATT_W), _rows(tm, ATT_W), _whole(g_sb), _whole(g_mla)],
        out_specs=[_rows(tm, ATT_W), _rows(tm, ATT_W), _whole(g_sb), _whole(g_mla)],
        out_shape=[jax.ShapeDtypeStruct((T, ATT_W), F32), jax.ShapeDtypeStruct((T, ATT_W), F32),
                   jax.ShapeDtypeStruct((1, ATT_W), F32), jax.ShapeDtypeStruct((1, ATT_W), F32)],
        args=(dx1, w_out, o_sb, o_mla, g_sb, g_mla))
    return tuple(outs) + (moved,)


def wgrad_call(name, a, b, tn=None, tt=512, by_chip=False):
    T, M = a.shape
    N = b.shape[1]
    tn = N if tn is None else tn
    assert T % tt == 0 and N % tn == 0
    if by_chip:
        out_spec = pl.BlockSpec((None, M, tn), lambda j, t: (j, 0, 0))
        out_shape = jax.ShapeDtypeStruct((N // tn, M, tn), F32)
    else:
        out_spec = pl.BlockSpec((M, tn), lambda j, t: (0, j))
        out_shape = jax.ShapeDtypeStruct((M, N), F32)

    def body(a_ref, b_ref, o_ref):
        @pl.when(pl.program_id(1) == 0)
        def _():
            o_ref[...] = jnp.zeros_like(o_ref)

        o_ref[...] += _tn(a_ref[...].astype(BF16), b_ref[...].astype(BF16))

    return pl.pallas_call(
        body, name=name, grid=(N // tn, T // tt),
        in_specs=[pl.BlockSpec((tt, M), lambda j, t: (t, 0)), pl.BlockSpec((tt, tn), lambda j, t: (t, j))],
        out_specs=out_spec, out_shape=out_shape,
        compiler_params=_params(("parallel", "arbitrary")),
    )(a, b)


UP_COLS = 2 * D_FF // N_CHIPS


def _shift_down_after(u, prev8, n):
    top = pltpu.roll(jnp.concatenate([prev8, u[0:8]], axis=0), n, 0)[8:16]
    return jnp.concatenate([top, pltpu.roll(u, n, 0)[8:]], axis=0)


def ffn_up_conv_call(hn, w4, conv_w, conv_b, S, tm=512):
    T, K = hn.shape
    per_seq = S // tm

    def body(a_ref, wg_ref, wv_ref, cg_ref, cv_ref, bg_ref, bv_ref, ug_ref, uv_ref, act_ref, og_ref, ov_ref, pg_ref, pv_ref):
        @pl.when(pl.program_id(1) % per_seq == 0)
        def _():
            pg_ref[...] = jnp.zeros_like(pg_ref)
            pv_ref[...] = jnp.zeros_like(pv_ref)

        a = a_ref[...]
        outs = []
        for w_ref, c_ref, b_ref, u_ref, prev_ref in ((wg_ref, cg_ref, bg_ref, ug_ref, pg_ref),
                                                     (wv_ref, cv_ref, bv_ref, uv_ref, pv_ref)):
            u = _nn(a, w_ref[...])
            u_ref[...] = u
            prev8 = prev_ref[...]
            taps = _taps(c_ref)
            outs.append(taps[0] * _shift_down_after(u, prev8, 2) + taps[1] * _shift_down_after(u, prev8, 1)
                        + taps[2] * u + b_ref[...])
            prev_ref[...] = u[tm - 8:, :]
        gate, val = outs
        act_ref[...] = (gate * (1.0 / (1.0 + jnp.exp(-gate))) * val).astype(BF16)
        og_ref[...] = gate.astype(BF16)
        ov_ref[...] = val.astype(BF16)

    out = pl.BlockSpec((tm, UP_COLS), lambda j, i: (i, j))

    def cols(rows, off):
        return pl.BlockSpec((rows, UP_COLS), lambda j, i: (0, off + j))

    return pl.pallas_call(
        body, name="ffn_up_conv", grid=(2, T // tm),
        in_specs=[pl.BlockSpec((tm, K), lambda j, i: (i, 0)),
                  pl.BlockSpec((None, K, UP_COLS), lambda j, i: (j, 0, 0)),
                  pl.BlockSpec((None, K, UP_COLS), lambda j, i: (2 + j, 0, 0)),
                  cols(3, 0), cols(3, 2), cols(1, 0), cols(1, 2)],
        out_specs=[out] * 5,
        out_shape=[jax.ShapeDtypeStruct((T, D_FF), F32)] * 2 + [jax.ShapeDtypeStruct((T, D_FF), BF16)] * 3,
        scratch_shapes=[pltpu.VMEM((8, UP_COLS), F32)] * 2,
        compiler_params=_params(("arbitrary", "arbitrary")),
    )(hn, w4, w4, conv_w, conv_w, conv_b, conv_b)


def _shift_up_before(d, next8, n):
    rows = d.shape[0]
    bottom = pltpu.roll(jnp.concatenate([d[rows - 8:], next8], axis=0), 16 - n, 0)[0:8]
    return jnp.concatenate([pltpu.roll(d, rows - n, 0)[:rows - 8], bottom], axis=0)


def conv_bwd_wgrad_up_call(hn, ug, uv, cg, cv, dx2, w_down, conv_w, S, tt=256):
    T, M = hn.shape
    nt = T // tt
    per_seq = S // tt

    def body(a_ref, ug_ref, uv_ref, cg_ref, cv_ref, dx_ref, wd_ref, wg_ref, wv_ref,
             dug_ref, duv_ref, dw_ref, dwg_ref, dwv_ref, dbg_ref, dbv_ref, ng_ref, nv_ref):
        step = pl.program_id(1)

        @pl.when(step == 0)
        def _():
            for r in (dw_ref, dwg_ref, dwv_ref, dbg_ref, dbv_ref):
                r[...] = jnp.zeros_like(r)

        @pl.when((nt - 1 - step) % per_seq == per_seq - 1)
        def _():
            ng_ref[...] = jnp.zeros_like(ng_ref)
            nv_ref[...] = jnp.zeros_like(nv_ref)

        gate = cg_ref[...].astype(F32)
        val = cv_ref[...].astype(F32)
        da = _nt(dx_ref[...], wd_ref[...])
        sig = 1.0 / (1.0 + jnp.exp(-gate))
        dval = da * (gate * sig)
        dgate = da * val * (sig * (1.0 + gate * (1.0 - sig)))
        a = a_ref[...]
        for k, (u_ref, d, w, du_ref, dcw_ref, dcb_ref, next_ref) in enumerate((
                (ug_ref, dgate, _taps(wg_ref), dug_ref, dwg_ref, dbg_ref, ng_ref),
                (uv_ref, dval, _taps(wv_ref), duv_ref, dwv_ref, dbv_ref, nv_ref))):
            u_ = u_ref[...]
            next8 = next_ref[...]
            d1 = _shift_up_before(d, next8, 1)
            d2 = _shift_up_before(d, next8, 2)
            du = (w[2] * d + w[1] * d1 + w[0] * d2).astype(BF16)
            du_ref[...] = du
            dw_ref[k] += _tn(a, du)
            dcb_ref[...] += jnp.sum(d, axis=0, keepdims=True)
            dcw_ref[0:1, :] += jnp.sum(d2 * u_, axis=0, keepdims=True)
            dcw_ref[1:2, :] += jnp.sum(d1 * u_, axis=0, keepdims=True)
            dcw_ref[2:3, :] += jnp.sum(d * u_, axis=0, keepdims=True)
            next_ref[...] = d[0:8, :]

    def tok(width, by_col):
        return pl.BlockSpec((tt, width), (lambda j, i: (nt - 1 - i, j)) if by_col else (lambda j, i: (nt - 1 - i, 0)))

    def cols(rows, off):
        return pl.BlockSpec((rows, UP_COLS), lambda j, i: (0, off + j))

    outs = pl.pallas_call(
        body, name="conv_bwd_wgrad_up", grid=(2, nt),
        in_specs=[tok(M, False), tok(UP_COLS, True), tok(UP_COLS, True), tok(UP_COLS, True), tok(UP_COLS, True),
                  tok(dx2.shape[1], False), pl.BlockSpec((UP_COLS, w_down.shape[1]), lambda j, i: (j, 0)),
                  cols(3, 0), cols(3, 2)],
        out_specs=[tok(UP_COLS, True), tok(UP_COLS, True),
                   pl.BlockSpec((2, None, M, UP_COLS), lambda j, i: (0, j, 0, 0)),
                   cols(3, 0), cols(3, 0), cols(1, 0), cols(1, 0)],
        out_shape=[jax.ShapeDtypeStruct((T, D_FF), BF16), jax.ShapeDtypeStruct((T, D_FF), BF16),
                   jax.ShapeDtypeStruct((2, 2, M, UP_COLS), F32),
                   jax.ShapeDtypeStruct((3, D_FF), F32), jax.ShapeDtypeStruct((3, D_FF), F32),
                   jax.ShapeDtypeStruct((1, D_FF), F32), jax.ShapeDtypeStruct((1, D_FF), F32)],
        scratch_shapes=[pltpu.VMEM((8, UP_COLS), F32)] * 2,
        compiler_params=_params(("arbitrary", "arbitrary")),
    )(hn, ug, uv, cg, cv, dx2, w_down, conv_w, conv_w)
    return (outs[0], outs[1], outs[2].reshape(N_CHIPS, M, UP_COLS)) + tuple(outs[3:])


def ffn_up_bwd_call(du_g, du_v, w4, x1, g_ffn, dx2, ex, tm=512):
    T = du_g.shape[0]
    N = w4.shape[1]

    def body(g_ref, v_ref, w_ref, x1_ref, gf_ref, dx2_ref, dx1_ref, dg_ref):
        @pl.when(pl.program_id(0) == 0)
        def _():
            dg_ref[...] = jnp.zeros_like(dg_ref)

        dhn = _nt(g_ref[:, :UP_COLS], w_ref[0]) + _nt(g_ref[:, UP_COLS:], w_ref[1])
        dhn = dhn + _nt(v_ref[:, :UP_COLS], w_ref[2]) + _nt(v_ref[:, UP_COLS:], w_ref[3])
        dx, dg = _rms_bwd(x1_ref[...], gf_ref[...], dhn, N)
        dx1_ref[...] = dx2_ref[...] + dx
        dg_ref[...] += dg

    outs, moved = _call(
        body, ex, name="ffn_up_bwd", grid=(T // tm,),
        in_specs=[_rows(tm, D_FF), _rows(tm, D_FF), _whole(w4, pipeline_mode=pl.Buffered(1)), _rows(tm, N),
                  _whole(g_ffn), _rows(tm, N)],
        out_specs=[_rows(tm, N), _whole(g_ffn)],
        out_shape=[jax.ShapeDtypeStruct((T, N), F32), jax.ShapeDtypeStruct((1, N), F32)],
        args=(du_g, du_v, w4, x1, g_ffn, dx2))
    return outs[0], outs[1], moved


def wgrad_up_call(hn, du_g, du_v, tt=512):
    T, M = hn.shape

    def body(a_ref, g_ref, v_ref, o_ref):
        @pl.when(pl.program_id(1) == 0)
        def _():
            o_ref[...] = jnp.zeros_like(o_ref)

        a = a_ref[...]
        o_ref[0] += _tn(a, g_ref[...])
        o_ref[1] += _tn(a, v_ref[...])

    col = pl.BlockSpec((tt, UP_COLS), lambda j, t: (t, j))
    out = pl.pallas_call(
        body, name="wgrad_up", grid=(2, T // tt),
        in_specs=[pl.BlockSpec((tt, M), lambda j, t: (t, 0)), col, col],
        out_specs=pl.BlockSpec((2, None, M, UP_COLS), lambda j, t: (0, j, 0, 0)),
        out_shape=jax.ShapeDtypeStruct((2, 2, M, UP_COLS), F32),
        compiler_params=_params(("parallel", "arbitrary")),
    )(hn, du_g, du_v)
    return out.reshape(N_CHIPS, M, UP_COLS)


IN_PIECES = ((0, ATT_W), (ATT_W, ATT_W), (2 * ATT_W, ATT_W), (3 * ATT_W, Q_RANK), (3 * ATT_W + Q_RANK, Q_RANK))


def proj_in_bwd_call(pieces, w_in_t, ex, tm=512):
    T = pieces[0].shape[0]
    N = w_in_t.shape[1]
    n = len(pieces)

    def body(*refs):
        o_ref = refs[2 * n]
        acc = _nn(refs[0][...].astype(BF16), refs[n][...])
        for i in range(1, n):
            acc = acc + _nn(refs[i][...].astype(BF16), refs[n + i][...])
        o_ref[...] = acc

    outs, moved = _call(
        body, ex, name="proj_in_bwd", grid=(T // tm,),
        in_specs=[pl.BlockSpec((tm, w), lambda i: (i, 0)) for _, w in IN_PIECES]
        + [pl.BlockSpec((w, N), functools.partial(lambda c, i: (c, 0), off // w)) for off, w in IN_PIECES],
        out_specs=[pl.BlockSpec((tm, N), lambda i: (i, 0))],
        out_shape=[jax.ShapeDtypeStruct((T, N), F32)], args=tuple(pieces) + (w_in_t,) * n)
    return outs[0], moved


def wgrad_in_call(h, pieces, tt=512):
    T, M = h.shape
    n = len(pieces)

    def body(*refs):
        a_ref, o_ref = refs[0], refs[n + 1]

        @pl.when(pl.program_id(0) == 0)
        def _():
            o_ref[...] = jnp.zeros_like(o_ref)

        a = a_ref[...]
        for i, (off, w) in enumerate(IN_PIECES):
            o_ref[off:off + w, :] += _tn(refs[1 + i][...].astype(BF16), a)

    return pl.pallas_call(
        body, name="wgrad_in", grid=(T // tt,),
        in_specs=[pl.BlockSpec((tt, M), lambda t: (t, 0))] + [pl.BlockSpec((tt, w), lambda t: (t, 0)) for _, w in IN_PIECES],
        out_specs=pl.BlockSpec((IN_COLS_PAD, M), lambda t: (0, 0)),
        out_shape=jax.ShapeDtypeStruct((IN_COLS_PAD, M), F32),
        compiler_params=_params(("arbitrary",)),
    )(h, *pieces)


def rmsnorm_fwd_call(name, x, g, tm=512, ex=None):
    T, d = x.shape

    def body(x_ref, g_ref, o_ref):
        x = x_ref[...]
        o_ref[...] = ((x * _rms_r(x, d)) * g_ref[...]).astype(BF16)

    row = pl.BlockSpec((tm, d), lambda i: (i, 0))
    outs, moved = _call(body, ex, name=name, grid=(T // tm,), in_specs=[row, pl.BlockSpec((1, d), lambda i: (0, 0))],
                        out_specs=[row], out_shape=[jax.ShapeDtypeStruct((T, d), BF16)], args=(x, g))
    return outs[0] if ex is None else (outs[0], moved)


def rmsnorm_bwd_call(name, x, g, dy, res, tm=512, ex=None):
    T, d = x.shape

    def body(x_ref, g_ref, dy_ref, r_ref, dx_ref, dg_ref):
        @pl.when(pl.program_id(0) == 0)
        def _():
            dg_ref[...] = jnp.zeros_like(dg_ref)

        dx, dg = _rms_bwd(x_ref[...], g_ref[...], dy_ref[...], d)
        dx_ref[...] = r_ref[...] + dx
        dg_ref[...] += dg

    row = pl.BlockSpec((tm, d), lambda i: (i, 0))
    vec = pl.BlockSpec((1, d), lambda i: (0, 0))
    outs, moved = _call(body, ex, name=name, grid=(T // tm,), in_specs=[row, vec, row, row], out_specs=[row, vec],
                        out_shape=[jax.ShapeDtypeStruct((T, d), F32), jax.ShapeDtypeStruct((1, d), F32)],
                        args=(x, g, dy, res))
    return tuple(outs) if ex is None else tuple(outs) + (moved,)


def outnorm_fwd_call(o_sb, o_mla, g_sb, g_mla, tm=512):
    T = o_sb.shape[0]

    def body(a_ref, b_ref, ga_ref, gb_ref, o_ref):
        a = a_ref[...]
        b = b_ref[...]
        ya = (a * _rms_r(a, ATT_W)) * ga_ref[...]
        yb = (b * _rms_r(b, ATT_W)) * gb_ref[...]
        o_ref[...] = jnp.concatenate([ya, yb], axis=1).astype(BF16)

    row = pl.BlockSpec((tm, ATT_W), lambda i: (i, 0))
    vec = pl.BlockSpec((1, ATT_W), lambda i: (0, 0))
    return pl.pallas_call(
        body, name="outnorm_fwd", grid=(T // tm,), in_specs=[row, row, vec, vec],
        out_specs=pl.BlockSpec((tm, 2 * ATT_W), lambda i: (i, 0)),
        out_shape=jax.ShapeDtypeStruct((T, 2 * ATT_W), BF16),
        compiler_params=_params(("parallel",)),
    )(o_sb, o_mla, g_sb, g_mla)


def outnorm_bwd_call(o_sb, o_mla, g_sb, g_mla, do_cat, tm=512, ex=None):
    T = o_sb.shape[0]

    def body(a_ref, b_ref, ga_ref, gb_ref, d_ref, da_ref, db_ref, dga_ref, dgb_ref):
        @pl.when(pl.program_id(0) == 0)
        def _():
            dga_ref[...] = jnp.zeros_like(dga_ref)
            dgb_ref[...] = jnp.zeros_like(dgb_ref)

        d = d_ref[...]
        da, dga = _rms_bwd(a_ref[...], ga_ref[...], d[:, :ATT_W], ATT_W)
        db, dgb = _rms_bwd(b_ref[...], gb_ref[...], d[:, ATT_W:], ATT_W)
        da_ref[...] = da
        db_ref[...] = db
        dga_ref[...] += dga
        dgb_ref[...] += dgb

    row = pl.BlockSpec((tm, ATT_W), lambda i: (i, 0))
    vec = pl.BlockSpec((1, ATT_W), lambda i: (0, 0))
    outs, moved = _call(
        body, ex, name="outnorm_bwd", grid=(T // tm,),
        in_specs=[row, row, vec, vec, pl.BlockSpec((tm, 2 * ATT_W), lambda i: (i, 0))],
        out_specs=[row, row, vec, vec],
        out_shape=[jax.ShapeDtypeStruct((T, ATT_W), F32), jax.ShapeDtypeStruct((T, ATT_W), F32),
                   jax.ShapeDtypeStruct((1, ATT_W), F32), jax.ShapeDtypeStruct((1, ATT_W), F32)],
        args=(o_sb, o_mla, g_sb, g_mla, do_cat))
    return tuple(outs) if ex is None else tuple(outs) + (moved,)


def final_loss_call(x2, g, target, tm=512):
    T, d = x2.shape

    def body(x_ref, g_ref, t_ref, dx_ref, dxb_ref, loss_ref, dg_ref):
        @pl.when(pl.program_id(0) == 0)
        def _():
            loss_ref[...] = jnp.zeros_like(loss_ref)
            dg_ref[...] = jnp.zeros_like(dg_ref)

        x = x_ref[...]
        g = g_ref[...]
        y = (x * _rms_r(x, d)) * g
        err = y - t_ref[...]
        loss_ref[...] += jnp.sum(jnp.sum(err * err, axis=1, keepdims=True), axis=0, keepdims=True) * (0.5 / d)
        dx, dg = _rms_bwd(x, g, err * (1.0 / d), d)
        dx_ref[...] = dx
        dxb_ref[...] = dx.astype(BF16)
        dg_ref[...] += dg

    row = pl.BlockSpec((tm, d), lambda i: (i, 0))
    vec = pl.BlockSpec((1, d), lambda i: (0, 0))
    return pl.pallas_call(
        body, name="final_loss", grid=(T // tm,), in_specs=[row, vec, row],
        out_specs=[row, row, pl.BlockSpec((1, LANES), lambda i: (0, 0)), vec],
        out_shape=[jax.ShapeDtypeStruct((T, d), F32), jax.ShapeDtypeStruct((T, d), BF16),
                   jax.ShapeDtypeStruct((1, LANES), F32), jax.ShapeDtypeStruct((1, d), F32)],
        compiler_params=_params(("arbitrary",)),
    )(x2, g, target)


def rope_tab_call(pos, inv_freq, tm=512):
    T = pos.shape[0]

    def body(p_ref, f_ref, c_ref, s_ref):
        ang = p_ref[...].astype(F32) * f_ref[...]
        lane = lax.broadcasted_iota(jnp.int32, ang.shape, 1)
        sn = jnp.sin(ang)
        c_ref[...] = jnp.cos(ang)
        s_ref[...] = jnp.where((lane & 31) < 16, -sn, sn)

    row = pl.BlockSpec((tm, LANES), lambda i: (i, 0))
    return pl.pallas_call(
        body, name="rope_tab", grid=(T // tm,),
        in_specs=[pl.BlockSpec((tm, 1), lambda i: (i, 0)), pl.BlockSpec((1, LANES), lambda i: (0, 0))],
        out_specs=[row, row],
        out_shape=[jax.ShapeDtypeStruct((T, LANES), F32)] * 2,
        compiler_params=_params(("parallel",)),
    )(pos, inv_freq)


def mla_prep_fwd_call(p, cos, sin, g_cq, g_ckv, w_uq_p, w_ukv_p, tm=512):
    T = p.shape[0]

    def body(cq_ref, ckvr_ref, c_ref, s_ref, gq_ref, gkv_ref, wq_ref, wkv_ref,
             qn_ref, qr_ref, kn_ref, vm_ref, krt_ref, cqn_ref, ckvn_ref):
        c = c_ref[...]
        s = s_ref[...]
        cq = cq_ref[...]
        cqn = ((cq * _rms_r(cq, Q_RANK)) * gq_ref[...]).astype(BF16)
        cqn_ref[...] = cqn
        q = _nn(cqn, wq_ref[...])
        qn_ref[...] = q[:, :ATT_W].astype(BF16)
        for g in range(ROPE_W // LANES):
            qr = q[:, ATT_W + g * LANES:ATT_W + (g + 1) * LANES]
            qr_ref[:, g * LANES:(g + 1) * LANES] = (qr * c + _rot(qr) * s).astype(BF16)
        ckvr = ckvr_ref[...]
        ckv = ckvr[:, :KV_RANK]
        ckvn = ((ckv * _rms_r(ckv, KV_RANK)) * gkv_ref[...]).astype(BF16)
        ckvn_ref[...] = ckvn
        kv = _nn(ckvn, wkv_ref[...])
        kn_ref[...] = kv[:, :ATT_W].astype(BF16)
        vm_ref[...] = kv[:, ATT_W:].astype(BF16)
        kr = _fold4(ckvr[:, KV_RANK:])
        krt_ref[...] = (kr * c + _rot(kr) * s).astype(BF16)

    def row(w, j=0):
        return pl.BlockSpec((tm, w), lambda i: (i, j))

    def full(a):
        return pl.BlockSpec(a.shape, lambda i: (0, 0))

    return pl.pallas_call(
        body, name="mla_prep_fwd", grid=(T // tm,),
        in_specs=[row(Q_RANK, 4), row(Q_RANK, 5), row(LANES), row(LANES), full(g_cq), full(g_ckv),
                  full(w_uq_p), full(w_ukv_p)],
        out_specs=[row(ATT_W), row(ROPE_W), row(ATT_W), row(ATT_W), row(LANES), row(Q_RANK), row(KV_RANK)],
        out_shape=[jax.ShapeDtypeStruct((T, w), BF16) for w in (ATT_W, ROPE_W, ATT_W, ATT_W, LANES, Q_RANK, KV_RANK)],
        compiler_params=_params(("parallel",)),
    )(p, p, cos, sin, g_cq, g_ckv, w_uq_p, w_ukv_p)


def mla_prep_bwd_call(p, cos, sin, g_cq, g_ckv, w_uq_p, w_ukv_p, dqn, dqr4, dkn, dvm, dkrt4, tm=512):
    T = p.shape[0]

    def body(cq_ref, ckvr_ref, c_ref, s_ref, gq_ref, gkv_ref, wq_ref, wkv_ref,
             dqn_ref, dqr4_ref, dkn_ref, dvm_ref, dkrt4_ref,
             dcq_ref, dckvr_ref, dq_ref, dkv_ref, dgq_ref, dgkv_ref):
        @pl.when(pl.program_id(0) == 0)
        def _():
            dgq_ref[...] = jnp.zeros_like(dgq_ref)
            dgkv_ref[...] = jnp.zeros_like(dgkv_ref)

        c = c_ref[...]
        s = s_ref[...]
        d4 = dqr4_ref[...]
        dqr = [d4[:, :128] + d4[:, 128:256], d4[:, 256:384] + d4[:, 384:]]
        dqr = [t * c + _rot(t * s) for t in dqr]
        dq = jnp.concatenate([dqn_ref[...]] + dqr, axis=1).astype(BF16)
        dq_ref[...] = dq
        dcq, dgq = _rms_bwd(cq_ref[...], gq_ref[...], _nt(dq, wq_ref[...]), Q_RANK)
        dcq_ref[...] = dcq
        dgq_ref[...] += dgq
        dkv = jnp.concatenate([dkn_ref[...], dvm_ref[...]], axis=1).astype(BF16)
        dkv_ref[...] = dkv
        ckvr = ckvr_ref[...]
        dckv, dgkv = _rms_bwd(ckvr[:, :KV_RANK], gkv_ref[...], _nt(dkv, wkv_ref[...]), KV_RANK)
        dgkv_ref[...] += dgkv
        k4 = dkrt4_ref[...]
        dkr = _fold4(k4[:, :128] + k4[:, 128:256] + k4[:, 256:384] + k4[:, 384:])
        dkr = dkr * c + _rot(dkr * s)
        lane = lax.broadcasted_iota(jnp.int32, dkr.shape, 1)
        dckvr_ref[...] = jnp.concatenate([dckv, jnp.where(lane < ROPE_DIM, dkr, 0.0)], axis=1)

    def row(w, j=0):
        return pl.BlockSpec((tm, w), lambda i: (i, j))

    def full(a):
        return pl.BlockSpec(a.shape, lambda i: (0, 0))

    return pl.pallas_call(
        body, name="mla_prep_bwd", grid=(T // tm,),
        in_specs=[row(Q_RANK, 4), row(Q_RANK, 5), row(LANES), row(LANES), full(g_cq), full(g_ckv),
                  full(w_uq_p), full(w_ukv_p), row(ATT_W), row(ATT_W), row(ATT_W), row(ATT_W), row(ATT_W)],
        out_specs=[row(Q_RANK), row(Q_RANK), row(ATT_W + ROPE_W), row(2 * ATT_W),
                   pl.BlockSpec((1, Q_RANK), lambda i: (0, 0)), pl.BlockSpec((1, KV_RANK), lambda i: (0, 0))],
        out_shape=[jax.ShapeDtypeStruct((T, Q_RANK), F32), jax.ShapeDtypeStruct((T, Q_RANK), F32),
                   jax.ShapeDtypeStruct((T, ATT_W + ROPE_W), BF16), jax.ShapeDtypeStruct((T, 2 * ATT_W), BF16),
                   jax.ShapeDtypeStruct((1, Q_RANK), F32), jax.ShapeDtypeStruct((1, KV_RANK), F32)],
        compiler_params=_params(("arbitrary",)),
    )(p, p, cos, sin, g_cq, g_ckv, w_uq_p, w_ukv_p, dqn, dqr4, dkn, dvm, dkrt4)


def _iota2(shape, axis):
    return lax.broadcasted_iota(jnp.int32, shape, axis)


def _head_masks():
    lane = _iota2((1, LANES), 1)
    return lane < HEAD_DIM, lane >= HEAD_DIM


def _pair(x, masks, dtype=BF16):
    return [jnp.where(m, x, 0.0).astype(dtype) for m in masks]


def _log_gates(z):
    keep = jnp.maximum(z, 0.0) + jnp.log2(1.0 + jnp.exp2(-jnp.abs(z)))
    return z - keep, keep


def _last_row(x):
    return _row_of(x[x.shape[0] - 8:, :], 7)


def _lane_selector(group):
    return jnp.where(_iota2((16, LANES), 1) // group == _iota2((16, LANES), 0), 1.0, 0.0).astype(BF16)


def _rows8(sel_t, x):
    hi = x.astype(BF16)
    r1 = x - hi.astype(F32)
    mid = r1.astype(BF16)
    lo = (r1 - mid.astype(F32)).astype(BF16)
    return _nt(sel_t, hi) + _nt(sel_t, mid) + _nt(sel_t, lo)


def _row_of(x8, j):
    return jnp.sum(jnp.where(_iota2(x8.shape, 0) == j, x8, 0.0), axis=0, keepdims=True)


def sb_fwd_call(p, B, S, ex=None):
    T = B * S
    TQ, TK = ATT_TQ, ATT_TK
    nq = S // TQ

    def body(q_ref, k_ref, v_ref, o_ref, lt_ref):
        qi = pl.program_id(2)
        masks = _head_masks()
        qm = [_pair(q_ref[:, sl] * SB_SCALE2, masks) for sl in PAIR_LANES]
        row = _iota2((TQ, TK), 0)
        col = _iota2((TQ, TK), 1)
        tri = jnp.where(row > col, 1.0, 0.0).astype(BF16)
        tri2 = jnp.concatenate([tri, tri], axis=0)
        vis = col < row
        o_ref[...] = jnp.zeros_like(o_ref)

        def group(k0, pairs, carry, diag):
            heads = [(pp, j) for pp in pairs for j in range(2)]
            n = range(len(heads))
            k = {pp: k_ref[pl.ds(k0, TK), PAIR_LANES[pp]].astype(BF16) for pp in pairs}
            vm = {pp: _pair(v_ref[pl.ds(k0, TK), PAIR_LANES[pp]], masks) for pp in pairs}
            z = [_nt(qm[pp][j], k[pp]) for pp, j in heads]
            if diag:
                z = [jnp.where(vis, x, NEG_BIG) for x in z]
            gates = [_log_gates(x) for x in z]
            lb = [g[0] for g in gates]
            keep = [g[1] for g in gates]
            tail = [_nn(jnp.concatenate(_split2(keep[h]), axis=1), tri2) + carry[h] for h in n]
            a = [jnp.exp2(lb[h] - tail[h]) for h in n]
            ab = [x.astype(BF16) for x in a]
            for i, pp in enumerate(pairs):
                o_ref[:, PAIR_LANES[pp]] += _nn(ab[2 * i], vm[pp][0]) + _nn(ab[2 * i + 1], vm[pp][1])
            return [tail[h][:, 0:1] + keep[h][:, 0:1] for h in n]

        def step(kb, carry, diag):
            k0 = pl.multiple_of(kb * TK, TK)
            out = []
            for g in range(0, ATT_PAIRS, SB_FWD_GROUP):
                out += group(k0, list(range(g, g + SB_FWD_GROUP)), carry[2 * g:2 * (g + SB_FWD_GROUP)], diag)
            return tuple(out)

        zero = jnp.zeros((TQ, 1), F32)
        carry = step(qi, (zero,) * (2 * ATT_PAIRS), True)
        carry = lax.fori_loop(0, qi, lambda i, c: step(qi - 1 - i, c, False), carry)
        lane = _iota2((TQ, LANES), 1)
        for pp, sl in enumerate(PAIR_LANES):
            lt_ref[:, sl] = jnp.where(lane == 0, carry[2 * pp], jnp.where(lane == 1, carry[2 * pp + 1], 0.0))

    W = ATT_PAIRS * LANES
    qspec = pl.BlockSpec((TQ, W), lambda b, h, i: (b * nq + i, h))
    outs, moved = _call(
        body, ex, name="sb_fwd", grid=(B, HEADS // 2 // ATT_PAIRS, nq),
        in_specs=[qspec,
                  pl.BlockSpec((S, W), lambda b, h, i: (b, ATT_W // W + h)),
                  pl.BlockSpec((S, W), lambda b, h, i: (b, 2 * ATT_W // W + h))],
        out_specs=[qspec, qspec],
        out_shape=[jax.ShapeDtypeStruct((T, ATT_W), F32)] * 2, args=(p, p, p))
    return tuple(outs) if ex is None else tuple(outs) + (moved,)


def sb_bwd_call(p, lt, do, B, S, ex=None):
    T = B * S
    TQ, TK = ATT_TQ, ATT_TK
    nq = S // TQ

    def body(q_ref, k_ref, v_ref, lt_ref, do_ref, dq_ref, dk_ref, dv_ref):
        qi = pl.program_id(2)

        @pl.when(qi == 0)
        def _():
            dk_ref[...] = jnp.zeros_like(dk_ref)
            dv_ref[...] = jnp.zeros_like(dv_ref)

        masks = _head_masks()
        qm = [_pair(q_ref[:, sl] * SB_SCALE2, masks) for sl in PAIR_LANES]
        dom = [_pair(do_ref[:, sl], masks) for sl in PAIR_LANES]
        start = []
        for sl in PAIR_LANES:
            l8 = _rows8(_lane_selector(1), lt_ref[:, sl])
            start += [-_row_of(l8, 0), jnp.zeros((1, TQ), F32), -_row_of(l8, 1), jnp.zeros((1, TQ), F32)]
        row = _iota2((TK, TQ), 0)
        col = _iota2((TK, TQ), 1)
        incl = jnp.where(col <= row, 1.0, 0.0).astype(BF16)
        incl2 = jnp.concatenate([incl, incl], axis=1)
        excl = jnp.where(col < row, 1.0, 0.0).astype(BF16)
        vis = row < col
        dq_ref[...] = jnp.zeros_like(dq_ref)

        def group(k0, pairs, carry, diag):
            heads = [(pp, j) for pp in pairs for j in range(2)]
            n = range(len(heads))
            kf = {pp: k_ref[pl.ds(k0, TK), PAIR_LANES[pp]] for pp in pairs}
            km = {pp: _pair(kf[pp], masks) for pp in pairs}
            v = {pp: v_ref[pl.ds(k0, TK), PAIR_LANES[pp]].astype(BF16) for pp in pairs}
            z = [_nt(kf[pp].astype(BF16), qm[pp][j]) for pp, j in heads]
            da = [_nt(v[pp], dom[pp][j]) for pp, j in heads]
            if diag:
                z = [jnp.where(vis, x, NEG_BIG) for x in z]
            gates = [_log_gates(x) for x in z]
            lb = [g[0] for g in gates]
            keep = [g[1] for g in gates]
            left = [_nn(incl2, jnp.concatenate(_split2(keep[h]), axis=0)) + carry[2 * h] for h in n]
            a = [jnp.exp2(lb[h] + left[h]) for h in n]
            e = [a[h] * da[h] for h in n]
            before = [_nn(excl, e[h].astype(BF16)) + carry[2 * h + 1] for h in n]
            dz = [e[h] - jnp.exp2(lb[h]) * (e[h] + before[h]) for h in n]
            dzb = [x.astype(BF16) for x in dz]
            ab = [x.astype(BF16) for x in a]
            out = []
            for h in n:
                out += [_last_row(left[h]), _last_row(before[h]) + _last_row(e[h])]
            for i, pp in enumerate(pairs):
                sl = PAIR_LANES[pp]
                dk_ref[pl.ds(k0, TK), sl] += _nn(dzb[2 * i], qm[pp][0]) + _nn(dzb[2 * i + 1], qm[pp][1])
                dv_ref[pl.ds(k0, TK), sl] += _nn(ab[2 * i], dom[pp][0]) + _nn(ab[2 * i + 1], dom[pp][1])
                dq_ref[:, sl] += _tn(dzb[2 * i], km[pp][0]) + _tn(dzb[2 * i + 1], km[pp][1])
            return out

        def step(kb, carry, diag):
            k0 = pl.multiple_of(kb * TK, TK)
            out = []
            for g in range(0, ATT_PAIRS, SB_BWD_GROUP):
                out += group(k0, list(range(g, g + SB_BWD_GROUP)), carry[4 * g:4 * (g + SB_BWD_GROUP)], diag)
            return tuple(out)

        carry = lax.fori_loop(0, qi, lambda i, c: step(i, c, False), tuple(start))
        step(qi, carry, True)
        dq_ref[...] *= SB_SCALE

        @pl.when(qi == nq - 1)
        def _():
            dk_ref[...] *= LN2

    W = ATT_PAIRS * LANES
    qspec = pl.BlockSpec((TQ, W), lambda b, h, i: (b * nq + i, h))
    sspec = pl.BlockSpec((S, W), lambda b, h, i: (b, h))
    outs, moved = _call(
        body, ex, name="sb_bwd", grid=(B, HEADS // 2 // ATT_PAIRS, nq),
        in_specs=[qspec,
                  pl.BlockSpec((S, W), lambda b, h, i: (b, ATT_W // W + h)),
                  pl.BlockSpec((S, W), lambda b, h, i: (b, 2 * ATT_W // W + h)),
                  qspec, qspec],
        out_specs=[qspec, sspec, sspec],
        out_shape=[jax.ShapeDtypeStruct((T, ATT_W), F32)] * 3, args=(p, p, p, lt, do))
    return tuple(outs) if ex is None else tuple(outs) + (moved,)


ALL_PAIRS = [slice(i * LANES, (i + 1) * LANES) for i in range(HEADS // 2)]


def _rope_masks(hp):
    grp = _iota2((1, LANES), 1) // ROPE_DIM
    return [grp == ((2 * hp + j) % 4) for j in range(2)]


def _mla_queries(qn_ref, qr_ref, masks):
    out = []
    for pp, sl in enumerate(ALL_PAIRS):
        qnv = qn_ref[:, sl]
        qrv = qr_ref[:, ALL_PAIRS[pp // 2]]
        rmasks = _rope_masks(pp)
        out.append([jnp.concatenate([jnp.where(masks[j], qnv, 0), jnp.where(rmasks[j], qrv, 0)], axis=1).astype(BF16)
                    for j in range(2)])
    return out


def mla_fwd_call(qn, qr, kn, krt, vm, B, S):
    T = B * S
    TQ, TK = ATT_TQ, ATT_TK
    nq = S // TQ

    def body(qn_ref, qr_ref, kn_ref, kr_ref, v_ref, o_ref, lse_ref):
        qi = pl.program_id(1)
        masks = _head_masks()
        qcat = _mla_queries(qn_ref, qr_ref, masks)
        row = _iota2((TQ, TK), 0)
        col = _iota2((TQ, TK), 1)
        vis = col <= row
        o_ref[...] = jnp.zeros_like(o_ref)

        def group(k0, pairs, carry, diag):
            heads = [(pp, j) for pp in pairs for j in range(2)]
            n = range(len(heads))
            krv = kr_ref[pl.ds(k0, TK), :]
            kcat = {pp: jnp.concatenate([kn_ref[pl.ds(k0, TK), ALL_PAIRS[pp]], krv], axis=1) for pp in pairs}
            vmk = {pp: _pair(v_ref[pl.ds(k0, TK), ALL_PAIRS[pp]], masks) for pp in pairs}
            s = [_nt(qcat[pp][j], kcat[pp]) * MLA_SCALE2 for pp, j in heads]
            if diag:
                s = [jnp.where(vis, x, NEG_BIG) for x in s]
            m_new = [jnp.maximum(carry[2 * h], jnp.max(s[h], axis=1, keepdims=True)) for h in n]
            alpha = [jnp.exp2(carry[2 * h] - m_new[h]) for h in n]
            pexp = [jnp.exp2(s[h] - m_new[h]) for h in n]
            out = []
            for h in n:
                out += [m_new[h], alpha[h] * carry[2 * h + 1] + jnp.sum(pexp[h], axis=1, keepdims=True)]
            pb = [x.astype(BF16) for x in pexp]
            for i, pp in enumerate(pairs):
                sl = ALL_PAIRS[pp]
                scale = jnp.where(masks[0], alpha[2 * i], alpha[2 * i + 1])
                o_ref[:, sl] = o_ref[:, sl] * scale + (_nn(pb[2 * i], vmk[pp][0]) + _nn(pb[2 * i + 1], vmk[pp][1]))
            return out

        def step(kb, carry, diag):
            k0 = pl.multiple_of(kb * TK, TK)
            out = []
            for g in range(0, len(ALL_PAIRS), MLA_GROUP):
                out += group(k0, list(range(g, g + MLA_GROUP)), carry[4 * g:4 * (g + MLA_GROUP)], diag)
            return tuple(out)

        neg = jnp.full((TQ, 1), NEG_BIG, F32)
        zero = jnp.zeros((TQ, 1), F32)
        carry = step(qi, (neg, zero) * (2 * len(ALL_PAIRS)), True)
        carry = lax.fori_loop(0, qi, lambda i, c: step(qi - 1 - i, c, False), carry)
        lane = _iota2((TQ, LANES), 1)
        for pp, sl in enumerate(ALL_PAIRS):
            m0, l0, m1, l1 = carry[4 * pp:4 * pp + 4]
            o_ref[:, sl] = o_ref[:, sl] * jnp.where(masks[0], 1.0 / l0, 1.0 / l1)
            lse_ref[:, sl] = jnp.where(lane == 0, m0 * LN2 + jnp.log(l0), jnp.where(lane == 1, m1 * LN2 + jnp.log(l1), 0.0))

    def rows(w):
        return pl.BlockSpec((TQ, w), lambda b, i: (b * nq + i, 0))

    def seq(w):
        return pl.BlockSpec((S, w), lambda b, i: (b, 0))

    return pl.pallas_call(
        body, name="mla_fwd", grid=(B, nq),
        in_specs=[rows(ATT_W), rows(ROPE_W), seq(ATT_W), seq(LANES), seq(ATT_W)],
        out_specs=[rows(ATT_W), rows(ATT_W)],
        out_shape=[jax.ShapeDtypeStruct((T, ATT_W), F32)] * 2,
        compiler_params=_params(("arbitrary", "arbitrary")),
    )(qn, qr, kn, krt, vm)


def mla_bwd_call(qn, qr, kn, krt, vm, o, lse, do, B, S, ex=None):
    T = B * S
    TQ, TK = ATT_TQ, ATT_TK
    nq = S // TQ

    def body(qn_ref, qr_ref, kn_ref, kr_ref, v_ref, o_ref, lse_ref, do_ref,
             dqn_ref, dqr_ref, dkn_ref, dv_ref, dkr_ref):
        qi = pl.program_id(1)

        @pl.when(qi == 0)
        def _():
            dkn_ref[...] = jnp.zeros_like(dkn_ref)
            dv_ref[...] = jnp.zeros_like(dv_ref)
            dkr_ref[...] = jnp.zeros_like(dkr_ref)

        masks = _head_masks()
        qcat = _mla_queries(qn_ref, qr_ref, masks)
        dom, dsum, lse = [], [], []
        for sl in ALL_PAIRS:
            do = do_ref[:, sl]
            dom.append(_pair(do, masks))
            d8 = _rows8(_lane_selector(HEAD_DIM), do * o_ref[:, sl])
            l8 = _rows8(_lane_selector(1), lse_ref[:, sl])
            dsum.append([_row_of(d8, j) for j in range(2)])
            lse.append([_row_of(l8, j) * LOG2E for j in range(2)])
        row = _iota2((TK, TQ), 0)
        col = _iota2((TK, TQ), 1)
        vis = row <= col
        dqn_ref[...] = jnp.zeros_like(dqn_ref)
        dqr_ref[...] = jnp.zeros_like(dqr_ref)

        def group(k0, pairs, diag):
            heads = [(pp, j) for pp in pairs for j in range(2)]
            n = range(len(heads))
            krv = kr_ref[pl.ds(k0, TK), :]
            knv = {pp: kn_ref[pl.ds(k0, TK), ALL_PAIRS[pp]] for pp in pairs}
            kcat = {pp: jnp.concatenate([knv[pp], krv], axis=1) for pp in pairs}
            v = {pp: v_ref[pl.ds(k0, TK), ALL_PAIRS[pp]] for pp in pairs}
            s = [_nt(kcat[pp], qcat[pp][j]) * MLA_SCALE2 for pp, j in heads]
            dp_ = [_nt(v[pp], dom[pp][j]) for pp, j in heads]
            pr = [jnp.exp2(s[h] - lse[pp][j]) for h, (pp, j) in enumerate(heads)]
            if diag:
                pr = [jnp.where(vis, x, 0.0) for x in pr]
            ds = [(pr[h] * (dp_[h] - dsum[pp][j]) * MLA_SCALE).astype(BF16) for h, (pp, j) in enumerate(heads)]
            pb = [x.astype(BF16) for x in pr]
            for i, pp in enumerate(pairs):
                sl = ALL_PAIRS[pp]
                rmasks = _rope_masks(pp)
                kcat_j = [jnp.concatenate([jnp.where(masks[j], knv[pp], 0), jnp.where(rmasks[j], krv, 0)],
                                          axis=1).astype(BF16) for j in range(2)]
                dv_ref[pl.ds(k0, TK), sl] += _nn(pb[2 * i], dom[pp][0]) + _nn(pb[2 * i + 1], dom[pp][1])
                dk = _nn(ds[2 * i], qcat[pp][0]) + _nn(ds[2 * i + 1], qcat[pp][1])
                dq = _tn(ds[2 * i], kcat_j[0]) + _tn(ds[2 * i + 1], kcat_j[1])
                dqn_ref[:, sl] += dq[:, :LANES]
                dqr_ref[:, sl] += dq[:, LANES:]
                dkn_ref[pl.ds(k0, TK), sl] += dk[:, :LANES]
                dkr_ref[pl.ds(k0, TK), sl] += dk[:, LANES:]

        def step(kb, diag):
            k0 = pl.multiple_of(kb * TK, TK)
            for g in range(0, len(ALL_PAIRS), MLA_GROUP):
                group(k0, list(range(g, g + MLA_GROUP)), diag)

        step(qi, True)

        def loop(i, c):
            step(qi - 1 - i, False)
            return c

        lax.fori_loop(0, qi, loop, 0)

    def rows(w):
        return pl.BlockSpec((TQ, w), lambda b, i: (b * nq + i, 0))

    def seq(w):
        return pl.BlockSpec((S, w), lambda b, i: (b, 0))

    outs, moved = _call(
        body, ex, name="mla_bwd", grid=(B, nq),
        in_specs=[rows(ATT_W), rows(ROPE_W), seq(ATT_W), seq(LANES), seq(ATT_W), rows(ATT_W), rows(ATT_W), rows(ATT_W)],
        out_specs=[rows(ATT_W), rows(ATT_W), seq(ATT_W), seq(ATT_W), seq(ATT_W)],
        out_shape=[jax.ShapeDtypeStruct((T, ATT_W), F32)] * 5, args=(qn, qr, kn, krt, vm, o, lse, do))
    return tuple(outs) if ex is None else tuple(outs) + (moved,)


CONV_TC = 256


def _shift_down(x, n):
    return jnp.where(_iota2(x.shape, 0) >= n, pltpu.roll(x, n, 0), 0.0)


def _shift_up(x, n):
    rows = x.shape[0]
    return jnp.where(_iota2(x.shape, 0) < rows - n, pltpu.roll(x, rows - n, 0), 0.0)


def _taps(w_ref):
    return [w_ref[k:k + 1, :] for k in range(3)]


def _conv3(u, w, b):
    return w[0] * _shift_down(u, 2) + w[1] * _shift_down(u, 1) + w[2] * u + b


def _ref_shift_down(ref, n):
    rows = ref.shape[0]
    return jnp.concatenate([_shift_down(ref[0:8, :], n), ref[8 - n:rows - n, :]], axis=0)


def _conv3_ref(u_ref, w, b):
    return w[0] * _ref_shift_down(u_ref, 2) + w[1] * _ref_shift_down(u_ref, 1) + w[2] * u_ref[...] + b


def conv_act_fwd_call(ug, uv, conv_w, conv_b, B, S):
    T = B * S
    nc = D_FF // CONV_TC

    def body(ug_ref, uv_ref, wg_ref, wv_ref, bg_ref, bv_ref, a_ref, cg_ref, cv_ref):
        gate = _conv3_ref(ug_ref, _taps(wg_ref), bg_ref[...])
        val = _conv3_ref(uv_ref, _taps(wv_ref), bv_ref[...])
        a_ref[...] = (gate * (1.0 / (1.0 + jnp.exp(-gate))) * val).astype(BF16)
        cg_ref[...] = gate.astype(BF16)
        cv_ref[...] = val.astype(BF16)

    def blk(rows, off):
        return pl.BlockSpec((rows, CONV_TC), lambda b, j: (b if rows == S else 0, off + j))

    return pl.pallas_call(
        body, name="conv_act_fwd", grid=(B, nc),
        in_specs=[blk(S, 0), blk(S, 0), blk(3, 0), blk(3, nc), blk(1, 0), blk(1, nc)],
        out_specs=[blk(S, 0)] * 3,
        out_shape=[jax.ShapeDtypeStruct((T, D_FF), BF16)] * 3,
        compiler_params=_params(("parallel", "parallel")),
    )(ug, uv, conv_w, conv_w, conv_b, conv_b)


def conv_act_bwd_call(ug, uv, cg, cv, dx2, w_down, conv_w, B, S):
    T = B * S
    nc = D_FF // CONV_TC

    def body(ug_ref, uv_ref, cg_ref, cv_ref, dx_ref, wd_ref, wg_ref, wv_ref,
             dug_ref, duv_ref, dwg_ref, dwv_ref, dbg_ref, dbv_ref):
        @pl.when(pl.program_id(1) == 0)
        def _():
            for r in (dwg_ref, dwv_ref, dbg_ref, dbv_ref):
                r[...] = jnp.zeros_like(r)

        gate = cg_ref[...].astype(F32)
        val = cv_ref[...].astype(F32)
        da = _nt(dx_ref[...], wd_ref[...])
        sig = 1.0 / (1.0 + jnp.exp(-gate))
        dval = da * (gate * sig)
        dgate = da * val * (sig * (1.0 + gate * (1.0 - sig)))
        for u_ref, d, w, du_ref, dw_ref, db_ref in ((ug_ref, dgate, _taps(wg_ref), dug_ref, dwg_ref, dbg_ref),
                                                   (uv_ref, dval, _taps(wv_ref), duv_ref, dwv_ref, dbv_ref)):
            u_ = u_ref[...]
            d1 = _shift_up(d, 1)
            d2 = _shift_up(d, 2)
            du_ref[...] = (w[2] * d + w[1] * d1 + w[0] * d2).astype(BF16)
            db_ref[...] += jnp.sum(d, axis=0, keepdims=True)
            dw_ref[0:1, :] += jnp.sum(d2 * u_, axis=0, keepdims=True)
            dw_ref[1:2, :] += jnp.sum(d1 * u_, axis=0, keepdims=True)
            dw_ref[2:3, :] += jnp.sum(d * u_, axis=0, keepdims=True)

    def blk(rows, off):
        return pl.BlockSpec((rows, CONV_TC), lambda j, b: (b if rows == S else 0, off + j))

    return pl.pallas_call(
        body, name="conv_act_bwd", grid=(nc, B),
        in_specs=[blk(S, 0), blk(S, 0), blk(S, 0), blk(S, 0), pl.BlockSpec((S, D_MODEL), lambda j, b: (b, 0)),
                  pl.BlockSpec((CONV_TC, D_MODEL), lambda j, b: (j, 0)), blk(3, 0), blk(3, nc)],
        out_specs=[blk(S, 0), blk(S, 0), blk(3, 0), blk(3, 0), blk(1, 0), blk(1, 0)],
        out_shape=[jax.ShapeDtypeStruct((T, D_FF), BF16), jax.ShapeDtypeStruct((T, D_FF), BF16),
                   jax.ShapeDtypeStruct((3, D_FF), F32), jax.ShapeDtypeStruct((3, D_FF), F32),
                   jax.ShapeDtypeStruct((1, D_FF), F32), jax.ShapeDtypeStruct((1, D_FF), F32)],
        compiler_params=_params(("parallel", "arbitrary")),
    )(ug, uv, cg, cv, dx2, w_down, conv_w, conv_w)


CHIP_MASKS = ((1, 0), (0, 1), (1, 1))


def _place():
    return lax.axis_index("x"), lax.axis_index("y"), lax.axis_index("c")


HALF_ALIGN = 32


def _any_specs(n):
    return [pl.BlockSpec(memory_space=pl.ANY)] * n


def _splits(shape):
    r, c = shape
    return "rows" if r % HALF_ALIGN == 0 else "cols" if c % (2 * LANES) == 0 else None


def _half(shape, half):
    r, c = shape
    how = _splits(shape)
    if how == "rows":
        return (pl.ds(pl.multiple_of(half * (r // 2), HALF_ALIGN // 2), r // 2), slice(None))
    if how == "cols":
        return (slice(None), pl.ds(pl.multiple_of(half * (c // 2), LANES), c // 2))
    return (slice(None), slice(None))


def _half_shape(shape):
    r, c = shape
    return {"rows": (r // 2, c), "cols": (r, c // 2)}[_splits(shape)]


def _remote(src, dst, send_sem, recv_sem, device):
    return pltpu.make_async_remote_copy(src_ref=src, dst_ref=dst, send_sem=send_sem, recv_sem=recv_sem,
                                        device_id=device, device_id_type=MESH)


class Exchange:
    def __init__(self, ins, out_shape, sems, start, finish):
        self.ins, self.out_shape, self.sems, self.start, self.finish = list(ins), list(out_shape), list(sems), start, finish


def gather_group(shards):
    n = len(shards)
    split = [_splits(s.shape) is not None for s in shards]

    def part(w, half):
        return _half(shards[w].shape, half)

    def copies(ins, outs, sems):
        ici_s, ici_r, _, _, local_sems = sems
        x, y, c = _place()
        chip = 2 * x + y
        local = [pltpu.make_async_copy(ins[w], outs[w].at[chip], local_sems.at[w]) for w in range(n)]
        sends = [_remote(ins[w].at[part(w, c)], outs[w].at[(chip,) + part(w, c)], ici_s.at[w, k], ici_r.at[w, k],
                         (x ^ fx, y ^ fy, c))
                 for w in range(n) for k, (fx, fy) in enumerate(CHIP_MASKS)]
        return local, sends

    def start(ins, outs, sems):
        local, sends = copies(ins, outs, sems)
        for cp in local + sends:
            cp.start()

    def finish(ins, outs, sems):
        ici_s, ici_r, d2d_s, d2d_r, _ = sems
        x, y, c = _place()
        sib = (x, y, 1 - c)
        local, sends = copies(ins, outs, sems)
        for w in range(n):
            for k, (fx, fy) in enumerate(CHIP_MASKS):
                landed = outs[w].at[(2 * (x ^ fx) + (y ^ fy),) + part(w, c)]
                _remote(landed, landed, ici_s.at[w, k], ici_r.at[w, k], sib).wait_recv()
                if split[w]:
                    cp = _remote(landed, landed, d2d_s.at[w, k], d2d_r.at[w, k], sib)
                    cp.start()
                    sends.append(cp)
        for w in range(n):
            for k, (fx, fy) in enumerate(CHIP_MASKS):
                if split[w]:
                    other = outs[w].at[(2 * (x ^ fx) + (y ^ fy),) + part(w, 1 - c)]
                    _remote(other, other, d2d_s.at[w, k], d2d_r.at[w, k], sib).wait_recv()
        for cp in sends:
            cp.wait_send()
        for cp in local:
            cp.wait()

    sems = pltpu.SemaphoreType.DMA((n, 3))
    return Exchange(shards, [jax.ShapeDtypeStruct((N_CHIPS,) + s.shape, s.dtype) for s in shards],
                    [sems, sems, sems, sems, pltpu.SemaphoreType.DMA((n,))], start, finish)


def swap_half(parts):
    n = len(parts)

    def copies(ins, outs, sems):
        x, y, c = _place()
        return [_remote(ins[w].at[(slice(None),) + _half(parts[w].shape[1:], 1 - c)], outs[w], sems[0].at[w], sems[1].at[w],
                        (x, y, 1 - c)) for w in range(n)]

    def start(ins, outs, sems):
        for cp in copies(ins, outs, sems):
            cp.start()

    def finish(ins, outs, sems):
        for cp in copies(ins, outs, sems):
            cp.wait_recv()
            cp.wait_send()

    return Exchange(parts, [jax.ShapeDtypeStruct((N_CHIPS,) + _half_shape(p.shape[1:]), F32) for p in parts],
                    [pltpu.SemaphoreType.DMA((n,))] * 2, start, finish)


def scatter_half(halves):
    n = len(halves)

    def copies(ins, outs, sems):
        x, y, c = _place()
        return [_remote(ins[w].at[2 * (x ^ fx) + (y ^ fy)], outs[w].at[k], sems[0].at[w, k], sems[1].at[w, k],
                        (x ^ fx, y ^ fy, c))
                for w in range(n) for k, (fx, fy) in enumerate(CHIP_MASKS)]

    def start(ins, outs, sems):
        for cp in copies(ins, outs, sems):
            cp.start()

    def finish(ins, outs, sems):
        for cp in copies(ins, outs, sems):
            cp.wait_recv()
            cp.wait_send()

    return Exchange(halves, [jax.ShapeDtypeStruct((3,) + h.shape[1:], h.dtype) for h in halves],
                    [pltpu.SemaphoreType.DMA((n, 3))] * 2, start, finish)


def swap_final(finals, shapes):
    n = len(finals)

    def copies(ins, outs, sems):
        x, y, c = _place()
        mine = [outs[w].at[_half(shapes[w], c)] for w in range(n)]
        local = [pltpu.make_async_copy(ins[w], mine[w], sems[2].at[w]) for w in range(n)]
        sends = [_remote(ins[w], mine[w], sems[0].at[w], sems[1].at[w], (x, y, 1 - c)) for w in range(n)]
        return local, sends

    def start(ins, outs, sems):
        local, sends = copies(ins, outs, sems)
        for cp in local + sends:
            cp.start()

    def finish(ins, outs, sems):
        x, y, c = _place()
        local, sends = copies(ins, outs, sems)
        for w in range(n):
            got = outs[w].at[_half(shapes[w], 1 - c)]
            _remote(got, got, sems[0].at[w], sems[1].at[w], (x, y, 1 - c)).wait_recv()
        for cp in sends:
            cp.wait_send()
        for cp in local:
            cp.wait()

    return Exchange(finals, [jax.ShapeDtypeStruct(tuple(s), F32) for s in shapes],
                    [pltpu.SemaphoreType.DMA((n,))] * 3, start, finish)


def exchange_call(name, ex):
    n, m = len(ex.ins), len(ex.out_shape)

    def body(*refs):
        ins, outs, sems = refs[:n], refs[n:n + m], refs[n + m:]
        ex.start(ins, outs, sems)
        ex.finish(ins, outs, sems)

    return pl.pallas_call(body, name=name, in_specs=_any_specs(n), out_specs=_any_specs(m), out_shape=ex.out_shape,
                          scratch_shapes=ex.sems, compiler_params=_params())(*ex.ins)


def _call(body, ex, *, name, grid, in_specs, out_specs, out_shape, args, scratch_shapes=()):
    sem = ("arbitrary",) * len(grid)
    if ex is None:
        outs = pl.pallas_call(body, name=name, grid=grid, in_specs=in_specs, out_specs=out_specs, out_shape=out_shape,
                              scratch_shapes=list(scratch_shapes), compiler_params=_params(sem))(*args)
        return outs, None
    ni, no, ns = len(in_specs), len(out_specs), len(scratch_shapes)
    ne, me = len(ex.ins), len(ex.out_shape)

    def wrapped(*refs):
        own_in, ex_in = refs[:ni], refs[ni:ni + ne]
        own_out, ex_out = refs[ni + ne:ni + ne + no], refs[ni + ne + no:ni + ne + no + me]
        own_scr, ex_sems = refs[ni + ne + no + me:ni + ne + no + me + ns], refs[ni + ne + no + me + ns:]
        ids = [pl.program_id(a) for a in range(len(grid))]
        first = functools.reduce(jnp.logical_and, [i == 0 for i in ids])
        last = functools.reduce(jnp.logical_and, [i == g - 1 for i, g in zip(ids, grid)])

        @pl.when(first)
        def _():
            ex.start(ex_in, ex_out, ex_sems)

        body(*own_in, *own_out, *own_scr)

        @pl.when(last)
        def _():
            ex.finish(ex_in, ex_out, ex_sems)

    outs = pl.pallas_call(
        wrapped, name=name, grid=grid, in_specs=list(in_specs) + _any_specs(ne),
        out_specs=list(out_specs) + _any_specs(me), out_shape=list(out_shape) + ex.out_shape,
        scratch_shapes=list(scratch_shapes) + ex.sems, compiler_params=_params(sem))(*args, *ex.ins)
    return outs[:no], outs[no:]


def _row_tile(rows, cap, mult=8):
    return max([t for t in range(mult, min(rows, cap) + 1, mult) if rows % t == 0] or [rows])


def add_half_call(name, part, got, where):
    _, rh, cols = got.shape
    tr = _row_tile(rh, 176, 16)
    nb = rh // tr
    by_rows = _splits(part.shape[1:]) == "rows"

    def body(where_ref, p_ref, g_ref, own_ref, send_ref):
        t = p_ref[...] + g_ref[...]
        send_ref[...] = t.astype(BF16)
        chip = where_ref[1]
        own_ref[...] = p_ref[chip] + g_ref[chip]

    blk = (N_CHIPS, tr, cols)
    return pl.pallas_call(
        body, name=name,
        grid_spec=pltpu.PrefetchScalarGridSpec(
            num_scalar_prefetch=1, grid=(nb,),
            in_specs=[pl.BlockSpec(blk, (lambda i, where_ref: (0, where_ref[0] * nb + i, 0)) if by_rows
                                   else (lambda i, where_ref: (0, i, where_ref[0]))),
                      pl.BlockSpec(blk, lambda i, where_ref: (0, i, 0))],
            out_specs=[pl.BlockSpec((tr, cols), lambda i, where_ref: (i, 0)),
                       pl.BlockSpec(blk, lambda i, where_ref: (0, i, 0))]),
        out_shape=[jax.ShapeDtypeStruct((rh, cols), F32), jax.ShapeDtypeStruct(got.shape, BF16)],
        compiler_params=_params(("parallel",)),
    )(where, part, got)


def sum_chips_call(name, own, got):
    _, rh, cols = got.shape
    tr = _row_tile(rh, 176, 16)

    def body(h_ref, g_ref, o_ref):
        o_ref[...] = ((h_ref[...] + g_ref[0].astype(F32)) + g_ref[1].astype(F32)) + g_ref[2].astype(F32)

    return pl.pallas_call(
        body, name=name, grid=(rh // tr,),
        in_specs=[pl.BlockSpec((tr, cols), lambda i: (i, 0)), pl.BlockSpec((3, tr, cols), lambda i: (0, i, 0))],
        out_specs=pl.BlockSpec((tr, cols), lambda i: (i, 0)),
        out_shape=jax.ShapeDtypeStruct((rh, cols), F32),
        compiler_params=_params(("parallel",)),
    )(own, got)


def _adamw(w, g, m, v):
    m = ADAM_B1 * m + (1.0 - ADAM_B1) * g
    v = ADAM_B2 * v + (1.0 - ADAM_B2) * (g * g)
    m_hat = m / (1.0 - ADAM_B1 ** ADAM_STEP)
    v_hat = v / (1.0 - ADAM_B2 ** ADAM_STEP)
    delta = -ADAM_LR * (m_hat / (jnp.sqrt(v_hat) + ADAM_EPS) + ADAM_WD * w)
    return delta, m, v


def adamw_call(name, g, w, m, v):
    r, cols = w.shape
    tr = r if r % 8 else _row_tile(r, 256)

    def body(g_ref, w_ref, m_ref, v_ref, go_ref, d_ref, nm_ref, nv_ref):
        g = g_ref[...]
        go_ref[...] = g
        d_ref[...], nm_ref[...], nv_ref[...] = _adamw(w_ref[...], g, m_ref[...], v_ref[...])

    spec = pl.BlockSpec((tr, cols), lambda i: (i, 0))
    return pl.pallas_call(
        body, name=name, grid=(r // tr,), in_specs=[spec] * 4, out_specs=[spec] * 4,
        out_shape=[jax.ShapeDtypeStruct((r, cols), F32)] * 4,
        compiler_params=_params(("parallel",)),
    )(g, w, m, v)


def allsum_small_call(v):
    R = v.shape[0]

    def body(v_ref, out_ref, buf, send_sems, recv_sems):
        x, y, c = _place()
        me = 4 * x + 2 * y + c
        buf[me] = v_ref[...]
        sends = []
        for k in range(1, N_DEV):
            fx, fy, fc = (k >> 2) & 1, (k >> 1) & 1, k & 1
            cp = pltpu.make_async_remote_copy(
                src_ref=v_ref, dst_ref=buf.at[me], send_sem=send_sems.at[k - 1], recv_sem=recv_sems.at[k - 1],
                device_id=(x ^ fx, y ^ fy, c ^ fc), device_id_type=MESH)
            cp.start()
            sends.append(cp)
        for k in range(1, N_DEV):
            pltpu.make_async_remote_copy(
                src_ref=v_ref, dst_ref=buf.at[me ^ k], send_sem=send_sems.at[k - 1], recv_sem=recv_sems.at[k - 1],
                device_id=(x, y, c), device_id_type=MESH).wait_recv()
        acc = buf[0]
        for d in range(1, N_DEV):
            acc = acc + buf[d]
        out_ref[...] = acc
        for cp in sends:
            cp.wait_send()

    vm = pl.BlockSpec(memory_space=pltpu.VMEM)
    return pl.pallas_call(
        body, name="allsum_small", in_specs=[vm], out_specs=vm,
        out_shape=jax.ShapeDtypeStruct((R, LANES), F32),
        scratch_shapes=[pltpu.VMEM((N_DEV, R, LANES), F32), pltpu.SemaphoreType.DMA((N_DEV - 1,)),
                        pltpu.SemaphoreType.DMA((N_DEV - 1,))],
        compiler_params=_params(),
    )(v)


def _slab(flat, mult):
    n = flat.shape[-1]
    rows = -(-n // (LANES * mult)) * mult
    flat = jnp.pad(flat, [(0, 0)] * (flat.ndim - 1) + [(0, rows * LANES - n)])
    return flat.reshape(flat.shape[:-1] + (rows, LANES))


def full_from_chips(blocks, by_col):
    _, r, c = blocks.shape
    return blocks.transpose(1, 0, 2).reshape(r, N_CHIPS * c) if by_col else blocks.reshape(N_CHIPS * r, c)


def chips_from_full(full, by_col):
    if by_col:
        r, c = full.shape[0], full.shape[1] // N_CHIPS
        return full.reshape(r, N_CHIPS, c).transpose(1, 0, 2)
    return full.reshape(N_CHIPS, full.shape[0] // N_CHIPS, full.shape[1])


SMALL_PACK = SMALL_W + ("loss", "conv_w")
SMALL_PACK_N = {**SMALL_N, "loss": 1, "conv_w": 3 * 2 * D_FF}


def pack_small(vals):
    zero = jnp.zeros((1,), F32)
    return _slab(jnp.concatenate([vals[n].reshape(-1) if n in vals else jnp.tile(zero, SMALL_PACK_N[n])
                                  for n in SMALL_PACK]), 8)


def unpack_small(slab, shapes):
    flat = slab.reshape(-1)
    out, off = {}, 0
    for n in SMALL_PACK:
        out[n] = flat[off:off + SMALL_PACK_N[n]].reshape(shapes[n])
        off += SMALL_PACK_N[n]
    return out


def _split_heads(w, a, b):
    r = w.shape[0]
    w3 = w.reshape(r, HEADS, a + b)
    return w3[:, :, :a].reshape(r, HEADS * a), w3[:, :, a:].reshape(r, HEADS * b)


def _merge_heads(wa, wb, a, b):
    r = wa.shape[0]
    return jnp.concatenate([wa.reshape(r, HEADS, a), wb.reshape(r, HEADS, b)], axis=2).reshape(r, HEADS * (a + b))


def kernel(x, positions, g_mix, w_in, g_cq, w_uq, g_ckv, w_ukv, g_sb_out, g_mla_out, w_out, g_ffn, w_up, conv_w, conv_b, w_down, g_final, loss_target, m_g_mix, m_w_in, m_g_cq, m_w_uq, m_g_ckv, m_w_ukv, m_g_sb_out, m_g_mla_out, m_w_out, m_g_ffn, m_w_up, m_conv_w, m_conv_b, m_w_down, m_g_final, v_g_mix, v_w_in, v_g_cq, v_w_uq, v_g_ckv, v_w_ukv, v_g_sb_out, v_g_mla_out, v_w_out, v_g_ffn, v_w_up, v_conv_w, v_conv_b, v_w_down, v_g_final):
    given = dict(locals())
    B, S, _ = x.shape
    T = B * S
    w_big = {n: given[n][0].T if n == "w_in" else given[n][0] for n in BIG_W}
    m_big = {n: given["m_" + n][0].T if n == "w_in" else given["m_" + n][0] for n in BIG_W}
    v_big = {n: given["v_" + n][0].T if n == "w_in" else given["v_" + n][0] for n in BIG_W}
    shard_shape = {n: w_big[n].shape for n in BIG_W}

    first = ("w_in", "w_uq", "w_ukv")
    later = ("w_out", "w_up", "w_down", "conv_w")
    x2d = x.reshape(T, D_MODEL)
    h, got_w = rmsnorm_fwd_call("norm_mix", x2d, g_mix, ex=gather_group([w_big[n].astype(BF16) for n in first]))
    full = {n: full_from_chips(g_, BIG_SHARD[n][2]) for n, g_ in zip(first, got_w) if n != "w_in"}
    gather_later = gather_group([w_big[n] if n == "conv_w" else w_big[n].astype(BF16) for n in later])
    w_in_t = jnp.pad(got_w[0].reshape(IN_COLS, D_MODEL), ((0, IN_COLS_PAD - IN_COLS), (0, 0)))
    w_uq_p = jnp.concatenate(_split_heads(full["w_uq"], HEAD_DIM, ROPE_DIM), axis=1)
    w_ukv_p = jnp.concatenate(_split_heads(full["w_ukv"], HEAD_DIM, HEAD_DIM), axis=1)

    half = ROPE_DIM // 2
    inv_freq = 1.0 / (ROPE_BASE ** (jnp.arange(half, dtype=F32) * (2.0 / ROPE_DIM)))
    cos, sin = rope_tab_call(positions.reshape(T, 1), jnp.tile(inv_freq, LANES // half).reshape(1, LANES))
    p = matmul_call("proj_in", h, w_in_t, "nt", tn=IN_COLS_PAD // 2)
    qn, qr, kn, vm, krt, cqn, ckvn = mla_prep_fwd_call(p, cos, sin, g_cq, g_ckv, w_uq_p, w_ukv_p)
    o_sb, lt_sb, got_w = sb_fwd_call(p, B, S, ex=gather_later)
    w_up4 = got_w[1]
    full.update({n: full_from_chips(g_, BIG_SHARD[n][2]) for n, g_ in zip(later, got_w) if n != "w_up"})
    conv_w_full = full["conv_w"]
    o_mla, lse = mla_fwd_call(qn, qr, kn, krt, vm, B, S)
    o_cat = outnorm_fwd_call(o_sb, o_mla, g_sb_out, g_mla_out)
    x1, hn = proj_out_norm_call(o_cat, full["w_out"], x2d, g_ffn)
    u_g, u_v, act, c_g, c_v = ffn_up_conv_call(hn, w_up4, conv_w_full, conv_b, S)
    dx2, dx2b, loss_row, dg_final = ffn_down_loss_call(
        act, full["w_down"], x1, g_final.reshape(1, D_MODEL), loss_target.reshape(T, D_MODEL))

    xi, yi, ci = _place()
    chip = (2 * xi + yi).astype(jnp.int32).reshape(1)
    where = jnp.stack([ci, 2 * xi + yi]).astype(jnp.int32)

    def add_halves(names, parts, sib_rows):
        return [add_half_call("add_half_" + n, p_, s_, where) for n, p_, s_ in zip(names, parts, sib_rows)]

    def sum_chips(names, halves, from_chips):
        return [sum_chips_call("sum_chips_" + n, h_[0], f_) for n, h_, f_ in zip(names, halves, from_chips)]

    ffn_w = ("w_down", "w_up")
    parts_ffn = [chips_from_full(wgrad_call("wgrad_down", act, dx2b, tn=512), False)]
    du_g, du_v, dw_up4, dcw_g, dcw_v, dcb_g, dcb_v = conv_bwd_wgrad_up_call(
        hn, u_g, u_v, c_g, c_v, dx2b, full["w_down"], conv_w_full, S)
    parts_ffn.append(dw_up4)
    dx1, dg_ffn, sib_ffn = ffn_up_bwd_call(du_g, du_v, w_up4, x1, g_ffn, dx2, swap_half(parts_ffn))
    parts_out = [chips_from_full(wgrad_call("wgrad_out", o_cat, dx1), False)]
    do_sb, do_mla, dg_sb_out, dg_mla_out, sib_out = proj_out_bwd_call(
        dx1, full["w_out"], o_sb, o_mla, g_sb_out, g_mla_out, swap_half(parts_out))
    early = ffn_w + ("w_out",)
    halves = add_halves(early, parts_ffn + parts_out, list(sib_ffn) + list(sib_out))
    dq_sb, dk_sb, dv_sb, from_chips = sb_bwd_call(p, lt_sb, do_sb, B, S, ex=scatter_half([h_[1] for h_ in halves]))
    finals = sum_chips(early, halves, from_chips)
    dqn, dqr4, dkn, dvm, dkrt4, done = mla_bwd_call(qn, qr, kn, krt, vm, o_mla, lse, do_mla, B, S,
        ex=swap_final(finals, [shard_shape[n] for n in early]))
    grads = dict(zip(early, done))
    dcq, dckvr, dq_cat, dkv_cat, dg_cq, dg_ckv = mla_prep_bwd_call(
        p, cos, sin, g_cq, g_ckv, w_uq_p, w_ukv_p, dqn, dqr4, dkn, dvm, dkrt4)
    dw_uq_p = wgrad_call("wgrad_uq", cqn, dq_cat)
    dw_ukv_p = wgrad_call("wgrad_ukv", ckvn, dkv_cat)
    dp = (dq_sb, dk_sb, dv_sb, dcq, dckvr)
    late = ("w_uq", "w_ukv", "w_in")
    parts_late = [chips_from_full(g_, True) for g_ in (
        _merge_heads(dw_uq_p[:, :ATT_W], dw_uq_p[:, ATT_W:], HEAD_DIM, ROPE_DIM),
        _merge_heads(dw_ukv_p[:, :ATT_W], dw_ukv_p[:, ATT_W:], HEAD_DIM, HEAD_DIM))]
    parts_late.append(chips_from_full(wgrad_in_call(h, dp)[:IN_COLS], False))
    dh, sib_late = proj_in_bwd_call(dp, w_in_t, swap_half(parts_late))
    halves = add_halves(late, parts_late, sib_late)
    grad_x, dg_mix, from_chips = rmsnorm_bwd_call(
        "norm_mix_bwd", x2d, g_mix, dh, dx1, ex=scatter_half([h_[1] for h_ in halves]))
    finals = sum_chips(late, halves, from_chips)
    grads.update(zip(late, exchange_call("swap_final_late", swap_final(finals, [shard_shape[n] for n in late]))))

    shapes = {n: given[n].shape for n in SMALL_W}
    shapes.update(loss=(), conv_w=(3, 2 * D_FF))
    small_g = {"g_mix": dg_mix, "g_cq": dg_cq, "g_ckv": dg_ckv, "g_sb_out": dg_sb_out, "g_mla_out": dg_mla_out,
               "g_ffn": dg_ffn, "conv_b": jnp.concatenate([dcb_g, dcb_v], axis=1), "g_final": dg_final,
               "loss": loss_row[0, :1], "conv_w": jnp.concatenate([dcw_g, dcw_v], axis=1)}
    gs_slab = allsum_small_call(pack_small(small_g))
    small_in = [pack_small({n: given[pre + n] for n in SMALL_W}) for pre in ("", "m_", "v_")]
    small_out = [unpack_small(s, shapes) for s in adamw_call("adamw_small", gs_slab, *small_in)]
    cw_cols = BIG_SHARD["conv_w"][1]
    grads["conv_w"] = lax.dynamic_slice_in_dim(small_out[0]["conv_w"], chip[0] * cw_cols, cw_cols, axis=1)

    big_out = {n: adamw_call("adamw_" + n, grads[n], w_big[n], m_big[n], v_big[n]) for n in BIG_W}
    weights = ("g_mix", "w_in", "g_cq", "w_uq", "g_ckv", "w_ukv", "g_sb_out", "g_mla_out", "w_out", "g_ffn",
               "w_up", "conv_w", "conv_b", "w_down", "g_final")
    outs = [small_out[0]["loss"], grad_x.reshape(B, S, D_MODEL)]
    for k in range(4):
        for n in weights:
            if n in BIG_W:
                outs.append((big_out[n][k].T if n == "w_in" else big_out[n][k])[None])
            else:
                outs.append(small_out[k][n])
    return tuple(outs)
```

```python
import functools

import jax
import jax.numpy as jnp
from jax import lax
from jax.experimental import pallas as pl
from jax.experimental.pallas import tpu as pltpu

F32 = jnp.float32
BF16 = jnp.bfloat16
MESH = pl.DeviceIdType.MESH

D_MODEL = 1024
HEADS = 8
HEAD_DIM = 64
ATT_W = HEADS * HEAD_DIM
ROPE_DIM = 32
ROPE_W = HEADS * ROPE_DIM
QK_DIM = HEAD_DIM + ROPE_DIM
Q_RANK = 384
KV_RANK = 256
D_FF = 2816
IN_COLS = 2208
IN_COLS_PAD = 2304
EPS = 1e-6
ROPE_BASE = 10000.0
SB_SCALE = HEAD_DIM ** -0.5
SB_SCALE2 = SB_SCALE * 1.4426950408889634
MLA_SCALE = QK_DIM ** -0.5
LOG2E = 1.4426950408889634
LN2 = 0.6931471805599453
MLA_SCALE2 = MLA_SCALE * LOG2E
LANES = 128
N_CHIPS = 4
N_DEV = 8
VMEM_LIMIT = 48 * 1024 * 1024
ATT_TQ = 256
ATT_TK = 256
ATT_PAIRS = 4
PAIR_LANES = [slice(i * LANES, (i + 1) * LANES) for i in range(ATT_PAIRS)]
SB_BWD_GROUP = 2
SB_FWD_GROUP = 4
MLA_GROUP = 4
NEG_BIG = -1e30

ADAM_LR = 0.001
ADAM_B1 = 0.9
ADAM_B2 = 0.999
ADAM_EPS = 1e-08
ADAM_WD = 0.01
ADAM_STEP = 10

BIG_W = ("w_in", "w_uq", "w_ukv", "w_out", "w_up", "conv_w", "w_down")
BIG_SHARD = {
    "w_in": (D_MODEL, IN_COLS // 4, True),
    "w_uq": (Q_RANK, HEADS * QK_DIM // 4, True),
    "w_ukv": (KV_RANK, 2 * ATT_W // 4, True),
    "w_out": (2 * ATT_W // 4, D_MODEL, False),
    "w_up": (D_MODEL, 2 * D_FF // 4, True),
    "conv_w": (3, 2 * D_FF // 4, True),
    "w_down": (D_FF // 4, D_MODEL, False),
}
SMALL_W = ("g_mix", "g_cq", "g_ckv", "g_sb_out", "g_mla_out", "g_ffn", "conv_b", "g_final")
SMALL_N = {"g_mix": D_MODEL, "g_cq": Q_RANK, "g_ckv": KV_RANK, "g_sb_out": ATT_W, "g_mla_out": ATT_W,
           "g_ffn": D_MODEL, "conv_b": 2 * D_FF, "g_final": D_MODEL}


def _params(sem=None, **kw):
    return pltpu.CompilerParams(dimension_semantics=sem, vmem_limit_bytes=VMEM_LIMIT, **kw)


def _dot(a, b, dims):
    return lax.dot_general(a, b, (dims, ((), ())), preferred_element_type=F32)


def _nn(a, b):
    return _dot(a, b, ((1,), (0,)))


def _nt(a, b):
    return _dot(a, b, ((1,), (1,)))


def _tn(a, b):
    return _dot(a, b, ((0,), (0,)))


def _split2(x):
    hi = x.astype(BF16)
    lo = (x - hi.astype(F32)).astype(BF16)
    return hi, lo


def _split3(x):
    hi = x.astype(BF16)
    r1 = x - hi.astype(F32)
    mid = r1.astype(BF16)
    return hi, mid, (r1 - mid.astype(F32)).astype(BF16)


def _rms_r(x, d):
    return lax.rsqrt(jnp.sum(x * x, axis=-1, keepdims=True) * (1.0 / d) + EPS)


def _rms_bwd(x, g, dy, d):
    r = _rms_r(x, d)
    xhat = x * r
    gy = dy * g
    dx = r * (gy - xhat * (jnp.sum(xhat * gy, axis=-1, keepdims=True) * (1.0 / d)))
    return dx, jnp.sum(dy * xhat, axis=0, keepdims=True)


def _rot(x):
    lane = lax.broadcasted_iota(jnp.int32, x.shape, x.ndim - 1)
    n = x.shape[-1]
    return jnp.where((lane & 31) < 16, pltpu.roll(x, n - 16, x.ndim - 1), pltpu.roll(x, 16, x.ndim - 1))


def _fold4(x):
    return x + pltpu.roll(x, 32, 1) + pltpu.roll(x, 64, 1) + pltpu.roll(x, 96, 1)


def matmul_call(name, a, b, mode, out_dtype=F32, res=None, tm=512, tn=None, ex=None):
    M, K = a.shape
    N = b.shape[1] if mode == "nn" else b.shape[0]
    tn = N if tn is None else tn
    assert M % tm == 0 and N % tn == 0

    def body(*refs):
        if res is None:
            a_ref, b_ref, o_ref = refs
        else:
            a_ref, b_ref, r_ref, o_ref = refs
        av = a_ref[...].astype(BF16)
        bv = b_ref[...].astype(BF16)
        acc = _nn(av, bv) if mode == "nn" else _nt(av, bv)
        if res is not None:
            acc = r_ref[...] + acc
        o_ref[...] = acc.astype(out_dtype)

    in_specs = [pl.BlockSpec((tm, K), lambda j, i: (i, 0))]
    if mode == "nn":
        in_specs.append(pl.BlockSpec((K, tn), lambda j, i: (0, j)))
    else:
        in_specs.append(pl.BlockSpec((tn, K), lambda j, i: (j, 0)))
    args = [a, b]
    if res is not None:
        in_specs.append(pl.BlockSpec((tm, tn), lambda j, i: (i, j)))
        args.append(res)
    outs, moved = _call(body, ex, name=name, grid=(N // tn, M // tm), in_specs=in_specs,
                        out_specs=[pl.BlockSpec((tm, tn), lambda j, i: (i, j))],
                        out_shape=[jax.ShapeDtypeStruct((M, N), out_dtype)], args=args)
    return outs[0] if ex is None else (outs[0], moved)


def _rows(tm, width):
    return pl.BlockSpec((tm, width), lambda i: (i, 0))


def _whole(a, **kw):
    return pl.BlockSpec(a.shape, lambda i: (0,) * a.ndim, **kw)


def proj_out_norm_call(o_cat, w_out, x, g_ffn, tm=512):
    T, K = o_cat.shape
    N = w_out.shape[1]

    def body(a_ref, w_ref, x_ref, g_ref, x1_ref, hn_ref):
        x1 = x_ref[...] + _nn(a_ref[...], w_ref[...])
        x1_ref[...] = x1
        hn_ref[...] = ((x1 * _rms_r(x1, N)) * g_ref[...]).astype(BF16)

    return pl.pallas_call(
        body, name="proj_out", grid=(T // tm,),
        in_specs=[_rows(tm, K), _whole(w_out), _rows(tm, N), _whole(g_ffn)], out_specs=[_rows(tm, N)] * 2,
        out_shape=[jax.ShapeDtypeStruct((T, N), F32), jax.ShapeDtypeStruct((T, N), BF16)],
        compiler_params=_params(("parallel",)),
    )(o_cat, w_out, x, g_ffn)


def ffn_down_loss_call(act, w_down, x1, g, target, tm=512):
    T, K = act.shape
    d = w_down.shape[1]

    def body(a_ref, w_ref, x1_ref, g_ref, t_ref, dx_ref, dxb_ref, loss_ref, dg_ref):
        @pl.when(pl.program_id(0) == 0)
        def _():
            loss_ref[...] = jnp.zeros_like(loss_ref)
            dg_ref[...] = jnp.zeros_like(dg_ref)

        x = x1_ref[...] + _nn(a_ref[...], w_ref[...])
        g = g_ref[...]
        y = (x * _rms_r(x, d)) * g
        err = y - t_ref[...]
        loss_ref[...] += jnp.sum(jnp.sum(err * err, axis=1, keepdims=True), axis=0, keepdims=True) * (0.5 / d)
        dx, dg = _rms_bwd(x, g, err * (1.0 / d), d)
        dx_ref[...] = dx
        dxb_ref[...] = dx.astype(BF16)
        dg_ref[...] += dg

    return pl.pallas_call(
        body, name="ffn_down_loss", grid=(T // tm,),
        in_specs=[_rows(tm, K), _whole(w_down), _rows(tm, d), _whole(g), _rows(tm, d)],
        out_specs=[_rows(tm, d), _rows(tm, d), pl.BlockSpec((1, LANES), lambda i: (0, 0)), _whole(g)],
        out_shape=[jax.ShapeDtypeStruct((T, d), F32), jax.ShapeDtypeStruct((T, d), BF16),
                   jax.ShapeDtypeStruct((1, LANES), F32), jax.ShapeDtypeStruct((1, d), F32)],
        compiler_params=_params(("arbitrary",)),
    )(act, w_down, x1, g, target)


def proj_out_bwd_call(dx1, w_out, o_sb, o_mla, g_sb, g_mla, ex, tm=512):
    T, N = dx1.shape

    def body(d_ref, w_ref, a_ref, b_ref, ga_ref, gb_ref, da_ref, db_ref, dga_ref, dgb_ref):
        @pl.when(pl.program_id(0) == 0)
        def _():
            dga_ref[...] = jnp.zeros_like(dga_ref)
            dgb_ref[...] = jnp.zeros_like(dgb_ref)

        d = _nt(d_ref[...].astype(BF16), w_ref[...])
        da, dga = _rms_bwd(a_ref[...], ga_ref[...], d[:, :ATT_W], ATT_W)
        db, dgb = _rms_bwd(b_ref[...], gb_ref[...], d[:, ATT_W:], ATT_W)
        da_ref[...] = da
        db_ref[...] = db
        dga_ref[...] += dga
        dgb_ref[...] += dgb

    outs, moved = _call(
        body, ex, name="proj_out_bwd", grid=(T // tm,),
        in_specs=[_rows(tm, N), _whole(w_out), _rows(tm, ATT_W), _rows(tm, ATT_W), _whole(g_sb), _whole(g_mla)],
        out_specs=[_rows(tm, ATT_W), _rows(tm, ATT_W), _whole(g_sb), _whole(g_mla)],
        out_shape=[jax.ShapeDtypeStruct((T, ATT_W), F32), jax.ShapeDtypeStruct((T, ATT_W), F32),
                   jax.ShapeDtypeStruct((1, ATT_W), F32), jax.ShapeDtypeStruct((1, ATT_W), F32)],
        args=(dx1, w_out, o_sb, o_mla, g_sb, g_mla))
    return tuple(outs) + (moved,)


def wgrad_call(name, a, b, tn=None, tt=512, by_chip=False):
    T, M = a.shape
    N = b.shape[1]
    tn = N if tn is None else tn
    tt = min(tt, T)
    assert T % tt == 0 and N % tn == 0
    if by_chip:
        out_spec = pl.BlockSpec((None, M, tn), lambda j, t: (j, 0, 0))
        out_shape = jax.ShapeDtypeStruct((N // tn, M, tn), F32)
    else:
        out_spec = pl.BlockSpec((M, tn), lambda j, t: (0, j))
        out_shape = jax.ShapeDtypeStruct((M, N), F32)

    def body(a_ref, b_ref, o_ref):
        @pl.when(pl.program_id(1) == 0)
        def _():
            o_ref[...] = jnp.zeros_like(o_ref)

        o_ref[...] += _tn(a_ref[...].astype(BF16), b_ref[...].astype(BF16))

    return pl.pallas_call(
        body, name=name, grid=(N // tn, T // tt),
        in_specs=[pl.BlockSpec((tt, M), lambda j, t: (t, 0)), pl.BlockSpec((tt, tn), lambda j, t: (t, j))],
        out_specs=out_spec, out_shape=out_shape,
        compiler_params=_params(("parallel", "arbitrary")),
    )(a, b)


UP_COLS = 2 * D_FF // N_CHIPS


def _shift_down_after(u, prev8, n):
    top = pltpu.roll(jnp.concatenate([prev8, u[0:8]], axis=0), n, 0)[8:16]
    return jnp.concatenate([top, pltpu.roll(u, n, 0)[8:]], axis=0)


def ffn_up_conv_call(hn, w4, conv_w, conv_b, S, tm=512):
    T, K = hn.shape
    per_seq = S // tm

    def body(a_ref, wg_ref, wv_ref, cg_ref, cv_ref, bg_ref, bv_ref, ug_ref, uv_ref, act_ref, og_ref, ov_ref, pg_ref, pv_ref):
        @pl.when(pl.program_id(1) % per_seq == 0)
        def _():
            pg_ref[...] = jnp.zeros_like(pg_ref)
            pv_ref[...] = jnp.zeros_like(pv_ref)

        a = a_ref[...]
        outs = []
        for w_ref, c_ref, b_ref, u_ref, prev_ref in ((wg_ref, cg_ref, bg_ref, ug_ref, pg_ref),
                                                     (wv_ref, cv_ref, bv_ref, uv_ref, pv_ref)):
            u = _nn(a, w_ref[...])
            u_ref[...] = u
            prev8 = prev_ref[...]
            taps = _taps(c_ref)
            outs.append(taps[0] * _shift_down_after(u, prev8, 2) + taps[1] * _shift_down_after(u, prev8, 1)
                        + taps[2] * u + b_ref[...])
            prev_ref[...] = u[tm - 8:, :]
        gate, val = outs
        sig = 1.0 / (1.0 + jnp.exp(-gate))
        silu = gate * sig
        act_ref[...] = (silu * val).astype(BF16)
        og_ref[...] = silu.astype(BF16)
        ov_ref[...] = (val * (sig * (1.0 + gate * (1.0 - sig)))).astype(BF16)

    out = pl.BlockSpec((tm, UP_COLS), lambda j, i: (i, j))

    def cols(rows, off):
        return pl.BlockSpec((rows, UP_COLS), lambda j, i: (0, off + j))

    return pl.pallas_call(
        body, name="ffn_up_conv", grid=(2, T // tm),
        in_specs=[pl.BlockSpec((tm, K), lambda j, i: (i, 0)),
                  pl.BlockSpec((None, K, UP_COLS), lambda j, i: (j, 0, 0)),
                  pl.BlockSpec((None, K, UP_COLS), lambda j, i: (2 + j, 0, 0)),
                  cols(3, 0), cols(3, 2), cols(1, 0), cols(1, 2)],
        out_specs=[out] * 5,
        out_shape=[jax.ShapeDtypeStruct((T, D_FF), F32)] * 2 + [jax.ShapeDtypeStruct((T, D_FF), BF16)] * 3,
        scratch_shapes=[pltpu.VMEM((8, UP_COLS), F32)] * 2,
        compiler_params=_params(("arbitrary", "arbitrary")),
    )(hn, w4, w4, conv_w, conv_w, conv_b, conv_b)


def _shift_up_before(d, next8, n):
    rows = d.shape[0]
    bottom = pltpu.roll(jnp.concatenate([d[rows - 8:], next8], axis=0), 16 - n, 0)[0:8]
    return jnp.concatenate([pltpu.roll(d, rows - n, 0)[:rows - 8], bottom], axis=0)


def conv_bwd_wgrad_up_call(hn, ug, uv, cg, cv, dx2, w_down, conv_w, S, tt=256):
    T, M = hn.shape
    nt = T // tt
    per_seq = S // tt

    def body(a_ref, ug_ref, uv_ref, cg_ref, cv_ref, dx_ref, wd_ref, wg_ref, wv_ref,
             dug_ref, duv_ref, dw_ref, dwg_ref, dwv_ref, dbg_ref, dbv_ref, ng_ref, nv_ref):
        step = pl.program_id(1)

        @pl.when(step == 0)
        def _():
            for r in (dw_ref, dwg_ref, dwv_ref, dbg_ref, dbv_ref):
                r[...] = jnp.zeros_like(r)

        @pl.when((nt - 1 - step) % per_seq == per_seq - 1)
        def _():
            ng_ref[...] = jnp.zeros_like(ng_ref)
            nv_ref[...] = jnp.zeros_like(nv_ref)

        da = _nt(dx_ref[...], wd_ref[...])
        dval = da * cg_ref[...].astype(F32)
        dgate = da * cv_ref[...].astype(F32)
        a = a_ref[...]
        for k, (u_ref, d, w, du_ref, dcw_ref, dcb_ref, next_ref) in enumerate((
                (ug_ref, dgate, _taps(wg_ref), dug_ref, dwg_ref, dbg_ref, ng_ref),
                (uv_ref, dval, _taps(wv_ref), duv_ref, dwv_ref, dbv_ref, nv_ref))):
            u_ = u_ref[...]
            next8 = next_ref[...]
            d1 = _shift_up_before(d, next8, 1)
            d2 = _shift_up_before(d, next8, 2)
            du = (w[2] * d + w[1] * d1 + w[0] * d2).astype(BF16)
            du_ref[...] = du
            dw_ref[k] += _tn(a, du)
            dcb_ref[...] += jnp.sum(d, axis=0, keepdims=True)
            dcw_ref[0:1, :] += jnp.sum(d2 * u_, axis=0, keepdims=True)
            dcw_ref[1:2, :] += jnp.sum(d1 * u_, axis=0, keepdims=True)
            dcw_ref[2:3, :] += jnp.sum(d * u_, axis=0, keepdims=True)
            next_ref[...] = d[0:8, :]

    def tok(width, by_col):
        return pl.BlockSpec((tt, width), (lambda j, i: (nt - 1 - i, j)) if by_col else (lambda j, i: (nt - 1 - i, 0)))

    def cols(rows, off):
        return pl.BlockSpec((rows, UP_COLS), lambda j, i: (0, off + j))

    outs = pl.pallas_call(
        body, name="conv_bwd_wgrad_up", grid=(2, nt),
        in_specs=[tok(M, False), tok(UP_COLS, True), tok(UP_COLS, True), tok(UP_COLS, True), tok(UP_COLS, True),
                  tok(dx2.shape[1], False), pl.BlockSpec((UP_COLS, w_down.shape[1]), lambda j, i: (j, 0)),
                  cols(3, 0), cols(3, 2)],
        out_specs=[tok(UP_COLS, True), tok(UP_COLS, True),
                   pl.BlockSpec((2, None, M, UP_COLS), lambda j, i: (0, j, 0, 0)),
                   cols(3, 0), cols(3, 0), cols(1, 0), cols(1, 0)],
        out_shape=[jax.ShapeDtypeStruct((T, D_FF), BF16), jax.ShapeDtypeStruct((T, D_FF), BF16),
                   jax.ShapeDtypeStruct((2, 2, M, UP_COLS), F32),
                   jax.ShapeDtypeStruct((3, D_FF), F32), jax.ShapeDtypeStruct((3, D_FF), F32),
                   jax.ShapeDtypeStruct((1, D_FF), F32), jax.ShapeDtypeStruct((1, D_FF), F32)],
        scratch_shapes=[pltpu.VMEM((8, UP_COLS), F32)] * 2,
        compiler_params=_params(("arbitrary", "arbitrary")),
    )(hn, ug, uv, cg, cv, dx2, w_down, conv_w, conv_w)
    return (outs[0], outs[1], outs[2].reshape(N_CHIPS, M, UP_COLS)) + tuple(outs[3:])


def ffn_up_bwd_call(du_g, du_v, w4, x1, g_ffn, dx2, ex, tm=512):
    T = du_g.shape[0]
    N = w4.shape[1]

    def body(g_ref, v_ref, w_ref, x1_ref, gf_ref, dx2_ref, dx1_ref, dg_ref):
        @pl.when(pl.program_id(0) == 0)
        def _():
            dg_ref[...] = jnp.zeros_like(dg_ref)

        dhn = _nt(g_ref[:, :UP_COLS], w_ref[0]) + _nt(g_ref[:, UP_COLS:], w_ref[1])
        dhn = dhn + _nt(v_ref[:, :UP_COLS], w_ref[2]) + _nt(v_ref[:, UP_COLS:], w_ref[3])
        dx, dg = _rms_bwd(x1_ref[...], gf_ref[...], dhn, N)
        dx1_ref[...] = dx2_ref[...] + dx
        dg_ref[...] += dg

    outs, moved = _call(
        body, ex, name="ffn_up_bwd", grid=(T // tm,),
        in_specs=[_rows(tm, D_FF), _rows(tm, D_FF), _whole(w4, pipeline_mode=pl.Buffered(1)), _rows(tm, N),
                  _whole(g_ffn), _rows(tm, N)],
        out_specs=[_rows(tm, N), _whole(g_ffn)],
        out_shape=[jax.ShapeDtypeStruct((T, N), F32), jax.ShapeDtypeStruct((1, N), F32)],
        args=(du_g, du_v, w4, x1, g_ffn, dx2))
    return outs[0], outs[1], moved


def wgrad_up_call(hn, du_g, du_v, tt=512):
    T, M = hn.shape

    def body(a_ref, g_ref, v_ref, o_ref):
        @pl.when(pl.program_id(1) == 0)
        def _():
            o_ref[...] = jnp.zeros_like(o_ref)

        a = a_ref[...]
        o_ref[0] += _tn(a, g_ref[...])
        o_ref[1] += _tn(a, v_ref[...])

    col = pl.BlockSpec((tt, UP_COLS), lambda j, t: (t, j))
    out = pl.pallas_call(
        body, name="wgrad_up", grid=(2, T // tt),
        in_specs=[pl.BlockSpec((tt, M), lambda j, t: (t, 0)), col, col],
        out_specs=pl.BlockSpec((2, None, M, UP_COLS), lambda j, t: (0, j, 0, 0)),
        out_shape=jax.ShapeDtypeStruct((2, 2, M, UP_COLS), F32),
        compiler_params=_params(("parallel", "arbitrary")),
    )(hn, du_g, du_v)
    return out.reshape(N_CHIPS, M, UP_COLS)


IN_PIECES = ((0, ATT_W), (ATT_W, ATT_W), (2 * ATT_W, ATT_W), (3 * ATT_W, Q_RANK), (3 * ATT_W + Q_RANK, Q_RANK))


def proj_in_bwd_call(pieces, w_in_t, ex, tm=512):
    T = pieces[0].shape[0]
    N = w_in_t.shape[1]
    n = len(pieces)

    def body(*refs):
        o_ref = refs[2 * n]
        acc = _nn(refs[0][...].astype(BF16), refs[n][...])
        for i in range(1, n):
            acc = acc + _nn(refs[i][...].astype(BF16), refs[n + i][...])
        o_ref[...] = acc

    outs, moved = _call(
        body, ex, name="proj_in_bwd", grid=(T // tm,),
        in_specs=[pl.BlockSpec((tm, w), lambda i: (i, 0)) for _, w in IN_PIECES]
        + [pl.BlockSpec((w, N), functools.partial(lambda c, i: (c, 0), off // w)) for off, w in IN_PIECES],
        out_specs=[pl.BlockSpec((tm, N), lambda i: (i, 0))],
        out_shape=[jax.ShapeDtypeStruct((T, N), F32)], args=tuple(pieces) + (w_in_t,) * n)
    return outs[0], moved


def wgrad_in_call(h, pieces, tt=512):
    T, M = h.shape
    n = len(pieces)

    def body(*refs):
        a_ref, o_ref = refs[0], refs[n + 1]

        @pl.when(pl.program_id(0) == 0)
        def _():
            o_ref[...] = jnp.zeros_like(o_ref)

        a = a_ref[...]
        for i, (off, w) in enumerate(IN_PIECES):
            o_ref[off:off + w, :] += _tn(refs[1 + i][...].astype(BF16), a)

    return pl.pallas_call(
        body, name="wgrad_in", grid=(T // tt,),
        in_specs=[pl.BlockSpec((tt, M), lambda t: (t, 0))] + [pl.BlockSpec((tt, w), lambda t: (t, 0)) for _, w in IN_PIECES],
        out_specs=pl.BlockSpec((IN_COLS_PAD, M), lambda t: (0, 0)),
        out_shape=jax.ShapeDtypeStruct((IN_COLS_PAD, M), F32),
        compiler_params=_params(("arbitrary",)),
    )(h, *pieces)


def rmsnorm_fwd_call(name, x, g, tm=512, ex=None):
    T, d = x.shape

    def body(x_ref, g_ref, o_ref):
        x = x_ref[...]
        o_ref[...] = ((x * _rms_r(x, d)) * g_ref[...]).astype(BF16)

    row = pl.BlockSpec((tm, d), lambda i: (i, 0))
    outs, moved = _call(body, ex, name=name, grid=(T // tm,), in_specs=[row, pl.BlockSpec((1, d), lambda i: (0, 0))],
                        out_specs=[row], out_shape=[jax.ShapeDtypeStruct((T, d), BF16)], args=(x, g))
    return outs[0] if ex is None else (outs[0], moved)


def rmsnorm_bwd_call(name, x, g, dy, res, tm=512, ex=None):
    T, d = x.shape

    def body(x_ref, g_ref, dy_ref, r_ref, dx_ref, dg_ref):
        @pl.when(pl.program_id(0) == 0)
        def _():
            dg_ref[...] = jnp.zeros_like(dg_ref)

        dx, dg = _rms_bwd(x_ref[...], g_ref[...], dy_ref[...], d)
        dx_ref[...] = r_ref[...] + dx
        dg_ref[...] += dg

    row = pl.BlockSpec((tm, d), lambda i: (i, 0))
    vec = pl.BlockSpec((1, d), lambda i: (0, 0))
    outs, moved = _call(body, ex, name=name, grid=(T // tm,), in_specs=[row, vec, row, row], out_specs=[row, vec],
                        out_shape=[jax.ShapeDtypeStruct((T, d), F32), jax.ShapeDtypeStruct((1, d), F32)],
                        args=(x, g, dy, res))
    return tuple(outs) if ex is None else tuple(outs) + (moved,)


def outnorm_fwd_call(o_sb, o_mla, g_sb, g_mla, tm=512):
    T = o_sb.shape[0]

    def body(a_ref, b_ref, ga_ref, gb_ref, o_ref):
        a = a_ref[...]
        b = b_ref[...]
        ya = (a * _rms_r(a, ATT_W)) * ga_ref[...]
        yb = (b * _rms_r(b, ATT_W)) * gb_ref[...]
        o_ref[...] = jnp.concatenate([ya, yb], axis=1).astype(BF16)

    row = pl.BlockSpec((tm, ATT_W), lambda i: (i, 0))
    vec = pl.BlockSpec((1, ATT_W), lambda i: (0, 0))
    return pl.pallas_call(
        body, name="outnorm_fwd", grid=(T // tm,), in_specs=[row, row, vec, vec],
        out_specs=pl.BlockSpec((tm, 2 * ATT_W), lambda i: (i, 0)),
        out_shape=jax.ShapeDtypeStruct((T, 2 * ATT_W), BF16),
        compiler_params=_params(("parallel",)),
    )(o_sb, o_mla, g_sb, g_mla)


def outnorm_bwd_call(o_sb, o_mla, g_sb, g_mla, do_cat, tm=512, ex=None):
    T = o_sb.shape[0]

    def body(a_ref, b_ref, ga_ref, gb_ref, d_ref, da_ref, db_ref, dga_ref, dgb_ref):
        @pl.when(pl.program_id(0) == 0)
        def _():
            dga_ref[...] = jnp.zeros_like(dga_ref)
            dgb_ref[...] = jnp.zeros_like(dgb_ref)

        d = d_ref[...]
        da, dga = _rms_bwd(a_ref[...], ga_ref[...], d[:, :ATT_W], ATT_W)
        db, dgb = _rms_bwd(b_ref[...], gb_ref[...], d[:, ATT_W:], ATT_W)
        da_ref[...] = da
        db_ref[...] = db
        dga_ref[...] += dga
        dgb_ref[...] += dgb

    row = pl.BlockSpec((tm, ATT_W), lambda i: (i, 0))
    vec = pl.BlockSpec((1, ATT_W), lambda i: (0, 0))
    outs, moved = _call(
        body, ex, name="outnorm_bwd", grid=(T // tm,),
        in_specs=[row, row, vec, vec, pl.BlockSpec((tm, 2 * ATT_W), lambda i: (i, 0))],
        out_specs=[row, row, vec, vec],
        out_shape=[jax.ShapeDtypeStruct((T, ATT_W), F32), jax.ShapeDtypeStruct((T, ATT_W), F32),
                   jax.ShapeDtypeStruct((1, ATT_W), F32), jax.ShapeDtypeStruct((1, ATT_W), F32)],
        args=(o_sb, o_mla, g_sb, g_mla, do_cat))
    return tuple(outs) if ex is None else tuple(outs) + (moved,)


def final_loss_call(x2, g, target, tm=512):
    T, d = x2.shape

    def body(x_ref, g_ref, t_ref, dx_ref, dxb_ref, loss_ref, dg_ref):
        @pl.when(pl.program_id(0) == 0)
        def _():
            loss_ref[...] = jnp.zeros_like(loss_ref)
            dg_ref[...] = jnp.zeros_like(dg_ref)

        x = x_ref[...]
        g = g_ref[...]
        y = (x * _rms_r(x, d)) * g
        err = y - t_ref[...]
        loss_ref[...] += jnp.sum(jnp.sum(err * err, axis=1, keepdims=True), axis=0, keepdims=True) * (0.5 / d)
        dx, dg = _rms_bwd(x, g, err * (1.0 / d), d)
        dx_ref[...] = dx
        dxb_ref[...] = dx.astype(BF16)
        dg_ref[...] += dg

    row = pl.BlockSpec((tm, d), lambda i: (i, 0))
    vec = pl.BlockSpec((1, d), lambda i: (0, 0))
    return pl.pallas_call(
        body, name="final_loss", grid=(T // tm,), in_specs=[row, vec, row],
        out_specs=[row, row, pl.BlockSpec((1, LANES), lambda i: (0, 0)), vec],
        out_shape=[jax.ShapeDtypeStruct((T, d), F32), jax.ShapeDtypeStruct((T, d), BF16),
                   jax.ShapeDtypeStruct((1, LANES), F32), jax.ShapeDtypeStruct((1, d), F32)],
        compiler_params=_params(("arbitrary",)),
    )(x2, g, target)


def rope_tab_call(pos, inv_freq, tm=512):
    T = pos.shape[0]

    def body(p_ref, f_ref, c_ref, s_ref):
        ang = p_ref[...].astype(F32) * f_ref[...]
        lane = lax.broadcasted_iota(jnp.int32, ang.shape, 1)
        sn = jnp.sin(ang)
        c_ref[...] = jnp.cos(ang)
        s_ref[...] = jnp.where((lane & 31) < 16, -sn, sn)

    row = pl.BlockSpec((tm, LANES), lambda i: (i, 0))
    return pl.pallas_call(
        body, name="rope_tab", grid=(T // tm,),
        in_specs=[pl.BlockSpec((tm, 1), lambda i: (i, 0)), pl.BlockSpec((1, LANES), lambda i: (0, 0))],
        out_specs=[row, row],
        out_shape=[jax.ShapeDtypeStruct((T, LANES), F32)] * 2,
        compiler_params=_params(("parallel",)),
    )(pos, inv_freq)


def mla_prep_fwd_call(p, cos, sin, g_cq, g_ckv, w_uq_p, w_ukv_p, tm=512):
    T = p.shape[0]

    def body(cq_ref, ckvr_ref, c_ref, s_ref, gq_ref, gkv_ref, wq_ref, wkv_ref,
             qn_ref, qr_ref, kn_ref, vm_ref, krt_ref, cqn_ref, ckvn_ref):
        c = c_ref[...]
        s = s_ref[...]
        cq = cq_ref[...]
        cqn = ((cq * _rms_r(cq, Q_RANK)) * gq_ref[...]).astype(BF16)
        cqn_ref[...] = cqn
        q = _nn(cqn, wq_ref[...])
        qn_ref[...] = q[:, :ATT_W].astype(BF16)
        for g in range(ROPE_W // LANES):
            qr = q[:, ATT_W + g * LANES:ATT_W + (g + 1) * LANES]
            qr_ref[:, g * LANES:(g + 1) * LANES] = (qr * c + _rot(qr) * s).astype(BF16)
        ckvr = ckvr_ref[...]
        ckv = ckvr[:, :KV_RANK]
        ckvn = ((ckv * _rms_r(ckv, KV_RANK)) * gkv_ref[...]).astype(BF16)
        ckvn_ref[...] = ckvn
        kv = _nn(ckvn, wkv_ref[...])
        kn_ref[...] = kv[:, :ATT_W].astype(BF16)
        vm_ref[...] = kv[:, ATT_W:].astype(BF16)
        kr = _fold4(ckvr[:, KV_RANK:])
        krt_ref[...] = (kr * c + _rot(kr) * s).astype(BF16)

    def row(w, j=0):
        return pl.BlockSpec((tm, w), lambda i: (i, j))

    def full(a):
        return pl.BlockSpec(a.shape, lambda i: (0, 0))

    return pl.pallas_call(
        body, name="mla_prep_fwd", grid=(T // tm,),
        in_specs=[row(Q_RANK, 4), row(Q_RANK, 5), row(LANES), row(LANES), full(g_cq), full(g_ckv),
                  full(w_uq_p), full(w_ukv_p)],
        out_specs=[row(ATT_W), row(ROPE_W), row(ATT_W), row(ATT_W), row(LANES), row(Q_RANK), row(KV_RANK)],
        out_shape=[jax.ShapeDtypeStruct((T, w), BF16) for w in (ATT_W, ROPE_W, ATT_W, ATT_W, LANES, Q_RANK, KV_RANK)],
        compiler_params=_params(("parallel",)),
    )(p, p, cos, sin, g_cq, g_ckv, w_uq_p, w_ukv_p)


def mla_prep_bwd_call(p, cos, sin, g_cq, g_ckv, w_uq_p, w_ukv_p, dqn, dqr4, dkn, dvm, dkrt4, tm=512):
    T = p.shape[0]

    def body(cq_ref, ckvr_ref, c_ref, s_ref, gq_ref, gkv_ref, wq_ref, wkv_ref,
             dqn_ref, dqr4_ref, dkn_ref, dvm_ref, dkrt4_ref,
             dcq_ref, dckvr_ref, dq_ref, dkv_ref, dgq_ref, dgkv_ref):
        @pl.when(pl.program_id(0) == 0)
        def _():
            dgq_ref[...] = jnp.zeros_like(dgq_ref)
            dgkv_ref[...] = jnp.zeros_like(dgkv_ref)

        c = c_ref[...]
        s = s_ref[...]
        d4 = dqr4_ref[...]
        dqr = [d4[:, :128] + d4[:, 128:256], d4[:, 256:384] + d4[:, 384:]]
        dqr = [t * c + _rot(t * s) for t in dqr]
        dq = jnp.concatenate([dqn_ref[...]] + dqr, axis=1).astype(BF16)
        dq_ref[...] = dq
        dcq, dgq = _rms_bwd(cq_ref[...], gq_ref[...], _nt(dq, wq_ref[...]), Q_RANK)
        dcq_ref[...] = dcq
        dgq_ref[...] += dgq
        dkv = jnp.concatenate([dkn_ref[...], dvm_ref[...]], axis=1).astype(BF16)
        dkv_ref[...] = dkv
        ckvr = ckvr_ref[...]
        dckv, dgkv = _rms_bwd(ckvr[:, :KV_RANK], gkv_ref[...], _nt(dkv, wkv_ref[...]), KV_RANK)
        dgkv_ref[...] += dgkv
        k4 = dkrt4_ref[...]
        dkr = _fold4(k4[:, :128] + k4[:, 128:256] + k4[:, 256:384] + k4[:, 384:])
        dkr = dkr * c + _rot(dkr * s)
        lane = lax.broadcasted_iota(jnp.int32, dkr.shape, 1)
        dckvr_ref[...] = jnp.concatenate([dckv, jnp.where(lane < ROPE_DIM, dkr, 0.0)], axis=1)

    def row(w, j=0):
        return pl.BlockSpec((tm, w), lambda i: (i, j))

    def full(a):
        return pl.BlockSpec(a.shape, lambda i: (0, 0))

    return pl.pallas_call(
        body, name="mla_prep_bwd", grid=(T // tm,),
        in_specs=[row(Q_RANK, 4), row(Q_RANK, 5), row(LANES), row(LANES), full(g_cq), full(g_ckv),
                  full(w_uq_p), full(w_ukv_p), row(ATT_W), row(ATT_W), row(ATT_W), row(ATT_W), row(ATT_W)],
        out_specs=[row(Q_RANK), row(Q_RANK), row(ATT_W + ROPE_W), row(2 * ATT_W),
                   pl.BlockSpec((1, Q_RANK), lambda i: (0, 0)), pl.BlockSpec((1, KV_RANK), lambda i: (0, 0))],
        out_shape=[jax.ShapeDtypeStruct((T, Q_RANK), F32), jax.ShapeDtypeStruct((T, Q_RANK), F32),
                   jax.ShapeDtypeStruct((T, ATT_W + ROPE_W), BF16), jax.ShapeDtypeStruct((T, 2 * ATT_W), BF16),
                   jax.ShapeDtypeStruct((1, Q_RANK), F32), jax.ShapeDtypeStruct((1, KV_RANK), F32)],
        compiler_params=_params(("arbitrary",)),
    )(p, p, cos, sin, g_cq, g_ckv, w_uq_p, w_ukv_p, dqn, dqr4, dkn, dvm, dkrt4)


def _iota2(shape, axis):
    return lax.broadcasted_iota(jnp.int32, shape, axis)


def _head_masks():
    lane = _iota2((1, LANES), 1)
    return lane < HEAD_DIM, lane >= HEAD_DIM


def _pair(x, masks, dtype=BF16):
    return [jnp.where(m, x, 0.0).astype(dtype) for m in masks]


def _log_gates(z):
    keep = jnp.maximum(z, 0.0) + jnp.log2(1.0 + jnp.exp2(-jnp.abs(z)))
    return z - keep, keep


def _last_row(x):
    return _row_of(x[x.shape[0] - 8:, :], 7)


def _lane_selector(group):
    return jnp.where(_iota2((16, LANES), 1) // group == _iota2((16, LANES), 0), 1.0, 0.0).astype(BF16)


def _rows8(sel_t, x):
    hi = x.astype(BF16)
    r1 = x - hi.astype(F32)
    mid = r1.astype(BF16)
    lo = (r1 - mid.astype(F32)).astype(BF16)
    return _nt(sel_t, hi) + _nt(sel_t, mid) + _nt(sel_t, lo)


def _row_of(x8, j):
    return jnp.sum(jnp.where(_iota2(x8.shape, 0) == j, x8, 0.0), axis=0, keepdims=True)


def sb_fwd_call(p, B, S, ex=None):
    T = B * S
    TQ, TK = ATT_TQ, ATT_TK
    nq = S // TQ

    def body(q_ref, k_ref, v_ref, o_ref, lt_ref):
        qi = pl.program_id(2)
        masks = _head_masks()
        qm = [_pair(q_ref[:, sl] * SB_SCALE2, masks) for sl in PAIR_LANES]
        row = _iota2((TQ, TK), 0)
        col = _iota2((TQ, TK), 1)
        tri = jnp.where(row > col, 1.0, 0.0).astype(BF16)
        tri2 = jnp.concatenate([tri, tri], axis=0)
        vis = col < row
        o_ref[...] = jnp.zeros_like(o_ref)

        def group(k0, pairs, carry, diag):
            heads = [(pp, j) for pp in pairs for j in range(2)]
            n = range(len(heads))
            k = {pp: k_ref[pl.ds(k0, TK), PAIR_LANES[pp]].astype(BF16) for pp in pairs}
            vm = {pp: _pair(v_ref[pl.ds(k0, TK), PAIR_LANES[pp]], masks) for pp in pairs}
            z = [_nt(qm[pp][j], k[pp]) for pp, j in heads]
            if diag:
                z = [jnp.where(vis, x, NEG_BIG) for x in z]
            gates = [_log_gates(x) for x in z]
            lb = [g[0] for g in gates]
            keep = [g[1] for g in gates]
            tail = [_nn(jnp.concatenate(_split2(keep[h]), axis=1), tri2) + carry[h] for h in n]
            a = [jnp.exp2(lb[h] - tail[h]) for h in n]
            ab = [x.astype(BF16) for x in a]
            for i, pp in enumerate(pairs):
                o_ref[:, PAIR_LANES[pp]] += _nn(ab[2 * i], vm[pp][0]) + _nn(ab[2 * i + 1], vm[pp][1])
            return [tail[h][:, 0:1] + keep[h][:, 0:1] for h in n]

        def step(kb, carry, diag):
            k0 = pl.multiple_of(kb * TK, TK)
            out = []
            for g in range(0, ATT_PAIRS, SB_FWD_GROUP):
                out += group(k0, list(range(g, g + SB_FWD_GROUP)), carry[2 * g:2 * (g + SB_FWD_GROUP)], diag)
            return tuple(out)

        zero = jnp.zeros((TQ, 1), F32)
        carry = step(qi, (zero,) * (2 * ATT_PAIRS), True)
        carry = lax.fori_loop(0, qi, lambda i, c: step(qi - 1 - i, c, False), carry)
        lane = _iota2((TQ, LANES), 1)
        for pp, sl in enumerate(PAIR_LANES):
            lt_ref[:, sl] = jnp.where(lane == 0, carry[2 * pp], jnp.where(lane == 1, carry[2 * pp + 1], 0.0))

    W = ATT_PAIRS * LANES
    qspec = pl.BlockSpec((TQ, W), lambda b, h, i: (b * nq + i, h))
    outs, moved = _call(
        body, ex, name="sb_fwd", grid=(B, HEADS // 2 // ATT_PAIRS, nq),
        in_specs=[qspec,
                  pl.BlockSpec((S, W), lambda b, h, i: (b, ATT_W // W + h)),
                  pl.BlockSpec((S, W), lambda b, h, i: (b, 2 * ATT_W // W + h))],
        out_specs=[qspec, qspec],
        out_shape=[jax.ShapeDtypeStruct((T, ATT_W), F32)] * 2, args=(p, p, p))
    return tuple(outs) if ex is None else tuple(outs) + (moved,)


def sb_bwd_call(p, lt, do, B, S, ex=None):
    T = B * S
    TQ, TK = ATT_TQ, ATT_TK
    nq = S // TQ

    def body(q_ref, k_ref, v_ref, lt_ref, do_ref, dq_ref, dk_ref, dv_ref):
        qi = pl.program_id(2)

        @pl.when(qi == 0)
        def _():
            dk_ref[...] = jnp.zeros_like(dk_ref)
            dv_ref[...] = jnp.zeros_like(dv_ref)

        masks = _head_masks()
        qm = [_pair(q_ref[:, sl] * SB_SCALE2, masks) for sl in PAIR_LANES]
        dom = [_pair(do_ref[:, sl], masks) for sl in PAIR_LANES]
        start = []
        for sl in PAIR_LANES:
            l8 = _rows8(_lane_selector(1), lt_ref[:, sl])
            start += [-_row_of(l8, 0), jnp.zeros((1, TQ), F32), -_row_of(l8, 1), jnp.zeros((1, TQ), F32)]
        row = _iota2((TK, TQ), 0)
        col = _iota2((TK, TQ), 1)
        incl = jnp.where(col <= row, 1.0, 0.0).astype(BF16)
        incl2 = jnp.concatenate([incl, incl], axis=1)
        excl = jnp.where(col < row, 1.0, 0.0).astype(BF16)
        vis = row < col
        dq_ref[...] = jnp.zeros_like(dq_ref)

        def group(k0, pairs, carry, diag):
            heads = [(pp, j) for pp in pairs for j in range(2)]
            n = range(len(heads))
            kf = {pp: k_ref[pl.ds(k0, TK), PAIR_LANES[pp]] for pp in pairs}
            km = {pp: _pair(kf[pp], masks) for pp in pairs}
            v = {pp: v_ref[pl.ds(k0, TK), PAIR_LANES[pp]].astype(BF16) for pp in pairs}
            z = [_nt(kf[pp].astype(BF16), qm[pp][j]) for pp, j in heads]
            da = [_nt(v[pp], dom[pp][j]) for pp, j in heads]
            if diag:
                z = [jnp.where(vis, x, NEG_BIG) for x in z]
            gates = [_log_gates(x) for x in z]
            lb = [g[0] for g in gates]
            keep = [g[1] for g in gates]
            left = [_nn(incl2, jnp.concatenate(_split2(keep[h]), axis=0)) + carry[2 * h] for h in n]
            a = [jnp.exp2(lb[h] + left[h]) for h in n]
            e = [a[h] * da[h] for h in n]
            before = [_nn(excl, e[h].astype(BF16)) + carry[2 * h + 1] for h in n]
            dz = [e[h] - jnp.exp2(lb[h]) * (e[h] + before[h]) for h in n]
            dzb = [x.astype(BF16) for x in dz]
            ab = [x.astype(BF16) for x in a]
            out = []
            for h in n:
                out += [_last_row(left[h]), _last_row(before[h]) + _last_row(e[h])]
            for i, pp in enumerate(pairs):
                sl = PAIR_LANES[pp]
                dk_ref[pl.ds(k0, TK), sl] += _nn(dzb[2 * i], qm[pp][0]) + _nn(dzb[2 * i + 1], qm[pp][1])
                dv_ref[pl.ds(k0, TK), sl] += _nn(ab[2 * i], dom[pp][0]) + _nn(ab[2 * i + 1], dom[pp][1])
                dq_ref[:, sl] += _tn(dzb[2 * i], km[pp][0]) + _tn(dzb[2 * i + 1], km[pp][1])
            return out

        def step(kb, carry, diag):
            k0 = pl.multiple_of(kb * TK, TK)
            out = []
            for g in range(0, ATT_PAIRS, SB_BWD_GROUP):
                out += group(k0, list(range(g, g + SB_BWD_GROUP)), carry[4 * g:4 * (g + SB_BWD_GROUP)], diag)
            return tuple(out)

        carry = lax.fori_loop(0, qi, lambda i, c: step(i, c, False), tuple(start))
        step(qi, carry, True)
        dq_ref[...] *= SB_SCALE

        @pl.when(qi == nq - 1)
        def _():
            dk_ref[...] *= LN2

    W = ATT_PAIRS * LANES
    qspec = pl.BlockSpec((TQ, W), lambda b, h, i: (b * nq + i, h))
    sspec = pl.BlockSpec((S, W), lambda b, h, i: (b, h))
    outs, moved = _call(
        body, ex, name="sb_bwd", grid=(B, HEADS // 2 // ATT_PAIRS, nq),
        in_specs=[qspec,
                  pl.BlockSpec((S, W), lambda b, h, i: (b, ATT_W // W + h)),
                  pl.BlockSpec((S, W), lambda b, h, i: (b, 2 * ATT_W // W + h)),
                  qspec, qspec],
        out_specs=[qspec, sspec, sspec],
        out_shape=[jax.ShapeDtypeStruct((T, ATT_W), F32)] * 3, args=(p, p, p, lt, do))
    return tuple(outs) if ex is None else tuple(outs) + (moved,)


ALL_PAIRS = [slice(i * LANES, (i + 1) * LANES) for i in range(HEADS // 2)]


def _rope_masks(hp):
    grp = _iota2((1, LANES), 1) // ROPE_DIM
    return [grp == ((2 * hp + j) % 4) for j in range(2)]


def _mla_queries(qn_ref, qr_ref, masks):
    out = []
    for pp, sl in enumerate(ALL_PAIRS):
        qnv = qn_ref[:, sl]
        qrv = qr_ref[:, ALL_PAIRS[pp // 2]]
        rmasks = _rope_masks(pp)
        out.append([jnp.concatenate([jnp.where(masks[j], qnv, 0), jnp.where(rmasks[j], qrv, 0)], axis=1).astype(BF16)
                    for j in range(2)])
    return out


def mla_fwd_call(qn, qr, kn, krt, vm, B, S):
    T = B * S
    TQ, TK = ATT_TQ, ATT_TK
    nq = S // TQ

    def body(qn_ref, qr_ref, kn_ref, kr_ref, v_ref, o_ref, lse_ref):
        qi = pl.program_id(1)
        masks = _head_masks()
        qcat = _mla_queries(qn_ref, qr_ref, masks)
        row = _iota2((TQ, TK), 0)
        col = _iota2((TQ, TK), 1)
        vis = col <= row
        o_ref[...] = jnp.zeros_like(o_ref)

        def group(k0, pairs, carry, diag):
            heads = [(pp, j) for pp in pairs for j in range(2)]
            n = range(len(heads))
            krv = kr_ref[pl.ds(k0, TK), :]
            kcat = {pp: jnp.concatenate([kn_ref[pl.ds(k0, TK), ALL_PAIRS[pp]], krv], axis=1) for pp in pairs}
            vmk = {pp: _pair(v_ref[pl.ds(k0, TK), ALL_PAIRS[pp]], masks) for pp in pairs}
            s = [_nt(qcat[pp][j], kcat[pp]) * MLA_SCALE2 for pp, j in heads]
            if diag:
                s = [jnp.where(vis, x, NEG_BIG) for x in s]
            m_new = [jnp.maximum(carry[2 * h], jnp.max(s[h], axis=1, keepdims=True)) for h in n]
            alpha = [jnp.exp2(carry[2 * h] - m_new[h]) for h in n]
            pexp = [jnp.exp2(s[h] - m_new[h]) for h in n]
            out = []
            for h in n:
                out += [m_new[h], alpha[h] * carry[2 * h + 1] + jnp.sum(pexp[h], axis=1, keepdims=True)]
            pb = [x.astype(BF16) for x in pexp]
            for i, pp in enumerate(pairs):
                sl = ALL_PAIRS[pp]
                scale = jnp.where(masks[0], alpha[2 * i], alpha[2 * i + 1])
                o_ref[:, sl] = o_ref[:, sl] * scale + (_nn(pb[2 * i], vmk[pp][0]) + _nn(pb[2 * i + 1], vmk[pp][1]))
            return out

        def step(kb, carry, diag):
            k0 = pl.multiple_of(kb * TK, TK)
            out = []
            for g in range(0, len(ALL_PAIRS), MLA_GROUP):
                out += group(k0, list(range(g, g + MLA_GROUP)), carry[4 * g:4 * (g + MLA_GROUP)], diag)
            return tuple(out)

        neg = jnp.full((TQ, 1), NEG_BIG, F32)
        zero = jnp.zeros((TQ, 1), F32)
        carry = step(qi, (neg, zero) * (2 * len(ALL_PAIRS)), True)
        carry = lax.fori_loop(0, qi, lambda i, c: step(qi - 1 - i, c, False), carry)
        lane = _iota2((TQ, LANES), 1)
        for pp, sl in enumerate(ALL_PAIRS):
            m0, l0, m1, l1 = carry[4 * pp:4 * pp + 4]
            o_ref[:, sl] = o_ref[:, sl] * jnp.where(masks[0], 1.0 / l0, 1.0 / l1)
            lse_ref[:, sl] = jnp.where(lane == 0, m0 * LN2 + jnp.log(l0), jnp.where(lane == 1, m1 * LN2 + jnp.log(l1), 0.0))

    def rows(w):
        return pl.BlockSpec((TQ, w), lambda b, i: (b * nq + i, 0))

    def seq(w):
        return pl.BlockSpec((S, w), lambda b, i: (b, 0))

    return pl.pallas_call(
        body, name="mla_fwd", grid=(B, nq),
        in_specs=[rows(ATT_W), rows(ROPE_W), seq(ATT_W), seq(LANES), seq(ATT_W)],
        out_specs=[rows(ATT_W), rows(ATT_W)],
        out_shape=[jax.ShapeDtypeStruct((T, ATT_W), F32)] * 2,
        compiler_params=_params(("arbitrary", "arbitrary")),
    )(qn, qr, kn, krt, vm)


def mla_bwd_call(qn, qr, kn, krt, vm, o, lse, do, B, S, ex=None):
    T = B * S
    TQ, TK = ATT_TQ, ATT_TK
    nq = S // TQ

    def body(qn_ref, qr_ref, kn_ref, kr_ref, v_ref, o_ref, lse_ref, do_ref,
             dqn_ref, dqr_ref, dkn_ref, dv_ref, dkr_ref):
        qi = pl.program_id(1)

        @pl.when(qi == 0)
        def _():
            dkn_ref[...] = jnp.zeros_like(dkn_ref)
            dv_ref[...] = jnp.zeros_like(dv_ref)
            dkr_ref[...] = jnp.zeros_like(dkr_ref)

        masks = _head_masks()
        qcat = _mla_queries(qn_ref, qr_ref, masks)
        dom, dsum, lse = [], [], []
        for sl in ALL_PAIRS:
            do = do_ref[:, sl]
            dom.append(_pair(do, masks))
            d8 = _rows8(_lane_selector(HEAD_DIM), do * o_ref[:, sl])
            l8 = _rows8(_lane_selector(1), lse_ref[:, sl])
            dsum.append([_row_of(d8, j) for j in range(2)])
            lse.append([_row_of(l8, j) * LOG2E for j in range(2)])
        row = _iota2((TK, TQ), 0)
        col = _iota2((TK, TQ), 1)
        vis = row <= col
        dqn_ref[...] = jnp.zeros_like(dqn_ref)
        dqr_ref[...] = jnp.zeros_like(dqr_ref)

        def group(k0, pairs, diag):
            heads = [(pp, j) for pp in pairs for j in range(2)]
            n = range(len(heads))
            krv = kr_ref[pl.ds(k0, TK), :]
            knv = {pp: kn_ref[pl.ds(k0, TK), ALL_PAIRS[pp]] for pp in pairs}
            kcat = {pp: jnp.concatenate([knv[pp], krv], axis=1) for pp in pairs}
            v = {pp: v_ref[pl.ds(k0, TK), ALL_PAIRS[pp]] for pp in pairs}
            s = [_nt(kcat[pp], qcat[pp][j]) * MLA_SCALE2 for pp, j in heads]
            dp_ = [_nt(v[pp], dom[pp][j]) for pp, j in heads]
            pr = [jnp.exp2(s[h] - lse[pp][j]) for h, (pp, j) in enumerate(heads)]
            if diag:
                pr = [jnp.where(vis, x, 0.0) for x in pr]
            ds = [(pr[h] * (dp_[h] - dsum[pp][j]) * MLA_SCALE).astype(BF16) for h, (pp, j) in enumerate(heads)]
            pb = [x.astype(BF16) for x in pr]
            for i, pp in enumerate(pairs):
                sl = ALL_PAIRS[pp]
                rmasks = _rope_masks(pp)
                kcat_j = [jnp.concatenate([jnp.where(masks[j], knv[pp], 0), jnp.where(rmasks[j], krv, 0)],
                                          axis=1).astype(BF16) for j in range(2)]
                dv_ref[pl.ds(k0, TK), sl] += _nn(pb[2 * i], dom[pp][0]) + _nn(pb[2 * i + 1], dom[pp][1])
                dk = _nn(ds[2 * i], qcat[pp][0]) + _nn(ds[2 * i + 1], qcat[pp][1])
                dq = _tn(ds[2 * i], kcat_j[0]) + _tn(ds[2 * i + 1], kcat_j[1])
                dqn_ref[:, sl] += dq[:, :LANES]
                dqr_ref[:, sl] += dq[:, LANES:]
                dkn_ref[pl.ds(k0, TK), sl] += dk[:, :LANES]
                dkr_ref[pl.ds(k0, TK), sl] += dk[:, LANES:]

        def step(kb, diag):
            k0 = pl.multiple_of(kb * TK, TK)
            for g in range(0, len(ALL_PAIRS), MLA_GROUP):
                group(k0, list(range(g, g + MLA_GROUP)), diag)

        step(qi, True)

        def loop(i, c):
            step(qi - 1 - i, False)
            return c

        lax.fori_loop(0, qi, loop, 0)

    def rows(w):
        return pl.BlockSpec((TQ, w), lambda b, i: (b * nq + i, 0))

    def seq(w):
        return pl.BlockSpec((S, w), lambda b, i: (b, 0))

    outs, moved = _call(
        body, ex, name="mla_bwd", grid=(B, nq),
        in_specs=[rows(ATT_W), rows(ROPE_W), seq(ATT_W), seq(LANES), seq(ATT_W), rows(ATT_W), rows(ATT_W), rows(ATT_W)],
        out_specs=[rows(ATT_W), rows(ATT_W), seq(ATT_W), seq(ATT_W), seq(ATT_W)],
        out_shape=[jax.ShapeDtypeStruct((T, ATT_W), F32)] * 5, args=(qn, qr, kn, krt, vm, o, lse, do))
    return tuple(outs) if ex is None else tuple(outs) + (moved,)


CONV_TC = 256


def _shift_down(x, n):
    return jnp.where(_iota2(x.shape, 0) >= n, pltpu.roll(x, n, 0), 0.0)


def _shift_up(x, n):
    rows = x.shape[0]
    return jnp.where(_iota2(x.shape, 0) < rows - n, pltpu.roll(x, rows - n, 0), 0.0)


def _taps(w_ref):
    return [w_ref[k:k + 1, :] for k in range(3)]


def _conv3(u, w, b):
    return w[0] * _shift_down(u, 2) + w[1] * _shift_down(u, 1) + w[2] * u + b


def _ref_shift_down(ref, n):
    rows = ref.shape[0]
    return jnp.concatenate([_shift_down(ref[0:8, :], n), ref[8 - n:rows - n, :]], axis=0)


def _conv3_ref(u_ref, w, b):
    return w[0] * _ref_shift_down(u_ref, 2) + w[1] * _ref_shift_down(u_ref, 1) + w[2] * u_ref[...] + b


def conv_act_fwd_call(ug, uv, conv_w, conv_b, B, S):
    T = B * S
    nc = D_FF // CONV_TC

    def body(ug_ref, uv_ref, wg_ref, wv_ref, bg_ref, bv_ref, a_ref, cg_ref, cv_ref):
        gate = _conv3_ref(ug_ref, _taps(wg_ref), bg_ref[...])
        val = _conv3_ref(uv_ref, _taps(wv_ref), bv_ref[...])
        a_ref[...] = (gate * (1.0 / (1.0 + jnp.exp(-gate))) * val).astype(BF16)
        cg_ref[...] = gate.astype(BF16)
        cv_ref[...] = val.astype(BF16)

    def blk(rows, off):
        return pl.BlockSpec((rows, CONV_TC), lambda b, j: (b if rows == S else 0, off + j))

    return pl.pallas_call(
        body, name="conv_act_fwd", grid=(B, nc),
        in_specs=[blk(S, 0), blk(S, 0), blk(3, 0), blk(3, nc), blk(1, 0), blk(1, nc)],
        out_specs=[blk(S, 0)] * 3,
        out_shape=[jax.ShapeDtypeStruct((T, D_FF), BF16)] * 3,
        compiler_params=_params(("parallel", "parallel")),
    )(ug, uv, conv_w, conv_w, conv_b, conv_b)


def conv_act_bwd_call(ug, uv, cg, cv, dx2, w_down, conv_w, B, S):
    T = B * S
    nc = D_FF // CONV_TC

    def body(ug_ref, uv_ref, cg_ref, cv_ref, dx_ref, wd_ref, wg_ref, wv_ref,
             dug_ref, duv_ref, dwg_ref, dwv_ref, dbg_ref, dbv_ref):
        @pl.when(pl.program_id(1) == 0)
        def _():
            for r in (dwg_ref, dwv_ref, dbg_ref, dbv_ref):
                r[...] = jnp.zeros_like(r)

        gate = cg_ref[...].astype(F32)
        val = cv_ref[...].astype(F32)
        da = _nt(dx_ref[...], wd_ref[...])
        sig = 1.0 / (1.0 + jnp.exp(-gate))
        dval = da * (gate * sig)
        dgate = da * val * (sig * (1.0 + gate * (1.0 - sig)))
        for u_ref, d, w, du_ref, dw_ref, db_ref in ((ug_ref, dgate, _taps(wg_ref), dug_ref, dwg_ref, dbg_ref),
                                                   (uv_ref, dval, _taps(wv_ref), duv_ref, dwv_ref, dbv_ref)):
            u_ = u_ref[...]
            d1 = _shift_up(d, 1)
            d2 = _shift_up(d, 2)
            du_ref[...] = (w[2] * d + w[1] * d1 + w[0] * d2).astype(BF16)
            db_ref[...] += jnp.sum(d, axis=0, keepdims=True)
            dw_ref[0:1, :] += jnp.sum(d2 * u_, axis=0, keepdims=True)
            dw_ref[1:2, :] += jnp.sum(d1 * u_, axis=0, keepdims=True)
            dw_ref[2:3, :] += jnp.sum(d * u_, axis=0, keepdims=True)

    def blk(rows, off):
        return pl.BlockSpec((rows, CONV_TC), lambda j, b: (b if rows == S else 0, off + j))

    return pl.pallas_call(
        body, name="conv_act_bwd", grid=(nc, B),
        in_specs=[blk(S, 0), blk(S, 0), blk(S, 0), blk(S, 0), pl.BlockSpec((S, D_MODEL), lambda j, b: (b, 0)),
                  pl.BlockSpec((CONV_TC, D_MODEL), lambda j, b: (j, 0)), blk(3, 0), blk(3, nc)],
        out_specs=[blk(S, 0), blk(S, 0), blk(3, 0), blk(3, 0), blk(1, 0), blk(1, 0)],
        out_shape=[jax.ShapeDtypeStruct((T, D_FF), BF16), jax.ShapeDtypeStruct((T, D_FF), BF16),
                   jax.ShapeDtypeStruct((3, D_FF), F32), jax.ShapeDtypeStruct((3, D_FF), F32),
                   jax.ShapeDtypeStruct((1, D_FF), F32), jax.ShapeDtypeStruct((1, D_FF), F32)],
        compiler_params=_params(("parallel", "arbitrary")),
    )(ug, uv, cg, cv, dx2, w_down, conv_w, conv_w)


CHIP_MASKS = ((1, 0), (0, 1), (1, 1))


def _place():
    return lax.axis_index("x"), lax.axis_index("y"), lax.axis_index("c")


HALF_ALIGN = 32


def _any_specs(n):
    return [pl.BlockSpec(memory_space=pl.ANY)] * n


def _splits(shape):
    r, c = shape
    return "rows" if r % HALF_ALIGN == 0 else "cols" if c % (2 * LANES) == 0 else None


def _half(shape, half):
    r, c = shape
    how = _splits(shape)
    if how == "rows":
        return (pl.ds(pl.multiple_of(half * (r // 2), HALF_ALIGN // 2), r // 2), slice(None))
    if how == "cols":
        return (slice(None), pl.ds(pl.multiple_of(half * (c // 2), LANES), c // 2))
    return (slice(None), slice(None))


def _half_shape(shape):
    r, c = shape
    return {"rows": (r // 2, c), "cols": (r, c // 2)}[_splits(shape)]


def _remote(src, dst, send_sem, recv_sem, device):
    return pltpu.make_async_remote_copy(src_ref=src, dst_ref=dst, send_sem=send_sem, recv_sem=recv_sem,
                                        device_id=device, device_id_type=MESH)


class Exchange:
    def __init__(self, ins, out_shape, sems, start, finish):
        self.ins, self.out_shape, self.sems, self.start, self.finish = list(ins), list(out_shape), list(sems), start, finish


def gather_group(shards):
    n = len(shards)
    split = [_splits(s.shape) is not None for s in shards]

    def part(w, half):
        return _half(shards[w].shape, half)

    def copies(ins, outs, sems):
        ici_s, ici_r, _, _, local_sems = sems
        x, y, c = _place()
        chip = 2 * x + y
        local = [pltpu.make_async_copy(ins[w], outs[w].at[chip], local_sems.at[w]) for w in range(n)]
        sends = [_remote(ins[w].at[part(w, c)], outs[w].at[(chip,) + part(w, c)], ici_s.at[w, k], ici_r.at[w, k],
                         (x ^ fx, y ^ fy, c))
                 for w in range(n) for k, (fx, fy) in enumerate(CHIP_MASKS)]
        return local, sends

    def start(ins, outs, sems):
        local, sends = copies(ins, outs, sems)
        for cp in local + sends:
            cp.start()

    def finish(ins, outs, sems):
        ici_s, ici_r, d2d_s, d2d_r, _ = sems
        x, y, c = _place()
        sib = (x, y, 1 - c)
        local, sends = copies(ins, outs, sems)
        for w in range(n):
            for k, (fx, fy) in enumerate(CHIP_MASKS):
                landed = outs[w].at[(2 * (x ^ fx) + (y ^ fy),) + part(w, c)]
                _remote(landed, landed, ici_s.at[w, k], ici_r.at[w, k], sib).wait_recv()
                if split[w]:
                    cp = _remote(landed, landed, d2d_s.at[w, k], d2d_r.at[w, k], sib)
                    cp.start()
                    sends.append(cp)
        for w in range(n):
            for k, (fx, fy) in enumerate(CHIP_MASKS):
                if split[w]:
                    other = outs[w].at[(2 * (x ^ fx) + (y ^ fy),) + part(w, 1 - c)]
                    _remote(other, other, d2d_s.at[w, k], d2d_r.at[w, k], sib).wait_recv()
        for cp in sends:
            cp.wait_send()
        for cp in local:
            cp.wait()

    sems = pltpu.SemaphoreType.DMA((n, 3))
    return Exchange(shards, [jax.ShapeDtypeStruct((N_CHIPS,) + s.shape, s.dtype) for s in shards],
                    [sems, sems, sems, sems, pltpu.SemaphoreType.DMA((n,))], start, finish)


def swap_half(parts):
    n = len(parts)

    def copies(ins, outs, sems):
        x, y, c = _place()
        return [_remote(ins[w].at[(slice(None),) + _half(parts[w].shape[1:], 1 - c)], outs[w], sems[0].at[w], sems[1].at[w],
                        (x, y, 1 - c)) for w in range(n)]

    def start(ins, outs, sems):
        for cp in copies(ins, outs, sems):
            cp.start()

    def finish(ins, outs, sems):
        for cp in copies(ins, outs, sems):
            cp.wait_recv()
            cp.wait_send()

    return Exchange(parts, [jax.ShapeDtypeStruct((N_CHIPS,) + _half_shape(p.shape[1:]), F32) for p in parts],
                    [pltpu.SemaphoreType.DMA((n,))] * 2, start, finish)


def scatter_half(halves):
    n = len(halves)

    def copies(ins, outs, sems):
        x, y, c = _place()
        return [_remote(ins[w].at[2 * (x ^ fx) + (y ^ fy)], outs[w].at[k], sems[0].at[w, k], sems[1].at[w, k],
                        (x ^ fx, y ^ fy, c))
                for w in range(n) for k, (fx, fy) in enumerate(CHIP_MASKS)]

    def start(ins, outs, sems):
        for cp in copies(ins, outs, sems):
            cp.start()

    def finish(ins, outs, sems):
        for cp in copies(ins, outs, sems):
            cp.wait_recv()
            cp.wait_send()

    return Exchange(halves, [jax.ShapeDtypeStruct((3,) + h.shape[1:], h.dtype) for h in halves],
                    [pltpu.SemaphoreType.DMA((n, 3))] * 2, start, finish)


def swap_final(finals, shapes):
    n = len(finals)

    def copies(ins, outs, sems):
        x, y, c = _place()
        mine = [outs[w].at[_half(shapes[w], c)] for w in range(n)]
        local = [pltpu.make_async_copy(ins[w], mine[w], sems[2].at[w]) for w in range(n)]
        sends = [_remote(ins[w], mine[w], sems[0].at[w], sems[1].at[w], (x, y, 1 - c)) for w in range(n)]
        return local, sends

    def start(ins, outs, sems):
        local, sends = copies(ins, outs, sems)
        for cp in local + sends:
            cp.start()

    def finish(ins, outs, sems):
        x, y, c = _place()
        local, sends = copies(ins, outs, sems)
        for w in range(n):
            got = outs[w].at[_half(shapes[w], 1 - c)]
            _remote(got, got, sems[0].at[w], sems[1].at[w], (x, y, 1 - c)).wait_recv()
        for cp in sends:
            cp.wait_send()
        for cp in local:
            cp.wait()

    return Exchange(finals, [jax.ShapeDtypeStruct(tuple(s), F32) for s in shapes],
                    [pltpu.SemaphoreType.DMA((n,))] * 3, start, finish)


def exchange_call(name, ex):
    n, m = len(ex.ins), len(ex.out_shape)

    def body(*refs):
        ins, outs, sems = refs[:n], refs[n:n + m], refs[n + m:]
        ex.start(ins, outs, sems)
        ex.finish(ins, outs, sems)

    return pl.pallas_call(body, name=name, in_specs=_any_specs(n), out_specs=_any_specs(m), out_shape=ex.out_shape,
                          scratch_shapes=ex.sems, compiler_params=_params())(*ex.ins)


def _call(body, ex, *, name, grid, in_specs, out_specs, out_shape, args, scratch_shapes=()):
    sem = ("arbitrary",) * len(grid)
    if ex is None:
        outs = pl.pallas_call(body, name=name, grid=grid, in_specs=in_specs, out_specs=out_specs, out_shape=out_shape,
                              scratch_shapes=list(scratch_shapes), compiler_params=_params(sem))(*args)
        return outs, None
    ni, no, ns = len(in_specs), len(out_specs), len(scratch_shapes)
    ne, me = len(ex.ins), len(ex.out_shape)

    def wrapped(*refs):
        own_in, ex_in = refs[:ni], refs[ni:ni + ne]
        own_out, ex_out = refs[ni + ne:ni + ne + no], refs[ni + ne + no:ni + ne + no + me]
        own_scr, ex_sems = refs[ni + ne + no + me:ni + ne + no + me + ns], refs[ni + ne + no + me + ns:]
        ids = [pl.program_id(a) for a in range(len(grid))]
        first = functools.reduce(jnp.logical_and, [i == 0 for i in ids])
        last = functools.reduce(jnp.logical_and, [i == g - 1 for i, g in zip(ids, grid)])

        @pl.when(first)
        def _():
            ex.start(ex_in, ex_out, ex_sems)

        body(*own_in, *own_out, *own_scr)

        @pl.when(last)
        def _():
            ex.finish(ex_in, ex_out, ex_sems)

    outs = pl.pallas_call(
        wrapped, name=name, grid=grid, in_specs=list(in_specs) + _any_specs(ne),
        out_specs=list(out_specs) + _any_specs(me), out_shape=list(out_shape) + ex.out_shape,
        scratch_shapes=list(scratch_shapes) + ex.sems, compiler_params=_params(sem))(*args, *ex.ins)
    return outs[:no], outs[no:]


def _row_tile(rows, cap, mult=8):
    return max([t for t in range(mult, min(rows, cap) + 1, mult) if rows % t == 0] or [rows])


def add_half_call(name, part, got, where):
    _, rh, cols = got.shape
    tr = _row_tile(rh, 176, 16)
    nb = rh // tr
    by_rows = _splits(part.shape[1:]) == "rows"

    def body(where_ref, p_ref, g_ref, own_ref, send_ref):
        t = p_ref[...] + g_ref[...]
        send_ref[...] = t.astype(BF16)
        chip = where_ref[1]
        own_ref[...] = p_ref[chip] + g_ref[chip]

    blk = (N_CHIPS, tr, cols)
    return pl.pallas_call(
        body, name=name,
        grid_spec=pltpu.PrefetchScalarGridSpec(
            num_scalar_prefetch=1, grid=(nb,),
            in_specs=[pl.BlockSpec(blk, (lambda i, where_ref: (0, where_ref[0] * nb + i, 0)) if by_rows
                                   else (lambda i, where_ref: (0, i, where_ref[0]))),
                      pl.BlockSpec(blk, lambda i, where_ref: (0, i, 0))],
            out_specs=[pl.BlockSpec((tr, cols), lambda i, where_ref: (i, 0)),
                       pl.BlockSpec(blk, lambda i, where_ref: (0, i, 0))]),
        out_shape=[jax.ShapeDtypeStruct((rh, cols), F32), jax.ShapeDtypeStruct(got.shape, BF16)],
        compiler_params=_params(("parallel",)),
    )(where, part, got)


def sum_chips_call(name, own, got):
    _, rh, cols = got.shape
    tr = _row_tile(rh, 176, 16)

    def body(h_ref, g_ref, o_ref):
        o_ref[...] = ((h_ref[...] + g_ref[0].astype(F32)) + g_ref[1].astype(F32)) + g_ref[2].astype(F32)

    return pl.pallas_call(
        body, name=name, grid=(rh // tr,),
        in_specs=[pl.BlockSpec((tr, cols), lambda i: (i, 0)), pl.BlockSpec((3, tr, cols), lambda i: (0, i, 0))],
        out_specs=pl.BlockSpec((tr, cols), lambda i: (i, 0)),
        out_shape=jax.ShapeDtypeStruct((rh, cols), F32),
        compiler_params=_params(("parallel",)),
    )(own, got)


def _adamw(w, g, m, v):
    m = ADAM_B1 * m + (1.0 - ADAM_B1) * g
    v = ADAM_B2 * v + (1.0 - ADAM_B2) * (g * g)
    m_hat = m / (1.0 - ADAM_B1 ** ADAM_STEP)
    v_hat = v / (1.0 - ADAM_B2 ** ADAM_STEP)
    delta = -ADAM_LR * (m_hat / (jnp.sqrt(v_hat) + ADAM_EPS) + ADAM_WD * w)
    return delta, m, v


def adamw_call(name, g, w, m, v):
    r, cols = w.shape
    tr = r if r % 8 else _row_tile(r, 256)

    def body(g_ref, w_ref, m_ref, v_ref, go_ref, d_ref, nm_ref, nv_ref):
        g = g_ref[...]
        go_ref[...] = g
        d_ref[...], nm_ref[...], nv_ref[...] = _adamw(w_ref[...], g, m_ref[...], v_ref[...])

    spec = pl.BlockSpec((tr, cols), lambda i: (i, 0))
    return pl.pallas_call(
        body, name=name, grid=(r // tr,), in_specs=[spec] * 4, out_specs=[spec] * 4,
        out_shape=[jax.ShapeDtypeStruct((r, cols), F32)] * 4,
        compiler_params=_params(("parallel",)),
    )(g, w, m, v)


def allsum_small_call(v):
    R = v.shape[0]

    def body(v_ref, out_ref, buf, send_sems, recv_sems):
        x, y, c = _place()
        me = 4 * x + 2 * y + c
        buf[me] = v_ref[...]
        sends = []
        for k in range(1, N_DEV):
            fx, fy, fc = (k >> 2) & 1, (k >> 1) & 1, k & 1
            cp = pltpu.make_async_remote_copy(
                src_ref=v_ref, dst_ref=buf.at[me], send_sem=send_sems.at[k - 1], recv_sem=recv_sems.at[k - 1],
                device_id=(x ^ fx, y ^ fy, c ^ fc), device_id_type=MESH)
            cp.start()
            sends.append(cp)
        for k in range(1, N_DEV):
            pltpu.make_async_remote_copy(
                src_ref=v_ref, dst_ref=buf.at[me ^ k], send_sem=send_sems.at[k - 1], recv_sem=recv_sems.at[k - 1],
                device_id=(x, y, c), device_id_type=MESH).wait_recv()
        acc = buf[0]
        for d in range(1, N_DEV):
            acc = acc + buf[d]
        out_ref[...] = acc
        for cp in sends:
            cp.wait_send()

    vm = pl.BlockSpec(memory_space=pltpu.VMEM)
    return pl.pallas_call(
        body, name="allsum_small", in_specs=[vm], out_specs=vm,
        out_shape=jax.ShapeDtypeStruct((R, LANES), F32),
        scratch_shapes=[pltpu.VMEM((N_DEV, R, LANES), F32), pltpu.SemaphoreType.DMA((N_DEV - 1,)),
                        pltpu.SemaphoreType.DMA((N_DEV - 1,))],
        compiler_params=_params(),
    )(v)


def _slab(flat, mult):
    n = flat.shape[-1]
    rows = -(-n // (LANES * mult)) * mult
    flat = jnp.pad(flat, [(0, 0)] * (flat.ndim - 1) + [(0, rows * LANES - n)])
    return flat.reshape(flat.shape[:-1] + (rows, LANES))


def full_from_chips(blocks, by_col):
    _, r, c = blocks.shape
    return blocks.transpose(1, 0, 2).reshape(r, N_CHIPS * c) if by_col else blocks.reshape(N_CHIPS * r, c)


def chips_from_full(full, by_col):
    if by_col:
        r, c = full.shape[0], full.shape[1] // N_CHIPS
        return full.reshape(r, N_CHIPS, c).transpose(1, 0, 2)
    return full.reshape(N_CHIPS, full.shape[0] // N_CHIPS, full.shape[1])


SMALL_PACK = SMALL_W + ("loss", "conv_w")
SMALL_PACK_N = {**SMALL_N, "loss": 1, "conv_w": 3 * 2 * D_FF}


def pack_small(vals):
    zero = jnp.zeros((1,), F32)
    return _slab(jnp.concatenate([vals[n].reshape(-1) if n in vals else jnp.tile(zero, SMALL_PACK_N[n])
                                  for n in SMALL_PACK]), 8)


def unpack_small(slab, shapes):
    flat = slab.reshape(-1)
    out, off = {}, 0
    for n in SMALL_PACK:
        out[n] = flat[off:off + SMALL_PACK_N[n]].reshape(shapes[n])
        off += SMALL_PACK_N[n]
    return out


def _split_heads(w, a, b):
    r = w.shape[0]
    w3 = w.reshape(r, HEADS, a + b)
    return w3[:, :, :a].reshape(r, HEADS * a), w3[:, :, a:].reshape(r, HEADS * b)


def _merge_heads(wa, wb, a, b):
    r = wa.shape[0]
    return jnp.concatenate([wa.reshape(r, HEADS, a), wb.reshape(r, HEADS, b)], axis=2).reshape(r, HEADS * (a + b))


def kernel(x, positions, g_mix, w_in, g_cq, w_uq, g_ckv, w_ukv, g_sb_out, g_mla_out, w_out, g_ffn, w_up, conv_w, conv_b, w_down, g_final, loss_target, m_g_mix, m_w_in, m_g_cq, m_w_uq, m_g_ckv, m_w_ukv, m_g_sb_out, m_g_mla_out, m_w_out, m_g_ffn, m_w_up, m_conv_w, m_conv_b, m_w_down, m_g_final, v_g_mix, v_w_in, v_g_cq, v_w_uq, v_g_ckv, v_w_ukv, v_g_sb_out, v_g_mla_out, v_w_out, v_g_ffn, v_w_up, v_conv_w, v_conv_b, v_w_down, v_g_final):
    given = dict(locals())
    B, S, _ = x.shape
    T = B * S
    w_big = {n: given[n][0].T if n == "w_in" else given[n][0] for n in BIG_W}
    m_big = {n: given["m_" + n][0].T if n == "w_in" else given["m_" + n][0] for n in BIG_W}
    v_big = {n: given["v_" + n][0].T if n == "w_in" else given["v_" + n][0] for n in BIG_W}
    shard_shape = {n: w_big[n].shape for n in BIG_W}

    first = ("w_in", "w_uq", "w_ukv")
    later = ("w_out", "w_up", "w_down", "conv_w")
    x2d = x.reshape(T, D_MODEL)
    h, got_w = rmsnorm_fwd_call("norm_mix", x2d, g_mix, ex=gather_group([w_big[n].astype(BF16) for n in first]))
    full = {n: full_from_chips(g_, BIG_SHARD[n][2]) for n, g_ in zip(first, got_w) if n != "w_in"}
    gather_later = gather_group([w_big[n] if n == "conv_w" else w_big[n].astype(BF16) for n in later])
    w_in_t = jnp.pad(got_w[0].reshape(IN_COLS, D_MODEL), ((0, IN_COLS_PAD - IN_COLS), (0, 0)))
    w_uq_p = jnp.concatenate(_split_heads(full["w_uq"], HEAD_DIM, ROPE_DIM), axis=1)
    w_ukv_p = jnp.concatenate(_split_heads(full["w_ukv"], HEAD_DIM, HEAD_DIM), axis=1)

    half = ROPE_DIM // 2
    inv_freq = 1.0 / (ROPE_BASE ** (jnp.arange(half, dtype=F32) * (2.0 / ROPE_DIM)))
    cos, sin = rope_tab_call(positions.reshape(T, 1), jnp.tile(inv_freq, LANES // half).reshape(1, LANES))
    p = matmul_call("proj_in", h, w_in_t, "nt", tn=IN_COLS_PAD // 2)
    qn, qr, kn, vm, krt, cqn, ckvn = mla_prep_fwd_call(p, cos, sin, g_cq, g_ckv, w_uq_p, w_ukv_p)
    o_sb, lt_sb, got_w = sb_fwd_call(p, B, S, ex=gather_later)
    w_up4 = got_w[1]
    full.update({n: full_from_chips(g_, BIG_SHARD[n][2]) for n, g_ in zip(later, got_w) if n != "w_up"})
    conv_w_full = full["conv_w"]
    o_mla, lse = mla_fwd_call(qn, qr, kn, krt, vm, B, S)
    o_cat = outnorm_fwd_call(o_sb, o_mla, g_sb_out, g_mla_out)
    x1, hn = proj_out_norm_call(o_cat, full["w_out"], x2d, g_ffn)
    u_g, u_v, act, c_g, c_v = ffn_up_conv_call(hn, w_up4, conv_w_full, conv_b, S)
    dx2, dx2b, loss_row, dg_final = ffn_down_loss_call(
        act, full["w_down"], x1, g_final.reshape(1, D_MODEL), loss_target.reshape(T, D_MODEL))

    xi, yi, ci = _place()
    chip = (2 * xi + yi).astype(jnp.int32).reshape(1)
    where = jnp.stack([ci, 2 * xi + yi]).astype(jnp.int32)

    def add_halves(names, parts, sib_rows):
        return [add_half_call("add_half_" + n, p_, s_, where) for n, p_, s_ in zip(names, parts, sib_rows)]

    def sum_chips(names, halves, from_chips):
        return [sum_chips_call("sum_chips_" + n, h_[0], f_) for n, h_, f_ in zip(names, halves, from_chips)]

    ffn_w = ("w_down", "w_up")
    parts_ffn = [chips_from_full(wgrad_call("wgrad_down", act, dx2b, tn=512, tt=1024), False)]
    du_g, du_v, dw_up4, dcw_g, dcw_v, dcb_g, dcb_v = conv_bwd_wgrad_up_call(
        hn, u_g, u_v, c_g, c_v, dx2b, full["w_down"], conv_w_full, S)
    parts_ffn.append(dw_up4)
    dx1, dg_ffn, sib_ffn = ffn_up_bwd_call(du_g, du_v, w_up4, x1, g_ffn, dx2, swap_half(parts_ffn))
    parts_out = [chips_from_full(wgrad_call("wgrad_out", o_cat, dx1, tt=1024), False)]
    do_sb, do_mla, dg_sb_out, dg_mla_out, sib_out = proj_out_bwd_call(
        dx1, full["w_out"], o_sb, o_mla, g_sb_out, g_mla_out, swap_half(parts_out))
    early = ffn_w + ("w_out",)
    halves = add_halves(early, parts_ffn + parts_out, list(sib_ffn) + list(sib_out))
    dq_sb, dk_sb, dv_sb, from_chips = sb_bwd_call(p, lt_sb, do_sb, B, S, ex=scatter_half([h_[1] for h_ in halves]))
    finals = sum_chips(early, halves, from_chips)
    dqn, dqr4, dkn, dvm, dkrt4, done = mla_bwd_call(qn, qr, kn, krt, vm, o_mla, lse, do_mla, B, S,
        ex=swap_final(finals, [shard_shape[n] for n in early]))
    grads = dict(zip(early, done))
    dcq, dckvr, dq_cat, dkv_cat, dg_cq, dg_ckv = mla_prep_bwd_call(
        p, cos, sin, g_cq, g_ckv, w_uq_p, w_ukv_p, dqn, dqr4, dkn, dvm, dkrt4)
    dw_uq_p = wgrad_call("wgrad_uq", cqn, dq_cat)
    dw_ukv_p = wgrad_call("wgrad_ukv", ckvn, dkv_cat)
    dp = (dq_sb, dk_sb, dv_sb, dcq, dckvr)
    late = ("w_uq", "w_ukv", "w_in")
    parts_late = [chips_from_full(g_, True) for g_ in (
        _merge_heads(dw_uq_p[:, :ATT_W], dw_uq_p[:, ATT_W:], HEAD_DIM, ROPE_DIM),
        _merge_heads(dw_ukv_p[:, :ATT_W], dw_ukv_p[:, ATT_W:], HEAD_DIM, HEAD_DIM))]
    parts_late.append(chips_from_full(wgrad_in_call(h, dp)[:IN_COLS], False))
    dh, sib_late = proj_in_bwd_call(dp, w_in_t, swap_half(parts_late))
    halves = add_halves(late, parts_late, sib_late)
    grad_x, dg_mix, from_chips = rmsnorm_bwd_call(
        "norm_mix_bwd", x2d, g_mix, dh, dx1, ex=scatter_half([h_[1] for h_ in halves]))
    finals = sum_chips(late, halves, from_chips)
    grads.update(zip(late, exchange_call("swap_final_late", swap_final(finals, [shard_shape[n] for n in late]))))

    shapes = {n: given[n].shape for n in SMALL_W}
    shapes.update(loss=(), conv_w=(3, 2 * D_FF))
    small_g = {"g_mix": dg_mix, "g_cq": dg_cq, "g_ckv": dg_ckv, "g_sb_out": dg_sb_out, "g_mla_out": dg_mla_out,
               "g_ffn": dg_ffn, "conv_b": jnp.concatenate([dcb_g, dcb_v], axis=1), "g_final": dg_final,
               "loss": loss_row[0, :1], "conv_w": jnp.concatenate([dcw_g, dcw_v], axis=1)}
    gs_slab = allsum_small_call(pack_small(small_g))
    small_in = [pack_small({n: given[pre + n] for n in SMALL_W}) for pre in ("", "m_", "v_")]
    small_out = [unpack_small(s, shapes) for s in adamw_call("adamw_small", gs_slab, *small_in)]
    cw_cols = BIG_SHARD["conv_w"][1]
    grads["conv_w"] = lax.dynamic_slice_in_dim(small_out[0]["conv_w"], chip[0] * cw_cols, cw_cols, axis=1)

    big_out = {n: adamw_call("adamw_" + n, grads[n], w_big[n], m_big[n], v_big[n]) for n in BIG_W}
    weights = ("g_mix", "w_in", "g_cq", "w_uq", "g_ckv", "w_ukv", "g_sb_out", "g_mla_out", "w_out", "g_ffn",
               "w_up", "conv_w", "conv_b", "w_down", "g_final")
    outs = [small_out[0]["loss"], grad_x.reshape(B, S, D_MODEL)]
    for k in range(4):
        for n in weights:
            if n in BIG_W:
                outs.append((big_out[n][k].T if n == "w_in" else big_out[n][k])[None])
            else:
                outs.append(small_out[k][n])
    return tuple(outs)
```

```python
import functools

import jax
import jax.numpy as jnp
from jax import lax
from jax.experimental import pallas as pl
from jax.experimental.pallas import tpu as pltpu

F32 = jnp.float32
BF16 = jnp.bfloat16
MESH = pl.DeviceIdType.MESH

D_MODEL = 1024
HEADS = 8
HEAD_DIM = 64
ATT_W = HEADS * HEAD_DIM
ROPE_DIM = 32
ROPE_W = HEADS * ROPE_DIM
QK_DIM = HEAD_DIM + ROPE_DIM
Q_RANK = 384
KV_RANK = 256
D_FF = 2816
IN_COLS = 2208
IN_COLS_PAD = 2304
EPS = 1e-6
ROPE_BASE = 10000.0
SB_SCALE = HEAD_DIM ** -0.5
SB_SCALE2 = SB_SCALE * 1.4426950408889634
MLA_SCALE = QK_DIM ** -0.5
LOG2E = 1.4426950408889634
LN2 = 0.6931471805599453
MLA_SCALE2 = MLA_SCALE * LOG2E
LANES = 128
N_CHIPS = 4
N_DEV = 8
VMEM_LIMIT = 48 * 1024 * 1024
ATT_TQ = 256
ATT_TK = 256
ATT_PAIRS = 4
PAIR_LANES = [slice(i * LANES, (i + 1) * LANES) for i in range(ATT_PAIRS)]
SB_BWD_GROUP = 2
SB_FWD_GROUP = 4
MLA_GROUP = 4
NEG_BIG = -1e30

ADAM_LR = 0.001
ADAM_B1 = 0.9
ADAM_B2 = 0.999
ADAM_EPS = 1e-08
ADAM_WD = 0.01
ADAM_STEP = 10

BIG_W = ("w_in", "w_uq", "w_ukv", "w_out", "w_up", "conv_w", "w_down")
BIG_SHARD = {
    "w_in": (D_MODEL, IN_COLS // 4, True),
    "w_uq": (Q_RANK, HEADS * QK_DIM // 4, True),
    "w_ukv": (KV_RANK, 2 * ATT_W // 4, True),
    "w_out": (2 * ATT_W // 4, D_MODEL, False),
    "w_up": (D_MODEL, 2 * D_FF // 4, True),
    "conv_w": (3, 2 * D_FF // 4, True),
    "w_down": (D_FF // 4, D_MODEL, False),
}
SMALL_W = ("g_mix", "g_cq", "g_ckv", "g_sb_out", "g_mla_out", "g_ffn", "conv_b", "g_final")
SMALL_N = {"g_mix": D_MODEL, "g_cq": Q_RANK, "g_ckv": KV_RANK, "g_sb_out": ATT_W, "g_mla_out": ATT_W,
           "g_ffn": D_MODEL, "conv_b": 2 * D_FF, "g_final": D_MODEL}


def _params(sem=None, **kw):
    return pltpu.CompilerParams(dimension_semantics=sem, vmem_limit_bytes=VMEM_LIMIT, **kw)


def _dot(a, b, dims):
    return lax.dot_general(a, b, (dims, ((), ())), preferred_element_type=F32)


def _nn(a, b):
    return _dot(a, b, ((1,), (0,)))


def _nt(a, b):
    return _dot(a, b, ((1,), (1,)))


def _tn(a, b):
    return _dot(a, b, ((0,), (0,)))


def _split2(x):
    hi = x.astype(BF16)
    lo = (x - hi.astype(F32)).astype(BF16)
    return hi, lo


def _split3(x):
    hi = x.astype(BF16)
    r1 = x - hi.astype(F32)
    mid = r1.astype(BF16)
    return hi, mid, (r1 - mid.astype(F32)).astype(BF16)


def _rms_r(x, d):
    return lax.rsqrt(jnp.sum(x * x, axis=-1, keepdims=True) * (1.0 / d) + EPS)


def _rms_bwd(x, g, dy, d):
    r = _rms_r(x, d)
    xhat = x * r
    gy = dy * g
    dx = r * (gy - xhat * (jnp.sum(xhat * gy, axis=-1, keepdims=True) * (1.0 / d)))
    return dx, jnp.sum(dy * xhat, axis=0, keepdims=True)


def _rot(x):
    lane = lax.broadcasted_iota(jnp.int32, x.shape, x.ndim - 1)
    n = x.shape[-1]
    return jnp.where((lane & 31) < 16, pltpu.roll(x, n - 16, x.ndim - 1), pltpu.roll(x, 16, x.ndim - 1))


def _fold4(x):
    return x + pltpu.roll(x, 32, 1) + pltpu.roll(x, 64, 1) + pltpu.roll(x, 96, 1)


def matmul_call(name, a, b, mode, out_dtype=F32, res=None, tm=512, tn=None, ex=None):
    M, K = a.shape
    N = b.shape[1] if mode == "nn" else b.shape[0]
    tn = N if tn is None else tn
    assert M % tm == 0 and N % tn == 0

    def body(*refs):
        if res is None:
            a_ref, b_ref, o_ref = refs
        else:
            a_ref, b_ref, r_ref, o_ref = refs
        av = a_ref[...].astype(BF16)
        bv = b_ref[...].astype(BF16)
        acc = _nn(av, bv) if mode == "nn" else _nt(av, bv)
        if res is not None:
            acc = r_ref[...] + acc
        o_ref[...] = acc.astype(out_dtype)

    in_specs = [pl.BlockSpec((tm, K), lambda j, i: (i, 0))]
    if mode == "nn":
        in_specs.append(pl.BlockSpec((K, tn), lambda j, i: (0, j)))
    else:
        in_specs.append(pl.BlockSpec((tn, K), lambda j, i: (j, 0)))
    args = [a, b]
    if res is not None:
        in_specs.append(pl.BlockSpec((tm, tn), lambda j, i: (i, j)))
        args.append(res)
    outs, moved = _call(body, ex, name=name, grid=(N // tn, M // tm), in_specs=in_specs,
                        out_specs=[pl.BlockSpec((tm, tn), lambda j, i: (i, j))],
                        out_shape=[jax.ShapeDtypeStruct((M, N), out_dtype)], args=args)
    return outs[0] if ex is None else (outs[0], moved)


def _rows(tm, width):
    return pl.BlockSpec((tm, width), lambda i: (i, 0))


def _whole(a, **kw):
    return pl.BlockSpec(a.shape, lambda i: (0,) * a.ndim, **kw)


def proj_out_norm_call(o_sb, o_mla, g_sb, g_mla, w_out, x, g_ffn, tm=512):
    T = o_sb.shape[0]
    N = w_out.shape[1]

    def body(a_ref, b_ref, ga_ref, gb_ref, w_ref, x_ref, g_ref, oc_ref, x1_ref, hn_ref):
        a = a_ref[...]
        b = b_ref[...]
        ya = (a * _rms_r(a, ATT_W)) * ga_ref[...]
        yb = (b * _rms_r(b, ATT_W)) * gb_ref[...]
        o_cat = jnp.concatenate([ya, yb], axis=1).astype(BF16)
        oc_ref[...] = o_cat
        x1 = x_ref[...] + _nn(o_cat, w_ref[...])
        x1_ref[...] = x1
        hn_ref[...] = ((x1 * _rms_r(x1, N)) * g_ref[...]).astype(BF16)

    return pl.pallas_call(
        body, name="proj_out", grid=(T // tm,),
        in_specs=[_rows(tm, ATT_W), _rows(tm, ATT_W), _whole(g_sb), _whole(g_mla), _whole(w_out), _rows(tm, N),
                  _whole(g_ffn)],
        out_specs=[_rows(tm, 2 * ATT_W), _rows(tm, N), _rows(tm, N)],
        out_shape=[jax.ShapeDtypeStruct((T, 2 * ATT_W), BF16), jax.ShapeDtypeStruct((T, N), F32),
                   jax.ShapeDtypeStruct((T, N), BF16)],
        compiler_params=_params(("parallel",)),
    )(o_sb, o_mla, g_sb, g_mla, w_out, x, g_ffn)


def norm_mix_rope_call(x, g, pos, inv_freq, ex, tm=512):
    T, d = x.shape

    def body(x_ref, g_ref, p_ref, f_ref, o_ref, c_ref, s_ref):
        xv = x_ref[...]
        o_ref[...] = ((xv * _rms_r(xv, d)) * g_ref[...]).astype(BF16)
        ang = p_ref[...].astype(F32) * f_ref[...]
        lane = lax.broadcasted_iota(jnp.int32, ang.shape, 1)
        sn = jnp.sin(ang)
        c_ref[...] = jnp.cos(ang)
        s_ref[...] = jnp.where((lane & 31) < 16, -sn, sn)

    outs, moved = _call(
        body, ex, name="norm_mix", grid=(T // tm,),
        in_specs=[_rows(tm, d), _whole(g), _rows(tm, 1), _whole(inv_freq)],
        out_specs=[_rows(tm, d), _rows(tm, LANES), _rows(tm, LANES)],
        out_shape=[jax.ShapeDtypeStruct((T, d), BF16), jax.ShapeDtypeStruct((T, LANES), F32),
                   jax.ShapeDtypeStruct((T, LANES), F32)], args=(x, g, pos, inv_freq))
    return tuple(outs) + (moved,)


def ffn_down_loss_call(act, w_down, x1, g, target, tm=512):
    T, K = act.shape
    d = w_down.shape[1]

    def body(a_ref, w_ref, x1_ref, g_ref, t_ref, dx_ref, dxb_ref, loss_ref, dg_ref):
        @pl.when(pl.program_id(0) == 0)
        def _():
            loss_ref[...] = jnp.zeros_like(loss_ref)
            dg_ref[...] = jnp.zeros_like(dg_ref)

        x = x1_ref[...] + _nn(a_ref[...], w_ref[...])
        g = g_ref[...]
        y = (x * _rms_r(x, d)) * g
        err = y - t_ref[...]
        loss_ref[...] += jnp.sum(jnp.sum(err * err, axis=1, keepdims=True), axis=0, keepdims=True) * (0.5 / d)
        dx, dg = _rms_bwd(x, g, err * (1.0 / d), d)
        dx_ref[...] = dx
        dxb_ref[...] = dx.astype(BF16)
        dg_ref[...] += dg

    return pl.pallas_call(
        body, name="ffn_down_loss", grid=(T // tm,),
        in_specs=[_rows(tm, K), _whole(w_down), _rows(tm, d), _whole(g), _rows(tm, d)],
        out_specs=[_rows(tm, d), _rows(tm, d), pl.BlockSpec((1, LANES), lambda i: (0, 0)), _whole(g)],
        out_shape=[jax.ShapeDtypeStruct((T, d), F32), jax.ShapeDtypeStruct((T, d), BF16),
                   jax.ShapeDtypeStruct((1, LANES), F32), jax.ShapeDtypeStruct((1, d), F32)],
        compiler_params=_params(("arbitrary",)),
    )(act, w_down, x1, g, target)


def proj_out_bwd_call(dx1, w_out, o_sb, o_mla, g_sb, g_mla, ex, tm=512):
    T, N = dx1.shape

    def body(d_ref, w_ref, a_ref, b_ref, ga_ref, gb_ref, da_ref, db_ref, dga_ref, dgb_ref):
        @pl.when(pl.program_id(0) == 0)
        def _():
            dga_ref[...] = jnp.zeros_like(dga_ref)
            dgb_ref[...] = jnp.zeros_like(dgb_ref)

        d = _nt(d_ref[...].astype(BF16), w_ref[...])
        da, dga = _rms_bwd(a_ref[...], ga_ref[...], d[:, :ATT_W], ATT_W)
        db, dgb = _rms_bwd(b_ref[...], gb_ref[...], d[:, ATT_W:], ATT_W)
        da_ref[...] = da
        db_ref[...] = db
        dga_ref[...] += dga
        dgb_ref[...] += dgb

    outs, moved = _call(
        body, ex, name="proj_out_bwd", grid=(T // tm,),
        in_specs=[_rows(tm, N), _whole(w_out), _rows(tm, ATT_W), _rows(tm, ATT_W), _whole(g_sb), _whole(g_mla)],
        out_specs=[_rows(tm, ATT_W), _rows(tm, ATT_W), _whole(g_sb), _whole(g_mla)],
        out_shape=[jax.ShapeDtypeStruct((T, ATT_W), F32), jax.ShapeDtypeStruct((T, ATT_W), F32),
                   jax.ShapeDtypeStruct((1, ATT_W), F32), jax.ShapeDtypeStruct((1, ATT_W), F32)],
        args=(dx1, w_out, o_sb, o_mla, g_sb, g_mla))
    return tuple(outs) + (moved,)


def wgrad_call(name, a, b, tn=None, tt=512, by_chip=False):
    T, M = a.shape
    N = b.shape[1]
    tn = N if tn is None else tn
    tt = min(tt, T)
    assert T % tt == 0 and N % tn == 0
    if by_chip:
        out_spec = pl.BlockSpec((None, M, tn), lambda j, t: (j, 0, 0))
        out_shape = jax.ShapeDtypeStruct((N // tn, M, tn), F32)
    else:
        out_spec = pl.BlockSpec((M, tn), lambda j, t: (0, j))
        out_shape = jax.ShapeDtypeStruct((M, N), F32)

    def body(a_ref, b_ref, o_ref):
        @pl.when(pl.program_id(1) == 0)
        def _():
            o_ref[...] = jnp.zeros_like(o_ref)

        o_ref[...] += _tn(a_ref[...].astype(BF16), b_ref[...].astype(BF16))

    return pl.pallas_call(
        body, name=name, grid=(N // tn, T // tt),
        in_specs=[pl.BlockSpec((tt, M), lambda j, t: (t, 0)), pl.BlockSpec((tt, tn), lambda j, t: (t, j))],
        out_specs=out_spec, out_shape=out_shape,
        compiler_params=_params(("parallel", "arbitrary")),
    )(a, b)


UP_COLS = 2 * D_FF // N_CHIPS


def _shift_down_after(u, prev8, n):
    top = pltpu.roll(jnp.concatenate([prev8, u[0:8]], axis=0), n, 0)[8:16]
    return jnp.concatenate([top, pltpu.roll(u, n, 0)[8:]], axis=0)


def ffn_up_conv_call(hn, w4, conv_w, conv_b, S, tm=512):
    T, K = hn.shape
    per_seq = S // tm

    def body(a_ref, wg_ref, wv_ref, cg_ref, cv_ref, bg_ref, bv_ref, ug_ref, uv_ref, act_ref, og_ref, ov_ref, pg_ref, pv_ref):
        @pl.when(pl.program_id(1) % per_seq == 0)
        def _():
            pg_ref[...] = jnp.zeros_like(pg_ref)
            pv_ref[...] = jnp.zeros_like(pv_ref)

        a = a_ref[...]
        outs = []
        for w_ref, c_ref, b_ref, u_ref, prev_ref in ((wg_ref, cg_ref, bg_ref, ug_ref, pg_ref),
                                                     (wv_ref, cv_ref, bv_ref, uv_ref, pv_ref)):
            u = _nn(a, w_ref[...])
            u_ref[...] = u
            prev8 = prev_ref[...]
            taps = _taps(c_ref)
            outs.append(taps[0] * _shift_down_after(u, prev8, 2) + taps[1] * _shift_down_after(u, prev8, 1)
                        + taps[2] * u + b_ref[...])
            prev_ref[...] = u[tm - 8:, :]
        gate, val = outs
        sig = 1.0 / (1.0 + jnp.exp(-gate))
        silu = gate * sig
        act_ref[...] = (silu * val).astype(BF16)
        og_ref[...] = silu.astype(BF16)
        ov_ref[...] = (val * (sig * (1.0 + gate * (1.0 - sig)))).astype(BF16)

    out = pl.BlockSpec((tm, UP_COLS), lambda j, i: (i, j))

    def cols(rows, off):
        return pl.BlockSpec((rows, UP_COLS), lambda j, i: (0, off + j))

    return pl.pallas_call(
        body, name="ffn_up_conv", grid=(2, T // tm),
        in_specs=[pl.BlockSpec((tm, K), lambda j, i: (i, 0)),
                  pl.BlockSpec((None, K, UP_COLS), lambda j, i: (j, 0, 0)),
                  pl.BlockSpec((None, K, UP_COLS), lambda j, i: (2 + j, 0, 0)),
                  cols(3, 0), cols(3, 2), cols(1, 0), cols(1, 2)],
        out_specs=[out] * 5,
        out_shape=[jax.ShapeDtypeStruct((T, D_FF), F32)] * 2 + [jax.ShapeDtypeStruct((T, D_FF), BF16)] * 3,
        scratch_shapes=[pltpu.VMEM((8, UP_COLS), F32)] * 2,
        compiler_params=_params(("arbitrary", "arbitrary")),
    )(hn, w4, w4, conv_w, conv_w, conv_b, conv_b)


def _shift_up_before(d, next8, n):
    rows = d.shape[0]
    bottom = pltpu.roll(jnp.concatenate([d[rows - 8:], next8], axis=0), 16 - n, 0)[0:8]
    return jnp.concatenate([pltpu.roll(d, rows - n, 0)[:rows - 8], bottom], axis=0)


def conv_bwd_wgrad_up_call(hn, ug, uv, cg, cv, dx2, w_down, conv_w, S, tt=256):
    T, M = hn.shape
    nt = T // tt
    per_seq = S // tt

    def body(a_ref, ug_ref, uv_ref, cg_ref, cv_ref, dx_ref, wd_ref, wg_ref, wv_ref,
             dug_ref, duv_ref, dw_ref, dwg_ref, dwv_ref, dbg_ref, dbv_ref, ng_ref, nv_ref):
        step = pl.program_id(1)

        @pl.when(step == 0)
        def _():
            for r in (dw_ref, dwg_ref, dwv_ref, dbg_ref, dbv_ref):
                r[...] = jnp.zeros_like(r)

        @pl.when((nt - 1 - step) % per_seq == per_seq - 1)
        def _():
            ng_ref[...] = jnp.zeros_like(ng_ref)
            nv_ref[...] = jnp.zeros_like(nv_ref)

        da = _nt(dx_ref[...], wd_ref[...])
        dval = da * cg_ref[...].astype(F32)
        dgate = da * cv_ref[...].astype(F32)
        a = a_ref[...]
        for k, (u_ref, d, w, du_ref, dcw_ref, dcb_ref, next_ref) in enumerate((
                (ug_ref, dgate, _taps(wg_ref), dug_ref, dwg_ref, dbg_ref, ng_ref),
                (uv_ref, dval, _taps(wv_ref), duv_ref, dwv_ref, dbv_ref, nv_ref))):
            u_ = u_ref[...]
            next8 = next_ref[...]
            d1 = _shift_up_before(d, next8, 1)
            d2 = _shift_up_before(d, next8, 2)
            du = (w[2] * d + w[1] * d1 + w[0] * d2).astype(BF16)
            du_ref[...] = du
            dw_ref[k] += _tn(a, du)
            dcb_ref[...] += jnp.sum(d, axis=0, keepdims=True)
            dcw_ref[0:1, :] += jnp.sum(d2 * u_, axis=0, keepdims=True)
            dcw_ref[1:2, :] += jnp.sum(d1 * u_, axis=0, keepdims=True)
            dcw_ref[2:3, :] += jnp.sum(d * u_, axis=0, keepdims=True)
            next_ref[...] = d[0:8, :]

    def tok(width, by_col):
        return pl.BlockSpec((tt, width), (lambda j, i: (nt - 1 - i, j)) if by_col else (lambda j, i: (nt - 1 - i, 0)))

    def cols(rows, off):
        return pl.BlockSpec((rows, UP_COLS), lambda j, i: (0, off + j))

    outs = pl.pallas_call(
        body, name="conv_bwd_wgrad_up", grid=(2, nt),
        in_specs=[tok(M, False), tok(UP_COLS, True), tok(UP_COLS, True), tok(UP_COLS, True), tok(UP_COLS, True),
                  tok(dx2.shape[1], False), pl.BlockSpec((UP_COLS, w_down.shape[1]), lambda j, i: (j, 0)),
                  cols(3, 0), cols(3, 2)],
        out_specs=[tok(UP_COLS, True), tok(UP_COLS, True),
                   pl.BlockSpec((2, None, M, UP_COLS), lambda j, i: (0, j, 0, 0)),
                   cols(3, 0), cols(3, 0), cols(1, 0), cols(1, 0)],
        out_shape=[jax.ShapeDtypeStruct((T, D_FF), BF16), jax.ShapeDtypeStruct((T, D_FF), BF16),
                   jax.ShapeDtypeStruct((2, 2, M, UP_COLS), F32),
                   jax.ShapeDtypeStruct((3, D_FF), F32), jax.ShapeDtypeStruct((3, D_FF), F32),
                   jax.ShapeDtypeStruct((1, D_FF), F32), jax.ShapeDtypeStruct((1, D_FF), F32)],
        scratch_shapes=[pltpu.VMEM((8, UP_COLS), F32)] * 2,
        compiler_params=_params(("arbitrary", "arbitrary")),
    )(hn, ug, uv, cg, cv, dx2, w_down, conv_w, conv_w)
    return (outs[0], outs[1], outs[2].reshape(N_CHIPS, M, UP_COLS)) + tuple(outs[3:])


def ffn_up_bwd_call(du_g, du_v, w4, x1, g_ffn, dx2, ex, tm=512):
    T = du_g.shape[0]
    N = w4.shape[1]

    def body(g_ref, v_ref, w_ref, x1_ref, gf_ref, dx2_ref, dx1_ref, dg_ref):
        @pl.when(pl.program_id(0) == 0)
        def _():
            dg_ref[...] = jnp.zeros_like(dg_ref)

        dhn = _nt(g_ref[:, :UP_COLS], w_ref[0]) + _nt(g_ref[:, UP_COLS:], w_ref[1])
        dhn = dhn + _nt(v_ref[:, :UP_COLS], w_ref[2]) + _nt(v_ref[:, UP_COLS:], w_ref[3])
        dx, dg = _rms_bwd(x1_ref[...], gf_ref[...], dhn, N)
        dx1_ref[...] = dx2_ref[...] + dx
        dg_ref[...] += dg

    outs, moved = _call(
        body, ex, name="ffn_up_bwd", grid=(T // tm,),
        in_specs=[_rows(tm, D_FF), _rows(tm, D_FF), _whole(w4, pipeline_mode=pl.Buffered(1)), _rows(tm, N),
                  _whole(g_ffn), _rows(tm, N)],
        out_specs=[_rows(tm, N), _whole(g_ffn)],
        out_shape=[jax.ShapeDtypeStruct((T, N), F32), jax.ShapeDtypeStruct((1, N), F32)],
        args=(du_g, du_v, w4, x1, g_ffn, dx2))
    return outs[0], outs[1], moved


def wgrad_up_call(hn, du_g, du_v, tt=512):
    T, M = hn.shape

    def body(a_ref, g_ref, v_ref, o_ref):
        @pl.when(pl.program_id(1) == 0)
        def _():
            o_ref[...] = jnp.zeros_like(o_ref)

        a = a_ref[...]
        o_ref[0] += _tn(a, g_ref[...])
        o_ref[1] += _tn(a, v_ref[...])

    col = pl.BlockSpec((tt, UP_COLS), lambda j, t: (t, j))
    out = pl.pallas_call(
        body, name="wgrad_up", grid=(2, T // tt),
        in_specs=[pl.BlockSpec((tt, M), lambda j, t: (t, 0)), col, col],
        out_specs=pl.BlockSpec((2, None, M, UP_COLS), lambda j, t: (0, j, 0, 0)),
        out_shape=jax.ShapeDtypeStruct((2, 2, M, UP_COLS), F32),
        compiler_params=_params(("parallel", "arbitrary")),
    )(hn, du_g, du_v)
    return out.reshape(N_CHIPS, M, UP_COLS)


IN_PIECES = ((0, ATT_W), (ATT_W, ATT_W), (2 * ATT_W, ATT_W), (3 * ATT_W, Q_RANK), (3 * ATT_W + Q_RANK, Q_RANK))


def proj_in_bwd_call(pieces, w_in_t, ex, tm=512):
    T = pieces[0].shape[0]
    N = w_in_t.shape[1]
    n = len(pieces)

    def body(*refs):
        o_ref = refs[2 * n]
        acc = _nn(refs[0][...].astype(BF16), refs[n][...])
        for i in range(1, n):
            acc = acc + _nn(refs[i][...].astype(BF16), refs[n + i][...])
        o_ref[...] = acc

    outs, moved = _call(
        body, ex, name="proj_in_bwd", grid=(T // tm,),
        in_specs=[pl.BlockSpec((tm, w), lambda i: (i, 0)) for _, w in IN_PIECES]
        + [pl.BlockSpec((w, N), functools.partial(lambda c, i: (c, 0), off // w)) for off, w in IN_PIECES],
        out_specs=[pl.BlockSpec((tm, N), lambda i: (i, 0))],
        out_shape=[jax.ShapeDtypeStruct((T, N), F32)], args=tuple(pieces) + (w_in_t,) * n)
    return outs[0], moved


def wgrad_in_call(h, pieces, tt=512):
    T, M = h.shape
    n = len(pieces)

    def body(*refs):
        a_ref, o_ref = refs[0], refs[n + 1]

        @pl.when(pl.program_id(0) == 0)
        def _():
            o_ref[...] = jnp.zeros_like(o_ref)

        a = a_ref[...]
        for i, (off, w) in enumerate(IN_PIECES):
            o_ref[off:off + w, :] += _tn(refs[1 + i][...].astype(BF16), a)

    return pl.pallas_call(
        body, name="wgrad_in", grid=(T // tt,),
        in_specs=[pl.BlockSpec((tt, M), lambda t: (t, 0))] + [pl.BlockSpec((tt, w), lambda t: (t, 0)) for _, w in IN_PIECES],
        out_specs=pl.BlockSpec((IN_COLS_PAD, M), lambda t: (0, 0)),
        out_shape=jax.ShapeDtypeStruct((IN_COLS_PAD, M), F32),
        compiler_params=_params(("arbitrary",)),
    )(h, *pieces)


def rmsnorm_fwd_call(name, x, g, tm=512, ex=None):
    T, d = x.shape

    def body(x_ref, g_ref, o_ref):
        x = x_ref[...]
        o_ref[...] = ((x * _rms_r(x, d)) * g_ref[...]).astype(BF16)

    row = pl.BlockSpec((tm, d), lambda i: (i, 0))
    outs, moved = _call(body, ex, name=name, grid=(T // tm,), in_specs=[row, pl.BlockSpec((1, d), lambda i: (0, 0))],
                        out_specs=[row], out_shape=[jax.ShapeDtypeStruct((T, d), BF16)], args=(x, g))
    return outs[0] if ex is None else (outs[0], moved)


def rmsnorm_bwd_call(name, x, g, dy, res, tm=512, ex=None):
    T, d = x.shape

    def body(x_ref, g_ref, dy_ref, r_ref, dx_ref, dg_ref):
        @pl.when(pl.program_id(0) == 0)
        def _():
            dg_ref[...] = jnp.zeros_like(dg_ref)

        dx, dg = _rms_bwd(x_ref[...], g_ref[...], dy_ref[...], d)
        dx_ref[...] = r_ref[...] + dx
        dg_ref[...] += dg

    row = pl.BlockSpec((tm, d), lambda i: (i, 0))
    vec = pl.BlockSpec((1, d), lambda i: (0, 0))
    outs, moved = _call(body, ex, name=name, grid=(T // tm,), in_specs=[row, vec, row, row], out_specs=[row, vec],
                        out_shape=[jax.ShapeDtypeStruct((T, d), F32), jax.ShapeDtypeStruct((1, d), F32)],
                        args=(x, g, dy, res))
    return tuple(outs) if ex is None else tuple(outs) + (moved,)


def outnorm_fwd_call(o_sb, o_mla, g_sb, g_mla, tm=512):
    T = o_sb.shape[0]

    def body(a_ref, b_ref, ga_ref, gb_ref, o_ref):
        a = a_ref[...]
        b = b_ref[...]
        ya = (a * _rms_r(a, ATT_W)) * ga_ref[...]
        yb = (b * _rms_r(b, ATT_W)) * gb_ref[...]
        o_ref[...] = jnp.concatenate([ya, yb], axis=1).astype(BF16)

    row = pl.BlockSpec((tm, ATT_W), lambda i: (i, 0))
    vec = pl.BlockSpec((1, ATT_W), lambda i: (0, 0))
    return pl.pallas_call(
        body, name="outnorm_fwd", grid=(T // tm,), in_specs=[row, row, vec, vec],
        out_specs=pl.BlockSpec((tm, 2 * ATT_W), lambda i: (i, 0)),
        out_shape=jax.ShapeDtypeStruct((T, 2 * ATT_W), BF16),
        compiler_params=_params(("parallel",)),
    )(o_sb, o_mla, g_sb, g_mla)


def outnorm_bwd_call(o_sb, o_mla, g_sb, g_mla, do_cat, tm=512, ex=None):
    T = o_sb.shape[0]

    def body(a_ref, b_ref, ga_ref, gb_ref, d_ref, da_ref, db_ref, dga_ref, dgb_ref):
        @pl.when(pl.program_id(0) == 0)
        def _():
            dga_ref[...] = jnp.zeros_like(dga_ref)
            dgb_ref[...] = jnp.zeros_like(dgb_ref)

        d = d_ref[...]
        da, dga = _rms_bwd(a_ref[...], ga_ref[...], d[:, :ATT_W], ATT_W)
        db, dgb = _rms_bwd(b_ref[...], gb_ref[...], d[:, ATT_W:], ATT_W)
        da_ref[...] = da
        db_ref[...] = db
        dga_ref[...] += dga
        dgb_ref[...] += dgb

    row = pl.BlockSpec((tm, ATT_W), lambda i: (i, 0))
    vec = pl.BlockSpec((1, ATT_W), lambda i: (0, 0))
    outs, moved = _call(
        body, ex, name="outnorm_bwd", grid=(T // tm,),
        in_specs=[row, row, vec, vec, pl.BlockSpec((tm, 2 * ATT_W), lambda i: (i, 0))],
        out_specs=[row, row, vec, vec],
        out_shape=[jax.ShapeDtypeStruct((T, ATT_W), F32), jax.ShapeDtypeStruct((T, ATT_W), F32),
                   jax.ShapeDtypeStruct((1, ATT_W), F32), jax.ShapeDtypeStruct((1, ATT_W), F32)],
        args=(o_sb, o_mla, g_sb, g_mla, do_cat))
    return tuple(outs) if ex is None else tuple(outs) + (moved,)


def final_loss_call(x2, g, target, tm=512):
    T, d = x2.shape

    def body(x_ref, g_ref, t_ref, dx_ref, dxb_ref, loss_ref, dg_ref):
        @pl.when(pl.program_id(0) == 0)
        def _():
            loss_ref[...] = jnp.zeros_like(loss_ref)
            dg_ref[...] = jnp.zeros_like(dg_ref)

        x = x_ref[...]
        g = g_ref[...]
        y = (x * _rms_r(x, d)) * g
        err = y - t_ref[...]
        loss_ref[...] += jnp.sum(jnp.sum(err * err, axis=1, keepdims=True), axis=0, keepdims=True) * (0.5 / d)
        dx, dg = _rms_bwd(x, g, err * (1.0 / d), d)
        dx_ref[...] = dx
        dxb_ref[...] = dx.astype(BF16)
        dg_ref[...] += dg

    row = pl.BlockSpec((tm, d), lambda i: (i, 0))
    vec = pl.BlockSpec((1, d), lambda i: (0, 0))
    return pl.pallas_call(
        body, name="final_loss", grid=(T // tm,), in_specs=[row, vec, row],
        out_specs=[row, row, pl.BlockSpec((1, LANES), lambda i: (0, 0)), vec],
        out_shape=[jax.ShapeDtypeStruct((T, d), F32), jax.ShapeDtypeStruct((T, d), BF16),
                   jax.ShapeDtypeStruct((1, LANES), F32), jax.ShapeDtypeStruct((1, d), F32)],
        compiler_params=_params(("arbitrary",)),
    )(x2, g, target)


def rope_tab_call(pos, inv_freq, tm=512):
    T = pos.shape[0]

    def body(p_ref, f_ref, c_ref, s_ref):
        ang = p_ref[...].astype(F32) * f_ref[...]
        lane = lax.broadcasted_iota(jnp.int32, ang.shape, 1)
        sn = jnp.sin(ang)
        c_ref[...] = jnp.cos(ang)
        s_ref[...] = jnp.where((lane & 31) < 16, -sn, sn)

    row = pl.BlockSpec((tm, LANES), lambda i: (i, 0))
    return pl.pallas_call(
        body, name="rope_tab", grid=(T // tm,),
        in_specs=[pl.BlockSpec((tm, 1), lambda i: (i, 0)), pl.BlockSpec((1, LANES), lambda i: (0, 0))],
        out_specs=[row, row],
        out_shape=[jax.ShapeDtypeStruct((T, LANES), F32)] * 2,
        compiler_params=_params(("parallel",)),
    )(pos, inv_freq)


def mla_prep_fwd_call(p, cos, sin, g_cq, g_ckv, w_uq_p, w_ukv_p, tm=512):
    T = p.shape[0]

    def body(cq_ref, ckvr_ref, c_ref, s_ref, gq_ref, gkv_ref, wq_ref, wkv_ref,
             qn_ref, qr_ref, kn_ref, vm_ref, krt_ref, cqn_ref, ckvn_ref):
        c = c_ref[...]
        s = s_ref[...]
        cq = cq_ref[...]
        cqn = ((cq * _rms_r(cq, Q_RANK)) * gq_ref[...]).astype(BF16)
        cqn_ref[...] = cqn
        q = _nn(cqn, wq_ref[...])
        qn_ref[...] = q[:, :ATT_W].astype(BF16)
        for g in range(ROPE_W // LANES):
            qr = q[:, ATT_W + g * LANES:ATT_W + (g + 1) * LANES]
            qr_ref[:, g * LANES:(g + 1) * LANES] = (qr * c + _rot(qr) * s).astype(BF16)
        ckvr = ckvr_ref[...]
        ckv = ckvr[:, :KV_RANK]
        ckvn = ((ckv * _rms_r(ckv, KV_RANK)) * gkv_ref[...]).astype(BF16)
        ckvn_ref[...] = ckvn
        kv = _nn(ckvn, wkv_ref[...])
        kn_ref[...] = kv[:, :ATT_W].astype(BF16)
        vm_ref[...] = kv[:, ATT_W:].astype(BF16)
        kr = _fold4(ckvr[:, KV_RANK:])
        krt_ref[...] = (kr * c + _rot(kr) * s).astype(BF16)

    def row(w, j=0):
        return pl.BlockSpec((tm, w), lambda i: (i, j))

    def full(a):
        return pl.BlockSpec(a.shape, lambda i: (0, 0))

    return pl.pallas_call(
        body, name="mla_prep_fwd", grid=(T // tm,),
        in_specs=[row(Q_RANK, 4), row(Q_RANK, 5), row(LANES), row(LANES), full(g_cq), full(g_ckv),
                  full(w_uq_p), full(w_ukv_p)],
        out_specs=[row(ATT_W), row(ROPE_W), row(ATT_W), row(ATT_W), row(LANES), row(Q_RANK), row(KV_RANK)],
        out_shape=[jax.ShapeDtypeStruct((T, w), BF16) for w in (ATT_W, ROPE_W, ATT_W, ATT_W, LANES, Q_RANK, KV_RANK)],
        compiler_params=_params(("parallel",)),
    )(p, p, cos, sin, g_cq, g_ckv, w_uq_p, w_ukv_p)


def mla_prep_bwd_call(p, cos, sin, g_cq, g_ckv, w_uq_p, w_ukv_p, dqn, dqr4, dkn, dvm, dkrt4, tm=512):
    T = p.shape[0]

    def body(cq_ref, ckvr_ref, c_ref, s_ref, gq_ref, gkv_ref, wq_ref, wkv_ref,
             dqn_ref, dqr4_ref, dkn_ref, dvm_ref, dkrt4_ref,
             dcq_ref, dckvr_ref, dq_ref, dkv_ref, dgq_ref, dgkv_ref):
        @pl.when(pl.program_id(0) == 0)
        def _():
            dgq_ref[...] = jnp.zeros_like(dgq_ref)
            dgkv_ref[...] = jnp.zeros_like(dgkv_ref)

        c = c_ref[...]
        s = s_ref[...]
        d4 = dqr4_ref[...]
        dqr = [d4[:, :128] + d4[:, 128:256], d4[:, 256:384] + d4[:, 384:]]
        dqr = [t * c + _rot(t * s) for t in dqr]
        dq = jnp.concatenate([dqn_ref[...]] + dqr, axis=1).astype(BF16)
        dq_ref[...] = dq
        dcq, dgq = _rms_bwd(cq_ref[...], gq_ref[...], _nt(dq, wq_ref[...]), Q_RANK)
        dcq_ref[...] = dcq
        dgq_ref[...] += dgq
        dkv = jnp.concatenate([dkn_ref[...], dvm_ref[...]], axis=1).astype(BF16)
        dkv_ref[...] = dkv
        ckvr = ckvr_ref[...]
        dckv, dgkv = _rms_bwd(ckvr[:, :KV_RANK], gkv_ref[...], _nt(dkv, wkv_ref[...]), KV_RANK)
        dgkv_ref[...] += dgkv
        k4 = dkrt4_ref[...]
        dkr = _fold4(k4[:, :128] + k4[:, 128:256] + k4[:, 256:384] + k4[:, 384:])
        dkr = dkr * c + _rot(dkr * s)
        lane = lax.broadcasted_iota(jnp.int32, dkr.shape, 1)
        dckvr_ref[...] = jnp.concatenate([dckv, jnp.where(lane < ROPE_DIM, dkr, 0.0)], axis=1)

    def row(w, j=0):
        return pl.BlockSpec((tm, w), lambda i: (i, j))

    def full(a):
        return pl.BlockSpec(a.shape, lambda i: (0, 0))

    return pl.pallas_call(
        body, name="mla_prep_bwd", grid=(T // tm,),
        in_specs=[row(Q_RANK, 4), row(Q_RANK, 5), row(LANES), row(LANES), full(g_cq), full(g_ckv),
                  full(w_uq_p), full(w_ukv_p), row(ATT_W), row(ATT_W), row(ATT_W), row(ATT_W), row(ATT_W)],
        out_specs=[row(Q_RANK), row(Q_RANK), row(ATT_W + ROPE_W), row(2 * ATT_W),
                   pl.BlockSpec((1, Q_RANK), lambda i: (0, 0)), pl.BlockSpec((1, KV_RANK), lambda i: (0, 0))],
        out_shape=[jax.ShapeDtypeStruct((T, Q_RANK), F32), jax.ShapeDtypeStruct((T, Q_RANK), F32),
                   jax.ShapeDtypeStruct((T, ATT_W + ROPE_W), BF16), jax.ShapeDtypeStruct((T, 2 * ATT_W), BF16),
                   jax.ShapeDtypeStruct((1, Q_RANK), F32), jax.ShapeDtypeStruct((1, KV_RANK), F32)],
        compiler_params=_params(("arbitrary",)),
    )(p, p, cos, sin, g_cq, g_ckv, w_uq_p, w_ukv_p, dqn, dqr4, dkn, dvm, dkrt4)


def _iota2(shape, axis):
    return lax.broadcasted_iota(jnp.int32, shape, axis)


def _head_masks():
    lane = _iota2((1, LANES), 1)
    return lane < HEAD_DIM, lane >= HEAD_DIM


def _pair(x, masks, dtype=BF16):
    return [jnp.where(m, x, 0.0).astype(dtype) for m in masks]


def _log_gates(z):
    keep = jnp.maximum(z, 0.0) + jnp.log2(1.0 + jnp.exp2(-jnp.abs(z)))
    return z - keep, keep


def _last_row(x):
    return _row_of(x[x.shape[0] - 8:, :], 7)


def _lane_selector(group):
    return jnp.where(_iota2((16, LANES), 1) // group == _iota2((16, LANES), 0), 1.0, 0.0).astype(BF16)


def _rows8(sel_t, x):
    hi = x.astype(BF16)
    r1 = x - hi.astype(F32)
    mid = r1.astype(BF16)
    lo = (r1 - mid.astype(F32)).astype(BF16)
    return _nt(sel_t, hi) + _nt(sel_t, mid) + _nt(sel_t, lo)


def _row_of(x8, j):
    return jnp.sum(jnp.where(_iota2(x8.shape, 0) == j, x8, 0.0), axis=0, keepdims=True)


def sb_fwd_call(p, B, S, ex=None):
    T = B * S
    TQ, TK = ATT_TQ, ATT_TK
    nq = S // TQ

    def body(q_ref, k_ref, v_ref, o_ref, lt_ref):
        qi = pl.program_id(2)
        masks = _head_masks()
        qm = [_pair(q_ref[:, sl] * SB_SCALE2, masks) for sl in PAIR_LANES]
        row = _iota2((TQ, TK), 0)
        col = _iota2((TQ, TK), 1)
        tri = jnp.where(row > col, 1.0, 0.0).astype(BF16)
        tri2 = jnp.concatenate([tri, tri], axis=0)
        vis = col < row
        o_ref[...] = jnp.zeros_like(o_ref)

        def group(k0, pairs, carry, diag):
            heads = [(pp, j) for pp in pairs for j in range(2)]
            n = range(len(heads))
            k = {pp: k_ref[pl.ds(k0, TK), PAIR_LANES[pp]].astype(BF16) for pp in pairs}
            vm = {pp: _pair(v_ref[pl.ds(k0, TK), PAIR_LANES[pp]], masks) for pp in pairs}
            z = [_nt(qm[pp][j], k[pp]) for pp, j in heads]
            if diag:
                z = [jnp.where(vis, x, NEG_BIG) for x in z]
            gates = [_log_gates(x) for x in z]
            lb = [g[0] for g in gates]
            keep = [g[1] for g in gates]
            tail = [_nn(jnp.concatenate(_split2(keep[h]), axis=1), tri2) + carry[h] for h in n]
            a = [jnp.exp2(lb[h] - tail[h]) for h in n]
            ab = [x.astype(BF16) for x in a]
            for i, pp in enumerate(pairs):
                o_ref[:, PAIR_LANES[pp]] += _nn(ab[2 * i], vm[pp][0]) + _nn(ab[2 * i + 1], vm[pp][1])
            return [tail[h][:, 0:1] + keep[h][:, 0:1] for h in n]

        def step(kb, carry, diag):
            k0 = pl.multiple_of(kb * TK, TK)
            out = []
            for g in range(0, ATT_PAIRS, SB_FWD_GROUP):
                out += group(k0, list(range(g, g + SB_FWD_GROUP)), carry[2 * g:2 * (g + SB_FWD_GROUP)], diag)
            return tuple(out)

        zero = jnp.zeros((TQ, 1), F32)
        carry = step(qi, (zero,) * (2 * ATT_PAIRS), True)
        carry = lax.fori_loop(0, qi, lambda i, c: step(qi - 1 - i, c, False), carry)
        lane = _iota2((TQ, LANES), 1)
        for pp, sl in enumerate(PAIR_LANES):
            lt_ref[:, sl] = jnp.where(lane == 0, carry[2 * pp], jnp.where(lane == 1, carry[2 * pp + 1], 0.0))

    W = ATT_PAIRS * LANES
    qspec = pl.BlockSpec((TQ, W), lambda b, h, i: (b * nq + i, h))
    outs, moved = _call(
        body, ex, name="sb_fwd", grid=(B, HEADS // 2 // ATT_PAIRS, nq),
        in_specs=[qspec,
                  pl.BlockSpec((S, W), lambda b, h, i: (b, ATT_W // W + h)),
                  pl.BlockSpec((S, W), lambda b, h, i: (b, 2 * ATT_W // W + h))],
        out_specs=[qspec, qspec],
        out_shape=[jax.ShapeDtypeStruct((T, ATT_W), F32)] * 2, args=(p, p, p))
    return tuple(outs) if ex is None else tuple(outs) + (moved,)


def sb_bwd_call(p, lt, do, B, S, ex=None):
    T = B * S
    TQ, TK = ATT_TQ, ATT_TK
    nq = S // TQ

    def body(q_ref, k_ref, v_ref, lt_ref, do_ref, dq_ref, dk_ref, dv_ref):
        qi = pl.program_id(2)

        @pl.when(qi == 0)
        def _():
            dk_ref[...] = jnp.zeros_like(dk_ref)
            dv_ref[...] = jnp.zeros_like(dv_ref)

        masks = _head_masks()
        qm = [_pair(q_ref[:, sl] * SB_SCALE2, masks) for sl in PAIR_LANES]
        dom = [_pair(do_ref[:, sl], masks) for sl in PAIR_LANES]
        start = []
        for sl in PAIR_LANES:
            l8 = _rows8(_lane_selector(1), lt_ref[:, sl])
            start += [-_row_of(l8, 0), jnp.zeros((1, TQ), F32), -_row_of(l8, 1), jnp.zeros((1, TQ), F32)]
        row = _iota2((TK, TQ), 0)
        col = _iota2((TK, TQ), 1)
        incl = jnp.where(col <= row, 1.0, 0.0).astype(BF16)
        incl2 = jnp.concatenate([incl, incl], axis=1)
        excl = jnp.where(col < row, 1.0, 0.0).astype(BF16)
        vis = row < col
        dq_ref[...] = jnp.zeros_like(dq_ref)

        def group(k0, pairs, carry, diag):
            heads = [(pp, j) for pp in pairs for j in range(2)]
            n = range(len(heads))
            kf = {pp: k_ref[pl.ds(k0, TK), PAIR_LANES[pp]] for pp in pairs}
            km = {pp: _pair(kf[pp], masks) for pp in pairs}
            v = {pp: v_ref[pl.ds(k0, TK), PAIR_LANES[pp]].astype(BF16) for pp in pairs}
            z = [_nt(kf[pp].astype(BF16), qm[pp][j]) for pp, j in heads]
            da = [_nt(v[pp], dom[pp][j]) for pp, j in heads]
            if diag:
                z = [jnp.where(vis, x, NEG_BIG) for x in z]
            gates = [_log_gates(x) for x in z]
            lb = [g[0] for g in gates]
            keep = [g[1] for g in gates]
            left = [_nn(incl2, jnp.concatenate(_split2(keep[h]), axis=0)) + carry[2 * h] for h in n]
            a = [jnp.exp2(lb[h] + left[h]) for h in n]
            e = [a[h] * da[h] for h in n]
            before = [_nn(excl, e[h].astype(BF16)) + carry[2 * h + 1] for h in n]
            dz = [e[h] - jnp.exp2(lb[h]) * (e[h] + before[h]) for h in n]
            dzb = [x.astype(BF16) for x in dz]
            ab = [x.astype(BF16) for x in a]
            out = []
            for h in n:
                out += [_last_row(left[h]), _last_row(before[h]) + _last_row(e[h])]
            for i, pp in enumerate(pairs):
                sl = PAIR_LANES[pp]
                dk_ref[pl.ds(k0, TK), sl] += _nn(dzb[2 * i], qm[pp][0]) + _nn(dzb[2 * i + 1], qm[pp][1])
                dv_ref[pl.ds(k0, TK), sl] += _nn(ab[2 * i], dom[pp][0]) + _nn(ab[2 * i + 1], dom[pp][1])
                dq_ref[:, sl] += _tn(dzb[2 * i], km[pp][0]) + _tn(dzb[2 * i + 1], km[pp][1])
            return out

        def step(kb, carry, diag):
            k0 = pl.multiple_of(kb * TK, TK)
            out = []
            for g in range(0, ATT_PAIRS, SB_BWD_GROUP):
                out += group(k0, list(range(g, g + SB_BWD_GROUP)), carry[4 * g:4 * (g + SB_BWD_GROUP)], diag)
            return tuple(out)

        carry = lax.fori_loop(0, qi, lambda i, c: step(i, c, False), tuple(start))
        step(qi, carry, True)
        dq_ref[...] *= SB_SCALE

        @pl.when(qi == nq - 1)
        def _():
            dk_ref[...] *= LN2

    W = ATT_PAIRS * LANES
    qspec = pl.BlockSpec((TQ, W), lambda b, h, i: (b * nq + i, h))
    sspec = pl.BlockSpec((S, W), lambda b, h, i: (b, h))
    outs, moved = _call(
        body, ex, name="sb_bwd", grid=(B, HEADS // 2 // ATT_PAIRS, nq),
        in_specs=[qspec,
                  pl.BlockSpec((S, W), lambda b, h, i: (b, ATT_W // W + h)),
                  pl.BlockSpec((S, W), lambda b, h, i: (b, 2 * ATT_W // W + h)),
                  qspec, qspec],
        out_specs=[qspec, sspec, sspec],
        out_shape=[jax.ShapeDtypeStruct((T, ATT_W), F32)] * 3, args=(p, p, p, lt, do))
    return tuple(outs) if ex is None else tuple(outs) + (moved,)


ALL_PAIRS = [slice(i * LANES, (i + 1) * LANES) for i in range(HEADS // 2)]


def _rope_masks(hp):
    grp = _iota2((1, LANES), 1) // ROPE_DIM
    return [grp == ((2 * hp + j) % 4) for j in range(2)]


def _mla_queries(qn_ref, qr_ref, masks):
    out = []
    for pp, sl in enumerate(ALL_PAIRS):
        qnv = qn_ref[:, sl]
        qrv = qr_ref[:, ALL_PAIRS[pp // 2]]
        rmasks = _rope_masks(pp)
        out.append([jnp.concatenate([jnp.where(masks[j], qnv, 0), jnp.where(rmasks[j], qrv, 0)], axis=1).astype(BF16)
                    for j in range(2)])
    return out


def mla_fwd_call(qn, qr, kn, krt, vm, B, S):
    T = B * S
    TQ, TK = ATT_TQ, ATT_TK
    nq = S // TQ

    def body(qn_ref, qr_ref, kn_ref, kr_ref, v_ref, o_ref, lse_ref):
        qi = pl.program_id(1)
        masks = _head_masks()
        qcat = _mla_queries(qn_ref, qr_ref, masks)
        row = _iota2((TQ, TK), 0)
        col = _iota2((TQ, TK), 1)
        vis = col <= row
        o_ref[...] = jnp.zeros_like(o_ref)

        def group(k0, pairs, carry, diag):
            heads = [(pp, j) for pp in pairs for j in range(2)]
            n = range(len(heads))
            krv = kr_ref[pl.ds(k0, TK), :]
            kcat = {pp: jnp.concatenate([kn_ref[pl.ds(k0, TK), ALL_PAIRS[pp]], krv], axis=1) for pp in pairs}
            vmk = {pp: _pair(v_ref[pl.ds(k0, TK), ALL_PAIRS[pp]], masks) for pp in pairs}
            s = [_nt(qcat[pp][j], kcat[pp]) * MLA_SCALE2 for pp, j in heads]
            if diag:
                s = [jnp.where(vis, x, NEG_BIG) for x in s]
            m_new = [jnp.maximum(carry[2 * h], jnp.max(s[h], axis=1, keepdims=True)) for h in n]
            alpha = [jnp.exp2(carry[2 * h] - m_new[h]) for h in n]
            pexp = [jnp.exp2(s[h] - m_new[h]) for h in n]
            out = []
            for h in n:
                out += [m_new[h], alpha[h] * carry[2 * h + 1] + jnp.sum(pexp[h], axis=1, keepdims=True)]
            pb = [x.astype(BF16) for x in pexp]
            for i, pp in enumerate(pairs):
                sl = ALL_PAIRS[pp]
                scale = jnp.where(masks[0], alpha[2 * i], alpha[2 * i + 1])
                o_ref[:, sl] = o_ref[:, sl] * scale + (_nn(pb[2 * i], vmk[pp][0]) + _nn(pb[2 * i + 1], vmk[pp][1]))
            return out

        def step(kb, carry, diag):
            k0 = pl.multiple_of(kb * TK, TK)
            out = []
            for g in range(0, len(ALL_PAIRS), MLA_GROUP):
                out += group(k0, list(range(g, g + MLA_GROUP)), carry[4 * g:4 * (g + MLA_GROUP)], diag)
            return tuple(out)

        neg = jnp.full((TQ, 1), NEG_BIG, F32)
        zero = jnp.zeros((TQ, 1), F32)
        carry = step(qi, (neg, zero) * (2 * len(ALL_PAIRS)), True)
        carry = lax.fori_loop(0, qi, lambda i, c: step(qi - 1 - i, c, False), carry)
        lane = _iota2((TQ, LANES), 1)
        for pp, sl in enumerate(ALL_PAIRS):
            m0, l0, m1, l1 = carry[4 * pp:4 * pp + 4]
            o_ref[:, sl] = o_ref[:, sl] * jnp.where(masks[0], 1.0 / l0, 1.0 / l1)
            lse_ref[:, sl] = jnp.where(lane == 0, m0 * LN2 + jnp.log(l0), jnp.where(lane == 1, m1 * LN2 + jnp.log(l1), 0.0))

    def rows(w):
        return pl.BlockSpec((TQ, w), lambda b, i: (b * nq + i, 0))

    def seq(w):
        return pl.BlockSpec((S, w), lambda b, i: (b, 0))

    return pl.pallas_call(
        body, name="mla_fwd", grid=(B, nq),
        in_specs=[rows(ATT_W), rows(ROPE_W), seq(ATT_W), seq(LANES), seq(ATT_W)],
        out_specs=[rows(ATT_W), rows(ATT_W)],
        out_shape=[jax.ShapeDtypeStruct((T, ATT_W), F32)] * 2,
        compiler_params=_params(("arbitrary", "arbitrary")),
    )(qn, qr, kn, krt, vm)


def mla_bwd_call(qn, qr, kn, krt, vm, o, lse, do, B, S, ex=None):
    T = B * S
    TQ, TK = ATT_TQ, ATT_TK
    nq = S // TQ

    def body(qn_ref, qr_ref, kn_ref, kr_ref, v_ref, o_ref, lse_ref, do_ref,
             dqn_ref, dqr_ref, dkn_ref, dv_ref, dkr_ref):
        qi = pl.program_id(1)

        @pl.when(qi == 0)
        def _():
            dkn_ref[...] = jnp.zeros_like(dkn_ref)
            dv_ref[...] = jnp.zeros_like(dv_ref)
            dkr_ref[...] = jnp.zeros_like(dkr_ref)

        masks = _head_masks()
        qcat = _mla_queries(qn_ref, qr_ref, masks)
        dom, dsum, lse = [], [], []
        for sl in ALL_PAIRS:
            do = do_ref[:, sl]
            dom.append(_pair(do, masks))
            d8 = _rows8(_lane_selector(HEAD_DIM), do * o_ref[:, sl])
            l8 = _rows8(_lane_selector(1), lse_ref[:, sl])
            dsum.append([_row_of(d8, j) for j in range(2)])
            lse.append([_row_of(l8, j) * LOG2E for j in range(2)])
        row = _iota2((TK, TQ), 0)
        col = _iota2((TK, TQ), 1)
        vis = row <= col
        dqn_ref[...] = jnp.zeros_like(dqn_ref)
        dqr_ref[...] = jnp.zeros_like(dqr_ref)

        def group(k0, pairs, diag):
            heads = [(pp, j) for pp in pairs for j in range(2)]
            n = range(len(heads))
            krv = kr_ref[pl.ds(k0, TK), :]
            knv = {pp: kn_ref[pl.ds(k0, TK), ALL_PAIRS[pp]] for pp in pairs}
            kcat = {pp: jnp.concatenate([knv[pp], krv], axis=1) for pp in pairs}
            v = {pp: v_ref[pl.ds(k0, TK), ALL_PAIRS[pp]] for pp in pairs}
            s = [_nt(kcat[pp], qcat[pp][j]) * MLA_SCALE2 for pp, j in heads]
            dp_ = [_nt(v[pp], dom[pp][j]) for pp, j in heads]
            pr = [jnp.exp2(s[h] - lse[pp][j]) for h, (pp, j) in enumerate(heads)]
            if diag:
                pr = [jnp.where(vis, x, 0.0) for x in pr]
            ds = [(pr[h] * (dp_[h] - dsum[pp][j]) * MLA_SCALE).astype(BF16) for h, (pp, j) in enumerate(heads)]
            pb = [x.astype(BF16) for x in pr]
            for i, pp in enumerate(pairs):
                sl = ALL_PAIRS[pp]
                rmasks = _rope_masks(pp)
                kcat_j = [jnp.concatenate([jnp.where(masks[j], knv[pp], 0), jnp.where(rmasks[j], krv, 0)],
                                          axis=1).astype(BF16) for j in range(2)]
                dv_ref[pl.ds(k0, TK), sl] += _nn(pb[2 * i], dom[pp][0]) + _nn(pb[2 * i + 1], dom[pp][1])
                dk = _nn(ds[2 * i], qcat[pp][0]) + _nn(ds[2 * i + 1], qcat[pp][1])
                dq = _tn(ds[2 * i], kcat_j[0]) + _tn(ds[2 * i + 1], kcat_j[1])
                dqn_ref[:, sl] += dq[:, :LANES]
                dqr_ref[:, sl] += dq[:, LANES:]
                dkn_ref[pl.ds(k0, TK), sl] += dk[:, :LANES]
                dkr_ref[pl.ds(k0, TK), sl] += dk[:, LANES:]

        def step(kb, diag):
            k0 = pl.multiple_of(kb * TK, TK)
            for g in range(0, len(ALL_PAIRS), MLA_GROUP):
                group(k0, list(range(g, g + MLA_GROUP)), diag)

        step(qi, True)

        def loop(i, c):
            step(qi - 1 - i, False)
            return c

        lax.fori_loop(0, qi, loop, 0)

    def rows(w):
        return pl.BlockSpec((TQ, w), lambda b, i: (b * nq + i, 0))

    def seq(w):
        return pl.BlockSpec((S, w), lambda b, i: (b, 0))

    outs, moved = _call(
        body, ex, name="mla_bwd", grid=(B, nq),
        in_specs=[rows(ATT_W), rows(ROPE_W), seq(ATT_W), seq(LANES), seq(ATT_W), rows(ATT_W), rows(ATT_W), rows(ATT_W)],
        out_specs=[rows(ATT_W), rows(ATT_W), seq(ATT_W), seq(ATT_W), seq(ATT_W)],
        out_shape=[jax.ShapeDtypeStruct((T, ATT_W), F32)] * 5, args=(qn, qr, kn, krt, vm, o, lse, do))
    return tuple(outs) if ex is None else tuple(outs) + (moved,)


CONV_TC = 256


def _shift_down(x, n):
    return jnp.where(_iota2(x.shape, 0) >= n, pltpu.roll(x, n, 0), 0.0)


def _shift_up(x, n):
    rows = x.shape[0]
    return jnp.where(_iota2(x.shape, 0) < rows - n, pltpu.roll(x, rows - n, 0), 0.0)


def _taps(w_ref):
    return [w_ref[k:k + 1, :] for k in range(3)]


def _conv3(u, w, b):
    return w[0] * _shift_down(u, 2) + w[1] * _shift_down(u, 1) + w[2] * u + b


def _ref_shift_down(ref, n):
    rows = ref.shape[0]
    return jnp.concatenate([_shift_down(ref[0:8, :], n), ref[8 - n:rows - n, :]], axis=0)


def _conv3_ref(u_ref, w, b):
    return w[0] * _ref_shift_down(u_ref, 2) + w[1] * _ref_shift_down(u_ref, 1) + w[2] * u_ref[...] + b


def conv_act_fwd_call(ug, uv, conv_w, conv_b, B, S):
    T = B * S
    nc = D_FF // CONV_TC

    def body(ug_ref, uv_ref, wg_ref, wv_ref, bg_ref, bv_ref, a_ref, cg_ref, cv_ref):
        gate = _conv3_ref(ug_ref, _taps(wg_ref), bg_ref[...])
        val = _conv3_ref(uv_ref, _taps(wv_ref), bv_ref[...])
        a_ref[...] = (gate * (1.0 / (1.0 + jnp.exp(-gate))) * val).astype(BF16)
        cg_ref[...] = gate.astype(BF16)
        cv_ref[...] = val.astype(BF16)

    def blk(rows, off):
        return pl.BlockSpec((rows, CONV_TC), lambda b, j: (b if rows == S else 0, off + j))

    return pl.pallas_call(
        body, name="conv_act_fwd", grid=(B, nc),
        in_specs=[blk(S, 0), blk(S, 0), blk(3, 0), blk(3, nc), blk(1, 0), blk(1, nc)],
        out_specs=[blk(S, 0)] * 3,
        out_shape=[jax.ShapeDtypeStruct((T, D_FF), BF16)] * 3,
        compiler_params=_params(("parallel", "parallel")),
    )(ug, uv, conv_w, conv_w, conv_b, conv_b)


def conv_act_bwd_call(ug, uv, cg, cv, dx2, w_down, conv_w, B, S):
    T = B * S
    nc = D_FF // CONV_TC

    def body(ug_ref, uv_ref, cg_ref, cv_ref, dx_ref, wd_ref, wg_ref, wv_ref,
             dug_ref, duv_ref, dwg_ref, dwv_ref, dbg_ref, dbv_ref):
        @pl.when(pl.program_id(1) == 0)
        def _():
            for r in (dwg_ref, dwv_ref, dbg_ref, dbv_ref):
                r[...] = jnp.zeros_like(r)

        gate = cg_ref[...].astype(F32)
        val = cv_ref[...].astype(F32)
        da = _nt(dx_ref[...], wd_ref[...])
        sig = 1.0 / (1.0 + jnp.exp(-gate))
        dval = da * (gate * sig)
        dgate = da * val * (sig * (1.0 + gate * (1.0 - sig)))
        for u_ref, d, w, du_ref, dw_ref, db_ref in ((ug_ref, dgate, _taps(wg_ref), dug_ref, dwg_ref, dbg_ref),
                                                   (uv_ref, dval, _taps(wv_ref), duv_ref, dwv_ref, dbv_ref)):
            u_ = u_ref[...]
            d1 = _shift_up(d, 1)
            d2 = _shift_up(d, 2)
            du_ref[...] = (w[2] * d + w[1] * d1 + w[0] * d2).astype(BF16)
            db_ref[...] += jnp.sum(d, axis=0, keepdims=True)
            dw_ref[0:1, :] += jnp.sum(d2 * u_, axis=0, keepdims=True)
            dw_ref[1:2, :] += jnp.sum(d1 * u_, axis=0, keepdims=True)
            dw_ref[2:3, :] += jnp.sum(d * u_, axis=0, keepdims=True)

    def blk(rows, off):
        return pl.BlockSpec((rows, CONV_TC), lambda j, b: (b if rows == S else 0, off + j))

    return pl.pallas_call(
        body, name="conv_act_bwd", grid=(nc, B),
        in_specs=[blk(S, 0), blk(S, 0), blk(S, 0), blk(S, 0), pl.BlockSpec((S, D_MODEL), lambda j, b: (b, 0)),
                  pl.BlockSpec((CONV_TC, D_MODEL), lambda j, b: (j, 0)), blk(3, 0), blk(3, nc)],
        out_specs=[blk(S, 0), blk(S, 0), blk(3, 0), blk(3, 0), blk(1, 0), blk(1, 0)],
        out_shape=[jax.ShapeDtypeStruct((T, D_FF), BF16), jax.ShapeDtypeStruct((T, D_FF), BF16),
                   jax.ShapeDtypeStruct((3, D_FF), F32), jax.ShapeDtypeStruct((3, D_FF), F32),
                   jax.ShapeDtypeStruct((1, D_FF), F32), jax.ShapeDtypeStruct((1, D_FF), F32)],
        compiler_params=_params(("parallel", "arbitrary")),
    )(ug, uv, cg, cv, dx2, w_down, conv_w, conv_w)


CHIP_MASKS = ((1, 0), (0, 1), (1, 1))


def _place():
    return lax.axis_index("x"), lax.axis_index("y"), lax.axis_index("c")


HALF_ALIGN = 32


def _any_specs(n):
    return [pl.BlockSpec(memory_space=pl.ANY)] * n


def _splits(shape):
    r, c = shape
    return "rows" if r % HALF_ALIGN == 0 else "cols" if c % (2 * LANES) == 0 else None


def _half(shape, half):
    r, c = shape
    how = _splits(shape)
    if how == "rows":
        return (pl.ds(pl.multiple_of(half * (r // 2), HALF_ALIGN // 2), r // 2), slice(None))
    if how == "cols":
        return (slice(None), pl.ds(pl.multiple_of(half * (c // 2), LANES), c // 2))
    return (slice(None), slice(None))


def _half_shape(shape):
    r, c = shape
    return {"rows": (r // 2, c), "cols": (r, c // 2)}[_splits(shape)]


def _remote(src, dst, send_sem, recv_sem, device):
    return pltpu.make_async_remote_copy(src_ref=src, dst_ref=dst, send_sem=send_sem, recv_sem=recv_sem,
                                        device_id=device, device_id_type=MESH)


class Exchange:
    def __init__(self, ins, out_shape, sems, start, finish):
        self.ins, self.out_shape, self.sems, self.start, self.finish = list(ins), list(out_shape), list(sems), start, finish


def gather_group(shards):
    n = len(shards)
    split = [_splits(s.shape) is not None for s in shards]

    def part(w, half):
        return _half(shards[w].shape, half)

    def copies(ins, outs, sems):
        ici_s, ici_r, _, _, local_sems = sems
        x, y, c = _place()
        chip = 2 * x + y
        local = [pltpu.make_async_copy(ins[w], outs[w].at[chip], local_sems.at[w]) for w in range(n)]
        sends = [_remote(ins[w].at[part(w, c)], outs[w].at[(chip,) + part(w, c)], ici_s.at[w, k], ici_r.at[w, k],
                         (x ^ fx, y ^ fy, c))
                 for w in range(n) for k, (fx, fy) in enumerate(CHIP_MASKS)]
        return local, sends

    def start(ins, outs, sems):
        local, sends = copies(ins, outs, sems)
        for cp in local + sends:
            cp.start()

    def finish(ins, outs, sems):
        ici_s, ici_r, d2d_s, d2d_r, _ = sems
        x, y, c = _place()
        sib = (x, y, 1 - c)
        local, sends = copies(ins, outs, sems)
        for w in range(n):
            for k, (fx, fy) in enumerate(CHIP_MASKS):
                landed = outs[w].at[(2 * (x ^ fx) + (y ^ fy),) + part(w, c)]
                _remote(landed, landed, ici_s.at[w, k], ici_r.at[w, k], sib).wait_recv()
                if split[w]:
                    cp = _remote(landed, landed, d2d_s.at[w, k], d2d_r.at[w, k], sib)
                    cp.start()
                    sends.append(cp)
        for w in range(n):
            for k, (fx, fy) in enumerate(CHIP_MASKS):
                if split[w]:
                    other = outs[w].at[(2 * (x ^ fx) + (y ^ fy),) + part(w, 1 - c)]
                    _remote(other, other, d2d_s.at[w, k], d2d_r.at[w, k], sib).wait_recv()
        for cp in sends:
            cp.wait_send()
        for cp in local:
            cp.wait()

    sems = pltpu.SemaphoreType.DMA((n, 3))
    return Exchange(shards, [jax.ShapeDtypeStruct((N_CHIPS,) + s.shape, s.dtype) for s in shards],
                    [sems, sems, sems, sems, pltpu.SemaphoreType.DMA((n,))], start, finish)


def swap_half(parts):
    n = len(parts)

    def copies(ins, outs, sems):
        x, y, c = _place()
        return [_remote(ins[w].at[(slice(None),) + _half(parts[w].shape[1:], 1 - c)], outs[w], sems[0].at[w], sems[1].at[w],
                        (x, y, 1 - c)) for w in range(n)]

    def start(ins, outs, sems):
        for cp in copies(ins, outs, sems):
            cp.start()

    def finish(ins, outs, sems):
        for cp in copies(ins, outs, sems):
            cp.wait_recv()
            cp.wait_send()

    return Exchange(parts, [jax.ShapeDtypeStruct((N_CHIPS,) + _half_shape(p.shape[1:]), F32) for p in parts],
                    [pltpu.SemaphoreType.DMA((n,))] * 2, start, finish)


def scatter_half(halves):
    n = len(halves)

    def copies(ins, outs, sems):
        x, y, c = _place()
        return [_remote(ins[w].at[2 * (x ^ fx) + (y ^ fy)], outs[w].at[k], sems[0].at[w, k], sems[1].at[w, k],
                        (x ^ fx, y ^ fy, c))
                for w in range(n) for k, (fx, fy) in enumerate(CHIP_MASKS)]

    def start(ins, outs, sems):
        for cp in copies(ins, outs, sems):
            cp.start()

    def finish(ins, outs, sems):
        for cp in copies(ins, outs, sems):
            cp.wait_recv()
            cp.wait_send()

    return Exchange(halves, [jax.ShapeDtypeStruct((3,) + h.shape[1:], h.dtype) for h in halves],
                    [pltpu.SemaphoreType.DMA((n, 3))] * 2, start, finish)


def swap_final(finals, shapes):
    n = len(finals)

    def copies(ins, outs, sems):
        x, y, c = _place()
        mine = [outs[w].at[_half(shapes[w], c)] for w in range(n)]
        local = [pltpu.make_async_copy(ins[w], mine[w], sems[2].at[w]) for w in range(n)]
        sends = [_remote(ins[w], mine[w], sems[0].at[w], sems[1].at[w], (x, y, 1 - c)) for w in range(n)]
        return local, sends

    def start(ins, outs, sems):
        local, sends = copies(ins, outs, sems)
        for cp in local + sends:
            cp.start()

    def finish(ins, outs, sems):
        x, y, c = _place()
        local, sends = copies(ins, outs, sems)
        for w in range(n):
            got = outs[w].at[_half(shapes[w], 1 - c)]
            _remote(got, got, sems[0].at[w], sems[1].at[w], (x, y, 1 - c)).wait_recv()
        for cp in sends:
            cp.wait_send()
        for cp in local:
            cp.wait()

    return Exchange(finals, [jax.ShapeDtypeStruct(tuple(s), F32) for s in shapes],
                    [pltpu.SemaphoreType.DMA((n,))] * 3, start, finish)


def exchange_call(name, ex):
    n, m = len(ex.ins), len(ex.out_shape)

    def body(*refs):
        ins, outs, sems = refs[:n], refs[n:n + m], refs[n + m:]
        ex.start(ins, outs, sems)
        ex.finish(ins, outs, sems)

    return pl.pallas_call(body, name=name, in_specs=_any_specs(n), out_specs=_any_specs(m), out_shape=ex.out_shape,
                          scratch_shapes=ex.sems, compiler_params=_params())(*ex.ins)


def _call(body, ex, *, name, grid, in_specs, out_specs, out_shape, args, scratch_shapes=()):
    sem = ("arbitrary",) * len(grid)
    if ex is None:
        outs = pl.pallas_call(body, name=name, grid=grid, in_specs=in_specs, out_specs=out_specs, out_shape=out_shape,
                              scratch_shapes=list(scratch_shapes), compiler_params=_params(sem))(*args)
        return outs, None
    ni, no, ns = len(in_specs), len(out_specs), len(scratch_shapes)
    ne, me = len(ex.ins), len(ex.out_shape)

    def wrapped(*refs):
        own_in, ex_in = refs[:ni], refs[ni:ni + ne]
        own_out, ex_out = refs[ni + ne:ni + ne + no], refs[ni + ne + no:ni + ne + no + me]
        own_scr, ex_sems = refs[ni + ne + no + me:ni + ne + no + me + ns], refs[ni + ne + no + me + ns:]
        ids = [pl.program_id(a) for a in range(len(grid))]
        first = functools.reduce(jnp.logical_and, [i == 0 for i in ids])
        last = functools.reduce(jnp.logical_and, [i == g - 1 for i, g in zip(ids, grid)])

        @pl.when(first)
        def _():
            ex.start(ex_in, ex_out, ex_sems)

        body(*own_in, *own_out, *own_scr)

        @pl.when(last)
        def _():
            ex.finish(ex_in, ex_out, ex_sems)

    outs = pl.pallas_call(
        wrapped, name=name, grid=grid, in_specs=list(in_specs) + _any_specs(ne),
        out_specs=list(out_specs) + _any_specs(me), out_shape=list(out_shape) + ex.out_shape,
        scratch_shapes=list(scratch_shapes) + ex.sems, compiler_params=_params(sem))(*args, *ex.ins)
    return outs[:no], outs[no:]


def _row_tile(rows, cap, mult=8):
    return max([t for t in range(mult, min(rows, cap) + 1, mult) if rows % t == 0] or [rows])


def add_half_call(name, part, got, where):
    _, rh, cols = got.shape
    tr = _row_tile(rh, 176, 16)
    nb = rh // tr
    by_rows = _splits(part.shape[1:]) == "rows"

    def body(where_ref, p_ref, g_ref, own_ref, send_ref):
        t = p_ref[...] + g_ref[...]
        send_ref[...] = t.astype(BF16)
        chip = where_ref[1]
        own_ref[...] = p_ref[chip] + g_ref[chip]

    blk = (N_CHIPS, tr, cols)
    return pl.pallas_call(
        body, name=name,
        grid_spec=pltpu.PrefetchScalarGridSpec(
            num_scalar_prefetch=1, grid=(nb,),
            in_specs=[pl.BlockSpec(blk, (lambda i, where_ref: (0, where_ref[0] * nb + i, 0)) if by_rows
                                   else (lambda i, where_ref: (0, i, where_ref[0]))),
                      pl.BlockSpec(blk, lambda i, where_ref: (0, i, 0))],
            out_specs=[pl.BlockSpec((tr, cols), lambda i, where_ref: (i, 0)),
                       pl.BlockSpec(blk, lambda i, where_ref: (0, i, 0))]),
        out_shape=[jax.ShapeDtypeStruct((rh, cols), F32), jax.ShapeDtypeStruct(got.shape, BF16)],
        compiler_params=_params(("parallel",)),
    )(where, part, got)


def sum_chips_call(name, own, got):
    _, rh, cols = got.shape
    tr = _row_tile(rh, 176, 16)

    def body(h_ref, g_ref, o_ref):
        o_ref[...] = ((h_ref[...] + g_ref[0].astype(F32)) + g_ref[1].astype(F32)) + g_ref[2].astype(F32)

    return pl.pallas_call(
        body, name=name, grid=(rh // tr,),
        in_specs=[pl.BlockSpec((tr, cols), lambda i: (i, 0)), pl.BlockSpec((3, tr, cols), lambda i: (0, i, 0))],
        out_specs=pl.BlockSpec((tr, cols), lambda i: (i, 0)),
        out_shape=jax.ShapeDtypeStruct((rh, cols), F32),
        compiler_params=_params(("parallel",)),
    )(own, got)


def _adamw(w, g, m, v):
    m = ADAM_B1 * m + (1.0 - ADAM_B1) * g
    v = ADAM_B2 * v + (1.0 - ADAM_B2) * (g * g)
    m_hat = m / (1.0 - ADAM_B1 ** ADAM_STEP)
    v_hat = v / (1.0 - ADAM_B2 ** ADAM_STEP)
    delta = -ADAM_LR * (m_hat / (jnp.sqrt(v_hat) + ADAM_EPS) + ADAM_WD * w)
    return delta, m, v


def adamw_call(name, g, w, m, v):
    r, cols = w.shape
    tr = r if r % 8 else _row_tile(r, 256)

    def body(g_ref, w_ref, m_ref, v_ref, go_ref, d_ref, nm_ref, nv_ref):
        g = g_ref[...]
        go_ref[...] = g
        d_ref[...], nm_ref[...], nv_ref[...] = _adamw(w_ref[...], g, m_ref[...], v_ref[...])

    spec = pl.BlockSpec((tr, cols), lambda i: (i, 0))
    return pl.pallas_call(
        body, name=name, grid=(r // tr,), in_specs=[spec] * 4, out_specs=[spec] * 4,
        out_shape=[jax.ShapeDtypeStruct((r, cols), F32)] * 4,
        compiler_params=_params(("parallel",)),
    )(g, w, m, v)


def allsum_small_call(v):
    R = v.shape[0]

    def body(v_ref, out_ref, buf, send_sems, recv_sems):
        x, y, c = _place()
        me = 4 * x + 2 * y + c
        buf[me] = v_ref[...]
        sends = []
        for k in range(1, N_DEV):
            fx, fy, fc = (k >> 2) & 1, (k >> 1) & 1, k & 1
            cp = pltpu.make_async_remote_copy(
                src_ref=v_ref, dst_ref=buf.at[me], send_sem=send_sems.at[k - 1], recv_sem=recv_sems.at[k - 1],
                device_id=(x ^ fx, y ^ fy, c ^ fc), device_id_type=MESH)
            cp.start()
            sends.append(cp)
        for k in range(1, N_DEV):
            pltpu.make_async_remote_copy(
                src_ref=v_ref, dst_ref=buf.at[me ^ k], send_sem=send_sems.at[k - 1], recv_sem=recv_sems.at[k - 1],
                device_id=(x, y, c), device_id_type=MESH).wait_recv()
        acc = buf[0]
        for d in range(1, N_DEV):
            acc = acc + buf[d]
        out_ref[...] = acc
        for cp in sends:
            cp.wait_send()

    vm = pl.BlockSpec(memory_space=pltpu.VMEM)
    return pl.pallas_call(
        body, name="allsum_small", in_specs=[vm], out_specs=vm,
        out_shape=jax.ShapeDtypeStruct((R, LANES), F32),
        scratch_shapes=[pltpu.VMEM((N_DEV, R, LANES), F32), pltpu.SemaphoreType.DMA((N_DEV - 1,)),
                        pltpu.SemaphoreType.DMA((N_DEV - 1,))],
        compiler_params=_params(),
    )(v)


def _slab(flat, mult):
    n = flat.shape[-1]
    rows = -(-n // (LANES * mult)) * mult
    flat = jnp.pad(flat, [(0, 0)] * (flat.ndim - 1) + [(0, rows * LANES - n)])
    return flat.reshape(flat.shape[:-1] + (rows, LANES))


def full_from_chips(blocks, by_col):
    _, r, c = blocks.shape
    return blocks.transpose(1, 0, 2).reshape(r, N_CHIPS * c) if by_col else blocks.reshape(N_CHIPS * r, c)


def chips_from_full(full, by_col):
    if by_col:
        r, c = full.shape[0], full.shape[1] // N_CHIPS
        return full.reshape(r, N_CHIPS, c).transpose(1, 0, 2)
    return full.reshape(N_CHIPS, full.shape[0] // N_CHIPS, full.shape[1])


SMALL_PACK = SMALL_W + ("loss", "conv_w")
SMALL_PACK_N = {**SMALL_N, "loss": 1, "conv_w": 3 * 2 * D_FF}


def pack_small(vals):
    zero = jnp.zeros((1,), F32)
    return _slab(jnp.concatenate([vals[n].reshape(-1) if n in vals else jnp.tile(zero, SMALL_PACK_N[n])
                                  for n in SMALL_PACK]), 8)


def unpack_small(slab, shapes):
    flat = slab.reshape(-1)
    out, off = {}, 0
    for n in SMALL_PACK:
        out[n] = flat[off:off + SMALL_PACK_N[n]].reshape(shapes[n])
        off += SMALL_PACK_N[n]
    return out


def _split_heads(w, a, b):
    r = w.shape[0]
    w3 = w.reshape(r, HEADS, a + b)
    return w3[:, :, :a].reshape(r, HEADS * a), w3[:, :, a:].reshape(r, HEADS * b)


def _merge_heads(wa, wb, a, b):
    r = wa.shape[0]
    return jnp.concatenate([wa.reshape(r, HEADS, a), wb.reshape(r, HEADS, b)], axis=2).reshape(r, HEADS * (a + b))


def kernel(x, positions, g_mix, w_in, g_cq, w_uq, g_ckv, w_ukv, g_sb_out, g_mla_out, w_out, g_ffn, w_up, conv_w, conv_b, w_down, g_final, loss_target, m_g_mix, m_w_in, m_g_cq, m_w_uq, m_g_ckv, m_w_ukv, m_g_sb_out, m_g_mla_out, m_w_out, m_g_ffn, m_w_up, m_conv_w, m_conv_b, m_w_down, m_g_final, v_g_mix, v_w_in, v_g_cq, v_w_uq, v_g_ckv, v_w_ukv, v_g_sb_out, v_g_mla_out, v_w_out, v_g_ffn, v_w_up, v_conv_w, v_conv_b, v_w_down, v_g_final):
    given = dict(locals())
    B, S, _ = x.shape
    T = B * S
    w_big = {n: given[n][0].T if n == "w_in" else given[n][0] for n in BIG_W}
    m_big = {n: given["m_" + n][0].T if n == "w_in" else given["m_" + n][0] for n in BIG_W}
    v_big = {n: given["v_" + n][0].T if n == "w_in" else given["v_" + n][0] for n in BIG_W}
    shard_shape = {n: w_big[n].shape for n in BIG_W}

    first = ("w_in", "w_uq", "w_ukv")
    later = ("w_out", "w_up", "w_down", "conv_w")
    x2d = x.reshape(T, D_MODEL)
    half = ROPE_DIM // 2
    inv_freq = 1.0 / (ROPE_BASE ** (jnp.arange(half, dtype=F32) * (2.0 / ROPE_DIM)))
    h, cos, sin, got_w = norm_mix_rope_call(
        x2d, g_mix, positions.reshape(T, 1), jnp.tile(inv_freq, LANES // half).reshape(1, LANES),
        gather_group([w_big[n].astype(BF16) for n in first]))
    full = {n: full_from_chips(g_, BIG_SHARD[n][2]) for n, g_ in zip(first, got_w) if n != "w_in"}
    gather_later = gather_group([w_big[n] if n == "conv_w" else w_big[n].astype(BF16) for n in later])
    w_in_t = jnp.pad(got_w[0].reshape(IN_COLS, D_MODEL), ((0, IN_COLS_PAD - IN_COLS), (0, 0)))
    w_uq_p = jnp.concatenate(_split_heads(full["w_uq"], HEAD_DIM, ROPE_DIM), axis=1)
    w_ukv_p = jnp.concatenate(_split_heads(full["w_ukv"], HEAD_DIM, HEAD_DIM), axis=1)

    p = matmul_call("proj_in", h, w_in_t, "nt", tn=IN_COLS_PAD // 2)
    qn, qr, kn, vm, krt, cqn, ckvn = mla_prep_fwd_call(p, cos, sin, g_cq, g_ckv, w_uq_p, w_ukv_p)
    o_sb, lt_sb, got_w = sb_fwd_call(p, B, S, ex=gather_later)
    w_up4 = got_w[1]
    full.update({n: full_from_chips(g_, BIG_SHARD[n][2]) for n, g_ in zip(later, got_w) if n != "w_up"})
    conv_w_full = full["conv_w"]
    o_mla, lse = mla_fwd_call(qn, qr, kn, krt, vm, B, S)
    o_cat, x1, hn = proj_out_norm_call(o_sb, o_mla, g_sb_out, g_mla_out, full["w_out"], x2d, g_ffn)
    u_g, u_v, act, c_g, c_v = ffn_up_conv_call(hn, w_up4, conv_w_full, conv_b, S)
    dx2, dx2b, loss_row, dg_final = ffn_down_loss_call(
        act, full["w_down"], x1, g_final.reshape(1, D_MODEL), loss_target.reshape(T, D_MODEL))

    xi, yi, ci = _place()
    chip = (2 * xi + yi).astype(jnp.int32).reshape(1)
    where = jnp.stack([ci, 2 * xi + yi]).astype(jnp.int32)

    def add_halves(names, parts, sib_rows):
        return [add_half_call("add_half_" + n, p_, s_, where) for n, p_, s_ in zip(names, parts, sib_rows)]

    def sum_chips(names, halves, from_chips):
        return [sum_chips_call("sum_chips_" + n, h_[0], f_) for n, h_, f_ in zip(names, halves, from_chips)]

    ffn_w = ("w_down", "w_up")
    parts_ffn = [chips_from_full(wgrad_call("wgrad_down", act, dx2b, tn=512, tt=1024), False)]
    du_g, du_v, dw_up4, dcw_g, dcw_v, dcb_g, dcb_v = conv_bwd_wgrad_up_call(
        hn, u_g, u_v, c_g, c_v, dx2b, full["w_down"], conv_w_full, S)
    parts_ffn.append(dw_up4)
    dx1, dg_ffn, sib_ffn = ffn_up_bwd_call(du_g, du_v, w_up4, x1, g_ffn, dx2, swap_half(parts_ffn))
    parts_out = [chips_from_full(wgrad_call("wgrad_out", o_cat, dx1, tt=1024), False)]
    do_sb, do_mla, dg_sb_out, dg_mla_out, sib_out = proj_out_bwd_call(
        dx1, full["w_out"], o_sb, o_mla, g_sb_out, g_mla_out, swap_half(parts_out))
    early = ffn_w + ("w_out",)
    halves = add_halves(early, parts_ffn + parts_out, list(sib_ffn) + list(sib_out))
    dq_sb, dk_sb, dv_sb, from_chips = sb_bwd_call(p, lt_sb, do_sb, B, S, ex=scatter_half([h_[1] for h_ in halves]))
    finals = sum_chips(early, halves, from_chips)
    dqn, dqr4, dkn, dvm, dkrt4, done = mla_bwd_call(qn, qr, kn, krt, vm, o_mla, lse, do_mla, B, S,
        ex=swap_final(finals, [shard_shape[n] for n in early]))
    grads = dict(zip(early, done))
    dcq, dckvr, dq_cat, dkv_cat, dg_cq, dg_ckv = mla_prep_bwd_call(
        p, cos, sin, g_cq, g_ckv, w_uq_p, w_ukv_p, dqn, dqr4, dkn, dvm, dkrt4)
    dw_uq_p = wgrad_call("wgrad_uq", cqn, dq_cat)
    dw_ukv_p = wgrad_call("wgrad_ukv", ckvn, dkv_cat)
    dp = (dq_sb, dk_sb, dv_sb, dcq, dckvr)
    late = ("w_uq", "w_ukv", "w_in")
    parts_late = [chips_from_full(g_, True) for g_ in (
        _merge_heads(dw_uq_p[:, :ATT_W], dw_uq_p[:, ATT_W:], HEAD_DIM, ROPE_DIM),
        _merge_heads(dw_ukv_p[:, :ATT_W], dw_ukv_p[:, ATT_W:], HEAD_DIM, HEAD_DIM))]
    parts_late.append(chips_from_full(wgrad_in_call(h, dp)[:IN_COLS], False))
    dh, sib_late = proj_in_bwd_call(dp, w_in_t, swap_half(parts_late))
    halves = add_halves(late, parts_late, sib_late)
    grad_x, dg_mix, from_chips = rmsnorm_bwd_call(
        "norm_mix_bwd", x2d, g_mix, dh, dx1, ex=scatter_half([h_[1] for h_ in halves]))
    finals = sum_chips(late, halves, from_chips)
    grads.update(zip(late, exchange_call("swap_final_late", swap_final(finals, [shard_shape[n] for n in late]))))

    shapes = {n: given[n].shape for n in SMALL_W}
    shapes.update(loss=(), conv_w=(3, 2 * D_FF))
    small_g = {"g_mix": dg_mix, "g_cq": dg_cq, "g_ckv": dg_ckv, "g_sb_out": dg_sb_out, "g_mla_out": dg_mla_out,
               "g_ffn": dg_ffn, "conv_b": jnp.concatenate([dcb_g, dcb_v], axis=1), "g_final": dg_final,
               "loss": loss_row[0, :1], "conv_w": jnp.concatenate([dcw_g, dcw_v], axis=1)}
    gs_slab = allsum_small_call(pack_small(small_g))
    small_in = [pack_small({n: given[pre + n] for n in SMALL_W}) for pre in ("", "m_", "v_")]
    small_out = [unpack_small(s, shapes) for s in adamw_call("adamw_small", gs_slab, *small_in)]
    cw_cols = BIG_SHARD["conv_w"][1]
    grads["conv_w"] = lax.dynamic_slice_in_dim(small_out[0]["conv_w"], chip[0] * cw_cols, cw_cols, axis=1)

    big_out = {n: adamw_call("adamw_" + n, grads[n], w_big[n], m_big[n], v_big[n]) for n in BIG_W}
    weights = ("g_mix", "w_in", "g_cq", "w_uq", "g_ckv", "w_ukv", "g_sb_out", "g_mla_out", "w_out", "g_ffn",
               "w_up", "conv_w", "conv_b", "w_down", "g_final")
    outs = [small_out[0]["loss"], grad_x.reshape(B, S, D_MODEL)]
    for k in range(4):
        for n in weights:
            if n in BIG_W:
                outs.append((big_out[n][k].T if n == "w_in" else big_out[n][k])[None])
            else:
                outs.append(small_out[k][n])
    return tuple(outs)
```

```python
import functools

import jax
import jax.numpy as jnp
from jax import lax
from jax.experimental import pallas as pl
from jax.experimental.pallas import tpu as pltpu

F32 = jnp.float32
BF16 = jnp.bfloat16
MESH = pl.DeviceIdType.MESH

D_MODEL = 1024
HEADS = 8
HEAD_DIM = 64
ATT_W = HEADS * HEAD_DIM
ROPE_DIM = 32
ROPE_W = HEADS * ROPE_DIM
QK_DIM = HEAD_DIM + ROPE_DIM
Q_RANK = 384
KV_RANK = 256
D_FF = 2816
IN_COLS = 2208
IN_COLS_PAD = 2304
EPS = 1e-6
ROPE_BASE = 10000.0
SB_SCALE = HEAD_DIM ** -0.5
SB_SCALE2 = SB_SCALE * 1.4426950408889634
MLA_SCALE = QK_DIM ** -0.5
LOG2E = 1.4426950408889634
LN2 = 0.6931471805599453
MLA_SCALE2 = MLA_SCALE * LOG2E
LANES = 128
N_CHIPS = 4
N_DEV = 8
VMEM_LIMIT = 48 * 1024 * 1024
ATT_TQ = 256
ATT_TK = 256
ATT_PAIRS = 4
PAIR_LANES = [slice(i * LANES, (i + 1) * LANES) for i in range(ATT_PAIRS)]
SB_BWD_GROUP = 2
SB_FWD_GROUP = 4
MLA_GROUP = 4
NEG_BIG = -1e30

ADAM_LR = 0.001
ADAM_B1 = 0.9
ADAM_B2 = 0.999
ADAM_EPS = 1e-08
ADAM_WD = 0.01
ADAM_STEP = 10

BIG_W = ("w_in", "w_uq", "w_ukv", "w_out", "w_up", "conv_w", "w_down")
BIG_SHARD = {
    "w_in": (D_MODEL, IN_COLS // 4, True),
    "w_uq": (Q_RANK, HEADS * QK_DIM // 4, True),
    "w_ukv": (KV_RANK, 2 * ATT_W // 4, True),
    "w_out": (2 * ATT_W // 4, D_MODEL, False),
    "w_up": (D_MODEL, 2 * D_FF // 4, True),
    "conv_w": (3, 2 * D_FF // 4, True),
    "w_down": (D_FF // 4, D_MODEL, False),
}
SMALL_W = ("g_mix", "g_cq", "g_ckv", "g_sb_out", "g_mla_out", "g_ffn", "conv_b", "g_final")
SMALL_N = {"g_mix": D_MODEL, "g_cq": Q_RANK, "g_ckv": KV_RANK, "g_sb_out": ATT_W, "g_mla_out": ATT_W,
           "g_ffn": D_MODEL, "conv_b": 2 * D_FF, "g_final": D_MODEL}


def _params(sem=None, **kw):
    return pltpu.CompilerParams(dimension_semantics=sem, vmem_limit_bytes=VMEM_LIMIT, **kw)


def _dot(a, b, dims):
    return lax.dot_general(a, b, (dims, ((), ())), preferred_element_type=F32)


def _nn(a, b):
    return _dot(a, b, ((1,), (0,)))


def _nt(a, b):
    return _dot(a, b, ((1,), (1,)))


def _tn(a, b):
    return _dot(a, b, ((0,), (0,)))


def _split2(x):
    hi = x.astype(BF16)
    lo = (x - hi.astype(F32)).astype(BF16)
    return hi, lo


def _split3(x):
    hi = x.astype(BF16)
    r1 = x - hi.astype(F32)
    mid = r1.astype(BF16)
    return hi, mid, (r1 - mid.astype(F32)).astype(BF16)


def _rms_r(x, d):
    return lax.rsqrt(jnp.sum(x * x, axis=-1, keepdims=True) * (1.0 / d) + EPS)


def _rms_bwd(x, g, dy, d):
    r = _rms_r(x, d)
    xhat = x * r
    gy = dy * g
    dx = r * (gy - xhat * (jnp.sum(xhat * gy, axis=-1, keepdims=True) * (1.0 / d)))
    return dx, jnp.sum(dy * xhat, axis=0, keepdims=True)


def _rot(x):
    lane = lax.broadcasted_iota(jnp.int32, x.shape, x.ndim - 1)
    n = x.shape[-1]
    return jnp.where((lane & 31) < 16, pltpu.roll(x, n - 16, x.ndim - 1), pltpu.roll(x, 16, x.ndim - 1))


def _fold4(x):
    return x + pltpu.roll(x, 32, 1) + pltpu.roll(x, 64, 1) + pltpu.roll(x, 96, 1)


def matmul_call(name, a, b, mode, out_dtype=F32, res=None, tm=512, tn=None, ex=None):
    M, K = a.shape
    N = b.shape[1] if mode == "nn" else b.shape[0]
    tn = N if tn is None else tn
    assert M % tm == 0 and N % tn == 0

    def body(*refs):
        if res is None:
            a_ref, b_ref, o_ref = refs
        else:
            a_ref, b_ref, r_ref, o_ref = refs
        av = a_ref[...].astype(BF16)
        bv = b_ref[...].astype(BF16)
        acc = _nn(av, bv) if mode == "nn" else _nt(av, bv)
        if res is not None:
            acc = r_ref[...] + acc
        o_ref[...] = acc.astype(out_dtype)

    in_specs = [pl.BlockSpec((tm, K), lambda j, i: (i, 0))]
    if mode == "nn":
        in_specs.append(pl.BlockSpec((K, tn), lambda j, i: (0, j)))
    else:
        in_specs.append(pl.BlockSpec((tn, K), lambda j, i: (j, 0)))
    args = [a, b]
    if res is not None:
        in_specs.append(pl.BlockSpec((tm, tn), lambda j, i: (i, j)))
        args.append(res)
    outs, moved = _call(body, ex, name=name, grid=(N // tn, M // tm), in_specs=in_specs,
                        out_specs=[pl.BlockSpec((tm, tn), lambda j, i: (i, j))],
                        out_shape=[jax.ShapeDtypeStruct((M, N), out_dtype)], args=args)
    return outs[0] if ex is None else (outs[0], moved)


def _rows(tm, width):
    return pl.BlockSpec((tm, width), lambda i: (i, 0))


def _whole(a, **kw):
    return pl.BlockSpec(a.shape, lambda i: (0,) * a.ndim, **kw)


def proj_out_norm_call(o_sb, o_mla, g_sb, g_mla, w_out, x, g_ffn, tm=512):
    T = o_sb.shape[0]
    N = w_out.shape[1]

    def body(a_ref, b_ref, ga_ref, gb_ref, w_ref, x_ref, g_ref, oc_ref, x1_ref, hn_ref):
        a = a_ref[...]
        b = b_ref[...]
        ya = (a * _rms_r(a, ATT_W)) * ga_ref[...]
        yb = (b * _rms_r(b, ATT_W)) * gb_ref[...]
        o_cat = jnp.concatenate([ya, yb], axis=1).astype(BF16)
        oc_ref[...] = o_cat
        x1 = x_ref[...] + _nn(o_cat, w_ref[...])
        x1_ref[...] = x1
        hn_ref[...] = ((x1 * _rms_r(x1, N)) * g_ref[...]).astype(BF16)

    return pl.pallas_call(
        body, name="proj_out", grid=(T // tm,),
        in_specs=[_rows(tm, ATT_W), _rows(tm, ATT_W), _whole(g_sb), _whole(g_mla), _whole(w_out), _rows(tm, N),
                  _whole(g_ffn)],
        out_specs=[_rows(tm, 2 * ATT_W), _rows(tm, N), _rows(tm, N)],
        out_shape=[jax.ShapeDtypeStruct((T, 2 * ATT_W), BF16), jax.ShapeDtypeStruct((T, N), F32),
                   jax.ShapeDtypeStruct((T, N), BF16)],
        compiler_params=_params(("parallel",)),
    )(o_sb, o_mla, g_sb, g_mla, w_out, x, g_ffn)


def norm_mix_rope_call(x, g, pos, inv_freq, ex, tm=512):
    T, d = x.shape

    def body(x_ref, g_ref, p_ref, f_ref, o_ref, c_ref, s_ref):
        xv = x_ref[...]
        o_ref[...] = ((xv * _rms_r(xv, d)) * g_ref[...]).astype(BF16)
        ang = p_ref[...].astype(F32) * f_ref[...]
        lane = lax.broadcasted_iota(jnp.int32, ang.shape, 1)
        sn = jnp.sin(ang)
        c_ref[...] = jnp.cos(ang)
        s_ref[...] = jnp.where((lane & 31) < 16, -sn, sn)

    outs, moved = _call(
        body, ex, name="norm_mix", grid=(T // tm,),
        in_specs=[_rows(tm, d), _whole(g), _rows(tm, 1), _whole(inv_freq)],
        out_specs=[_rows(tm, d), _rows(tm, LANES), _rows(tm, LANES)],
        out_shape=[jax.ShapeDtypeStruct((T, d), BF16), jax.ShapeDtypeStruct((T, LANES), F32),
                   jax.ShapeDtypeStruct((T, LANES), F32)], args=(x, g, pos, inv_freq))
    return tuple(outs) + (moved,)


def ffn_down_loss_call(act, w_down, x1, g, target, tm=512):
    T, K = act.shape
    d = w_down.shape[1]

    def body(a_ref, w_ref, x1_ref, g_ref, t_ref, dx_ref, dxb_ref, loss_ref, dg_ref):
        @pl.when(pl.program_id(0) == 0)
        def _():
            loss_ref[...] = jnp.zeros_like(loss_ref)
            dg_ref[...] = jnp.zeros_like(dg_ref)

        x = x1_ref[...] + _nn(a_ref[...], w_ref[...])
        g = g_ref[...]
        y = (x * _rms_r(x, d)) * g
        err = y - t_ref[...]
        loss_ref[...] += jnp.sum(jnp.sum(err * err, axis=1, keepdims=True), axis=0, keepdims=True) * (0.5 / d)
        dx, dg = _rms_bwd(x, g, err * (1.0 / d), d)
        dx_ref[...] = dx
        dxb_ref[...] = dx.astype(BF16)
        dg_ref[...] += dg

    return pl.pallas_call(
        body, name="ffn_down_loss", grid=(T // tm,),
        in_specs=[_rows(tm, K), _whole(w_down), _rows(tm, d), _whole(g), _rows(tm, d)],
        out_specs=[_rows(tm, d), _rows(tm, d), pl.BlockSpec((1, LANES), lambda i: (0, 0)), _whole(g)],
        out_shape=[jax.ShapeDtypeStruct((T, d), F32), jax.ShapeDtypeStruct((T, d), BF16),
                   jax.ShapeDtypeStruct((1, LANES), F32), jax.ShapeDtypeStruct((1, d), F32)],
        compiler_params=_params(("arbitrary",)),
    )(act, w_down, x1, g, target)


def proj_out_bwd_call(dx1, w_out, o_sb, o_mla, g_sb, g_mla, ex, tm=512):
    T, N = dx1.shape

    def body(d_ref, w_ref, a_ref, b_ref, ga_ref, gb_ref, da_ref, db_ref, dga_ref, dgb_ref):
        @pl.when(pl.program_id(0) == 0)
        def _():
            dga_ref[...] = jnp.zeros_like(dga_ref)
            dgb_ref[...] = jnp.zeros_like(dgb_ref)

        d = _nt(d_ref[...].astype(BF16), w_ref[...])
        da, dga = _rms_bwd(a_ref[...], ga_ref[...], d[:, :ATT_W], ATT_W)
        db, dgb = _rms_bwd(b_ref[...], gb_ref[...], d[:, ATT_W:], ATT_W)
        da_ref[...] = da
        db_ref[...] = db
        dga_ref[...] += dga
        dgb_ref[...] += dgb

    outs, moved = _call(
        body, ex, name="proj_out_bwd", grid=(T // tm,),
        in_specs=[_rows(tm, N), _whole(w_out), _rows(tm, ATT_W), _rows(tm, ATT_W), _whole(g_sb), _whole(g_mla)],
        out_specs=[_rows(tm, ATT_W), _rows(tm, ATT_W), _whole(g_sb), _whole(g_mla)],
        out_shape=[jax.ShapeDtypeStruct((T, ATT_W), F32), jax.ShapeDtypeStruct((T, ATT_W), F32),
                   jax.ShapeDtypeStruct((1, ATT_W), F32), jax.ShapeDtypeStruct((1, ATT_W), F32)],
        args=(dx1, w_out, o_sb, o_mla, g_sb, g_mla))
    return tuple(outs) + (moved,)


def wgrad_call(name, a, b, tn=None, tt=512, by_chip=False):
    T, M = a.shape
    N = b.shape[1]
    tn = N if tn is None else tn
    tt = min(tt, T)
    assert T % tt == 0 and N % tn == 0
    if by_chip:
        out_spec = pl.BlockSpec((None, M, tn), lambda j, t: (j, 0, 0))
        out_shape = jax.ShapeDtypeStruct((N // tn, M, tn), F32)
    else:
        out_spec = pl.BlockSpec((M, tn), lambda j, t: (0, j))
        out_shape = jax.ShapeDtypeStruct((M, N), F32)

    def body(a_ref, b_ref, o_ref):
        @pl.when(pl.program_id(1) == 0)
        def _():
            o_ref[...] = jnp.zeros_like(o_ref)

        o_ref[...] += _tn(a_ref[...].astype(BF16), b_ref[...].astype(BF16))

    return pl.pallas_call(
        body, name=name, grid=(N // tn, T // tt),
        in_specs=[pl.BlockSpec((tt, M), lambda j, t: (t, 0)), pl.BlockSpec((tt, tn), lambda j, t: (t, j))],
        out_specs=out_spec, out_shape=out_shape,
        compiler_params=_params(("parallel", "arbitrary")),
    )(a, b)


UP_COLS = 2 * D_FF // N_CHIPS


def _shift_down_after(u, prev8, n):
    top = pltpu.roll(jnp.concatenate([prev8, u[0:8]], axis=0), n, 0)[8:16]
    return jnp.concatenate([top, pltpu.roll(u, n, 0)[8:]], axis=0)


def ffn_up_conv_call(hn, w4, conv_w, conv_b, S, tm=512):
    T, K = hn.shape
    per_seq = S // tm

    def body(a_ref, wg_ref, wv_ref, cg_ref, cv_ref, bg_ref, bv_ref, ug_ref, uv_ref, act_ref, og_ref, ov_ref, pg_ref, pv_ref):
        @pl.when(pl.program_id(1) % per_seq == 0)
        def _():
            pg_ref[...] = jnp.zeros_like(pg_ref)
            pv_ref[...] = jnp.zeros_like(pv_ref)

        a = a_ref[...]
        outs = []
        for w_ref, c_ref, b_ref, u_ref, prev_ref in ((wg_ref, cg_ref, bg_ref, ug_ref, pg_ref),
                                                     (wv_ref, cv_ref, bv_ref, uv_ref, pv_ref)):
            u = _nn(a, w_ref[...])
            u_ref[...] = u
            prev8 = prev_ref[...]
            taps = _taps(c_ref)
            outs.append(taps[0] * _shift_down_after(u, prev8, 2) + taps[1] * _shift_down_after(u, prev8, 1)
                        + taps[2] * u + b_ref[...])
            prev_ref[...] = u[tm - 8:, :]
        gate, val = outs
        sig = 1.0 / (1.0 + jnp.exp(-gate))
        silu = gate * sig
        act_ref[...] = (silu * val).astype(BF16)
        og_ref[...] = silu.astype(BF16)
        ov_ref[...] = (val * (sig * (1.0 + gate * (1.0 - sig)))).astype(BF16)

    out = pl.BlockSpec((tm, UP_COLS), lambda j, i: (i, j))

    def cols(rows, off):
        return pl.BlockSpec((rows, UP_COLS), lambda j, i: (0, off + j))

    return pl.pallas_call(
        body, name="ffn_up_conv", grid=(2, T // tm),
        in_specs=[pl.BlockSpec((tm, K), lambda j, i: (i, 0)),
                  pl.BlockSpec((None, K, UP_COLS), lambda j, i: (j, 0, 0)),
                  pl.BlockSpec((None, K, UP_COLS), lambda j, i: (2 + j, 0, 0)),
                  cols(3, 0), cols(3, 2), cols(1, 0), cols(1, 2)],
        out_specs=[out] * 5,
        out_shape=[jax.ShapeDtypeStruct((T, D_FF), F32)] * 2 + [jax.ShapeDtypeStruct((T, D_FF), BF16)] * 3,
        scratch_shapes=[pltpu.VMEM((8, UP_COLS), F32)] * 2,
        compiler_params=_params(("arbitrary", "arbitrary")),
    )(hn, w4, w4, conv_w, conv_w, conv_b, conv_b)


def _shift_up_before(d, next8, n):
    rows = d.shape[0]
    bottom = pltpu.roll(jnp.concatenate([d[rows - 8:], next8], axis=0), 16 - n, 0)[0:8]
    return jnp.concatenate([pltpu.roll(d, rows - n, 0)[:rows - 8], bottom], axis=0)


def conv_bwd_wgrad_up_call(hn, ug, uv, cg, cv, dx2, w_down, conv_w, S, tt=256):
    T, M = hn.shape
    nt = T // tt
    per_seq = S // tt

    def body(a_ref, ug_ref, uv_ref, cg_ref, cv_ref, dx_ref, wd_ref, wg_ref, wv_ref,
             dug_ref, duv_ref, dw_ref, dwg_ref, dwv_ref, dbg_ref, dbv_ref, ng_ref, nv_ref):
        step = pl.program_id(1)

        @pl.when(step == 0)
        def _():
            for r in (dw_ref, dwg_ref, dwv_ref, dbg_ref, dbv_ref):
                r[...] = jnp.zeros_like(r)

        @pl.when((nt - 1 - step) % per_seq == per_seq - 1)
        def _():
            ng_ref[...] = jnp.zeros_like(ng_ref)
            nv_ref[...] = jnp.zeros_like(nv_ref)

        da = _nt(dx_ref[...], wd_ref[...])
        dval = da * cg_ref[...].astype(F32)
        dgate = da * cv_ref[...].astype(F32)
        a = a_ref[...]
        for k, (u_ref, d, w, du_ref, dcw_ref, dcb_ref, next_ref) in enumerate((
                (ug_ref, dgate, _taps(wg_ref), dug_ref, dwg_ref, dbg_ref, ng_ref),
                (uv_ref, dval, _taps(wv_ref), duv_ref, dwv_ref, dbv_ref, nv_ref))):
            u_ = u_ref[...]
            next8 = next_ref[...]
            d1 = _shift_up_before(d, next8, 1)
            d2 = _shift_up_before(d, next8, 2)
            du = (w[2] * d + w[1] * d1 + w[0] * d2).astype(BF16)
            du_ref[...] = du
            dw_ref[k] += _tn(a, du)
            dcb_ref[...] += jnp.sum(d, axis=0, keepdims=True)
            dcw_ref[0:1, :] += jnp.sum(d2 * u_, axis=0, keepdims=True)
            dcw_ref[1:2, :] += jnp.sum(d1 * u_, axis=0, keepdims=True)
            dcw_ref[2:3, :] += jnp.sum(d * u_, axis=0, keepdims=True)
            next_ref[...] = d[0:8, :]

    def tok(width, by_col):
        return pl.BlockSpec((tt, width), (lambda j, i: (nt - 1 - i, j)) if by_col else (lambda j, i: (nt - 1 - i, 0)))

    def cols(rows, off):
        return pl.BlockSpec((rows, UP_COLS), lambda j, i: (0, off + j))

    outs = pl.pallas_call(
        body, name="conv_bwd_wgrad_up", grid=(2, nt),
        in_specs=[tok(M, False), tok(UP_COLS, True), tok(UP_COLS, True), tok(UP_COLS, True), tok(UP_COLS, True),
                  tok(dx2.shape[1], False), pl.BlockSpec((UP_COLS, w_down.shape[1]), lambda j, i: (j, 0)),
                  cols(3, 0), cols(3, 2)],
        out_specs=[tok(UP_COLS, True), tok(UP_COLS, True),
                   pl.BlockSpec((2, None, M, UP_COLS), lambda j, i: (0, j, 0, 0)),
                   cols(3, 0), cols(3, 0), cols(1, 0), cols(1, 0)],
        out_shape=[jax.ShapeDtypeStruct((T, D_FF), BF16), jax.ShapeDtypeStruct((T, D_FF), BF16),
                   jax.ShapeDtypeStruct((2, 2, M, UP_COLS), F32),
                   jax.ShapeDtypeStruct((3, D_FF), F32), jax.ShapeDtypeStruct((3, D_FF), F32),
                   jax.ShapeDtypeStruct((1, D_FF), F32), jax.ShapeDtypeStruct((1, D_FF), F32)],
        scratch_shapes=[pltpu.VMEM((8, UP_COLS), F32)] * 2,
        compiler_params=_params(("arbitrary", "arbitrary")),
    )(hn, ug, uv, cg, cv, dx2, w_down, conv_w, conv_w)
    return (outs[0], outs[1], outs[2].reshape(N_CHIPS, M, UP_COLS)) + tuple(outs[3:])


def ffn_up_bwd_call(du_g, du_v, w4, x1, g_ffn, dx2, ex, tm=512):
    T = du_g.shape[0]
    N = w4.shape[1]

    def body(g_ref, v_ref, w_ref, x1_ref, gf_ref, dx2_ref, dx1_ref, dg_ref):
        @pl.when(pl.program_id(0) == 0)
        def _():
            dg_ref[...] = jnp.zeros_like(dg_ref)

        dhn = _nt(g_ref[:, :UP_COLS], w_ref[0]) + _nt(g_ref[:, UP_COLS:], w_ref[1])
        dhn = dhn + _nt(v_ref[:, :UP_COLS], w_ref[2]) + _nt(v_ref[:, UP_COLS:], w_ref[3])
        dx, dg = _rms_bwd(x1_ref[...], gf_ref[...], dhn, N)
        dx1_ref[...] = dx2_ref[...] + dx
        dg_ref[...] += dg

    outs, moved = _call(
        body, ex, name="ffn_up_bwd", grid=(T // tm,),
        in_specs=[_rows(tm, D_FF), _rows(tm, D_FF), _whole(w4, pipeline_mode=pl.Buffered(1)), _rows(tm, N),
                  _whole(g_ffn), _rows(tm, N)],
        out_specs=[_rows(tm, N), _whole(g_ffn)],
        out_shape=[jax.ShapeDtypeStruct((T, N), F32), jax.ShapeDtypeStruct((1, N), F32)],
        args=(du_g, du_v, w4, x1, g_ffn, dx2))
    return outs[0], outs[1], moved


def wgrad_up_call(hn, du_g, du_v, tt=512):
    T, M = hn.shape

    def body(a_ref, g_ref, v_ref, o_ref):
        @pl.when(pl.program_id(1) == 0)
        def _():
            o_ref[...] = jnp.zeros_like(o_ref)

        a = a_ref[...]
        o_ref[0] += _tn(a, g_ref[...])
        o_ref[1] += _tn(a, v_ref[...])

    col = pl.BlockSpec((tt, UP_COLS), lambda j, t: (t, j))
    out = pl.pallas_call(
        body, name="wgrad_up", grid=(2, T // tt),
        in_specs=[pl.BlockSpec((tt, M), lambda j, t: (t, 0)), col, col],
        out_specs=pl.BlockSpec((2, None, M, UP_COLS), lambda j, t: (0, j, 0, 0)),
        out_shape=jax.ShapeDtypeStruct((2, 2, M, UP_COLS), F32),
        compiler_params=_params(("parallel", "arbitrary")),
    )(hn, du_g, du_v)
    return out.reshape(N_CHIPS, M, UP_COLS)


IN_PIECES = ((0, ATT_W), (ATT_W, ATT_W), (2 * ATT_W, ATT_W), (3 * ATT_W, Q_RANK), (3 * ATT_W + Q_RANK, Q_RANK))


def proj_in_bwd_call(pieces, w_in_t, ex, tm=512):
    T = pieces[0].shape[0]
    N = w_in_t.shape[1]
    n = len(pieces)

    def body(*refs):
        o_ref = refs[2 * n]
        acc = _nn(refs[0][...].astype(BF16), refs[n][...])
        for i in range(1, n):
            acc = acc + _nn(refs[i][...].astype(BF16), refs[n + i][...])
        o_ref[...] = acc

    outs, moved = _call(
        body, ex, name="proj_in_bwd", grid=(T // tm,),
        in_specs=[pl.BlockSpec((tm, w), lambda i: (i, 0)) for _, w in IN_PIECES]
        + [pl.BlockSpec((w, N), functools.partial(lambda c, i: (c, 0), off // w)) for off, w in IN_PIECES],
        out_specs=[pl.BlockSpec((tm, N), lambda i: (i, 0))],
        out_shape=[jax.ShapeDtypeStruct((T, N), F32)], args=tuple(pieces) + (w_in_t,) * n)
    return outs[0], moved


def wgrad_in_call(h, pieces, tt=512):
    T, M = h.shape
    n = len(pieces)

    def body(*refs):
        a_ref, o_ref = refs[0], refs[n + 1]

        @pl.when(pl.program_id(0) == 0)
        def _():
            o_ref[...] = jnp.zeros_like(o_ref)

        a = a_ref[...]
        for i, (off, w) in enumerate(IN_PIECES):
            rows = min(w, IN_COLS - off)
            o_ref[off:off + rows, :] += _tn(refs[1 + i][...].astype(BF16), a)[:rows]

    return pl.pallas_call(
        body, name="wgrad_in", grid=(T // tt,),
        in_specs=[pl.BlockSpec((tt, M), lambda t: (t, 0))] + [pl.BlockSpec((tt, w), lambda t: (t, 0)) for _, w in IN_PIECES],
        out_specs=pl.BlockSpec((IN_COLS, M), lambda t: (0, 0)),
        out_shape=jax.ShapeDtypeStruct((IN_COLS, M), F32),
        compiler_params=_params(("arbitrary",)),
    )(h, *pieces)


def rmsnorm_fwd_call(name, x, g, tm=512, ex=None):
    T, d = x.shape

    def body(x_ref, g_ref, o_ref):
        x = x_ref[...]
        o_ref[...] = ((x * _rms_r(x, d)) * g_ref[...]).astype(BF16)

    row = pl.BlockSpec((tm, d), lambda i: (i, 0))
    outs, moved = _call(body, ex, name=name, grid=(T // tm,), in_specs=[row, pl.BlockSpec((1, d), lambda i: (0, 0))],
                        out_specs=[row], out_shape=[jax.ShapeDtypeStruct((T, d), BF16)], args=(x, g))
    return outs[0] if ex is None else (outs[0], moved)


def rmsnorm_bwd_call(name, x, g, dy, res, tm=512, ex=None):
    T, d = x.shape

    def body(x_ref, g_ref, dy_ref, r_ref, dx_ref, dg_ref):
        @pl.when(pl.program_id(0) == 0)
        def _():
            dg_ref[...] = jnp.zeros_like(dg_ref)

        dx, dg = _rms_bwd(x_ref[...], g_ref[...], dy_ref[...], d)
        dx_ref[...] = r_ref[...] + dx
        dg_ref[...] += dg

    row = pl.BlockSpec((tm, d), lambda i: (i, 0))
    vec = pl.BlockSpec((1, d), lambda i: (0, 0))
    outs, moved = _call(body, ex, name=name, grid=(T // tm,), in_specs=[row, vec, row, row], out_specs=[row, vec],
                        out_shape=[jax.ShapeDtypeStruct((T, d), F32), jax.ShapeDtypeStruct((1, d), F32)],
                        args=(x, g, dy, res))
    return tuple(outs) if ex is None else tuple(outs) + (moved,)


def outnorm_fwd_call(o_sb, o_mla, g_sb, g_mla, tm=512):
    T = o_sb.shape[0]

    def body(a_ref, b_ref, ga_ref, gb_ref, o_ref):
        a = a_ref[...]
        b = b_ref[...]
        ya = (a * _rms_r(a, ATT_W)) * ga_ref[...]
        yb = (b * _rms_r(b, ATT_W)) * gb_ref[...]
        o_ref[...] = jnp.concatenate([ya, yb], axis=1).astype(BF16)

    row = pl.BlockSpec((tm, ATT_W), lambda i: (i, 0))
    vec = pl.BlockSpec((1, ATT_W), lambda i: (0, 0))
    return pl.pallas_call(
        body, name="outnorm_fwd", grid=(T // tm,), in_specs=[row, row, vec, vec],
        out_specs=pl.BlockSpec((tm, 2 * ATT_W), lambda i: (i, 0)),
        out_shape=jax.ShapeDtypeStruct((T, 2 * ATT_W), BF16),
        compiler_params=_params(("parallel",)),
    )(o_sb, o_mla, g_sb, g_mla)


def outnorm_bwd_call(o_sb, o_mla, g_sb, g_mla, do_cat, tm=512, ex=None):
    T = o_sb.shape[0]

    def body(a_ref, b_ref, ga_ref, gb_ref, d_ref, da_ref, db_ref, dga_ref, dgb_ref):
        @pl.when(pl.program_id(0) == 0)
        def _():
            dga_ref[...] = jnp.zeros_like(dga_ref)
            dgb_ref[...] = jnp.zeros_like(dgb_ref)

        d = d_ref[...]
        da, dga = _rms_bwd(a_ref[...], ga_ref[...], d[:, :ATT_W], ATT_W)
        db, dgb = _rms_bwd(b_ref[...], gb_ref[...], d[:, ATT_W:], ATT_W)
        da_ref[...] = da
        db_ref[...] = db
        dga_ref[...] += dga
        dgb_ref[...] += dgb

    row = pl.BlockSpec((tm, ATT_W), lambda i: (i, 0))
    vec = pl.BlockSpec((1, ATT_W), lambda i: (0, 0))
    outs, moved = _call(
        body, ex, name="outnorm_bwd", grid=(T // tm,),
        in_specs=[row, row, vec, vec, pl.BlockSpec((tm, 2 * ATT_W), lambda i: (i, 0))],
        out_specs=[row, row, vec, vec],
        out_shape=[jax.ShapeDtypeStruct((T, ATT_W), F32), jax.ShapeDtypeStruct((T, ATT_W), F32),
                   jax.ShapeDtypeStruct((1, ATT_W), F32), jax.ShapeDtypeStruct((1, ATT_W), F32)],
        args=(o_sb, o_mla, g_sb, g_mla, do_cat))
    return tuple(outs) if ex is None else tuple(outs) + (moved,)


def final_loss_call(x2, g, target, tm=512):
    T, d = x2.shape

    def body(x_ref, g_ref, t_ref, dx_ref, dxb_ref, loss_ref, dg_ref):
        @pl.when(pl.program_id(0) == 0)
        def _():
            loss_ref[...] = jnp.zeros_like(loss_ref)
            dg_ref[...] = jnp.zeros_like(dg_ref)

        x = x_ref[...]
        g = g_ref[...]
        y = (x * _rms_r(x, d)) * g
        err = y - t_ref[...]
        loss_ref[...] += jnp.sum(jnp.sum(err * err, axis=1, keepdims=True), axis=0, keepdims=True) * (0.5 / d)
        dx, dg = _rms_bwd(x, g, err * (1.0 / d), d)
        dx_ref[...] = dx
        dxb_ref[...] = dx.astype(BF16)
        dg_ref[...] += dg

    row = pl.BlockSpec((tm, d), lambda i: (i, 0))
    vec = pl.BlockSpec((1, d), lambda i: (0, 0))
    return pl.pallas_call(
        body, name="final_loss", grid=(T // tm,), in_specs=[row, vec, row],
        out_specs=[row, row, pl.BlockSpec((1, LANES), lambda i: (0, 0)), vec],
        out_shape=[jax.ShapeDtypeStruct((T, d), F32), jax.ShapeDtypeStruct((T, d), BF16),
                   jax.ShapeDtypeStruct((1, LANES), F32), jax.ShapeDtypeStruct((1, d), F32)],
        compiler_params=_params(("arbitrary",)),
    )(x2, g, target)


def rope_tab_call(pos, inv_freq, tm=512):
    T = pos.shape[0]

    def body(p_ref, f_ref, c_ref, s_ref):
        ang = p_ref[...].astype(F32) * f_ref[...]
        lane = lax.broadcasted_iota(jnp.int32, ang.shape, 1)
        sn = jnp.sin(ang)
        c_ref[...] = jnp.cos(ang)
        s_ref[...] = jnp.where((lane & 31) < 16, -sn, sn)

    row = pl.BlockSpec((tm, LANES), lambda i: (i, 0))
    return pl.pallas_call(
        body, name="rope_tab", grid=(T // tm,),
        in_specs=[pl.BlockSpec((tm, 1), lambda i: (i, 0)), pl.BlockSpec((1, LANES), lambda i: (0, 0))],
        out_specs=[row, row],
        out_shape=[jax.ShapeDtypeStruct((T, LANES), F32)] * 2,
        compiler_params=_params(("parallel",)),
    )(pos, inv_freq)


def mla_prep_fwd_call(p, cos, sin, g_cq, g_ckv, w_uq_p, w_ukv_p, tm=512):
    T = p.shape[0]

    def body(cq_ref, ckvr_ref, c_ref, s_ref, gq_ref, gkv_ref, wq_ref, wkv_ref,
             qn_ref, qr_ref, kn_ref, vm_ref, krt_ref, cqn_ref, ckvn_ref):
        c = c_ref[...]
        s = s_ref[...]
        cq = cq_ref[...]
        cqn = ((cq * _rms_r(cq, Q_RANK)) * gq_ref[...]).astype(BF16)
        cqn_ref[...] = cqn
        q = _nn(cqn, wq_ref[...])
        qn_ref[...] = q[:, :ATT_W].astype(BF16)
        for g in range(ROPE_W // LANES):
            qr = q[:, ATT_W + g * LANES:ATT_W + (g + 1) * LANES]
            qr_ref[:, g * LANES:(g + 1) * LANES] = (qr * c + _rot(qr) * s).astype(BF16)
        ckvr = ckvr_ref[...]
        ckv = ckvr[:, :KV_RANK]
        ckvn = ((ckv * _rms_r(ckv, KV_RANK)) * gkv_ref[...]).astype(BF16)
        ckvn_ref[...] = ckvn
        kv = _nn(ckvn, wkv_ref[...])
        kn_ref[...] = kv[:, :ATT_W].astype(BF16)
        vm_ref[...] = kv[:, ATT_W:].astype(BF16)
        kr = _fold4(ckvr[:, KV_RANK:])
        krt_ref[...] = (kr * c + _rot(kr) * s).astype(BF16)

    def row(w, j=0):
        return pl.BlockSpec((tm, w), lambda i: (i, j))

    def full(a):
        return pl.BlockSpec(a.shape, lambda i: (0, 0))

    return pl.pallas_call(
        body, name="mla_prep_fwd", grid=(T // tm,),
        in_specs=[row(Q_RANK, 4), row(Q_RANK, 5), row(LANES), row(LANES), full(g_cq), full(g_ckv),
                  full(w_uq_p), full(w_ukv_p)],
        out_specs=[row(ATT_W), row(ROPE_W), row(ATT_W), row(ATT_W), row(LANES), row(Q_RANK), row(KV_RANK)],
        out_shape=[jax.ShapeDtypeStruct((T, w), BF16) for w in (ATT_W, ROPE_W, ATT_W, ATT_W, LANES, Q_RANK, KV_RANK)],
        compiler_params=_params(("parallel",)),
    )(p, p, cos, sin, g_cq, g_ckv, w_uq_p, w_ukv_p)


def mla_prep_bwd_call(p, cos, sin, g_cq, g_ckv, w_uq_p, w_ukv_p, dqn, dqr4, dkn, dvm, dkrt4, tm=512):
    T = p.shape[0]

    def body(cq_ref, ckvr_ref, c_ref, s_ref, gq_ref, gkv_ref, wq_ref, wkv_ref,
             dqn_ref, dqr4_ref, dkn_ref, dvm_ref, dkrt4_ref,
             dcq_ref, dckvr_ref, dq_ref, dkv_ref, dgq_ref, dgkv_ref):
        @pl.when(pl.program_id(0) == 0)
        def _():
            dgq_ref[...] = jnp.zeros_like(dgq_ref)
            dgkv_ref[...] = jnp.zeros_like(dgkv_ref)

        c = c_ref[...]
        s = s_ref[...]
        d4 = dqr4_ref[...]
        dqr = [d4[:, :128] + d4[:, 128:256], d4[:, 256:384] + d4[:, 384:]]
        dqr = [t * c + _rot(t * s) for t in dqr]
        dq = jnp.concatenate([dqn_ref[...]] + dqr, axis=1).astype(BF16)
        dq_ref[...] = dq
        dcq, dgq = _rms_bwd(cq_ref[...], gq_ref[...], _nt(dq, wq_ref[...]), Q_RANK)
        dcq_ref[...] = dcq
        dgq_ref[...] += dgq
        dkv = jnp.concatenate([dkn_ref[...], dvm_ref[...]], axis=1).astype(BF16)
        dkv_ref[...] = dkv
        ckvr = ckvr_ref[...]
        dckv, dgkv = _rms_bwd(ckvr[:, :KV_RANK], gkv_ref[...], _nt(dkv, wkv_ref[...]), KV_RANK)
        dgkv_ref[...] += dgkv
        k4 = dkrt4_ref[...]
        dkr = _fold4(k4[:, :128] + k4[:, 128:256] + k4[:, 256:384] + k4[:, 384:])
        dkr = dkr * c + _rot(dkr * s)
        lane = lax.broadcasted_iota(jnp.int32, dkr.shape, 1)
        dckvr_ref[...] = jnp.concatenate([dckv, jnp.where(lane < ROPE_DIM, dkr, 0.0)], axis=1)

    def row(w, j=0):
        return pl.BlockSpec((tm, w), lambda i: (i, j))

    def full(a):
        return pl.BlockSpec(a.shape, lambda i: (0, 0))

    return pl.pallas_call(
        body, name="mla_prep_bwd", grid=(T // tm,),
        in_specs=[row(Q_RANK, 4), row(Q_RANK, 5), row(LANES), row(LANES), full(g_cq), full(g_ckv),
                  full(w_uq_p), full(w_ukv_p), row(ATT_W), row(ATT_W), row(ATT_W), row(ATT_W), row(ATT_W)],
        out_specs=[row(Q_RANK), row(Q_RANK), row(ATT_W + ROPE_W), row(2 * ATT_W),
                   pl.BlockSpec((1, Q_RANK), lambda i: (0, 0)), pl.BlockSpec((1, KV_RANK), lambda i: (0, 0))],
        out_shape=[jax.ShapeDtypeStruct((T, Q_RANK), F32), jax.ShapeDtypeStruct((T, Q_RANK), F32),
                   jax.ShapeDtypeStruct((T, ATT_W + ROPE_W), BF16), jax.ShapeDtypeStruct((T, 2 * ATT_W), BF16),
                   jax.ShapeDtypeStruct((1, Q_RANK), F32), jax.ShapeDtypeStruct((1, KV_RANK), F32)],
        compiler_params=_params(("arbitrary",)),
    )(p, p, cos, sin, g_cq, g_ckv, w_uq_p, w_ukv_p, dqn, dqr4, dkn, dvm, dkrt4)


def _iota2(shape, axis):
    return lax.broadcasted_iota(jnp.int32, shape, axis)


def _head_masks():
    lane = _iota2((1, LANES), 1)
    return lane < HEAD_DIM, lane >= HEAD_DIM


def _pair(x, masks, dtype=BF16):
    return [jnp.where(m, x, 0.0).astype(dtype) for m in masks]


def _log_gates(z):
    keep = jnp.maximum(z, 0.0) + jnp.log2(1.0 + jnp.exp2(-jnp.abs(z)))
    return z - keep, keep


def _last_row(x):
    return _row_of(x[x.shape[0] - 8:, :], 7)


def _lane_selector(group):
    return jnp.where(_iota2((16, LANES), 1) // group == _iota2((16, LANES), 0), 1.0, 0.0).astype(BF16)


def _rows8(sel_t, x):
    hi = x.astype(BF16)
    r1 = x - hi.astype(F32)
    mid = r1.astype(BF16)
    lo = (r1 - mid.astype(F32)).astype(BF16)
    return _nt(sel_t, hi) + _nt(sel_t, mid) + _nt(sel_t, lo)


def _row_of(x8, j):
    return jnp.sum(jnp.where(_iota2(x8.shape, 0) == j, x8, 0.0), axis=0, keepdims=True)


def sb_fwd_call(p, B, S, ex=None):
    T = B * S
    TQ, TK = ATT_TQ, ATT_TK
    nq = S // TQ

    def body(q_ref, k_ref, v_ref, o_ref, lt_ref):
        qi = pl.program_id(2)
        masks = _head_masks()
        qm = [_pair(q_ref[:, sl] * SB_SCALE2, masks) for sl in PAIR_LANES]
        row = _iota2((TQ, TK), 0)
        col = _iota2((TQ, TK), 1)
        tri = jnp.where(row > col, 1.0, 0.0).astype(BF16)
        tri2 = jnp.concatenate([tri, tri], axis=0)
        vis = col < row
        o_ref[...] = jnp.zeros_like(o_ref)

        def group(k0, pairs, carry, diag):
            heads = [(pp, j) for pp in pairs for j in range(2)]
            n = range(len(heads))
            k = {pp: k_ref[pl.ds(k0, TK), PAIR_LANES[pp]].astype(BF16) for pp in pairs}
            vm = {pp: _pair(v_ref[pl.ds(k0, TK), PAIR_LANES[pp]], masks) for pp in pairs}
            z = [_nt(qm[pp][j], k[pp]) for pp, j in heads]
            if diag:
                z = [jnp.where(vis, x, NEG_BIG) for x in z]
            gates = [_log_gates(x) for x in z]
            lb = [g[0] for g in gates]
            keep = [g[1] for g in gates]
            tail = [_nn(jnp.concatenate(_split2(keep[h]), axis=1), tri2) + carry[h] for h in n]
            a = [jnp.exp2(lb[h] - tail[h]) for h in n]
            ab = [x.astype(BF16) for x in a]
            for i, pp in enumerate(pairs):
                o_ref[:, PAIR_LANES[pp]] += _nn(ab[2 * i], vm[pp][0]) + _nn(ab[2 * i + 1], vm[pp][1])
            return [tail[h][:, 0:1] + keep[h][:, 0:1] for h in n]

        def step(kb, carry, diag):
            k0 = pl.multiple_of(kb * TK, TK)
            out = []
            for g in range(0, ATT_PAIRS, SB_FWD_GROUP):
                out += group(k0, list(range(g, g + SB_FWD_GROUP)), carry[2 * g:2 * (g + SB_FWD_GROUP)], diag)
            return tuple(out)

        zero = jnp.zeros((TQ, 1), F32)
        carry = step(qi, (zero,) * (2 * ATT_PAIRS), True)
        carry = lax.fori_loop(0, qi, lambda i, c: step(qi - 1 - i, c, False), carry)
        lane = _iota2((TQ, LANES), 1)
        for pp, sl in enumerate(PAIR_LANES):
            lt_ref[:, sl] = jnp.where(lane == 0, carry[2 * pp], jnp.where(lane == 1, carry[2 * pp + 1], 0.0))

    W = ATT_PAIRS * LANES
    qspec = pl.BlockSpec((TQ, W), lambda b, h, i: (b * nq + i, h))
    outs, moved = _call(
        body, ex, name="sb_fwd", grid=(B, HEADS // 2 // ATT_PAIRS, nq),
        in_specs=[qspec,
                  pl.BlockSpec((S, W), lambda b, h, i: (b, ATT_W // W + h)),
                  pl.BlockSpec((S, W), lambda b, h, i: (b, 2 * ATT_W // W + h))],
        out_specs=[qspec, qspec],
        out_shape=[jax.ShapeDtypeStruct((T, ATT_W), F32)] * 2, args=(p, p, p))
    return tuple(outs) if ex is None else tuple(outs) + (moved,)


def sb_bwd_call(p, lt, do, B, S, ex=None):
    T = B * S
    TQ, TK = ATT_TQ, ATT_TK
    nq = S // TQ

    def body(q_ref, k_ref, v_ref, lt_ref, do_ref, dq_ref, dk_ref, dv_ref):
        qi = pl.program_id(2)

        @pl.when(qi == 0)
        def _():
            dk_ref[...] = jnp.zeros_like(dk_ref)
            dv_ref[...] = jnp.zeros_like(dv_ref)

        masks = _head_masks()
        qm = [_pair(q_ref[:, sl] * SB_SCALE2, masks) for sl in PAIR_LANES]
        dom = [_pair(do_ref[:, sl], masks) for sl in PAIR_LANES]
        start = []
        for sl in PAIR_LANES:
            l8 = _rows8(_lane_selector(1), lt_ref[:, sl])
            start += [-_row_of(l8, 0), jnp.zeros((1, TQ), F32), -_row_of(l8, 1), jnp.zeros((1, TQ), F32)]
        row = _iota2((TK, TQ), 0)
        col = _iota2((TK, TQ), 1)
        incl = jnp.where(col <= row, 1.0, 0.0).astype(BF16)
        incl2 = jnp.concatenate([incl, incl], axis=1)
        excl = jnp.where(col < row, 1.0, 0.0).astype(BF16)
        vis = row < col
        dq_ref[...] = jnp.zeros_like(dq_ref)

        def group(k0, pairs, carry, diag):
            heads = [(pp, j) for pp in pairs for j in range(2)]
            n = range(len(heads))
            kf = {pp: k_ref[pl.ds(k0, TK), PAIR_LANES[pp]] for pp in pairs}
            km = {pp: _pair(kf[pp], masks) for pp in pairs}
            v = {pp: v_ref[pl.ds(k0, TK), PAIR_LANES[pp]].astype(BF16) for pp in pairs}
            z = [_nt(kf[pp].astype(BF16), qm[pp][j]) for pp, j in heads]
            da = [_nt(v[pp], dom[pp][j]) for pp, j in heads]
            if diag:
                z = [jnp.where(vis, x, NEG_BIG) for x in z]
            gates = [_log_gates(x) for x in z]
            lb = [g[0] for g in gates]
            keep = [g[1] for g in gates]
            left = [_nn(incl2, jnp.concatenate(_split2(keep[h]), axis=0)) + carry[2 * h] for h in n]
            a = [jnp.exp2(lb[h] + left[h]) for h in n]
            e = [a[h] * da[h] for h in n]
            before = [_nn(excl, e[h].astype(BF16)) + carry[2 * h + 1] for h in n]
            dz = [e[h] - jnp.exp2(lb[h]) * (e[h] + before[h]) for h in n]
            dzb = [x.astype(BF16) for x in dz]
            ab = [x.astype(BF16) for x in a]
            out = []
            for h in n:
                out += [_last_row(left[h]), _last_row(before[h]) + _last_row(e[h])]
            for i, pp in enumerate(pairs):
                sl = PAIR_LANES[pp]
                dk_ref[pl.ds(k0, TK), sl] += _nn(dzb[2 * i], qm[pp][0]) + _nn(dzb[2 * i + 1], qm[pp][1])
                dv_ref[pl.ds(k0, TK), sl] += _nn(ab[2 * i], dom[pp][0]) + _nn(ab[2 * i + 1], dom[pp][1])
                dq_ref[:, sl] += _tn(dzb[2 * i], km[pp][0]) + _tn(dzb[2 * i + 1], km[pp][1])
            return out

        def step(kb, carry, diag):
            k0 = pl.multiple_of(kb * TK, TK)
            out = []
            for g in range(0, ATT_PAIRS, SB_BWD_GROUP):
                out += group(k0, list(range(g, g + SB_BWD_GROUP)), carry[4 * g:4 * (g + SB_BWD_GROUP)], diag)
            return tuple(out)

        carry = lax.fori_loop(0, qi, lambda i, c: step(i, c, False), tuple(start))
        step(qi, carry, True)
        dq_ref[...] *= SB_SCALE

        @pl.when(qi == nq - 1)
        def _():
            dk_ref[...] *= LN2

    W = ATT_PAIRS * LANES
    qspec = pl.BlockSpec((TQ, W), lambda b, h, i: (b * nq + i, h))
    sspec = pl.BlockSpec((S, W), lambda b, h, i: (b, h))
    outs, moved = _call(
        body, ex, name="sb_bwd", grid=(B, HEADS // 2 // ATT_PAIRS, nq),
        in_specs=[qspec,
                  pl.BlockSpec((S, W), lambda b, h, i: (b, ATT_W // W + h)),
                  pl.BlockSpec((S, W), lambda b, h, i: (b, 2 * ATT_W // W + h)),
                  qspec, qspec],
        out_specs=[qspec, sspec, sspec],
        out_shape=[jax.ShapeDtypeStruct((T, ATT_W), F32)] * 3, args=(p, p, p, lt, do))
    return tuple(outs) if ex is None else tuple(outs) + (moved,)


ALL_PAIRS = [slice(i * LANES, (i + 1) * LANES) for i in range(HEADS // 2)]


def _rope_masks(hp):
    grp = _iota2((1, LANES), 1) // ROPE_DIM
    return [grp == ((2 * hp + j) % 4) for j in range(2)]


def _mla_queries(qn_ref, qr_ref, masks):
    out = []
    for pp, sl in enumerate(ALL_PAIRS):
        qnv = qn_ref[:, sl]
        qrv = qr_ref[:, ALL_PAIRS[pp // 2]]
        rmasks = _rope_masks(pp)
        out.append([jnp.concatenate([jnp.where(masks[j], qnv, 0), jnp.where(rmasks[j], qrv, 0)], axis=1).astype(BF16)
                    for j in range(2)])
    return out


def mla_fwd_call(qn, qr, kn, krt, vm, B, S):
    T = B * S
    TQ, TK = ATT_TQ, ATT_TK
    nq = S // TQ

    def body(qn_ref, qr_ref, kn_ref, kr_ref, v_ref, o_ref, lse_ref):
        qi = pl.program_id(1)
        masks = _head_masks()
        qcat = _mla_queries(qn_ref, qr_ref, masks)
        row = _iota2((TQ, TK), 0)
        col = _iota2((TQ, TK), 1)
        vis = col <= row
        o_ref[...] = jnp.zeros_like(o_ref)

        def group(k0, pairs, carry, diag):
            heads = [(pp, j) for pp in pairs for j in range(2)]
            n = range(len(heads))
            krv = kr_ref[pl.ds(k0, TK), :]
            kcat = {pp: jnp.concatenate([kn_ref[pl.ds(k0, TK), ALL_PAIRS[pp]], krv], axis=1) for pp in pairs}
            vmk = {pp: _pair(v_ref[pl.ds(k0, TK), ALL_PAIRS[pp]], masks) for pp in pairs}
            s = [_nt(qcat[pp][j], kcat[pp]) * MLA_SCALE2 for pp, j in heads]
            if diag:
                s = [jnp.where(vis, x, NEG_BIG) for x in s]
            m_new = [jnp.maximum(carry[2 * h], jnp.max(s[h], axis=1, keepdims=True)) for h in n]
            alpha = [jnp.exp2(carry[2 * h] - m_new[h]) for h in n]
            pexp = [jnp.exp2(s[h] - m_new[h]) for h in n]
            out = []
            for h in n:
                out += [m_new[h], alpha[h] * carry[2 * h + 1] + jnp.sum(pexp[h], axis=1, keepdims=True)]
            pb = [x.astype(BF16) for x in pexp]
            for i, pp in enumerate(pairs):
                sl = ALL_PAIRS[pp]
                scale = jnp.where(masks[0], alpha[2 * i], alpha[2 * i + 1])
                o_ref[:, sl] = o_ref[:, sl] * scale + (_nn(pb[2 * i], vmk[pp][0]) + _nn(pb[2 * i + 1], vmk[pp][1]))
            return out

        def step(kb, carry, diag):
            k0 = pl.multiple_of(kb * TK, TK)
            out = []
            for g in range(0, len(ALL_PAIRS), MLA_GROUP):
                out += group(k0, list(range(g, g + MLA_GROUP)), carry[4 * g:4 * (g + MLA_GROUP)], diag)
            return tuple(out)

        neg = jnp.full((TQ, 1), NEG_BIG, F32)
        zero = jnp.zeros((TQ, 1), F32)
        carry = step(qi, (neg, zero) * (2 * len(ALL_PAIRS)), True)
        carry = lax.fori_loop(0, qi, lambda i, c: step(qi - 1 - i, c, False), carry)
        lane = _iota2((TQ, LANES), 1)
        for pp, sl in enumerate(ALL_PAIRS):
            m0, l0, m1, l1 = carry[4 * pp:4 * pp + 4]
            o_ref[:, sl] = o_ref[:, sl] * jnp.where(masks[0], 1.0 / l0, 1.0 / l1)
            lse_ref[:, sl] = jnp.where(lane == 0, m0 * LN2 + jnp.log(l0), jnp.where(lane == 1, m1 * LN2 + jnp.log(l1), 0.0))

    def rows(w):
        return pl.BlockSpec((TQ, w), lambda b, i: (b * nq + i, 0))

    def seq(w):
        return pl.BlockSpec((S, w), lambda b, i: (b, 0))

    return pl.pallas_call(
        body, name="mla_fwd", grid=(B, nq),
        in_specs=[rows(ATT_W), rows(ROPE_W), seq(ATT_W), seq(LANES), seq(ATT_W)],
        out_specs=[rows(ATT_W), rows(ATT_W)],
        out_shape=[jax.ShapeDtypeStruct((T, ATT_W), F32)] * 2,
        compiler_params=_params(("arbitrary", "arbitrary")),
    )(qn, qr, kn, krt, vm)


def mla_bwd_call(qn, qr, kn, krt, vm, o, lse, do, B, S, ex=None):
    T = B * S
    TQ, TK = ATT_TQ, ATT_TK
    nq = S // TQ

    def body(qn_ref, qr_ref, kn_ref, kr_ref, v_ref, o_ref, lse_ref, do_ref,
             dqn_ref, dqr_ref, dkn_ref, dv_ref, dkr_ref):
        qi = pl.program_id(1)

        @pl.when(qi == 0)
        def _():
            dkn_ref[...] = jnp.zeros_like(dkn_ref)
            dv_ref[...] = jnp.zeros_like(dv_ref)
            dkr_ref[...] = jnp.zeros_like(dkr_ref)

        masks = _head_masks()
        qcat = _mla_queries(qn_ref, qr_ref, masks)
        dom, dsum, lse = [], [], []
        for sl in ALL_PAIRS:
            do = do_ref[:, sl]
            dom.append(_pair(do, masks))
            d8 = _rows8(_lane_selector(HEAD_DIM), do * o_ref[:, sl])
            l8 = _rows8(_lane_selector(1), lse_ref[:, sl])
            dsum.append([_row_of(d8, j) for j in range(2)])
            lse.append([_row_of(l8, j) * LOG2E for j in range(2)])
        row = _iota2((TK, TQ), 0)
        col = _iota2((TK, TQ), 1)
        vis = row <= col
        dqn_ref[...] = jnp.zeros_like(dqn_ref)
        dqr_ref[...] = jnp.zeros_like(dqr_ref)

        def group(k0, pairs, diag):
            heads = [(pp, j) for pp in pairs for j in range(2)]
            n = range(len(heads))
            krv = kr_ref[pl.ds(k0, TK), :]
            knv = {pp: kn_ref[pl.ds(k0, TK), ALL_PAIRS[pp]] for pp in pairs}
            kcat = {pp: jnp.concatenate([knv[pp], krv], axis=1) for pp in pairs}
            v = {pp: v_ref[pl.ds(k0, TK), ALL_PAIRS[pp]] for pp in pairs}
            s = [_nt(kcat[pp], qcat[pp][j]) * MLA_SCALE2 for pp, j in heads]
            dp_ = [_nt(v[pp], dom[pp][j]) for pp, j in heads]
            pr = [jnp.exp2(s[h] - lse[pp][j]) for h, (pp, j) in enumerate(heads)]
            if diag:
                pr = [jnp.where(vis, x, 0.0) for x in pr]
            ds = [(pr[h] * (dp_[h] - dsum[pp][j]) * MLA_SCALE).astype(BF16) for h, (pp, j) in enumerate(heads)]
            pb = [x.astype(BF16) for x in pr]
            for i, pp in enumerate(pairs):
                sl = ALL_PAIRS[pp]
                rmasks = _rope_masks(pp)
                kcat_j = [jnp.concatenate([jnp.where(masks[j], knv[pp], 0), jnp.where(rmasks[j], krv, 0)],
                                          axis=1).astype(BF16) for j in range(2)]
                dv_ref[pl.ds(k0, TK), sl] += _nn(pb[2 * i], dom[pp][0]) + _nn(pb[2 * i + 1], dom[pp][1])
                dk = _nn(ds[2 * i], qcat[pp][0]) + _nn(ds[2 * i + 1], qcat[pp][1])
                dq = _tn(ds[2 * i], kcat_j[0]) + _tn(ds[2 * i + 1], kcat_j[1])
                dqn_ref[:, sl] += dq[:, :LANES]
                dqr_ref[:, sl] += dq[:, LANES:]
                dkn_ref[pl.ds(k0, TK), sl] += dk[:, :LANES]
                dkr_ref[pl.ds(k0, TK), sl] += dk[:, LANES:]

        def step(kb, diag):
            k0 = pl.multiple_of(kb * TK, TK)
            for g in range(0, len(ALL_PAIRS), MLA_GROUP):
                group(k0, list(range(g, g + MLA_GROUP)), diag)

        step(qi, True)

        def loop(i, c):
            step(qi - 1 - i, False)
            return c

        lax.fori_loop(0, qi, loop, 0)

    def rows(w):
        return pl.BlockSpec((TQ, w), lambda b, i: (b * nq + i, 0))

    def seq(w):
        return pl.BlockSpec((S, w), lambda b, i: (b, 0))

    outs, moved = _call(
        body, ex, name="mla_bwd", grid=(B, nq),
        in_specs=[rows(ATT_W), rows(ROPE_W), seq(ATT_W), seq(LANES), seq(ATT_W), rows(ATT_W), rows(ATT_W), rows(ATT_W)],
        out_specs=[rows(ATT_W), rows(ATT_W), seq(ATT_W), seq(ATT_W), seq(ATT_W)],
        out_shape=[jax.ShapeDtypeStruct((T, ATT_W), F32)] * 5, args=(qn, qr, kn, krt, vm, o, lse, do))
    return tuple(outs) if ex is None else tuple(outs) + (moved,)


CONV_TC = 256


def _shift_down(x, n):
    return jnp.where(_iota2(x.shape, 0) >= n, pltpu.roll(x, n, 0), 0.0)


def _shift_up(x, n):
    rows = x.shape[0]
    return jnp.where(_iota2(x.shape, 0) < rows - n, pltpu.roll(x, rows - n, 0), 0.0)


def _taps(w_ref):
    return [w_ref[k:k + 1, :] for k in range(3)]


def _conv3(u, w, b):
    return w[0] * _shift_down(u, 2) + w[1] * _shift_down(u, 1) + w[2] * u + b


def _ref_shift_down(ref, n):
    rows = ref.shape[0]
    return jnp.concatenate([_shift_down(ref[0:8, :], n), ref[8 - n:rows - n, :]], axis=0)


def _conv3_ref(u_ref, w, b):
    return w[0] * _ref_shift_down(u_ref, 2) + w[1] * _ref_shift_down(u_ref, 1) + w[2] * u_ref[...] + b


def conv_act_fwd_call(ug, uv, conv_w, conv_b, B, S):
    T = B * S
    nc = D_FF // CONV_TC

    def body(ug_ref, uv_ref, wg_ref, wv_ref, bg_ref, bv_ref, a_ref, cg_ref, cv_ref):
        gate = _conv3_ref(ug_ref, _taps(wg_ref), bg_ref[...])
        val = _conv3_ref(uv_ref, _taps(wv_ref), bv_ref[...])
        a_ref[...] = (gate * (1.0 / (1.0 + jnp.exp(-gate))) * val).astype(BF16)
        cg_ref[...] = gate.astype(BF16)
        cv_ref[...] = val.astype(BF16)

    def blk(rows, off):
        return pl.BlockSpec((rows, CONV_TC), lambda b, j: (b if rows == S else 0, off + j))

    return pl.pallas_call(
        body, name="conv_act_fwd", grid=(B, nc),
        in_specs=[blk(S, 0), blk(S, 0), blk(3, 0), blk(3, nc), blk(1, 0), blk(1, nc)],
        out_specs=[blk(S, 0)] * 3,
        out_shape=[jax.ShapeDtypeStruct((T, D_FF), BF16)] * 3,
        compiler_params=_params(("parallel", "parallel")),
    )(ug, uv, conv_w, conv_w, conv_b, conv_b)


def conv_act_bwd_call(ug, uv, cg, cv, dx2, w_down, conv_w, B, S):
    T = B * S
    nc = D_FF // CONV_TC

    def body(ug_ref, uv_ref, cg_ref, cv_ref, dx_ref, wd_ref, wg_ref, wv_ref,
             dug_ref, duv_ref, dwg_ref, dwv_ref, dbg_ref, dbv_ref):
        @pl.when(pl.program_id(1) == 0)
        def _():
            for r in (dwg_ref, dwv_ref, dbg_ref, dbv_ref):
                r[...] = jnp.zeros_like(r)

        gate = cg_ref[...].astype(F32)
        val = cv_ref[...].astype(F32)
        da = _nt(dx_ref[...], wd_ref[...])
        sig = 1.0 / (1.0 + jnp.exp(-gate))
        dval = da * (gate * sig)
        dgate = da * val * (sig * (1.0 + gate * (1.0 - sig)))
        for u_ref, d, w, du_ref, dw_ref, db_ref in ((ug_ref, dgate, _taps(wg_ref), dug_ref, dwg_ref, dbg_ref),
                                                   (uv_ref, dval, _taps(wv_ref), duv_ref, dwv_ref, dbv_ref)):
            u_ = u_ref[...]
            d1 = _shift_up(d, 1)
            d2 = _shift_up(d, 2)
            du_ref[...] = (w[2] * d + w[1] * d1 + w[0] * d2).astype(BF16)
            db_ref[...] += jnp.sum(d, axis=0, keepdims=True)
            dw_ref[0:1, :] += jnp.sum(d2 * u_, axis=0, keepdims=True)
            dw_ref[1:2, :] += jnp.sum(d1 * u_, axis=0, keepdims=True)
            dw_ref[2:3, :] += jnp.sum(d * u_, axis=0, keepdims=True)

    def blk(rows, off):
        return pl.BlockSpec((rows, CONV_TC), lambda j, b: (b if rows == S else 0, off + j))

    return pl.pallas_call(
        body, name="conv_act_bwd", grid=(nc, B),
        in_specs=[blk(S, 0), blk(S, 0), blk(S, 0), blk(S, 0), pl.BlockSpec((S, D_MODEL), lambda j, b: (b, 0)),
                  pl.BlockSpec((CONV_TC, D_MODEL), lambda j, b: (j, 0)), blk(3, 0), blk(3, nc)],
        out_specs=[blk(S, 0), blk(S, 0), blk(3, 0), blk(3, 0), blk(1, 0), blk(1, 0)],
        out_shape=[jax.ShapeDtypeStruct((T, D_FF), BF16), jax.ShapeDtypeStruct((T, D_FF), BF16),
                   jax.ShapeDtypeStruct((3, D_FF), F32), jax.ShapeDtypeStruct((3, D_FF), F32),
                   jax.ShapeDtypeStruct((1, D_FF), F32), jax.ShapeDtypeStruct((1, D_FF), F32)],
        compiler_params=_params(("parallel", "arbitrary")),
    )(ug, uv, cg, cv, dx2, w_down, conv_w, conv_w)


CHIP_MASKS = ((1, 0), (0, 1), (1, 1))


def _place():
    return lax.axis_index("x"), lax.axis_index("y"), lax.axis_index("c")


HALF_ALIGN = 32


def _any_specs(n):
    return [pl.BlockSpec(memory_space=pl.ANY)] * n


def _splits(shape):
    r, c = shape
    return "rows" if r % HALF_ALIGN == 0 else "cols" if c % (2 * LANES) == 0 else None


def _half(shape, half):
    r, c = shape
    how = _splits(shape)
    if how == "rows":
        return (pl.ds(pl.multiple_of(half * (r // 2), HALF_ALIGN // 2), r // 2), slice(None))
    if how == "cols":
        return (slice(None), pl.ds(pl.multiple_of(half * (c // 2), LANES), c // 2))
    return (slice(None), slice(None))


def _half_shape(shape):
    r, c = shape
    return {"rows": (r // 2, c), "cols": (r, c // 2)}[_splits(shape)]


def _remote(src, dst, send_sem, recv_sem, device):
    return pltpu.make_async_remote_copy(src_ref=src, dst_ref=dst, send_sem=send_sem, recv_sem=recv_sem,
                                        device_id=device, device_id_type=MESH)


class Exchange:
    def __init__(self, ins, out_shape, sems, start, finish):
        self.ins, self.out_shape, self.sems, self.start, self.finish = list(ins), list(out_shape), list(sems), start, finish


def gather_group(shards):
    n = len(shards)
    split = [_splits(s.shape) is not None for s in shards]

    def part(w, half):
        return _half(shards[w].shape, half)

    def copies(ins, outs, sems):
        ici_s, ici_r, _, _, local_sems = sems
        x, y, c = _place()
        chip = 2 * x + y
        local = [pltpu.make_async_copy(ins[w], outs[w].at[chip], local_sems.at[w]) for w in range(n)]
        sends = [_remote(ins[w].at[part(w, c)], outs[w].at[(chip,) + part(w, c)], ici_s.at[w, k], ici_r.at[w, k],
                         (x ^ fx, y ^ fy, c))
                 for w in range(n) for k, (fx, fy) in enumerate(CHIP_MASKS)]
        return local, sends

    def start(ins, outs, sems):
        local, sends = copies(ins, outs, sems)
        for cp in local + sends:
            cp.start()

    def finish(ins, outs, sems):
        ici_s, ici_r, d2d_s, d2d_r, _ = sems
        x, y, c = _place()
        sib = (x, y, 1 - c)
        local, sends = copies(ins, outs, sems)
        for w in range(n):
            for k, (fx, fy) in enumerate(CHIP_MASKS):
                landed = outs[w].at[(2 * (x ^ fx) + (y ^ fy),) + part(w, c)]
                _remote(landed, landed, ici_s.at[w, k], ici_r.at[w, k], sib).wait_recv()
                if split[w]:
                    cp = _remote(landed, landed, d2d_s.at[w, k], d2d_r.at[w, k], sib)
                    cp.start()
                    sends.append(cp)
        for w in range(n):
            for k, (fx, fy) in enumerate(CHIP_MASKS):
                if split[w]:
                    other = outs[w].at[(2 * (x ^ fx) + (y ^ fy),) + part(w, 1 - c)]
                    _remote(other, other, d2d_s.at[w, k], d2d_r.at[w, k], sib).wait_recv()
        for cp in sends:
            cp.wait_send()
        for cp in local:
            cp.wait()

    sems = pltpu.SemaphoreType.DMA((n, 3))
    return Exchange(shards, [jax.ShapeDtypeStruct((N_CHIPS,) + s.shape, s.dtype) for s in shards],
                    [sems, sems, sems, sems, pltpu.SemaphoreType.DMA((n,))], start, finish)


def swap_half(parts):
    n = len(parts)

    def copies(ins, outs, sems):
        x, y, c = _place()
        return [_remote(ins[w].at[(slice(None),) + _half(parts[w].shape[1:], 1 - c)], outs[w], sems[0].at[w], sems[1].at[w],
                        (x, y, 1 - c)) for w in range(n)]

    def start(ins, outs, sems):
        for cp in copies(ins, outs, sems):
            cp.start()

    def finish(ins, outs, sems):
        for cp in copies(ins, outs, sems):
            cp.wait_recv()
            cp.wait_send()

    return Exchange(parts, [jax.ShapeDtypeStruct((N_CHIPS,) + _half_shape(p.shape[1:]), F32) for p in parts],
                    [pltpu.SemaphoreType.DMA((n,))] * 2, start, finish)


def scatter_half(halves):
    n = len(halves)

    def copies(ins, outs, sems):
        x, y, c = _place()
        return [_remote(ins[w].at[2 * (x ^ fx) + (y ^ fy)], outs[w].at[k], sems[0].at[w, k], sems[1].at[w, k],
                        (x ^ fx, y ^ fy, c))
                for w in range(n) for k, (fx, fy) in enumerate(CHIP_MASKS)]

    def start(ins, outs, sems):
        for cp in copies(ins, outs, sems):
            cp.start()

    def finish(ins, outs, sems):
        for cp in copies(ins, outs, sems):
            cp.wait_recv()
            cp.wait_send()

    return Exchange(halves, [jax.ShapeDtypeStruct((3,) + h.shape[1:], h.dtype) for h in halves],
                    [pltpu.SemaphoreType.DMA((n, 3))] * 2, start, finish)


def swap_final(finals, shapes):
    n = len(finals)

    def copies(ins, outs, sems):
        x, y, c = _place()
        mine = [outs[w].at[_half(shapes[w], c)] for w in range(n)]
        local = [pltpu.make_async_copy(ins[w], mine[w], sems[2].at[w]) for w in range(n)]
        sends = [_remote(ins[w], mine[w], sems[0].at[w], sems[1].at[w], (x, y, 1 - c)) for w in range(n)]
        return local, sends

    def start(ins, outs, sems):
        local, sends = copies(ins, outs, sems)
        for cp in local + sends:
            cp.start()

    def finish(ins, outs, sems):
        x, y, c = _place()
        local, sends = copies(ins, outs, sems)
        for w in range(n):
            got = outs[w].at[_half(shapes[w], 1 - c)]
            _remote(got, got, sems[0].at[w], sems[1].at[w], (x, y, 1 - c)).wait_recv()
        for cp in sends:
            cp.wait_send()
        for cp in local:
            cp.wait()

    return Exchange(finals, [jax.ShapeDtypeStruct(tuple(s), F32) for s in shapes],
                    [pltpu.SemaphoreType.DMA((n,))] * 3, start, finish)


def exchange_call(name, ex):
    n, m = len(ex.ins), len(ex.out_shape)

    def body(*refs):
        ins, outs, sems = refs[:n], refs[n:n + m], refs[n + m:]
        ex.start(ins, outs, sems)
        ex.finish(ins, outs, sems)

    return pl.pallas_call(body, name=name, in_specs=_any_specs(n), out_specs=_any_specs(m), out_shape=ex.out_shape,
                          scratch_shapes=ex.sems, compiler_params=_params())(*ex.ins)


def _call(body, ex, *, name, grid, in_specs, out_specs, out_shape, args, scratch_shapes=()):
    sem = ("arbitrary",) * len(grid)
    if ex is None:
        outs = pl.pallas_call(body, name=name, grid=grid, in_specs=in_specs, out_specs=out_specs, out_shape=out_shape,
                              scratch_shapes=list(scratch_shapes), compiler_params=_params(sem))(*args)
        return outs, None
    ni, no, ns = len(in_specs), len(out_specs), len(scratch_shapes)
    ne, me = len(ex.ins), len(ex.out_shape)

    def wrapped(*refs):
        own_in, ex_in = refs[:ni], refs[ni:ni + ne]
        own_out, ex_out = refs[ni + ne:ni + ne + no], refs[ni + ne + no:ni + ne + no + me]
        own_scr, ex_sems = refs[ni + ne + no + me:ni + ne + no + me + ns], refs[ni + ne + no + me + ns:]
        ids = [pl.program_id(a) for a in range(len(grid))]
        first = functools.reduce(jnp.logical_and, [i == 0 for i in ids])
        last = functools.reduce(jnp.logical_and, [i == g - 1 for i, g in zip(ids, grid)])

        @pl.when(first)
        def _():
            ex.start(ex_in, ex_out, ex_sems)

        body(*own_in, *own_out, *own_scr)

        @pl.when(last)
        def _():
            ex.finish(ex_in, ex_out, ex_sems)

    outs = pl.pallas_call(
        wrapped, name=name, grid=grid, in_specs=list(in_specs) + _any_specs(ne),
        out_specs=list(out_specs) + _any_specs(me), out_shape=list(out_shape) + ex.out_shape,
        scratch_shapes=list(scratch_shapes) + ex.sems, compiler_params=_params(sem))(*args, *ex.ins)
    return outs[:no], outs[no:]


def _row_tile(rows, cap, mult=8):
    return max([t for t in range(mult, min(rows, cap) + 1, mult) if rows % t == 0] or [rows])


def add_half_call(name, part, got, where):
    _, rh, cols = got.shape
    tr = _row_tile(rh, 176, 16)
    nb = rh // tr
    by_rows = _splits(part.shape[1:]) == "rows"

    def body(where_ref, p_ref, g_ref, own_ref, send_ref):
        t = p_ref[...] + g_ref[...]
        send_ref[...] = t.astype(BF16)
        chip = where_ref[1]
        own_ref[...] = p_ref[chip] + g_ref[chip]

    blk = (N_CHIPS, tr, cols)
    return pl.pallas_call(
        body, name=name,
        grid_spec=pltpu.PrefetchScalarGridSpec(
            num_scalar_prefetch=1, grid=(nb,),
            in_specs=[pl.BlockSpec(blk, (lambda i, where_ref: (0, where_ref[0] * nb + i, 0)) if by_rows
                                   else (lambda i, where_ref: (0, i, where_ref[0]))),
                      pl.BlockSpec(blk, lambda i, where_ref: (0, i, 0))],
            out_specs=[pl.BlockSpec((tr, cols), lambda i, where_ref: (i, 0)),
                       pl.BlockSpec(blk, lambda i, where_ref: (0, i, 0))]),
        out_shape=[jax.ShapeDtypeStruct((rh, cols), F32), jax.ShapeDtypeStruct(got.shape, BF16)],
        compiler_params=_params(("parallel",)),
    )(where, part, got)


def sum_chips_call(name, own, got):
    _, rh, cols = got.shape
    tr = _row_tile(rh, 176, 16)

    def body(h_ref, g_ref, o_ref):
        o_ref[...] = ((h_ref[...] + g_ref[0].astype(F32)) + g_ref[1].astype(F32)) + g_ref[2].astype(F32)

    return pl.pallas_call(
        body, name=name, grid=(rh // tr,),
        in_specs=[pl.BlockSpec((tr, cols), lambda i: (i, 0)), pl.BlockSpec((3, tr, cols), lambda i: (0, i, 0))],
        out_specs=pl.BlockSpec((tr, cols), lambda i: (i, 0)),
        out_shape=jax.ShapeDtypeStruct((rh, cols), F32),
        compiler_params=_params(("parallel",)),
    )(own, got)


def _adamw(w, g, m, v):
    m = ADAM_B1 * m + (1.0 - ADAM_B1) * g
    v = ADAM_B2 * v + (1.0 - ADAM_B2) * (g * g)
    m_hat = m / (1.0 - ADAM_B1 ** ADAM_STEP)
    v_hat = v / (1.0 - ADAM_B2 ** ADAM_STEP)
    delta = -ADAM_LR * (m_hat / (jnp.sqrt(v_hat) + ADAM_EPS) + ADAM_WD * w)
    return delta, m, v


def adamw_call(name, g, w, m, v):
    r, cols = w.shape
    tr = r if r % 8 else _row_tile(r, 256)

    def body(g_ref, w_ref, m_ref, v_ref, go_ref, d_ref, nm_ref, nv_ref):
        g = g_ref[...]
        go_ref[...] = g
        d_ref[...], nm_ref[...], nv_ref[...] = _adamw(w_ref[...], g, m_ref[...], v_ref[...])

    spec = pl.BlockSpec((tr, cols), lambda i: (i, 0))
    return pl.pallas_call(
        body, name=name, grid=(r // tr,), in_specs=[spec] * 4, out_specs=[spec] * 4,
        out_shape=[jax.ShapeDtypeStruct((r, cols), F32)] * 4,
        compiler_params=_params(("parallel",)),
    )(g, w, m, v)


def allsum_small_call(v):
    R = v.shape[0]

    def body(v_ref, out_ref, buf, send_sems, recv_sems):
        x, y, c = _place()
        me = 4 * x + 2 * y + c
        buf[me] = v_ref[...]
        sends = []
        for k in range(1, N_DEV):
            fx, fy, fc = (k >> 2) & 1, (k >> 1) & 1, k & 1
            cp = pltpu.make_async_remote_copy(
                src_ref=v_ref, dst_ref=buf.at[me], send_sem=send_sems.at[k - 1], recv_sem=recv_sems.at[k - 1],
                device_id=(x ^ fx, y ^ fy, c ^ fc), device_id_type=MESH)
            cp.start()
            sends.append(cp)
        for k in range(1, N_DEV):
            pltpu.make_async_remote_copy(
                src_ref=v_ref, dst_ref=buf.at[me ^ k], send_sem=send_sems.at[k - 1], recv_sem=recv_sems.at[k - 1],
                device_id=(x, y, c), device_id_type=MESH).wait_recv()
        acc = buf[0]
        for d in range(1, N_DEV):
            acc = acc + buf[d]
        out_ref[...] = acc
        for cp in sends:
            cp.wait_send()

    vm = pl.BlockSpec(memory_space=pltpu.VMEM)
    return pl.pallas_call(
        body, name="allsum_small", in_specs=[vm], out_specs=vm,
        out_shape=jax.ShapeDtypeStruct((R, LANES), F32),
        scratch_shapes=[pltpu.VMEM((N_DEV, R, LANES), F32), pltpu.SemaphoreType.DMA((N_DEV - 1,)),
                        pltpu.SemaphoreType.DMA((N_DEV - 1,))],
        compiler_params=_params(),
    )(v)


def _slab(flat, mult):
    n = flat.shape[-1]
    rows = -(-n // (LANES * mult)) * mult
    flat = jnp.pad(flat, [(0, 0)] * (flat.ndim - 1) + [(0, rows * LANES - n)])
    return flat.reshape(flat.shape[:-1] + (rows, LANES))


def full_from_chips(blocks, by_col):
    _, r, c = blocks.shape
    return blocks.transpose(1, 0, 2).reshape(r, N_CHIPS * c) if by_col else blocks.reshape(N_CHIPS * r, c)


def chips_from_full(full, by_col):
    if by_col:
        r, c = full.shape[0], full.shape[1] // N_CHIPS
        return full.reshape(r, N_CHIPS, c).transpose(1, 0, 2)
    return full.reshape(N_CHIPS, full.shape[0] // N_CHIPS, full.shape[1])


SMALL_PACK = SMALL_W + ("loss", "conv_w")
SMALL_PACK_N = {**SMALL_N, "loss": 1, "conv_w": 3 * 2 * D_FF}


def pack_small(vals):
    zero = jnp.zeros((1,), F32)
    return _slab(jnp.concatenate([vals[n].reshape(-1) if n in vals else jnp.tile(zero, SMALL_PACK_N[n])
                                  for n in SMALL_PACK]), 8)


def unpack_small(slab, shapes):
    flat = slab.reshape(-1)
    out, off = {}, 0
    for n in SMALL_PACK:
        out[n] = flat[off:off + SMALL_PACK_N[n]].reshape(shapes[n])
        off += SMALL_PACK_N[n]
    return out


def _split_heads(w, a, b):
    r = w.shape[0]
    w3 = w.reshape(r, HEADS, a + b)
    return w3[:, :, :a].reshape(r, HEADS * a), w3[:, :, a:].reshape(r, HEADS * b)


def _merge_heads(wa, wb, a, b):
    r = wa.shape[0]
    return jnp.concatenate([wa.reshape(r, HEADS, a), wb.reshape(r, HEADS, b)], axis=2).reshape(r, HEADS * (a + b))


def kernel(x, positions, g_mix, w_in, g_cq, w_uq, g_ckv, w_ukv, g_sb_out, g_mla_out, w_out, g_ffn, w_up, conv_w, conv_b, w_down, g_final, loss_target, m_g_mix, m_w_in, m_g_cq, m_w_uq, m_g_ckv, m_w_ukv, m_g_sb_out, m_g_mla_out, m_w_out, m_g_ffn, m_w_up, m_conv_w, m_conv_b, m_w_down, m_g_final, v_g_mix, v_w_in, v_g_cq, v_w_uq, v_g_ckv, v_w_ukv, v_g_sb_out, v_g_mla_out, v_w_out, v_g_ffn, v_w_up, v_conv_w, v_conv_b, v_w_down, v_g_final):
    given = dict(locals())
    B, S, _ = x.shape
    T = B * S
    w_big = {n: given[n][0].T if n == "w_in" else given[n][0] for n in BIG_W}
    m_big = {n: given["m_" + n][0].T if n == "w_in" else given["m_" + n][0] for n in BIG_W}
    v_big = {n: given["v_" + n][0].T if n == "w_in" else given["v_" + n][0] for n in BIG_W}
    shard_shape = {n: w_big[n].shape for n in BIG_W}

    first = ("w_in", "w_uq", "w_ukv")
    later = ("w_out", "w_up", "w_down", "conv_w")
    x2d = x.reshape(T, D_MODEL)
    half = ROPE_DIM // 2
    inv_freq = 1.0 / (ROPE_BASE ** (jnp.arange(half, dtype=F32) * (2.0 / ROPE_DIM)))
    h, cos, sin, got_w = norm_mix_rope_call(
        x2d, g_mix, positions.reshape(T, 1), jnp.tile(inv_freq, LANES // half).reshape(1, LANES),
        gather_group([w_big[n].astype(BF16) for n in first]))
    full = {n: full_from_chips(g_, BIG_SHARD[n][2]) for n, g_ in zip(first, got_w) if n != "w_in"}
    gather_later = gather_group([w_big[n] if n == "conv_w" else w_big[n].astype(BF16) for n in later])
    w_in_t = jnp.pad(got_w[0].reshape(IN_COLS, D_MODEL), ((0, IN_COLS_PAD - IN_COLS), (0, 0)))
    w_uq_p = jnp.concatenate(_split_heads(full["w_uq"], HEAD_DIM, ROPE_DIM), axis=1)
    w_ukv_p = jnp.concatenate(_split_heads(full["w_ukv"], HEAD_DIM, HEAD_DIM), axis=1)

    p = matmul_call("proj_in", h, w_in_t, "nt", tn=IN_COLS_PAD // 2)
    qn, qr, kn, vm, krt, cqn, ckvn = mla_prep_fwd_call(p, cos, sin, g_cq, g_ckv, w_uq_p, w_ukv_p)
    o_sb, lt_sb, got_w = sb_fwd_call(p, B, S, ex=gather_later)
    w_up4 = got_w[1]
    full.update({n: full_from_chips(g_, BIG_SHARD[n][2]) for n, g_ in zip(later, got_w) if n != "w_up"})
    conv_w_full = full["conv_w"]
    o_mla, lse = mla_fwd_call(qn, qr, kn, krt, vm, B, S)
    o_cat, x1, hn = proj_out_norm_call(o_sb, o_mla, g_sb_out, g_mla_out, full["w_out"], x2d, g_ffn)
    u_g, u_v, act, c_g, c_v = ffn_up_conv_call(hn, w_up4, conv_w_full, conv_b, S)
    dx2, dx2b, loss_row, dg_final = ffn_down_loss_call(
        act, full["w_down"], x1, g_final.reshape(1, D_MODEL), loss_target.reshape(T, D_MODEL))

    xi, yi, ci = _place()
    chip = (2 * xi + yi).astype(jnp.int32).reshape(1)
    where = jnp.stack([ci, 2 * xi + yi]).astype(jnp.int32)

    def add_halves(names, parts, sib_rows):
        return [add_half_call("add_half_" + n, p_, s_, where) for n, p_, s_ in zip(names, parts, sib_rows)]

    def sum_chips(names, halves, from_chips):
        return [sum_chips_call("sum_chips_" + n, h_[0], f_) for n, h_, f_ in zip(names, halves, from_chips)]

    ffn_w = ("w_down", "w_up")
    parts_ffn = [chips_from_full(wgrad_call("wgrad_down", act, dx2b, tn=512, tt=1024), False)]
    du_g, du_v, dw_up4, dcw_g, dcw_v, dcb_g, dcb_v = conv_bwd_wgrad_up_call(
        hn, u_g, u_v, c_g, c_v, dx2b, full["w_down"], conv_w_full, S)
    parts_ffn.append(dw_up4)
    dx1, dg_ffn, sib_ffn = ffn_up_bwd_call(du_g, du_v, w_up4, x1, g_ffn, dx2, swap_half(parts_ffn))
    parts_out = [chips_from_full(wgrad_call("wgrad_out", o_cat, dx1, tt=1024), False)]
    do_sb, do_mla, dg_sb_out, dg_mla_out, sib_out = proj_out_bwd_call(
        dx1, full["w_out"], o_sb, o_mla, g_sb_out, g_mla_out, swap_half(parts_out))
    early = ffn_w + ("w_out",)
    halves = add_halves(early, parts_ffn + parts_out, list(sib_ffn) + list(sib_out))
    dq_sb, dk_sb, dv_sb, from_chips = sb_bwd_call(p, lt_sb, do_sb, B, S, ex=scatter_half([h_[1] for h_ in halves]))
    finals = sum_chips(early, halves, from_chips)
    dqn, dqr4, dkn, dvm, dkrt4, done = mla_bwd_call(qn, qr, kn, krt, vm, o_mla, lse, do_mla, B, S,
        ex=swap_final(finals, [shard_shape[n] for n in early]))
    grads = dict(zip(early, done))
    dcq, dckvr, dq_cat, dkv_cat, dg_cq, dg_ckv = mla_prep_bwd_call(
        p, cos, sin, g_cq, g_ckv, w_uq_p, w_ukv_p, dqn, dqr4, dkn, dvm, dkrt4)
    dw_uq_p = wgrad_call("wgrad_uq", cqn, dq_cat)
    dw_ukv_p = wgrad_call("wgrad_ukv", ckvn, dkv_cat)
    dp = (dq_sb, dk_sb, dv_sb, dcq, dckvr)
    late = ("w_uq", "w_ukv", "w_in")
    parts_late = [chips_from_full(g_, True) for g_ in (
        _merge_heads(dw_uq_p[:, :ATT_W], dw_uq_p[:, ATT_W:], HEAD_DIM, ROPE_DIM),
        _merge_heads(dw_ukv_p[:, :ATT_W], dw_ukv_p[:, ATT_W:], HEAD_DIM, HEAD_DIM))]
    parts_late.append(chips_from_full(wgrad_in_call(h, dp), False))
    dh, sib_late = proj_in_bwd_call(dp, w_in_t, swap_half(parts_late))
    halves = add_halves(late, parts_late, sib_late)
    grad_x, dg_mix, from_chips = rmsnorm_bwd_call(
        "norm_mix_bwd", x2d, g_mix, dh, dx1, ex=scatter_half([h_[1] for h_ in halves]))
    finals = sum_chips(late, halves, from_chips)
    grads.update(zip(late, exchange_call("swap_final_late", swap_final(finals, [shard_shape[n] for n in late]))))

    shapes = {n: given[n].shape for n in SMALL_W}
    shapes.update(loss=(), conv_w=(3, 2 * D_FF))
    small_g = {"g_mix": dg_mix, "g_cq": dg_cq, "g_ckv": dg_ckv, "g_sb_out": dg_sb_out, "g_mla_out": dg_mla_out,
               "g_ffn": dg_ffn, "conv_b": jnp.concatenate([dcb_g, dcb_v], axis=1), "g_final": dg_final,
               "loss": loss_row[0, :1], "conv_w": jnp.concatenate([dcw_g, dcw_v], axis=1)}
    gs_slab = allsum_small_call(pack_small(small_g))
    small_in = [pack_small({n: given[pre + n] for n in SMALL_W}) for pre in ("", "m_", "v_")]
    small_out = [unpack_small(s, shapes) for s in adamw_call("adamw_small", gs_slab, *small_in)]
    cw_cols = BIG_SHARD["conv_w"][1]
    grads["conv_w"] = lax.dynamic_slice_in_dim(small_out[0]["conv_w"], chip[0] * cw_cols, cw_cols, axis=1)

    big_out = {n: adamw_call("adamw_" + n, grads[n], w_big[n], m_big[n], v_big[n]) for n in BIG_W}
    weights = ("g_mix", "w_in", "g_cq", "w_uq", "g_ckv", "w_ukv", "g_sb_out", "g_mla_out", "w_out", "g_ffn",
               "w_up", "conv_w", "conv_b", "w_down", "g_final")
    outs = [small_out[0]["loss"], grad_x.reshape(B, S, D_MODEL)]
    for k in range(4):
        for n in weights:
            if n in BIG_W:
                outs.append((big_out[n][k].T if n == "w_in" else big_out[n][k])[None])
            else:
                outs.append(small_out[k][n])
    return tuple(outs)
```

```python
import functools

import jax
import jax.numpy as jnp
from jax import lax
from jax.experimental import pallas as pl
from jax.experimental.pallas import tpu as pltpu

F32 = jnp.float32
BF16 = jnp.bfloat16
MESH = pl.DeviceIdType.MESH

D_MODEL = 1024
HEADS = 8
HEAD_DIM = 64
ATT_W = HEADS * HEAD_DIM
ROPE_DIM = 32
ROPE_W = HEADS * ROPE_DIM
QK_DIM = HEAD_DIM + ROPE_DIM
Q_RANK = 384
KV_RANK = 256
D_FF = 2816
IN_COLS = 2208
IN_COLS_PAD = 2304
EPS = 1e-6
ROPE_BASE = 10000.0
SB_SCALE = HEAD_DIM ** -0.5
SB_SCALE2 = SB_SCALE * 1.4426950408889634
MLA_SCALE = QK_DIM ** -0.5
LOG2E = 1.4426950408889634
LN2 = 0.6931471805599453
MLA_SCALE2 = MLA_SCALE * LOG2E
LANES = 128
N_CHIPS = 4
N_DEV = 8
VMEM_LIMIT = 48 * 1024 * 1024
ATT_TQ = 256
ATT_TK = 256
ATT_PAIRS = 4
PAIR_LANES = [slice(i * LANES, (i + 1) * LANES) for i in range(ATT_PAIRS)]
SB_BWD_GROUP = 2
SB_FWD_GROUP = 4
MLA_GROUP = 4
NEG_BIG = -1e30

ADAM_LR = 0.001
ADAM_B1 = 0.9
ADAM_B2 = 0.999
ADAM_EPS = 1e-08
ADAM_WD = 0.01
ADAM_STEP = 10

BIG_W = ("w_in", "w_uq", "w_ukv", "w_out", "w_up", "conv_w", "w_down")
BIG_SHARD = {
    "w_in": (D_MODEL, IN_COLS // 4, True),
    "w_uq": (Q_RANK, HEADS * QK_DIM // 4, True),
    "w_ukv": (KV_RANK, 2 * ATT_W // 4, True),
    "w_out": (2 * ATT_W // 4, D_MODEL, False),
    "w_up": (D_MODEL, 2 * D_FF // 4, True),
    "conv_w": (3, 2 * D_FF // 4, True),
    "w_down": (D_FF // 4, D_MODEL, False),
}
SMALL_W = ("g_mix", "g_cq", "g_ckv", "g_sb_out", "g_mla_out", "g_ffn", "conv_b", "g_final")
SMALL_N = {"g_mix": D_MODEL, "g_cq": Q_RANK, "g_ckv": KV_RANK, "g_sb_out": ATT_W, "g_mla_out": ATT_W,
           "g_ffn": D_MODEL, "conv_b": 2 * D_FF, "g_final": D_MODEL}


def _params(sem=None, **kw):
    return pltpu.CompilerParams(dimension_semantics=sem, vmem_limit_bytes=VMEM_LIMIT, **kw)


def _dot(a, b, dims):
    return lax.dot_general(a, b, (dims, ((), ())), preferred_element_type=F32)


def _nn(a, b):
    return _dot(a, b, ((1,), (0,)))


def _nt(a, b):
    return _dot(a, b, ((1,), (1,)))


def _tn(a, b):
    return _dot(a, b, ((0,), (0,)))


def _split2(x):
    hi = x.astype(BF16)
    lo = (x - hi.astype(F32)).astype(BF16)
    return hi, lo


def _rms_r(x, d):
    return lax.rsqrt(jnp.sum(x * x, axis=-1, keepdims=True) * (1.0 / d) + EPS)


def _rms_bwd(x, g, dy, d):
    r = _rms_r(x, d)
    xhat = x * r
    gy = dy * g
    dx = r * (gy - xhat * (jnp.sum(xhat * gy, axis=-1, keepdims=True) * (1.0 / d)))
    return dx, jnp.sum(dy * xhat, axis=0, keepdims=True)


def _rot(x):
    lane = lax.broadcasted_iota(jnp.int32, x.shape, x.ndim - 1)
    n = x.shape[-1]
    return jnp.where((lane & 31) < 16, pltpu.roll(x, n - 16, x.ndim - 1), pltpu.roll(x, 16, x.ndim - 1))


def _fold4(x):
    return x + pltpu.roll(x, 32, 1) + pltpu.roll(x, 64, 1) + pltpu.roll(x, 96, 1)


def matmul_call(name, a, b, mode, out_dtype=F32, res=None, tm=512, tn=None, ex=None):
    M, K = a.shape
    N = b.shape[1] if mode == "nn" else b.shape[0]
    tn = N if tn is None else tn
    assert M % tm == 0 and N % tn == 0

    def body(*refs):
        if res is None:
            a_ref, b_ref, o_ref = refs
        else:
            a_ref, b_ref, r_ref, o_ref = refs
        av = a_ref[...].astype(BF16)
        bv = b_ref[...].astype(BF16)
        acc = _nn(av, bv) if mode == "nn" else _nt(av, bv)
        if res is not None:
            acc = r_ref[...] + acc
        o_ref[...] = acc.astype(out_dtype)

    in_specs = [pl.BlockSpec((tm, K), lambda j, i: (i, 0))]
    if mode == "nn":
        in_specs.append(pl.BlockSpec((K, tn), lambda j, i: (0, j)))
    else:
        in_specs.append(pl.BlockSpec((tn, K), lambda j, i: (j, 0)))
    args = [a, b]
    if res is not None:
        in_specs.append(pl.BlockSpec((tm, tn), lambda j, i: (i, j)))
        args.append(res)
    outs, moved = _call(body, ex, name=name, grid=(N // tn, M // tm), in_specs=in_specs,
                        out_specs=[pl.BlockSpec((tm, tn), lambda j, i: (i, j))],
                        out_shape=[jax.ShapeDtypeStruct((M, N), out_dtype)], args=args)
    return outs[0] if ex is None else (outs[0], moved)


def _rows(tm, width):
    return pl.BlockSpec((tm, width), lambda i: (i, 0))


def _whole(a, **kw):
    return pl.BlockSpec(a.shape, lambda i: (0,) * a.ndim, **kw)


def proj_out_norm_call(o_sb, o_mla, g_sb, g_mla, w_out, x, g_ffn, tm=512):
    T = o_sb.shape[0]
    N = w_out.shape[1]

    def body(a_ref, b_ref, ga_ref, gb_ref, w_ref, x_ref, g_ref, oc_ref, x1_ref, hn_ref):
        a = a_ref[...]
        b = b_ref[...]
        ya = (a * _rms_r(a, ATT_W)) * ga_ref[...]
        yb = (b * _rms_r(b, ATT_W)) * gb_ref[...]
        o_cat = jnp.concatenate([ya, yb], axis=1).astype(BF16)
        oc_ref[...] = o_cat
        x1 = x_ref[...] + _nn(o_cat, w_ref[...])
        x1_ref[...] = x1
        hn_ref[...] = ((x1 * _rms_r(x1, N)) * g_ref[...]).astype(BF16)

    return pl.pallas_call(
        body, name="proj_out", grid=(T // tm,),
        in_specs=[_rows(tm, ATT_W), _rows(tm, ATT_W), _whole(g_sb), _whole(g_mla), _whole(w_out), _rows(tm, N),
                  _whole(g_ffn)],
        out_specs=[_rows(tm, 2 * ATT_W), _rows(tm, N), _rows(tm, N)],
        out_shape=[jax.ShapeDtypeStruct((T, 2 * ATT_W), BF16), jax.ShapeDtypeStruct((T, N), F32),
                   jax.ShapeDtypeStruct((T, N), BF16)],
        compiler_params=_params(("parallel",)),
    )(o_sb, o_mla, g_sb, g_mla, w_out, x, g_ffn)


def norm_mix_rope_call(x, g, pos, inv_freq, ex, tm=512):
    T, d = x.shape

    def body(x_ref, g_ref, p_ref, f_ref, o_ref, c_ref, s_ref):
        xv = x_ref[...]
        o_ref[...] = ((xv * _rms_r(xv, d)) * g_ref[...]).astype(BF16)
        ang = p_ref[...].astype(F32) * f_ref[...]
        lane = lax.broadcasted_iota(jnp.int32, ang.shape, 1)
        sn = jnp.sin(ang)
        c_ref[...] = jnp.cos(ang)
        s_ref[...] = jnp.where((lane & 31) < 16, -sn, sn)

    outs, moved = _call(
        body, ex, name="norm_mix", grid=(T // tm,),
        in_specs=[_rows(tm, d), _whole(g), _rows(tm, 1), _whole(inv_freq)],
        out_specs=[_rows(tm, d), _rows(tm, LANES), _rows(tm, LANES)],
        out_shape=[jax.ShapeDtypeStruct((T, d), BF16), jax.ShapeDtypeStruct((T, LANES), F32),
                   jax.ShapeDtypeStruct((T, LANES), F32)], args=(x, g, pos, inv_freq))
    return tuple(outs) + (moved,)


def ffn_down_loss_call(act, w_down, x1, g, target, tm=512):
    T, K = act.shape
    d = w_down.shape[1]

    def body(a_ref, w_ref, x1_ref, g_ref, t_ref, dx_ref, dxb_ref, loss_ref, dg_ref):
        @pl.when(pl.program_id(0) == 0)
        def _():
            loss_ref[...] = jnp.zeros_like(loss_ref)
            dg_ref[...] = jnp.zeros_like(dg_ref)

        x = x1_ref[...] + _nn(a_ref[...], w_ref[...])
        g = g_ref[...]
        y = (x * _rms_r(x, d)) * g
        err = y - t_ref[...]
        loss_ref[...] += jnp.sum(jnp.sum(err * err, axis=1, keepdims=True), axis=0, keepdims=True) * (0.5 / d)
        dx, dg = _rms_bwd(x, g, err * (1.0 / d), d)
        dx_ref[...] = dx
        dxb_ref[...] = dx.astype(BF16)
        dg_ref[...] += dg

    return pl.pallas_call(
        body, name="ffn_down_loss", grid=(T // tm,),
        in_specs=[_rows(tm, K), _whole(w_down), _rows(tm, d), _whole(g), _rows(tm, d)],
        out_specs=[_rows(tm, d), _rows(tm, d), pl.BlockSpec((1, LANES), lambda i: (0, 0)), _whole(g)],
        out_shape=[jax.ShapeDtypeStruct((T, d), F32), jax.ShapeDtypeStruct((T, d), BF16),
                   jax.ShapeDtypeStruct((1, LANES), F32), jax.ShapeDtypeStruct((1, d), F32)],
        compiler_params=_params(("arbitrary",)),
    )(act, w_down, x1, g, target)


def proj_out_bwd_call(dx1, w_out, o_sb, o_mla, g_sb, g_mla, ex, tm=512):
    T, N = dx1.shape

    def body(d_ref, w_ref, a_ref, b_ref, ga_ref, gb_ref, da_ref, db_ref, dga_ref, dgb_ref):
        @pl.when(pl.program_id(0) == 0)
        def _():
            dga_ref[...] = jnp.zeros_like(dga_ref)
            dgb_ref[...] = jnp.zeros_like(dgb_ref)

        d = _nt(d_ref[...].astype(BF16), w_ref[...])
        da, dga = _rms_bwd(a_ref[...], ga_ref[...], d[:, :ATT_W], ATT_W)
        db, dgb = _rms_bwd(b_ref[...], gb_ref[...], d[:, ATT_W:], ATT_W)
        da_ref[...] = da
        db_ref[...] = db
        dga_ref[...] += dga
        dgb_ref[...] += dgb

    outs, moved = _call(
        body, ex, name="proj_out_bwd", grid=(T // tm,),
        in_specs=[_rows(tm, N), _whole(w_out), _rows(tm, ATT_W), _rows(tm, ATT_W), _whole(g_sb), _whole(g_mla)],
        out_specs=[_rows(tm, ATT_W), _rows(tm, ATT_W), _whole(g_sb), _whole(g_mla)],
        out_shape=[jax.ShapeDtypeStruct((T, ATT_W), F32), jax.ShapeDtypeStruct((T, ATT_W), F32),
                   jax.ShapeDtypeStruct((1, ATT_W), F32), jax.ShapeDtypeStruct((1, ATT_W), F32)],
        args=(dx1, w_out, o_sb, o_mla, g_sb, g_mla))
    return tuple(outs) + (moved,)


def wgrad_call(name, a, b, tn=None, tt=512, by_chip=False):
    T, M = a.shape
    N = b.shape[1]
    tn = N if tn is None else tn
    tt = min(tt, T)
    assert T % tt == 0 and N % tn == 0
    if by_chip:
        out_spec = pl.BlockSpec((None, M, tn), lambda j, t: (j, 0, 0))
        out_shape = jax.ShapeDtypeStruct((N // tn, M, tn), F32)
    else:
        out_spec = pl.BlockSpec((M, tn), lambda j, t: (0, j))
        out_shape = jax.ShapeDtypeStruct((M, N), F32)

    def body(a_ref, b_ref, o_ref):
        @pl.when(pl.program_id(1) == 0)
        def _():
            o_ref[...] = jnp.zeros_like(o_ref)

        o_ref[...] += _tn(a_ref[...].astype(BF16), b_ref[...].astype(BF16))

    return pl.pallas_call(
        body, name=name, grid=(N // tn, T // tt),
        in_specs=[pl.BlockSpec((tt, M), lambda j, t: (t, 0)), pl.BlockSpec((tt, tn), lambda j, t: (t, j))],
        out_specs=out_spec, out_shape=out_shape,
        compiler_params=_params(("parallel", "arbitrary")),
    )(a, b)


UP_COLS = 2 * D_FF // N_CHIPS


def _shift_down_after(u, prev8, n):
    top = pltpu.roll(jnp.concatenate([prev8, u[0:8]], axis=0), n, 0)[8:16]
    return jnp.concatenate([top, pltpu.roll(u, n, 0)[8:]], axis=0)


def ffn_up_conv_call(hn, w4, conv_w, conv_b, S, tm=512):
    T, K = hn.shape
    per_seq = S // tm

    def body(a_ref, wg_ref, wv_ref, cg_ref, cv_ref, bg_ref, bv_ref, ug_ref, uv_ref, act_ref, og_ref, ov_ref, pg_ref, pv_ref):
        @pl.when(pl.program_id(1) % per_seq == 0)
        def _():
            pg_ref[...] = jnp.zeros_like(pg_ref)
            pv_ref[...] = jnp.zeros_like(pv_ref)

        a = a_ref[...]
        outs = []
        for w_ref, c_ref, b_ref, u_ref, prev_ref in ((wg_ref, cg_ref, bg_ref, ug_ref, pg_ref),
                                                     (wv_ref, cv_ref, bv_ref, uv_ref, pv_ref)):
            u = _nn(a, w_ref[...])
            u_ref[...] = u
            prev8 = prev_ref[...]
            taps = _taps(c_ref)
            outs.append(taps[0] * _shift_down_after(u, prev8, 2) + taps[1] * _shift_down_after(u, prev8, 1)
                        + taps[2] * u + b_ref[...])
            prev_ref[...] = u[tm - 8:, :]
        gate, val = outs
        sig = 1.0 / (1.0 + jnp.exp(-gate))
        silu = gate * sig
        act_ref[...] = (silu * val).astype(BF16)
        og_ref[...] = silu.astype(BF16)
        ov_ref[...] = (val * (sig * (1.0 + gate * (1.0 - sig)))).astype(BF16)

    out = pl.BlockSpec((tm, UP_COLS), lambda j, i: (i, j))

    def cols(rows, off):
        return pl.BlockSpec((rows, UP_COLS), lambda j, i: (0, off + j))

    return pl.pallas_call(
        body, name="ffn_up_conv", grid=(2, T // tm),
        in_specs=[pl.BlockSpec((tm, K), lambda j, i: (i, 0)),
                  pl.BlockSpec((None, K, UP_COLS), lambda j, i: (j, 0, 0)),
                  pl.BlockSpec((None, K, UP_COLS), lambda j, i: (2 + j, 0, 0)),
                  cols(3, 0), cols(3, 2), cols(1, 0), cols(1, 2)],
        out_specs=[out] * 5,
        out_shape=[jax.ShapeDtypeStruct((T, D_FF), F32)] * 2 + [jax.ShapeDtypeStruct((T, D_FF), BF16)] * 3,
        scratch_shapes=[pltpu.VMEM((8, UP_COLS), F32)] * 2,
        compiler_params=_params(("arbitrary", "arbitrary")),
    )(hn, w4, w4, conv_w, conv_w, conv_b, conv_b)


def _shift_up_before(d, next8, n):
    rows = d.shape[0]
    bottom = pltpu.roll(jnp.concatenate([d[rows - 8:], next8], axis=0), 16 - n, 0)[0:8]
    return jnp.concatenate([pltpu.roll(d, rows - n, 0)[:rows - 8], bottom], axis=0)


def conv_bwd_wgrad_up_call(hn, ug, uv, cg, cv, dx2, w_down, conv_w, S, tt=256):
    T, M = hn.shape
    nt = T // tt
    per_seq = S // tt

    def body(a_ref, ug_ref, uv_ref, cg_ref, cv_ref, dx_ref, wd_ref, wg_ref, wv_ref,
             dug_ref, duv_ref, dw_ref, dwg_ref, dwv_ref, dbg_ref, dbv_ref, ng_ref, nv_ref):
        step = pl.program_id(1)

        @pl.when(step == 0)
        def _():
            for r in (dw_ref, dwg_ref, dwv_ref, dbg_ref, dbv_ref):
                r[...] = jnp.zeros_like(r)

        @pl.when((nt - 1 - step) % per_seq == per_seq - 1)
        def _():
            ng_ref[...] = jnp.zeros_like(ng_ref)
            nv_ref[...] = jnp.zeros_like(nv_ref)

        da = _nt(dx_ref[...], wd_ref[...])
        dval = da * cg_ref[...].astype(F32)
        dgate = da * cv_ref[...].astype(F32)
        a = a_ref[...]
        for k, (u_ref, d, w, du_ref, dcw_ref, dcb_ref, next_ref) in enumerate((
                (ug_ref, dgate, _taps(wg_ref), dug_ref, dwg_ref, dbg_ref, ng_ref),
                (uv_ref, dval, _taps(wv_ref), duv_ref, dwv_ref, dbv_ref, nv_ref))):
            u_ = u_ref[...]
            next8 = next_ref[...]
            d1 = _shift_up_before(d, next8, 1)
            d2 = _shift_up_before(d, next8, 2)
            du = (w[2] * d + w[1] * d1 + w[0] * d2).astype(BF16)
            du_ref[...] = du
            dw_ref[k] += _tn(a, du)
            dcb_ref[...] += jnp.sum(d, axis=0, keepdims=True)
            dcw_ref[0:1, :] += jnp.sum(d2 * u_, axis=0, keepdims=True)
            dcw_ref[1:2, :] += jnp.sum(d1 * u_, axis=0, keepdims=True)
            dcw_ref[2:3, :] += jnp.sum(d * u_, axis=0, keepdims=True)
            next_ref[...] = d[0:8, :]

    def tok(width, by_col):
        return pl.BlockSpec((tt, width), (lambda j, i: (nt - 1 - i, j)) if by_col else (lambda j, i: (nt - 1 - i, 0)))

    def cols(rows, off):
        return pl.BlockSpec((rows, UP_COLS), lambda j, i: (0, off + j))

    outs = pl.pallas_call(
        body, name="conv_bwd_wgrad_up", grid=(2, nt),
        in_specs=[tok(M, False), tok(UP_COLS, True), tok(UP_COLS, True), tok(UP_COLS, True), tok(UP_COLS, True),
                  tok(dx2.shape[1], False), pl.BlockSpec((UP_COLS, w_down.shape[1]), lambda j, i: (j, 0)),
                  cols(3, 0), cols(3, 2)],
        out_specs=[tok(UP_COLS, True), tok(UP_COLS, True),
                   pl.BlockSpec((2, None, M, UP_COLS), lambda j, i: (0, j, 0, 0)),
                   cols(3, 0), cols(3, 0), cols(1, 0), cols(1, 0)],
        out_shape=[jax.ShapeDtypeStruct((T, D_FF), BF16), jax.ShapeDtypeStruct((T, D_FF), BF16),
                   jax.ShapeDtypeStruct((2, 2, M, UP_COLS), F32),
                   jax.ShapeDtypeStruct((3, D_FF), F32), jax.ShapeDtypeStruct((3, D_FF), F32),
                   jax.ShapeDtypeStruct((1, D_FF), F32), jax.ShapeDtypeStruct((1, D_FF), F32)],
        scratch_shapes=[pltpu.VMEM((8, UP_COLS), F32)] * 2,
        compiler_params=_params(("arbitrary", "arbitrary")),
    )(hn, ug, uv, cg, cv, dx2, w_down, conv_w, conv_w)
    return (outs[0], outs[1], outs[2].reshape(N_CHIPS, M, UP_COLS)) + tuple(outs[3:])


def ffn_up_bwd_call(du_g, du_v, w4, x1, g_ffn, dx2, ex, tm=512):
    T = du_g.shape[0]
    N = w4.shape[1]

    def body(g_ref, v_ref, w_ref, x1_ref, gf_ref, dx2_ref, dx1_ref, dg_ref):
        @pl.when(pl.program_id(0) == 0)
        def _():
            dg_ref[...] = jnp.zeros_like(dg_ref)

        dhn = _nt(g_ref[:, :UP_COLS], w_ref[0]) + _nt(g_ref[:, UP_COLS:], w_ref[1])
        dhn = dhn + _nt(v_ref[:, :UP_COLS], w_ref[2]) + _nt(v_ref[:, UP_COLS:], w_ref[3])
        dx, dg = _rms_bwd(x1_ref[...], gf_ref[...], dhn, N)
        dx1_ref[...] = dx2_ref[...] + dx
        dg_ref[...] += dg

    outs, moved = _call(
        body, ex, name="ffn_up_bwd", grid=(T // tm,),
        in_specs=[_rows(tm, D_FF), _rows(tm, D_FF), _whole(w4, pipeline_mode=pl.Buffered(1)), _rows(tm, N),
                  _whole(g_ffn), _rows(tm, N)],
        out_specs=[_rows(tm, N), _whole(g_ffn)],
        out_shape=[jax.ShapeDtypeStruct((T, N), F32), jax.ShapeDtypeStruct((1, N), F32)],
        args=(du_g, du_v, w4, x1, g_ffn, dx2))
    return outs[0], outs[1], moved


IN_PIECES = ((0, ATT_W), (ATT_W, ATT_W), (2 * ATT_W, ATT_W), (3 * ATT_W, Q_RANK), (3 * ATT_W + Q_RANK, Q_RANK))


def proj_in_bwd_call(pieces, w_in_t, ex, tm=512):
    T = pieces[0].shape[0]
    N = w_in_t.shape[1]
    n = len(pieces)

    def body(*refs):
        o_ref = refs[2 * n]
        acc = _nn(refs[0][...].astype(BF16), refs[n][...])
        for i in range(1, n):
            acc = acc + _nn(refs[i][...].astype(BF16), refs[n + i][...])
        o_ref[...] = acc

    outs, moved = _call(
        body, ex, name="proj_in_bwd", grid=(T // tm,),
        in_specs=[pl.BlockSpec((tm, w), lambda i: (i, 0)) for _, w in IN_PIECES]
        + [pl.BlockSpec((w, N), functools.partial(lambda c, i: (c, 0), off // w)) for off, w in IN_PIECES],
        out_specs=[pl.BlockSpec((tm, N), lambda i: (i, 0))],
        out_shape=[jax.ShapeDtypeStruct((T, N), F32)], args=tuple(pieces) + (w_in_t,) * n)
    return outs[0], moved


def wgrad_in_call(h, pieces, tt=512):
    T, M = h.shape
    n = len(pieces)

    def body(*refs):
        a_ref, o_ref = refs[0], refs[n + 1]

        @pl.when(pl.program_id(0) == 0)
        def _():
            o_ref[...] = jnp.zeros_like(o_ref)

        a = a_ref[...]
        for i, (off, w) in enumerate(IN_PIECES):
            rows = min(w, IN_COLS - off)
            o_ref[off:off + rows, :] += _tn(refs[1 + i][...].astype(BF16), a)[:rows]

    return pl.pallas_call(
        body, name="wgrad_in", grid=(T // tt,),
        in_specs=[pl.BlockSpec((tt, M), lambda t: (t, 0))] + [pl.BlockSpec((tt, w), lambda t: (t, 0)) for _, w in IN_PIECES],
        out_specs=pl.BlockSpec((IN_COLS, M), lambda t: (0, 0)),
        out_shape=jax.ShapeDtypeStruct((IN_COLS, M), F32),
        compiler_params=_params(("arbitrary",)),
    )(h, *pieces)


def rmsnorm_bwd_call(name, x, g, dy, res, tm=512, ex=None):
    T, d = x.shape

    def body(x_ref, g_ref, dy_ref, r_ref, dx_ref, dg_ref):
        @pl.when(pl.program_id(0) == 0)
        def _():
            dg_ref[...] = jnp.zeros_like(dg_ref)

        dx, dg = _rms_bwd(x_ref[...], g_ref[...], dy_ref[...], d)
        dx_ref[...] = r_ref[...] + dx
        dg_ref[...] += dg

    row = pl.BlockSpec((tm, d), lambda i: (i, 0))
    vec = pl.BlockSpec((1, d), lambda i: (0, 0))
    outs, moved = _call(body, ex, name=name, grid=(T // tm,), in_specs=[row, vec, row, row], out_specs=[row, vec],
                        out_shape=[jax.ShapeDtypeStruct((T, d), F32), jax.ShapeDtypeStruct((1, d), F32)],
                        args=(x, g, dy, res))
    return tuple(outs) if ex is None else tuple(outs) + (moved,)


def mla_prep_fwd_call(p, cos, sin, g_cq, g_ckv, w_uq_p, w_ukv_p, tm=512):
    T = p.shape[0]

    def body(cq_ref, ckvr_ref, c_ref, s_ref, gq_ref, gkv_ref, wq_ref, wkv_ref,
             qn_ref, qr_ref, kn_ref, vm_ref, krt_ref, cqn_ref, ckvn_ref):
        c = c_ref[...]
        s = s_ref[...]
        cq = cq_ref[...]
        cqn = ((cq * _rms_r(cq, Q_RANK)) * gq_ref[...]).astype(BF16)
        cqn_ref[...] = cqn
        q = _nn(cqn, wq_ref[...])
        qn_ref[...] = q[:, :ATT_W].astype(BF16)
        for g in range(ROPE_W // LANES):
            qr = q[:, ATT_W + g * LANES:ATT_W + (g + 1) * LANES]
            qr_ref[:, g * LANES:(g + 1) * LANES] = (qr * c + _rot(qr) * s).astype(BF16)
        ckvr = ckvr_ref[...]
        ckv = ckvr[:, :KV_RANK]
        ckvn = ((ckv * _rms_r(ckv, KV_RANK)) * gkv_ref[...]).astype(BF16)
        ckvn_ref[...] = ckvn
        kv = _nn(ckvn, wkv_ref[...])
        kn_ref[...] = kv[:, :ATT_W].astype(BF16)
        vm_ref[...] = kv[:, ATT_W:].astype(BF16)
        kr = _fold4(ckvr[:, KV_RANK:])
        krt_ref[...] = (kr * c + _rot(kr) * s).astype(BF16)

    def row(w, j=0):
        return pl.BlockSpec((tm, w), lambda i: (i, j))

    def full(a):
        return pl.BlockSpec(a.shape, lambda i: (0, 0))

    return pl.pallas_call(
        body, name="mla_prep_fwd", grid=(T // tm,),
        in_specs=[row(Q_RANK, 4), row(Q_RANK, 5), row(LANES), row(LANES), full(g_cq), full(g_ckv),
                  full(w_uq_p), full(w_ukv_p)],
        out_specs=[row(ATT_W), row(ROPE_W), row(ATT_W), row(ATT_W), row(LANES), row(Q_RANK), row(KV_RANK)],
        out_shape=[jax.ShapeDtypeStruct((T, w), BF16) for w in (ATT_W, ROPE_W, ATT_W, ATT_W, LANES, Q_RANK, KV_RANK)],
        compiler_params=_params(("parallel",)),
    )(p, p, cos, sin, g_cq, g_ckv, w_uq_p, w_ukv_p)


def mla_prep_bwd_call(p, cos, sin, g_cq, g_ckv, w_uq_p, w_ukv_p, dqn, dqr4, dkn, dvm, dkrt4, tm=512):
    T = p.shape[0]

    def body(cq_ref, ckvr_ref, c_ref, s_ref, gq_ref, gkv_ref, wq_ref, wkv_ref,
             dqn_ref, dqr4_ref, dkn_ref, dvm_ref, dkrt4_ref,
             dcq_ref, dckvr_ref, dq_ref, dkv_ref, dgq_ref, dgkv_ref):
        @pl.when(pl.program_id(0) == 0)
        def _():
            dgq_ref[...] = jnp.zeros_like(dgq_ref)
            dgkv_ref[...] = jnp.zeros_like(dgkv_ref)

        c = c_ref[...]
        s = s_ref[...]
        d4 = dqr4_ref[...]
        dqr = [d4[:, :128] + d4[:, 128:256], d4[:, 256:384] + d4[:, 384:]]
        dqr = [t * c + _rot(t * s) for t in dqr]
        dq = jnp.concatenate([dqn_ref[...]] + dqr, axis=1).astype(BF16)
        dq_ref[...] = dq
        dcq, dgq = _rms_bwd(cq_ref[...], gq_ref[...], _nt(dq, wq_ref[...]), Q_RANK)
        dcq_ref[...] = dcq
        dgq_ref[...] += dgq
        dkv = jnp.concatenate([dkn_ref[...], dvm_ref[...]], axis=1).astype(BF16)
        dkv_ref[...] = dkv
        ckvr = ckvr_ref[...]
        dckv, dgkv = _rms_bwd(ckvr[:, :KV_RANK], gkv_ref[...], _nt(dkv, wkv_ref[...]), KV_RANK)
        dgkv_ref[...] += dgkv
        k4 = dkrt4_ref[...]
        dkr = _fold4(k4[:, :128] + k4[:, 128:256] + k4[:, 256:384] + k4[:, 384:])
        dkr = dkr * c + _rot(dkr * s)
        lane = lax.broadcasted_iota(jnp.int32, dkr.shape, 1)
        dckvr_ref[...] = jnp.concatenate([dckv, jnp.where(lane < ROPE_DIM, dkr, 0.0)], axis=1)

    def row(w, j=0):
        return pl.BlockSpec((tm, w), lambda i: (i, j))

    def full(a):
        return pl.BlockSpec(a.shape, lambda i: (0, 0))

    return pl.pallas_call(
        body, name="mla_prep_bwd", grid=(T // tm,),
        in_specs=[row(Q_RANK, 4), row(Q_RANK, 5), row(LANES), row(LANES), full(g_cq), full(g_ckv),
                  full(w_uq_p), full(w_ukv_p), row(ATT_W), row(ATT_W), row(ATT_W), row(ATT_W), row(ATT_W)],
        out_specs=[row(Q_RANK), row(Q_RANK), row(ATT_W + ROPE_W), row(2 * ATT_W),
                   pl.BlockSpec((1, Q_RANK), lambda i: (0, 0)), pl.BlockSpec((1, KV_RANK), lambda i: (0, 0))],
        out_shape=[jax.ShapeDtypeStruct((T, Q_RANK), F32), jax.ShapeDtypeStruct((T, Q_RANK), F32),
                   jax.ShapeDtypeStruct((T, ATT_W + ROPE_W), BF16), jax.ShapeDtypeStruct((T, 2 * ATT_W), BF16),
                   jax.ShapeDtypeStruct((1, Q_RANK), F32), jax.ShapeDtypeStruct((1, KV_RANK), F32)],
        compiler_params=_params(("arbitrary",)),
    )(p, p, cos, sin, g_cq, g_ckv, w_uq_p, w_ukv_p, dqn, dqr4, dkn, dvm, dkrt4)


def _iota2(shape, axis):
    return lax.broadcasted_iota(jnp.int32, shape, axis)


def _head_masks():
    lane = _iota2((1, LANES), 1)
    return lane < HEAD_DIM, lane >= HEAD_DIM


def _pair(x, masks, dtype=BF16):
    return [jnp.where(m, x, 0.0).astype(dtype) for m in masks]


def _log_gates(z):
    keep = jnp.maximum(z, 0.0) + jnp.log2(1.0 + jnp.exp2(-jnp.abs(z)))
    return z - keep, keep


def _last_row(x):
    return _row_of(x[x.shape[0] - 8:, :], 7)


def _lane_selector(group):
    return jnp.where(_iota2((16, LANES), 1) // group == _iota2((16, LANES), 0), 1.0, 0.0).astype(BF16)


def _rows8(sel_t, x):
    hi = x.astype(BF16)
    r1 = x - hi.astype(F32)
    mid = r1.astype(BF16)
    lo = (r1 - mid.astype(F32)).astype(BF16)
    return _nt(sel_t, hi) + _nt(sel_t, mid) + _nt(sel_t, lo)


def _row_of(x8, j):
    return jnp.sum(jnp.where(_iota2(x8.shape, 0) == j, x8, 0.0), axis=0, keepdims=True)


def sb_fwd_call(p, B, S, ex=None):
    T = B * S
    TQ, TK = ATT_TQ, ATT_TK
    nq = S // TQ

    def body(q_ref, k_ref, v_ref, o_ref, lt_ref):
        qi = pl.program_id(2)
        masks = _head_masks()
        qm = [_pair(q_ref[:, sl] * SB_SCALE2, masks) for sl in PAIR_LANES]
        row = _iota2((TQ, TK), 0)
        col = _iota2((TQ, TK), 1)
        tri = jnp.where(row > col, 1.0, 0.0).astype(BF16)
        tri2 = jnp.concatenate([tri, tri], axis=0)
        vis = col < row
        o_ref[...] = jnp.zeros_like(o_ref)

        def group(k0, pairs, carry, diag):
            heads = [(pp, j) for pp in pairs for j in range(2)]
            n = range(len(heads))
            k = {pp: k_ref[pl.ds(k0, TK), PAIR_LANES[pp]].astype(BF16) for pp in pairs}
            vm = {pp: _pair(v_ref[pl.ds(k0, TK), PAIR_LANES[pp]], masks) for pp in pairs}
            z = [_nt(qm[pp][j], k[pp]) for pp, j in heads]
            if diag:
                z = [jnp.where(vis, x, NEG_BIG) for x in z]
            gates = [_log_gates(x) for x in z]
            lb = [g[0] for g in gates]
            keep = [g[1] for g in gates]
            tail = [_nn(jnp.concatenate(_split2(keep[h]), axis=1), tri2) + carry[h] for h in n]
            a = [jnp.exp2(lb[h] - tail[h]) for h in n]
            ab = [x.astype(BF16) for x in a]
            for i, pp in enumerate(pairs):
                o_ref[:, PAIR_LANES[pp]] += _nn(ab[2 * i], vm[pp][0]) + _nn(ab[2 * i + 1], vm[pp][1])
            return [tail[h][:, 0:1] + keep[h][:, 0:1] for h in n]

        def step(kb, carry, diag):
            k0 = pl.multiple_of(kb * TK, TK)
            out = []
            for g in range(0, ATT_PAIRS, SB_FWD_GROUP):
                out += group(k0, list(range(g, g + SB_FWD_GROUP)), carry[2 * g:2 * (g + SB_FWD_GROUP)], diag)
            return tuple(out)

        zero = jnp.zeros((TQ, 1), F32)
        carry = step(qi, (zero,) * (2 * ATT_PAIRS), True)
        carry = lax.fori_loop(0, qi, lambda i, c: step(qi - 1 - i, c, False), carry)
        lane = _iota2((TQ, LANES), 1)
        for pp, sl in enumerate(PAIR_LANES):
            lt_ref[:, sl] = jnp.where(lane == 0, carry[2 * pp], jnp.where(lane == 1, carry[2 * pp + 1], 0.0))

    W = ATT_PAIRS * LANES
    qspec = pl.BlockSpec((TQ, W), lambda b, h, i: (b * nq + i, h))
    outs, moved = _call(
        body, ex, name="sb_fwd", grid=(B, HEADS // 2 // ATT_PAIRS, nq),
        in_specs=[qspec,
                  pl.BlockSpec((S, W), lambda b, h, i: (b, ATT_W // W + h)),
                  pl.BlockSpec((S, W), lambda b, h, i: (b, 2 * ATT_W // W + h))],
        out_specs=[qspec, qspec],
        out_shape=[jax.ShapeDtypeStruct((T, ATT_W), F32)] * 2, args=(p, p, p))
    return tuple(outs) if ex is None else tuple(outs) + (moved,)


def sb_bwd_call(p, lt, do, B, S, ex=None):
    T = B * S
    TQ, TK = ATT_TQ, ATT_TK
    nq = S // TQ

    def body(q_ref, k_ref, v_ref, lt_ref, do_ref, dq_ref, dk_ref, dv_ref):
        qi = pl.program_id(2)

        @pl.when(qi == 0)
        def _():
            dk_ref[...] = jnp.zeros_like(dk_ref)
            dv_ref[...] = jnp.zeros_like(dv_ref)

        masks = _head_masks()
        qm = [_pair(q_ref[:, sl] * SB_SCALE2, masks) for sl in PAIR_LANES]
        dom = [_pair(do_ref[:, sl], masks) for sl in PAIR_LANES]
        start = []
        for sl in PAIR_LANES:
            l8 = _rows8(_lane_selector(1), lt_ref[:, sl])
            start += [-_row_of(l8, 0), jnp.zeros((1, TQ), F32), -_row_of(l8, 1), jnp.zeros((1, TQ), F32)]
        row = _iota2((TK, TQ), 0)
        col = _iota2((TK, TQ), 1)
        incl = jnp.where(col <= row, 1.0, 0.0).astype(BF16)
        incl2 = jnp.concatenate([incl, incl], axis=1)
        excl = jnp.where(col < row, 1.0, 0.0).astype(BF16)
        vis = row < col
        dq_ref[...] = jnp.zeros_like(dq_ref)

        def group(k0, pairs, carry, diag):
            heads = [(pp, j) for pp in pairs for j in range(2)]
            n = range(len(heads))
            kf = {pp: k_ref[pl.ds(k0, TK), PAIR_LANES[pp]] for pp in pairs}
            km = {pp: _pair(kf[pp], masks) for pp in pairs}
            v = {pp: v_ref[pl.ds(k0, TK), PAIR_LANES[pp]].astype(BF16) for pp in pairs}
            z = [_nt(kf[pp].astype(BF16), qm[pp][j]) for pp, j in heads]
            da = [_nt(v[pp], dom[pp][j]) for pp, j in heads]
            if diag:
                z = [jnp.where(vis, x, NEG_BIG) for x in z]
            gates = [_log_gates(x) for x in z]
            lb = [g[0] for g in gates]
            keep = [g[1] for g in gates]
            left = [_nn(incl2, jnp.concatenate(_split2(keep[h]), axis=0)) + carry[2 * h] for h in n]
            a = [jnp.exp2(lb[h] + left[h]) for h in n]
            e = [a[h] * da[h] for h in n]
            before = [_nn(excl, e[h].astype(BF16)) + carry[2 * h + 1] for h in n]
            dz = [e[h] - jnp.exp2(lb[h]) * (e[h] + before[h]) for h in n]
            dzb = [x.astype(BF16) for x in dz]
            ab = [x.astype(BF16) for x in a]
            out = []
            for h in n:
                out += [_last_row(left[h]), _last_row(before[h]) + _last_row(e[h])]
            for i, pp in enumerate(pairs):
                sl = PAIR_LANES[pp]
                dk_ref[pl.ds(k0, TK), sl] += _nn(dzb[2 * i], qm[pp][0]) + _nn(dzb[2 * i + 1], qm[pp][1])
                dv_ref[pl.ds(k0, TK), sl] += _nn(ab[2 * i], dom[pp][0]) + _nn(ab[2 * i + 1], dom[pp][1])
                dq_ref[:, sl] += _tn(dzb[2 * i], km[pp][0]) + _tn(dzb[2 * i + 1], km[pp][1])
            return out

        def step(kb, carry, diag):
            k0 = pl.multiple_of(kb * TK, TK)
            out = []
            for g in range(0, ATT_PAIRS, SB_BWD_GROUP):
                out += group(k0, list(range(g, g + SB_BWD_GROUP)), carry[4 * g:4 * (g + SB_BWD_GROUP)], diag)
            return tuple(out)

        carry = lax.fori_loop(0, qi, lambda i, c: step(i, c, False), tuple(start))
        step(qi, carry, True)
        dq_ref[...] *= SB_SCALE

        @pl.when(qi == nq - 1)
        def _():
            dk_ref[...] *= LN2

    W = ATT_PAIRS * LANES
    qspec = pl.BlockSpec((TQ, W), lambda b, h, i: (b * nq + i, h))
    sspec = pl.BlockSpec((S, W), lambda b, h, i: (b, h))
    outs, moved = _call(
        body, ex, name="sb_bwd", grid=(B, HEADS // 2 // ATT_PAIRS, nq),
        in_specs=[qspec,
                  pl.BlockSpec((S, W), lambda b, h, i: (b, ATT_W // W + h)),
                  pl.BlockSpec((S, W), lambda b, h, i: (b, 2 * ATT_W // W + h)),
                  qspec, qspec],
        out_specs=[qspec, sspec, sspec],
        out_shape=[jax.ShapeDtypeStruct((T, ATT_W), F32)] * 3, args=(p, p, p, lt, do))
    return tuple(outs) if ex is None else tuple(outs) + (moved,)


ALL_PAIRS = [slice(i * LANES, (i + 1) * LANES) for i in range(HEADS // 2)]


def _rope_masks(hp):
    grp = _iota2((1, LANES), 1) // ROPE_DIM
    return [grp == ((2 * hp + j) % 4) for j in range(2)]


def _mla_queries(qn_ref, qr_ref, masks):
    out = []
    for pp, sl in enumerate(ALL_PAIRS):
        qnv = qn_ref[:, sl]
        qrv = qr_ref[:, ALL_PAIRS[pp // 2]]
        rmasks = _rope_masks(pp)
        out.append([jnp.concatenate([jnp.where(masks[j], qnv, 0), jnp.where(rmasks[j], qrv, 0)], axis=1).astype(BF16)
                    for j in range(2)])
    return out


def mla_fwd_call(qn, qr, kn, krt, vm, B, S):
    T = B * S
    TQ, TK = ATT_TQ, ATT_TK
    nq = S // TQ

    def body(qn_ref, qr_ref, kn_ref, kr_ref, v_ref, o_ref, lse_ref):
        qi = pl.program_id(1)
        masks = _head_masks()
        qcat = _mla_queries(qn_ref, qr_ref, masks)
        row = _iota2((TQ, TK), 0)
        col = _iota2((TQ, TK), 1)
        vis = col <= row
        o_ref[...] = jnp.zeros_like(o_ref)

        def group(k0, pairs, carry, diag):
            heads = [(pp, j) for pp in pairs for j in range(2)]
            n = range(len(heads))
            krv = kr_ref[pl.ds(k0, TK), :]
            kcat = {pp: jnp.concatenate([kn_ref[pl.ds(k0, TK), ALL_PAIRS[pp]], krv], axis=1) for pp in pairs}
            vmk = {pp: _pair(v_ref[pl.ds(k0, TK), ALL_PAIRS[pp]], masks) for pp in pairs}
            s = [_nt(qcat[pp][j], kcat[pp]) * MLA_SCALE2 for pp, j in heads]
            if diag:
                s = [jnp.where(vis, x, NEG_BIG) for x in s]
            m_new = [jnp.maximum(carry[2 * h], jnp.max(s[h], axis=1, keepdims=True)) for h in n]
            alpha = [jnp.exp2(carry[2 * h] - m_new[h]) for h in n]
            pexp = [jnp.exp2(s[h] - m_new[h]) for h in n]
            out = []
            for h in n:
                out += [m_new[h], alpha[h] * carry[2 * h + 1] + jnp.sum(pexp[h], axis=1, keepdims=True)]
            pb = [x.astype(BF16) for x in pexp]
            for i, pp in enumerate(pairs):
                sl = ALL_PAIRS[pp]
                scale = jnp.where(masks[0], alpha[2 * i], alpha[2 * i + 1])
                o_ref[:, sl] = o_ref[:, sl] * scale + (_nn(pb[2 * i], vmk[pp][0]) + _nn(pb[2 * i + 1], vmk[pp][1]))
            return out

        def step(kb, carry, diag):
            k0 = pl.multiple_of(kb * TK, TK)
            out = []
            for g in range(0, len(ALL_PAIRS), MLA_GROUP):
                out += group(k0, list(range(g, g + MLA_GROUP)), carry[4 * g:4 * (g + MLA_GROUP)], diag)
            return tuple(out)

        neg = jnp.full((TQ, 1), NEG_BIG, F32)
        zero = jnp.zeros((TQ, 1), F32)
        carry = step(qi, (neg, zero) * (2 * len(ALL_PAIRS)), True)
        carry = lax.fori_loop(0, qi, lambda i, c: step(qi - 1 - i, c, False), carry)
        lane = _iota2((TQ, LANES), 1)
        for pp, sl in enumerate(ALL_PAIRS):
            m0, l0, m1, l1 = carry[4 * pp:4 * pp + 4]
            o_ref[:, sl] = o_ref[:, sl] * jnp.where(masks[0], 1.0 / l0, 1.0 / l1)
            lse_ref[:, sl] = jnp.where(lane == 0, m0 * LN2 + jnp.log(l0), jnp.where(lane == 1, m1 * LN2 + jnp.log(l1), 0.0))

    def rows(w):
        return pl.BlockSpec((TQ, w), lambda b, i: (b * nq + i, 0))

    def seq(w):
        return pl.BlockSpec((S, w), lambda b, i: (b, 0))

    return pl.pallas_call(
        body, name="mla_fwd", grid=(B, nq),
        in_specs=[rows(ATT_W), rows(ROPE_W), seq(ATT_W), seq(LANES), seq(ATT_W)],
        out_specs=[rows(ATT_W), rows(ATT_W)],
        out_shape=[jax.ShapeDtypeStruct((T, ATT_W), F32)] * 2,
        compiler_params=_params(("arbitrary", "arbitrary")),
    )(qn, qr, kn, krt, vm)


def mla_bwd_call(qn, qr, kn, krt, vm, o, lse, do, B, S, ex=None):
    T = B * S
    TQ, TK = ATT_TQ, ATT_TK
    nq = S // TQ

    def body(qn_ref, qr_ref, kn_ref, kr_ref, v_ref, o_ref, lse_ref, do_ref,
             dqn_ref, dqr_ref, dkn_ref, dv_ref, dkr_ref):
        qi = pl.program_id(1)

        @pl.when(qi == 0)
        def _():
            dkn_ref[...] = jnp.zeros_like(dkn_ref)
            dv_ref[...] = jnp.zeros_like(dv_ref)
            dkr_ref[...] = jnp.zeros_like(dkr_ref)

        masks = _head_masks()
        qcat = _mla_queries(qn_ref, qr_ref, masks)
        dom, dsum, lse = [], [], []
        for sl in ALL_PAIRS:
            do = do_ref[:, sl]
            dom.append(_pair(do, masks))
            d8 = _rows8(_lane_selector(HEAD_DIM), do * o_ref[:, sl])
            l8 = _rows8(_lane_selector(1), lse_ref[:, sl])
            dsum.append([_row_of(d8, j) for j in range(2)])
            lse.append([_row_of(l8, j) * LOG2E for j in range(2)])
        row = _iota2((TK, TQ), 0)
        col = _iota2((TK, TQ), 1)
        vis = row <= col
        dqn_ref[...] = jnp.zeros_like(dqn_ref)
        dqr_ref[...] = jnp.zeros_like(dqr_ref)

        def group(k0, pairs, diag):
            heads = [(pp, j) for pp in pairs for j in range(2)]
            n = range(len(heads))
            krv = kr_ref[pl.ds(k0, TK), :]
            knv = {pp: kn_ref[pl.ds(k0, TK), ALL_PAIRS[pp]] for pp in pairs}
            kcat = {pp: jnp.concatenate([knv[pp], krv], axis=1) for pp in pairs}
            v = {pp: v_ref[pl.ds(k0, TK), ALL_PAIRS[pp]] for pp in pairs}
            s = [_nt(kcat[pp], qcat[pp][j]) * MLA_SCALE2 for pp, j in heads]
            dp_ = [_nt(v[pp], dom[pp][j]) for pp, j in heads]
            pr = [jnp.exp2(s[h] - lse[pp][j]) for h, (pp, j) in enumerate(heads)]
            if diag:
                pr = [jnp.where(vis, x, 0.0) for x in pr]
            ds = [(pr[h] * (dp_[h] - dsum[pp][j]) * MLA_SCALE).astype(BF16) for h, (pp, j) in enumerate(heads)]
            pb = [x.astype(BF16) for x in pr]
            for i, pp in enumerate(pairs):
                sl = ALL_PAIRS[pp]
                rmasks = _rope_masks(pp)
                kcat_j = [jnp.concatenate([jnp.where(masks[j], knv[pp], 0), jnp.where(rmasks[j], krv, 0)],
                                          axis=1).astype(BF16) for j in range(2)]
                dv_ref[pl.ds(k0, TK), sl] += _nn(pb[2 * i], dom[pp][0]) + _nn(pb[2 * i + 1], dom[pp][1])
                dk = _nn(ds[2 * i], qcat[pp][0]) + _nn(ds[2 * i + 1], qcat[pp][1])
                dq = _tn(ds[2 * i], kcat_j[0]) + _tn(ds[2 * i + 1], kcat_j[1])
                dqn_ref[:, sl] += dq[:, :LANES]
                dqr_ref[:, sl] += dq[:, LANES:]
                dkn_ref[pl.ds(k0, TK), sl] += dk[:, :LANES]
                dkr_ref[pl.ds(k0, TK), sl] += dk[:, LANES:]

        def step(kb, diag):
            k0 = pl.multiple_of(kb * TK, TK)
            for g in range(0, len(ALL_PAIRS), MLA_GROUP):
                group(k0, list(range(g, g + MLA_GROUP)), diag)

        step(qi, True)

        def loop(i, c):
            step(qi - 1 - i, False)
            return c

        lax.fori_loop(0, qi, loop, 0)

    def rows(w):
        return pl.BlockSpec((TQ, w), lambda b, i: (b * nq + i, 0))

    def seq(w):
        return pl.BlockSpec((S, w), lambda b, i: (b, 0))

    outs, moved = _call(
        body, ex, name="mla_bwd", grid=(B, nq),
        in_specs=[rows(ATT_W), rows(ROPE_W), seq(ATT_W), seq(LANES), seq(ATT_W), rows(ATT_W), rows(ATT_W), rows(ATT_W)],
        out_specs=[rows(ATT_W), rows(ATT_W), seq(ATT_W), seq(ATT_W), seq(ATT_W)],
        out_shape=[jax.ShapeDtypeStruct((T, ATT_W), F32)] * 5, args=(qn, qr, kn, krt, vm, o, lse, do))
    return tuple(outs) if ex is None else tuple(outs) + (moved,)


def _taps(w_ref):
    return [w_ref[k:k + 1, :] for k in range(3)]


CHIP_MASKS = ((1, 0), (0, 1), (1, 1))


def _place():
    return lax.axis_index("x"), lax.axis_index("y"), lax.axis_index("c")


HALF_ALIGN = 32


def _any_specs(n):
    return [pl.BlockSpec(memory_space=pl.ANY)] * n


def _splits(shape):
    r, c = shape
    return "rows" if r % HALF_ALIGN == 0 else "cols" if c % (2 * LANES) == 0 else None


def _half(shape, half):
    r, c = shape
    how = _splits(shape)
    if how == "rows":
        return (pl.ds(pl.multiple_of(half * (r // 2), HALF_ALIGN // 2), r // 2), slice(None))
    if how == "cols":
        return (slice(None), pl.ds(pl.multiple_of(half * (c // 2), LANES), c // 2))
    return (slice(None), slice(None))


def _half_shape(shape):
    r, c = shape
    return {"rows": (r // 2, c), "cols": (r, c // 2)}[_splits(shape)]


def _remote(src, dst, send_sem, recv_sem, device):
    return pltpu.make_async_remote_copy(src_ref=src, dst_ref=dst, send_sem=send_sem, recv_sem=recv_sem,
                                        device_id=device, device_id_type=MESH)


class Exchange:
    def __init__(self, ins, out_shape, sems, start, finish):
        self.ins, self.out_shape, self.sems, self.start, self.finish = list(ins), list(out_shape), list(sems), start, finish


def gather_group(shards):
    n = len(shards)
    split = [_splits(s.shape) is not None for s in shards]

    def part(w, half):
        return _half(shards[w].shape, half)

    def copies(ins, outs, sems):
        ici_s, ici_r, _, _, local_sems = sems
        x, y, c = _place()
        chip = 2 * x + y
        local = [pltpu.make_async_copy(ins[w], outs[w].at[chip], local_sems.at[w]) for w in range(n)]
        sends = [_remote(ins[w].at[part(w, c)], outs[w].at[(chip,) + part(w, c)], ici_s.at[w, k], ici_r.at[w, k],
                         (x ^ fx, y ^ fy, c))
                 for w in range(n) for k, (fx, fy) in enumerate(CHIP_MASKS)]
        return local, sends

    def start(ins, outs, sems):
        local, sends = copies(ins, outs, sems)
        for cp in local + sends:
            cp.start()

    def finish(ins, outs, sems):
        ici_s, ici_r, d2d_s, d2d_r, _ = sems
        x, y, c = _place()
        sib = (x, y, 1 - c)
        local, sends = copies(ins, outs, sems)
        for w in range(n):
            for k, (fx, fy) in enumerate(CHIP_MASKS):
                landed = outs[w].at[(2 * (x ^ fx) + (y ^ fy),) + part(w, c)]
                _remote(landed, landed, ici_s.at[w, k], ici_r.at[w, k], sib).wait_recv()
                if split[w]:
                    cp = _remote(landed, landed, d2d_s.at[w, k], d2d_r.at[w, k], sib)
                    cp.start()
                    sends.append(cp)
        for w in range(n):
            for k, (fx, fy) in enumerate(CHIP_MASKS):
                if split[w]:
                    other = outs[w].at[(2 * (x ^ fx) + (y ^ fy),) + part(w, 1 - c)]
                    _remote(other, other, d2d_s.at[w, k], d2d_r.at[w, k], sib).wait_recv()
        for cp in sends:
            cp.wait_send()
        for cp in local:
            cp.wait()

    sems = pltpu.SemaphoreType.DMA((n, 3))
    return Exchange(shards, [jax.ShapeDtypeStruct((N_CHIPS,) + s.shape, s.dtype) for s in shards],
                    [sems, sems, sems, sems, pltpu.SemaphoreType.DMA((n,))], start, finish)


def swap_half(parts):
    n = len(parts)

    def copies(ins, outs, sems):
        x, y, c = _place()
        return [_remote(ins[w].at[(slice(None),) + _half(parts[w].shape[1:], 1 - c)], outs[w], sems[0].at[w], sems[1].at[w],
                        (x, y, 1 - c)) for w in range(n)]

    def start(ins, outs, sems):
        for cp in copies(ins, outs, sems):
            cp.start()

    def finish(ins, outs, sems):
        for cp in copies(ins, outs, sems):
            cp.wait_recv()
            cp.wait_send()

    return Exchange(parts, [jax.ShapeDtypeStruct((N_CHIPS,) + _half_shape(p.shape[1:]), F32) for p in parts],
                    [pltpu.SemaphoreType.DMA((n,))] * 2, start, finish)


def scatter_half(halves):
    n = len(halves)

    def copies(ins, outs, sems):
        x, y, c = _place()
        return [_remote(ins[w].at[2 * (x ^ fx) + (y ^ fy)], outs[w].at[k], sems[0].at[w, k], sems[1].at[w, k],
                        (x ^ fx, y ^ fy, c))
                for w in range(n) for k, (fx, fy) in enumerate(CHIP_MASKS)]

    def start(ins, outs, sems):
        for cp in copies(ins, outs, sems):
            cp.start()

    def finish(ins, outs, sems):
        for cp in copies(ins, outs, sems):
            cp.wait_recv()
            cp.wait_send()

    return Exchange(halves, [jax.ShapeDtypeStruct((3,) + h.shape[1:], h.dtype) for h in halves],
                    [pltpu.SemaphoreType.DMA((n, 3))] * 2, start, finish)


def swap_final(finals, shapes):
    n = len(finals)

    def copies(ins, outs, sems):
        x, y, c = _place()
        mine = [outs[w].at[_half(shapes[w], c)] for w in range(n)]
        local = [pltpu.make_async_copy(ins[w], mine[w], sems[2].at[w]) for w in range(n)]
        sends = [_remote(ins[w], mine[w], sems[0].at[w], sems[1].at[w], (x, y, 1 - c)) for w in range(n)]
        return local, sends

    def start(ins, outs, sems):
        local, sends = copies(ins, outs, sems)
        for cp in local + sends:
            cp.start()

    def finish(ins, outs, sems):
        x, y, c = _place()
        local, sends = copies(ins, outs, sems)
        for w in range(n):
            got = outs[w].at[_half(shapes[w], 1 - c)]
            _remote(got, got, sems[0].at[w], sems[1].at[w], (x, y, 1 - c)).wait_recv()
        for cp in sends:
            cp.wait_send()
        for cp in local:
            cp.wait()

    return Exchange(finals, [jax.ShapeDtypeStruct(tuple(s), F32) for s in shapes],
                    [pltpu.SemaphoreType.DMA((n,))] * 3, start, finish)


def exchange_call(name, ex):
    n, m = len(ex.ins), len(ex.out_shape)

    def body(*refs):
        ins, outs, sems = refs[:n], refs[n:n + m], refs[n + m:]
        ex.start(ins, outs, sems)
        ex.finish(ins, outs, sems)

    return pl.pallas_call(body, name=name, in_specs=_any_specs(n), out_specs=_any_specs(m), out_shape=ex.out_shape,
                          scratch_shapes=ex.sems, compiler_params=_params())(*ex.ins)


def _call(body, ex, *, name, grid, in_specs, out_specs, out_shape, args, scratch_shapes=()):
    sem = ("arbitrary",) * len(grid)
    if ex is None:
        outs = pl.pallas_call(body, name=name, grid=grid, in_specs=in_specs, out_specs=out_specs, out_shape=out_shape,
                              scratch_shapes=list(scratch_shapes), compiler_params=_params(sem))(*args)
        return outs, None
    ni, no, ns = len(in_specs), len(out_specs), len(scratch_shapes)
    ne, me = len(ex.ins), len(ex.out_shape)

    def wrapped(*refs):
        own_in, ex_in = refs[:ni], refs[ni:ni + ne]
        own_out, ex_out = refs[ni + ne:ni + ne + no], refs[ni + ne + no:ni + ne + no + me]
        own_scr, ex_sems = refs[ni + ne + no + me:ni + ne + no + me + ns], refs[ni + ne + no + me + ns:]
        ids = [pl.program_id(a) for a in range(len(grid))]
        first = functools.reduce(jnp.logical_and, [i == 0 for i in ids])
        last = functools.reduce(jnp.logical_and, [i == g - 1 for i, g in zip(ids, grid)])

        @pl.when(first)
        def _():
            ex.start(ex_in, ex_out, ex_sems)

        body(*own_in, *own_out, *own_scr)

        @pl.when(last)
        def _():
            ex.finish(ex_in, ex_out, ex_sems)

    outs = pl.pallas_call(
        wrapped, name=name, grid=grid, in_specs=list(in_specs) + _any_specs(ne),
        out_specs=list(out_specs) + _any_specs(me), out_shape=list(out_shape) + ex.out_shape,
        scratch_shapes=list(scratch_shapes) + ex.sems, compiler_params=_params(sem))(*args, *ex.ins)
    return outs[:no], outs[no:]


def _row_tile(rows, cap, mult=8):
    return max([t for t in range(mult, min(rows, cap) + 1, mult) if rows % t == 0] or [rows])


def add_half_call(name, part, got, where):
    _, rh, cols = got.shape
    tr = _row_tile(rh, 176, 16)
    nb = rh // tr
    by_rows = _splits(part.shape[1:]) == "rows"

    def body(where_ref, p_ref, g_ref, own_ref, send_ref):
        t = p_ref[...] + g_ref[...]
        send_ref[...] = t.astype(BF16)
        chip = where_ref[1]
        own_ref[...] = p_ref[chip] + g_ref[chip]

    blk = (N_CHIPS, tr, cols)
    return pl.pallas_call(
        body, name=name,
        grid_spec=pltpu.PrefetchScalarGridSpec(
            num_scalar_prefetch=1, grid=(nb,),
            in_specs=[pl.BlockSpec(blk, (lambda i, where_ref: (0, where_ref[0] * nb + i, 0)) if by_rows
                                   else (lambda i, where_ref: (0, i, where_ref[0]))),
                      pl.BlockSpec(blk, lambda i, where_ref: (0, i, 0))],
            out_specs=[pl.BlockSpec((tr, cols), lambda i, where_ref: (i, 0)),
                       pl.BlockSpec(blk, lambda i, where_ref: (0, i, 0))]),
        out_shape=[jax.ShapeDtypeStruct((rh, cols), F32), jax.ShapeDtypeStruct(got.shape, BF16)],
        compiler_params=_params(("parallel",)),
    )(where, part, got)


def sum_chips_call(name, own, got):
    _, rh, cols = got.shape
    tr = _row_tile(rh, 176, 16)

    def body(h_ref, g_ref, o_ref):
        o_ref[...] = ((h_ref[...] + g_ref[0].astype(F32)) + g_ref[1].astype(F32)) + g_ref[2].astype(F32)

    return pl.pallas_call(
        body, name=name, grid=(rh // tr,),
        in_specs=[pl.BlockSpec((tr, cols), lambda i: (i, 0)), pl.BlockSpec((3, tr, cols), lambda i: (0, i, 0))],
        out_specs=pl.BlockSpec((tr, cols), lambda i: (i, 0)),
        out_shape=jax.ShapeDtypeStruct((rh, cols), F32),
        compiler_params=_params(("parallel",)),
    )(own, got)


def _adamw(w, g, m, v):
    m = ADAM_B1 * m + (1.0 - ADAM_B1) * g
    v = ADAM_B2 * v + (1.0 - ADAM_B2) * (g * g)
    m_hat = m / (1.0 - ADAM_B1 ** ADAM_STEP)
    v_hat = v / (1.0 - ADAM_B2 ** ADAM_STEP)
    delta = -ADAM_LR * (m_hat / (jnp.sqrt(v_hat) + ADAM_EPS) + ADAM_WD * w)
    return delta, m, v


def adamw_call(name, g, w, m, v):
    r, cols = w.shape
    tr = r if r % 8 else _row_tile(r, 256)

    def body(g_ref, w_ref, m_ref, v_ref, go_ref, d_ref, nm_ref, nv_ref):
        g = g_ref[...]
        go_ref[...] = g
        d_ref[...], nm_ref[...], nv_ref[...] = _adamw(w_ref[...], g, m_ref[...], v_ref[...])

    spec = pl.BlockSpec((tr, cols), lambda i: (i, 0))
    return pl.pallas_call(
        body, name=name, grid=(r // tr,), in_specs=[spec] * 4, out_specs=[spec] * 4,
        out_shape=[jax.ShapeDtypeStruct((r, cols), F32)] * 4,
        compiler_params=_params(("parallel",)),
    )(g, w, m, v)


def allsum_small_call(v):
    R = v.shape[0]

    def body(v_ref, out_ref, buf, send_sems, recv_sems):
        x, y, c = _place()
        me = 4 * x + 2 * y + c
        buf[me] = v_ref[...]
        sends = []
        for k in range(1, N_DEV):
            fx, fy, fc = (k >> 2) & 1, (k >> 1) & 1, k & 1
            cp = pltpu.make_async_remote_copy(
                src_ref=v_ref, dst_ref=buf.at[me], send_sem=send_sems.at[k - 1], recv_sem=recv_sems.at[k - 1],
                device_id=(x ^ fx, y ^ fy, c ^ fc), device_id_type=MESH)
            cp.start()
            sends.append(cp)
        for k in range(1, N_DEV):
            pltpu.make_async_remote_copy(
                src_ref=v_ref, dst_ref=buf.at[me ^ k], send_sem=send_sems.at[k - 1], recv_sem=recv_sems.at[k - 1],
                device_id=(x, y, c), device_id_type=MESH).wait_recv()
        acc = buf[0]
        for d in range(1, N_DEV):
            acc = acc + buf[d]
        out_ref[...] = acc
        for cp in sends:
            cp.wait_send()

    vm = pl.BlockSpec(memory_space=pltpu.VMEM)
    return pl.pallas_call(
        body, name="allsum_small", in_specs=[vm], out_specs=vm,
        out_shape=jax.ShapeDtypeStruct((R, LANES), F32),
        scratch_shapes=[pltpu.VMEM((N_DEV, R, LANES), F32), pltpu.SemaphoreType.DMA((N_DEV - 1,)),
                        pltpu.SemaphoreType.DMA((N_DEV - 1,))],
        compiler_params=_params(),
    )(v)


def _slab(flat, mult):
    n = flat.shape[-1]
    rows = -(-n // (LANES * mult)) * mult
    flat = jnp.pad(flat, [(0, 0)] * (flat.ndim - 1) + [(0, rows * LANES - n)])
    return flat.reshape(flat.shape[:-1] + (rows, LANES))


def full_from_chips(blocks, by_col):
    _, r, c = blocks.shape
    return blocks.transpose(1, 0, 2).reshape(r, N_CHIPS * c) if by_col else blocks.reshape(N_CHIPS * r, c)


def chips_from_full(full, by_col):
    if by_col:
        r, c = full.shape[0], full.shape[1] // N_CHIPS
        return full.reshape(r, N_CHIPS, c).transpose(1, 0, 2)
    return full.reshape(N_CHIPS, full.shape[0] // N_CHIPS, full.shape[1])


SMALL_PACK = SMALL_W + ("loss", "conv_w")
SMALL_PACK_N = {**SMALL_N, "loss": 1, "conv_w": 3 * 2 * D_FF}


def pack_small(vals):
    zero = jnp.zeros((1,), F32)
    return _slab(jnp.concatenate([vals[n].reshape(-1) if n in vals else jnp.tile(zero, SMALL_PACK_N[n])
                                  for n in SMALL_PACK]), 8)


def unpack_small(slab, shapes):
    flat = slab.reshape(-1)
    out, off = {}, 0
    for n in SMALL_PACK:
        out[n] = flat[off:off + SMALL_PACK_N[n]].reshape(shapes[n])
        off += SMALL_PACK_N[n]
    return out


def _split_heads(w, a, b):
    r = w.shape[0]
    w3 = w.reshape(r, HEADS, a + b)
    return w3[:, :, :a].reshape(r, HEADS * a), w3[:, :, a:].reshape(r, HEADS * b)


def _merge_heads(wa, wb, a, b):
    r = wa.shape[0]
    return jnp.concatenate([wa.reshape(r, HEADS, a), wb.reshape(r, HEADS, b)], axis=2).reshape(r, HEADS * (a + b))


def kernel(x, positions, g_mix, w_in, g_cq, w_uq, g_ckv, w_ukv, g_sb_out, g_mla_out, w_out, g_ffn, w_up, conv_w, conv_b, w_down, g_final, loss_target, m_g_mix, m_w_in, m_g_cq, m_w_uq, m_g_ckv, m_w_ukv, m_g_sb_out, m_g_mla_out, m_w_out, m_g_ffn, m_w_up, m_conv_w, m_conv_b, m_w_down, m_g_final, v_g_mix, v_w_in, v_g_cq, v_w_uq, v_g_ckv, v_w_ukv, v_g_sb_out, v_g_mla_out, v_w_out, v_g_ffn, v_w_up, v_conv_w, v_conv_b, v_w_down, v_g_final):
    given = dict(locals())
    B, S, _ = x.shape
    T = B * S
    w_big = {n: given[n][0].T if n == "w_in" else given[n][0] for n in BIG_W}
    m_big = {n: given["m_" + n][0].T if n == "w_in" else given["m_" + n][0] for n in BIG_W}
    v_big = {n: given["v_" + n][0].T if n == "w_in" else given["v_" + n][0] for n in BIG_W}
    shard_shape = {n: w_big[n].shape for n in BIG_W}

    mla_w = ("w_uq", "w_ukv")
    later = ("w_out", "w_up", "w_down", "conv_w")
    x2d = x.reshape(T, D_MODEL)
    half = ROPE_DIM // 2
    inv_freq = 1.0 / (ROPE_BASE ** (jnp.arange(half, dtype=F32) * (2.0 / ROPE_DIM)))
    h, cos, sin, got_w = norm_mix_rope_call(
        x2d, g_mix, positions.reshape(T, 1), jnp.tile(inv_freq, LANES // half).reshape(1, LANES),
        gather_group([w_big["w_in"].astype(BF16)]))
    gather_later = gather_group([w_big[n] if n == "conv_w" else w_big[n].astype(BF16) for n in later])
    w_in_t = jnp.pad(got_w[0].reshape(IN_COLS, D_MODEL), ((0, IN_COLS_PAD - IN_COLS), (0, 0)))

    p, got_w = matmul_call("proj_in", h, w_in_t, "nt", tn=IN_COLS_PAD // 2,
                           ex=gather_group([w_big[n].astype(BF16) for n in mla_w]))
    full = {n: full_from_chips(g_, BIG_SHARD[n][2]) for n, g_ in zip(mla_w, got_w)}
    w_uq_p = jnp.concatenate(_split_heads(full["w_uq"], HEAD_DIM, ROPE_DIM), axis=1)
    w_ukv_p = jnp.concatenate(_split_heads(full["w_ukv"], HEAD_DIM, HEAD_DIM), axis=1)
    qn, qr, kn, vm, krt, cqn, ckvn = mla_prep_fwd_call(p, cos, sin, g_cq, g_ckv, w_uq_p, w_ukv_p)
    o_sb, lt_sb, got_w = sb_fwd_call(p, B, S, ex=gather_later)
    w_up4 = got_w[1]
    full.update({n: full_from_chips(g_, BIG_SHARD[n][2]) for n, g_ in zip(later, got_w) if n != "w_up"})
    conv_w_full = full["conv_w"]
    o_mla, lse = mla_fwd_call(qn, qr, kn, krt, vm, B, S)
    o_cat, x1, hn = proj_out_norm_call(o_sb, o_mla, g_sb_out, g_mla_out, full["w_out"], x2d, g_ffn)
    u_g, u_v, act, c_g, c_v = ffn_up_conv_call(hn, w_up4, conv_w_full, conv_b, S)
    dx2, dx2b, loss_row, dg_final = ffn_down_loss_call(
        act, full["w_down"], x1, g_final.reshape(1, D_MODEL), loss_target.reshape(T, D_MODEL))

    xi, yi, ci = _place()
    chip = (2 * xi + yi).astype(jnp.int32).reshape(1)
    where = jnp.stack([ci, 2 * xi + yi]).astype(jnp.int32)

    def add_halves(names, parts, sib_rows):
        return [add_half_call("add_half_" + n, p_, s_, where) for n, p_, s_ in zip(names, parts, sib_rows)]

    def sum_chips(names, halves, from_chips):
        return [sum_chips_call("sum_chips_" + n, h_[0], f_) for n, h_, f_ in zip(names, halves, from_chips)]

    ffn_w = ("w_down", "w_up")
    parts_ffn = [chips_from_full(wgrad_call("wgrad_down", act, dx2b, tn=512, tt=1024), False)]
    du_g, du_v, dw_up4, dcw_g, dcw_v, dcb_g, dcb_v = conv_bwd_wgrad_up_call(
        hn, u_g, u_v, c_g, c_v, dx2b, full["w_down"], conv_w_full, S)
    parts_ffn.append(dw_up4)
    dx1, dg_ffn, sib_ffn = ffn_up_bwd_call(du_g, du_v, w_up4, x1, g_ffn, dx2, swap_half(parts_ffn))
    parts_out = [chips_from_full(wgrad_call("wgrad_out", o_cat, dx1, tt=1024), False)]
    do_sb, do_mla, dg_sb_out, dg_mla_out, sib_out = proj_out_bwd_call(
        dx1, full["w_out"], o_sb, o_mla, g_sb_out, g_mla_out, swap_half(parts_out))
    early = ffn_w + ("w_out",)
    halves = add_halves(early, parts_ffn + parts_out, list(sib_ffn) + list(sib_out))
    dq_sb, dk_sb, dv_sb, from_chips = sb_bwd_call(p, lt_sb, do_sb, B, S, ex=scatter_half([h_[1] for h_ in halves]))
    finals = sum_chips(early, halves, from_chips)
    dqn, dqr4, dkn, dvm, dkrt4, done = mla_bwd_call(qn, qr, kn, krt, vm, o_mla, lse, do_mla, B, S,
        ex=swap_final(finals, [shard_shape[n] for n in early]))
    grads = dict(zip(early, done))
    dcq, dckvr, dq_cat, dkv_cat, dg_cq, dg_ckv = mla_prep_bwd_call(
        p, cos, sin, g_cq, g_ckv, w_uq_p, w_ukv_p, dqn, dqr4, dkn, dvm, dkrt4)
    dw_uq_p = wgrad_call("wgrad_uq", cqn, dq_cat)
    dw_ukv_p = wgrad_call("wgrad_ukv", ckvn, dkv_cat)
    dp = (dq_sb, dk_sb, dv_sb, dcq, dckvr)
    late = ("w_uq", "w_ukv", "w_in")
    parts_late = [chips_from_full(g_, True) for g_ in (
        _merge_heads(dw_uq_p[:, :ATT_W], dw_uq_p[:, ATT_W:], HEAD_DIM, ROPE_DIM),
        _merge_heads(dw_ukv_p[:, :ATT_W], dw_ukv_p[:, ATT_W:], HEAD_DIM, HEAD_DIM))]
    parts_late.append(chips_from_full(wgrad_in_call(h, dp), False))
    dh, sib_late = proj_in_bwd_call(dp, w_in_t, swap_half(parts_late))
    halves = add_halves(late, parts_late, sib_late)
    grad_x, dg_mix, from_chips = rmsnorm_bwd_call(
        "norm_mix_bwd", x2d, g_mix, dh, dx1, ex=scatter_half([h_[1] for h_ in halves]))
    finals = sum_chips(late, halves, from_chips)
    grads.update(zip(late, exchange_call("swap_final_late", swap_final(finals, [shard_shape[n] for n in late]))))

    shapes = {n: given[n].shape for n in SMALL_W}
    shapes.update(loss=(), conv_w=(3, 2 * D_FF))
    small_g = {"g_mix": dg_mix, "g_cq": dg_cq, "g_ckv": dg_ckv, "g_sb_out": dg_sb_out, "g_mla_out": dg_mla_out,
               "g_ffn": dg_ffn, "conv_b": jnp.concatenate([dcb_g, dcb_v], axis=1), "g_final": dg_final,
               "loss": loss_row[0, :1], "conv_w": jnp.concatenate([dcw_g, dcw_v], axis=1)}
    gs_slab = allsum_small_call(pack_small(small_g))
    small_in = [pack_small({n: given[pre + n] for n in SMALL_W}) for pre in ("", "m_", "v_")]
    small_out = [unpack_small(s, shapes) for s in adamw_call("adamw_small", gs_slab, *small_in)]
    cw_cols = BIG_SHARD["conv_w"][1]
    grads["conv_w"] = lax.dynamic_slice_in_dim(small_out[0]["conv_w"], chip[0] * cw_cols, cw_cols, axis=1)

    big_out = {n: adamw_call("adamw_" + n, grads[n], w_big[n], m_big[n], v_big[n]) for n in BIG_W}
    weights = ("g_mix", "w_in", "g_cq", "w_uq", "g_ckv", "w_ukv", "g_sb_out", "g_mla_out", "w_out", "g_ffn",
               "w_up", "conv_w", "conv_b", "w_down", "g_final")
    outs = [small_out[0]["loss"], grad_x.reshape(B, S, D_MODEL)]
    for k in range(4):
        for n in weights:
            if n in BIG_W:
                outs.append((big_out[n][k].T if n == "w_in" else big_out[n][k])[None])
            else:
                outs.append(small_out[k][n])
    return tuple(outs)
```

```python
import functools

import jax
import jax.numpy as jnp
from jax import lax
from jax.experimental import pallas as pl
from jax.experimental.pallas import tpu as pltpu

F32 = jnp.float32
BF16 = jnp.bfloat16
MESH = pl.DeviceIdType.MESH

D_MODEL = 1024
HEADS = 8
HEAD_DIM = 64
ATT_W = HEADS * HEAD_DIM
ROPE_DIM = 32
ROPE_W = HEADS * ROPE_DIM
QK_DIM = HEAD_DIM + ROPE_DIM
Q_RANK = 384
KV_RANK = 256
D_FF = 2816
IN_COLS = 2208
IN_COLS_PAD = 2304
EPS = 1e-6
ROPE_BASE = 10000.0
SB_SCALE = HEAD_DIM ** -0.5
SB_SCALE2 = SB_SCALE * 1.4426950408889634
MLA_SCALE = QK_DIM ** -0.5
LOG2E = 1.4426950408889634
LN2 = 0.6931471805599453
MLA_SCALE2 = MLA_SCALE * LOG2E
LANES = 128
N_CHIPS = 4
N_DEV = 8
VMEM_LIMIT = 48 * 1024 * 1024
ATT_TQ = 256
ATT_TK = 256
ATT_PAIRS = 4
PAIR_LANES = [slice(i * LANES, (i + 1) * LANES) for i in range(ATT_PAIRS)]
SB_BWD_GROUP = 2
SB_FWD_GROUP = 4
MLA_GROUP = 4
NEG_BIG = -1e30

ADAM_LR = 0.001
ADAM_B1 = 0.9
ADAM_B2 = 0.999
ADAM_EPS = 1e-08
ADAM_WD = 0.01
ADAM_STEP = 10

BIG_W = ("w_in", "w_uq", "w_ukv", "w_out", "w_up", "conv_w", "w_down")
BIG_SHARD = {
    "w_in": (D_MODEL, IN_COLS // 4, True),
    "w_uq": (Q_RANK, HEADS * QK_DIM // 4, True),
    "w_ukv": (KV_RANK, 2 * ATT_W // 4, True),
    "w_out": (2 * ATT_W // 4, D_MODEL, False),
    "w_up": (D_MODEL, 2 * D_FF // 4, True),
    "conv_w": (3, 2 * D_FF // 4, True),
    "w_down": (D_FF // 4, D_MODEL, False),
}
SMALL_W = ("g_mix", "g_cq", "g_ckv", "g_sb_out", "g_mla_out", "g_ffn", "conv_b", "g_final")
SMALL_N = {"g_mix": D_MODEL, "g_cq": Q_RANK, "g_ckv": KV_RANK, "g_sb_out": ATT_W, "g_mla_out": ATT_W,
           "g_ffn": D_MODEL, "conv_b": 2 * D_FF, "g_final": D_MODEL}


def _params(sem=None, **kw):
    return pltpu.CompilerParams(dimension_semantics=sem, vmem_limit_bytes=VMEM_LIMIT, **kw)


def _dot(a, b, dims):
    return lax.dot_general(a, b, (dims, ((), ())), preferred_element_type=F32)


def _nn(a, b):
    return _dot(a, b, ((1,), (0,)))


def _nt(a, b):
    return _dot(a, b, ((1,), (1,)))


def _tn(a, b):
    return _dot(a, b, ((0,), (0,)))


def _split2(x):
    hi = x.astype(BF16)
    lo = (x - hi.astype(F32)).astype(BF16)
    return hi, lo


def _rms_r(x, d):
    return lax.rsqrt(jnp.sum(x * x, axis=-1, keepdims=True) * (1.0 / d) + EPS)


def _rms_bwd(x, g, dy, d):
    r = _rms_r(x, d)
    xhat = x * r
    gy = dy * g
    dx = r * (gy - xhat * (jnp.sum(xhat * gy, axis=-1, keepdims=True) * (1.0 / d)))
    return dx, jnp.sum(dy * xhat, axis=0, keepdims=True)


def _rot(x):
    lane = lax.broadcasted_iota(jnp.int32, x.shape, x.ndim - 1)
    n = x.shape[-1]
    return jnp.where((lane & 31) < 16, pltpu.roll(x, n - 16, x.ndim - 1), pltpu.roll(x, 16, x.ndim - 1))


def _fold4(x):
    return x + pltpu.roll(x, 32, 1) + pltpu.roll(x, 64, 1) + pltpu.roll(x, 96, 1)


def matmul_call(name, a, b, mode, out_dtype=F32, res=None, tm=512, tn=None, ex=None):
    M, K = a.shape
    N = b.shape[1] if mode == "nn" else b.shape[0]
    tn = N if tn is None else tn
    assert M % tm == 0 and N % tn == 0

    def body(*refs):
        if res is None:
            a_ref, b_ref, o_ref = refs
        else:
            a_ref, b_ref, r_ref, o_ref = refs
        av = a_ref[...].astype(BF16)
        bv = b_ref[...].astype(BF16)
        acc = _nn(av, bv) if mode == "nn" else _nt(av, bv)
        if res is not None:
            acc = r_ref[...] + acc
        o_ref[...] = acc.astype(out_dtype)

    in_specs = [pl.BlockSpec((tm, K), lambda j, i: (i, 0))]
    if mode == "nn":
        in_specs.append(pl.BlockSpec((K, tn), lambda j, i: (0, j)))
    else:
        in_specs.append(pl.BlockSpec((tn, K), lambda j, i: (j, 0)))
    args = [a, b]
    if res is not None:
        in_specs.append(pl.BlockSpec((tm, tn), lambda j, i: (i, j)))
        args.append(res)
    outs, moved = _call(body, ex, name=name, grid=(N // tn, M // tm), in_specs=in_specs,
                        out_specs=[pl.BlockSpec((tm, tn), lambda j, i: (i, j))],
                        out_shape=[jax.ShapeDtypeStruct((M, N), out_dtype)], args=args)
    return outs[0] if ex is None else (outs[0], moved)


def _rows(tm, width):
    return pl.BlockSpec((tm, width), lambda i: (i, 0))


def _whole(a, **kw):
    return pl.BlockSpec(a.shape, lambda i: (0,) * a.ndim, **kw)


def proj_out_norm_call(o_sb, o_mla, g_sb, g_mla, w_out, x, g_ffn, tm=512):
    T = o_sb.shape[0]
    N = w_out.shape[1]

    def body(a_ref, b_ref, ga_ref, gb_ref, w_ref, x_ref, g_ref, oc_ref, x1_ref, hn_ref):
        a = a_ref[...]
        b = b_ref[...]
        ya = (a * _rms_r(a, ATT_W)) * ga_ref[...]
        yb = (b * _rms_r(b, ATT_W)) * gb_ref[...]
        o_cat = jnp.concatenate([ya, yb], axis=1).astype(BF16)
        oc_ref[...] = o_cat
        x1 = x_ref[...] + _nn(o_cat, w_ref[...])
        x1_ref[...] = x1
        hn_ref[...] = ((x1 * _rms_r(x1, N)) * g_ref[...]).astype(BF16)

    return pl.pallas_call(
        body, name="proj_out", grid=(T // tm,),
        in_specs=[_rows(tm, ATT_W), _rows(tm, ATT_W), _whole(g_sb), _whole(g_mla), _whole(w_out), _rows(tm, N),
                  _whole(g_ffn)],
        out_specs=[_rows(tm, 2 * ATT_W), _rows(tm, N), _rows(tm, N)],
        out_shape=[jax.ShapeDtypeStruct((T, 2 * ATT_W), BF16), jax.ShapeDtypeStruct((T, N), F32),
                   jax.ShapeDtypeStruct((T, N), BF16)],
        compiler_params=_params(("parallel",)),
    )(o_sb, o_mla, g_sb, g_mla, w_out, x, g_ffn)


def norm_mix_rope_call(x, g, pos, inv_freq, ex, tm=512):
    T, d = x.shape

    def body(x_ref, g_ref, p_ref, f_ref, o_ref, c_ref, s_ref):
        xv = x_ref[...]
        o_ref[...] = ((xv * _rms_r(xv, d)) * g_ref[...]).astype(BF16)
        ang = p_ref[...].astype(F32) * f_ref[...]
        lane = lax.broadcasted_iota(jnp.int32, ang.shape, 1)
        sn = jnp.sin(ang)
        c_ref[...] = jnp.cos(ang)
        s_ref[...] = jnp.where((lane & 31) < 16, -sn, sn)

    outs, moved = _call(
        body, ex, name="norm_mix", grid=(T // tm,),
        in_specs=[_rows(tm, d), _whole(g), _rows(tm, 1), _whole(inv_freq)],
        out_specs=[_rows(tm, d), _rows(tm, LANES), _rows(tm, LANES)],
        out_shape=[jax.ShapeDtypeStruct((T, d), BF16), jax.ShapeDtypeStruct((T, LANES), F32),
                   jax.ShapeDtypeStruct((T, LANES), F32)], args=(x, g, pos, inv_freq))
    return tuple(outs) + (moved,)


def ffn_down_loss_call(act, w_down, x1, g, target, tm=512):
    T, K = act.shape
    d = w_down.shape[1]

    def body(a_ref, w_ref, x1_ref, g_ref, t_ref, dx_ref, dxb_ref, loss_ref, dg_ref):
        @pl.when(pl.program_id(0) == 0)
        def _():
            loss_ref[...] = jnp.zeros_like(loss_ref)
            dg_ref[...] = jnp.zeros_like(dg_ref)

        x = x1_ref[...] + _nn(a_ref[...], w_ref[...])
        g = g_ref[...]
        y = (x * _rms_r(x, d)) * g
        err = y - t_ref[...]
        loss_ref[...] += jnp.sum(jnp.sum(err * err, axis=1, keepdims=True), axis=0, keepdims=True) * (0.5 / d)
        dx, dg = _rms_bwd(x, g, err * (1.0 / d), d)
        dx_ref[...] = dx
        dxb_ref[...] = dx.astype(BF16)
        dg_ref[...] += dg

    return pl.pallas_call(
        body, name="ffn_down_loss", grid=(T // tm,),
        in_specs=[_rows(tm, K), _whole(w_down), _rows(tm, d), _whole(g), _rows(tm, d)],
        out_specs=[_rows(tm, d), _rows(tm, d), pl.BlockSpec((1, LANES), lambda i: (0, 0)), _whole(g)],
        out_shape=[jax.ShapeDtypeStruct((T, d), F32), jax.ShapeDtypeStruct((T, d), BF16),
                   jax.ShapeDtypeStruct((1, LANES), F32), jax.ShapeDtypeStruct((1, d), F32)],
        compiler_params=_params(("arbitrary",)),
    )(act, w_down, x1, g, target)


def proj_out_bwd_call(dx1, w_out, o_sb, o_mla, g_sb, g_mla, ex, tm=512):
    T, N = dx1.shape

    def body(d_ref, w_ref, a_ref, b_ref, ga_ref, gb_ref, da_ref, db_ref, dga_ref, dgb_ref):
        @pl.when(pl.program_id(0) == 0)
        def _():
            dga_ref[...] = jnp.zeros_like(dga_ref)
            dgb_ref[...] = jnp.zeros_like(dgb_ref)

        d = _nt(d_ref[...].astype(BF16), w_ref[...])
        da, dga = _rms_bwd(a_ref[...], ga_ref[...], d[:, :ATT_W], ATT_W)
        db, dgb = _rms_bwd(b_ref[...], gb_ref[...], d[:, ATT_W:], ATT_W)
        da_ref[...] = da
        db_ref[...] = db
        dga_ref[...] += dga
        dgb_ref[...] += dgb

    outs, moved = _call(
        body, ex, name="proj_out_bwd", grid=(T // tm,),
        in_specs=[_rows(tm, N), _whole(w_out), _rows(tm, ATT_W), _rows(tm, ATT_W), _whole(g_sb), _whole(g_mla)],
        out_specs=[_rows(tm, ATT_W), _rows(tm, ATT_W), _whole(g_sb), _whole(g_mla)],
        out_shape=[jax.ShapeDtypeStruct((T, ATT_W), F32), jax.ShapeDtypeStruct((T, ATT_W), F32),
                   jax.ShapeDtypeStruct((1, ATT_W), F32), jax.ShapeDtypeStruct((1, ATT_W), F32)],
        args=(dx1, w_out, o_sb, o_mla, g_sb, g_mla))
    return tuple(outs) + (moved,)


def wgrad_call(name, a, b, tn=None, tt=512, by_chip=False):
    T, M = a.shape
    N = b.shape[1]
    tn = N if tn is None else tn
    tt = min(tt, T)
    assert T % tt == 0 and N % tn == 0
    if by_chip:
        out_spec = pl.BlockSpec((None, M, tn), lambda j, t: (j, 0, 0))
        out_shape = jax.ShapeDtypeStruct((N // tn, M, tn), F32)
    else:
        out_spec = pl.BlockSpec((M, tn), lambda j, t: (0, j))
        out_shape = jax.ShapeDtypeStruct((M, N), F32)

    def body(a_ref, b_ref, o_ref):
        @pl.when(pl.program_id(1) == 0)
        def _():
            o_ref[...] = jnp.zeros_like(o_ref)

        o_ref[...] += _tn(a_ref[...].astype(BF16), b_ref[...].astype(BF16))

    return pl.pallas_call(
        body, name=name, grid=(N // tn, T // tt),
        in_specs=[pl.BlockSpec((tt, M), lambda j, t: (t, 0)), pl.BlockSpec((tt, tn), lambda j, t: (t, j))],
        out_specs=out_spec, out_shape=out_shape,
        compiler_params=_params(("parallel", "arbitrary")),
    )(a, b)


UP_COLS = 2 * D_FF // N_CHIPS


def _shift_down_after(u, prev8, n):
    top = pltpu.roll(jnp.concatenate([prev8, u[0:8]], axis=0), n, 0)[8:16]
    return jnp.concatenate([top, pltpu.roll(u, n, 0)[8:]], axis=0)


def ffn_up_conv_call(hn, w4, conv_w, conv_b, S, tm=512):
    T, K = hn.shape
    per_seq = S // tm

    def body(a_ref, wg_ref, wv_ref, cg_ref, cv_ref, bg_ref, bv_ref, ug_ref, uv_ref, act_ref, og_ref, ov_ref, pg_ref, pv_ref):
        @pl.when(pl.program_id(1) % per_seq == 0)
        def _():
            pg_ref[...] = jnp.zeros_like(pg_ref)
            pv_ref[...] = jnp.zeros_like(pv_ref)

        a = a_ref[...]
        outs = []
        for w_ref, c_ref, b_ref, u_ref, prev_ref in ((wg_ref, cg_ref, bg_ref, ug_ref, pg_ref),
                                                     (wv_ref, cv_ref, bv_ref, uv_ref, pv_ref)):
            u = _nn(a, w_ref[...])
            u_ref[...] = u
            prev8 = prev_ref[...]
            taps = _taps(c_ref)
            outs.append(taps[0] * _shift_down_after(u, prev8, 2) + taps[1] * _shift_down_after(u, prev8, 1)
                        + taps[2] * u + b_ref[...])
            prev_ref[...] = u[tm - 8:, :]
        gate, val = outs
        sig = 1.0 / (1.0 + jnp.exp(-gate))
        silu = gate * sig
        act_ref[...] = (silu * val).astype(BF16)
        og_ref[...] = silu.astype(BF16)
        ov_ref[...] = (val * (sig * (1.0 + gate * (1.0 - sig)))).astype(BF16)

    out = pl.BlockSpec((tm, UP_COLS), lambda j, i: (i, j))

    def cols(rows, off):
        return pl.BlockSpec((rows, UP_COLS), lambda j, i: (0, off + j))

    return pl.pallas_call(
        body, name="ffn_up_conv", grid=(2, T // tm),
        in_specs=[pl.BlockSpec((tm, K), lambda j, i: (i, 0)),
                  pl.BlockSpec((None, K, UP_COLS), lambda j, i: (j, 0, 0)),
                  pl.BlockSpec((None, K, UP_COLS), lambda j, i: (2 + j, 0, 0)),
                  cols(3, 0), cols(3, 2), cols(1, 0), cols(1, 2)],
        out_specs=[out] * 5,
        out_shape=[jax.ShapeDtypeStruct((T, D_FF), F32)] * 2 + [jax.ShapeDtypeStruct((T, D_FF), BF16)] * 3,
        scratch_shapes=[pltpu.VMEM((8, UP_COLS), F32)] * 2,
        compiler_params=_params(("arbitrary", "arbitrary")),
    )(hn, w4, w4, conv_w, conv_w, conv_b, conv_b)


def _shift_up_before(d, next8, n):
    rows = d.shape[0]
    bottom = pltpu.roll(jnp.concatenate([d[rows - 8:], next8], axis=0), 16 - n, 0)[0:8]
    return jnp.concatenate([pltpu.roll(d, rows - n, 0)[:rows - 8], bottom], axis=0)


def conv_bwd_wgrad_up_call(hn, ug, uv, cg, cv, dx2, w_down, conv_w, S, tt=512):
    T, M = hn.shape
    nt = T // tt
    per_seq = S // tt

    def body(a_ref, ug_ref, uv_ref, cg_ref, cv_ref, dx_ref, wd_ref, wg_ref, wv_ref,
             dug_ref, duv_ref, dw_ref, dwg_ref, dwv_ref, dbg_ref, dbv_ref, ng_ref, nv_ref):
        step = pl.program_id(1)

        @pl.when(step == 0)
        def _():
            for r in (dw_ref, dwg_ref, dwv_ref, dbg_ref, dbv_ref):
                r[...] = jnp.zeros_like(r)

        @pl.when((nt - 1 - step) % per_seq == per_seq - 1)
        def _():
            ng_ref[...] = jnp.zeros_like(ng_ref)
            nv_ref[...] = jnp.zeros_like(nv_ref)

        da = _nt(dx_ref[...], wd_ref[...])
        dval = da * cg_ref[...].astype(F32)
        dgate = da * cv_ref[...].astype(F32)
        a = a_ref[...]
        for k, (u_ref, d, w, du_ref, dcw_ref, dcb_ref, next_ref) in enumerate((
                (ug_ref, dgate, _taps(wg_ref), dug_ref, dwg_ref, dbg_ref, ng_ref),
                (uv_ref, dval, _taps(wv_ref), duv_ref, dwv_ref, dbv_ref, nv_ref))):
            u_ = u_ref[...]
            next8 = next_ref[...]
            d1 = _shift_up_before(d, next8, 1)
            d2 = _shift_up_before(d, next8, 2)
            du = (w[2] * d + w[1] * d1 + w[0] * d2).astype(BF16)
            du_ref[...] = du
            dw_ref[k] += _tn(a, du)
            dcb_ref[...] += jnp.sum(d, axis=0, keepdims=True)
            dcw_ref[0:1, :] += jnp.sum(d2 * u_, axis=0, keepdims=True)
            dcw_ref[1:2, :] += jnp.sum(d1 * u_, axis=0, keepdims=True)
            dcw_ref[2:3, :] += jnp.sum(d * u_, axis=0, keepdims=True)
            next_ref[...] = d[0:8, :]

    def tok(width, by_col):
        return pl.BlockSpec((tt, width), (lambda j, i: (nt - 1 - i, j)) if by_col else (lambda j, i: (nt - 1 - i, 0)))

    def cols(rows, off):
        return pl.BlockSpec((rows, UP_COLS), lambda j, i: (0, off + j))

    outs = pl.pallas_call(
        body, name="conv_bwd_wgrad_up", grid=(2, nt),
        in_specs=[tok(M, False), tok(UP_COLS, True), tok(UP_COLS, True), tok(UP_COLS, True), tok(UP_COLS, True),
                  tok(dx2.shape[1], False), pl.BlockSpec((UP_COLS, w_down.shape[1]), lambda j, i: (j, 0)),
                  cols(3, 0), cols(3, 2)],
        out_specs=[tok(UP_COLS, True), tok(UP_COLS, True),
                   pl.BlockSpec((2, None, M, UP_COLS), lambda j, i: (0, j, 0, 0), pipeline_mode=pl.Buffered(1)),
                   cols(3, 0), cols(3, 0), cols(1, 0), cols(1, 0)],
        out_shape=[jax.ShapeDtypeStruct((T, D_FF), BF16), jax.ShapeDtypeStruct((T, D_FF), BF16),
                   jax.ShapeDtypeStruct((2, 2, M, UP_COLS), F32),
                   jax.ShapeDtypeStruct((3, D_FF), F32), jax.ShapeDtypeStruct((3, D_FF), F32),
                   jax.ShapeDtypeStruct((1, D_FF), F32), jax.ShapeDtypeStruct((1, D_FF), F32)],
        scratch_shapes=[pltpu.VMEM((8, UP_COLS), F32)] * 2,
        compiler_params=_params(("arbitrary", "arbitrary")),
    )(hn, ug, uv, cg, cv, dx2, w_down, conv_w, conv_w)
    return (outs[0], outs[1], outs[2].reshape(N_CHIPS, M, UP_COLS)) + tuple(outs[3:])


def ffn_up_bwd_call(du_g, du_v, w4, x1, g_ffn, dx2, ex, tm=512):
    T = du_g.shape[0]
    N = w4.shape[1]

    def body(g_ref, v_ref, w_ref, x1_ref, gf_ref, dx2_ref, dx1_ref, dg_ref):
        @pl.when(pl.program_id(0) == 0)
        def _():
            dg_ref[...] = jnp.zeros_like(dg_ref)

        dhn = _nt(g_ref[:, :UP_COLS], w_ref[0]) + _nt(g_ref[:, UP_COLS:], w_ref[1])
        dhn = dhn + _nt(v_ref[:, :UP_COLS], w_ref[2]) + _nt(v_ref[:, UP_COLS:], w_ref[3])
        dx, dg = _rms_bwd(x1_ref[...], gf_ref[...], dhn, N)
        dx1_ref[...] = dx2_ref[...] + dx
        dg_ref[...] += dg

    outs, moved = _call(
        body, ex, name="ffn_up_bwd", grid=(T // tm,),
        in_specs=[_rows(tm, D_FF), _rows(tm, D_FF), _whole(w4, pipeline_mode=pl.Buffered(1)), _rows(tm, N),
                  _whole(g_ffn), _rows(tm, N)],
        out_specs=[_rows(tm, N), _whole(g_ffn)],
        out_shape=[jax.ShapeDtypeStruct((T, N), F32), jax.ShapeDtypeStruct((1, N), F32)],
        args=(du_g, du_v, w4, x1, g_ffn, dx2))
    return outs[0], outs[1], moved


IN_PIECES = ((0, ATT_W), (ATT_W, ATT_W), (2 * ATT_W, ATT_W), (3 * ATT_W, Q_RANK), (3 * ATT_W + Q_RANK, Q_RANK))


def proj_in_bwd_call(pieces, w_in_t, ex, tm=512):
    T = pieces[0].shape[0]
    N = w_in_t.shape[1]
    n = len(pieces)

    def body(*refs):
        o_ref = refs[2 * n]
        acc = _nn(refs[0][...].astype(BF16), refs[n][...])
        for i in range(1, n):
            acc = acc + _nn(refs[i][...].astype(BF16), refs[n + i][...])
        o_ref[...] = acc

    outs, moved = _call(
        body, ex, name="proj_in_bwd", grid=(T // tm,),
        in_specs=[pl.BlockSpec((tm, w), lambda i: (i, 0)) for _, w in IN_PIECES]
        + [pl.BlockSpec((w, N), functools.partial(lambda c, i: (c, 0), off // w)) for off, w in IN_PIECES],
        out_specs=[pl.BlockSpec((tm, N), lambda i: (i, 0))],
        out_shape=[jax.ShapeDtypeStruct((T, N), F32)], args=tuple(pieces) + (w_in_t,) * n)
    return outs[0], moved


def wgrad_in_call(h, pieces, tt=512):
    T, M = h.shape
    n = len(pieces)

    def body(*refs):
        a_ref, o_ref = refs[0], refs[n + 1]

        @pl.when(pl.program_id(0) == 0)
        def _():
            o_ref[...] = jnp.zeros_like(o_ref)

        a = a_ref[...]
        for i, (off, w) in enumerate(IN_PIECES):
            rows = min(w, IN_COLS - off)
            o_ref[off:off + rows, :] += _tn(refs[1 + i][...].astype(BF16), a)[:rows]

    return pl.pallas_call(
        body, name="wgrad_in", grid=(T // tt,),
        in_specs=[pl.BlockSpec((tt, M), lambda t: (t, 0))] + [pl.BlockSpec((tt, w), lambda t: (t, 0)) for _, w in IN_PIECES],
        out_specs=pl.BlockSpec((IN_COLS, M), lambda t: (0, 0)),
        out_shape=jax.ShapeDtypeStruct((IN_COLS, M), F32),
        compiler_params=_params(("arbitrary",)),
    )(h, *pieces)


def rmsnorm_bwd_call(name, x, g, dy, res, tm=512, ex=None):
    T, d = x.shape

    def body(x_ref, g_ref, dy_ref, r_ref, dx_ref, dg_ref):
        @pl.when(pl.program_id(0) == 0)
        def _():
            dg_ref[...] = jnp.zeros_like(dg_ref)

        dx, dg = _rms_bwd(x_ref[...], g_ref[...], dy_ref[...], d)
        dx_ref[...] = r_ref[...] + dx
        dg_ref[...] += dg

    row = pl.BlockSpec((tm, d), lambda i: (i, 0))
    vec = pl.BlockSpec((1, d), lambda i: (0, 0))
    outs, moved = _call(body, ex, name=name, grid=(T // tm,), in_specs=[row, vec, row, row], out_specs=[row, vec],
                        out_shape=[jax.ShapeDtypeStruct((T, d), F32), jax.ShapeDtypeStruct((1, d), F32)],
                        args=(x, g, dy, res))
    return tuple(outs) if ex is None else tuple(outs) + (moved,)


def mla_prep_fwd_call(p, cos, sin, g_cq, g_ckv, w_uq_p, w_ukv_p, tm=512):
    T = p.shape[0]

    def body(cq_ref, ckvr_ref, c_ref, s_ref, gq_ref, gkv_ref, wq_ref, wkv_ref,
             qn_ref, qr_ref, kn_ref, vm_ref, krt_ref, cqn_ref, ckvn_ref):
        c = c_ref[...]
        s = s_ref[...]
        cq = cq_ref[...]
        cqn = ((cq * _rms_r(cq, Q_RANK)) * gq_ref[...]).astype(BF16)
        cqn_ref[...] = cqn
        q = _nn(cqn, wq_ref[...])
        qn_ref[...] = q[:, :ATT_W].astype(BF16)
        for g in range(ROPE_W // LANES):
            qr = q[:, ATT_W + g * LANES:ATT_W + (g + 1) * LANES]
            qr_ref[:, g * LANES:(g + 1) * LANES] = (qr * c + _rot(qr) * s).astype(BF16)
        ckvr = ckvr_ref[...]
        ckv = ckvr[:, :KV_RANK]
        ckvn = ((ckv * _rms_r(ckv, KV_RANK)) * gkv_ref[...]).astype(BF16)
        ckvn_ref[...] = ckvn
        kv = _nn(ckvn, wkv_ref[...])
        kn_ref[...] = kv[:, :ATT_W].astype(BF16)
        vm_ref[...] = kv[:, ATT_W:].astype(BF16)
        kr = _fold4(ckvr[:, KV_RANK:])
        krt_ref[...] = (kr * c + _rot(kr) * s).astype(BF16)

    def row(w, j=0):
        return pl.BlockSpec((tm, w), lambda i: (i, j))

    def full(a):
        return pl.BlockSpec(a.shape, lambda i: (0, 0))

    return pl.pallas_call(
        body, name="mla_prep_fwd", grid=(T // tm,),
        in_specs=[row(Q_RANK, 4), row(Q_RANK, 5), row(LANES), row(LANES), full(g_cq), full(g_ckv),
                  full(w_uq_p), full(w_ukv_p)],
        out_specs=[row(ATT_W), row(ROPE_W), row(ATT_W), row(ATT_W), row(LANES), row(Q_RANK), row(KV_RANK)],
        out_shape=[jax.ShapeDtypeStruct((T, w), BF16) for w in (ATT_W, ROPE_W, ATT_W, ATT_W, LANES, Q_RANK, KV_RANK)],
        compiler_params=_params(("parallel",)),
    )(p, p, cos, sin, g_cq, g_ckv, w_uq_p, w_ukv_p)


def mla_prep_bwd_call(p, cos, sin, g_cq, g_ckv, w_uq_p, w_ukv_p, dqn, dqr4, dkn, dvm, dkrt4, tm=512):
    T = p.shape[0]

    def body(cq_ref, ckvr_ref, c_ref, s_ref, gq_ref, gkv_ref, wq_ref, wkv_ref,
             dqn_ref, dqr4_ref, dkn_ref, dvm_ref, dkrt4_ref,
             dcq_ref, dckvr_ref, dq_ref, dkv_ref, dgq_ref, dgkv_ref):
        @pl.when(pl.program_id(0) == 0)
        def _():
            dgq_ref[...] = jnp.zeros_like(dgq_ref)
            dgkv_ref[...] = jnp.zeros_like(dgkv_ref)

        c = c_ref[...]
        s = s_ref[...]
        d4 = dqr4_ref[...]
        dqr = [d4[:, :128] + d4[:, 128:256], d4[:, 256:384] + d4[:, 384:]]
        dqr = [t * c + _rot(t * s) for t in dqr]
        dq = jnp.concatenate([dqn_ref[...]] + dqr, axis=1).astype(BF16)
        dq_ref[...] = dq
        dcq, dgq = _rms_bwd(cq_ref[...], gq_ref[...], _nt(dq, wq_ref[...]), Q_RANK)
        dcq_ref[...] = dcq
        dgq_ref[...] += dgq
        dkv = jnp.concatenate([dkn_ref[...], dvm_ref[...]], axis=1).astype(BF16)
        dkv_ref[...] = dkv
        ckvr = ckvr_ref[...]
        dckv, dgkv = _rms_bwd(ckvr[:, :KV_RANK], gkv_ref[...], _nt(dkv, wkv_ref[...]), KV_RANK)
        dgkv_ref[...] += dgkv
        k4 = dkrt4_ref[...]
        dkr = _fold4(k4[:, :128] + k4[:, 128:256] + k4[:, 256:384] + k4[:, 384:])
        dkr = dkr * c + _rot(dkr * s)
        lane = lax.broadcasted_iota(jnp.int32, dkr.shape, 1)
        dckvr_ref[...] = jnp.concatenate([dckv, jnp.where(lane < ROPE_DIM, dkr, 0.0)], axis=1)

    def row(w, j=0):
        return pl.BlockSpec((tm, w), lambda i: (i, j))

    def full(a):
        return pl.BlockSpec(a.shape, lambda i: (0, 0))

    return pl.pallas_call(
        body, name="mla_prep_bwd", grid=(T // tm,),
        in_specs=[row(Q_RANK, 4), row(Q_RANK, 5), row(LANES), row(LANES), full(g_cq), full(g_ckv),
                  full(w_uq_p), full(w_ukv_p), row(ATT_W), row(ATT_W), row(ATT_W), row(ATT_W), row(ATT_W)],
        out_specs=[row(Q_RANK), row(Q_RANK), row(ATT_W + ROPE_W), row(2 * ATT_W),
                   pl.BlockSpec((1, Q_RANK), lambda i: (0, 0)), pl.BlockSpec((1, KV_RANK), lambda i: (0, 0))],
        out_shape=[jax.ShapeDtypeStruct((T, Q_RANK), F32), jax.ShapeDtypeStruct((T, Q_RANK), F32),
                   jax.ShapeDtypeStruct((T, ATT_W + ROPE_W), BF16), jax.ShapeDtypeStruct((T, 2 * ATT_W), BF16),
                   jax.ShapeDtypeStruct((1, Q_RANK), F32), jax.ShapeDtypeStruct((1, KV_RANK), F32)],
        compiler_params=_params(("arbitrary",)),
    )(p, p, cos, sin, g_cq, g_ckv, w_uq_p, w_ukv_p, dqn, dqr4, dkn, dvm, dkrt4)


def _iota2(shape, axis):
    return lax.broadcasted_iota(jnp.int32, shape, axis)


def _head_masks():
    lane = _iota2((1, LANES), 1)
    return lane < HEAD_DIM, lane >= HEAD_DIM


def _pair(x, masks, dtype=BF16):
    return [jnp.where(m, x, 0.0).astype(dtype) for m in masks]


def _log_gates(z):
    keep = jnp.maximum(z, 0.0) + jnp.log2(1.0 + jnp.exp2(-jnp.abs(z)))
    return z - keep, keep


def _last_row(x):
    return _row_of(x[x.shape[0] - 8:, :], 7)


def _lane_selector(group):
    return jnp.where(_iota2((16, LANES), 1) // group == _iota2((16, LANES), 0), 1.0, 0.0).astype(BF16)


def _rows8(sel_t, x):
    hi = x.astype(BF16)
    r1 = x - hi.astype(F32)
    mid = r1.astype(BF16)
    lo = (r1 - mid.astype(F32)).astype(BF16)
    return _nt(sel_t, hi) + _nt(sel_t, mid) + _nt(sel_t, lo)


def _row_of(x8, j):
    return jnp.sum(jnp.where(_iota2(x8.shape, 0) == j, x8, 0.0), axis=0, keepdims=True)


def sb_fwd_call(p, B, S, ex=None):
    T = B * S
    TQ, TK = ATT_TQ, ATT_TK
    nq = S // TQ

    def body(q_ref, k_ref, v_ref, o_ref, lt_ref):
        qi = pl.program_id(2)
        masks = _head_masks()
        qm = [_pair(q_ref[:, sl] * SB_SCALE2, masks) for sl in PAIR_LANES]
        row = _iota2((TQ, TK), 0)
        col = _iota2((TQ, TK), 1)
        tri = jnp.where(row > col, 1.0, 0.0).astype(BF16)
        tri2 = jnp.concatenate([tri, tri], axis=0)
        vis = col < row
        o_ref[...] = jnp.zeros_like(o_ref)

        def group(k0, pairs, carry, diag):
            heads = [(pp, j) for pp in pairs for j in range(2)]
            n = range(len(heads))
            k = {pp: k_ref[pl.ds(k0, TK), PAIR_LANES[pp]].astype(BF16) for pp in pairs}
            vm = {pp: _pair(v_ref[pl.ds(k0, TK), PAIR_LANES[pp]], masks) for pp in pairs}
            z = [_nt(qm[pp][j], k[pp]) for pp, j in heads]
            if diag:
                z = [jnp.where(vis, x, NEG_BIG) for x in z]
            gates = [_log_gates(x) for x in z]
            lb = [g[0] for g in gates]
            keep = [g[1] for g in gates]
            tail = [_nn(jnp.concatenate(_split2(keep[h]), axis=1), tri2) + carry[h] for h in n]
            a = [jnp.exp2(lb[h] - tail[h]) for h in n]
            ab = [x.astype(BF16) for x in a]
            for i, pp in enumerate(pairs):
                o_ref[:, PAIR_LANES[pp]] += _nn(ab[2 * i], vm[pp][0]) + _nn(ab[2 * i + 1], vm[pp][1])
            return [tail[h][:, 0:1] + keep[h][:, 0:1] for h in n]

        def step(kb, carry, diag):
            k0 = pl.multiple_of(kb * TK, TK)
            out = []
            for g in range(0, ATT_PAIRS, SB_FWD_GROUP):
                out += group(k0, list(range(g, g + SB_FWD_GROUP)), carry[2 * g:2 * (g + SB_FWD_GROUP)], diag)
            return tuple(out)

        zero = jnp.zeros((TQ, 1), F32)
        carry = step(qi, (zero,) * (2 * ATT_PAIRS), True)
        carry = lax.fori_loop(0, qi, lambda i, c: step(qi - 1 - i, c, False), carry)
        lane = _iota2((TQ, LANES), 1)
        for pp, sl in enumerate(PAIR_LANES):
            lt_ref[:, sl] = jnp.where(lane == 0, carry[2 * pp], jnp.where(lane == 1, carry[2 * pp + 1], 0.0))

    W = ATT_PAIRS * LANES
    qspec = pl.BlockSpec((TQ, W), lambda b, h, i: (b * nq + i, h))
    outs, moved = _call(
        body, ex, name="sb_fwd", grid=(B, HEADS // 2 // ATT_PAIRS, nq),
        in_specs=[qspec,
                  pl.BlockSpec((S, W), lambda b, h, i: (b, ATT_W // W + h)),
                  pl.BlockSpec((S, W), lambda b, h, i: (b, 2 * ATT_W // W + h))],
        out_specs=[qspec, qspec],
        out_shape=[jax.ShapeDtypeStruct((T, ATT_W), F32)] * 2, args=(p, p, p))
    return tuple(outs) if ex is None else tuple(outs) + (moved,)


def sb_bwd_call(p, lt, do, B, S, ex=None):
    T = B * S
    TQ, TK = ATT_TQ, ATT_TK
    nq = S // TQ

    def body(q_ref, k_ref, v_ref, lt_ref, do_ref, dq_ref, dk_ref, dv_ref):
        qi = pl.program_id(2)

        @pl.when(qi == 0)
        def _():
            dk_ref[...] = jnp.zeros_like(dk_ref)
            dv_ref[...] = jnp.zeros_like(dv_ref)

        masks = _head_masks()
        qm = [_pair(q_ref[:, sl] * SB_SCALE2, masks) for sl in PAIR_LANES]
        dom = [_pair(do_ref[:, sl], masks) for sl in PAIR_LANES]
        start = []
        for sl in PAIR_LANES:
            l8 = _rows8(_lane_selector(1), lt_ref[:, sl])
            start += [-_row_of(l8, 0), jnp.zeros((1, TQ), F32), -_row_of(l8, 1), jnp.zeros((1, TQ), F32)]
        row = _iota2((TK, TQ), 0)
        col = _iota2((TK, TQ), 1)
        incl = jnp.where(col <= row, 1.0, 0.0).astype(BF16)
        incl2 = jnp.concatenate([incl, incl], axis=1)
        excl = jnp.where(col < row, 1.0, 0.0).astype(BF16)
        vis = row < col
        dq_ref[...] = jnp.zeros_like(dq_ref)

        def group(k0, pairs, carry, diag):
            heads = [(pp, j) for pp in pairs for j in range(2)]
            n = range(len(heads))
            kf = {pp: k_ref[pl.ds(k0, TK), PAIR_LANES[pp]] for pp in pairs}
            km = {pp: _pair(kf[pp], masks) for pp in pairs}
            v = {pp: v_ref[pl.ds(k0, TK), PAIR_LANES[pp]].astype(BF16) for pp in pairs}
            z = [_nt(kf[pp].astype(BF16), qm[pp][j]) for pp, j in heads]
            da = [_nt(v[pp], dom[pp][j]) for pp, j in heads]
            if diag:
                z = [jnp.where(vis, x, NEG_BIG) for x in z]
            gates = [_log_gates(x) for x in z]
            lb = [g[0] for g in gates]
            keep = [g[1] for g in gates]
            left = [_nn(incl2, jnp.concatenate(_split2(keep[h]), axis=0)) + carry[2 * h] for h in n]
            a = [jnp.exp2(lb[h] + left[h]) for h in n]
            e = [a[h] * da[h] for h in n]
            before = [_nn(excl, e[h].astype(BF16)) + carry[2 * h + 1] for h in n]
            dz = [e[h] - jnp.exp2(lb[h]) * (e[h] + before[h]) for h in n]
            dzb = [x.astype(BF16) for x in dz]
            ab = [x.astype(BF16) for x in a]
            out = []
            for h in n:
                out += [_last_row(left[h]), _last_row(before[h]) + _last_row(e[h])]
            for i, pp in enumerate(pairs):
                sl = PAIR_LANES[pp]
                dk_ref[pl.ds(k0, TK), sl] += _nn(dzb[2 * i], qm[pp][0]) + _nn(dzb[2 * i + 1], qm[pp][1])
                dv_ref[pl.ds(k0, TK), sl] += _nn(ab[2 * i], dom[pp][0]) + _nn(ab[2 * i + 1], dom[pp][1])
                dq_ref[:, sl] += _tn(dzb[2 * i], km[pp][0]) + _tn(dzb[2 * i + 1], km[pp][1])
            return out

        def step(kb, carry, diag):
            k0 = pl.multiple_of(kb * TK, TK)
            out = []
            for g in range(0, ATT_PAIRS, SB_BWD_GROUP):
                out += group(k0, list(range(g, g + SB_BWD_GROUP)), carry[4 * g:4 * (g + SB_BWD_GROUP)], diag)
            return tuple(out)

        carry = lax.fori_loop(0, qi, lambda i, c: step(i, c, False), tuple(start))
        step(qi, carry, True)
        dq_ref[...] *= SB_SCALE

        @pl.when(qi == nq - 1)
        def _():
            dk_ref[...] *= LN2

    W = ATT_PAIRS * LANES
    qspec = pl.BlockSpec((TQ, W), lambda b, h, i: (b * nq + i, h))
    sspec = pl.BlockSpec((S, W), lambda b, h, i: (b, h))
    outs, moved = _call(
        body, ex, name="sb_bwd", grid=(B, HEADS // 2 // ATT_PAIRS, nq),
        in_specs=[qspec,
                  pl.BlockSpec((S, W), lambda b, h, i: (b, ATT_W // W + h)),
                  pl.BlockSpec((S, W), lambda b, h, i: (b, 2 * ATT_W // W + h)),
                  qspec, qspec],
        out_specs=[qspec, sspec, sspec],
        out_shape=[jax.ShapeDtypeStruct((T, ATT_W), F32)] * 3, args=(p, p, p, lt, do))
    return tuple(outs) if ex is None else tuple(outs) + (moved,)


ALL_PAIRS = [slice(i * LANES, (i + 1) * LANES) for i in range(HEADS // 2)]


def _rope_masks(hp):
    grp = _iota2((1, LANES), 1) // ROPE_DIM
    return [grp == ((2 * hp + j) % 4) for j in range(2)]


def _mla_queries(qn_ref, qr_ref, masks):
    out = []
    for pp, sl in enumerate(ALL_PAIRS):
        qnv = qn_ref[:, sl]
        qrv = qr_ref[:, ALL_PAIRS[pp // 2]]
        rmasks = _rope_masks(pp)
        out.append([jnp.concatenate([jnp.where(masks[j], qnv, 0), jnp.where(rmasks[j], qrv, 0)], axis=1).astype(BF16)
                    for j in range(2)])
    return out


def mla_fwd_call(qn, qr, kn, krt, vm, B, S):
    T = B * S
    TQ, TK = ATT_TQ, ATT_TK
    nq = S // TQ

    def body(qn_ref, qr_ref, kn_ref, kr_ref, v_ref, o_ref, lse_ref):
        qi = pl.program_id(1)
        masks = _head_masks()
        qcat = _mla_queries(qn_ref, qr_ref, masks)
        row = _iota2((TQ, TK), 0)
        col = _iota2((TQ, TK), 1)
        vis = col <= row
        o_ref[...] = jnp.zeros_like(o_ref)

        def group(k0, pairs, carry, diag):
            heads = [(pp, j) for pp in pairs for j in range(2)]
            n = range(len(heads))
            krv = kr_ref[pl.ds(k0, TK), :]
            kcat = {pp: jnp.concatenate([kn_ref[pl.ds(k0, TK), ALL_PAIRS[pp]], krv], axis=1) for pp in pairs}
            vmk = {pp: _pair(v_ref[pl.ds(k0, TK), ALL_PAIRS[pp]], masks) for pp in pairs}
            s = [_nt(qcat[pp][j], kcat[pp]) * MLA_SCALE2 for pp, j in heads]
            if diag:
                s = [jnp.where(vis, x, NEG_BIG) for x in s]
            m_new = [jnp.maximum(carry[2 * h], jnp.max(s[h], axis=1, keepdims=True)) for h in n]
            alpha = [jnp.exp2(carry[2 * h] - m_new[h]) for h in n]
            pexp = [jnp.exp2(s[h] - m_new[h]) for h in n]
            out = []
            for h in n:
                out += [m_new[h], alpha[h] * carry[2 * h + 1] + jnp.sum(pexp[h], axis=1, keepdims=True)]
            pb = [x.astype(BF16) for x in pexp]
            for i, pp in enumerate(pairs):
                sl = ALL_PAIRS[pp]
                scale = jnp.where(masks[0], alpha[2 * i], alpha[2 * i + 1])
                o_ref[:, sl] = o_ref[:, sl] * scale + (_nn(pb[2 * i], vmk[pp][0]) + _nn(pb[2 * i + 1], vmk[pp][1]))
            return out

        def step(kb, carry, diag):
            k0 = pl.multiple_of(kb * TK, TK)
            out = []
            for g in range(0, len(ALL_PAIRS), MLA_GROUP):
                out += group(k0, list(range(g, g + MLA_GROUP)), carry[4 * g:4 * (g + MLA_GROUP)], diag)
            return tuple(out)

        neg = jnp.full((TQ, 1), NEG_BIG, F32)
        zero = jnp.zeros((TQ, 1), F32)
        carry = step(qi, (neg, zero) * (2 * len(ALL_PAIRS)), True)
        carry = lax.fori_loop(0, qi, lambda i, c: step(qi - 1 - i, c, False), carry)
        lane = _iota2((TQ, LANES), 1)
        for pp, sl in enumerate(ALL_PAIRS):
            m0, l0, m1, l1 = carry[4 * pp:4 * pp + 4]
            o_ref[:, sl] = o_ref[:, sl] * jnp.where(masks[0], 1.0 / l0, 1.0 / l1)
            lse_ref[:, sl] = jnp.where(lane == 0, m0 * LN2 + jnp.log(l0), jnp.where(lane == 1, m1 * LN2 + jnp.log(l1), 0.0))

    def rows(w):
        return pl.BlockSpec((TQ, w), lambda b, i: (b * nq + i, 0))

    def seq(w):
        return pl.BlockSpec((S, w), lambda b, i: (b, 0))

    return pl.pallas_call(
        body, name="mla_fwd", grid=(B, nq),
        in_specs=[rows(ATT_W), rows(ROPE_W), seq(ATT_W), seq(LANES), seq(ATT_W)],
        out_specs=[rows(ATT_W), rows(ATT_W)],
        out_shape=[jax.ShapeDtypeStruct((T, ATT_W), F32)] * 2,
        compiler_params=_params(("arbitrary", "arbitrary")),
    )(qn, qr, kn, krt, vm)


def mla_bwd_call(qn, qr, kn, krt, vm, o, lse, do, B, S, ex=None):
    T = B * S
    TQ, TK = ATT_TQ, ATT_TK
    nq = S // TQ

    def body(qn_ref, qr_ref, kn_ref, kr_ref, v_ref, o_ref, lse_ref, do_ref,
             dqn_ref, dqr_ref, dkn_ref, dv_ref, dkr_ref):
        qi = pl.program_id(1)

        @pl.when(qi == 0)
        def _():
            dkn_ref[...] = jnp.zeros_like(dkn_ref)
            dv_ref[...] = jnp.zeros_like(dv_ref)
            dkr_ref[...] = jnp.zeros_like(dkr_ref)

        masks = _head_masks()
        qcat = _mla_queries(qn_ref, qr_ref, masks)
        dom, dsum, lse = [], [], []
        for sl in ALL_PAIRS:
            do = do_ref[:, sl]
            dom.append(_pair(do, masks))
            d8 = _rows8(_lane_selector(HEAD_DIM), do * o_ref[:, sl])
            l8 = _rows8(_lane_selector(1), lse_ref[:, sl])
            dsum.append([_row_of(d8, j) for j in range(2)])
            lse.append([_row_of(l8, j) * LOG2E for j in range(2)])
        row = _iota2((TK, TQ), 0)
        col = _iota2((TK, TQ), 1)
        vis = row <= col
        dqn_ref[...] = jnp.zeros_like(dqn_ref)
        dqr_ref[...] = jnp.zeros_like(dqr_ref)

        def group(k0, pairs, diag):
            heads = [(pp, j) for pp in pairs for j in range(2)]
            n = range(len(heads))
            krv = kr_ref[pl.ds(k0, TK), :]
            knv = {pp: kn_ref[pl.ds(k0, TK), ALL_PAIRS[pp]] for pp in pairs}
            kcat = {pp: jnp.concatenate([knv[pp], krv], axis=1) for pp in pairs}
            v = {pp: v_ref[pl.ds(k0, TK), ALL_PAIRS[pp]] for pp in pairs}
            s = [_nt(kcat[pp], qcat[pp][j]) * MLA_SCALE2 for pp, j in heads]
            dp_ = [_nt(v[pp], dom[pp][j]) for pp, j in heads]
            pr = [jnp.exp2(s[h] - lse[pp][j]) for h, (pp, j) in enumerate(heads)]
            if diag:
                pr = [jnp.where(vis, x, 0.0) for x in pr]
            ds = [(pr[h] * (dp_[h] - dsum[pp][j]) * MLA_SCALE).astype(BF16) for h, (pp, j) in enumerate(heads)]
            pb = [x.astype(BF16) for x in pr]
            for i, pp in enumerate(pairs):
                sl = ALL_PAIRS[pp]
                rmasks = _rope_masks(pp)
                kcat_j = [jnp.concatenate([jnp.where(masks[j], knv[pp], 0), jnp.where(rmasks[j], krv, 0)],
                                          axis=1).astype(BF16) for j in range(2)]
                dv_ref[pl.ds(k0, TK), sl] += _nn(pb[2 * i], dom[pp][0]) + _nn(pb[2 * i + 1], dom[pp][1])
                dk = _nn(ds[2 * i], qcat[pp][0]) + _nn(ds[2 * i + 1], qcat[pp][1])
                dq = _tn(ds[2 * i], kcat_j[0]) + _tn(ds[2 * i + 1], kcat_j[1])
                dqn_ref[:, sl] += dq[:, :LANES]
                dqr_ref[:, sl] += dq[:, LANES:]
                dkn_ref[pl.ds(k0, TK), sl] += dk[:, :LANES]
                dkr_ref[pl.ds(k0, TK), sl] += dk[:, LANES:]

        def step(kb, diag):
            k0 = pl.multiple_of(kb * TK, TK)
            for g in range(0, len(ALL_PAIRS), MLA_GROUP):
                group(k0, list(range(g, g + MLA_GROUP)), diag)

        step(qi, True)

        def loop(i, c):
            step(qi - 1 - i, False)
            return c

        lax.fori_loop(0, qi, loop, 0)

    def rows(w):
        return pl.BlockSpec((TQ, w), lambda b, i: (b * nq + i, 0))

    def seq(w):
        return pl.BlockSpec((S, w), lambda b, i: (b, 0))

    outs, moved = _call(
        body, ex, name="mla_bwd", grid=(B, nq),
        in_specs=[rows(ATT_W), rows(ROPE_W), seq(ATT_W), seq(LANES), seq(ATT_W), rows(ATT_W), rows(ATT_W), rows(ATT_W)],
        out_specs=[rows(ATT_W), rows(ATT_W), seq(ATT_W), seq(ATT_W), seq(ATT_W)],
        out_shape=[jax.ShapeDtypeStruct((T, ATT_W), F32)] * 5, args=(qn, qr, kn, krt, vm, o, lse, do))
    return tuple(outs) if ex is None else tuple(outs) + (moved,)


def _taps(w_ref):
    return [w_ref[k:k + 1, :] for k in range(3)]


CHIP_MASKS = ((1, 0), (0, 1), (1, 1))


def _place():
    return lax.axis_index("x"), lax.axis_index("y"), lax.axis_index("c")


HALF_ALIGN = 32


def _any_specs(n):
    return [pl.BlockSpec(memory_space=pl.ANY)] * n


def _splits(shape):
    r, c = shape
    return "rows" if r % HALF_ALIGN == 0 else "cols" if c % (2 * LANES) == 0 else None


def _half(shape, half):
    r, c = shape
    how = _splits(shape)
    if how == "rows":
        return (pl.ds(pl.multiple_of(half * (r // 2), HALF_ALIGN // 2), r // 2), slice(None))
    if how == "cols":
        return (slice(None), pl.ds(pl.multiple_of(half * (c // 2), LANES), c // 2))
    return (slice(None), slice(None))


def _half_shape(shape):
    r, c = shape
    return {"rows": (r // 2, c), "cols": (r, c // 2)}[_splits(shape)]


def _remote(src, dst, send_sem, recv_sem, device):
    return pltpu.make_async_remote_copy(src_ref=src, dst_ref=dst, send_sem=send_sem, recv_sem=recv_sem,
                                        device_id=device, device_id_type=MESH)


class Exchange:
    def __init__(self, ins, out_shape, sems, start, finish):
        self.ins, self.out_shape, self.sems, self.start, self.finish = list(ins), list(out_shape), list(sems), start, finish


def gather_group(shards):
    n = len(shards)
    split = [_splits(s.shape) is not None for s in shards]

    def part(w, half):
        return _half(shards[w].shape, half)

    def copies(ins, outs, sems):
        ici_s, ici_r, _, _, local_sems = sems
        x, y, c = _place()
        chip = 2 * x + y
        local = [pltpu.make_async_copy(ins[w], outs[w].at[chip], local_sems.at[w]) for w in range(n)]
        sends = [_remote(ins[w].at[part(w, c)], outs[w].at[(chip,) + part(w, c)], ici_s.at[w, k], ici_r.at[w, k],
                         (x ^ fx, y ^ fy, c))
                 for w in range(n) for k, (fx, fy) in enumerate(CHIP_MASKS)]
        return local, sends

    def start(ins, outs, sems):
        local, sends = copies(ins, outs, sems)
        for cp in local + sends:
            cp.start()

    def finish(ins, outs, sems):
        ici_s, ici_r, d2d_s, d2d_r, _ = sems
        x, y, c = _place()
        sib = (x, y, 1 - c)
        local, sends = copies(ins, outs, sems)
        for w in range(n):
            for k, (fx, fy) in enumerate(CHIP_MASKS):
                landed = outs[w].at[(2 * (x ^ fx) + (y ^ fy),) + part(w, c)]
                _remote(landed, landed, ici_s.at[w, k], ici_r.at[w, k], sib).wait_recv()
                if split[w]:
                    cp = _remote(landed, landed, d2d_s.at[w, k], d2d_r.at[w, k], sib)
                    cp.start()
                    sends.append(cp)
        for w in range(n):
            for k, (fx, fy) in enumerate(CHIP_MASKS):
                if split[w]:
                    other = outs[w].at[(2 * (x ^ fx) + (y ^ fy),) + part(w, 1 - c)]
                    _remote(other, other, d2d_s.at[w, k], d2d_r.at[w, k], sib).wait_recv()
        for cp in sends:
            cp.wait_send()
        for cp in local:
            cp.wait()

    sems = pltpu.SemaphoreType.DMA((n, 3))
    return Exchange(shards, [jax.ShapeDtypeStruct((N_CHIPS,) + s.shape, s.dtype) for s in shards],
                    [sems, sems, sems, sems, pltpu.SemaphoreType.DMA((n,))], start, finish)


def swap_half(parts):
    n = len(parts)

    def copies(ins, outs, sems):
        x, y, c = _place()
        return [_remote(ins[w].at[(slice(None),) + _half(parts[w].shape[1:], 1 - c)], outs[w], sems[0].at[w], sems[1].at[w],
                        (x, y, 1 - c)) for w in range(n)]

    def start(ins, outs, sems):
        for cp in copies(ins, outs, sems):
            cp.start()

    def finish(ins, outs, sems):
        for cp in copies(ins, outs, sems):
            cp.wait_recv()
            cp.wait_send()

    return Exchange(parts, [jax.ShapeDtypeStruct((N_CHIPS,) + _half_shape(p.shape[1:]), F32) for p in parts],
                    [pltpu.SemaphoreType.DMA((n,))] * 2, start, finish)


def scatter_half(halves):
    n = len(halves)

    def copies(ins, outs, sems):
        x, y, c = _place()
        return [_remote(ins[w].at[2 * (x ^ fx) + (y ^ fy)], outs[w].at[k], sems[0].at[w, k], sems[1].at[w, k],
                        (x ^ fx, y ^ fy, c))
                for w in range(n) for k, (fx, fy) in enumerate(CHIP_MASKS)]

    def start(ins, outs, sems):
        for cp in copies(ins, outs, sems):
            cp.start()

    def finish(ins, outs, sems):
        for cp in copies(ins, outs, sems):
            cp.wait_recv()
            cp.wait_send()

    return Exchange(halves, [jax.ShapeDtypeStruct((3,) + h.shape[1:], h.dtype) for h in halves],
                    [pltpu.SemaphoreType.DMA((n, 3))] * 2, start, finish)


def swap_final(finals, shapes):
    n = len(finals)

    def copies(ins, outs, sems):
        x, y, c = _place()
        mine = [outs[w].at[_half(shapes[w], c)] for w in range(n)]
        local = [pltpu.make_async_copy(ins[w], mine[w], sems[2].at[w]) for w in range(n)]
        sends = [_remote(ins[w], mine[w], sems[0].at[w], sems[1].at[w], (x, y, 1 - c)) for w in range(n)]
        return local, sends

    def start(ins, outs, sems):
        local, sends = copies(ins, outs, sems)
        for cp in local + sends:
            cp.start()

    def finish(ins, outs, sems):
        x, y, c = _place()
        local, sends = copies(ins, outs, sems)
        for w in range(n):
            got = outs[w].at[_half(shapes[w], 1 - c)]
            _remote(got, got, sems[0].at[w], sems[1].at[w], (x, y, 1 - c)).wait_recv()
        for cp in sends:
            cp.wait_send()
        for cp in local:
            cp.wait()

    return Exchange(finals, [jax.ShapeDtypeStruct(tuple(s), F32) for s in shapes],
                    [pltpu.SemaphoreType.DMA((n,))] * 3, start, finish)


def exchange_call(name, ex):
    n, m = len(ex.ins), len(ex.out_shape)

    def body(*refs):
        ins, outs, sems = refs[:n], refs[n:n + m], refs[n + m:]
        ex.start(ins, outs, sems)
        ex.finish(ins, outs, sems)

    return pl.pallas_call(body, name=name, in_specs=_any_specs(n), out_specs=_any_specs(m), out_shape=ex.out_shape,
                          scratch_shapes=ex.sems, compiler_params=_params())(*ex.ins)


def _call(body, ex, *, name, grid, in_specs, out_specs, out_shape, args, scratch_shapes=()):
    sem = ("arbitrary",) * len(grid)
    if ex is None:
        outs = pl.pallas_call(body, name=name, grid=grid, in_specs=in_specs, out_specs=out_specs, out_shape=out_shape,
                              scratch_shapes=list(scratch_shapes), compiler_params=_params(sem))(*args)
        return outs, None
    ni, no, ns = len(in_specs), len(out_specs), len(scratch_shapes)
    ne, me = len(ex.ins), len(ex.out_shape)

    def wrapped(*refs):
        own_in, ex_in = refs[:ni], refs[ni:ni + ne]
        own_out, ex_out = refs[ni + ne:ni + ne + no], refs[ni + ne + no:ni + ne + no + me]
        own_scr, ex_sems = refs[ni + ne + no + me:ni + ne + no + me + ns], refs[ni + ne + no + me + ns:]
        ids = [pl.program_id(a) for a in range(len(grid))]
        first = functools.reduce(jnp.logical_and, [i == 0 for i in ids])
        last = functools.reduce(jnp.logical_and, [i == g - 1 for i, g in zip(ids, grid)])

        @pl.when(first)
        def _():
            ex.start(ex_in, ex_out, ex_sems)

        body(*own_in, *own_out, *own_scr)

        @pl.when(last)
        def _():
            ex.finish(ex_in, ex_out, ex_sems)

    outs = pl.pallas_call(
        wrapped, name=name, grid=grid, in_specs=list(in_specs) + _any_specs(ne),
        out_specs=list(out_specs) + _any_specs(me), out_shape=list(out_shape) + ex.out_shape,
        scratch_shapes=list(scratch_shapes) + ex.sems, compiler_params=_params(sem))(*args, *ex.ins)
    return outs[:no], outs[no:]


def _row_tile(rows, cap, mult=8):
    return max([t for t in range(mult, min(rows, cap) + 1, mult) if rows % t == 0] or [rows])


def add_half_call(name, part, got, where):
    _, rh, cols = got.shape
    tr = _row_tile(rh, 176, 16)
    nb = rh // tr
    by_rows = _splits(part.shape[1:]) == "rows"

    def body(where_ref, p_ref, g_ref, own_ref, send_ref):
        t = p_ref[...] + g_ref[...]
        send_ref[...] = t.astype(BF16)
        chip = where_ref[1]
        own_ref[...] = p_ref[chip] + g_ref[chip]

    blk = (N_CHIPS, tr, cols)
    return pl.pallas_call(
        body, name=name,
        grid_spec=pltpu.PrefetchScalarGridSpec(
            num_scalar_prefetch=1, grid=(nb,),
            in_specs=[pl.BlockSpec(blk, (lambda i, where_ref: (0, where_ref[0] * nb + i, 0)) if by_rows
                                   else (lambda i, where_ref: (0, i, where_ref[0]))),
                      pl.BlockSpec(blk, lambda i, where_ref: (0, i, 0))],
            out_specs=[pl.BlockSpec((tr, cols), lambda i, where_ref: (i, 0)),
                       pl.BlockSpec(blk, lambda i, where_ref: (0, i, 0))]),
        out_shape=[jax.ShapeDtypeStruct((rh, cols), F32), jax.ShapeDtypeStruct(got.shape, BF16)],
        compiler_params=_params(("parallel",)),
    )(where, part, got)


def sum_chips_call(name, own, got):
    _, rh, cols = got.shape
    tr = _row_tile(rh, 176, 16)

    def body(h_ref, g_ref, o_ref):
        o_ref[...] = ((h_ref[...] + g_ref[0].astype(F32)) + g_ref[1].astype(F32)) + g_ref[2].astype(F32)

    return pl.pallas_call(
        body, name=name, grid=(rh // tr,),
        in_specs=[pl.BlockSpec((tr, cols), lambda i: (i, 0)), pl.BlockSpec((3, tr, cols), lambda i: (0, i, 0))],
        out_specs=pl.BlockSpec((tr, cols), lambda i: (i, 0)),
        out_shape=jax.ShapeDtypeStruct((rh, cols), F32),
        compiler_params=_params(("parallel",)),
    )(own, got)


def _adamw(w, g, m, v):
    m = ADAM_B1 * m + (1.0 - ADAM_B1) * g
    v = ADAM_B2 * v + (1.0 - ADAM_B2) * (g * g)
    m_hat = m / (1.0 - ADAM_B1 ** ADAM_STEP)
    v_hat = v / (1.0 - ADAM_B2 ** ADAM_STEP)
    delta = -ADAM_LR * (m_hat / (jnp.sqrt(v_hat) + ADAM_EPS) + ADAM_WD * w)
    return delta, m, v


def adamw_call(name, g, w, m, v):
    r, cols = w.shape
    tr = r if r % 8 else _row_tile(r, 256)

    def body(g_ref, w_ref, m_ref, v_ref, go_ref, d_ref, nm_ref, nv_ref):
        g = g_ref[...]
        go_ref[...] = g
        d_ref[...], nm_ref[...], nv_ref[...] = _adamw(w_ref[...], g, m_ref[...], v_ref[...])

    spec = pl.BlockSpec((tr, cols), lambda i: (i, 0))
    return pl.pallas_call(
        body, name=name, grid=(r // tr,), in_specs=[spec] * 4, out_specs=[spec] * 4,
        out_shape=[jax.ShapeDtypeStruct((r, cols), F32)] * 4,
        compiler_params=_params(("parallel",)),
    )(g, w, m, v)


def allsum_small_call(v):
    R = v.shape[0]

    def body(v_ref, out_ref, buf, send_sems, recv_sems):
        x, y, c = _place()
        me = 4 * x + 2 * y + c
        buf[me] = v_ref[...]
        sends = []
        for k in range(1, N_DEV):
            fx, fy, fc = (k >> 2) & 1, (k >> 1) & 1, k & 1
            cp = pltpu.make_async_remote_copy(
                src_ref=v_ref, dst_ref=buf.at[me], send_sem=send_sems.at[k - 1], recv_sem=recv_sems.at[k - 1],
                device_id=(x ^ fx, y ^ fy, c ^ fc), device_id_type=MESH)
            cp.start()
            sends.append(cp)
        for k in range(1, N_DEV):
            pltpu.make_async_remote_copy(
                src_ref=v_ref, dst_ref=buf.at[me ^ k], send_sem=send_sems.at[k - 1], recv_sem=recv_sems.at[k - 1],
                device_id=(x, y, c), device_id_type=MESH).wait_recv()
        acc = buf[0]
        for d in range(1, N_DEV):
            acc = acc + buf[d]
        out_ref[...] = acc
        for cp in sends:
            cp.wait_send()

    vm = pl.BlockSpec(memory_space=pltpu.VMEM)
    return pl.pallas_call(
        body, name="allsum_small", in_specs=[vm], out_specs=vm,
        out_shape=jax.ShapeDtypeStruct((R, LANES), F32),
        scratch_shapes=[pltpu.VMEM((N_DEV, R, LANES), F32), pltpu.SemaphoreType.DMA((N_DEV - 1,)),
                        pltpu.SemaphoreType.DMA((N_DEV - 1,))],
        compiler_params=_params(),
    )(v)


def _slab(flat, mult):
    n = flat.shape[-1]
    rows = -(-n // (LANES * mult)) * mult
    flat = jnp.pad(flat, [(0, 0)] * (flat.ndim - 1) + [(0, rows * LANES - n)])
    return flat.reshape(flat.shape[:-1] + (rows, LANES))


def full_from_chips(blocks, by_col):
    _, r, c = blocks.shape
    return blocks.transpose(1, 0, 2).reshape(r, N_CHIPS * c) if by_col else blocks.reshape(N_CHIPS * r, c)


def chips_from_full(full, by_col):
    if by_col:
        r, c = full.shape[0], full.shape[1] // N_CHIPS
        return full.reshape(r, N_CHIPS, c).transpose(1, 0, 2)
    return full.reshape(N_CHIPS, full.shape[0] // N_CHIPS, full.shape[1])


SMALL_PACK = SMALL_W + ("loss", "conv_w")
SMALL_PACK_N = {**SMALL_N, "loss": 1, "conv_w": 3 * 2 * D_FF}


def pack_small(vals):
    zero = jnp.zeros((1,), F32)
    return _slab(jnp.concatenate([vals[n].reshape(-1) if n in vals else jnp.tile(zero, SMALL_PACK_N[n])
                                  for n in SMALL_PACK]), 8)


def unpack_small(slab, shapes):
    flat = slab.reshape(-1)
    out, off = {}, 0
    for n in SMALL_PACK:
        out[n] = flat[off:off + SMALL_PACK_N[n]].reshape(shapes[n])
        off += SMALL_PACK_N[n]
    return out


def _split_heads(w, a, b):
    r = w.shape[0]
    w3 = w.reshape(r, HEADS, a + b)
    return w3[:, :, :a].reshape(r, HEADS * a), w3[:, :, a:].reshape(r, HEADS * b)


def _merge_heads(wa, wb, a, b):
    r = wa.shape[0]
    return jnp.concatenate([wa.reshape(r, HEADS, a), wb.reshape(r, HEADS, b)], axis=2).reshape(r, HEADS * (a + b))


def kernel(x, positions, g_mix, w_in, g_cq, w_uq, g_ckv, w_ukv, g_sb_out, g_mla_out, w_out, g_ffn, w_up, conv_w, conv_b, w_down, g_final, loss_target, m_g_mix, m_w_in, m_g_cq, m_w_uq, m_g_ckv, m_w_ukv, m_g_sb_out, m_g_mla_out, m_w_out, m_g_ffn, m_w_up, m_conv_w, m_conv_b, m_w_down, m_g_final, v_g_mix, v_w_in, v_g_cq, v_w_uq, v_g_ckv, v_w_ukv, v_g_sb_out, v_g_mla_out, v_w_out, v_g_ffn, v_w_up, v_conv_w, v_conv_b, v_w_down, v_g_final):
    given = dict(locals())
    B, S, _ = x.shape
    T = B * S
    w_big = {n: given[n][0].T if n == "w_in" else given[n][0] for n in BIG_W}
    m_big = {n: given["m_" + n][0].T if n == "w_in" else given["m_" + n][0] for n in BIG_W}
    v_big = {n: given["v_" + n][0].T if n == "w_in" else given["v_" + n][0] for n in BIG_W}
    shard_shape = {n: w_big[n].shape for n in BIG_W}

    first = ("w_in", "w_uq", "w_ukv")
    later = ("w_out", "w_up", "w_down", "conv_w")
    x2d = x.reshape(T, D_MODEL)
    half = ROPE_DIM // 2
    inv_freq = 1.0 / (ROPE_BASE ** (jnp.arange(half, dtype=F32) * (2.0 / ROPE_DIM)))
    h, cos, sin, got_w = norm_mix_rope_call(
        x2d, g_mix, positions.reshape(T, 1), jnp.tile(inv_freq, LANES // half).reshape(1, LANES),
        gather_group([w_big[n].astype(BF16) for n in first]))
    full = {n: full_from_chips(g_, BIG_SHARD[n][2]) for n, g_ in zip(first, got_w) if n != "w_in"}
    gather_later = gather_group([w_big[n] if n == "conv_w" else w_big[n].astype(BF16) for n in later])
    w_in_t = jnp.pad(got_w[0].reshape(IN_COLS, D_MODEL), ((0, IN_COLS_PAD - IN_COLS), (0, 0)))
    w_uq_p = jnp.concatenate(_split_heads(full["w_uq"], HEAD_DIM, ROPE_DIM), axis=1)
    w_ukv_p = jnp.concatenate(_split_heads(full["w_ukv"], HEAD_DIM, HEAD_DIM), axis=1)

    p = matmul_call("proj_in", h, w_in_t, "nt", tn=IN_COLS_PAD // 2)
    qn, qr, kn, vm, krt, cqn, ckvn = mla_prep_fwd_call(p, cos, sin, g_cq, g_ckv, w_uq_p, w_ukv_p)
    o_sb, lt_sb, got_w = sb_fwd_call(p, B, S, ex=gather_later)
    w_up4 = got_w[1]
    full.update({n: full_from_chips(g_, BIG_SHARD[n][2]) for n, g_ in zip(later, got_w) if n != "w_up"})
    conv_w_full = full["conv_w"]
    o_mla, lse = mla_fwd_call(qn, qr, kn, krt, vm, B, S)
    o_cat, x1, hn = proj_out_norm_call(o_sb, o_mla, g_sb_out, g_mla_out, full["w_out"], x2d, g_ffn)
    u_g, u_v, act, c_g, c_v = ffn_up_conv_call(hn, w_up4, conv_w_full, conv_b, S)
    dx2, dx2b, loss_row, dg_final = ffn_down_loss_call(
        act, full["w_down"], x1, g_final.reshape(1, D_MODEL), loss_target.reshape(T, D_MODEL))

    xi, yi, ci = _place()
    chip = (2 * xi + yi).astype(jnp.int32).reshape(1)
    where = jnp.stack([ci, 2 * xi + yi]).astype(jnp.int32)

    def add_halves(names, parts, sib_rows):
        return [add_half_call("add_half_" + n, p_, s_, where) for n, p_, s_ in zip(names, parts, sib_rows)]

    def sum_chips(names, halves, from_chips):
        return [sum_chips_call("sum_chips_" + n, h_[0], f_) for n, h_, f_ in zip(names, halves, from_chips)]

    ffn_w = ("w_down", "w_up")
    parts_ffn = [chips_from_full(wgrad_call("wgrad_down", act, dx2b, tn=512, tt=1024), False)]
    du_g, du_v, dw_up4, dcw_g, dcw_v, dcb_g, dcb_v = conv_bwd_wgrad_up_call(
        hn, u_g, u_v, c_g, c_v, dx2b, full["w_down"], conv_w_full, S)
    parts_ffn.append(dw_up4)
    dx1, dg_ffn, sib_ffn = ffn_up_bwd_call(du_g, du_v, w_up4, x1, g_ffn, dx2, swap_half(parts_ffn))
    parts_out = [chips_from_full(wgrad_call("wgrad_out", o_cat, dx1, tt=1024), False)]
    do_sb, do_mla, dg_sb_out, dg_mla_out, sib_out = proj_out_bwd_call(
        dx1, full["w_out"], o_sb, o_mla, g_sb_out, g_mla_out, swap_half(parts_out))
    early = ffn_w + ("w_out",)
    halves = add_halves(early, parts_ffn + parts_out, list(sib_ffn) + list(sib_out))
    dq_sb, dk_sb, dv_sb, from_chips = sb_bwd_call(p, lt_sb, do_sb, B, S, ex=scatter_half([h_[1] for h_ in halves]))
    finals = sum_chips(early, halves, from_chips)
    dqn, dqr4, dkn, dvm, dkrt4, done = mla_bwd_call(qn, qr, kn, krt, vm, o_mla, lse, do_mla, B, S,
        ex=swap_final(finals, [shard_shape[n] for n in early]))
    grads = dict(zip(early, done))
    dcq, dckvr, dq_cat, dkv_cat, dg_cq, dg_ckv = mla_prep_bwd_call(
        p, cos, sin, g_cq, g_ckv, w_uq_p, w_ukv_p, dqn, dqr4, dkn, dvm, dkrt4)
    dw_uq_p = wgrad_call("wgrad_uq", cqn, dq_cat)
    dw_ukv_p = wgrad_call("wgrad_ukv", ckvn, dkv_cat)
    dp = (dq_sb, dk_sb, dv_sb, dcq, dckvr)
    late = ("w_uq", "w_ukv", "w_in")
    parts_late = [chips_from_full(g_, True) for g_ in (
        _merge_heads(dw_uq_p[:, :ATT_W], dw_uq_p[:, ATT_W:], HEAD_DIM, ROPE_DIM),
        _merge_heads(dw_ukv_p[:, :ATT_W], dw_ukv_p[:, ATT_W:], HEAD_DIM, HEAD_DIM))]
    parts_late.append(chips_from_full(wgrad_in_call(h, dp), False))
    dh, sib_late = proj_in_bwd_call(dp, w_in_t, swap_half(parts_late))
    halves = add_halves(late, parts_late, sib_late)
    grad_x, dg_mix, from_chips = rmsnorm_bwd_call(
        "norm_mix_bwd", x2d, g_mix, dh, dx1, ex=scatter_half([h_[1] for h_ in halves]))
    finals = sum_chips(late, halves, from_chips)
    grads.update(zip(late, exchange_call("swap_final_late", swap_final(finals, [shard_shape[n] for n in late]))))

    shapes = {n: given[n].shape for n in SMALL_W}
    shapes.update(loss=(), conv_w=(3, 2 * D_FF))
    small_g = {"g_mix": dg_mix, "g_cq": dg_cq, "g_ckv": dg_ckv, "g_sb_out": dg_sb_out, "g_mla_out": dg_mla_out,
               "g_ffn": dg_ffn, "conv_b": jnp.concatenate([dcb_g, dcb_v], axis=1), "g_final": dg_final,
               "loss": loss_row[0, :1], "conv_w": jnp.concatenate([dcw_g, dcw_v], axis=1)}
    gs_slab = allsum_small_call(pack_small(small_g))
    small_in = [pack_small({n: given[pre + n] for n in SMALL_W}) for pre in ("", "m_", "v_")]
    small_out = [unpack_small(s, shapes) for s in adamw_call("adamw_small", gs_slab, *small_in)]
    cw_cols = BIG_SHARD["conv_w"][1]
    grads["conv_w"] = lax.dynamic_slice_in_dim(small_out[0]["conv_w"], chip[0] * cw_cols, cw_cols, axis=1)

    big_out = {n: adamw_call("adamw_" + n, grads[n], w_big[n], m_big[n], v_big[n]) for n in BIG_W}
    weights = ("g_mix", "w_in", "g_cq", "w_uq", "g_ckv", "w_ukv", "g_sb_out", "g_mla_out", "w_out", "g_ffn",
               "w_up", "conv_w", "conv_b", "w_down", "g_final")
    outs = [small_out[0]["loss"], grad_x.reshape(B, S, D_MODEL)]
    for k in range(4):
        for n in weights:
            if n in BIG_W:
                outs.append((big_out[n][k].T if n == "w_in" else big_out[n][k])[None])
            else:
                outs.append(small_out[k][n])
    return tuple(outs)
```
